```python
import jax, jax.numpy as jnp
from jax import lax
import numpy as np

D_MODEL = 1024
BATCH = 8
SEQ = 4096
DEPTH = 1

HEAD_DIM = 64
RWKV_HEADS = D_MODEL // HEAD_DIM
RWKV_WIDTH = RWKV_HEADS * HEAD_DIM
DECAY_LORA = 64
AAA_LORA = 64
GATE_LORA = 160
RWKV_COLS = 3 * RWKV_WIDTH + DECAY_LORA + AAA_LORA + GATE_LORA
GN_EPS = 64e-5
ATTN_PAIRS = ((128, 1), (512, 4), (2048, 16))
ATTN_GROUPS = 3
ATTN_HEADS_PER_GROUP = 4
ATTN_HEADS = ATTN_GROUPS * ATTN_HEADS_PER_GROUP
ATTN_WIDTH = ATTN_HEADS * HEAD_DIM
N_BRANCHES = 2
IN_COLS = RWKV_COLS + 3 * ATTN_WIDTH + N_BRANCHES * D_MODEL
D_FF = 2816
RMS_EPS = 1e-6
NEG_INF = -1e30

kernel_name = 'hybrid_rwkv7_dilated_attn_macaron'


def _split_last(t, sizes):
    out, start = [], 0
    for n in sizes:
        out.append(t[..., start:start + n])
        start += n
    return out


def rms_norm(x, g):
    xf = x.astype(jnp.float32)
    y = xf * lax.rsqrt(jnp.mean(xf * xf, axis=-1, keepdims=True) + RMS_EPS)
    return y.astype(x.dtype) * g


def swiglu(h, w_in, w_out):
    gate, up = _split_last(h @ w_in, (D_FF, D_FF))
    return (jax.nn.silu(gate) * up) @ w_out


def token_shift(p, mu):
    prev = jnp.pad(p, ((0, 0), (1, 0), (0, 0)))[:, :-1]
    return p + (prev - p) * mu


def wkv7_scan(r, w, k, v, a, b):
    B, S, H, N = r.shape
    to_time = lambda t: jnp.moveaxis(t, 1, 0)

    def step(state, inp):
        r_t, w_t, k_t, v_t, a_t, b_t = inp
        sa = jnp.einsum('bhvk,bhk->bhv', state, a_t)
        state = (state * w_t[:, :, None, :] + sa[..., None] * b_t[:, :, None, :]
                 + v_t[..., None] * k_t[:, :, None, :])
        return state, jnp.einsum('bhvk,bhk->bhv', state, r_t)

    state0 = jnp.zeros((B, H, N, N), jnp.float32)
    _, out = lax.scan(step, state0, (to_time(r), to_time(w), to_time(k),
                                      to_time(v), to_time(a), to_time(b)))
    return jnp.moveaxis(out, 0, 1)


def rwkv7_time_mix(p, mu, w0, w2, a0, a2, g2, k_k, k_a, r_k, ln_w, ln_b):
    B, S, _ = p.shape
    H, N = RWKV_HEADS, HEAD_DIM
    p = token_shift(p.astype(jnp.float32), mu)
    r, k, v, wd, ad, gd = _split_last(
        p, (RWKV_WIDTH, RWKV_WIDTH, RWKV_WIDTH, DECAY_LORA, AAA_LORA, GATE_LORA))
    w = -jax.nn.softplus(-(w0 + jnp.tanh(wd) @ w2)) - 0.5
    decay = jnp.exp(-jnp.exp(w))
    a = jax.nn.sigmoid(a0 + ad @ a2)
    g = jax.nn.sigmoid(gd) @ g2
    heads = lambda t: t.reshape(B, S, H, N)
    kk = heads(k * k_k)
    kk = kk / jnp.maximum(jnp.sqrt(jnp.sum(kk * kk, axis=-1, keepdims=True)), 1e-12)
    k = k * (1.0 + (a - 1.0) * k_a)
    rh, kh, vh, ah = heads(r), heads(k), heads(v), heads(a)
    wkv = wkv7_scan(rh, heads(decay), kh, vh, -kk, kk * ah)
    mean = jnp.mean(wkv, axis=-1, keepdims=True)
    var = jnp.mean(jnp.square(wkv - mean), axis=-1, keepdims=True)
    y = ((wkv - mean) * lax.rsqrt(var + GN_EPS)).reshape(B, S, RWKV_WIDTH) * ln_w + ln_b
    bonus = jnp.sum(rh * kh * r_k, axis=-1, keepdims=True) * vh
    return (y + bonus.reshape(B, S, RWKV_WIDTH)) * g


def dilated_group_attention(q, k, v, window, dilation):
    B, S, Hg, E = q.shape
    span = window // dilation
    blk = span
    sub = S // dilation
    nb = -(-sub // blk)
    pad = nb * blk - sub

    def to_sub(t):
        t = t.reshape(B, sub, dilation, Hg, E).transpose(0, 2, 1, 3, 4)
        t = jnp.pad(t, ((0, 0), (0, 0), (0, pad), (0, 0), (0, 0)))
        return t.reshape(B, dilation, nb, blk, Hg, E)

    def with_prev(t):
        tp = jnp.pad(t, ((0, 0), (0, 0), (1, 0), (0, 0), (0, 0), (0, 0)))
        return jnp.concatenate([tp[:, :, :-1], tp[:, :, 1:]], axis=3)

    qb = to_sub(q)
    kw, vw = with_prev(to_sub(k)), with_prev(to_sub(v))
    s = jnp.einsum('bdnqhe,bdnkhe->bdnhqk', qb, kw)
    qi = jnp.arange(blk)[:, None]
    kj = jnp.arange(2 * blk)[None, :]
    dist = qi + blk - kj
    bidx = jnp.arange(nb)[:, None, None]
    valid = (dist >= 0) & (dist <= span) & (bidx * blk + kj - blk >= 0)
    s = jnp.where(valid[None, None, :, None], s, NEG_INF)
    m = jnp.max(s, axis=-1, keepdims=True)
    pexp = jnp.exp(s - m)
    den = jnp.sum(pexp, axis=-1, keepdims=True)
    o = jnp.einsum('bdnhqk,bdnkhe->bdnqhe', pexp / den, vw)
    lse = (m + jnp.log(den))[..., 0].transpose(0, 1, 2, 4, 3)
    o = o.reshape(B, dilation, nb * blk, Hg, E)[:, :, :sub]
    o = o.transpose(0, 2, 1, 3, 4).reshape(B, S, Hg, E)
    lse = lse.reshape(B, dilation, nb * blk, Hg)[:, :, :sub]
    lse = lse.transpose(0, 2, 1, 3).reshape(B, S, Hg)
    return o, lse


def dilated_attention(pq, pk, pv, q_gain, k_gain):
    B, S, _ = pq.shape
    heads = lambda t: t.astype(jnp.float32).reshape(B, S, ATTN_HEADS, HEAD_DIM)
    q = rms_norm(heads(pq), q_gain) * (HEAD_DIM ** -0.5)
    k = rms_norm(heads(pk), k_gain)
    v = heads(pv)
    outs, lses = [], []
    for gi, (window, dilation) in enumerate(ATTN_PAIRS):
        sl = slice(gi * ATTN_HEADS_PER_GROUP, (gi + 1) * ATTN_HEADS_PER_GROUP)
        o, lse = dilated_group_attention(q[:, :, sl], k[:, :, sl], v[:, :, sl], window, dilation)
        outs.append(o)
        lses.append(lse)
    o = jnp.stack(outs, axis=2)
    alpha = jax.nn.softmax(jnp.stack(lses, axis=2), axis=2)
    return (o * alpha[..., None]).reshape(B, S, ATTN_WIDTH)


def _fwd_setup_inputs(seed: int = 0) -> dict:
    key = jax.random.key(seed)
    ks = jax.random.split(key, 27)
    f32 = jnp.float32
    L, D = DEPTH, D_MODEL

    def nrm(k, shape, scale):
        return scale * jax.random.normal(k, shape, f32)

    def gain(k, shape):
        return 1.0 + 0.1 * jax.random.normal(k, shape, f32)

    return {
        'x': jax.random.normal(ks[0], (BATCH, SEQ, D), f32),
        'ffn1_norm': gain(ks[1], (L, D)),
        'ffn1_w_in': nrm(ks[2], (L, D, 2 * D_FF), D ** -0.5),
        'ffn1_w_out': nrm(ks[3], (L, D_FF, D), D_FF ** -0.5),
        'mix_norm': gain(ks[4], (L, D)),
        'w_in': nrm(ks[5], (L, D, IN_COLS), D ** -0.5),
        'b_gate': nrm(ks[6], (L, N_BRANCHES * D), 0.1),
        'rwkv_mu': jax.random.uniform(ks[7], (L, RWKV_COLS), f32),
        'rwkv_w0': jax.random.uniform(ks[8], (L, RWKV_WIDTH), f32, -6.0, 0.0),
        'rwkv_w2': nrm(ks[9], (L, DECAY_LORA, RWKV_WIDTH), 0.1 * DECAY_LORA ** -0.5),
        'rwkv_a0': nrm(ks[10], (L, RWKV_WIDTH), 0.1),
        'rwkv_a2': nrm(ks[11], (L, AAA_LORA, RWKV_WIDTH), AAA_LORA ** -0.5),
        'rwkv_g2': nrm(ks[12], (L, GATE_LORA, RWKV_WIDTH), GATE_LORA ** -0.5),
        'rwkv_k_k': gain(ks[13], (L, RWKV_WIDTH)),
        'rwkv_k_a': gain(ks[14], (L, RWKV_WIDTH)),
        'rwkv_r_k': nrm(ks[15], (L, RWKV_HEADS, HEAD_DIM), 0.1),
        'rwkv_ln_w': gain(ks[16], (L, RWKV_WIDTH)),
        'rwkv_ln_b': nrm(ks[17], (L, RWKV_WIDTH), 0.01),
        'attn_q_norm': gain(ks[18], (L, HEAD_DIM)),
        'attn_k_norm': gain(ks[19], (L, HEAD_DIM)),
        'w_proj_rwkv': nrm(ks[20], (L, RWKV_WIDTH, D), RWKV_WIDTH ** -0.5),
        'w_proj_attn': nrm(ks[21], (L, ATTN_WIDTH, D), ATTN_WIDTH ** -0.5),
        'w_out': nrm(ks[22], (L, D, D), D ** -0.5),
        'ffn2_norm': gain(ks[23], (L, D)),
        'ffn2_w_in': nrm(ks[24], (L, D, 2 * D_FF), D ** -0.5),
        'ffn2_w_out': nrm(ks[25], (L, D_FF, D), D_FF ** -0.5),
    }


def _fwd_reference(x, ffn1_norm, ffn1_w_in, ffn1_w_out, mix_norm, w_in, b_gate, rwkv_mu,
              rwkv_w0, rwkv_w2, rwkv_a0, rwkv_a2, rwkv_g2, rwkv_k_k, rwkv_k_a, rwkv_r_k,
              rwkv_ln_w, rwkv_ln_b, attn_q_norm, attn_k_norm, w_proj_rwkv, w_proj_attn,
              w_out, ffn2_norm, ffn2_w_in, ffn2_w_out):
    for l in range(DEPTH):
        x = x + 0.5 * swiglu(rms_norm(x, ffn1_norm[l]), ffn1_w_in[l], ffn1_w_out[l])
        h = rms_norm(x, mix_norm[l])
        p_rwkv, p_q, p_k, p_v, p_gate = _split_last(
            h @ w_in[l], (RWKV_COLS, ATTN_WIDTH, ATTN_WIDTH, ATTN_WIDTH, N_BRANCHES * D_MODEL))
        y_a = rwkv7_time_mix(p_rwkv, rwkv_mu[l], rwkv_w0[l], rwkv_w2[l], rwkv_a0[l],
                             rwkv_a2[l], rwkv_g2[l], rwkv_k_k[l], rwkv_k_a[l], rwkv_r_k[l],
                             rwkv_ln_w[l], rwkv_ln_b[l]).astype(x.dtype)
        y_b = dilated_attention(p_q, p_k, p_v, attn_q_norm[l], attn_k_norm[l]).astype(x.dtype)
        g_a, g_b = _split_last(jax.nn.sigmoid(p_gate + b_gate[l]), (D_MODEL, D_MODEL))
        merged = g_a * (y_a @ w_proj_rwkv[l]) + g_b * (y_b @ w_proj_attn[l])
        x = x + merged @ w_out[l]
        x = x + 0.5 * swiglu(rms_norm(x, ffn2_norm[l]), ffn2_w_in[l], ffn2_w_out[l])
    return x


import jax as _jax
import jax.numpy as _jnp

TWIN_FORMAT = 'train_step'
FWD_PARAMS = ['x', 'ffn1_norm', 'ffn1_w_in', 'ffn1_w_out', 'mix_norm', 'w_in', 'b_gate', 'rwkv_mu', 'rwkv_w0', 'rwkv_w2', 'rwkv_a0', 'rwkv_a2', 'rwkv_g2', 'rwkv_k_k', 'rwkv_k_a', 'rwkv_r_k', 'rwkv_ln_w', 'rwkv_ln_b', 'attn_q_norm', 'attn_k_norm', 'w_proj_rwkv', 'w_proj_attn', 'w_out', 'ffn2_norm', 'ffn2_w_in', 'ffn2_w_out']
TWIN_WEIGHTS = ['ffn1_norm', 'ffn1_w_in', 'ffn1_w_out', 'mix_norm', 'w_in', 'b_gate', 'rwkv_mu', 'rwkv_w0', 'rwkv_w2', 'rwkv_a0', 'rwkv_a2', 'rwkv_g2', 'rwkv_k_k', 'rwkv_k_a', 'rwkv_r_k', 'rwkv_ln_w', 'rwkv_ln_b', 'attn_q_norm', 'attn_k_norm', 'w_proj_rwkv', 'w_proj_attn', 'w_out', 'ffn2_norm', 'ffn2_w_in', 'ffn2_w_out']
TWIN_DIFF_INPUT = 'x'
TWIN_INPUTS = ['x', 'ffn1_norm', 'ffn1_w_in', 'ffn1_w_out', 'mix_norm', 'w_in', 'b_gate', 'rwkv_mu', 'rwkv_w0', 'rwkv_w2', 'rwkv_a0', 'rwkv_a2', 'rwkv_g2', 'rwkv_k_k', 'rwkv_k_a', 'rwkv_r_k', 'rwkv_ln_w', 'rwkv_ln_b', 'attn_q_norm', 'attn_k_norm', 'w_proj_rwkv', 'w_proj_attn', 'w_out', 'ffn2_norm', 'ffn2_w_in', 'ffn2_w_out', 'loss_target', 'm_ffn1_norm', 'm_ffn1_w_in', 'm_ffn1_w_out', 'm_mix_norm', 'm_w_in', 'm_b_gate', 'm_rwkv_mu', 'm_rwkv_w0', 'm_rwkv_w2', 'm_rwkv_a0', 'm_rwkv_a2', 'm_rwkv_g2', 'm_rwkv_k_k', 'm_rwkv_k_a', 'm_rwkv_r_k', 'm_rwkv_ln_w', 'm_rwkv_ln_b', 'm_attn_q_norm', 'm_attn_k_norm', 'm_w_proj_rwkv', 'm_w_proj_attn', 'm_w_out', 'm_ffn2_norm', 'm_ffn2_w_in', 'm_ffn2_w_out', 'v_ffn1_norm', 'v_ffn1_w_in', 'v_ffn1_w_out', 'v_mix_norm', 'v_w_in', 'v_b_gate', 'v_rwkv_mu', 'v_rwkv_w0', 'v_rwkv_w2', 'v_rwkv_a0', 'v_rwkv_a2', 'v_rwkv_g2', 'v_rwkv_k_k', 'v_rwkv_k_a', 'v_rwkv_r_k', 'v_rwkv_ln_w', 'v_rwkv_ln_b', 'v_attn_q_norm', 'v_attn_k_norm', 'v_w_proj_rwkv', 'v_w_proj_attn', 'v_w_out', 'v_ffn2_norm', 'v_ffn2_w_in', 'v_ffn2_w_out']
TWIN_OUTPUTS = ['loss', 'grad_x', 'grad_ffn1_norm', 'grad_ffn1_w_in', 'grad_ffn1_w_out', 'grad_mix_norm', 'grad_w_in', 'grad_b_gate', 'grad_rwkv_mu', 'grad_rwkv_w0', 'grad_rwkv_w2', 'grad_rwkv_a0', 'grad_rwkv_a2', 'grad_rwkv_g2', 'grad_rwkv_k_k', 'grad_rwkv_k_a', 'grad_rwkv_r_k', 'grad_rwkv_ln_w', 'grad_rwkv_ln_b', 'grad_attn_q_norm', 'grad_attn_k_norm', 'grad_w_proj_rwkv', 'grad_w_proj_attn', 'grad_w_out', 'grad_ffn2_norm', 'grad_ffn2_w_in', 'grad_ffn2_w_out', 'delta_ffn1_norm', 'delta_ffn1_w_in', 'delta_ffn1_w_out', 'delta_mix_norm', 'delta_w_in', 'delta_b_gate', 'delta_rwkv_mu', 'delta_rwkv_w0', 'delta_rwkv_w2', 'delta_rwkv_a0', 'delta_rwkv_a2', 'delta_rwkv_g2', 'delta_rwkv_k_k', 'delta_rwkv_k_a', 'delta_rwkv_r_k', 'delta_rwkv_ln_w', 'delta_rwkv_ln_b', 'delta_attn_q_norm', 'delta_attn_k_norm', 'delta_w_proj_rwkv', 'delta_w_proj_attn', 'delta_w_out', 'delta_ffn2_norm', 'delta_ffn2_w_in', 'delta_ffn2_w_out', 'new_m_ffn1_norm', 'new_m_ffn1_w_in', 'new_m_ffn1_w_out', 'new_m_mix_norm', 'new_m_w_in', 'new_m_b_gate', 'new_m_rwkv_mu', 'new_m_rwkv_w0', 'new_m_rwkv_w2', 'new_m_rwkv_a0', 'new_m_rwkv_a2', 'new_m_rwkv_g2', 'new_m_rwkv_k_k', 'new_m_rwkv_k_a', 'new_m_rwkv_r_k', 'new_m_rwkv_ln_w', 'new_m_rwkv_ln_b', 'new_m_attn_q_norm', 'new_m_attn_k_norm', 'new_m_w_proj_rwkv', 'new_m_w_proj_attn', 'new_m_w_out', 'new_m_ffn2_norm', 'new_m_ffn2_w_in', 'new_m_ffn2_w_out', 'new_v_ffn1_norm', 'new_v_ffn1_w_in', 'new_v_ffn1_w_out', 'new_v_mix_norm', 'new_v_w_in', 'new_v_b_gate', 'new_v_rwkv_mu', 'new_v_rwkv_w0', 'new_v_rwkv_w2', 'new_v_rwkv_a0', 'new_v_rwkv_a2', 'new_v_rwkv_g2', 'new_v_rwkv_k_k', 'new_v_rwkv_k_a', 'new_v_rwkv_r_k', 'new_v_rwkv_ln_w', 'new_v_rwkv_ln_b', 'new_v_attn_q_norm', 'new_v_attn_k_norm', 'new_v_w_proj_rwkv', 'new_v_w_proj_attn', 'new_v_w_out', 'new_v_ffn2_norm', 'new_v_ffn2_w_in', 'new_v_ffn2_w_out']
TWIN_LEAF_KINDS = {'loss': 'loss', 'grad_x': 'grad_x', 'grad_ffn1_norm': 'grad_w', 'grad_ffn1_w_in': 'grad_w', 'grad_ffn1_w_out': 'grad_w', 'grad_mix_norm': 'grad_w', 'grad_w_in': 'grad_w', 'grad_b_gate': 'grad_w', 'grad_rwkv_mu': 'grad_w', 'grad_rwkv_w0': 'grad_w', 'grad_rwkv_w2': 'grad_w', 'grad_rwkv_a0': 'grad_w', 'grad_rwkv_a2': 'grad_w', 'grad_rwkv_g2': 'grad_w', 'grad_rwkv_k_k': 'grad_w', 'grad_rwkv_k_a': 'grad_w', 'grad_rwkv_r_k': 'grad_w', 'grad_rwkv_ln_w': 'grad_w', 'grad_rwkv_ln_b': 'grad_w', 'grad_attn_q_norm': 'grad_w', 'grad_attn_k_norm': 'grad_w', 'grad_w_proj_rwkv': 'grad_w', 'grad_w_proj_attn': 'grad_w', 'grad_w_out': 'grad_w', 'grad_ffn2_norm': 'grad_w', 'grad_ffn2_w_in': 'grad_w', 'grad_ffn2_w_out': 'grad_w', 'delta_ffn1_norm': 'delta_w', 'delta_ffn1_w_in': 'delta_w', 'delta_ffn1_w_out': 'delta_w', 'delta_mix_norm': 'delta_w', 'delta_w_in': 'delta_w', 'delta_b_gate': 'delta_w', 'delta_rwkv_mu': 'delta_w', 'delta_rwkv_w0': 'delta_w', 'delta_rwkv_w2': 'delta_w', 'delta_rwkv_a0': 'delta_w', 'delta_rwkv_a2': 'delta_w', 'delta_rwkv_g2': 'delta_w', 'delta_rwkv_k_k': 'delta_w', 'delta_rwkv_k_a': 'delta_w', 'delta_rwkv_r_k': 'delta_w', 'delta_rwkv_ln_w': 'delta_w', 'delta_rwkv_ln_b': 'delta_w', 'delta_attn_q_norm': 'delta_w', 'delta_attn_k_norm': 'delta_w', 'delta_w_proj_rwkv': 'delta_w', 'delta_w_proj_attn': 'delta_w', 'delta_w_out': 'delta_w', 'delta_ffn2_norm': 'delta_w', 'delta_ffn2_w_in': 'delta_w', 'delta_ffn2_w_out': 'delta_w', 'new_m_ffn1_norm': 'new_m', 'new_m_ffn1_w_in': 'new_m', 'new_m_ffn1_w_out': 'new_m', 'new_m_mix_norm': 'new_m', 'new_m_w_in': 'new_m', 'new_m_b_gate': 'new_m', 'new_m_rwkv_mu': 'new_m', 'new_m_rwkv_w0': 'new_m', 'new_m_rwkv_w2': 'new_m', 'new_m_rwkv_a0': 'new_m', 'new_m_rwkv_a2': 'new_m', 'new_m_rwkv_g2': 'new_m', 'new_m_rwkv_k_k': 'new_m', 'new_m_rwkv_k_a': 'new_m', 'new_m_rwkv_r_k': 'new_m', 'new_m_rwkv_ln_w': 'new_m', 'new_m_rwkv_ln_b': 'new_m', 'new_m_attn_q_norm': 'new_m', 'new_m_attn_k_norm': 'new_m', 'new_m_w_proj_rwkv': 'new_m', 'new_m_w_proj_attn': 'new_m', 'new_m_w_out': 'new_m', 'new_m_ffn2_norm': 'new_m', 'new_m_ffn2_w_in': 'new_m', 'new_m_ffn2_w_out': 'new_m', 'new_v_ffn1_norm': 'new_v', 'new_v_ffn1_w_in': 'new_v', 'new_v_ffn1_w_out': 'new_v', 'new_v_mix_norm': 'new_v', 'new_v_w_in': 'new_v', 'new_v_b_gate': 'new_v', 'new_v_rwkv_mu': 'new_v', 'new_v_rwkv_w0': 'new_v', 'new_v_rwkv_w2': 'new_v', 'new_v_rwkv_a0': 'new_v', 'new_v_rwkv_a2': 'new_v', 'new_v_rwkv_g2': 'new_v', 'new_v_rwkv_k_k': 'new_v', 'new_v_rwkv_k_a': 'new_v', 'new_v_rwkv_r_k': 'new_v', 'new_v_rwkv_ln_w': 'new_v', 'new_v_rwkv_ln_b': 'new_v', 'new_v_attn_q_norm': 'new_v', 'new_v_attn_k_norm': 'new_v', 'new_v_w_proj_rwkv': 'new_v', 'new_v_w_proj_attn': 'new_v', 'new_v_w_out': 'new_v', 'new_v_ffn2_norm': 'new_v', 'new_v_ffn2_w_in': 'new_v', 'new_v_ffn2_w_out': 'new_v'}


def _forward(args):
    return _fwd_reference(*[args[k] for k in FWD_PARAMS])


def _output_shape():
    out = _jax.eval_shape(lambda: _forward(_fwd_setup_inputs(0)))
    return out.shape, out.dtype

N_MICROBATCH = 1
ADAM_LR = 0.001
ADAM_B1 = 0.9
ADAM_B2 = 0.999
ADAM_EPS = 1e-08
ADAM_WD = 0.01
ADAM_STEP = 10
PER_EXAMPLE_BATCH_AXIS = {'x': 0, 'loss_target': 0}
SHARED_INPUTS = []
_WEIGHT_DTYPES = {'ffn1_norm': _jnp.float32, 'ffn1_w_in': _jnp.float32, 'ffn1_w_out': _jnp.float32, 'mix_norm': _jnp.float32, 'w_in': _jnp.float32, 'b_gate': _jnp.float32, 'rwkv_mu': _jnp.float32, 'rwkv_w0': _jnp.float32, 'rwkv_w2': _jnp.float32, 'rwkv_a0': _jnp.float32, 'rwkv_a2': _jnp.float32, 'rwkv_g2': _jnp.float32, 'rwkv_k_k': _jnp.float32, 'rwkv_k_a': _jnp.float32, 'rwkv_r_k': _jnp.float32, 'rwkv_ln_w': _jnp.float32, 'rwkv_ln_b': _jnp.float32, 'attn_q_norm': _jnp.float32, 'attn_k_norm': _jnp.float32, 'w_proj_rwkv': _jnp.float32, 'w_proj_attn': _jnp.float32, 'w_out': _jnp.float32, 'ffn2_norm': _jnp.float32, 'ffn2_w_in': _jnp.float32, 'ffn2_w_out': _jnp.float32}
MOMENT_SCALE = {'ffn1_norm': 6.165289e+00, 'ffn1_w_in': 7.249191e-02, 'ffn1_w_out': 1.220761e-01, 'mix_norm': 8.356165e-01, 'w_in': 7.694370e-02, 'b_gate': 7.325829e-01, 'rwkv_mu': 1.288542e+00, 'rwkv_w0': 5.011520e-02, 'rwkv_w2': 4.841952e-03, 'rwkv_a0': 1.146497e-01, 'rwkv_a2': 3.601885e-02, 'rwkv_g2': 2.646710e+00, 'rwkv_k_k': 1.917741e-01, 'rwkv_k_a': 2.749157e-01, 'rwkv_r_k': 1.909811e+00, 'rwkv_ln_w': 4.536674e+00, 'rwkv_ln_b': 2.439361e-01, 'attn_q_norm': 2.512059e-01, 'attn_k_norm': 2.617892e-01, 'w_proj_rwkv': 1.577162e-01, 'w_proj_attn': 1.846671e-02, 'w_out': 1.332634e-01, 'ffn2_norm': 6.163397e+00, 'ffn2_w_in': 6.167965e-02, 'ffn2_w_out': 1.069560e-01}


def _to_microbatches(a, axis):
    t = _jnp.moveaxis(a, axis, 0)
    t = t.reshape((N_MICROBATCH, t.shape[0] // N_MICROBATCH) + t.shape[1:])
    return _jnp.moveaxis(t, 1, axis + 1)


def setup_inputs(seed: int = 0) -> dict:
    inp = _fwd_setup_inputs(seed)
    key = _jax.random.fold_in(_jax.random.key(seed), 7919)
    shape, _ = _output_shape()
    out = dict(inp)
    out["loss_target"] = _jax.random.normal(_jax.random.fold_in(key, 0), shape, _jnp.float32)
    for i, name in enumerate(TWIN_WEIGHTS):
        w = inp[name].astype(_jnp.float32)
        if MOMENT_SCALE is None:
            s = _jnp.sqrt(_jnp.mean(_jnp.square(w)) + 1e-30)
        else:
            s = MOMENT_SCALE[name]
        km, kv = _jax.random.split(_jax.random.fold_in(key, i + 1))
        out[name] = w
        out["m_" + name] = s * _jax.random.normal(km, w.shape, _jnp.float32)
        out["v_" + name] = (s * s) * _jax.random.uniform(kv, w.shape, _jnp.float32, 0.5, 1.5)
    if N_MICROBATCH > 1:
        for name, axis in PER_EXAMPLE_BATCH_AXIS.items():
            out[name] = _to_microbatches(out[name], axis)
    return {'x': out['x'], 'ffn1_norm': out['ffn1_norm'], 'ffn1_w_in': out['ffn1_w_in'], 'ffn1_w_out': out['ffn1_w_out'], 'mix_norm': out['mix_norm'], 'w_in': out['w_in'], 'b_gate': out['b_gate'], 'rwkv_mu': out['rwkv_mu'], 'rwkv_w0': out['rwkv_w0'], 'rwkv_w2': out['rwkv_w2'], 'rwkv_a0': out['rwkv_a0'], 'rwkv_a2': out['rwkv_a2'], 'rwkv_g2': out['rwkv_g2'], 'rwkv_k_k': out['rwkv_k_k'], 'rwkv_k_a': out['rwkv_k_a'], 'rwkv_r_k': out['rwkv_r_k'], 'rwkv_ln_w': out['rwkv_ln_w'], 'rwkv_ln_b': out['rwkv_ln_b'], 'attn_q_norm': out['attn_q_norm'], 'attn_k_norm': out['attn_k_norm'], 'w_proj_rwkv': out['w_proj_rwkv'], 'w_proj_attn': out['w_proj_attn'], 'w_out': out['w_out'], 'ffn2_norm': out['ffn2_norm'], 'ffn2_w_in': out['ffn2_w_in'], 'ffn2_w_out': out['ffn2_w_out'], 'loss_target': out['loss_target'], 'm_ffn1_norm': out['m_ffn1_norm'], 'm_ffn1_w_in': out['m_ffn1_w_in'], 'm_ffn1_w_out': out['m_ffn1_w_out'], 'm_mix_norm': out['m_mix_norm'], 'm_w_in': out['m_w_in'], 'm_b_gate': out['m_b_gate'], 'm_rwkv_mu': out['m_rwkv_mu'], 'm_rwkv_w0': out['m_rwkv_w0'], 'm_rwkv_w2': out['m_rwkv_w2'], 'm_rwkv_a0': out['m_rwkv_a0'], 'm_rwkv_a2': out['m_rwkv_a2'], 'm_rwkv_g2': out['m_rwkv_g2'], 'm_rwkv_k_k': out['m_rwkv_k_k'], 'm_rwkv_k_a': out['m_rwkv_k_a'], 'm_rwkv_r_k': out['m_rwkv_r_k'], 'm_rwkv_ln_w': out['m_rwkv_ln_w'], 'm_rwkv_ln_b': out['m_rwkv_ln_b'], 'm_attn_q_norm': out['m_attn_q_norm'], 'm_attn_k_norm': out['m_attn_k_norm'], 'm_w_proj_rwkv': out['m_w_proj_rwkv'], 'm_w_proj_attn': out['m_w_proj_attn'], 'm_w_out': out['m_w_out'], 'm_ffn2_norm': out['m_ffn2_norm'], 'm_ffn2_w_in': out['m_ffn2_w_in'], 'm_ffn2_w_out': out['m_ffn2_w_out'], 'v_ffn1_norm': out['v_ffn1_norm'], 'v_ffn1_w_in': out['v_ffn1_w_in'], 'v_ffn1_w_out': out['v_ffn1_w_out'], 'v_mix_norm': out['v_mix_norm'], 'v_w_in': out['v_w_in'], 'v_b_gate': out['v_b_gate'], 'v_rwkv_mu': out['v_rwkv_mu'], 'v_rwkv_w0': out['v_rwkv_w0'], 'v_rwkv_w2': out['v_rwkv_w2'], 'v_rwkv_a0': out['v_rwkv_a0'], 'v_rwkv_a2': out['v_rwkv_a2'], 'v_rwkv_g2': out['v_rwkv_g2'], 'v_rwkv_k_k': out['v_rwkv_k_k'], 'v_rwkv_k_a': out['v_rwkv_k_a'], 'v_rwkv_r_k': out['v_rwkv_r_k'], 'v_rwkv_ln_w': out['v_rwkv_ln_w'], 'v_rwkv_ln_b': out['v_rwkv_ln_b'], 'v_attn_q_norm': out['v_attn_q_norm'], 'v_attn_k_norm': out['v_attn_k_norm'], 'v_w_proj_rwkv': out['v_w_proj_rwkv'], 'v_w_proj_attn': out['v_w_proj_attn'], 'v_w_out': out['v_w_out'], 'v_ffn2_norm': out['v_ffn2_norm'], 'v_ffn2_w_in': out['v_ffn2_w_in'], 'v_ffn2_w_out': out['v_ffn2_w_out']}


def _loss(weights, diff, rest, loss_target):
    with _jax.named_scope("forward"):
        args = {**rest, TWIN_DIFF_INPUT: diff, **{k: w.astype(_WEIGHT_DTYPES[k]) for k, w in weights.items()}}
        y = _forward(args)
    with _jax.named_scope("loss_head"):
        err = _jnp.square(y.astype(_jnp.float32) - loss_target)
        return 0.5 * _jnp.sum(_jnp.mean(err, axis=-1)) if err.ndim else 0.5 * err


def _adamw(w, g, m, v):
    m = ADAM_B1 * m + (1.0 - ADAM_B1) * g
    v = ADAM_B2 * v + (1.0 - ADAM_B2) * _jnp.square(g)
    m_hat = m / (1.0 - ADAM_B1 ** ADAM_STEP)
    v_hat = v / (1.0 - ADAM_B2 ** ADAM_STEP)
    delta = -ADAM_LR * (m_hat / (_jnp.sqrt(v_hat) + ADAM_EPS) + ADAM_WD * w)
    return delta, m, v


def reference(x, ffn1_norm, ffn1_w_in, ffn1_w_out, mix_norm, w_in, b_gate, rwkv_mu, rwkv_w0, rwkv_w2, rwkv_a0, rwkv_a2, rwkv_g2, rwkv_k_k, rwkv_k_a, rwkv_r_k, rwkv_ln_w, rwkv_ln_b, attn_q_norm, attn_k_norm, w_proj_rwkv, w_proj_attn, w_out, ffn2_norm, ffn2_w_in, ffn2_w_out, loss_target, m_ffn1_norm, m_ffn1_w_in, m_ffn1_w_out, m_mix_norm, m_w_in, m_b_gate, m_rwkv_mu, m_rwkv_w0, m_rwkv_w2, m_rwkv_a0, m_rwkv_a2, m_rwkv_g2, m_rwkv_k_k, m_rwkv_k_a, m_rwkv_r_k, m_rwkv_ln_w, m_rwkv_ln_b, m_attn_q_norm, m_attn_k_norm, m_w_proj_rwkv, m_w_proj_attn, m_w_out, m_ffn2_norm, m_ffn2_w_in, m_ffn2_w_out, v_ffn1_norm, v_ffn1_w_in, v_ffn1_w_out, v_mix_norm, v_w_in, v_b_gate, v_rwkv_mu, v_rwkv_w0, v_rwkv_w2, v_rwkv_a0, v_rwkv_a2, v_rwkv_g2, v_rwkv_k_k, v_rwkv_k_a, v_rwkv_r_k, v_rwkv_ln_w, v_rwkv_ln_b, v_attn_q_norm, v_attn_k_norm, v_w_proj_rwkv, v_w_proj_attn, v_w_out, v_ffn2_norm, v_ffn2_w_in, v_ffn2_w_out):
    given = dict(x=x, ffn1_norm=ffn1_norm, ffn1_w_in=ffn1_w_in, ffn1_w_out=ffn1_w_out, mix_norm=mix_norm, w_in=w_in, b_gate=b_gate, rwkv_mu=rwkv_mu, rwkv_w0=rwkv_w0, rwkv_w2=rwkv_w2, rwkv_a0=rwkv_a0, rwkv_a2=rwkv_a2, rwkv_g2=rwkv_g2, rwkv_k_k=rwkv_k_k, rwkv_k_a=rwkv_k_a, rwkv_r_k=rwkv_r_k, rwkv_ln_w=rwkv_ln_w, rwkv_ln_b=rwkv_ln_b, attn_q_norm=attn_q_norm, attn_k_norm=attn_k_norm, w_proj_rwkv=w_proj_rwkv, w_proj_attn=w_proj_attn, w_out=w_out, ffn2_norm=ffn2_norm, ffn2_w_in=ffn2_w_in, ffn2_w_out=ffn2_w_out, loss_target=loss_target, m_ffn1_norm=m_ffn1_norm, m_ffn1_w_in=m_ffn1_w_in, m_ffn1_w_out=m_ffn1_w_out, m_mix_norm=m_mix_norm, m_w_in=m_w_in, m_b_gate=m_b_gate, m_rwkv_mu=m_rwkv_mu, m_rwkv_w0=m_rwkv_w0, m_rwkv_w2=m_rwkv_w2, m_rwkv_a0=m_rwkv_a0, m_rwkv_a2=m_rwkv_a2, m_rwkv_g2=m_rwkv_g2, m_rwkv_k_k=m_rwkv_k_k, m_rwkv_k_a=m_rwkv_k_a, m_rwkv_r_k=m_rwkv_r_k, m_rwkv_ln_w=m_rwkv_ln_w, m_rwkv_ln_b=m_rwkv_ln_b, m_attn_q_norm=m_attn_q_norm, m_attn_k_norm=m_attn_k_norm, m_w_proj_rwkv=m_w_proj_rwkv, m_w_proj_attn=m_w_proj_attn, m_w_out=m_w_out, m_ffn2_norm=m_ffn2_norm, m_ffn2_w_in=m_ffn2_w_in, m_ffn2_w_out=m_ffn2_w_out, v_ffn1_norm=v_ffn1_norm, v_ffn1_w_in=v_ffn1_w_in, v_ffn1_w_out=v_ffn1_w_out, v_mix_norm=v_mix_norm, v_w_in=v_w_in, v_b_gate=v_b_gate, v_rwkv_mu=v_rwkv_mu, v_rwkv_w0=v_rwkv_w0, v_rwkv_w2=v_rwkv_w2, v_rwkv_a0=v_rwkv_a0, v_rwkv_a2=v_rwkv_a2, v_rwkv_g2=v_rwkv_g2, v_rwkv_k_k=v_rwkv_k_k, v_rwkv_k_a=v_rwkv_k_a, v_rwkv_r_k=v_rwkv_r_k, v_rwkv_ln_w=v_rwkv_ln_w, v_rwkv_ln_b=v_rwkv_ln_b, v_attn_q_norm=v_attn_q_norm, v_attn_k_norm=v_attn_k_norm, v_w_proj_rwkv=v_w_proj_rwkv, v_w_proj_attn=v_w_proj_attn, v_w_out=v_w_out, v_ffn2_norm=v_ffn2_norm, v_ffn2_w_in=v_ffn2_w_in, v_ffn2_w_out=v_ffn2_w_out)
    weights = {n: given[n] for n in TWIN_WEIGHTS}
    shared = {n: given[n] for n in SHARED_INPUTS}
    per_example = {n: given[n] for n in ['x']}
    grad_fn = _jax.value_and_grad(_loss, argnums=(0, 1))

    def one_microbatch(ex, loss_target):
        ex = dict(ex)
        diff = ex.pop(TWIN_DIFF_INPUT)
        return grad_fn(weights, diff, {**shared, **ex}, loss_target)

    if N_MICROBATCH == 1:
        loss, (grad_w, grad_x) = one_microbatch(per_example, given["loss_target"])
    else:
        def body(carry, xs):
            loss_sum, grad_sum = carry
            l_k, (gw_k, gx_k) = one_microbatch(xs[0], xs[1])
            with _jax.named_scope("update"):
                return (loss_sum + l_k, _jax.tree.map(_jnp.add, grad_sum, gw_k)), gx_k

        init = (_jnp.zeros((), _jnp.float32), _jax.tree.map(_jnp.zeros_like, weights))
        (loss, grad_w), grad_x = _jax.lax.scan(body, init, (per_example, given["loss_target"]))
    with _jax.named_scope("update"):
        delta_w, new_m, new_v = {}, {}, {}
        for n in TWIN_WEIGHTS:
            delta_w[n], new_m[n], new_v[n] = _adamw(weights[n], grad_w[n], given["m_" + n], given["v_" + n])
    return (loss, grad_x, *[grad_w[n] for n in TWIN_WEIGHTS], *[delta_w[n] for n in TWIN_WEIGHTS],
            *[new_m[n] for n in TWIN_WEIGHTS], *[new_v[n] for n in TWIN_WEIGHTS])
```

```python
import functools

import jax
import jax.numpy as jnp
from jax import lax
from jax.experimental import pallas as pl
from jax.experimental.pallas import tpu as pltpu

f32 = jnp.float32
bf16 = jnp.bfloat16
HI = lax.Precision.HIGHEST
MESH = pl.DeviceIdType.MESH

N_DEV = 8
D = 1024
D_FF = 2816
HEAD = 64
RW_HEADS = 16
ATTN_PAIRS = ((128, 1), (512, 4), (2048, 16))
ATTN_BLK = 128
HEADS_PER_GROUP = 4
ATTN_W = 768
LORA_PAD = (128, 128, 256)
LORA_W = (64, 64, 160)
GN_EPS = 64e-5
RMS_EPS = 1e-6
NEG_INF = -1e30
WKV_T = 64
PACK_ROWS = 3584
SMALL_ROWS = 24
VMEM_LIMIT = 56 * 1024 * 1024

ADAM_LR, ADAM_B1, ADAM_B2, ADAM_EPS, ADAM_WD, ADAM_STEP = 0.001, 0.9, 0.999, 1e-08, 0.01, 10


def _cparams(sem):
    return pltpu.CompilerParams(dimension_semantics=sem, vmem_limit_bytes=VMEM_LIMIT)


def _pick(n, cands):
    for c in cands:
        if n % c == 0:
            return c
    return n


def _rowwise(name, fn, rows, params, outs, accs=(), tm=256):
    S = rows[0].shape[0]
    tm = min(tm, S)
    n_in = len(rows) + len(params)
    n_out = len(outs)
    n_acc = len(accs)

    def body(*refs):
        res = fn(*[r[...] for r in refs[:n_in]])
        if not isinstance(res, (tuple, list)):
            res = (res,)
        out_refs = refs[n_in:]
        for j in range(n_out):
            out_refs[j][...] = res[j].astype(out_refs[j].dtype)
        if n_acc:
            @pl.when(pl.program_id(0) == 0)
            def _():
                for j in range(n_acc):
                    out_refs[n_out + j][...] = jnp.zeros(out_refs[n_out + j].shape, f32)
            for j in range(n_acc):
                out_refs[n_out + j][...] += res[n_out + j]

    in_specs = [pl.BlockSpec((tm, a.shape[1]), lambda i: (i, 0)) for a in rows]
    in_specs += [pl.BlockSpec(p.shape, lambda i, nd=p.ndim: (0,) * nd) for p in params]
    out_specs = [pl.BlockSpec((tm, w), lambda i: (i, 0)) for w, _ in outs]
    out_specs += [pl.BlockSpec(s, lambda i: (0, 0)) for s in accs]
    out_shape = [jax.ShapeDtypeStruct((S, w), dt) for w, dt in outs]
    out_shape += [jax.ShapeDtypeStruct(s, f32) for s in accs]
    res = pl.pallas_call(
        body, name=name, grid=(S // tm,), in_specs=in_specs, out_specs=out_specs, out_shape=out_shape,
        compiler_params=_cparams(("arbitrary",)),
    )(*rows, *params)
    return res


def _mm(name, a, b, mode, out_dtype=f32, scale=None):
    if mode == "nn":
        (M, K), (_, N) = a.shape, b.shape
    elif mode == "nt":
        (M, K), (N, _) = a.shape, b.shape
    else:
        (K, M), (_, N) = a.shape, b.shape
    tm = _pick(M, (512, 256, 128))
    tn = _pick(N, (512, 256, 128))
    tk = K if K <= 2816 else _pick(K, (1024, 512, 256, 128))
    nk = K // tk
    if mode == "nn":
        a_spec = pl.BlockSpec((tm, tk), lambda i, j, k: (i, k))
        b_spec = pl.BlockSpec((tk, tn), lambda i, j, k: (k, j))
        dims = (((1,), (0,)), ((), ()))
    elif mode == "nt":
        a_spec = pl.BlockSpec((tm, tk), lambda i, j, k: (i, k))
        b_spec = pl.BlockSpec((tn, tk), lambda i, j, k: (j, k))
        dims = (((1,), (1,)), ((), ()))
    else:
        a_spec = pl.BlockSpec((tk, tm), lambda i, j, k: (k, i))
        b_spec = pl.BlockSpec((tk, tn), lambda i, j, k: (k, j))
        dims = (((0,), (0,)), ((), ()))

    def finish(acc):
        return acc if scale is None else acc * scale

    def body(a_ref, b_ref, o_ref, *scratch):
        part = lax.dot_general(a_ref[...].astype(bf16), b_ref[...].astype(bf16), dims,
                               preferred_element_type=f32)
        if nk == 1:
            o_ref[...] = finish(part).astype(o_ref.dtype)
        else:
            acc_ref = scratch[0]
            k = pl.program_id(2)

            @pl.when(k == 0)
            def _():
                acc_ref[...] = part

            @pl.when(k > 0)
            def _():
                acc_ref[...] += part

            @pl.when(k == nk - 1)
            def _():
                o_ref[...] = finish(acc_ref[...]).astype(o_ref.dtype)

    return pl.pallas_call(
        body, name=name, grid=(M // tm, N // tn, nk), in_specs=[a_spec, b_spec],
        out_specs=pl.BlockSpec((tm, tn), lambda i, j, k: (i, j)),
        out_shape=jax.ShapeDtypeStruct((M, N), out_dtype),
        scratch_shapes=[] if nk == 1 else [pltpu.VMEM((tm, tn), f32)],
        compiler_params=_cparams(("parallel", "parallel", "arbitrary")),
    )(a, b)


def _bmm(eq, a, b):
    return jnp.einsum(eq, a, b, precision=HI, preferred_element_type=f32)


def _tri_masks(T):
    ti = lax.broadcasted_iota(jnp.int32, (T, T), 0)
    si = lax.broadcasted_iota(jnp.int32, (T, T), 1)
    return ti >= si, ti > si


def _wkv_prep(r, lw, k, kkr, a):
    H, T, _ = r.shape
    low_i, low_s = _tri_masks(T)
    nrm = jnp.sqrt(jnp.sum(kkr * kkr, axis=-1, keepdims=True))
    den = jnp.maximum(nrm, 1e-12)
    kk = kkr / den
    tri = jnp.broadcast_to(low_i.astype(f32)[None], (H, T, T))
    cl = _bmm("hts,hsn->htn", tri, lw)
    c = jnp.exp(cl)
    cprev = jnp.exp(cl - lw)
    cinv = jnp.exp(-cl)
    rt = r * c
    at = -kk * cprev
    kt = k * cinv
    bt = kk * a * cinv
    A_ab = jnp.where(low_s[None], _bmm("htn,hsn->hts", at, bt), 0.0)
    A_ak = jnp.where(low_s[None], _bmm("htn,hsn->hts", at, kt), 0.0)
    A_rb = jnp.where(low_i[None], _bmm("htn,hsn->hts", rt, bt), 0.0)
    A_rk = jnp.where(low_i[None], _bmm("htn,hsn->hts", rt, kt), 0.0)
    eye = (lax.broadcasted_iota(jnp.int32, (T, T), 0) == lax.broadcasted_iota(jnp.int32, (T, T), 1)).astype(f32)
    inv = eye[None] + A_ab
    X = A_ab
    n = 1
    while 2 * n < T:
        X = _bmm("hts,hsu->htu", X, X)
        inv = inv + _bmm("hts,hsu->htu", inv, X)
        n *= 2
    return dict(kk=kk, den=den, nrm=nrm, c=c, cprev=cprev, cinv=cinv, rt=rt, at=at, kt=kt, bt=bt,
                A_ak=A_ak, A_rb=A_rb, A_rk=A_rk, inv=inv, cT=c[:, T - 1:T, :])


def _wkv_chunk_fwd(S0, r, lw, k, v, kkr, a):
    q = _wkv_prep(r, lw, k, kkr, a)
    rhs = _bmm("htk,hvk->htv", q["at"], S0) + _bmm("hts,hsv->htv", q["A_ak"], v)
    U = _bmm("hts,hsv->htv", q["inv"], rhs)
    Y = (_bmm("htk,hvk->htv", q["rt"], S0) + _bmm("hts,hsv->htv", q["A_rb"], U)
         + _bmm("hts,hsv->htv", q["A_rk"], v))
    S1 = (S0 + _bmm("htv,htk->hvk", U, q["bt"]) + _bmm("htv,htk->hvk", v, q["kt"])) * q["cT"]
    return Y, U, S1


def _wkv_chunk_bwd(S0, Hin, Q, r, lw, k, v, kkr, a, U, dY):
    H, T, _ = r.shape
    low_i, low_s = _tri_masks(T)
    li, ls = low_i[None], low_s[None]
    q = _wkv_prep(r, lw, k, kkr, a)
    rt, at, kt, bt, inv = q["rt"], q["at"], q["kt"], q["bt"], q["inv"]
    Hh = Hin * q["cT"]
    rhs = _bmm("htk,hvk->htv", bt, Hh) + _bmm("hst,hsv->htv", q["A_rb"], dY)
    Z = _bmm("hst,hsv->htv", inv, rhs)
    dYU = _bmm("htv,hsv->hts", dY, U)
    dYV = _bmm("htv,hsv->hts", dY, v)
    ZU = _bmm("htv,hsv->hts", Z, U)
    ZV = _bmm("htv,hsv->hts", Z, v)
    dYU_i, dYV_i = jnp.where(li, dYU, 0.0), jnp.where(li, dYV, 0.0)
    ZU_s, ZV_s = jnp.where(ls, ZU, 0.0), jnp.where(ls, ZV, 0.0)
    dr = (_bmm("htv,hvk->htk", dY, S0) + _bmm("hts,hsk->htk", dYU_i, bt)
          + _bmm("hts,hsk->htk", dYV_i, kt)) * q["c"]
    dv = (_bmm("htk,hvk->htv", kt, Hh) + _bmm("hst,hsv->htv", q["A_rk"], dY)
          + _bmm("hst,hsv->htv", q["A_ak"], Z))
    dk = (_bmm("htv,hvk->htk", v, Hh) + _bmm("hst,hsk->htk", dYV_i, rt)
          + _bmm("hst,hsk->htk", ZV_s, at)) * q["cinv"]
    db = (_bmm("htv,hvk->htk", U, Hh) + _bmm("hst,hsk->htk", dYU_i, rt)
          + _bmm("hst,hsk->htk", ZU_s, at)) * q["cinv"]
    da = (_bmm("htv,hvk->htk", Z, S0) + _bmm("hts,hsk->htk", ZU_s, bt)
          + _bmm("hts,hsk->htk", ZV_s, kt)) * q["cprev"]
    H0 = Hh + _bmm("htv,htk->hvk", dY, rt) + _bmm("htv,htk->hvk", Z, at)
    kk = q["kk"]
    e = r * dr - kk * a * db - k * dk
    f = -kk * da
    tri_i = jnp.broadcast_to(low_i.astype(f32)[None], (H, T, T))
    tri_s = jnp.broadcast_to(low_s.astype(f32)[None], (H, T, T))
    dlw = _bmm("hst,hsn->htn", tri_i, e) + _bmm("hst,hsn->htn", tri_s, f) + Q
    Qn = Q + jnp.sum(e + f, axis=1, keepdims=True)
    dkk = db * a - da
    dasig = db * kk
    proj = jnp.sum(dkk * kk, axis=-1, keepdims=True)
    dkkr = jnp.where(q["nrm"] > 1e-12, dkk - kk * proj, dkk) / q["den"]
    return dr, dlw, dk, dv, dkkr, dasig, H0, Qn


def _wkv_fwd(r, lw, k, v, kkr, a, r_k, ln_w, ln_b):
    H, S, N = r.shape
    T = WKV_T
    nc = S // T

    def body(r_ref, lw_ref, k_ref, v_ref, kkr_ref, a_ref, rk_ref, lnw_ref, lnb_ref,
             y_ref, wkv_ref, u_ref, s0_ref, state):
        @pl.when(pl.program_id(0) == 0)
        def _():
            state[...] = jnp.zeros(state.shape, f32)

        S0 = state[...]
        s0_ref[0] = S0
        rr, kk2, vv = r_ref[...], k_ref[...], v_ref[...]
        Y, U, S1 = _wkv_chunk_fwd(S0, rr, lw_ref[...], kk2, vv, kkr_ref[...], a_ref[...])
        state[...] = S1
        wkv_ref[...] = Y
        u_ref[...] = U
        mean = jnp.mean(Y, axis=-1, keepdims=True)
        var = jnp.mean(jnp.square(Y - mean), axis=-1, keepdims=True)
        yn = (Y - mean) * lax.rsqrt(var + GN_EPS)
        bonus = jnp.sum(rr * kk2 * rk_ref[...], axis=-1, keepdims=True) * vv
        y_ref[...] = yn * lnw_ref[...] + lnb_ref[...] + bonus

    blk = pl.BlockSpec((H, T, N), lambda i: (0, i, 0))
    par = pl.BlockSpec((H, 1, N), lambda i: (0, 0, 0))
    seq = jax.ShapeDtypeStruct((H, S, N), f32)
    return pl.pallas_call(
        body, name="wkv_fwd", grid=(nc,), in_specs=[blk] * 6 + [par] * 3,
        out_specs=[blk, blk, blk, pl.BlockSpec((1, H, N, N), lambda i: (i, 0, 0, 0))],
        out_shape=[seq, seq, seq, jax.ShapeDtypeStruct((nc, H, N, N), f32)],
        scratch_shapes=[pltpu.VMEM((H, N, N), f32)],
        compiler_params=_cparams(("arbitrary",)),
    )(r, lw, k, v, kkr, a, r_k, ln_w, ln_b)


def _wkv_bwd(dy, r, lw, k, v, kkr, a, wkv, U, S0s, r_k, ln_w, ln_b):
    H, S, N = r.shape
    T = WKV_T
    nc = S // T

    def body(dy_ref, r_ref, lw_ref, k_ref, v_ref, kkr_ref, a_ref, wkv_ref, u_ref, s0_ref, rk_ref, lnw_ref, lnb_ref,
             dr_ref, dlw_ref, dk_ref, dv_ref, dkkr_ref, da_ref, drk_ref, dlnw_ref, dlnb_ref, hst, qst):
        @pl.when(pl.program_id(0) == 0)
        def _():
            hst[...] = jnp.zeros(hst.shape, f32)
            qst[...] = jnp.zeros(qst.shape, f32)
            drk_ref[...] = jnp.zeros(drk_ref.shape, f32)
            dlnw_ref[...] = jnp.zeros(dlnw_ref.shape, f32)
            dlnb_ref[...] = jnp.zeros(dlnb_ref.shape, f32)

        dya = dy_ref[...]
        rr, kk2, vv, Y = r_ref[...], k_ref[...], v_ref[...], wkv_ref[...]
        rk = rk_ref[...]
        s = jnp.sum(rr * kk2 * rk, axis=-1, keepdims=True)
        ds = jnp.sum(dya * vv, axis=-1, keepdims=True)
        mean = jnp.mean(Y, axis=-1, keepdims=True)
        var = jnp.mean(jnp.square(Y - mean), axis=-1, keepdims=True)
        rstd = lax.rsqrt(var + GN_EPS)
        yn = (Y - mean) * rstd
        dyn = dya * lnw_ref[...]
        dY = rstd * (dyn - jnp.mean(dyn, axis=-1, keepdims=True) - yn * jnp.mean(dyn * yn, axis=-1, keepdims=True))
        drk_ref[...] += jnp.sum(ds * rr * kk2, axis=1, keepdims=True)
        dlnw_ref[...] += jnp.sum(dya * yn, axis=1, keepdims=True)
        dlnb_ref[...] += jnp.sum(dya, axis=1, keepdims=True)
        dr, dlw, dk, dv, dkkr, dasig, H0, Qn = _wkv_chunk_bwd(
            s0_ref[0], hst[...], qst[...], rr, lw_ref[...], kk2, vv, kkr_ref[...], a_ref[...], u_ref[...], dY)
        hst[...] = H0
        qst[...] = Qn
        dr_ref[...] = dr + ds * kk2 * rk
        dlw_ref[...] = dlw
        dk_ref[...] = dk + ds * rr * rk
        dv_ref[...] = dv + dya * s
        dkkr_ref[...] = dkkr
        da_ref[...] = dasig

    blk = pl.BlockSpec((H, T, N), lambda i: (0, nc - 1 - i, 0))
    par = pl.BlockSpec((H, 1, N), lambda i: (0, 0, 0))
    seq = jax.ShapeDtypeStruct((H, S, N), f32)
    pout = jax.ShapeDtypeStruct((H, 1, N), f32)
    return pl.pallas_call(
        body, name="wkv_bwd", grid=(nc,),
        in_specs=[blk] * 9 + [pl.BlockSpec((1, H, N, N), lambda i: (nc - 1 - i, 0, 0, 0))] + [par] * 3,
        out_specs=[blk] * 6 + [par] * 3,
        out_shape=[seq] * 6 + [pout] * 3,
        scratch_shapes=[pltpu.VMEM((H, N, N), f32), pltpu.VMEM((H, 1, N), f32)],
        compiler_params=_cparams(("arbitrary",)),
    )(dy, r, lw, k, v, kkr, a, wkv, U, S0s, r_k, ln_w, ln_b)


ATTN_G = 8


def _attn_first_mask(S, G):
    nbg = HEADS_PER_GROUP * S // ATTN_BLK
    b = pl.program_id(0) * G + lax.broadcasted_iota(jnp.int32, (G, 1, 1), 0)
    nbs = [S // (d * ATTN_BLK) for _, d in ATTN_PAIRS]
    per = jnp.where(b < nbg, nbs[0], jnp.where(b < 2 * nbg, nbs[1], nbs[2]))
    return jnp.bitwise_and(b, per - 1) == 0


def _attn_norm(x, gain, scale):
    rs = lax.rsqrt(jnp.mean(x * x, axis=-1, keepdims=True) + RMS_EPS)
    return x * rs * (gain * scale), rs


def _attn_scores(qn, kn_c, kn_p, first):
    s_c = jnp.einsum("gqe,gke->gqk", qn.astype(bf16), kn_c.astype(bf16), preferred_element_type=f32)
    s_p = jnp.einsum("gqe,gke->gqk", qn.astype(bf16), kn_p.astype(bf16), preferred_element_type=f32)
    qi = lax.broadcasted_iota(jnp.int32, (1, ATTN_BLK, ATTN_BLK), 1)
    ki = lax.broadcasted_iota(jnp.int32, (1, ATTN_BLK, ATTN_BLK), 2)
    s_c = jnp.where(qi >= ki, s_c, NEG_INF)
    s_p = jnp.where(jnp.logical_and(ki >= qi, jnp.logical_not(first)), s_p, NEG_INF)
    return s_c, s_p


def _attn_fwd(q, k, kp, v, vp, qg, kg, S):
    NB = q.shape[0]
    G = ATTN_G

    def body(q_ref, k_ref, kp_ref, v_ref, vp_ref, qg_ref, kg_ref, o_ref, lse_ref):
        first = _attn_first_mask(S, G)
        qn, _ = _attn_norm(q_ref[...], qg_ref[...], HEAD ** -0.5)
        kn_c, _ = _attn_norm(k_ref[...], kg_ref[...], 1.0)
        kn_p, _ = _attn_norm(kp_ref[...], kg_ref[...], 1.0)
        s_c, s_p = _attn_scores(qn, kn_c, kn_p, first)
        m = jnp.maximum(jnp.max(s_c, axis=-1, keepdims=True), jnp.max(s_p, axis=-1, keepdims=True))
        p_c = jnp.exp(s_c - m)
        p_p = jnp.exp(s_p - m)
        den = jnp.sum(p_c, axis=-1, keepdims=True) + jnp.sum(p_p, axis=-1, keepdims=True)
        inv = 1.0 / den
        o = jnp.einsum("gqk,gke->gqe", (p_c * inv).astype(bf16), v_ref[...].astype(bf16), preferred_element_type=f32)
        o += jnp.einsum("gqk,gke->gqe", (p_p * inv).astype(bf16), vp_ref[...].astype(bf16), preferred_element_type=f32)
        o_ref[...] = o
        lse_ref[...] = jnp.broadcast_to(m + jnp.log(den), o.shape)

    blk = pl.BlockSpec((G, ATTN_BLK, HEAD), lambda i: (i, 0, 0))
    par = pl.BlockSpec((1, 1, HEAD), lambda i: (0, 0, 0))
    shp = jax.ShapeDtypeStruct((NB, ATTN_BLK, HEAD), f32)
    return pl.pallas_call(
        body, name="attn_fwd", grid=(NB // G,), in_specs=[blk] * 5 + [par] * 2, out_specs=[blk, blk],
        out_shape=[shp, shp], compiler_params=_cparams(("arbitrary",)),
    )(q, k, kp, v, vp, qg, kg)


def _attn_bwd(q, k, kp, v, vp, o, lse, do, dlse, qg, kg, S):
    NB = q.shape[0]
    G = ATTN_G

    def norm_bwd(dxn, x, rs, gain, scale):
        xh = x * rs
        dxh = dxn * (gain * scale)
        dx = rs * (dxh - xh * jnp.mean(dxh * xh, axis=-1, keepdims=True))
        dgain = jnp.sum(jnp.sum(dxn * xh * scale, axis=1, keepdims=True), axis=0, keepdims=True)
        return dx, dgain

    def body(q_ref, k_ref, kp_ref, v_ref, vp_ref, o_ref, lse_ref, do_ref, dlse_ref, qg_ref, kg_ref,
             dq_ref, dkc_ref, dkp_ref, dvc_ref, dvp_ref, dqg_ref, dkg_ref):
        @pl.when(pl.program_id(0) == 0)
        def _():
            dqg_ref[...] = jnp.zeros(dqg_ref.shape, f32)
            dkg_ref[...] = jnp.zeros(dkg_ref.shape, f32)

        first = _attn_first_mask(S, G)
        qx, kx, kpx = q_ref[...], k_ref[...], kp_ref[...]
        qg, kg = qg_ref[...], kg_ref[...]
        qn, rq = _attn_norm(qx, qg, HEAD ** -0.5)
        kn_c, rk_c = _attn_norm(kx, kg, 1.0)
        kn_p, rk_p = _attn_norm(kpx, kg, 1.0)
        s_c, s_p = _attn_scores(qn, kn_c, kn_p, first)
        lse = lse_ref[...][:, :, 0:1]
        p_c = jnp.exp(s_c - lse)
        p_p = jnp.exp(s_p - lse)
        dO = do_ref[...]
        dOb = dO.astype(bf16)
        vb, vpb = v_ref[...].astype(bf16), vp_ref[...].astype(bf16)
        dp_c = jnp.einsum("gqe,gke->gqk", dOb, vb, preferred_element_type=f32)
        dp_p = jnp.einsum("gqe,gke->gqk", dOb, vpb, preferred_element_type=f32)
        corr = dlse_ref[...][:, :, 0:1] - jnp.sum(dO * o_ref[...], axis=-1, keepdims=True)
        ds_c = (p_c * (dp_c + corr)).astype(bf16)
        ds_p = (p_p * (dp_p + corr)).astype(bf16)
        qnb, kcb, kpb = qn.astype(bf16), kn_c.astype(bf16), kn_p.astype(bf16)
        dqn = (jnp.einsum("gqk,gke->gqe", ds_c, kcb, preferred_element_type=f32)
               + jnp.einsum("gqk,gke->gqe", ds_p, kpb, preferred_element_type=f32))
        dkn_c = jnp.einsum("gqk,gqe->gke", ds_c, qnb, preferred_element_type=f32)
        dkn_p = jnp.einsum("gqk,gqe->gke", ds_p, qnb, preferred_element_type=f32)
        dvc_ref[...] = jnp.einsum("gqk,gqe->gke", p_c.astype(bf16), dOb, preferred_element_type=f32)
        dvp_ref[...] = jnp.einsum("gqk,gqe->gke", p_p.astype(bf16), dOb, preferred_element_type=f32)
        dq, dqg = norm_bwd(dqn, qx, rq, qg, HEAD ** -0.5)
        dkc, dkg1 = norm_bwd(dkn_c, kx, rk_c, kg, 1.0)
        dkp, dkg2 = norm_bwd(dkn_p, kpx, rk_p, kg, 1.0)
        dq_ref[...] = dq
        dkc_ref[...] = dkc
        dkp_ref[...] = dkp
        dqg_ref[...] += dqg
        dkg_ref[...] += dkg1 + dkg2

    blk = pl.BlockSpec((G, ATTN_BLK, HEAD), lambda i: (i, 0, 0))
    par = pl.BlockSpec((1, 1, HEAD), lambda i: (0, 0, 0))
    shp = jax.ShapeDtypeStruct((NB, ATTN_BLK, HEAD), f32)
    pshp = jax.ShapeDtypeStruct((1, 1, HEAD), f32)
    return pl.pallas_call(
        body, name="attn_bwd", grid=(NB // G,), in_specs=[blk] * 9 + [par] * 2,
        out_specs=[blk] * 5 + [par] * 2, out_shape=[shp] * 5 + [pshp] * 2,
        compiler_params=_cparams(("arbitrary",)),
    )(q, k, kp, v, vp, o, lse, do, dlse, qg, kg)


def _to_blocks(t, S):
    outs = []
    for gi, (_, d) in enumerate(ATTN_PAIRS):
        tg = t[:, gi * 256:(gi + 1) * 256].reshape(S // d, d, HEADS_PER_GROUP, HEAD)
        outs.append(tg.transpose(2, 1, 0, 3).reshape(-1, ATTN_BLK, HEAD))
    return jnp.concatenate(outs, axis=0)


def _from_blocks(b, S):
    nbg = HEADS_PER_GROUP * S // ATTN_BLK
    outs = []
    for gi, (_, d) in enumerate(ATTN_PAIRS):
        bg = b[gi * nbg:(gi + 1) * nbg].reshape(HEADS_PER_GROUP, d, S // d, HEAD)
        outs.append(bg.transpose(2, 1, 0, 3).reshape(S, HEADS_PER_GROUP * HEAD))
    return jnp.concatenate(outs, axis=1)


def _prev_block(b):
    return jnp.concatenate([jnp.zeros_like(b[:1]), b[:-1]], axis=0)


def _next_block(b):
    return jnp.concatenate([b[1:], jnp.zeros_like(b[:1])], axis=0)


def _to_heads(t):
    S = t.shape[0]
    return t.reshape(S, RW_HEADS, HEAD).transpose(1, 0, 2)


def _from_heads(t):
    S = t.shape[1]
    return t.transpose(1, 0, 2).reshape(S, RW_HEADS * HEAD)


def _shift_down(t):
    return jnp.concatenate([jnp.zeros_like(t[:1]), t[:-1]], axis=0)


def _shift_up(t):
    return jnp.concatenate([t[1:], jnp.zeros_like(t[:1])], axis=0)


def _rms(x, g):
    rs = lax.rsqrt(jnp.mean(x * x, axis=-1, keepdims=True) + RMS_EPS)
    return x * rs * g


def _f_rms(x, g):
    return _rms(x, g)


def _f_resid_rms(coef, x, f, g):
    xn = x + coef * f
    return xn, _rms(xn, g)


def _f_swiglu(u):
    gate, up = u[:, :D_FF], u[:, D_FF:]
    return gate * jax.nn.sigmoid(gate) * up


def _f_swiglu_bwd(dact, u):
    gate, up = u[:, :D_FF], u[:, D_FF:]
    sg = jax.nn.sigmoid(gate)
    silu = gate * sg
    dact = 0.5 * dact
    return jnp.concatenate([dact * up * (sg * (1.0 + gate * (1.0 - sg))), dact * silu], axis=1)


def _f_rms_bwd(n_parts, *args):
    dns = args[:n_parts]
    x, dres, g = args[n_parts:]
    dn = dns[0]
    for t in dns[1:]:
        dn = dn + t
    rs = lax.rsqrt(jnp.mean(x * x, axis=-1, keepdims=True) + RMS_EPS)
    xh = x * rs
    dxh = dn * g
    dx = rs * (dxh - xh * jnp.mean(dxh * xh, axis=-1, keepdims=True))
    return dres + dx, jnp.sum(dn * xh, axis=0, keepdims=True)


def _f_loss(x, f, tgt):
    y = x + 0.5 * f
    diff = y - tgt
    part = 0.5 * jnp.sum(jnp.mean(diff * diff, axis=-1, keepdims=True), axis=0, keepdims=True)
    return diff * (1.0 / D), jnp.broadcast_to(part, (1, 128))


def _dotb(a, b, dims):
    return lax.dot_general(a.astype(bf16), b.astype(bf16), dims, preferred_element_type=f32)


_NN = (((1,), (0,)), ((), ()))
_NT = (((1,), (1,)), ((), ()))
_TN = (((0,), (0,)), ((), ()))


def _rwkv_pre_core(prkv, prkv_prev, plora, plora_prev, mu_rkv, mu_lora, w0, w2p, a0, a2p, g2p, k_k, k_a):
    xs = prkv + (prkv_prev - prkv) * mu_rkv
    xl = plora + (plora_prev - plora) * mu_lora
    r, k, v = xs[:, :D], xs[:, D:2 * D], xs[:, 2 * D:]
    wd, ad, gd = xl[:, :128], xl[:, 128:256], xl[:, 256:]
    tw = jnp.tanh(wd)
    zw = w0 + _dotb(tw, w2p, _NN)
    sp = jnp.maximum(-zw, 0.0) + jnp.log(1.0 + jnp.exp(-jnp.abs(zw)))
    lw = -jnp.exp(-sp - 0.5)
    a = jax.nn.sigmoid(a0 + _dotb(ad, a2p, _NN))
    sg = jax.nn.sigmoid(gd)
    return dict(r=r, k=k, v=v, tw=tw, zw=zw, lw=lw, a=a, sg=sg, ad=ad)


def _f_rwkv_pre(*args):
    c = _rwkv_pre_core(*args)
    g2p, k_k, k_a = args[10], args[11], args[12]
    g = _dotb(c["sg"], g2p, _NN)
    k, a = c["k"], c["a"]
    return c["r"], c["lw"], k * (1.0 + (a - 1.0) * k_a), c["v"], k * k_k, a, g


def _f_rwkv_pre_bwd(prkv, prkv_prev, plora, plora_prev, dr, dlw, dk2, dv, dkkr, da, dya, yap,
                    mu_rkv, mu_lora, w0, w2p, a0, a2p, g2p, k_k, k_a):
    c = _rwkv_pre_core(prkv, prkv_prev, plora, plora_prev, mu_rkv, mu_lora, w0, w2p, a0, a2p, g2p, k_k, k_a)
    k, a, sg, tw, zw, lw = c["k"], c["a"], c["sg"], c["tw"], c["zw"], c["lw"]
    dg = dya * yap
    dsg = _dotb(dg, g2p, _NT)
    dgd = dsg * sg * (1.0 - sg)
    dg2p = _dotb(sg, dg, _TN)
    dk = dk2 * (1.0 + (a - 1.0) * k_a) + dkkr * k_k
    da_t = da + dk2 * k * k_a
    dk_a = jnp.sum(dk2 * k * (a - 1.0), axis=0, keepdims=True)
    dk_k = jnp.sum(dkkr * k, axis=0, keepdims=True)
    dza = da_t * a * (1.0 - a)
    da0 = jnp.sum(dza, axis=0, keepdims=True)
    dad = _dotb(dza, a2p, _NT)
    da2p = _dotb(c["ad"], dza, _TN)
    dzw = dlw * lw * jax.nn.sigmoid(-zw)
    dw0 = jnp.sum(dzw, axis=0, keepdims=True)
    dtw = _dotb(dzw, w2p, _NT)
    dw2p = _dotb(tw, dzw, _TN)
    dwd = dtw * (1.0 - tw * tw)
    dxs = jnp.concatenate([dr, dk, dv], axis=1)
    dxl = jnp.concatenate([dwd, dad, dgd], axis=1)
    dmu_rkv = jnp.sum(dxs * (prkv_prev - prkv), axis=0, keepdims=True)
    dmu_lora = jnp.sum(dxl * (plora_prev - plora), axis=0, keepdims=True)
    return (dxs * (1.0 - mu_rkv), dxs * mu_rkv, dxl * (1.0 - mu_lora), dxl * mu_lora,
            dmu_rkv, dmu_lora, dw0, da0, dk_k, dk_a, dw2p, da2p, dg2p)


def _group_alpha(lse):
    l0, l1, l2 = lse[:, :256], lse[:, 256:512], lse[:, 512:]
    m = jnp.maximum(jnp.maximum(l0, l1), l2)
    e0, e1, e2 = jnp.exp(l0 - m), jnp.exp(l1 - m), jnp.exp(l2 - m)
    inv = 1.0 / (e0 + e1 + e2)
    return jnp.concatenate([e0 * inv, e1 * inv, e2 * inv], axis=1)


def _f_combine(o, lse):
    return o * _group_alpha(lse)


def _f_combine_bwd(dyb, o, lse, bd):
    alpha = _group_alpha(lse)
    e = jnp.dot(dyb * o, bd, precision=HI, preferred_element_type=f32)
    ae = alpha * e
    tot = ae[:, :256] + ae[:, 256:512] + ae[:, 512:]
    return dyb * alpha, ae - alpha * jnp.concatenate([tot, tot, tot], axis=1)


def _f_merge(pgate, ta, tb, b_gate):
    gate = jax.nn.sigmoid(pgate + b_gate)
    return gate[:, :D] * ta + gate[:, D:] * tb


def _f_merge_bwd(dm, pgate, ta, tb, b_gate):
    gate = jax.nn.sigmoid(pgate + b_gate)
    ga, gb = gate[:, :D], gate[:, D:]
    dpg = jnp.concatenate([dm * ta * ga * (1.0 - ga), dm * tb * gb * (1.0 - gb)], axis=1)
    return dm * ga, dm * gb, dpg, jnp.sum(dpg, axis=0, keepdims=True)


def _f_mul(a, b):
    return a * b


def _f_adamw(w, g, m, v):
    m2 = ADAM_B1 * m + (1.0 - ADAM_B1) * g
    v2 = ADAM_B2 * v + (1.0 - ADAM_B2) * jnp.square(g)
    m_hat = m2 / (1.0 - ADAM_B1 ** ADAM_STEP)
    v_hat = v2 / (1.0 - ADAM_B2 ** ADAM_STEP)
    delta = -ADAM_LR * (m_hat / (jnp.sqrt(v_hat) + ADAM_EPS) + ADAM_WD * w)
    return delta, m2, v2


def _ffn_fwd(tag, n, WiT, Wo):
    S = n.shape[0]
    u = _mm(f"{tag}_up", n, WiT, "nt")
    (act,) = _rowwise(f"{tag}_swiglu", _f_swiglu, [u], [], [(D_FF, bf16)])
    f = _mm(f"{tag}_down", act, Wo, "nn")
    return u, act, f


def _ffn_bwd(tag, dxo, x_in, n, u, act, g, WiT, Wo):
    dact = _mm(f"{tag}_dact", dxo, Wo, "nt")
    dWo = _mm(f"{tag}_dwo", act, dxo, "tn", scale=0.5)
    (du,) = _rowwise(f"{tag}_dswiglu", _f_swiglu_bwd, [dact, u], [], [(2 * D_FF, bf16)], tm=128)
    dn = _mm(f"{tag}_dn", du, WiT, "nn")
    dWiT = _mm(f"{tag}_dwi", du, n, "tn")
    dx, dg = _rowwise(f"{tag}_drms", functools.partial(_f_rms_bwd, 1), [dn, x_in, dxo], [g], [(D, f32)], [(1, D)])
    return dx, dg, dWiT, dWo


def _local_step(x0, tgt, W, P):
    S = x0.shape[0]
    (n1,) = _rowwise("f1_rms", _f_rms, [x0], [P["ffn1_norm"]], [(D, bf16)])
    u1, act1, f1 = _ffn_fwd("f1", n1, W["f1_iT"], W["f1_o"])
    x1, h = _rowwise("mix_rms", functools.partial(_f_resid_rms, 0.5), [x0, f1], [P["mix_norm"]],
                     [(D, f32), (D, bf16)])
    prkv = _mm("p_rkv", h, W["in_rkvT"], "nt")
    plora = _mm("p_lora", h, W["in_loraT"], "nt")
    pqkv = _mm("p_qkv", h, W["in_qkvT"], "nt")
    pgate = _mm("p_gate", h, W["in_gateT"], "nt")
    prkv_prev, plora_prev = _shift_down(prkv), _shift_down(plora)
    pre_params = [P["mu_rkv"], P["mu_lora"], P["w0"], W["w2p"], P["a0"], W["a2p"], W["g2p"], P["k_k"], P["k_a"]]
    r, lw, k2, v, kkr, a, g = _rowwise("rwkv_pre", _f_rwkv_pre, [prkv, prkv_prev, plora, plora_prev], pre_params,
                                       [(D, f32)] * 7, tm=128)
    hm = [_to_heads(t) for t in (r, lw, k2, v, kkr, a)]
    hp = [P["r_k"].reshape(RW_HEADS, 1, HEAD), P["ln_w"].reshape(RW_HEADS, 1, HEAD), P["ln_b"].reshape(RW_HEADS, 1, HEAD)]
    yap_h, wkv_h, U_h, S0s = _wkv_fwd(*hm, *hp)
    yap = _from_heads(yap_h)
    (ya,) = _rowwise("ya_gate", _f_mul, [yap, g], [], [(D, bf16)])
    ta = _mm("proj_a", ya, W["pr"], "nn")
    qb, kb, vb = (_to_blocks(pqkv[:, i * ATTN_W:(i + 1) * ATTN_W], S) for i in range(3))
    kpb, vpb = _prev_block(kb), _prev_block(vb)
    qg, kg = P["q_norm"].reshape(1, 1, HEAD), P["k_norm"].reshape(1, 1, HEAD)
    o_b, lse_b = _attn_fwd(qb, kb, kpb, vb, vpb, qg, kg, S)
    o_t, lse_t = _from_blocks(o_b, S), _from_blocks(lse_b, S)
    (yb,) = _rowwise("attn_combine", _f_combine, [o_t, lse_t], [], [(ATTN_W, bf16)])
    tb = _mm("proj_b", yb, W["paT"], "nt")
    (merged,) = _rowwise("merge", _f_merge, [pgate, ta, tb], [P["b_gate"]], [(D, bf16)])
    mo = _mm("mix_out", merged, W["out"], "nn")
    x2, n2 = _rowwise("f2_rms", functools.partial(_f_resid_rms, 1.0), [x1, mo], [P["ffn2_norm"]],
                      [(D, f32), (D, bf16)])
    u2, act2, f2 = _ffn_fwd("f2", n2, W["f2_iT"], W["f2_o"])
    dx3, loss = _rowwise("loss", _f_loss, [x2, f2, tgt], [], [(D, f32)], [(1, 128)])
    G, Gs = {}, {}
    dx2, Gs["ffn2_norm"], G["f2_iT"], G["f2_o"] = _ffn_bwd("f2", dx3, x2, n2, u2, act2, P["ffn2_norm"],
                                                          W["f2_iT"], W["f2_o"])
    dmerged = _mm("d_merged", dx2, W["out"], "nt")
    G["out"] = _mm("dw_out", merged, dx2, "tn")
    dta, dtb, dpgate, Gs["b_gate"] = _rowwise("merge_bwd", _f_merge_bwd, [dmerged, pgate, ta, tb], [P["b_gate"]],
                                              [(D, bf16), (D, bf16), (2 * D, bf16)], [(1, 2 * D)])
    dya = _mm("d_ya", dta, W["pr"], "nt")
    G["pr"] = _mm("dw_pr", ya, dta, "tn")
    dyb = _mm("d_yb", dtb, W["paT"], "nn")
    G["paT"] = _mm("dw_pa", dtb, yb, "tn")
    (dyap,) = _rowwise("ya_gate_bwd", _f_mul, [dya, g], [], [(D, f32)])
    hg = _wkv_bwd(_to_heads(dyap), *hm, wkv_h, U_h, S0s, *hp)
    dr, dlw, dk2, dv, dkkr, da = (_from_heads(t) for t in hg[:6])
    Gs["r_k"], Gs["ln_w"], Gs["ln_b"] = (t.reshape(1, D) for t in hg[6:])
    lp = sum(LORA_PAD)
    (dprkv_a, dprkv_b, dplora_a, dplora_b, Gs["mu_rkv"], Gs["mu_lora"], Gs["w0"], Gs["a0"], Gs["k_k"], Gs["k_a"],
     dw2p, da2p, dg2p) = _rowwise(
        "rwkv_pre_bwd", _f_rwkv_pre_bwd,
        [prkv, prkv_prev, plora, plora_prev, dr, dlw, dk2, dv, dkkr, da, dya, yap], pre_params,
        [(3 * D, f32), (3 * D, f32), (lp, f32), (lp, f32)],
        [(1, 3 * D), (1, lp), (1, D), (1, D), (1, D), (1, D), (LORA_PAD[0], D), (LORA_PAD[1], D), (LORA_PAD[2], D)],
        tm=128)
    G["w2T"], G["a2T"], G["g2T"] = dw2p[:LORA_W[0]].T, da2p[:LORA_W[1]].T, dg2p[:LORA_W[2]].T
    dprkv = (dprkv_a + _shift_up(dprkv_b)).astype(bf16)
    dplora = (dplora_a + _shift_up(dplora_b)).astype(bf16)
    bd = (jnp.arange(ATTN_W)[:, None] // HEAD == jnp.arange(ATTN_W)[None, :] // HEAD).astype(f32)
    do_t, dlse_t = _rowwise("attn_combine_bwd", _f_combine_bwd, [dyb, o_t, lse_t], [bd], [(ATTN_W, f32)] * 2)
    dq_b, dkc_b, dkp_b, dvc_b, dvp_b, dqg, dkg = _attn_bwd(qb, kb, kpb, vb, vpb, o_b, lse_b, _to_blocks(do_t, S),
                                                            _to_blocks(dlse_t, S), qg, kg, S)
    Gs["q_norm"], Gs["k_norm"] = dqg.reshape(1, HEAD), dkg.reshape(1, HEAD)
    dpqkv = jnp.concatenate([_from_blocks(dq_b, S), _from_blocks(dkc_b + _next_block(dkp_b), S),
                             _from_blocks(dvc_b + _next_block(dvp_b), S)], axis=1).astype(bf16)
    dh = [_mm("dh_rkv", dprkv, W["in_rkvT"], "nn"), _mm("dh_lora", dplora, W["in_loraT"], "nn"),
          _mm("dh_qkv", dpqkv, W["in_qkvT"], "nn"), _mm("dh_gate", dpgate, W["in_gateT"], "nn")]
    dW_rkv = _mm("dw_rkv", dprkv, h, "tn")
    dW_lora = _mm("dw_lora", dplora, h, "tn")
    dW_qkv = _mm("dw_qkv", dpqkv, h, "tn")
    dW_gate = _mm("dw_gate", dpgate, h, "tn")
    o1, o2 = LORA_PAD[0], LORA_PAD[0] + LORA_PAD[1]
    G["inT"] = jnp.concatenate([dW_rkv, dW_lora[:LORA_W[0]], dW_lora[o1:o1 + LORA_W[1]], dW_lora[o2:o2 + LORA_W[2]],
                                dW_qkv, dW_gate], axis=0)
    dx1, Gs["mix_norm"] = _rowwise("mix_drms", functools.partial(_f_rms_bwd, 4), [*dh, x1, dx2], [P["mix_norm"]],
                                   [(D, f32)], [(1, D)])
    dx0, Gs["ffn1_norm"], G["f1_iT"], G["f1_o"] = _ffn_bwd("f1", dx1, x0, n1, u1, act1, P["ffn1_norm"],
                                                          W["f1_iT"], W["f1_o"])
    return loss[0, 0], dx0, G, Gs


def _peer(k):
    x, y, c = lax.axis_index("x"), lax.axis_index("y"), lax.axis_index("c")
    px = 1 - x if k & 4 else x
    py = 1 - y if k & 2 else y
    pc = 1 - c if k & 1 else c
    return (px, py, pc), 4 * px + 2 * py + pc


def _all_gather(pack):
    R, C = pack.shape

    def body(x_ref, out_ref, send_sems, recv_sems, local_sem):
        _, me = _peer(0)
        mine = pltpu.make_async_copy(x_ref, out_ref.at[me], local_sem)
        mine.start()
        sends = []
        for k in range(1, N_DEV):
            dev, _ = _peer(k)
            cp = pltpu.make_async_remote_copy(src_ref=x_ref, dst_ref=out_ref.at[me], send_sem=send_sems.at[k - 1],
                                              recv_sem=recv_sems.at[k - 1], device_id=dev, device_id_type=MESH)
            cp.start()
            sends.append(cp)
        for k in range(1, N_DEV):
            dev, idx = _peer(k)
            pltpu.make_async_remote_copy(src_ref=x_ref, dst_ref=out_ref.at[idx], send_sem=send_sems.at[k - 1],
                                         recv_sem=recv_sems.at[k - 1], device_id=dev, device_id_type=MESH).wait_recv()
        for cp in sends:
            cp.wait_send()
        mine.wait()

    return pl.pallas_call(
        body, name="weight_all_gather", out_shape=jax.ShapeDtypeStruct((N_DEV, R, C), pack.dtype),
        in_specs=[pl.BlockSpec(memory_space=pl.ANY)], out_specs=pl.BlockSpec(memory_space=pl.ANY),
        scratch_shapes=[pltpu.SemaphoreType.DMA((N_DEV - 1,)), pltpu.SemaphoreType.DMA((N_DEV - 1,)),
                        pltpu.SemaphoreType.DMA(())],
    )(pack)


def _grad_exchange(gpack):
    _, R, C = gpack.shape

    def body(g_ref, out_ref, send_sems, recv_sems, local_sem):
        _, me = _peer(0)
        mine = pltpu.make_async_copy(g_ref.at[me], out_ref.at[me], local_sem)
        mine.start()
        sends = []
        for k in range(1, N_DEV):
            dev, idx = _peer(k)
            cp = pltpu.make_async_remote_copy(src_ref=g_ref.at[idx], dst_ref=out_ref.at[me], send_sem=send_sems.at[k - 1],
                                              recv_sem=recv_sems.at[k - 1], device_id=dev, device_id_type=MESH)
            cp.start()
            sends.append(cp)
        for k in range(1, N_DEV):
            dev, idx = _peer(k)
            pltpu.make_async_remote_copy(src_ref=g_ref.at[idx], dst_ref=out_ref.at[idx], send_sem=send_sems.at[k - 1],
                                         recv_sem=recv_sems.at[k - 1], device_id=dev, device_id_type=MESH).wait_recv()
        for cp in sends:
            cp.wait_send()
        mine.wait()

    return pl.pallas_call(
        body, name="grad_exchange", out_shape=jax.ShapeDtypeStruct(gpack.shape, gpack.dtype),
        in_specs=[pl.BlockSpec(memory_space=pl.ANY)], out_specs=pl.BlockSpec(memory_space=pl.ANY),
        scratch_shapes=[pltpu.SemaphoreType.DMA((N_DEV - 1,)), pltpu.SemaphoreType.DMA((N_DEV - 1,)),
                        pltpu.SemaphoreType.DMA(())],
    )(gpack)


def _sum_slots(name, parts, tr):
    n, R, C = parts.shape

    def body(p_ref, o_ref):
        acc = p_ref[0]
        for i in range(1, n):
            acc = acc + p_ref[i]
        o_ref[...] = acc

    return pl.pallas_call(
        body, name=name, grid=(R // tr,), in_specs=[pl.BlockSpec((n, tr, C), lambda i: (0, i, 0))],
        out_specs=pl.BlockSpec((tr, C), lambda i: (i, 0)), out_shape=jax.ShapeDtypeStruct((R, C), f32),
        compiler_params=_cparams(("parallel",)),
    )(parts)


def _small_all_reduce(small):
    R, C = small.shape

    def body(x_ref, o_ref, buf, send_sems, recv_sems):
        _, me = _peer(0)
        buf[me] = x_ref[...]
        sends = []
        for k in range(1, N_DEV):
            dev, _ = _peer(k)
            cp = pltpu.make_async_remote_copy(src_ref=x_ref, dst_ref=buf.at[me], send_sem=send_sems.at[k - 1],
                                              recv_sem=recv_sems.at[k - 1], device_id=dev, device_id_type=MESH)
            cp.start()
            sends.append(cp)
        for k in range(1, N_DEV):
            dev, idx = _peer(k)
            pltpu.make_async_remote_copy(src_ref=x_ref, dst_ref=buf.at[idx], send_sem=send_sems.at[k - 1],
                                         recv_sem=recv_sems.at[k - 1], device_id=dev, device_id_type=MESH).wait_recv()
        for cp in sends:
            cp.wait_send()
        acc = buf[0]
        for i in range(1, N_DEV):
            acc = acc + buf[i]
        o_ref[...] = acc

    return pl.pallas_call(
        body, name="small_all_reduce", out_shape=jax.ShapeDtypeStruct((R, C), f32),
        in_specs=[pl.BlockSpec(memory_space=pltpu.VMEM)], out_specs=pl.BlockSpec(memory_space=pltpu.VMEM),
        scratch_shapes=[pltpu.VMEM((N_DEV, R, C), f32), pltpu.SemaphoreType.DMA((N_DEV - 1,)),
                        pltpu.SemaphoreType.DMA((N_DEV - 1,))],
    )(small)


_BIG = (
    ("ffn1_w_in", True), ("ffn1_w_out", False), ("w_in", True), ("rwkv_w2", True), ("rwkv_a2", True),
    ("rwkv_g2", True), ("w_proj_rwkv", False), ("w_proj_attn", True), ("w_out", False),
    ("ffn2_w_in", True), ("ffn2_w_out", False))
_SMALL = ("ffn1_norm", "mix_norm", "b_gate", "rwkv_mu", "rwkv_w0", "rwkv_a0", "rwkv_k_k", "rwkv_k_a", "rwkv_r_k",
          "rwkv_ln_w", "rwkv_ln_b", "attn_q_norm", "attn_k_norm", "ffn2_norm")


def _pack_big(shards):
    parts = []
    for name, tr in _BIG:
        t = shards[name]
        parts.append((t.T if tr else t).reshape(-1, D))
    used = sum(p.shape[0] for p in parts)
    parts.append(jnp.zeros((PACK_ROWS - used, D), parts[0].dtype))
    return jnp.concatenate(parts, axis=0)


def _unpack_big(pack, like):
    out, off = {}, 0
    for name, tr in _BIG:
        shp = like[name].shape
        n = shp[0] * shp[1] // D
        t = pack[off:off + n]
        out[name] = t.reshape(shp[1], shp[0]).T if tr else t.reshape(shp)
        off += n
    return out


def _small_rows(name, t):
    flat = t.reshape(-1)
    pad = (-flat.shape[0]) % D
    return jnp.pad(flat, (0, pad)).reshape(-1, D)


def _pack_small(vals):
    parts = [_small_rows(n, vals[n]) for n in _SMALL]
    used = sum(p.shape[0] for p in parts)
    parts.append(jnp.zeros((SMALL_ROWS - used, D), f32))
    return jnp.concatenate(parts, axis=0)


def _unpack_small(pack, like):
    out, off = {}, 0
    for n in _SMALL:
        size = like[n].size
        rows = -(-size // D)
        out[n] = pack[off:off + rows].reshape(-1)[:size].reshape(like[n].shape)
        off += rows
    return out


def _pad_rows(t, n):
    return jnp.concatenate([t, jnp.zeros((n - t.shape[0],) + t.shape[1:], t.dtype)], axis=0)


def _build_W(full):
    inT = full["w_in"]
    z64, z96 = jnp.zeros((64, D), inT.dtype), jnp.zeros((96, D), inT.dtype)
    return {
        "f1_iT": full["ffn1_w_in"], "f1_o": full["ffn1_w_out"], "f2_iT": full["ffn2_w_in"], "f2_o": full["ffn2_w_out"],
        "in_rkvT": inT[:3 * D],
        "in_loraT": jnp.concatenate([inT[3072:3136], z64, inT[3136:3200], z64, inT[3200:3360], z96], axis=0),
        "in_qkvT": inT[3360:3360 + 3 * ATTN_W], "in_gateT": inT[3360 + 3 * ATTN_W:],
        "w2p": jnp.concatenate([full["rwkv_w2"].T, z64], axis=0),
        "a2p": jnp.concatenate([full["rwkv_a2"].T, z64], axis=0),
        "g2p": jnp.concatenate([full["rwkv_g2"].T, z96], axis=0),
        "pr": full["w_proj_rwkv"], "paT": full["w_proj_attn"], "out": full["w_out"],
    }


def _build_P(Wl):
    mu = Wl["rwkv_mu"]
    z64f, z96f = jnp.zeros((1, 64), f32), jnp.zeros((1, 96), f32)
    return {
        "ffn1_norm": Wl["ffn1_norm"][None], "mix_norm": Wl["mix_norm"][None], "ffn2_norm": Wl["ffn2_norm"][None],
        "b_gate": Wl["b_gate"][None], "mu_rkv": mu[None, :3 * D],
        "mu_lora": jnp.concatenate([mu[None, 3072:3136], z64f, mu[None, 3136:3200], z64f, mu[None, 3200:3360], z96f], axis=1),
        "w0": Wl["rwkv_w0"][None], "a0": Wl["rwkv_a0"][None], "k_k": Wl["rwkv_k_k"][None], "k_a": Wl["rwkv_k_a"][None],
        "r_k": Wl["rwkv_r_k"].reshape(1, D), "ln_w": Wl["rwkv_ln_w"][None], "ln_b": Wl["rwkv_ln_b"][None],
        "q_norm": Wl["attn_q_norm"][None], "k_norm": Wl["attn_k_norm"][None],
    }


def kernel(x, ffn1_norm, ffn1_w_in, ffn1_w_out, mix_norm, w_in, b_gate, rwkv_mu, rwkv_w0, rwkv_w2, rwkv_a0, rwkv_a2, rwkv_g2, rwkv_k_k, rwkv_k_a, rwkv_r_k, rwkv_ln_w, rwkv_ln_b, attn_q_norm, attn_k_norm, w_proj_rwkv, w_proj_attn, w_out, ffn2_norm, ffn2_w_in, ffn2_w_out, loss_target, m_ffn1_norm, m_ffn1_w_in, m_ffn1_w_out, m_mix_norm, m_w_in, m_b_gate, m_rwkv_mu, m_rwkv_w0, m_rwkv_w2, m_rwkv_a0, m_rwkv_a2, m_rwkv_g2, m_rwkv_k_k, m_rwkv_k_a, m_rwkv_r_k, m_rwkv_ln_w, m_rwkv_ln_b, m_attn_q_norm, m_attn_k_norm, m_w_proj_rwkv, m_w_proj_attn, m_w_out, m_ffn2_norm, m_ffn2_w_in, m_ffn2_w_out, v_ffn1_norm, v_ffn1_w_in, v_ffn1_w_out, v_mix_norm, v_w_in, v_b_gate, v_rwkv_mu, v_rwkv_w0, v_rwkv_w2, v_rwkv_a0, v_rwkv_a2, v_rwkv_g2, v_rwkv_k_k, v_rwkv_k_a, v_rwkv_r_k, v_rwkv_ln_w, v_rwkv_ln_b, v_attn_q_norm, v_attn_k_norm, v_w_proj_rwkv, v_w_proj_attn, v_w_out, v_ffn2_norm, v_ffn2_w_in, v_ffn2_w_out):
    names = ("ffn1_norm", "ffn1_w_in", "ffn1_w_out", "mix_norm", "w_in", "b_gate", "rwkv_mu", "rwkv_w0", "rwkv_w2",
             "rwkv_a0", "rwkv_a2", "rwkv_g2", "rwkv_k_k", "rwkv_k_a", "rwkv_r_k", "rwkv_ln_w", "rwkv_ln_b",
             "attn_q_norm", "attn_k_norm", "w_proj_rwkv", "w_proj_attn", "w_out", "ffn2_norm", "ffn2_w_in", "ffn2_w_out")
    w_all = (ffn1_norm, ffn1_w_in, ffn1_w_out, mix_norm, w_in, b_gate, rwkv_mu, rwkv_w0, rwkv_w2, rwkv_a0, rwkv_a2,
             rwkv_g2, rwkv_k_k, rwkv_k_a, rwkv_r_k, rwkv_ln_w, rwkv_ln_b, attn_q_norm, attn_k_norm, w_proj_rwkv,
             w_proj_attn, w_out, ffn2_norm, ffn2_w_in, ffn2_w_out)
    m_all = (m_ffn1_norm, m_ffn1_w_in, m_ffn1_w_out, m_mix_norm, m_w_in, m_b_gate, m_rwkv_mu, m_rwkv_w0, m_rwkv_w2,
             m_rwkv_a0, m_rwkv_a2, m_rwkv_g2, m_rwkv_k_k, m_rwkv_k_a, m_rwkv_r_k, m_rwkv_ln_w, m_rwkv_ln_b,
             m_attn_q_norm, m_attn_k_norm, m_w_proj_rwkv, m_w_proj_attn, m_w_out, m_ffn2_norm, m_ffn2_w_in, m_ffn2_w_out)
    v_all = (v_ffn1_norm, v_ffn1_w_in, v_ffn1_w_out, v_mix_norm, v_w_in, v_b_gate, v_rwkv_mu, v_rwkv_w0, v_rwkv_w2,
             v_rwkv_a0, v_rwkv_a2, v_rwkv_g2, v_rwkv_k_k, v_rwkv_k_a, v_rwkv_r_k, v_rwkv_ln_w, v_rwkv_ln_b,
             v_attn_q_norm, v_attn_k_norm, v_w_proj_rwkv, v_w_proj_attn, v_w_out, v_ffn2_norm, v_ffn2_w_in, v_ffn2_w_out)
    Wl = {n: t[0] for n, t in zip(names, w_all)}
    Ml = {n: t[0] for n, t in zip(names, m_all)}
    Vl = {n: t[0] for n, t in zip(names, v_all)}
    big = [n for n, _ in _BIG]

    w_pack = _pack_big({n: Wl[n] for n in big})
    gathered = _all_gather(w_pack.astype(bf16))
    full, off = {}, 0
    for n, tr in _BIG:
        shp = Wl[n].shape
        rows = shp[0] * shp[1] // D
        t = gathered[:, off:off + rows]
        r_loc, c_loc = (shp[1], shp[0]) if tr else shp
        full[n] = t.reshape(N_DEV * r_loc, c_loc)
        off += rows
    W, P = _build_W(full), _build_P(Wl)

    loss_local, dx0, G, Gs = _local_step(x[0], loss_target[0], W, P)

    g_full = {"ffn1_w_in": G["f1_iT"], "ffn1_w_out": G["f1_o"], "w_in": G["inT"], "rwkv_w2": G["w2T"],
              "rwkv_a2": G["a2T"], "rwkv_g2": G["g2T"], "w_proj_rwkv": G["pr"], "w_proj_attn": G["paT"],
              "w_out": G["out"], "ffn2_w_in": G["f2_iT"], "ffn2_w_out": G["f2_o"]}
    parts = [g_full[n].reshape(N_DEV, -1, D) for n in big]
    used = sum(p.shape[1] for p in parts)
    parts.append(jnp.zeros((N_DEV, PACK_ROWS - used, D), f32))
    g_pack = _sum_slots("grad_sum", _grad_exchange(jnp.concatenate(parts, axis=1)), 256)

    mu_g = Gs["mu_rkv"], Gs["mu_lora"]
    o1, o2 = LORA_PAD[0], LORA_PAD[0] + LORA_PAD[1]
    g_small_local = {
        "ffn1_norm": Gs["ffn1_norm"], "mix_norm": Gs["mix_norm"], "b_gate": Gs["b_gate"],
        "rwkv_mu": jnp.concatenate([mu_g[0], mu_g[1][:, :64], mu_g[1][:, o1:o1 + 64], mu_g[1][:, o2:o2 + 160]], axis=1),
        "rwkv_w0": Gs["w0"], "rwkv_a0": Gs["a0"], "rwkv_k_k": Gs["k_k"], "rwkv_k_a": Gs["k_a"], "rwkv_r_k": Gs["r_k"],
        "rwkv_ln_w": Gs["ln_w"], "rwkv_ln_b": Gs["ln_b"], "attn_q_norm": Gs["q_norm"], "attn_k_norm": Gs["k_norm"],
        "ffn2_norm": Gs["ffn2_norm"]}
    gs_pack = _small_all_reduce(_pack_small(g_small_local))

    d_pack, m_pack, v_pack = _rowwise(
        "adamw_big", _f_adamw, [w_pack, g_pack, _pack_big({n: Ml[n] for n in big}), _pack_big({n: Vl[n] for n in big})],
        [], [(D, f32)] * 3)
    ds_pack, ms_pack, vs_pack = _rowwise(
        "adamw_small", _f_adamw, [_pack_small(Wl), gs_pack, _pack_small(Ml), _pack_small(Vl)], [], [(D, f32)] * 3)

    def unpack(pb, ps):
        d = _unpack_big(pb, Wl)
        d.update(_unpack_small(ps, Wl))
        return [d[n][None] for n in names]

    loss = lax.psum(loss_local, ("x", "y", "c"))
    return (loss, dx0[None], *unpack(g_pack, gs_pack), *unpack(d_pack, ds_pack), *unpack(m_pack, ms_pack),
            *unpack(v_pack, vs_pack))
```

```python
import functools

import jax
import jax.numpy as jnp
from jax import lax
from jax.experimental import pallas as pl
from jax.experimental.pallas import tpu as pltpu

f32 = jnp.float32
bf16 = jnp.bfloat16
HI = lax.Precision.HIGHEST
MESH = pl.DeviceIdType.MESH

N_DEV = 8
D = 1024
D_FF = 2816
HEAD = 64
RW_HEADS = 16
ATTN_PAIRS = ((128, 1), (512, 4), (2048, 16))
ATTN_BLK = 128
HEADS_PER_GROUP = 4
ATTN_W = 768
LORA_PAD = (128, 128, 256)
LORA_W = (64, 64, 160)
GN_EPS = 64e-5
RMS_EPS = 1e-6
NEG_INF = -1e30
WKV_T = 64
GRAD_WIRE = bf16
PACK_ROWS = 3584
SMALL_ROWS = 24
VMEM_LIMIT = 56 * 1024 * 1024

ADAM_LR, ADAM_B1, ADAM_B2, ADAM_EPS, ADAM_WD, ADAM_STEP = 0.001, 0.9, 0.999, 1e-08, 0.01, 10


def _cparams(sem):
    return pltpu.CompilerParams(dimension_semantics=sem, vmem_limit_bytes=VMEM_LIMIT)


def _pick(n, cands):
    for c in cands:
        if n % c == 0:
            return c
    return n


def _rowwise(name, fn, rows, params, outs, accs=(), tm=256):
    S = rows[0].shape[0]
    tm = min(tm, S)
    n_in = len(rows) + len(params)
    n_out = len(outs)
    n_acc = len(accs)

    def body(*refs):
        res = fn(*[r[...] for r in refs[:n_in]])
        if not isinstance(res, (tuple, list)):
            res = (res,)
        out_refs = refs[n_in:]
        for j in range(n_out):
            out_refs[j][...] = res[j].astype(out_refs[j].dtype)
        if n_acc:
            @pl.when(pl.program_id(0) == 0)
            def _():
                for j in range(n_acc):
                    out_refs[n_out + j][...] = jnp.zeros(out_refs[n_out + j].shape, f32)
            for j in range(n_acc):
                out_refs[n_out + j][...] += res[n_out + j]

    in_specs = [pl.BlockSpec((tm, a.shape[1]), lambda i: (i, 0)) for a in rows]
    in_specs += [pl.BlockSpec(p.shape, lambda i, nd=p.ndim: (0,) * nd) for p in params]
    out_specs = [pl.BlockSpec((tm, w), lambda i: (i, 0)) for w, _ in outs]
    out_specs += [pl.BlockSpec(s, lambda i: (0, 0)) for s in accs]
    out_shape = [jax.ShapeDtypeStruct((S, w), dt) for w, dt in outs]
    out_shape += [jax.ShapeDtypeStruct(s, f32) for s in accs]
    res = pl.pallas_call(
        body, name=name, grid=(S // tm,), in_specs=in_specs, out_specs=out_specs, out_shape=out_shape,
        compiler_params=_cparams(("arbitrary",)),
    )(*rows, *params)
    return res


def _mm(name, a, b, mode, out_dtype=f32, scale=None):
    if mode == "nn":
        (M, K), (_, N) = a.shape, b.shape
    elif mode == "nt":
        (M, K), (N, _) = a.shape, b.shape
    else:
        (K, M), (_, N) = a.shape, b.shape
    tm = _pick(M, (512, 256, 128))
    tn = _pick(N, (512, 256, 128))
    tk = K if K <= 2816 else _pick(K, (1024, 512, 256, 128))
    nk = K // tk
    if mode == "nn":
        a_spec = pl.BlockSpec((tm, tk), lambda i, j, k: (i, k))
        b_spec = pl.BlockSpec((tk, tn), lambda i, j, k: (k, j))
        dims = (((1,), (0,)), ((), ()))
    elif mode == "nt":
        a_spec = pl.BlockSpec((tm, tk), lambda i, j, k: (i, k))
        b_spec = pl.BlockSpec((tn, tk), lambda i, j, k: (j, k))
        dims = (((1,), (1,)), ((), ()))
    else:
        a_spec = pl.BlockSpec((tk, tm), lambda i, j, k: (k, i))
        b_spec = pl.BlockSpec((tk, tn), lambda i, j, k: (k, j))
        dims = (((0,), (0,)), ((), ()))

    def finish(acc):
        return acc if scale is None else acc * scale

    def body(a_ref, b_ref, o_ref, *scratch):
        part = lax.dot_general(a_ref[...].astype(bf16), b_ref[...].astype(bf16), dims,
                               preferred_element_type=f32)
        if nk == 1:
            o_ref[...] = finish(part).astype(o_ref.dtype)
        else:
            acc_ref = scratch[0]
            k = pl.program_id(2)

            @pl.when(k == 0)
            def _():
                acc_ref[...] = part

            @pl.when(k > 0)
            def _():
                acc_ref[...] += part

            @pl.when(k == nk - 1)
            def _():
                o_ref[...] = finish(acc_ref[...]).astype(o_ref.dtype)

    return pl.pallas_call(
        body, name=name, grid=(M // tm, N // tn, nk), in_specs=[a_spec, b_spec],
        out_specs=pl.BlockSpec((tm, tn), lambda i, j, k: (i, j)),
        out_shape=jax.ShapeDtypeStruct((M, N), out_dtype),
        scratch_shapes=[] if nk == 1 else [pltpu.VMEM((tm, tn), f32)],
        compiler_params=_cparams(("parallel", "parallel", "arbitrary")),
    )(a, b)


def _sp(x):
    hi = x.astype(bf16)
    return hi, (x - hi.astype(f32)).astype(bf16)


def _cat(parts):
    return tuple(jnp.concatenate(p, axis=1) for p in zip(*parts))


def _bmm(eq, a, b):
    (ah, al), (bh, bl) = a, b
    dot = functools.partial(jnp.einsum, eq, preferred_element_type=f32)
    return dot(ah, bh) + (dot(ah, bl) + dot(al, bh))


def _tri_dot(eq, tri, x):
    h1 = x.astype(bf16)
    r1 = x - h1.astype(f32)
    h2 = r1.astype(bf16)
    h3 = (r1 - h2.astype(f32)).astype(bf16)
    dot = functools.partial(jnp.einsum, eq, preferred_element_type=f32)
    return dot(tri, h1) + (dot(tri, h2) + dot(tri, h3))


def _tri_masks(T):
    ti = lax.broadcasted_iota(jnp.int32, (T, T), 0)
    si = lax.broadcasted_iota(jnp.int32, (T, T), 1)
    return ti >= si, ti > si


def _wkv_prep(r, lw, k, kkr, a):
    H, T, _ = r.shape
    low_i, low_s = _tri_masks(T)
    nrm = jnp.sqrt(jnp.sum(kkr * kkr, axis=-1, keepdims=True))
    den = jnp.maximum(nrm, 1e-12)
    kk = kkr / den
    tri = jnp.broadcast_to(low_i.astype(bf16)[None], (H, T, T))
    cl = _tri_dot("hts,hsn->htn", tri, lw)
    c = jnp.exp(cl)
    cprev = jnp.exp(cl - lw)
    cinv = jnp.exp(-cl)
    bt, kt = _sp(kk * a * cinv), _sp(k * cinv)
    L = _cat([_sp(r * c), _sp(-kk * cprev)])
    Mb = _bmm("htn,hsn->hts", L, bt)
    Mk = _bmm("htn,hsn->hts", L, kt)
    A_rb = jnp.where(low_i[None], Mb[:, :T], 0.0)
    A_ab = jnp.where(low_s[None], Mb[:, T:], 0.0)
    Mk = jnp.concatenate([jnp.where(low_i[None], Mk[:, :T], 0.0), jnp.where(low_s[None], Mk[:, T:], 0.0)], axis=1)
    return dict(kk=kk, den=den, nrm=nrm, c=c, cprev=cprev, cinv=cinv, L=L, kt=kt, bt=bt,
                A_ab=A_ab, A_rb=A_rb, Mk=Mk, cT=c[:, T - 1:T, :])


def _tri_inverse(A):
    T = A.shape[-1]
    eye = (lax.broadcasted_iota(jnp.int32, (T, T), 0) == lax.broadcasted_iota(jnp.int32, (T, T), 1)).astype(f32)
    inv = eye[None] + A
    X = A
    n = 1
    while 2 * n < T:
        Xs = _sp(X)
        X = _bmm("hts,hsu->htu", Xs, Xs)
        inv = inv + _bmm("hts,hsu->htu", _sp(inv), _sp(X))
        n *= 2
    return inv


def _wkv_chunk_fwd(S0, r, lw, k, v, kkr, a):
    T = r.shape[1]
    q = _wkv_prep(r, lw, k, kkr, a)
    inv = _tri_inverse(q["A_ab"])
    vs = _sp(v)
    P = _bmm("htk,hvk->htv", q["L"], _sp(S0)) + _bmm("hts,hsv->htv", _sp(q["Mk"]), vs)
    U = _bmm("hts,hsv->htv", _sp(inv), _sp(P[:, T:]))
    Us = _sp(U)
    Y = P[:, :T] + _bmm("hts,hsv->htv", _sp(q["A_rb"]), Us)
    S1 = (S0 + _bmm("htv,htk->hvk", _cat([Us, vs]), _cat([q["bt"], q["kt"]]))) * q["cT"]
    return Y, U, inv, S1


def _wkv_chunk_bwd(S0, Hin, Q, r, lw, k, v, kkr, a, U, inv, dY):
    H, T, _ = r.shape
    low_i, low_s = _tri_masks(T)
    q = _wkv_prep(r, lw, k, kkr, a)
    L, kt, bt = q["L"], q["kt"], q["bt"]
    R = _cat([bt, kt])
    Hh = Hin * q["cT"]
    Hs, S0s, dYs, vs, Us = _sp(Hh), _sp(S0), _sp(dY), _sp(v), _sp(U)
    RH = _bmm("htk,hvk->htv", R, Hs)
    Z = _bmm("hst,hsv->htv", _sp(inv), _sp(RH[:, :T] + _bmm("hst,hsv->htv", _sp(q["A_rb"]), dYs)))
    DZ = _cat([dYs, _sp(Z)])
    both = jnp.concatenate([jnp.broadcast_to(low_i[None], (1, T, T)), jnp.broadcast_to(low_s[None], (1, T, T))], axis=1)
    NU = _sp(jnp.where(both, _bmm("htv,hsv->hts", DZ, Us), 0.0))
    NV = _sp(jnp.where(both, _bmm("htv,hsv->hts", DZ, vs), 0.0))
    ra = _bmm("htv,hvk->htk", DZ, S0s) + _bmm("hts,hsk->htk", NU, bt) + _bmm("hts,hsk->htk", NV, kt)
    dr = ra[:, :T] * q["c"]
    da = ra[:, T:] * q["cprev"]
    dv = RH[:, T:] + _bmm("hst,hsv->htv", _sp(q["Mk"]), DZ)
    VH = _bmm("htv,hvk->htk", _cat([vs, Us]), Hs)
    dk = (VH[:, :T] + _bmm("hst,hsk->htk", NV, L)) * q["cinv"]
    db = (VH[:, T:] + _bmm("hst,hsk->htk", NU, L)) * q["cinv"]
    H0 = Hh + _bmm("htv,htk->hvk", DZ, L)
    kk = q["kk"]
    e = r * dr - kk * a * db - k * dk
    f = -kk * da
    tri_i = jnp.broadcast_to(low_i.astype(bf16)[None], (H, T, T))
    tri_s = jnp.broadcast_to(low_s.astype(bf16)[None], (H, T, T))
    dlw = _tri_dot("hst,hsn->htn", tri_i, e) + _tri_dot("hst,hsn->htn", tri_s, f) + Q
    Qn = Q + jnp.sum(e + f, axis=1, keepdims=True)
    dkk = db * a - da
    dasig = db * kk
    proj = jnp.sum(dkk * kk, axis=-1, keepdims=True)
    dkkr = jnp.where(q["nrm"] > 1e-12, dkk - kk * proj, dkk) / q["den"]
    return dr, dlw, dk, dv, dkkr, dasig, H0, Qn


def _heads(ref):
    return jnp.stack([ref[:, h * HEAD:(h + 1) * HEAD] for h in range(RW_HEADS)], axis=0)


def _put_heads(ref, val):
    for h in range(RW_HEADS):
        ref[:, h * HEAD:(h + 1) * HEAD] = val[h]


def _wkv_fwd(r, lw, k, v, kkr, a, r_k, ln_w, ln_b):
    S = r.shape[0]
    H, N, T = RW_HEADS, HEAD, WKV_T
    nc = S // T

    def body(r_ref, lw_ref, k_ref, v_ref, kkr_ref, a_ref, rk_ref, lnw_ref, lnb_ref,
             y_ref, wkv_ref, u_ref, inv_ref, s0_ref, state):
        @pl.when(pl.program_id(0) == 0)
        def _():
            state[...] = jnp.zeros(state.shape, f32)

        S0 = state[...]
        s0_ref[0] = S0
        rr, kk2, vv = _heads(r_ref), _heads(k_ref), _heads(v_ref)
        Y, U, inv, S1 = _wkv_chunk_fwd(S0, rr, _heads(lw_ref), kk2, vv, _heads(kkr_ref), _heads(a_ref))
        state[...] = S1
        wkv_ref[...] = Y
        u_ref[...] = U
        inv_ref[...] = inv
        mean = jnp.mean(Y, axis=-1, keepdims=True)
        var = jnp.mean(jnp.square(Y - mean), axis=-1, keepdims=True)
        yn = (Y - mean) * lax.rsqrt(var + GN_EPS)
        bonus = jnp.sum(rr * kk2 * rk_ref[...], axis=-1, keepdims=True) * vv
        _put_heads(y_ref, yn * lnw_ref[...] + lnb_ref[...] + bonus)

    tok = pl.BlockSpec((T, H * N), lambda i: (i, 0))
    blk = pl.BlockSpec((H, T, N), lambda i: (0, i, 0))
    par = pl.BlockSpec((H, 1, N), lambda i: (0, 0, 0))
    seq = jax.ShapeDtypeStruct((H, S, N), f32)
    return pl.pallas_call(
        body, name="wkv_fwd", grid=(nc,), in_specs=[tok] * 6 + [par] * 3,
        out_specs=[tok, blk, blk, blk, pl.BlockSpec((1, H, N, N), lambda i: (i, 0, 0, 0))],
        out_shape=[jax.ShapeDtypeStruct((S, H * N), f32), seq, seq, seq, jax.ShapeDtypeStruct((nc, H, N, N), f32)],
        scratch_shapes=[pltpu.VMEM((H, N, N), f32)],
        compiler_params=_cparams(("arbitrary",)),
    )(r, lw, k, v, kkr, a, r_k, ln_w, ln_b)


def _wkv_bwd(dy, r, lw, k, v, kkr, a, wkv, U, inv, S0s, r_k, ln_w, ln_b):
    S = r.shape[0]
    H, N, T = RW_HEADS, HEAD, WKV_T
    nc = S // T

    def body(dy_ref, r_ref, lw_ref, k_ref, v_ref, kkr_ref, a_ref, wkv_ref, u_ref, inv_ref, s0_ref,
             rk_ref, lnw_ref, lnb_ref,
             dr_ref, dlw_ref, dk_ref, dv_ref, dkkr_ref, da_ref, drk_ref, dlnw_ref, dlnb_ref, hst, qst):
        @pl.when(pl.program_id(0) == 0)
        def _():
            hst[...] = jnp.zeros(hst.shape, f32)
            qst[...] = jnp.zeros(qst.shape, f32)
            drk_ref[...] = jnp.zeros(drk_ref.shape, f32)
            dlnw_ref[...] = jnp.zeros(dlnw_ref.shape, f32)
            dlnb_ref[...] = jnp.zeros(dlnb_ref.shape, f32)

        dya = _heads(dy_ref)
        rr, kk2, vv, Y = _heads(r_ref), _heads(k_ref), _heads(v_ref), wkv_ref[...]
        rk = rk_ref[...]
        s = jnp.sum(rr * kk2 * rk, axis=-1, keepdims=True)
        ds = jnp.sum(dya * vv, axis=-1, keepdims=True)
        mean = jnp.mean(Y, axis=-1, keepdims=True)
        var = jnp.mean(jnp.square(Y - mean), axis=-1, keepdims=True)
        rstd = lax.rsqrt(var + GN_EPS)
        yn = (Y - mean) * rstd
        dyn = dya * lnw_ref[...]
        dY = rstd * (dyn - jnp.mean(dyn, axis=-1, keepdims=True) - yn * jnp.mean(dyn * yn, axis=-1, keepdims=True))
        drk_ref[...] += jnp.sum(ds * rr * kk2, axis=1, keepdims=True)
        dlnw_ref[...] += jnp.sum(dya * yn, axis=1, keepdims=True)
        dlnb_ref[...] += jnp.sum(dya, axis=1, keepdims=True)
        dr, dlw, dk, dv, dkkr, dasig, H0, Qn = _wkv_chunk_bwd(
            s0_ref[0], hst[...], qst[...], rr, _heads(lw_ref), kk2, vv, _heads(kkr_ref), _heads(a_ref), u_ref[...],
            inv_ref[...], dY)
        hst[...] = H0
        qst[...] = Qn
        _put_heads(dr_ref, dr + ds * kk2 * rk)
        _put_heads(dlw_ref, dlw)
        _put_heads(dk_ref, dk + ds * rr * rk)
        _put_heads(dv_ref, dv + dya * s)
        _put_heads(dkkr_ref, dkkr)
        _put_heads(da_ref, dasig)

    tok = pl.BlockSpec((T, H * N), lambda i: (nc - 1 - i, 0))
    blk = pl.BlockSpec((H, T, N), lambda i: (0, nc - 1 - i, 0))
    par = pl.BlockSpec((H, 1, N), lambda i: (0, 0, 0))
    seq = jax.ShapeDtypeStruct((S, H * N), f32)
    pout = jax.ShapeDtypeStruct((H, 1, N), f32)
    return pl.pallas_call(
        body, name="wkv_bwd", grid=(nc,),
        in_specs=[tok] * 7 + [blk] * 3 + [pl.BlockSpec((1, H, N, N), lambda i: (nc - 1 - i, 0, 0, 0))] + [par] * 3,
        out_specs=[tok] * 6 + [par] * 3,
        out_shape=[seq] * 6 + [pout] * 3,
        scratch_shapes=[pltpu.VMEM((H, N, N), f32), pltpu.VMEM((H, 1, N), f32)],
        compiler_params=_cparams(("arbitrary",)),
    )(dy, r, lw, k, v, kkr, a, wkv, U, inv, S0s, r_k, ln_w, ln_b)


ATTN_G = 8


def _attn_first_mask(S, G):
    nbg = HEADS_PER_GROUP * S // ATTN_BLK
    b = pl.program_id(0) * G + lax.broadcasted_iota(jnp.int32, (G, 1, 1), 0)
    nbs = [S // (d * ATTN_BLK) for _, d in ATTN_PAIRS]
    per = jnp.where(b < nbg, nbs[0], jnp.where(b < 2 * nbg, nbs[1], nbs[2]))
    return jnp.bitwise_and(b, per - 1) == 0


def _attn_norm(x, gain, scale):
    rs = lax.rsqrt(jnp.mean(x * x, axis=-1, keepdims=True) + RMS_EPS)
    return x * rs * (gain * scale), rs


def _attn_scores(qn, kn_c, kn_p, first):
    s_c = jnp.einsum("gqe,gke->gqk", qn.astype(bf16), kn_c.astype(bf16), preferred_element_type=f32)
    s_p = jnp.einsum("gqe,gke->gqk", qn.astype(bf16), kn_p.astype(bf16), preferred_element_type=f32)
    qi = lax.broadcasted_iota(jnp.int32, (1, ATTN_BLK, ATTN_BLK), 1)
    ki = lax.broadcasted_iota(jnp.int32, (1, ATTN_BLK, ATTN_BLK), 2)
    s_c = jnp.where(qi >= ki, s_c, NEG_INF)
    s_p = jnp.where(jnp.logical_and(ki >= qi, jnp.logical_not(first)), s_p, NEG_INF)
    return s_c, s_p


def _attn_fwd(q, k, kp, v, vp, qg, kg, S):
    NB = q.shape[0]
    G = ATTN_G

    def body(q_ref, k_ref, kp_ref, v_ref, vp_ref, qg_ref, kg_ref, o_ref, lse_ref):
        first = _attn_first_mask(S, G)
        qn, _ = _attn_norm(q_ref[...], qg_ref[...], HEAD ** -0.5)
        kn_c, _ = _attn_norm(k_ref[...], kg_ref[...], 1.0)
        kn_p, _ = _attn_norm(kp_ref[...], kg_ref[...], 1.0)
        s_c, s_p = _attn_scores(qn, kn_c, kn_p, first)
        m = jnp.maximum(jnp.max(s_c, axis=-1, keepdims=True), jnp.max(s_p, axis=-1, keepdims=True))
        p_c = jnp.exp(s_c - m)
        p_p = jnp.exp(s_p - m)
        den = jnp.sum(p_c, axis=-1, keepdims=True) + jnp.sum(p_p, axis=-1, keepdims=True)
        inv = 1.0 / den
        o = jnp.einsum("gqk,gke->gqe", (p_c * inv).astype(bf16), v_ref[...].astype(bf16), preferred_element_type=f32)
        o += jnp.einsum("gqk,gke->gqe", (p_p * inv).astype(bf16), vp_ref[...].astype(bf16), preferred_element_type=f32)
        o_ref[...] = o
        lse_ref[...] = jnp.broadcast_to(m + jnp.log(den), o.shape)

    blk = pl.BlockSpec((G, ATTN_BLK, HEAD), lambda i: (i, 0, 0))
    par = pl.BlockSpec((1, 1, HEAD), lambda i: (0, 0, 0))
    shp = jax.ShapeDtypeStruct((NB, ATTN_BLK, HEAD), f32)
    return pl.pallas_call(
        body, name="attn_fwd", grid=(NB // G,), in_specs=[blk] * 5 + [par] * 2, out_specs=[blk, blk],
        out_shape=[shp, shp], compiler_params=_cparams(("arbitrary",)),
    )(q, k, kp, v, vp, qg, kg)


def _attn_bwd(q, k, kp, v, vp, o, lse, do, dlse, qg, kg, S):
    NB = q.shape[0]
    G = ATTN_G

    def norm_bwd(dxn, x, rs, gain, scale):
        xh = x * rs
        dxh = dxn * (gain * scale)
        dx = rs * (dxh - xh * jnp.mean(dxh * xh, axis=-1, keepdims=True))
        dgain = jnp.sum(jnp.sum(dxn * xh * scale, axis=1, keepdims=True), axis=0, keepdims=True)
        return dx, dgain

    def body(q_ref, k_ref, kp_ref, v_ref, vp_ref, o_ref, lse_ref, do_ref, dlse_ref, qg_ref, kg_ref,
             dq_ref, dkc_ref, dkp_ref, dvc_ref, dvp_ref, dqg_ref, dkg_ref):
        @pl.when(pl.program_id(0) == 0)
        def _():
            dqg_ref[...] = jnp.zeros(dqg_ref.shape, f32)
            dkg_ref[...] = jnp.zeros(dkg_ref.shape, f32)

        first = _attn_first_mask(S, G)
        qx, kx, kpx = q_ref[...], k_ref[...], kp_ref[...]
        qg, kg = qg_ref[...], kg_ref[...]
        qn, rq = _attn_norm(qx, qg, HEAD ** -0.5)
        kn_c, rk_c = _attn_norm(kx, kg, 1.0)
        kn_p, rk_p = _attn_norm(kpx, kg, 1.0)
        s_c, s_p = _attn_scores(qn, kn_c, kn_p, first)
        lse = lse_ref[...][:, :, 0:1]
        p_c = jnp.exp(s_c - lse)
        p_p = jnp.exp(s_p - lse)
        dO = do_ref[...]
        dOb = dO.astype(bf16)
        vb, vpb = v_ref[...].astype(bf16), vp_ref[...].astype(bf16)
        dp_c = jnp.einsum("gqe,gke->gqk", dOb, vb, preferred_element_type=f32)
        dp_p = jnp.einsum("gqe,gke->gqk", dOb, vpb, preferred_element_type=f32)
        corr = dlse_ref[...][:, :, 0:1] - jnp.sum(dO * o_ref[...], axis=-1, keepdims=True)
        ds_c = (p_c * (dp_c + corr)).astype(bf16)
        ds_p = (p_p * (dp_p + corr)).astype(bf16)
        qnb, kcb, kpb = qn.astype(bf16), kn_c.astype(bf16), kn_p.astype(bf16)
        dqn = (jnp.einsum("gqk,gke->gqe", ds_c, kcb, preferred_element_type=f32)
               + jnp.einsum("gqk,gke->gqe", ds_p, kpb, preferred_element_type=f32))
        dkn_c = jnp.einsum("gqk,gqe->gke", ds_c, qnb, preferred_element_type=f32)
        dkn_p = jnp.einsum("gqk,gqe->gke", ds_p, qnb, preferred_element_type=f32)
        dvc_ref[...] = jnp.einsum("gqk,gqe->gke", p_c.astype(bf16), dOb, preferred_element_type=f32)
        dvp_ref[...] = jnp.einsum("gqk,gqe->gke", p_p.astype(bf16), dOb, preferred_element_type=f32)
        dq, dqg = norm_bwd(dqn, qx, rq, qg, HEAD ** -0.5)
        dkc, dkg1 = norm_bwd(dkn_c, kx, rk_c, kg, 1.0)
        dkp, dkg2 = norm_bwd(dkn_p, kpx, rk_p, kg, 1.0)
        dq_ref[...] = dq
        dkc_ref[...] = dkc
        dkp_ref[...] = dkp
        dqg_ref[...] += dqg
        dkg_ref[...] += dkg1 + dkg2

    blk = pl.BlockSpec((G, ATTN_BLK, HEAD), lambda i: (i, 0, 0))
    par = pl.BlockSpec((1, 1, HEAD), lambda i: (0, 0, 0))
    shp = jax.ShapeDtypeStruct((NB, ATTN_BLK, HEAD), f32)
    pshp = jax.ShapeDtypeStruct((1, 1, HEAD), f32)
    return pl.pallas_call(
        body, name="attn_bwd", grid=(NB // G,), in_specs=[blk] * 9 + [par] * 2,
        out_specs=[blk] * 5 + [par] * 2, out_shape=[shp] * 5 + [pshp] * 2,
        compiler_params=_cparams(("arbitrary",)),
    )(q, k, kp, v, vp, o, lse, do, dlse, qg, kg)


def _to_blocks(t, S):
    outs = []
    for gi, (_, d) in enumerate(ATTN_PAIRS):
        tg = t[:, gi * 256:(gi + 1) * 256].reshape(S // d, d, HEADS_PER_GROUP, HEAD)
        outs.append(tg.transpose(2, 1, 0, 3).reshape(-1, ATTN_BLK, HEAD))
    return jnp.concatenate(outs, axis=0)


def _from_blocks(b, S):
    nbg = HEADS_PER_GROUP * S // ATTN_BLK
    outs = []
    for gi, (_, d) in enumerate(ATTN_PAIRS):
        bg = b[gi * nbg:(gi + 1) * nbg].reshape(HEADS_PER_GROUP, d, S // d, HEAD)
        outs.append(bg.transpose(2, 1, 0, 3).reshape(S, HEADS_PER_GROUP * HEAD))
    return jnp.concatenate(outs, axis=1)


def _prev_block(b):
    return jnp.concatenate([jnp.zeros_like(b[:1]), b[:-1]], axis=0)


def _next_block(b):
    return jnp.concatenate([b[1:], jnp.zeros_like(b[:1])], axis=0)


def _shift_down(t):
    return jnp.concatenate([jnp.zeros_like(t[:1]), t[:-1]], axis=0)


def _shift_up(t):
    return jnp.concatenate([t[1:], jnp.zeros_like(t[:1])], axis=0)


def _rms(x, g):
    rs = lax.rsqrt(jnp.mean(x * x, axis=-1, keepdims=True) + RMS_EPS)
    return x * rs * g


def _f_rms(x, g):
    return _rms(x, g)


def _f_resid_rms(coef, x, f, g):
    xn = x + coef * f
    return xn, _rms(xn, g)


def _f_swiglu(u):
    gate, up = u[:, :D_FF], u[:, D_FF:]
    return gate * jax.nn.sigmoid(gate) * up


def _f_swiglu_bwd(dact, u):
    gate, up = u[:, :D_FF], u[:, D_FF:]
    sg = jax.nn.sigmoid(gate)
    silu = gate * sg
    dact = 0.5 * dact
    return jnp.concatenate([dact * up * (sg * (1.0 + gate * (1.0 - sg))), dact * silu], axis=1)


def _f_rms_bwd(n_parts, *args):
    dns = args[:n_parts]
    x, dres, g = args[n_parts:]
    dn = dns[0]
    for t in dns[1:]:
        dn = dn + t
    rs = lax.rsqrt(jnp.mean(x * x, axis=-1, keepdims=True) + RMS_EPS)
    xh = x * rs
    dxh = dn * g
    dx = rs * (dxh - xh * jnp.mean(dxh * xh, axis=-1, keepdims=True))
    return dres + dx, jnp.sum(dn * xh, axis=0, keepdims=True)


def _f_loss(x, f, tgt):
    y = x + 0.5 * f
    diff = y - tgt
    part = 0.5 * jnp.sum(jnp.mean(diff * diff, axis=-1, keepdims=True), axis=0, keepdims=True)
    return diff * (1.0 / D), jnp.broadcast_to(part, (1, 128))


def _dotb(a, b, dims):
    return lax.dot_general(a.astype(bf16), b.astype(bf16), dims, preferred_element_type=f32)


_NN = (((1,), (0,)), ((), ()))
_NT = (((1,), (1,)), ((), ()))
_TN = (((0,), (0,)), ((), ()))


def _rwkv_pre_core(prkv, prkv_prev, plora, plora_prev, mu_rkv, mu_lora, w0, w2p, a0, a2p, g2p, k_k, k_a):
    xs = prkv + (prkv_prev - prkv) * mu_rkv
    xl = plora + (plora_prev - plora) * mu_lora
    r, k, v = xs[:, :D], xs[:, D:2 * D], xs[:, 2 * D:]
    wd, ad, gd = xl[:, :128], xl[:, 128:256], xl[:, 256:]
    tw = jnp.tanh(wd)
    zw = w0 + _dotb(tw, w2p, _NN)
    sp = jnp.maximum(-zw, 0.0) + jnp.log(1.0 + jnp.exp(-jnp.abs(zw)))
    lw = -jnp.exp(-sp - 0.5)
    a = jax.nn.sigmoid(a0 + _dotb(ad, a2p, _NN))
    sg = jax.nn.sigmoid(gd)
    return dict(r=r, k=k, v=v, tw=tw, zw=zw, lw=lw, a=a, sg=sg, ad=ad)


def _f_rwkv_pre(*args):
    c = _rwkv_pre_core(*args)
    g2p, k_k, k_a = args[10], args[11], args[12]
    g = _dotb(c["sg"], g2p, _NN)
    k, a = c["k"], c["a"]
    return c["r"], c["lw"], k * (1.0 + (a - 1.0) * k_a), c["v"], k * k_k, a, g


def _f_rwkv_pre_bwd(prkv, prkv_prev, plora, plora_prev, dr, dlw, dk2, dv, dkkr, da, dya, yap,
                    mu_rkv, mu_lora, w0, w2p, a0, a2p, g2p, k_k, k_a):
    c = _rwkv_pre_core(prkv, prkv_prev, plora, plora_prev, mu_rkv, mu_lora, w0, w2p, a0, a2p, g2p, k_k, k_a)
    k, a, sg, tw, zw, lw = c["k"], c["a"], c["sg"], c["tw"], c["zw"], c["lw"]
    dg = dya * yap
    dsg = _dotb(dg, g2p, _NT)
    dgd = dsg * sg * (1.0 - sg)
    dg2p = _dotb(sg, dg, _TN)
    dk = dk2 * (1.0 + (a - 1.0) * k_a) + dkkr * k_k
    da_t = da + dk2 * k * k_a
    dk_a = jnp.sum(dk2 * k * (a - 1.0), axis=0, keepdims=True)
    dk_k = jnp.sum(dkkr * k, axis=0, keepdims=True)
    dza = da_t * a * (1.0 - a)
    da0 = jnp.sum(dza, axis=0, keepdims=True)
    dad = _dotb(dza, a2p, _NT)
    da2p = _dotb(c["ad"], dza, _TN)
    dzw = dlw * lw * jax.nn.sigmoid(-zw)
    dw0 = jnp.sum(dzw, axis=0, keepdims=True)
    dtw = _dotb(dzw, w2p, _NT)
    dw2p = _dotb(tw, dzw, _TN)
    dwd = dtw * (1.0 - tw * tw)
    dxs = jnp.concatenate([dr, dk, dv], axis=1)
    dxl = jnp.concatenate([dwd, dad, dgd], axis=1)
    dmu_rkv = jnp.sum(dxs * (prkv_prev - prkv), axis=0, keepdims=True)
    dmu_lora = jnp.sum(dxl * (plora_prev - plora), axis=0, keepdims=True)
    return (dxs * (1.0 - mu_rkv), dxs * mu_rkv, dxl * (1.0 - mu_lora), dxl * mu_lora,
            dmu_rkv, dmu_lora, dw0, da0, dk_k, dk_a, dw2p, da2p, dg2p)


def _group_alpha(lse):
    l0, l1, l2 = lse[:, :256], lse[:, 256:512], lse[:, 512:]
    m = jnp.maximum(jnp.maximum(l0, l1), l2)
    e0, e1, e2 = jnp.exp(l0 - m), jnp.exp(l1 - m), jnp.exp(l2 - m)
    inv = 1.0 / (e0 + e1 + e2)
    return jnp.concatenate([e0 * inv, e1 * inv, e2 * inv], axis=1)


def _f_combine(o, lse):
    return o * _group_alpha(lse)


def _f_combine_bwd(dyb, o, lse, bd):
    alpha = _group_alpha(lse)
    e = jnp.dot(dyb * o, bd, precision=HI, preferred_element_type=f32)
    ae = alpha * e
    tot = ae[:, :256] + ae[:, 256:512] + ae[:, 512:]
    return dyb * alpha, ae - alpha * jnp.concatenate([tot, tot, tot], axis=1)


def _f_merge(pgate, ta, tb, b_gate):
    gate = jax.nn.sigmoid(pgate + b_gate)
    return gate[:, :D] * ta + gate[:, D:] * tb


def _f_merge_bwd(dm, pgate, ta, tb, b_gate):
    gate = jax.nn.sigmoid(pgate + b_gate)
    ga, gb = gate[:, :D], gate[:, D:]
    dpg = jnp.concatenate([dm * ta * ga * (1.0 - ga), dm * tb * gb * (1.0 - gb)], axis=1)
    return dm * ga, dm * gb, dpg, jnp.sum(dpg, axis=0, keepdims=True)


def _f_mul(a, b):
    return a * b


def _f_adamw(w, g, m, v):
    m2 = ADAM_B1 * m + (1.0 - ADAM_B1) * g
    v2 = ADAM_B2 * v + (1.0 - ADAM_B2) * jnp.square(g)
    m_hat = m2 / (1.0 - ADAM_B1 ** ADAM_STEP)
    v_hat = v2 / (1.0 - ADAM_B2 ** ADAM_STEP)
    delta = -ADAM_LR * (m_hat / (jnp.sqrt(v_hat) + ADAM_EPS) + ADAM_WD * w)
    return delta, m2, v2


def _ffn_fwd(tag, n, WiT, Wo):
    S = n.shape[0]
    u = _mm(f"{tag}_up", n, WiT, "nt")
    (act,) = _rowwise(f"{tag}_swiglu", _f_swiglu, [u], [], [(D_FF, bf16)])
    f = _mm(f"{tag}_down", act, Wo, "nn")
    return u, act, f


def _ffn_bwd(tag, dxo, x_in, n, u, act, g, WiT, Wo):
    dact = _mm(f"{tag}_dact", dxo, Wo, "nt")
    dWo = _mm(f"{tag}_dwo", act, dxo, "tn", out_dtype=GRAD_WIRE, scale=0.5)
    (du,) = _rowwise(f"{tag}_dswiglu", _f_swiglu_bwd, [dact, u], [], [(2 * D_FF, bf16)], tm=128)
    dn = _mm(f"{tag}_dn", du, WiT, "nn")
    dWiT = _mm(f"{tag}_dwi", du, n, "tn", out_dtype=GRAD_WIRE)
    dx, dg = _rowwise(f"{tag}_drms", functools.partial(_f_rms_bwd, 1), [dn, x_in, dxo], [g], [(D, f32)], [(1, D)])
    return dx, dg, dWiT, dWo


def _local_step(x0, tgt, W, P):
    S = x0.shape[0]
    (n1,) = _rowwise("f1_rms", _f_rms, [x0], [P["ffn1_norm"]], [(D, bf16)])
    u1, act1, f1 = _ffn_fwd("f1", n1, W["f1_iT"], W["f1_o"])
    x1, h = _rowwise("mix_rms", functools.partial(_f_resid_rms, 0.5), [x0, f1], [P["mix_norm"]],
                     [(D, f32), (D, bf16)])
    prkv = _mm("p_rkv", h, W["in_rkvT"], "nt")
    plora = _mm("p_lora", h, W["in_loraT"], "nt")
    pqkv = _mm("p_qkv", h, W["in_qkvT"], "nt")
    pgate = _mm("p_gate", h, W["in_gateT"], "nt")
    prkv_prev, plora_prev = _shift_down(prkv), _shift_down(plora)
    pre_params = [P["mu_rkv"], P["mu_lora"], P["w0"], W["w2p"], P["a0"], W["a2p"], W["g2p"], P["k_k"], P["k_a"]]
    r, lw, k2, v, kkr, a, g = _rowwise("rwkv_pre", _f_rwkv_pre, [prkv, prkv_prev, plora, plora_prev], pre_params,
                                       [(D, f32)] * 7, tm=128)
    hm = [r, lw, k2, v, kkr, a]
    hp = [P["r_k"].reshape(RW_HEADS, 1, HEAD), P["ln_w"].reshape(RW_HEADS, 1, HEAD), P["ln_b"].reshape(RW_HEADS, 1, HEAD)]
    yap, wkv_h, U_h, inv_h, S0s = _wkv_fwd(*hm, *hp)
    (ya,) = _rowwise("ya_gate", _f_mul, [yap, g], [], [(D, bf16)])
    ta = _mm("proj_a", ya, W["pr"], "nn")
    qb, kb, vb = (_to_blocks(pqkv[:, i * ATTN_W:(i + 1) * ATTN_W], S) for i in range(3))
    kpb, vpb = _prev_block(kb), _prev_block(vb)
    qg, kg = P["q_norm"].reshape(1, 1, HEAD), P["k_norm"].reshape(1, 1, HEAD)
    o_b, lse_b = _attn_fwd(qb, kb, kpb, vb, vpb, qg, kg, S)
    o_t, lse_t = _from_blocks(o_b, S), _from_blocks(lse_b, S)
    (yb,) = _rowwise("attn_combine", _f_combine, [o_t, lse_t], [], [(ATTN_W, bf16)])
    tb = _mm("proj_b", yb, W["paT"], "nt")
    (merged,) = _rowwise("merge", _f_merge, [pgate, ta, tb], [P["b_gate"]], [(D, bf16)])
    mo = _mm("mix_out", merged, W["out"], "nn")
    x2, n2 = _rowwise("f2_rms", functools.partial(_f_resid_rms, 1.0), [x1, mo], [P["ffn2_norm"]],
                      [(D, f32), (D, bf16)])
    u2, act2, f2 = _ffn_fwd("f2", n2, W["f2_iT"], W["f2_o"])
    dx3, loss = _rowwise("loss", _f_loss, [x2, f2, tgt], [], [(D, f32)], [(1, 128)])
    G, Gs = {}, {}
    dx2, Gs["ffn2_norm"], G["f2_iT"], G["f2_o"] = _ffn_bwd("f2", dx3, x2, n2, u2, act2, P["ffn2_norm"],
                                                          W["f2_iT"], W["f2_o"])
    dmerged = _mm("d_merged", dx2, W["out"], "nt")
    G["out"] = _mm("dw_out", merged, dx2, "tn", out_dtype=GRAD_WIRE)
    dta, dtb, dpgate, Gs["b_gate"] = _rowwise("merge_bwd", _f_merge_bwd, [dmerged, pgate, ta, tb], [P["b_gate"]],
                                              [(D, bf16), (D, bf16), (2 * D, bf16)], [(1, 2 * D)])
    dya = _mm("d_ya", dta, W["pr"], "nt")
    G["pr"] = _mm("dw_pr", ya, dta, "tn", out_dtype=GRAD_WIRE)
    dyb = _mm("d_yb", dtb, W["paT"], "nn")
    G["paT"] = _mm("dw_pa", dtb, yb, "tn", out_dtype=GRAD_WIRE)
    (dyap,) = _rowwise("ya_gate_bwd", _f_mul, [dya, g], [], [(D, f32)])
    hg = _wkv_bwd(dyap, *hm, wkv_h, U_h, inv_h, S0s, *hp)
    dr, dlw, dk2, dv, dkkr, da = hg[:6]
    Gs["r_k"], Gs["ln_w"], Gs["ln_b"] = (t.reshape(1, D) for t in hg[6:])
    lp = sum(LORA_PAD)
    (dprkv_a, dprkv_b, dplora_a, dplora_b, Gs["mu_rkv"], Gs["mu_lora"], Gs["w0"], Gs["a0"], Gs["k_k"], Gs["k_a"],
     dw2p, da2p, dg2p) = _rowwise(
        "rwkv_pre_bwd", _f_rwkv_pre_bwd,
        [prkv, prkv_prev, plora, plora_prev, dr, dlw, dk2, dv, dkkr, da, dya, yap], pre_params,
        [(3 * D, f32), (3 * D, f32), (lp, f32), (lp, f32)],
        [(1, 3 * D), (1, lp), (1, D), (1, D), (1, D), (1, D), (LORA_PAD[0], D), (LORA_PAD[1], D), (LORA_PAD[2], D)],
        tm=128)
    G["w2T"], G["a2T"], G["g2T"] = dw2p[:LORA_W[0]].T, da2p[:LORA_W[1]].T, dg2p[:LORA_W[2]].T
    dprkv = (dprkv_a + _shift_up(dprkv_b)).astype(bf16)
    dplora = (dplora_a + _shift_up(dplora_b)).astype(bf16)
    bd = (jnp.arange(ATTN_W)[:, None] // HEAD == jnp.arange(ATTN_W)[None, :] // HEAD).astype(f32)
    do_t, dlse_t = _rowwise("attn_combine_bwd", _f_combine_bwd, [dyb, o_t, lse_t], [bd], [(ATTN_W, f32)] * 2)
    dq_b, dkc_b, dkp_b, dvc_b, dvp_b, dqg, dkg = _attn_bwd(qb, kb, kpb, vb, vpb, o_b, lse_b, _to_blocks(do_t, S),
                                                            _to_blocks(dlse_t, S), qg, kg, S)
    Gs["q_norm"], Gs["k_norm"] = dqg.reshape(1, HEAD), dkg.reshape(1, HEAD)
    dpqkv = jnp.concatenate([_from_blocks(dq_b, S), _from_blocks(dkc_b + _next_block(dkp_b), S),
                             _from_blocks(dvc_b + _next_block(dvp_b), S)], axis=1).astype(bf16)
    dh = [_mm("dh_rkv", dprkv, W["in_rkvT"], "nn"), _mm("dh_lora", dplora, W["in_loraT"], "nn"),
          _mm("dh_qkv", dpqkv, W["in_qkvT"], "nn"), _mm("dh_gate", dpgate, W["in_gateT"], "nn")]
    dW_rkv = _mm("dw_rkv", dprkv, h, "tn", out_dtype=GRAD_WIRE)
    dW_lora = _mm("dw_lora", dplora, h, "tn", out_dtype=GRAD_WIRE)
    dW_qkv = _mm("dw_qkv", dpqkv, h, "tn", out_dtype=GRAD_WIRE)
    dW_gate = _mm("dw_gate", dpgate, h, "tn", out_dtype=GRAD_WIRE)
    o1, o2 = LORA_PAD[0], LORA_PAD[0] + LORA_PAD[1]
    G["inT"] = jnp.concatenate([dW_rkv, dW_lora[:LORA_W[0]], dW_lora[o1:o1 + LORA_W[1]], dW_lora[o2:o2 + LORA_W[2]],
                                dW_qkv, dW_gate], axis=0)
    dx1, Gs["mix_norm"] = _rowwise("mix_drms", functools.partial(_f_rms_bwd, 4), [*dh, x1, dx2], [P["mix_norm"]],
                                   [(D, f32)], [(1, D)])
    dx0, Gs["ffn1_norm"], G["f1_iT"], G["f1_o"] = _ffn_bwd("f1", dx1, x0, n1, u1, act1, P["ffn1_norm"],
                                                          W["f1_iT"], W["f1_o"])
    return loss[0, 0], dx0, G, Gs


def _peer(k):
    x, y, c = lax.axis_index("x"), lax.axis_index("y"), lax.axis_index("c")
    px = 1 - x if k & 4 else x
    py = 1 - y if k & 2 else y
    pc = 1 - c if k & 1 else c
    return (px, py, pc), 4 * px + 2 * py + pc


def _all_gather(pack):
    R, C = pack.shape

    def body(x_ref, out_ref, send_sems, recv_sems, local_sem):
        x, y, c = lax.axis_index("x"), lax.axis_index("y"), lax.axis_index("c")
        me, sibling = (x, y, c), (x, y, 1 - c)
        chips = [(1 - x, y), (x, 1 - y), (1 - x, 1 - y)]

        def slot(px, py, pc):
            return out_ref.at[4 * px + 2 * py + pc]

        def copy(k, block, to, src=None):
            return pltpu.make_async_remote_copy(
                src_ref=slot(*block) if src is None else src, dst_ref=slot(*block), send_sem=send_sems.at[k],
                recv_sem=recv_sems.at[k], device_id=to, device_id_type=MESH)

        mine = pltpu.make_async_copy(x_ref, slot(*me), local_sem)
        mine.start()
        first = [copy(0, me, sibling, src=x_ref)]
        first += [copy(1 + j, me, (*chip, c), src=x_ref) for j, chip in enumerate(chips)]
        for cp in first:
            cp.start()
        passed = [copy(4 + j, (*chip, c), sibling) for j, chip in enumerate(chips)]
        for j, chip in enumerate(chips):
            copy(1 + j, (*chip, c), me).wait_recv()
            passed[j].start()
        copy(0, sibling, me).wait_recv()
        for j, chip in enumerate(chips):
            copy(4 + j, (*chip, 1 - c), me).wait_recv()
        for cp in first + passed:
            cp.wait_send()
        mine.wait()

    return pl.pallas_call(
        body, name="weight_all_gather", out_shape=jax.ShapeDtypeStruct((N_DEV, R, C), pack.dtype),
        in_specs=[pl.BlockSpec(memory_space=pl.ANY)], out_specs=pl.BlockSpec(memory_space=pl.ANY),
        scratch_shapes=[pltpu.SemaphoreType.DMA((N_DEV - 1,)), pltpu.SemaphoreType.DMA((N_DEV - 1,)),
                        pltpu.SemaphoreType.DMA(())],
    )(pack)


N_CHIP = 4


def _grad_pair(gpack):
    _, R, C = gpack.shape

    def body(g_ref, out_ref, send_sems, recv_sems):
        x, y, c = lax.axis_index("x"), lax.axis_index("y"), lax.axis_index("c")
        sibling = (x, y, 1 - c)
        copies = []
        for k in range(N_CHIP):
            dst_dev = 4 * (k // 2) + 2 * (k % 2) + (1 - c)
            cp = pltpu.make_async_remote_copy(src_ref=g_ref.at[dst_dev], dst_ref=out_ref.at[k], send_sem=send_sems.at[k],
                                              recv_sem=recv_sems.at[k], device_id=sibling, device_id_type=MESH)
            cp.start()
            copies.append(cp)
        for cp in copies:
            cp.wait()

    return pl.pallas_call(
        body, name="grad_pair", out_shape=jax.ShapeDtypeStruct((N_CHIP, R, C), gpack.dtype),
        in_specs=[pl.BlockSpec(memory_space=pl.ANY)], out_specs=pl.BlockSpec(memory_space=pl.ANY),
        scratch_shapes=[pltpu.SemaphoreType.DMA((N_CHIP,)), pltpu.SemaphoreType.DMA((N_CHIP,))],
    )(gpack)


def _pair_add(gpack, other, c, tr):
    _, R, C = gpack.shape

    def body(c_ref, g_ref, o_ref, out_ref):
        out_ref[...] = (g_ref[...].astype(f32) + o_ref[...].astype(f32)).astype(out_ref.dtype)

    return pl.pallas_call(
        body, name="pair_add",
        grid_spec=pltpu.PrefetchScalarGridSpec(
            num_scalar_prefetch=1, grid=(N_CHIP, R // tr),
            in_specs=[pl.BlockSpec((1, None, tr, C), lambda k, i, c_ref: (k, c_ref[0], i, 0)),
                      pl.BlockSpec((1, tr, C), lambda k, i, c_ref: (k, i, 0))],
            out_specs=pl.BlockSpec((1, tr, C), lambda k, i, c_ref: (k, i, 0))),
        out_shape=jax.ShapeDtypeStruct((N_CHIP, R, C), gpack.dtype),
        compiler_params=_cparams(("parallel", "parallel")),
    )(c, gpack.reshape(N_CHIP, 2, R, C), other)


def _grad_cross(part):
    _, R, C = part.shape

    def body(p_ref, out_ref, send_sems, recv_sems, local_sem):
        x, y, c = lax.axis_index("x"), lax.axis_index("y"), lax.axis_index("c")
        my_chip = 2 * x + y
        mine = pltpu.make_async_copy(p_ref.at[my_chip], out_ref.at[my_chip], local_sem)
        mine.start()
        flips = [(1, 0), (0, 1), (1, 1)]
        copies = []
        for j, (fx, fy) in enumerate(flips):
            px = 1 - x if fx else x
            py = 1 - y if fy else y
            cp = pltpu.make_async_remote_copy(src_ref=p_ref.at[2 * px + py], dst_ref=out_ref.at[my_chip],
                                              send_sem=send_sems.at[j], recv_sem=recv_sems.at[j],
                                              device_id=(px, py, c), device_id_type=MESH)
            cp.start()
            copies.append((cp, 2 * px + py))
        for j, (cp, peer_chip) in enumerate(copies):
            pltpu.make_async_remote_copy(src_ref=p_ref.at[peer_chip], dst_ref=out_ref.at[peer_chip],
                                         send_sem=send_sems.at[j], recv_sem=recv_sems.at[j],
                                         device_id=(x, y, c), device_id_type=MESH).wait_recv()
        for cp, _ in copies:
            cp.wait_send()
        mine.wait()

    return pl.pallas_call(
        body, name="grad_cross", out_shape=jax.ShapeDtypeStruct(part.shape, part.dtype),
        in_specs=[pl.BlockSpec(memory_space=pl.ANY)], out_specs=pl.BlockSpec(memory_space=pl.ANY),
        scratch_shapes=[pltpu.SemaphoreType.DMA((3,)), pltpu.SemaphoreType.DMA((3,)), pltpu.SemaphoreType.DMA(())],
    )(part)


def _sum_slots(name, parts, tr):
    n, R, C = parts.shape

    def body(p_ref, o_ref):
        acc = p_ref[0].astype(f32)
        for i in range(1, n):
            acc = acc + p_ref[i].astype(f32)
        o_ref[...] = acc

    return pl.pallas_call(
        body, name=name, grid=(R // tr,), in_specs=[pl.BlockSpec((n, tr, C), lambda i: (0, i, 0))],
        out_specs=pl.BlockSpec((tr, C), lambda i: (i, 0)), out_shape=jax.ShapeDtypeStruct((R, C), f32),
        compiler_params=_cparams(("parallel",)),
    )(parts)


def _small_all_reduce(small):
    R, C = small.shape

    def body(x_ref, o_ref, buf, send_sems, recv_sems):
        _, me = _peer(0)
        buf[me] = x_ref[...]
        sends = []
        for k in range(1, N_DEV):
            dev, _ = _peer(k)
            cp = pltpu.make_async_remote_copy(src_ref=x_ref, dst_ref=buf.at[me], send_sem=send_sems.at[k - 1],
                                              recv_sem=recv_sems.at[k - 1], device_id=dev, device_id_type=MESH)
            cp.start()
            sends.append(cp)
        for k in range(1, N_DEV):
            dev, idx = _peer(k)
            pltpu.make_async_remote_copy(src_ref=x_ref, dst_ref=buf.at[idx], send_sem=send_sems.at[k - 1],
                                         recv_sem=recv_sems.at[k - 1], device_id=dev, device_id_type=MESH).wait_recv()
        for cp in sends:
            cp.wait_send()
        acc = buf[0]
        for i in range(1, N_DEV):
            acc = acc + buf[i]
        o_ref[...] = acc

    return pl.pallas_call(
        body, name="small_all_reduce", out_shape=jax.ShapeDtypeStruct((R, C), f32),
        in_specs=[pl.BlockSpec(memory_space=pltpu.VMEM)], out_specs=pl.BlockSpec(memory_space=pltpu.VMEM),
        scratch_shapes=[pltpu.VMEM((N_DEV, R, C), f32), pltpu.SemaphoreType.DMA((N_DEV - 1,)),
                        pltpu.SemaphoreType.DMA((N_DEV - 1,))],
    )(small)


_BIG = (
    ("ffn1_w_in", True), ("ffn1_w_out", False), ("w_in", True), ("rwkv_w2", True), ("rwkv_a2", True),
    ("rwkv_g2", True), ("w_proj_rwkv", False), ("w_proj_attn", True), ("w_out", False),
    ("ffn2_w_in", True), ("ffn2_w_out", False))
_SMALL = ("ffn1_norm", "mix_norm", "b_gate", "rwkv_mu", "rwkv_w0", "rwkv_a0", "rwkv_k_k", "rwkv_k_a", "rwkv_r_k",
          "rwkv_ln_w", "rwkv_ln_b", "attn_q_norm", "attn_k_norm", "ffn2_norm")


def _pack_big(shards):
    parts = []
    for name, tr in _BIG:
        t = shards[name]
        parts.append((t.T if tr else t).reshape(-1, D))
    used = sum(p.shape[0] for p in parts)
    parts.append(jnp.zeros((PACK_ROWS - used, D), parts[0].dtype))
    return jnp.concatenate(parts, axis=0)


def _unpack_big(pack, like):
    out, off = {}, 0
    for name, tr in _BIG:
        shp = like[name].shape
        n = shp[0] * shp[1] // D
        t = pack[off:off + n]
        out[name] = t.reshape(shp[1], shp[0]).T if tr else t.reshape(shp)
        off += n
    return out


def _small_rows(name, t):
    flat = t.reshape(-1)
    pad = (-flat.shape[0]) % D
    return jnp.pad(flat, (0, pad)).reshape(-1, D)


def _pack_small(vals):
    parts = [_small_rows(n, vals[n]) for n in _SMALL]
    used = sum(p.shape[0] for p in parts)
    parts.append(jnp.zeros((SMALL_ROWS - used, D), f32))
    return jnp.concatenate(parts, axis=0)


def _unpack_small(pack, like):
    out, off = {}, 0
    for n in _SMALL:
        size = like[n].size
        rows = -(-size // D)
        out[n] = pack[off:off + rows].reshape(-1)[:size].reshape(like[n].shape)
        off += rows
    return out


def _pad_rows(t, n):
    return jnp.concatenate([t, jnp.zeros((n - t.shape[0],) + t.shape[1:], t.dtype)], axis=0)


def _build_W(full):
    inT = full["w_in"]
    z64, z96 = jnp.zeros((64, D), inT.dtype), jnp.zeros((96, D), inT.dtype)
    return {
        "f1_iT": full["ffn1_w_in"], "f1_o": full["ffn1_w_out"], "f2_iT": full["ffn2_w_in"], "f2_o": full["ffn2_w_out"],
        "in_rkvT": inT[:3 * D],
        "in_loraT": jnp.concatenate([inT[3072:3136], z64, inT[3136:3200], z64, inT[3200:3360], z96], axis=0),
        "in_qkvT": inT[3360:3360 + 3 * ATTN_W], "in_gateT": inT[3360 + 3 * ATTN_W:],
        "w2p": jnp.concatenate([full["rwkv_w2"].T, z64], axis=0),
        "a2p": jnp.concatenate([full["rwkv_a2"].T, z64], axis=0),
        "g2p": jnp.concatenate([full["rwkv_g2"].T, z96], axis=0),
        "pr": full["w_proj_rwkv"], "paT": full["w_proj_attn"], "out": full["w_out"],
    }


def _build_P(Wl):
    mu = Wl["rwkv_mu"]
    z64f, z96f = jnp.zeros((1, 64), f32), jnp.zeros((1, 96), f32)
    return {
        "ffn1_norm": Wl["ffn1_norm"][None], "mix_norm": Wl["mix_norm"][None], "ffn2_norm": Wl["ffn2_norm"][None],
        "b_gate": Wl["b_gate"][None], "mu_rkv": mu[None, :3 * D],
        "mu_lora": jnp.concatenate([mu[None, 3072:3136], z64f, mu[None, 3136:3200], z64f, mu[None, 3200:3360], z96f], axis=1),
        "w0": Wl["rwkv_w0"][None], "a0": Wl["rwkv_a0"][None], "k_k": Wl["rwkv_k_k"][None], "k_a": Wl["rwkv_k_a"][None],
        "r_k": Wl["rwkv_r_k"].reshape(1, D), "ln_w": Wl["rwkv_ln_w"][None], "ln_b": Wl["rwkv_ln_b"][None],
        "q_norm": Wl["attn_q_norm"][None], "k_norm": Wl["attn_k_norm"][None],
    }


def kernel(x, ffn1_norm, ffn1_w_in, ffn1_w_out, mix_norm, w_in, b_gate, rwkv_mu, rwkv_w0, rwkv_w2, rwkv_a0, rwkv_a2, rwkv_g2, rwkv_k_k, rwkv_k_a, rwkv_r_k, rwkv_ln_w, rwkv_ln_b, attn_q_norm, attn_k_norm, w_proj_rwkv, w_proj_attn, w_out, ffn2_norm, ffn2_w_in, ffn2_w_out, loss_target, m_ffn1_norm, m_ffn1_w_in, m_ffn1_w_out, m_mix_norm, m_w_in, m_b_gate, m_rwkv_mu, m_rwkv_w0, m_rwkv_w2, m_rwkv_a0, m_rwkv_a2, m_rwkv_g2, m_rwkv_k_k, m_rwkv_k_a, m_rwkv_r_k, m_rwkv_ln_w, m_rwkv_ln_b, m_attn_q_norm, m_attn_k_norm, m_w_proj_rwkv, m_w_proj_attn, m_w_out, m_ffn2_norm, m_ffn2_w_in, m_ffn2_w_out, v_ffn1_norm, v_ffn1_w_in, v_ffn1_w_out, v_mix_norm, v_w_in, v_b_gate, v_rwkv_mu, v_rwkv_w0, v_rwkv_w2, v_rwkv_a0, v_rwkv_a2, v_rwkv_g2, v_rwkv_k_k, v_rwkv_k_a, v_rwkv_r_k, v_rwkv_ln_w, v_rwkv_ln_b, v_attn_q_norm, v_attn_k_norm, v_w_proj_rwkv, v_w_proj_attn, v_w_out, v_ffn2_norm, v_ffn2_w_in, v_ffn2_w_out):
    names = ("ffn1_norm", "ffn1_w_in", "ffn1_w_out", "mix_norm", "w_in", "b_gate", "rwkv_mu", "rwkv_w0", "rwkv_w2",
             "rwkv_a0", "rwkv_a2", "rwkv_g2", "rwkv_k_k", "rwkv_k_a", "rwkv_r_k", "rwkv_ln_w", "rwkv_ln_b",
             "attn_q_norm", "attn_k_norm", "w_proj_rwkv", "w_proj_attn", "w_out", "ffn2_norm", "ffn2_w_in", "ffn2_w_out")
    w_all = (ffn1_norm, ffn1_w_in, ffn1_w_out, mix_norm, w_in, b_gate, rwkv_mu, rwkv_w0, rwkv_w2, rwkv_a0, rwkv_a2,
             rwkv_g2, rwkv_k_k, rwkv_k_a, rwkv_r_k, rwkv_ln_w, rwkv_ln_b, attn_q_norm, attn_k_norm, w_proj_rwkv,
             w_proj_attn, w_out, ffn2_norm, ffn2_w_in, ffn2_w_out)
    m_all = (m_ffn1_norm, m_ffn1_w_in, m_ffn1_w_out, m_mix_norm, m_w_in, m_b_gate, m_rwkv_mu, m_rwkv_w0, m_rwkv_w2,
             m_rwkv_a0, m_rwkv_a2, m_rwkv_g2, m_rwkv_k_k, m_rwkv_k_a, m_rwkv_r_k, m_rwkv_ln_w, m_rwkv_ln_b,
             m_attn_q_norm, m_attn_k_norm, m_w_proj_rwkv, m_w_proj_attn, m_w_out, m_ffn2_norm, m_ffn2_w_in, m_ffn2_w_out)
    v_all = (v_ffn1_norm, v_ffn1_w_in, v_ffn1_w_out, v_mix_norm, v_w_in, v_b_gate, v_rwkv_mu, v_rwkv_w0, v_rwkv_w2,
             v_rwkv_a0, v_rwkv_a2, v_rwkv_g2, v_rwkv_k_k, v_rwkv_k_a, v_rwkv_r_k, v_rwkv_ln_w, v_rwkv_ln_b,
             v_attn_q_norm, v_attn_k_norm, v_w_proj_rwkv, v_w_proj_attn, v_w_out, v_ffn2_norm, v_ffn2_w_in, v_ffn2_w_out)
    Wl = {n: t[0] for n, t in zip(names, w_all)}
    Ml = {n: t[0] for n, t in zip(names, m_all)}
    Vl = {n: t[0] for n, t in zip(names, v_all)}
    big = [n for n, _ in _BIG]

    w_pack = _pack_big({n: Wl[n] for n in big})
    gathered = _all_gather(w_pack.astype(bf16))
    full, off = {}, 0
    for n, tr in _BIG:
        shp = Wl[n].shape
        rows = shp[0] * shp[1] // D
        t = gathered[:, off:off + rows]
        r_loc, c_loc = (shp[1], shp[0]) if tr else shp
        full[n] = t.reshape(N_DEV * r_loc, c_loc)
        off += rows
    W, P = _build_W(full), _build_P(Wl)

    loss_local, dx0, G, Gs = _local_step(x[0], loss_target[0], W, P)

    g_full = {"ffn1_w_in": G["f1_iT"], "ffn1_w_out": G["f1_o"], "w_in": G["inT"], "rwkv_w2": G["w2T"],
              "rwkv_a2": G["a2T"], "rwkv_g2": G["g2T"], "w_proj_rwkv": G["pr"], "w_proj_attn": G["paT"],
              "w_out": G["out"], "ffn2_w_in": G["f2_iT"], "ffn2_w_out": G["f2_o"]}
    parts = [g_full[n].astype(GRAD_WIRE).reshape(N_DEV, -1, D) for n in big]
    used = sum(p.shape[1] for p in parts)
    parts.append(jnp.zeros((N_DEV, PACK_ROWS - used, D), GRAD_WIRE))
    g_wire = jnp.concatenate(parts, axis=1)
    my_c = lax.axis_index("c").astype(jnp.int32).reshape(1)
    chip_part = _pair_add(g_wire, _grad_pair(g_wire), my_c, 512)
    g_pack = _sum_slots("grad_sum", _grad_cross(chip_part), 256)

    mu_g = Gs["mu_rkv"], Gs["mu_lora"]
    o1, o2 = LORA_PAD[0], LORA_PAD[0] + LORA_PAD[1]
    g_small_local = {
        "ffn1_norm": Gs["ffn1_norm"], "mix_norm": Gs["mix_norm"], "b_gate": Gs["b_gate"],
        "rwkv_mu": jnp.concatenate([mu_g[0], mu_g[1][:, :64], mu_g[1][:, o1:o1 + 64], mu_g[1][:, o2:o2 + 160]], axis=1),
        "rwkv_w0": Gs["w0"], "rwkv_a0": Gs["a0"], "rwkv_k_k": Gs["k_k"], "rwkv_k_a": Gs["k_a"], "rwkv_r_k": Gs["r_k"],
        "rwkv_ln_w": Gs["ln_w"], "rwkv_ln_b": Gs["ln_b"], "attn_q_norm": Gs["q_norm"], "attn_k_norm": Gs["k_norm"],
        "ffn2_norm": Gs["ffn2_norm"]}
    gs_pack = _small_all_reduce(_pack_small(g_small_local))

    d_pack, m_pack, v_pack = _rowwise(
        "adamw_big", _f_adamw, [w_pack, g_pack, _pack_big({n: Ml[n] for n in big}), _pack_big({n: Vl[n] for n in big})],
        [], [(D, f32)] * 3)
    ds_pack, ms_pack, vs_pack = _rowwise(
        "adamw_small", _f_adamw, [_pack_small(Wl), gs_pack, _pack_small(Ml), _pack_small(Vl)], [], [(D, f32)] * 3)

    def unpack(pb, ps):
        d = _unpack_big(pb, Wl)
        d.update(_unpack_small(ps, Wl))
        return [d[n][None] for n in names]

    loss = lax.psum(loss_local, ("x", "y", "c"))
    return (loss, dx0[None], *unpack(g_pack, gs_pack), *unpack(d_pack, ds_pack), *unpack(m_pack, ms_pack),
            *unpack(v_pack, vs_pack))
```

```python
import functools

import jax
import jax.numpy as jnp
from jax import lax
from jax.experimental import pallas as pl
from jax.experimental.pallas import tpu as pltpu

f32 = jnp.float32
bf16 = jnp.bfloat16
HI = lax.Precision.HIGHEST
MESH = pl.DeviceIdType.MESH

N_DEV = 8
D = 1024
D_FF = 2816
HEAD = 64
RW_HEADS = 16
ATTN_PAIRS = ((128, 1), (512, 4), (2048, 16))
ATTN_BLK = 128
HEADS_PER_GROUP = 4
ATTN_W = 768
LORA_PAD = (128, 128, 256)
LORA_W = (64, 64, 160)
GN_EPS = 64e-5
RMS_EPS = 1e-6
NEG_INF = -1e30
WKV_T = 64
GRAD_WIRE = bf16
PACK_ROWS = 3584
SMALL_ROWS = 24
VMEM_LIMIT = 56 * 1024 * 1024

ADAM_LR, ADAM_B1, ADAM_B2, ADAM_EPS, ADAM_WD, ADAM_STEP = 0.001, 0.9, 0.999, 1e-08, 0.01, 10


def _cparams(sem):
    return pltpu.CompilerParams(dimension_semantics=sem, vmem_limit_bytes=VMEM_LIMIT)


def _pick(n, cands):
    for c in cands:
        if n % c == 0:
            return c
    return n


def _rowwise(name, fn, rows, params, outs, accs=(), tm=256):
    S = rows[0].shape[0]
    tm = min(tm, S)
    n_in = len(rows) + len(params)
    n_out = len(outs)
    n_acc = len(accs)

    def body(*refs):
        res = fn(*[r[...] for r in refs[:n_in]])
        if not isinstance(res, (tuple, list)):
            res = (res,)
        out_refs = refs[n_in:]
        for j in range(n_out):
            out_refs[j][...] = res[j].astype(out_refs[j].dtype)
        if n_acc:
            @pl.when(pl.program_id(0) == 0)
            def _():
                for j in range(n_acc):
                    out_refs[n_out + j][...] = jnp.zeros(out_refs[n_out + j].shape, f32)
            for j in range(n_acc):
                out_refs[n_out + j][...] += res[n_out + j]

    in_specs = [pl.BlockSpec((tm, a.shape[1]), lambda i: (i, 0)) for a in rows]
    in_specs += [pl.BlockSpec(p.shape, lambda i, nd=p.ndim: (0,) * nd) for p in params]
    out_specs = [pl.BlockSpec((tm, w), lambda i: (i, 0)) for w, _ in outs]
    out_specs += [pl.BlockSpec(s, lambda i: (0, 0)) for s in accs]
    out_shape = [jax.ShapeDtypeStruct((S, w), dt) for w, dt in outs]
    out_shape += [jax.ShapeDtypeStruct(s, f32) for s in accs]
    res = pl.pallas_call(
        body, name=name, grid=(S // tm,), in_specs=in_specs, out_specs=out_specs, out_shape=out_shape,
        compiler_params=_cparams(("arbitrary",)),
    )(*rows, *params)
    return res


MM_VMEM_BUDGET = 40 * 1024 * 1024
MM_STEP_US = 0.35
MM_FLOPS_PER_US = 9.0e8
MM_HBM_BYTES_PER_US = 3.0e6


def _tile_options(n, cap):
    opts = [d for d in range(128, min(n, cap) + 1, 128) if n % d == 0]
    return opts or [n]


def _mm_tiles(M, N, K, sa, sb, so):
    best, best_cost = None, None
    for tm in _tile_options(M, 2048):
        for tn in _tile_options(N, 2048):
            for tk in _tile_options(K, 4096):
                vmem = 2 * (tm * tk * sa + tk * tn * sb) + 2 * tm * tn * so + (tm * tn * 4 if tk < K else 0)
                if vmem > MM_VMEM_BUDGET:
                    continue
                steps = (M // tm) * (N // tn) * (K // tk)
                traffic = M * K * sa * (N // tn) + K * N * sb * (M // tm) + M * N * so
                cost = (max(2.0 * M * N * K / MM_FLOPS_PER_US, traffic / MM_HBM_BYTES_PER_US) + steps * MM_STEP_US
                        + (tm * tk * sa + tk * tn * sb) / MM_HBM_BYTES_PER_US)
                if best_cost is None or cost < best_cost:
                    best, best_cost = (tm, tn, tk), cost
    return best


def _mm(name, a, b, mode, out_dtype=f32, scale=None):
    if mode == "nn":
        (M, K), (_, N) = a.shape, b.shape
    elif mode == "nt":
        (M, K), (N, _) = a.shape, b.shape
    else:
        (K, M), (_, N) = a.shape, b.shape
    tm, tn, tk = _mm_tiles(M, N, K, a.dtype.itemsize, b.dtype.itemsize, jnp.dtype(out_dtype).itemsize)
    nk = K // tk
    if mode == "nn":
        a_spec = pl.BlockSpec((tm, tk), lambda i, j, k: (i, k))
        b_spec = pl.BlockSpec((tk, tn), lambda i, j, k: (k, j))
        dims = (((1,), (0,)), ((), ()))
    elif mode == "nt":
        a_spec = pl.BlockSpec((tm, tk), lambda i, j, k: (i, k))
        b_spec = pl.BlockSpec((tn, tk), lambda i, j, k: (j, k))
        dims = (((1,), (1,)), ((), ()))
    else:
        a_spec = pl.BlockSpec((tk, tm), lambda i, j, k: (k, i))
        b_spec = pl.BlockSpec((tk, tn), lambda i, j, k: (k, j))
        dims = (((0,), (0,)), ((), ()))

    def finish(acc):
        return acc if scale is None else acc * scale

    def body(a_ref, b_ref, o_ref, *scratch):
        part = lax.dot_general(a_ref[...].astype(bf16), b_ref[...].astype(bf16), dims,
                               preferred_element_type=f32)
        if nk == 1:
            o_ref[...] = finish(part).astype(o_ref.dtype)
        else:
            acc_ref = scratch[0]
            k = pl.program_id(2)

            @pl.when(k == 0)
            def _():
                acc_ref[...] = part

            @pl.when(k > 0)
            def _():
                acc_ref[...] += part

            @pl.when(k == nk - 1)
            def _():
                o_ref[...] = finish(acc_ref[...]).astype(o_ref.dtype)

    return pl.pallas_call(
        body, name=name, grid=(M // tm, N // tn, nk), in_specs=[a_spec, b_spec],
        out_specs=pl.BlockSpec((tm, tn), lambda i, j, k: (i, j)),
        out_shape=jax.ShapeDtypeStruct((M, N), out_dtype),
        scratch_shapes=[] if nk == 1 else [pltpu.VMEM((tm, tn), f32)],
        compiler_params=_cparams(("parallel", "parallel", "arbitrary")),
    )(a, b)


def _sp(x):
    hi = x.astype(bf16)
    return hi, (x - hi.astype(f32)).astype(bf16)


def _cat(parts):
    return tuple(jnp.concatenate(p, axis=1) for p in zip(*parts))


def _bmm(eq, a, b):
    (ah, al), (bh, bl) = a, b
    dot = functools.partial(jnp.einsum, eq, preferred_element_type=f32)
    return dot(ah, bh) + (dot(ah, bl) + dot(al, bh))


def _tri_dot(eq, tri, x):
    h1 = x.astype(bf16)
    r1 = x - h1.astype(f32)
    h2 = r1.astype(bf16)
    h3 = (r1 - h2.astype(f32)).astype(bf16)
    dot = functools.partial(jnp.einsum, eq, preferred_element_type=f32)
    return dot(tri, h1) + (dot(tri, h2) + dot(tri, h3))


def _tri_masks(T):
    ti = lax.broadcasted_iota(jnp.int32, (T, T), 0)
    si = lax.broadcasted_iota(jnp.int32, (T, T), 1)
    return ti >= si, ti > si


def _wkv_prep(r, lw, k, kkr, a):
    H, T, _ = r.shape
    low_i, low_s = _tri_masks(T)
    nrm = jnp.sqrt(jnp.sum(kkr * kkr, axis=-1, keepdims=True))
    den = jnp.maximum(nrm, 1e-12)
    kk = kkr / den
    tri = jnp.broadcast_to(low_i.astype(bf16)[None], (H, T, T))
    cl = _tri_dot("hts,hsn->htn", tri, lw)
    c = jnp.exp(cl)
    cprev = jnp.exp(cl - lw)
    cinv = jnp.exp(-cl)
    bt, kt = _sp(kk * a * cinv), _sp(k * cinv)
    L = _cat([_sp(r * c), _sp(-kk * cprev)])
    Mb = _bmm("htn,hsn->hts", L, bt)
    Mk = _bmm("htn,hsn->hts", L, kt)
    A_rb = jnp.where(low_i[None], Mb[:, :T], 0.0)
    A_ab = jnp.where(low_s[None], Mb[:, T:], 0.0)
    Mk = jnp.concatenate([jnp.where(low_i[None], Mk[:, :T], 0.0), jnp.where(low_s[None], Mk[:, T:], 0.0)], axis=1)
    return dict(kk=kk, den=den, nrm=nrm, c=c, cprev=cprev, cinv=cinv, L=L, kt=kt, bt=bt,
                A_ab=A_ab, A_rb=A_rb, Mk=Mk, cT=c[:, T - 1:T, :])


def _tri_inverse(A):
    T = A.shape[-1]
    eye = (lax.broadcasted_iota(jnp.int32, (T, T), 0) == lax.broadcasted_iota(jnp.int32, (T, T), 1)).astype(f32)
    inv = eye[None] + A
    X = A
    n = 1
    while 2 * n < T:
        Xs = _sp(X)
        X = _bmm("hts,hsu->htu", Xs, Xs)
        inv = inv + _bmm("hts,hsu->htu", _sp(inv), _sp(X))
        n *= 2
    return inv


def _wkv_chunk_fwd(S0, r, lw, k, v, kkr, a):
    T = r.shape[1]
    q = _wkv_prep(r, lw, k, kkr, a)
    inv = _tri_inverse(q["A_ab"])
    vs = _sp(v)
    P = _bmm("htk,hvk->htv", q["L"], _sp(S0)) + _bmm("hts,hsv->htv", _sp(q["Mk"]), vs)
    U = _bmm("hts,hsv->htv", _sp(inv), _sp(P[:, T:]))
    Us = _sp(U)
    Y = P[:, :T] + _bmm("hts,hsv->htv", _sp(q["A_rb"]), Us)
    S1 = (S0 + _bmm("htv,htk->hvk", _cat([Us, vs]), _cat([q["bt"], q["kt"]]))) * q["cT"]
    return Y, U, inv, S1


def _wkv_chunk_bwd(S0, Hin, Q, r, lw, k, v, kkr, a, U, inv, dY):
    H, T, _ = r.shape
    low_i, low_s = _tri_masks(T)
    q = _wkv_prep(r, lw, k, kkr, a)
    L, kt, bt = q["L"], q["kt"], q["bt"]
    R = _cat([bt, kt])
    Hh = Hin * q["cT"]
    Hs, S0s, dYs, vs, Us = _sp(Hh), _sp(S0), _sp(dY), _sp(v), _sp(U)
    RH = _bmm("htk,hvk->htv", R, Hs)
    Z = _bmm("hst,hsv->htv", _sp(inv), _sp(RH[:, :T] + _bmm("hst,hsv->htv", _sp(q["A_rb"]), dYs)))
    DZ = _cat([dYs, _sp(Z)])
    both = jnp.concatenate([jnp.broadcast_to(low_i[None], (1, T, T)), jnp.broadcast_to(low_s[None], (1, T, T))], axis=1)
    NU = _sp(jnp.where(both, _bmm("htv,hsv->hts", DZ, Us), 0.0))
    NV = _sp(jnp.where(both, _bmm("htv,hsv->hts", DZ, vs), 0.0))
    ra = _bmm("htv,hvk->htk", DZ, S0s) + _bmm("hts,hsk->htk", NU, bt) + _bmm("hts,hsk->htk", NV, kt)
    dr = ra[:, :T] * q["c"]
    da = ra[:, T:] * q["cprev"]
    dv = RH[:, T:] + _bmm("hst,hsv->htv", _sp(q["Mk"]), DZ)
    VH = _bmm("htv,hvk->htk", _cat([vs, Us]), Hs)
    dk = (VH[:, :T] + _bmm("hst,hsk->htk", NV, L)) * q["cinv"]
    db = (VH[:, T:] + _bmm("hst,hsk->htk", NU, L)) * q["cinv"]
    H0 = Hh + _bmm("htv,htk->hvk", DZ, L)
    kk = q["kk"]
    e = r * dr - kk * a * db - k * dk
    f = -kk * da
    tri_i = jnp.broadcast_to(low_i.astype(bf16)[None], (H, T, T))
    tri_s = jnp.broadcast_to(low_s.astype(bf16)[None], (H, T, T))
    dlw = _tri_dot("hst,hsn->htn", tri_i, e) + _tri_dot("hst,hsn->htn", tri_s, f) + Q
    Qn = Q + jnp.sum(e + f, axis=1, keepdims=True)
    dkk = db * a - da
    dasig = db * kk
    proj = jnp.sum(dkk * kk, axis=-1, keepdims=True)
    dkkr = jnp.where(q["nrm"] > 1e-12, dkk - kk * proj, dkk) / q["den"]
    return dr, dlw, dk, dv, dkkr, dasig, H0, Qn


def _heads(ref):
    return jnp.stack([ref[:, h * HEAD:(h + 1) * HEAD] for h in range(RW_HEADS)], axis=0)


def _put_heads(ref, val):
    for h in range(RW_HEADS):
        ref[:, h * HEAD:(h + 1) * HEAD] = val[h]


def _wkv_fwd(r, lw, k, v, kkr, a, r_k, ln_w, ln_b):
    S = r.shape[0]
    H, N, T = RW_HEADS, HEAD, WKV_T
    nc = S // T

    def body(r_ref, lw_ref, k_ref, v_ref, kkr_ref, a_ref, rk_ref, lnw_ref, lnb_ref,
             y_ref, wkv_ref, u_ref, inv_ref, s0_ref, state):
        @pl.when(pl.program_id(0) == 0)
        def _():
            state[...] = jnp.zeros(state.shape, f32)

        S0 = state[...]
        s0_ref[0] = S0
        rr, kk2, vv = _heads(r_ref), _heads(k_ref), _heads(v_ref)
        Y, U, inv, S1 = _wkv_chunk_fwd(S0, rr, _heads(lw_ref), kk2, vv, _heads(kkr_ref), _heads(a_ref))
        state[...] = S1
        wkv_ref[...] = Y
        u_ref[...] = U
        inv_ref[...] = inv
        mean = jnp.mean(Y, axis=-1, keepdims=True)
        var = jnp.mean(jnp.square(Y - mean), axis=-1, keepdims=True)
        yn = (Y - mean) * lax.rsqrt(var + GN_EPS)
        bonus = jnp.sum(rr * kk2 * rk_ref[...], axis=-1, keepdims=True) * vv
        _put_heads(y_ref, yn * lnw_ref[...] + lnb_ref[...] + bonus)

    tok = pl.BlockSpec((T, H * N), lambda i: (i, 0))
    blk = pl.BlockSpec((H, T, N), lambda i: (0, i, 0))
    par = pl.BlockSpec((H, 1, N), lambda i: (0, 0, 0))
    seq = jax.ShapeDtypeStruct((H, S, N), f32)
    return pl.pallas_call(
        body, name="wkv_fwd", grid=(nc,), in_specs=[tok] * 6 + [par] * 3,
        out_specs=[tok, blk, blk, blk, pl.BlockSpec((1, H, N, N), lambda i: (i, 0, 0, 0))],
        out_shape=[jax.ShapeDtypeStruct((S, H * N), f32), seq, seq, seq, jax.ShapeDtypeStruct((nc, H, N, N), f32)],
        scratch_shapes=[pltpu.VMEM((H, N, N), f32)],
        compiler_params=_cparams(("arbitrary",)),
    )(r, lw, k, v, kkr, a, r_k, ln_w, ln_b)


def _wkv_bwd(dy, r, lw, k, v, kkr, a, wkv, U, inv, S0s, r_k, ln_w, ln_b):
    S = r.shape[0]
    H, N, T = RW_HEADS, HEAD, WKV_T
    nc = S // T

    def body(dy_ref, r_ref, lw_ref, k_ref, v_ref, kkr_ref, a_ref, wkv_ref, u_ref, inv_ref, s0_ref,
             rk_ref, lnw_ref, lnb_ref,
             dr_ref, dlw_ref, dk_ref, dv_ref, dkkr_ref, da_ref, drk_ref, dlnw_ref, dlnb_ref, hst, qst):
        @pl.when(pl.program_id(0) == 0)
        def _():
            hst[...] = jnp.zeros(hst.shape, f32)
            qst[...] = jnp.zeros(qst.shape, f32)
            drk_ref[...] = jnp.zeros(drk_ref.shape, f32)
            dlnw_ref[...] = jnp.zeros(dlnw_ref.shape, f32)
            dlnb_ref[...] = jnp.zeros(dlnb_ref.shape, f32)

        dya = _heads(dy_ref)
        rr, kk2, vv, Y = _heads(r_ref), _heads(k_ref), _heads(v_ref), wkv_ref[...]
        rk = rk_ref[...]
        s = jnp.sum(rr * kk2 * rk, axis=-1, keepdims=True)
        ds = jnp.sum(dya * vv, axis=-1, keepdims=True)
        mean = jnp.mean(Y, axis=-1, keepdims=True)
        var = jnp.mean(jnp.square(Y - mean), axis=-1, keepdims=True)
        rstd = lax.rsqrt(var + GN_EPS)
        yn = (Y - mean) * rstd
        dyn = dya * lnw_ref[...]
        dY = rstd * (dyn - jnp.mean(dyn, axis=-1, keepdims=True) - yn * jnp.mean(dyn * yn, axis=-1, keepdims=True))
        drk_ref[...] += jnp.sum(ds * rr * kk2, axis=1, keepdims=True)
        dlnw_ref[...] += jnp.sum(dya * yn, axis=1, keepdims=True)
        dlnb_ref[...] += jnp.sum(dya, axis=1, keepdims=True)
        dr, dlw, dk, dv, dkkr, dasig, H0, Qn = _wkv_chunk_bwd(
            s0_ref[0], hst[...], qst[...], rr, _heads(lw_ref), kk2, vv, _heads(kkr_ref), _heads(a_ref), u_ref[...],
            inv_ref[...], dY)
        hst[...] = H0
        qst[...] = Qn
        _put_heads(dr_ref, dr + ds * kk2 * rk)
        _put_heads(dlw_ref, dlw)
        _put_heads(dk_ref, dk + ds * rr * rk)
        _put_heads(dv_ref, dv + dya * s)
        _put_heads(dkkr_ref, dkkr)
        _put_heads(da_ref, dasig)

    tok = pl.BlockSpec((T, H * N), lambda i: (nc - 1 - i, 0))
    blk = pl.BlockSpec((H, T, N), lambda i: (0, nc - 1 - i, 0))
    par = pl.BlockSpec((H, 1, N), lambda i: (0, 0, 0))
    seq = jax.ShapeDtypeStruct((S, H * N), f32)
    pout = jax.ShapeDtypeStruct((H, 1, N), f32)
    return pl.pallas_call(
        body, name="wkv_bwd", grid=(nc,),
        in_specs=[tok] * 7 + [blk] * 3 + [pl.BlockSpec((1, H, N, N), lambda i: (nc - 1 - i, 0, 0, 0))] + [par] * 3,
        out_specs=[tok] * 6 + [par] * 3,
        out_shape=[seq] * 6 + [pout] * 3,
        scratch_shapes=[pltpu.VMEM((H, N, N), f32), pltpu.VMEM((H, 1, N), f32)],
        compiler_params=_cparams(("arbitrary",)),
    )(dy, r, lw, k, v, kkr, a, wkv, U, inv, S0s, r_k, ln_w, ln_b)


ATTN_G = 8


def _attn_first_mask(S, G):
    nbg = HEADS_PER_GROUP * S // ATTN_BLK
    b = pl.program_id(0) * G + lax.broadcasted_iota(jnp.int32, (G, 1, 1), 0)
    nbs = [S // (d * ATTN_BLK) for _, d in ATTN_PAIRS]
    per = jnp.where(b < nbg, nbs[0], jnp.where(b < 2 * nbg, nbs[1], nbs[2]))
    return jnp.bitwise_and(b, per - 1) == 0


def _attn_norm(x, gain, scale):
    rs = lax.rsqrt(jnp.mean(x * x, axis=-1, keepdims=True) + RMS_EPS)
    return x * rs * (gain * scale), rs


def _attn_scores(qn, kn_c, kn_p, first):
    s_c = jnp.einsum("gqe,gke->gqk", qn.astype(bf16), kn_c.astype(bf16), preferred_element_type=f32)
    s_p = jnp.einsum("gqe,gke->gqk", qn.astype(bf16), kn_p.astype(bf16), preferred_element_type=f32)
    qi = lax.broadcasted_iota(jnp.int32, (1, ATTN_BLK, ATTN_BLK), 1)
    ki = lax.broadcasted_iota(jnp.int32, (1, ATTN_BLK, ATTN_BLK), 2)
    s_c = jnp.where(qi >= ki, s_c, NEG_INF)
    s_p = jnp.where(jnp.logical_and(ki >= qi, jnp.logical_not(first)), s_p, NEG_INF)
    return s_c, s_p


def _attn_fwd(q, k, kp, v, vp, qg, kg, S):
    NB = q.shape[0]
    G = ATTN_G

    def body(q_ref, k_ref, kp_ref, v_ref, vp_ref, qg_ref, kg_ref, o_ref, lse_ref):
        first = _attn_first_mask(S, G)
        qn, _ = _attn_norm(q_ref[...], qg_ref[...], HEAD ** -0.5)
        kn_c, _ = _attn_norm(k_ref[...], kg_ref[...], 1.0)
        kn_p, _ = _attn_norm(kp_ref[...], kg_ref[...], 1.0)
        s_c, s_p = _attn_scores(qn, kn_c, kn_p, first)
        m = jnp.maximum(jnp.max(s_c, axis=-1, keepdims=True), jnp.max(s_p, axis=-1, keepdims=True))
        p_c = jnp.exp(s_c - m)
        p_p = jnp.exp(s_p - m)
        den = jnp.sum(p_c, axis=-1, keepdims=True) + jnp.sum(p_p, axis=-1, keepdims=True)
        inv = 1.0 / den
        o = jnp.einsum("gqk,gke->gqe", (p_c * inv).astype(bf16), v_ref[...].astype(bf16), preferred_element_type=f32)
        o += jnp.einsum("gqk,gke->gqe", (p_p * inv).astype(bf16), vp_ref[...].astype(bf16), preferred_element_type=f32)
        o_ref[...] = o
        lse_ref[...] = jnp.broadcast_to(m + jnp.log(den), o.shape)

    blk = pl.BlockSpec((G, ATTN_BLK, HEAD), lambda i: (i, 0, 0))
    par = pl.BlockSpec((1, 1, HEAD), lambda i: (0, 0, 0))
    shp = jax.ShapeDtypeStruct((NB, ATTN_BLK, HEAD), f32)
    return pl.pallas_call(
        body, name="attn_fwd", grid=(NB // G,), in_specs=[blk] * 5 + [par] * 2, out_specs=[blk, blk],
        out_shape=[shp, shp], compiler_params=_cparams(("arbitrary",)),
    )(q, k, kp, v, vp, qg, kg)


def _attn_bwd(q, k, kp, v, vp, o, lse, do, dlse, qg, kg, S):
    NB = q.shape[0]
    G = ATTN_G

    def norm_bwd(dxn, x, rs, gain, scale):
        xh = x * rs
        dxh = dxn * (gain * scale)
        dx = rs * (dxh - xh * jnp.mean(dxh * xh, axis=-1, keepdims=True))
        dgain = jnp.sum(jnp.sum(dxn * xh * scale, axis=1, keepdims=True), axis=0, keepdims=True)
        return dx, dgain

    def body(q_ref, k_ref, kp_ref, v_ref, vp_ref, o_ref, lse_ref, do_ref, dlse_ref, qg_ref, kg_ref,
             dq_ref, dkc_ref, dkp_ref, dvc_ref, dvp_ref, dqg_ref, dkg_ref):
        @pl.when(pl.program_id(0) == 0)
        def _():
            dqg_ref[...] = jnp.zeros(dqg_ref.shape, f32)
            dkg_ref[...] = jnp.zeros(dkg_ref.shape, f32)

        first = _attn_first_mask(S, G)
        qx, kx, kpx = q_ref[...], k_ref[...], kp_ref[...]
        qg, kg = qg_ref[...], kg_ref[...]
        qn, rq = _attn_norm(qx, qg, HEAD ** -0.5)
        kn_c, rk_c = _attn_norm(kx, kg, 1.0)
        kn_p, rk_p = _attn_norm(kpx, kg, 1.0)
        s_c, s_p = _attn_scores(qn, kn_c, kn_p, first)
        lse = lse_ref[...][:, :, 0:1]
        p_c = jnp.exp(s_c - lse)
        p_p = jnp.exp(s_p - lse)
        dO = do_ref[...]
        dOb = dO.astype(bf16)
        vb, vpb = v_ref[...].astype(bf16), vp_ref[...].astype(bf16)
        dp_c = jnp.einsum("gqe,gke->gqk", dOb, vb, preferred_element_type=f32)
        dp_p = jnp.einsum("gqe,gke->gqk", dOb, vpb, preferred_element_type=f32)
        corr = dlse_ref[...][:, :, 0:1] - jnp.sum(dO * o_ref[...], axis=-1, keepdims=True)
        ds_c = (p_c * (dp_c + corr)).astype(bf16)
        ds_p = (p_p * (dp_p + corr)).astype(bf16)
        qnb, kcb, kpb = qn.astype(bf16), kn_c.astype(bf16), kn_p.astype(bf16)
        dqn = (jnp.einsum("gqk,gke->gqe", ds_c, kcb, preferred_element_type=f32)
               + jnp.einsum("gqk,gke->gqe", ds_p, kpb, preferred_element_type=f32))
        dkn_c = jnp.einsum("gqk,gqe->gke", ds_c, qnb, preferred_element_type=f32)
        dkn_p = jnp.einsum("gqk,gqe->gke", ds_p, qnb, preferred_element_type=f32)
        dvc_ref[...] = jnp.einsum("gqk,gqe->gke", p_c.astype(bf16), dOb, preferred_element_type=f32)
        dvp_ref[...] = jnp.einsum("gqk,gqe->gke", p_p.astype(bf16), dOb, preferred_element_type=f32)
        dq, dqg = norm_bwd(dqn, qx, rq, qg, HEAD ** -0.5)
        dkc, dkg1 = norm_bwd(dkn_c, kx, rk_c, kg, 1.0)
        dkp, dkg2 = norm_bwd(dkn_p, kpx, rk_p, kg, 1.0)
        dq_ref[...] = dq
        dkc_ref[...] = dkc
        dkp_ref[...] = dkp
        dqg_ref[...] += dqg
        dkg_ref[...] += dkg1 + dkg2

    blk = pl.BlockSpec((G, ATTN_BLK, HEAD), lambda i: (i, 0, 0))
    par = pl.BlockSpec((1, 1, HEAD), lambda i: (0, 0, 0))
    shp = jax.ShapeDtypeStruct((NB, ATTN_BLK, HEAD), f32)
    pshp = jax.ShapeDtypeStruct((1, 1, HEAD), f32)
    return pl.pallas_call(
        body, name="attn_bwd", grid=(NB // G,), in_specs=[blk] * 9 + [par] * 2,
        out_specs=[blk] * 5 + [par] * 2, out_shape=[shp] * 5 + [pshp] * 2,
        compiler_params=_cparams(("arbitrary",)),
    )(q, k, kp, v, vp, o, lse, do, dlse, qg, kg)


def _to_blocks(t, S):
    outs = []
    for gi, (_, d) in enumerate(ATTN_PAIRS):
        tg = t[:, gi * 256:(gi + 1) * 256].reshape(S // d, d, HEADS_PER_GROUP, HEAD)
        outs.append(tg.transpose(2, 1, 0, 3).reshape(-1, ATTN_BLK, HEAD))
    return jnp.concatenate(outs, axis=0)


def _from_blocks(b, S):
    nbg = HEADS_PER_GROUP * S // ATTN_BLK
    outs = []
    for gi, (_, d) in enumerate(ATTN_PAIRS):
        bg = b[gi * nbg:(gi + 1) * nbg].reshape(HEADS_PER_GROUP, d, S // d, HEAD)
        outs.append(bg.transpose(2, 1, 0, 3).reshape(S, HEADS_PER_GROUP * HEAD))
    return jnp.concatenate(outs, axis=1)


def _prev_block(b):
    return jnp.concatenate([jnp.zeros_like(b[:1]), b[:-1]], axis=0)


def _next_block(b):
    return jnp.concatenate([b[1:], jnp.zeros_like(b[:1])], axis=0)


def _shift_down(t):
    return jnp.concatenate([jnp.zeros_like(t[:1]), t[:-1]], axis=0)


def _shift_up(t):
    return jnp.concatenate([t[1:], jnp.zeros_like(t[:1])], axis=0)


def _rms(x, g):
    rs = lax.rsqrt(jnp.mean(x * x, axis=-1, keepdims=True) + RMS_EPS)
    return x * rs * g


def _f_rms(x, g):
    return _rms(x, g)


def _f_resid_rms(coef, x, f, g):
    xn = x + coef * f
    return xn, _rms(xn, g)


def _f_swiglu(u):
    gate, up = u[:, :D_FF], u[:, D_FF:]
    return gate * jax.nn.sigmoid(gate) * up


def _f_swiglu_bwd(dact, u):
    gate, up = u[:, :D_FF], u[:, D_FF:]
    sg = jax.nn.sigmoid(gate)
    silu = gate * sg
    dact = 0.5 * dact
    return jnp.concatenate([dact * up * (sg * (1.0 + gate * (1.0 - sg))), dact * silu], axis=1)


def _f_rms_bwd(n_parts, *args):
    dns = args[:n_parts]
    x, dres, g = args[n_parts:]
    dn = dns[0]
    for t in dns[1:]:
        dn = dn + t
    rs = lax.rsqrt(jnp.mean(x * x, axis=-1, keepdims=True) + RMS_EPS)
    xh = x * rs
    dxh = dn * g
    dx = dres + rs * (dxh - xh * jnp.mean(dxh * xh, axis=-1, keepdims=True))
    return dx, dx, jnp.sum(dn * xh, axis=0, keepdims=True)


def _f_loss(x, f, tgt):
    y = x + 0.5 * f
    diff = y - tgt
    part = 0.5 * jnp.sum(jnp.mean(diff * diff, axis=-1, keepdims=True), axis=0, keepdims=True)
    dy = diff * (1.0 / D)
    return dy, dy, jnp.broadcast_to(part, (1, 128))


def _dotb(a, b, dims):
    return lax.dot_general(a.astype(bf16), b.astype(bf16), dims, preferred_element_type=f32)


_NN = (((1,), (0,)), ((), ()))
_NT = (((1,), (1,)), ((), ()))
_TN = (((0,), (0,)), ((), ()))


def _rwkv_pre_core(prkv, prkv_prev, plora, plora_prev, mu_rkv, mu_lora, w0, w2p, a0, a2p, g2p, k_k, k_a):
    xs = prkv + (prkv_prev - prkv) * mu_rkv
    xl = plora + (plora_prev - plora) * mu_lora
    r, k, v = xs[:, :D], xs[:, D:2 * D], xs[:, 2 * D:]
    wd, ad, gd = xl[:, :128], xl[:, 128:256], xl[:, 256:]
    tw = jnp.tanh(wd)
    zw = w0 + _dotb(tw, w2p, _NN)
    sp = jnp.maximum(-zw, 0.0) + jnp.log(1.0 + jnp.exp(-jnp.abs(zw)))
    lw = -jnp.exp(-sp - 0.5)
    a = jax.nn.sigmoid(a0 + _dotb(ad, a2p, _NN))
    sg = jax.nn.sigmoid(gd)
    return dict(r=r, k=k, v=v, tw=tw, zw=zw, lw=lw, a=a, sg=sg, ad=ad)


def _f_rwkv_pre(*args):
    c = _rwkv_pre_core(*args)
    g2p, k_k, k_a = args[10], args[11], args[12]
    g = _dotb(c["sg"], g2p, _NN)
    k, a = c["k"], c["a"]
    return c["r"], c["lw"], k * (1.0 + (a - 1.0) * k_a), c["v"], k * k_k, a, g


def _f_rwkv_pre_bwd(prkv, prkv_prev, plora, plora_prev, dr, dlw, dk2, dv, dkkr, da, dya, yap,
                    mu_rkv, mu_lora, w0, w2p, a0, a2p, g2p, k_k, k_a):
    c = _rwkv_pre_core(prkv, prkv_prev, plora, plora_prev, mu_rkv, mu_lora, w0, w2p, a0, a2p, g2p, k_k, k_a)
    k, a, sg, tw, zw, lw = c["k"], c["a"], c["sg"], c["tw"], c["zw"], c["lw"]
    dg = dya * yap
    dsg = _dotb(dg, g2p, _NT)
    dgd = dsg * sg * (1.0 - sg)
    dg2p = _dotb(sg, dg, _TN)
    dk = dk2 * (1.0 + (a - 1.0) * k_a) + dkkr * k_k
    da_t = da + dk2 * k * k_a
    dk_a = jnp.sum(dk2 * k * (a - 1.0), axis=0, keepdims=True)
    dk_k = jnp.sum(dkkr * k, axis=0, keepdims=True)
    dza = da_t * a * (1.0 - a)
    da0 = jnp.sum(dza, axis=0, keepdims=True)
    dad = _dotb(dza, a2p, _NT)
    da2p = _dotb(c["ad"], dza, _TN)
    dzw = dlw * lw * jax.nn.sigmoid(-zw)
    dw0 = jnp.sum(dzw, axis=0, keepdims=True)
    dtw = _dotb(dzw, w2p, _NT)
    dw2p = _dotb(tw, dzw, _TN)
    dwd = dtw * (1.0 - tw * tw)
    dxs = jnp.concatenate([dr, dk, dv], axis=1)
    dxl = jnp.concatenate([dwd, dad, dgd], axis=1)
    dmu_rkv = jnp.sum(dxs * (prkv_prev - prkv), axis=0, keepdims=True)
    dmu_lora = jnp.sum(dxl * (plora_prev - plora), axis=0, keepdims=True)
    return (dxs * (1.0 - mu_rkv), dxs * mu_rkv, dxl * (1.0 - mu_lora), dxl * mu_lora,
            dmu_rkv, dmu_lora, dw0, da0, dk_k, dk_a, dw2p, da2p, dg2p)


def _group_alpha(lse):
    l0, l1, l2 = lse[:, :256], lse[:, 256:512], lse[:, 512:]
    m = jnp.maximum(jnp.maximum(l0, l1), l2)
    e0, e1, e2 = jnp.exp(l0 - m), jnp.exp(l1 - m), jnp.exp(l2 - m)
    inv = 1.0 / (e0 + e1 + e2)
    return jnp.concatenate([e0 * inv, e1 * inv, e2 * inv], axis=1)


def _f_combine(o, lse):
    return o * _group_alpha(lse)


def _f_combine_bwd(dyb, o, lse, bd):
    alpha = _group_alpha(lse)
    e = jnp.dot(dyb * o, bd, precision=HI, preferred_element_type=f32)
    ae = alpha * e
    tot = ae[:, :256] + ae[:, 256:512] + ae[:, 512:]
    return dyb * alpha, ae - alpha * jnp.concatenate([tot, tot, tot], axis=1)


def _f_merge(pgate, ta, tb, b_gate):
    gate = jax.nn.sigmoid(pgate + b_gate)
    return gate[:, :D] * ta + gate[:, D:] * tb


def _f_merge_bwd(dm, pgate, ta, tb, b_gate):
    gate = jax.nn.sigmoid(pgate + b_gate)
    ga, gb = gate[:, :D], gate[:, D:]
    dpg = jnp.concatenate([dm * ta * ga * (1.0 - ga), dm * tb * gb * (1.0 - gb)], axis=1)
    return dm * ga, dm * gb, dpg, jnp.sum(dpg, axis=0, keepdims=True)


def _f_mul(a, b):
    return a * b


def _f_adamw(w, g, m, v):
    m2 = ADAM_B1 * m + (1.0 - ADAM_B1) * g
    v2 = ADAM_B2 * v + (1.0 - ADAM_B2) * jnp.square(g)
    m_hat = m2 / (1.0 - ADAM_B1 ** ADAM_STEP)
    v_hat = v2 / (1.0 - ADAM_B2 ** ADAM_STEP)
    delta = -ADAM_LR * (m_hat / (jnp.sqrt(v_hat) + ADAM_EPS) + ADAM_WD * w)
    return delta, m2, v2


def _ffn_fwd(tag, n, WiT, Wo):
    S = n.shape[0]
    u = _mm(f"{tag}_up", n, WiT, "nt")
    (act,) = _rowwise(f"{tag}_swiglu", _f_swiglu, [u], [], [(D_FF, bf16)])
    f = _mm(f"{tag}_down", act, Wo, "nn")
    return u, act, f


def _ffn_bwd(tag, dxo, dxo_b, x_in, n, u, act, g, WiT, Wo):
    dact = _mm(f"{tag}_dact", dxo_b, Wo, "nt")
    dWo = _mm(f"{tag}_dwo", act, dxo_b, "tn", out_dtype=GRAD_WIRE, scale=0.5)
    (du,) = _rowwise(f"{tag}_dswiglu", _f_swiglu_bwd, [dact, u], [], [(2 * D_FF, bf16)], tm=128)
    dn = _mm(f"{tag}_dn", du, WiT, "nn")
    dWiT = _mm(f"{tag}_dwi", du, n, "tn", out_dtype=GRAD_WIRE)
    dx, dx_b, dg = _rowwise(f"{tag}_drms", functools.partial(_f_rms_bwd, 1), [dn, x_in, dxo], [g],
                            [(D, f32), (D, bf16)], [(1, D)])
    return dx, dx_b, dg, dWiT, dWo


def _local_step(x0, tgt, W, P):
    S = x0.shape[0]
    (n1,) = _rowwise("f1_rms", _f_rms, [x0], [P["ffn1_norm"]], [(D, bf16)])
    u1, act1, f1 = _ffn_fwd("f1", n1, W["f1_iT"], W["f1_o"])
    x1, h = _rowwise("mix_rms", functools.partial(_f_resid_rms, 0.5), [x0, f1], [P["mix_norm"]],
                     [(D, f32), (D, bf16)])
    prkv = _mm("p_rkv", h, W["in_rkvT"], "nt")
    plora = _mm("p_lora", h, W["in_loraT"], "nt")
    pqkv = _mm("p_qkv", h, W["in_qkvT"], "nt")
    pgate = _mm("p_gate", h, W["in_gateT"], "nt")
    prkv_prev, plora_prev = _shift_down(prkv), _shift_down(plora)
    pre_params = [P["mu_rkv"], P["mu_lora"], P["w0"], W["w2p"], P["a0"], W["a2p"], W["g2p"], P["k_k"], P["k_a"]]
    r, lw, k2, v, kkr, a, g = _rowwise("rwkv_pre", _f_rwkv_pre, [prkv, prkv_prev, plora, plora_prev], pre_params,
                                       [(D, f32)] * 7, tm=128)
    hm = [r, lw, k2, v, kkr, a]
    hp = [P["r_k"].reshape(RW_HEADS, 1, HEAD), P["ln_w"].reshape(RW_HEADS, 1, HEAD), P["ln_b"].reshape(RW_HEADS, 1, HEAD)]
    yap, wkv_h, U_h, inv_h, S0s = _wkv_fwd(*hm, *hp)
    (ya,) = _rowwise("ya_gate", _f_mul, [yap, g], [], [(D, bf16)])
    ta = _mm("proj_a", ya, W["pr"], "nn")
    qb, kb, vb = (_to_blocks(pqkv[:, i * ATTN_W:(i + 1) * ATTN_W], S) for i in range(3))
    kpb, vpb = _prev_block(kb), _prev_block(vb)
    qg, kg = P["q_norm"].reshape(1, 1, HEAD), P["k_norm"].reshape(1, 1, HEAD)
    o_b, lse_b = _attn_fwd(qb, kb, kpb, vb, vpb, qg, kg, S)
    o_t, lse_t = _from_blocks(o_b, S), _from_blocks(lse_b, S)
    (yb,) = _rowwise("attn_combine", _f_combine, [o_t, lse_t], [], [(ATTN_W, bf16)])
    tb = _mm("proj_b", yb, W["paT"], "nt")
    (merged,) = _rowwise("merge", _f_merge, [pgate, ta, tb], [P["b_gate"]], [(D, bf16)])
    mo = _mm("mix_out", merged, W["out"], "nn")
    x2, n2 = _rowwise("f2_rms", functools.partial(_f_resid_rms, 1.0), [x1, mo], [P["ffn2_norm"]],
                      [(D, f32), (D, bf16)])
    u2, act2, f2 = _ffn_fwd("f2", n2, W["f2_iT"], W["f2_o"])
    dx3, dx3_b, loss = _rowwise("loss", _f_loss, [x2, f2, tgt], [], [(D, f32), (D, bf16)], [(1, 128)])
    G, Gs = {}, {}
    dx2, dx2_b, Gs["ffn2_norm"], G["f2_iT"], G["f2_o"] = _ffn_bwd("f2", dx3, dx3_b, x2, n2, u2, act2, P["ffn2_norm"],
                                                                 W["f2_iT"], W["f2_o"])
    dmerged = _mm("d_merged", dx2_b, W["out"], "nt")
    G["out"] = _mm("dw_out", merged, dx2_b, "tn", out_dtype=GRAD_WIRE)
    dta, dtb, dpgate, Gs["b_gate"] = _rowwise("merge_bwd", _f_merge_bwd, [dmerged, pgate, ta, tb], [P["b_gate"]],
                                              [(D, bf16), (D, bf16), (2 * D, bf16)], [(1, 2 * D)])
    dya = _mm("d_ya", dta, W["pr"], "nt")
    G["pr"] = _mm("dw_pr", ya, dta, "tn", out_dtype=GRAD_WIRE)
    dyb = _mm("d_yb", dtb, W["paT"], "nn")
    G["paT"] = _mm("dw_pa", dtb, yb, "tn", out_dtype=GRAD_WIRE)
    (dyap,) = _rowwise("ya_gate_bwd", _f_mul, [dya, g], [], [(D, f32)])
    hg = _wkv_bwd(dyap, *hm, wkv_h, U_h, inv_h, S0s, *hp)
    dr, dlw, dk2, dv, dkkr, da = hg[:6]
    Gs["r_k"], Gs["ln_w"], Gs["ln_b"] = (t.reshape(1, D) for t in hg[6:])
    lp = sum(LORA_PAD)
    (dprkv_a, dprkv_b, dplora_a, dplora_b, Gs["mu_rkv"], Gs["mu_lora"], Gs["w0"], Gs["a0"], Gs["k_k"], Gs["k_a"],
     dw2p, da2p, dg2p) = _rowwise(
        "rwkv_pre_bwd", _f_rwkv_pre_bwd,
        [prkv, prkv_prev, plora, plora_prev, dr, dlw, dk2, dv, dkkr, da, dya, yap], pre_params,
        [(3 * D, f32), (3 * D, f32), (lp, f32), (lp, f32)],
        [(1, 3 * D), (1, lp), (1, D), (1, D), (1, D), (1, D), (LORA_PAD[0], D), (LORA_PAD[1], D), (LORA_PAD[2], D)],
        tm=128)
    G["w2T"], G["a2T"], G["g2T"] = dw2p[:LORA_W[0]].T, da2p[:LORA_W[1]].T, dg2p[:LORA_W[2]].T
    dprkv = (dprkv_a + _shift_up(dprkv_b)).astype(bf16)
    dplora = (dplora_a + _shift_up(dplora_b)).astype(bf16)
    bd = (jnp.arange(ATTN_W)[:, None] // HEAD == jnp.arange(ATTN_W)[None, :] // HEAD).astype(f32)
    do_t, dlse_t = _rowwise("attn_combine_bwd", _f_combine_bwd, [dyb, o_t, lse_t], [bd], [(ATTN_W, f32)] * 2)
    dq_b, dkc_b, dkp_b, dvc_b, dvp_b, dqg, dkg = _attn_bwd(qb, kb, kpb, vb, vpb, o_b, lse_b, _to_blocks(do_t, S),
                                                            _to_blocks(dlse_t, S), qg, kg, S)
    Gs["q_norm"], Gs["k_norm"] = dqg.reshape(1, HEAD), dkg.reshape(1, HEAD)
    dpqkv = jnp.concatenate([_from_blocks(dq_b, S), _from_blocks(dkc_b + _next_block(dkp_b), S),
                             _from_blocks(dvc_b + _next_block(dvp_b), S)], axis=1).astype(bf16)
    dh = [_mm("dh_rkv", dprkv, W["in_rkvT"], "nn"), _mm("dh_lora", dplora, W["in_loraT"], "nn"),
          _mm("dh_qkv", dpqkv, W["in_qkvT"], "nn"), _mm("dh_gate", dpgate, W["in_gateT"], "nn")]
    dW_rkv = _mm("dw_rkv", dprkv, h, "tn", out_dtype=GRAD_WIRE)
    dW_lora = _mm("dw_lora", dplora, h, "tn", out_dtype=GRAD_WIRE)
    dW_qkv = _mm("dw_qkv", dpqkv, h, "tn", out_dtype=GRAD_WIRE)
    dW_gate = _mm("dw_gate", dpgate, h, "tn", out_dtype=GRAD_WIRE)
    o1, o2 = LORA_PAD[0], LORA_PAD[0] + LORA_PAD[1]
    G["inT"] = jnp.concatenate([dW_rkv, dW_lora[:LORA_W[0]], dW_lora[o1:o1 + LORA_W[1]], dW_lora[o2:o2 + LORA_W[2]],
                                dW_qkv, dW_gate], axis=0)
    dx1, dx1_b, Gs["mix_norm"] = _rowwise("mix_drms", functools.partial(_f_rms_bwd, 4), [*dh, x1, dx2],
                                          [P["mix_norm"]], [(D, f32), (D, bf16)], [(1, D)])
    dx0, _, Gs["ffn1_norm"], G["f1_iT"], G["f1_o"] = _ffn_bwd("f1", dx1, dx1_b, x0, n1, u1, act1, P["ffn1_norm"],
                                                             W["f1_iT"], W["f1_o"])
    return loss[0, 0], dx0, G, Gs


def _peer(k):
    x, y, c = lax.axis_index("x"), lax.axis_index("y"), lax.axis_index("c")
    px = 1 - x if k & 4 else x
    py = 1 - y if k & 2 else y
    pc = 1 - c if k & 1 else c
    return (px, py, pc), 4 * px + 2 * py + pc


def _all_gather(pack):
    R, C = pack.shape

    def body(x_ref, out_ref, send_sems, recv_sems, local_sem):
        x, y, c = lax.axis_index("x"), lax.axis_index("y"), lax.axis_index("c")
        me, sibling = (x, y, c), (x, y, 1 - c)
        chips = [(1 - x, y), (x, 1 - y), (1 - x, 1 - y)]

        def slot(px, py, pc):
            return out_ref.at[4 * px + 2 * py + pc]

        def copy(k, block, to, src=None):
            return pltpu.make_async_remote_copy(
                src_ref=slot(*block) if src is None else src, dst_ref=slot(*block), send_sem=send_sems.at[k],
                recv_sem=recv_sems.at[k], device_id=to, device_id_type=MESH)

        mine = pltpu.make_async_copy(x_ref, slot(*me), local_sem)
        mine.start()
        first = [copy(0, me, sibling, src=x_ref)]
        first += [copy(1 + j, me, (*chip, c), src=x_ref) for j, chip in enumerate(chips)]
        for cp in first:
            cp.start()
        passed = [copy(4 + j, (*chip, c), sibling) for j, chip in enumerate(chips)]
        for j, chip in enumerate(chips):
            copy(1 + j, (*chip, c), me).wait_recv()
            passed[j].start()
        copy(0, sibling, me).wait_recv()
        for j, chip in enumerate(chips):
            copy(4 + j, (*chip, 1 - c), me).wait_recv()
        for cp in first + passed:
            cp.wait_send()
        mine.wait()

    return pl.pallas_call(
        body, name="weight_all_gather", out_shape=jax.ShapeDtypeStruct((N_DEV, R, C), pack.dtype),
        in_specs=[pl.BlockSpec(memory_space=pl.ANY)], out_specs=pl.BlockSpec(memory_space=pl.ANY),
        scratch_shapes=[pltpu.SemaphoreType.DMA((N_DEV - 1,)), pltpu.SemaphoreType.DMA((N_DEV - 1,)),
                        pltpu.SemaphoreType.DMA(())],
    )(pack)


N_CHIP = 4


def _grad_pair(pieces):
    n = len(pieces)
    C = pieces[0].shape[2]
    rows = [p.shape[1] for p in pieces]
    offs = [sum(rows[:i]) for i in range(n)]
    R = sum(rows)

    def body(*refs):
        g_refs, (mine_ref, other_ref, send_sems, recv_sems, local_sems) = refs[:n], refs[n:]
        x, y, c = lax.axis_index("x"), lax.axis_index("y"), lax.axis_index("c")
        sibling = (x, y, 1 - c)
        remote, local = [], []
        for i, g_ref in enumerate(g_refs):
            for k in range(N_CHIP):
                chip_dev = 4 * (k // 2) + 2 * (k % 2)
                where = (k, pl.ds(offs[i], rows[i]))
                cp = pltpu.make_async_remote_copy(
                    src_ref=g_ref.at[chip_dev + 1 - c], dst_ref=other_ref.at[where], send_sem=send_sems.at[i * N_CHIP + k],
                    recv_sem=recv_sems.at[i * N_CHIP + k], device_id=sibling, device_id_type=MESH)
                cp.start()
                remote.append(cp)
                lc = pltpu.make_async_copy(g_ref.at[chip_dev + c], mine_ref.at[where], local_sems.at[i * N_CHIP + k])
                lc.start()
                local.append(lc)
        for cp in remote + local:
            cp.wait()

    shp = jax.ShapeDtypeStruct((N_CHIP, R, C), pieces[0].dtype)
    return pl.pallas_call(
        body, name="grad_pair", out_shape=[shp, shp],
        in_specs=[pl.BlockSpec(memory_space=pl.ANY)] * n, out_specs=[pl.BlockSpec(memory_space=pl.ANY)] * 2,
        scratch_shapes=[pltpu.SemaphoreType.DMA((n * N_CHIP,))] * 3,
    )(*pieces)


def _pair_add(mine, other, tr):
    _, R, C = mine.shape

    def body(g_ref, o_ref, out_ref):
        out_ref[...] = (g_ref[...].astype(f32) + o_ref[...].astype(f32)).astype(out_ref.dtype)

    blk = pl.BlockSpec((1, tr, C), lambda k, i: (k, i, 0))
    return pl.pallas_call(
        body, name="pair_add", grid=(N_CHIP, R // tr), in_specs=[blk, blk], out_specs=blk,
        out_shape=jax.ShapeDtypeStruct((N_CHIP, R, C), mine.dtype),
        compiler_params=_cparams(("parallel", "parallel")),
    )(mine, other)


def _grad_cross(part):
    _, R, C = part.shape

    def body(p_ref, out_ref, send_sems, recv_sems, local_sem):
        x, y, c = lax.axis_index("x"), lax.axis_index("y"), lax.axis_index("c")
        my_chip = 2 * x + y
        mine = pltpu.make_async_copy(p_ref.at[my_chip], out_ref.at[my_chip], local_sem)
        mine.start()
        flips = [(1, 0), (0, 1), (1, 1)]
        copies = []
        for j, (fx, fy) in enumerate(flips):
            px = 1 - x if fx else x
            py = 1 - y if fy else y
            cp = pltpu.make_async_remote_copy(src_ref=p_ref.at[2 * px + py], dst_ref=out_ref.at[my_chip],
                                              send_sem=send_sems.at[j], recv_sem=recv_sems.at[j],
                                              device_id=(px, py, c), device_id_type=MESH)
            cp.start()
            copies.append((cp, 2 * px + py))
        for j, (cp, peer_chip) in enumerate(copies):
            pltpu.make_async_remote_copy(src_ref=p_ref.at[peer_chip], dst_ref=out_ref.at[peer_chip],
                                         send_sem=send_sems.at[j], recv_sem=recv_sems.at[j],
                                         device_id=(x, y, c), device_id_type=MESH).wait_recv()
        for cp, _ in copies:
            cp.wait_send()
        mine.wait()

    return pl.pallas_call(
        body, name="grad_cross", out_shape=jax.ShapeDtypeStruct(part.shape, part.dtype),
        in_specs=[pl.BlockSpec(memory_space=pl.ANY)], out_specs=pl.BlockSpec(memory_space=pl.ANY),
        scratch_shapes=[pltpu.SemaphoreType.DMA((3,)), pltpu.SemaphoreType.DMA((3,)), pltpu.SemaphoreType.DMA(())],
    )(part)


def _sum_slots(name, parts, tr):
    n, R, C = parts.shape

    def body(p_ref, o_ref):
        acc = p_ref[0].astype(f32)
        for i in range(1, n):
            acc = acc + p_ref[i].astype(f32)
        o_ref[...] = acc

    return pl.pallas_call(
        body, name=name, grid=(R // tr,), in_specs=[pl.BlockSpec((n, tr, C), lambda i: (0, i, 0))],
        out_specs=pl.BlockSpec((tr, C), lambda i: (i, 0)), out_shape=jax.ShapeDtypeStruct((R, C), f32),
        compiler_params=_cparams(("parallel",)),
    )(parts)


def _small_all_reduce(small):
    R, C = small.shape

    def body(x_ref, o_ref, buf, send_sems, recv_sems):
        _, me = _peer(0)
        buf[me] = x_ref[...]
        sends = []
        for k in range(1, N_DEV):
            dev, _ = _peer(k)
            cp = pltpu.make_async_remote_copy(src_ref=x_ref, dst_ref=buf.at[me], send_sem=send_sems.at[k - 1],
                                              recv_sem=recv_sems.at[k - 1], device_id=dev, device_id_type=MESH)
            cp.start()
            sends.append(cp)
        for k in range(1, N_DEV):
            dev, idx = _peer(k)
            pltpu.make_async_remote_copy(src_ref=x_ref, dst_ref=buf.at[idx], send_sem=send_sems.at[k - 1],
                                         recv_sem=recv_sems.at[k - 1], device_id=dev, device_id_type=MESH).wait_recv()
        for cp in sends:
            cp.wait_send()
        acc = buf[0]
        for i in range(1, N_DEV):
            acc = acc + buf[i]
        o_ref[...] = acc

    return pl.pallas_call(
        body, name="small_all_reduce", out_shape=jax.ShapeDtypeStruct((R, C), f32),
        in_specs=[pl.BlockSpec(memory_space=pltpu.VMEM)], out_specs=pl.BlockSpec(memory_space=pltpu.VMEM),
        scratch_shapes=[pltpu.VMEM((N_DEV, R, C), f32), pltpu.SemaphoreType.DMA((N_DEV - 1,)),
                        pltpu.SemaphoreType.DMA((N_DEV - 1,))],
    )(small)


_BIG = (
    ("ffn1_w_in", True), ("ffn1_w_out", False), ("w_in", True), ("rwkv_w2", True), ("rwkv_a2", True),
    ("rwkv_g2", True), ("w_proj_rwkv", False), ("w_proj_attn", True), ("w_out", False),
    ("ffn2_w_in", True), ("ffn2_w_out", False))
_SMALL = ("ffn1_norm", "mix_norm", "b_gate", "rwkv_mu", "rwkv_w0", "rwkv_a0", "rwkv_k_k", "rwkv_k_a", "rwkv_r_k",
          "rwkv_ln_w", "rwkv_ln_b", "attn_q_norm", "attn_k_norm", "ffn2_norm")


PACK_ALIGN = 16


def _pack_rows(shape):
    n = shape[0] * shape[1] // D
    return n, -(-n // PACK_ALIGN) * PACK_ALIGN


def _pack_big(shards):
    parts, used = [], 0
    for name, tr in _BIG:
        t = shards[name]
        n, n_pad = _pack_rows(t.shape)
        parts.append((t.T if tr else t).reshape(n, D))
        if n_pad > n:
            parts.append(jnp.zeros((n_pad - n, D), t.dtype))
        used += n_pad
    parts.append(jnp.zeros((PACK_ROWS - used, D), parts[0].dtype))
    return jnp.concatenate(parts, axis=0)


def _unpack_big(pack, like):
    out, off = {}, 0
    for name, tr in _BIG:
        shp = like[name].shape
        n, n_pad = _pack_rows(shp)
        t = pack[off:off + n]
        out[name] = t.reshape(shp[1], shp[0]).T if tr else t.reshape(shp)
        off += n_pad
    return out


def _small_rows(name, t):
    flat = t.reshape(-1)
    pad = (-flat.shape[0]) % D
    return jnp.pad(flat, (0, pad)).reshape(-1, D)


def _pack_small(vals):
    parts = [_small_rows(n, vals[n]) for n in _SMALL]
    used = sum(p.shape[0] for p in parts)
    parts.append(jnp.zeros((SMALL_ROWS - used, D), f32))
    return jnp.concatenate(parts, axis=0)


def _unpack_small(pack, like):
    out, off = {}, 0
    for n in _SMALL:
        size = like[n].size
        rows = -(-size // D)
        out[n] = pack[off:off + rows].reshape(-1)[:size].reshape(like[n].shape)
        off += rows
    return out


def _pad_rows(t, n):
    return jnp.concatenate([t, jnp.zeros((n - t.shape[0],) + t.shape[1:], t.dtype)], axis=0)


def _build_W(full):
    inT = full["w_in"]
    z64, z96 = jnp.zeros((64, D), inT.dtype), jnp.zeros((96, D), inT.dtype)
    return {
        "f1_iT": full["ffn1_w_in"], "f1_o": full["ffn1_w_out"], "f2_iT": full["ffn2_w_in"], "f2_o": full["ffn2_w_out"],
        "in_rkvT": inT[:3 * D],
        "in_loraT": jnp.concatenate([inT[3072:3136], z64, inT[3136:3200], z64, inT[3200:3360], z96], axis=0),
        "in_qkvT": inT[3360:3360 + 3 * ATTN_W], "in_gateT": inT[3360 + 3 * ATTN_W:],
        "w2p": jnp.concatenate([full["rwkv_w2"].T, z64], axis=0),
        "a2p": jnp.concatenate([full["rwkv_a2"].T, z64], axis=0),
        "g2p": jnp.concatenate([full["rwkv_g2"].T, z96], axis=0),
        "pr": full["w_proj_rwkv"], "paT": full["w_proj_attn"], "out": full["w_out"],
    }


def _build_P(Wl):
    mu = Wl["rwkv_mu"]
    z64f, z96f = jnp.zeros((1, 64), f32), jnp.zeros((1, 96), f32)
    return {
        "ffn1_norm": Wl["ffn1_norm"][None], "mix_norm": Wl["mix_norm"][None], "ffn2_norm": Wl["ffn2_norm"][None],
        "b_gate": Wl["b_gate"][None], "mu_rkv": mu[None, :3 * D],
        "mu_lora": jnp.concatenate([mu[None, 3072:3136], z64f, mu[None, 3136:3200], z64f, mu[None, 3200:3360], z96f], axis=1),
        "w0": Wl["rwkv_w0"][None], "a0": Wl["rwkv_a0"][None], "k_k": Wl["rwkv_k_k"][None], "k_a": Wl["rwkv_k_a"][None],
        "r_k": Wl["rwkv_r_k"].reshape(1, D), "ln_w": Wl["rwkv_ln_w"][None], "ln_b": Wl["rwkv_ln_b"][None],
        "q_norm": Wl["attn_q_norm"][None], "k_norm": Wl["attn_k_norm"][None],
    }


def kernel(x, ffn1_norm, ffn1_w_in, ffn1_w_out, mix_norm, w_in, b_gate, rwkv_mu, rwkv_w0, rwkv_w2, rwkv_a0, rwkv_a2, rwkv_g2, rwkv_k_k, rwkv_k_a, rwkv_r_k, rwkv_ln_w, rwkv_ln_b, attn_q_norm, attn_k_norm, w_proj_rwkv, w_proj_attn, w_out, ffn2_norm, ffn2_w_in, ffn2_w_out, loss_target, m_ffn1_norm, m_ffn1_w_in, m_ffn1_w_out, m_mix_norm, m_w_in, m_b_gate, m_rwkv_mu, m_rwkv_w0, m_rwkv_w2, m_rwkv_a0, m_rwkv_a2, m_rwkv_g2, m_rwkv_k_k, m_rwkv_k_a, m_rwkv_r_k, m_rwkv_ln_w, m_rwkv_ln_b, m_attn_q_norm, m_attn_k_norm, m_w_proj_rwkv, m_w_proj_attn, m_w_out, m_ffn2_norm, m_ffn2_w_in, m_ffn2_w_out, v_ffn1_norm, v_ffn1_w_in, v_ffn1_w_out, v_mix_norm, v_w_in, v_b_gate, v_rwkv_mu, v_rwkv_w0, v_rwkv_w2, v_rwkv_a0, v_rwkv_a2, v_rwkv_g2, v_rwkv_k_k, v_rwkv_k_a, v_rwkv_r_k, v_rwkv_ln_w, v_rwkv_ln_b, v_attn_q_norm, v_attn_k_norm, v_w_proj_rwkv, v_w_proj_attn, v_w_out, v_ffn2_norm, v_ffn2_w_in, v_ffn2_w_out):
    names = ("ffn1_norm", "ffn1_w_in", "ffn1_w_out", "mix_norm", "w_in", "b_gate", "rwkv_mu", "rwkv_w0", "rwkv_w2",
             "rwkv_a0", "rwkv_a2", "rwkv_g2", "rwkv_k_k", "rwkv_k_a", "rwkv_r_k", "rwkv_ln_w", "rwkv_ln_b",
             "attn_q_norm", "attn_k_norm", "w_proj_rwkv", "w_proj_attn", "w_out", "ffn2_norm", "ffn2_w_in", "ffn2_w_out")
    w_all = (ffn1_norm, ffn1_w_in, ffn1_w_out, mix_norm, w_in, b_gate, rwkv_mu, rwkv_w0, rwkv_w2, rwkv_a0, rwkv_a2,
             rwkv_g2, rwkv_k_k, rwkv_k_a, rwkv_r_k, rwkv_ln_w, rwkv_ln_b, attn_q_norm, attn_k_norm, w_proj_rwkv,
             w_proj_attn, w_out, ffn2_norm, ffn2_w_in, ffn2_w_out)
    m_all = (m_ffn1_norm, m_ffn1_w_in, m_ffn1_w_out, m_mix_norm, m_w_in, m_b_gate, m_rwkv_mu, m_rwkv_w0, m_rwkv_w2,
             m_rwkv_a0, m_rwkv_a2, m_rwkv_g2, m_rwkv_k_k, m_rwkv_k_a, m_rwkv_r_k, m_rwkv_ln_w, m_rwkv_ln_b,
             m_attn_q_norm, m_attn_k_norm, m_w_proj_rwkv, m_w_proj_attn, m_w_out, m_ffn2_norm, m_ffn2_w_in, m_ffn2_w_out)
    v_all = (v_ffn1_norm, v_ffn1_w_in, v_ffn1_w_out, v_mix_norm, v_w_in, v_b_gate, v_rwkv_mu, v_rwkv_w0, v_rwkv_w2,
             v_rwkv_a0, v_rwkv_a2, v_rwkv_g2, v_rwkv_k_k, v_rwkv_k_a, v_rwkv_r_k, v_rwkv_ln_w, v_rwkv_ln_b,
             v_attn_q_norm, v_attn_k_norm, v_w_proj_rwkv, v_w_proj_attn, v_w_out, v_ffn2_norm, v_ffn2_w_in, v_ffn2_w_out)
    Wl = {n: t[0] for n, t in zip(names, w_all)}
    Ml = {n: t[0] for n, t in zip(names, m_all)}
    Vl = {n: t[0] for n, t in zip(names, v_all)}
    big = [n for n, _ in _BIG]

    w_pack = _pack_big({n: Wl[n] for n in big})
    gathered = _all_gather(w_pack.astype(bf16))
    full, off = {}, 0
    for n, tr in _BIG:
        shp = Wl[n].shape
        rows, rows_pad = _pack_rows(shp)
        t = gathered[:, off:off + rows]
        r_loc, c_loc = (shp[1], shp[0]) if tr else shp
        full[n] = t.reshape(N_DEV * r_loc, c_loc)
        off += rows_pad
    W, P = _build_W(full), _build_P(Wl)

    loss_local, dx0, G, Gs = _local_step(x[0], loss_target[0], W, P)

    g_full = {"ffn1_w_in": G["f1_iT"], "ffn1_w_out": G["f1_o"], "w_in": G["inT"], "rwkv_w2": G["w2T"],
              "rwkv_a2": G["a2T"], "rwkv_g2": G["g2T"], "w_proj_rwkv": G["pr"], "w_proj_attn": G["paT"],
              "w_out": G["out"], "ffn2_w_in": G["f2_iT"], "ffn2_w_out": G["f2_o"]}
    pieces, used = [], 0
    for n in big:
        rows, rows_pad = _pack_rows(Wl[n].shape)
        piece = g_full[n].astype(GRAD_WIRE).reshape(N_DEV, rows, D)
        if rows_pad > rows:
            piece = jnp.pad(piece, ((0, 0), (0, rows_pad - rows), (0, 0)))
        pieces.append(piece)
        used += rows_pad
    pieces.append(jnp.zeros((N_DEV, PACK_ROWS - used, D), GRAD_WIRE))
    chip_part = _pair_add(*_grad_pair(pieces), 512)
    g_pack = _sum_slots("grad_sum", _grad_cross(chip_part), 256)

    mu_g = Gs["mu_rkv"], Gs["mu_lora"]
    o1, o2 = LORA_PAD[0], LORA_PAD[0] + LORA_PAD[1]
    g_small_local = {
        "ffn1_norm": Gs["ffn1_norm"], "mix_norm": Gs["mix_norm"], "b_gate": Gs["b_gate"],
        "rwkv_mu": jnp.concatenate([mu_g[0], mu_g[1][:, :64], mu_g[1][:, o1:o1 + 64], mu_g[1][:, o2:o2 + 160]], axis=1),
        "rwkv_w0": Gs["w0"], "rwkv_a0": Gs["a0"], "rwkv_k_k": Gs["k_k"], "rwkv_k_a": Gs["k_a"], "rwkv_r_k": Gs["r_k"],
        "rwkv_ln_w": Gs["ln_w"], "rwkv_ln_b": Gs["ln_b"], "attn_q_norm": Gs["q_norm"], "attn_k_norm": Gs["k_norm"],
        "ffn2_norm": Gs["ffn2_norm"]}
    gs_pack = _small_all_reduce(_pack_small(g_small_local))

    d_pack, m_pack, v_pack = _rowwise(
        "adamw_big", _f_adamw, [w_pack, g_pack, _pack_big({n: Ml[n] for n in big}), _pack_big({n: Vl[n] for n in big})],
        [], [(D, f32)] * 3)
    ds_pack, ms_pack, vs_pack = _rowwise(
        "adamw_small", _f_adamw, [_pack_small(Wl), gs_pack, _pack_small(Ml), _pack_small(Vl)], [], [(D, f32)] * 3)

    def unpack(pb, ps):
        d = _unpack_big(pb, Wl)
        d.update(_unpack_small(ps, Wl))
        return [d[n][None] for n in names]

    loss = lax.psum(loss_local, ("x", "y", "c"))
    return (loss, dx0[None], *unpack(g_pack, gs_pack), *unpack(d_pack, ds_pack), *unpack(m_pack, ms_pack),
            *unpack(v_pack, vs_pack))
```

```python
import functools

import jax
import jax.numpy as jnp
from jax import lax
from jax.experimental import pallas as pl
from jax.experimental.pallas import tpu as pltpu

f32 = jnp.float32
bf16 = jnp.bfloat16
HI = lax.Precision.HIGHEST
MESH = pl.DeviceIdType.MESH

N_DEV = 8
D = 1024
D_FF = 2816
HEAD = 64
RW_HEADS = 16
ATTN_PAIRS = ((128, 1), (512, 4), (2048, 16))
ATTN_BLK = 128
HEADS_PER_GROUP = 4
ATTN_W = 768
LORA_PAD = (128, 128, 256)
LORA_W = (64, 64, 160)
GN_EPS = 64e-5
RMS_EPS = 1e-6
NEG_INF = -1e30
WKV_T = 64
GRAD_WIRE = bf16
PACK_BLOCK = 128
PACK_ROWS = 3840
SMALL_ROWS = 24
VMEM_LIMIT = 56 * 1024 * 1024

ADAM_LR, ADAM_B1, ADAM_B2, ADAM_EPS, ADAM_WD, ADAM_STEP = 0.001, 0.9, 0.999, 1e-08, 0.01, 10


def _cparams(sem):
    return pltpu.CompilerParams(dimension_semantics=sem, vmem_limit_bytes=VMEM_LIMIT)


def _pick(n, cands):
    for c in cands:
        if n % c == 0:
            return c
    return n


def _rowwise(name, fn, rows, params, outs, accs=(), tm=256):
    S = rows[0].shape[0]
    tm = min(tm, S)
    n_in = len(rows) + len(params)
    n_out = len(outs)
    n_acc = len(accs)

    def body(*refs):
        res = fn(*[r[...] for r in refs[:n_in]])
        if not isinstance(res, (tuple, list)):
            res = (res,)
        out_refs = refs[n_in:]
        for j in range(n_out):
            out_refs[j][...] = res[j].astype(out_refs[j].dtype)
        if n_acc:
            @pl.when(pl.program_id(0) == 0)
            def _():
                for j in range(n_acc):
                    out_refs[n_out + j][...] = jnp.zeros(out_refs[n_out + j].shape, f32)
            for j in range(n_acc):
                out_refs[n_out + j][...] += res[n_out + j]

    in_specs = [pl.BlockSpec((tm, a.shape[1]), lambda i: (i, 0)) for a in rows]
    in_specs += [pl.BlockSpec(p.shape, lambda i, nd=p.ndim: (0,) * nd) for p in params]
    out_specs = [pl.BlockSpec((tm, w), lambda i: (i, 0)) for w, _ in outs]
    out_specs += [pl.BlockSpec(s, lambda i: (0, 0)) for s in accs]
    out_shape = [jax.ShapeDtypeStruct((S, w), dt) for w, dt in outs]
    out_shape += [jax.ShapeDtypeStruct(s, f32) for s in accs]
    res = pl.pallas_call(
        body, name=name, grid=(S // tm,), in_specs=in_specs, out_specs=out_specs, out_shape=out_shape,
        compiler_params=_cparams(("arbitrary",)),
    )(*rows, *params)
    return res


MM_VMEM_BUDGET = 40 * 1024 * 1024
MM_STEP_US = 0.35
MM_FLOPS_PER_US = 9.0e8
MM_HBM_BYTES_PER_US = 3.0e6


def _tile_options(n, cap):
    opts = [d for d in range(128, min(n, cap) + 1, 128) if n % d == 0]
    return opts or [n]


def _mm_tiles(M, N, K, sa, sb, so):
    best, best_cost = None, None
    for tm in _tile_options(M, 2048):
        for tn in _tile_options(N, 2048):
            for tk in _tile_options(K, 4096):
                vmem = 2 * (tm * tk * sa + tk * tn * sb) + 2 * tm * tn * so + (tm * tn * 4 if tk < K else 0)
                if vmem > MM_VMEM_BUDGET:
                    continue
                steps = (M // tm) * (N // tn) * (K // tk)
                traffic = M * K * sa * (N // tn) + K * N * sb * (M // tm) + M * N * so
                cost = (max(2.0 * M * N * K / MM_FLOPS_PER_US, traffic / MM_HBM_BYTES_PER_US) + steps * MM_STEP_US
                        + (tm * tk * sa + tk * tn * sb) / MM_HBM_BYTES_PER_US)
                if best_cost is None or cost < best_cost:
                    best, best_cost = (tm, tn, tk), cost
    return best


def _mm(name, a, b, mode, out_dtype=f32, scale=None):
    if mode == "nn":
        (M, K), (_, N) = a.shape, b.shape
    elif mode == "nt":
        (M, K), (N, _) = a.shape, b.shape
    else:
        (K, M), (_, N) = a.shape, b.shape
    tm, tn, tk = _mm_tiles(M, N, K, a.dtype.itemsize, b.dtype.itemsize, jnp.dtype(out_dtype).itemsize)
    nk = K // tk
    if mode == "nn":
        a_spec = pl.BlockSpec((tm, tk), lambda i, j, k: (i, k))
        b_spec = pl.BlockSpec((tk, tn), lambda i, j, k: (k, j))
        dims = (((1,), (0,)), ((), ()))
    elif mode == "nt":
        a_spec = pl.BlockSpec((tm, tk), lambda i, j, k: (i, k))
        b_spec = pl.BlockSpec((tn, tk), lambda i, j, k: (j, k))
        dims = (((1,), (1,)), ((), ()))
    else:
        a_spec = pl.BlockSpec((tk, tm), lambda i, j, k: (k, i))
        b_spec = pl.BlockSpec((tk, tn), lambda i, j, k: (k, j))
        dims = (((0,), (0,)), ((), ()))

    def finish(acc):
        return acc if scale is None else acc * scale

    def body(a_ref, b_ref, o_ref, *scratch):
        part = lax.dot_general(a_ref[...].astype(bf16), b_ref[...].astype(bf16), dims,
                               preferred_element_type=f32)
        if nk == 1:
            o_ref[...] = finish(part).astype(o_ref.dtype)
        else:
            acc_ref = scratch[0]
            k = pl.program_id(2)

            @pl.when(k == 0)
            def _():
                acc_ref[...] = part

            @pl.when(k > 0)
            def _():
                acc_ref[...] += part

            @pl.when(k == nk - 1)
            def _():
                o_ref[...] = finish(acc_ref[...]).astype(o_ref.dtype)

    return pl.pallas_call(
        body, name=name, grid=(M // tm, N // tn, nk), in_specs=[a_spec, b_spec],
        out_specs=pl.BlockSpec((tm, tn), lambda i, j, k: (i, j)),
        out_shape=jax.ShapeDtypeStruct((M, N), out_dtype),
        scratch_shapes=[] if nk == 1 else [pltpu.VMEM((tm, tn), f32)],
        compiler_params=_cparams(("parallel", "parallel", "arbitrary")),
    )(a, b)


def _sp(x):
    hi = x.astype(bf16)
    return hi, (x - hi.astype(f32)).astype(bf16)


def _cat(parts):
    return tuple(jnp.concatenate(p, axis=1) for p in zip(*parts))


def _bmm(eq, a, b):
    (ah, al), (bh, bl) = a, b
    dot = functools.partial(jnp.einsum, eq, preferred_element_type=f32)
    return dot(ah, bh) + (dot(ah, bl) + dot(al, bh))


def _tri_dot(eq, tri, x):
    h1 = x.astype(bf16)
    r1 = x - h1.astype(f32)
    h2 = r1.astype(bf16)
    h3 = (r1 - h2.astype(f32)).astype(bf16)
    dot = functools.partial(jnp.einsum, eq, preferred_element_type=f32)
    return dot(tri, h1) + (dot(tri, h2) + dot(tri, h3))


def _tri_masks(T):
    ti = lax.broadcasted_iota(jnp.int32, (T, T), 0)
    si = lax.broadcasted_iota(jnp.int32, (T, T), 1)
    return ti >= si, ti > si


def _wkv_prep(r, lw, k, kkr, a):
    H, T, _ = r.shape
    low_i, low_s = _tri_masks(T)
    nrm = jnp.sqrt(jnp.sum(kkr * kkr, axis=-1, keepdims=True))
    den = jnp.maximum(nrm, 1e-12)
    kk = kkr / den
    tri = jnp.broadcast_to(low_i.astype(bf16)[None], (H, T, T))
    cl = _tri_dot("hts,hsn->htn", tri, lw)
    c = jnp.exp(cl)
    cprev = jnp.exp(cl - lw)
    cinv = jnp.exp(-cl)
    bt, kt = _sp(kk * a * cinv), _sp(k * cinv)
    L = _cat([_sp(r * c), _sp(-kk * cprev)])
    Mb = _bmm("htn,hsn->hts", L, bt)
    Mk = _bmm("htn,hsn->hts", L, kt)
    A_rb = jnp.where(low_i[None], Mb[:, :T], 0.0)
    A_ab = jnp.where(low_s[None], Mb[:, T:], 0.0)
    Mk = jnp.concatenate([jnp.where(low_i[None], Mk[:, :T], 0.0), jnp.where(low_s[None], Mk[:, T:], 0.0)], axis=1)
    return dict(kk=kk, den=den, nrm=nrm, c=c, cprev=cprev, cinv=cinv, L=L, kt=kt, bt=bt,
                A_ab=A_ab, A_rb=A_rb, Mk=Mk, cT=c[:, T - 1:T, :])


def _tri_inverse(A):
    T = A.shape[-1]
    eye = (lax.broadcasted_iota(jnp.int32, (T, T), 0) == lax.broadcasted_iota(jnp.int32, (T, T), 1)).astype(f32)
    inv = eye[None] + A
    X = A
    n = 1
    while 2 * n < T:
        Xs = _sp(X)
        X = _bmm("hts,hsu->htu", Xs, Xs)
        inv = inv + _bmm("hts,hsu->htu", _sp(inv), _sp(X))
        n *= 2
    return inv


def _wkv_chunk_fwd(S0, r, lw, k, v, kkr, a):
    T = r.shape[1]
    q = _wkv_prep(r, lw, k, kkr, a)
    inv = _tri_inverse(q["A_ab"])
    vs = _sp(v)
    P = _bmm("htk,hvk->htv", q["L"], _sp(S0)) + _bmm("hts,hsv->htv", _sp(q["Mk"]), vs)
    U = _bmm("hts,hsv->htv", _sp(inv), _sp(P[:, T:]))
    Us = _sp(U)
    Y = P[:, :T] + _bmm("hts,hsv->htv", _sp(q["A_rb"]), Us)
    S1 = (S0 + _bmm("htv,htk->hvk", _cat([Us, vs]), _cat([q["bt"], q["kt"]]))) * q["cT"]
    return Y, U, inv, S1


def _wkv_chunk_bwd(S0, Hin, Q, r, lw, k, v, kkr, a, U, inv, dY):
    H, T, _ = r.shape
    low_i, low_s = _tri_masks(T)
    q = _wkv_prep(r, lw, k, kkr, a)
    L, kt, bt = q["L"], q["kt"], q["bt"]
    R = _cat([bt, kt])
    Hh = Hin * q["cT"]
    Hs, S0s, dYs, vs, Us = _sp(Hh), _sp(S0), _sp(dY), _sp(v), _sp(U)
    RH = _bmm("htk,hvk->htv", R, Hs)
    Z = _bmm("hst,hsv->htv", _sp(inv), _sp(RH[:, :T] + _bmm("hst,hsv->htv", _sp(q["A_rb"]), dYs)))
    DZ = _cat([dYs, _sp(Z)])
    both = jnp.concatenate([jnp.broadcast_to(low_i[None], (1, T, T)), jnp.broadcast_to(low_s[None], (1, T, T))], axis=1)
    NU = _sp(jnp.where(both, _bmm("htv,hsv->hts", DZ, Us), 0.0))
    NV = _sp(jnp.where(both, _bmm("htv,hsv->hts", DZ, vs), 0.0))
    ra = _bmm("htv,hvk->htk", DZ, S0s) + _bmm("hts,hsk->htk", NU, bt) + _bmm("hts,hsk->htk", NV, kt)
    dr = ra[:, :T] * q["c"]
    da = ra[:, T:] * q["cprev"]
    dv = RH[:, T:] + _bmm("hst,hsv->htv", _sp(q["Mk"]), DZ)
    VH = _bmm("htv,hvk->htk", _cat([vs, Us]), Hs)
    dk = (VH[:, :T] + _bmm("hst,hsk->htk", NV, L)) * q["cinv"]
    db = (VH[:, T:] + _bmm("hst,hsk->htk", NU, L)) * q["cinv"]
    H0 = Hh + _bmm("htv,htk->hvk", DZ, L)
    kk = q["kk"]
    e = r * dr - kk * a * db - k * dk
    f = -kk * da
    tri_i = jnp.broadcast_to(low_i.astype(bf16)[None], (H, T, T))
    tri_s = jnp.broadcast_to(low_s.astype(bf16)[None], (H, T, T))
    dlw = _tri_dot("hst,hsn->htn", tri_i, e) + _tri_dot("hst,hsn->htn", tri_s, f) + Q
    Qn = Q + jnp.sum(e + f, axis=1, keepdims=True)
    dkk = db * a - da
    dasig = db * kk
    proj = jnp.sum(dkk * kk, axis=-1, keepdims=True)
    dkkr = jnp.where(q["nrm"] > 1e-12, dkk - kk * proj, dkk) / q["den"]
    return dr, dlw, dk, dv, dkkr, dasig, H0, Qn


def _heads(ref):
    return jnp.stack([ref[:, h * HEAD:(h + 1) * HEAD] for h in range(RW_HEADS)], axis=0)


def _put_heads(ref, val):
    for h in range(RW_HEADS):
        ref[:, h * HEAD:(h + 1) * HEAD] = val[h]


def _wkv_fwd(r, lw, k, v, kkr, a, r_k, ln_w, ln_b):
    S = r.shape[0]
    H, N, T = RW_HEADS, HEAD, WKV_T
    nc = S // T

    def body(r_ref, lw_ref, k_ref, v_ref, kkr_ref, a_ref, rk_ref, lnw_ref, lnb_ref,
             y_ref, wkv_ref, u_ref, inv_ref, s0_ref, state):
        @pl.when(pl.program_id(0) == 0)
        def _():
            state[...] = jnp.zeros(state.shape, f32)

        S0 = state[...]
        s0_ref[0] = S0
        rr, kk2, vv = _heads(r_ref), _heads(k_ref), _heads(v_ref)
        Y, U, inv, S1 = _wkv_chunk_fwd(S0, rr, _heads(lw_ref), kk2, vv, _heads(kkr_ref), _heads(a_ref))
        state[...] = S1
        wkv_ref[...] = Y
        u_ref[...] = U
        inv_ref[...] = inv
        mean = jnp.mean(Y, axis=-1, keepdims=True)
        var = jnp.mean(jnp.square(Y - mean), axis=-1, keepdims=True)
        yn = (Y - mean) * lax.rsqrt(var + GN_EPS)
        bonus = jnp.sum(rr * kk2 * rk_ref[...], axis=-1, keepdims=True) * vv
        _put_heads(y_ref, yn * lnw_ref[...] + lnb_ref[...] + bonus)

    tok = pl.BlockSpec((T, H * N), lambda i: (i, 0))
    blk = pl.BlockSpec((H, T, N), lambda i: (0, i, 0))
    par = pl.BlockSpec((H, 1, N), lambda i: (0, 0, 0))
    seq = jax.ShapeDtypeStruct((H, S, N), f32)
    return pl.pallas_call(
        body, name="wkv_fwd", grid=(nc,), in_specs=[tok] * 6 + [par] * 3,
        out_specs=[tok, blk, blk, blk, pl.BlockSpec((1, H, N, N), lambda i: (i, 0, 0, 0))],
        out_shape=[jax.ShapeDtypeStruct((S, H * N), f32), seq, seq, seq, jax.ShapeDtypeStruct((nc, H, N, N), f32)],
        scratch_shapes=[pltpu.VMEM((H, N, N), f32)],
        compiler_params=_cparams(("arbitrary",)),
    )(r, lw, k, v, kkr, a, r_k, ln_w, ln_b)


def _wkv_bwd(dy, r, lw, k, v, kkr, a, wkv, U, inv, S0s, r_k, ln_w, ln_b):
    S = r.shape[0]
    H, N, T = RW_HEADS, HEAD, WKV_T
    nc = S // T

    def body(dy_ref, r_ref, lw_ref, k_ref, v_ref, kkr_ref, a_ref, wkv_ref, u_ref, inv_ref, s0_ref,
             rk_ref, lnw_ref, lnb_ref,
             dr_ref, dlw_ref, dk_ref, dv_ref, dkkr_ref, da_ref, drk_ref, dlnw_ref, dlnb_ref, hst, qst):
        @pl.when(pl.program_id(0) == 0)
        def _():
            hst[...] = jnp.zeros(hst.shape, f32)
            qst[...] = jnp.zeros(qst.shape, f32)
            drk_ref[...] = jnp.zeros(drk_ref.shape, f32)
            dlnw_ref[...] = jnp.zeros(dlnw_ref.shape, f32)
            dlnb_ref[...] = jnp.zeros(dlnb_ref.shape, f32)

        dya = _heads(dy_ref)
        rr, kk2, vv, Y = _heads(r_ref), _heads(k_ref), _heads(v_ref), wkv_ref[...]
        rk = rk_ref[...]
        s = jnp.sum(rr * kk2 * rk, axis=-1, keepdims=True)
        ds = jnp.sum(dya * vv, axis=-1, keepdims=True)
        mean = jnp.mean(Y, axis=-1, keepdims=True)
        var = jnp.mean(jnp.square(Y - mean), axis=-1, keepdims=True)
        rstd = lax.rsqrt(var + GN_EPS)
        yn = (Y - mean) * rstd
        dyn = dya * lnw_ref[...]
        dY = rstd * (dyn - jnp.mean(dyn, axis=-1, keepdims=True) - yn * jnp.mean(dyn * yn, axis=-1, keepdims=True))
        drk_ref[...] += jnp.sum(ds * rr * kk2, axis=1, keepdims=True)
        dlnw_ref[...] += jnp.sum(dya * yn, axis=1, keepdims=True)
        dlnb_ref[...] += jnp.sum(dya, axis=1, keepdims=True)
        dr, dlw, dk, dv, dkkr, dasig, H0, Qn = _wkv_chunk_bwd(
            s0_ref[0], hst[...], qst[...], rr, _heads(lw_ref), kk2, vv, _heads(kkr_ref), _heads(a_ref), u_ref[...],
            inv_ref[...], dY)
        hst[...] = H0
        qst[...] = Qn
        _put_heads(dr_ref, dr + ds * kk2 * rk)
        _put_heads(dlw_ref, dlw)
        _put_heads(dk_ref, dk + ds * rr * rk)
        _put_heads(dv_ref, dv + dya * s)
        _put_heads(dkkr_ref, dkkr)
        _put_heads(da_ref, dasig)

    tok = pl.BlockSpec((T, H * N), lambda i: (nc - 1 - i, 0))
    blk = pl.BlockSpec((H, T, N), lambda i: (0, nc - 1 - i, 0))
    par = pl.BlockSpec((H, 1, N), lambda i: (0, 0, 0))
    seq = jax.ShapeDtypeStruct((S, H * N), f32)
    pout = jax.ShapeDtypeStruct((H, 1, N), f32)
    return pl.pallas_call(
        body, name="wkv_bwd", grid=(nc,),
        in_specs=[tok] * 7 + [blk] * 3 + [pl.BlockSpec((1, H, N, N), lambda i: (nc - 1 - i, 0, 0, 0))] + [par] * 3,
        out_specs=[tok] * 6 + [par] * 3,
        out_shape=[seq] * 6 + [pout] * 3,
        scratch_shapes=[pltpu.VMEM((H, N, N), f32), pltpu.VMEM((H, 1, N), f32)],
        compiler_params=_cparams(("arbitrary",)),
    )(dy, r, lw, k, v, kkr, a, wkv, U, inv, S0s, r_k, ln_w, ln_b)


ATTN_G = 8


def _attn_first_mask(S, G):
    nbg = HEADS_PER_GROUP * S // ATTN_BLK
    b = pl.program_id(0) * G + lax.broadcasted_iota(jnp.int32, (G, 1, 1), 0)
    nbs = [S // (d * ATTN_BLK) for _, d in ATTN_PAIRS]
    per = jnp.where(b < nbg, nbs[0], jnp.where(b < 2 * nbg, nbs[1], nbs[2]))
    return jnp.bitwise_and(b, per - 1) == 0


def _attn_norm(x, gain, scale):
    rs = lax.rsqrt(jnp.mean(x * x, axis=-1, keepdims=True) + RMS_EPS)
    return x * rs * (gain * scale), rs


def _attn_scores(qn, kn_c, kn_p, first):
    s_c = jnp.einsum("gqe,gke->gqk", qn.astype(bf16), kn_c.astype(bf16), preferred_element_type=f32)
    s_p = jnp.einsum("gqe,gke->gqk", qn.astype(bf16), kn_p.astype(bf16), preferred_element_type=f32)
    qi = lax.broadcasted_iota(jnp.int32, (1, ATTN_BLK, ATTN_BLK), 1)
    ki = lax.broadcasted_iota(jnp.int32, (1, ATTN_BLK, ATTN_BLK), 2)
    s_c = jnp.where(qi >= ki, s_c, NEG_INF)
    s_p = jnp.where(jnp.logical_and(ki >= qi, jnp.logical_not(first)), s_p, NEG_INF)
    return s_c, s_p


def _attn_fwd(q, k, kp, v, vp, qg, kg, S):
    NB = q.shape[0]
    G = ATTN_G

    def body(q_ref, k_ref, kp_ref, v_ref, vp_ref, qg_ref, kg_ref, o_ref, lse_ref):
        first = _attn_first_mask(S, G)
        qn, _ = _attn_norm(q_ref[...], qg_ref[...], HEAD ** -0.5)
        kn_c, _ = _attn_norm(k_ref[...], kg_ref[...], 1.0)
        kn_p, _ = _attn_norm(kp_ref[...], kg_ref[...], 1.0)
        s_c, s_p = _attn_scores(qn, kn_c, kn_p, first)
        m = jnp.maximum(jnp.max(s_c, axis=-1, keepdims=True), jnp.max(s_p, axis=-1, keepdims=True))
        p_c = jnp.exp(s_c - m)
        p_p = jnp.exp(s_p - m)
        den = jnp.sum(p_c, axis=-1, keepdims=True) + jnp.sum(p_p, axis=-1, keepdims=True)
        inv = 1.0 / den
        o = jnp.einsum("gqk,gke->gqe", (p_c * inv).astype(bf16), v_ref[...].astype(bf16), preferred_element_type=f32)
        o += jnp.einsum("gqk,gke->gqe", (p_p * inv).astype(bf16), vp_ref[...].astype(bf16), preferred_element_type=f32)
        o_ref[...] = o
        lse_ref[...] = jnp.broadcast_to(m + jnp.log(den), o.shape)

    blk = pl.BlockSpec((G, ATTN_BLK, HEAD), lambda i: (i, 0, 0))
    par = pl.BlockSpec((1, 1, HEAD), lambda i: (0, 0, 0))
    shp = jax.ShapeDtypeStruct((NB, ATTN_BLK, HEAD), f32)
    return pl.pallas_call(
        body, name="attn_fwd", grid=(NB // G,), in_specs=[blk] * 5 + [par] * 2, out_specs=[blk, blk],
        out_shape=[shp, shp], compiler_params=_cparams(("arbitrary",)),
    )(q, k, kp, v, vp, qg, kg)


def _attn_bwd(q, k, kp, v, vp, o, lse, do, dlse, qg, kg, S):
    NB = q.shape[0]
    G = ATTN_G

    def norm_bwd(dxn, x, rs, gain, scale):
        xh = x * rs
        dxh = dxn * (gain * scale)
        dx = rs * (dxh - xh * jnp.mean(dxh * xh, axis=-1, keepdims=True))
        dgain = jnp.sum(jnp.sum(dxn * xh * scale, axis=1, keepdims=True), axis=0, keepdims=True)
        return dx, dgain

    def body(q_ref, k_ref, kp_ref, v_ref, vp_ref, o_ref, lse_ref, do_ref, dlse_ref, qg_ref, kg_ref,
             dq_ref, dkc_ref, dkp_ref, dvc_ref, dvp_ref, dqg_ref, dkg_ref):
        @pl.when(pl.program_id(0) == 0)
        def _():
            dqg_ref[...] = jnp.zeros(dqg_ref.shape, f32)
            dkg_ref[...] = jnp.zeros(dkg_ref.shape, f32)

        first = _attn_first_mask(S, G)
        qx, kx, kpx = q_ref[...], k_ref[...], kp_ref[...]
        qg, kg = qg_ref[...], kg_ref[...]
        qn, rq = _attn_norm(qx, qg, HEAD ** -0.5)
        kn_c, rk_c = _attn_norm(kx, kg, 1.0)
        kn_p, rk_p = _attn_norm(kpx, kg, 1.0)
        s_c, s_p = _attn_scores(qn, kn_c, kn_p, first)
        lse = lse_ref[...][:, :, 0:1]
        p_c = jnp.exp(s_c - lse)
        p_p = jnp.exp(s_p - lse)
        dO = do_ref[...]
        dOb = dO.astype(bf16)
        vb, vpb = v_ref[...].astype(bf16), vp_ref[...].astype(bf16)
        dp_c = jnp.einsum("gqe,gke->gqk", dOb, vb, preferred_element_type=f32)
        dp_p = jnp.einsum("gqe,gke->gqk", dOb, vpb, preferred_element_type=f32)
        corr = dlse_ref[...][:, :, 0:1] - jnp.sum(dO * o_ref[...], axis=-1, keepdims=True)
        ds_c = (p_c * (dp_c + corr)).astype(bf16)
        ds_p = (p_p * (dp_p + corr)).astype(bf16)
        qnb, kcb, kpb = qn.astype(bf16), kn_c.astype(bf16), kn_p.astype(bf16)
        dqn = (jnp.einsum("gqk,gke->gqe", ds_c, kcb, preferred_element_type=f32)
               + jnp.einsum("gqk,gke->gqe", ds_p, kpb, preferred_element_type=f32))
        dkn_c = jnp.einsum("gqk,gqe->gke", ds_c, qnb, preferred_element_type=f32)
        dkn_p = jnp.einsum("gqk,gqe->gke", ds_p, qnb, preferred_element_type=f32)
        dvc_ref[...] = jnp.einsum("gqk,gqe->gke", p_c.astype(bf16), dOb, preferred_element_type=f32)
        dvp_ref[...] = jnp.einsum("gqk,gqe->gke", p_p.astype(bf16), dOb, preferred_element_type=f32)
        dq, dqg = norm_bwd(dqn, qx, rq, qg, HEAD ** -0.5)
        dkc, dkg1 = norm_bwd(dkn_c, kx, rk_c, kg, 1.0)
        dkp, dkg2 = norm_bwd(dkn_p, kpx, rk_p, kg, 1.0)
        dq_ref[...] = dq
        dkc_ref[...] = dkc
        dkp_ref[...] = dkp
        dqg_ref[...] += dqg
        dkg_ref[...] += dkg1 + dkg2

    blk = pl.BlockSpec((G, ATTN_BLK, HEAD), lambda i: (i, 0, 0))
    par = pl.BlockSpec((1, 1, HEAD), lambda i: (0, 0, 0))
    shp = jax.ShapeDtypeStruct((NB, ATTN_BLK, HEAD), f32)
    pshp = jax.ShapeDtypeStruct((1, 1, HEAD), f32)
    return pl.pallas_call(
        body, name="attn_bwd", grid=(NB // G,), in_specs=[blk] * 9 + [par] * 2,
        out_specs=[blk] * 5 + [par] * 2, out_shape=[shp] * 5 + [pshp] * 2,
        compiler_params=_cparams(("arbitrary",)),
    )(q, k, kp, v, vp, o, lse, do, dlse, qg, kg)


def _to_blocks(t, S):
    outs = []
    for gi, (_, d) in enumerate(ATTN_PAIRS):
        tg = t[:, gi * 256:(gi + 1) * 256].reshape(S // d, d, HEADS_PER_GROUP, HEAD)
        outs.append(tg.transpose(2, 1, 0, 3).reshape(-1, ATTN_BLK, HEAD))
    return jnp.concatenate(outs, axis=0)


def _from_blocks(b, S):
    nbg = HEADS_PER_GROUP * S // ATTN_BLK
    outs = []
    for gi, (_, d) in enumerate(ATTN_PAIRS):
        bg = b[gi * nbg:(gi + 1) * nbg].reshape(HEADS_PER_GROUP, d, S // d, HEAD)
        outs.append(bg.transpose(2, 1, 0, 3).reshape(S, HEADS_PER_GROUP * HEAD))
    return jnp.concatenate(outs, axis=1)


def _prev_block(b):
    return jnp.concatenate([jnp.zeros_like(b[:1]), b[:-1]], axis=0)


def _next_block(b):
    return jnp.concatenate([b[1:], jnp.zeros_like(b[:1])], axis=0)


def _shift_down(t):
    return jnp.concatenate([jnp.zeros_like(t[:1]), t[:-1]], axis=0)


def _shift_up(t):
    return jnp.concatenate([t[1:], jnp.zeros_like(t[:1])], axis=0)


def _rms(x, g):
    rs = lax.rsqrt(jnp.mean(x * x, axis=-1, keepdims=True) + RMS_EPS)
    return x * rs * g


def _f_rms(x, g):
    return _rms(x, g)


def _f_resid_rms(coef, x, f, g):
    xn = x + coef * f
    return xn, _rms(xn, g)


def _f_swiglu(u):
    gate, up = u[:, :D_FF], u[:, D_FF:]
    return gate * jax.nn.sigmoid(gate) * up


def _f_swiglu_bwd(dact, u):
    gate, up = u[:, :D_FF], u[:, D_FF:]
    sg = jax.nn.sigmoid(gate)
    silu = gate * sg
    dact = 0.5 * dact
    return jnp.concatenate([dact * up * (sg * (1.0 + gate * (1.0 - sg))), dact * silu], axis=1)


def _f_rms_bwd(n_parts, *args):
    dns = args[:n_parts]
    x, dres, g = args[n_parts:]
    dn = dns[0]
    for t in dns[1:]:
        dn = dn + t
    rs = lax.rsqrt(jnp.mean(x * x, axis=-1, keepdims=True) + RMS_EPS)
    xh = x * rs
    dxh = dn * g
    dx = dres + rs * (dxh - xh * jnp.mean(dxh * xh, axis=-1, keepdims=True))
    return dx, dx, jnp.sum(dn * xh, axis=0, keepdims=True)


def _f_loss(x, f, tgt):
    y = x + 0.5 * f
    diff = y - tgt
    part = 0.5 * jnp.sum(jnp.mean(diff * diff, axis=-1, keepdims=True), axis=0, keepdims=True)
    dy = diff * (1.0 / D)
    return dy, dy, jnp.broadcast_to(part, (1, 128))


def _dotb(a, b, dims):
    return lax.dot_general(a.astype(bf16), b.astype(bf16), dims, preferred_element_type=f32)


_NN = (((1,), (0,)), ((), ()))
_NT = (((1,), (1,)), ((), ()))
_TN = (((0,), (0,)), ((), ()))


def _rwkv_pre_core(prkv, prkv_prev, plora, plora_prev, mu_rkv, mu_lora, w0, w2p, a0, a2p, g2p, k_k, k_a):
    xs = prkv + (prkv_prev - prkv) * mu_rkv
    xl = plora + (plora_prev - plora) * mu_lora
    r, k, v = xs[:, :D], xs[:, D:2 * D], xs[:, 2 * D:]
    wd, ad, gd = xl[:, :128], xl[:, 128:256], xl[:, 256:]
    tw = jnp.tanh(wd)
    zw = w0 + _dotb(tw, w2p, _NN)
    sp = jnp.maximum(-zw, 0.0) + jnp.log(1.0 + jnp.exp(-jnp.abs(zw)))
    lw = -jnp.exp(-sp - 0.5)
    a = jax.nn.sigmoid(a0 + _dotb(ad, a2p, _NN))
    sg = jax.nn.sigmoid(gd)
    return dict(r=r, k=k, v=v, tw=tw, zw=zw, lw=lw, a=a, sg=sg, ad=ad)


def _f_rwkv_pre(*args):
    c = _rwkv_pre_core(*args)
    g2p, k_k, k_a = args[10], args[11], args[12]
    g = _dotb(c["sg"], g2p, _NN)
    k, a = c["k"], c["a"]
    return c["r"], c["lw"], k * (1.0 + (a - 1.0) * k_a), c["v"], k * k_k, a, g


def _f_rwkv_pre_bwd(prkv, prkv_prev, plora, plora_prev, dr, dlw, dk2, dv, dkkr, da, dya, yap,
                    mu_rkv, mu_lora, w0, w2p, a0, a2p, g2p, k_k, k_a):
    c = _rwkv_pre_core(prkv, prkv_prev, plora, plora_prev, mu_rkv, mu_lora, w0, w2p, a0, a2p, g2p, k_k, k_a)
    k, a, sg, tw, zw, lw = c["k"], c["a"], c["sg"], c["tw"], c["zw"], c["lw"]
    dg = dya * yap
    dsg = _dotb(dg, g2p, _NT)
    dgd = dsg * sg * (1.0 - sg)
    dg2p = _dotb(sg, dg, _TN)
    dk = dk2 * (1.0 + (a - 1.0) * k_a) + dkkr * k_k
    da_t = da + dk2 * k * k_a
    dk_a = jnp.sum(dk2 * k * (a - 1.0), axis=0, keepdims=True)
    dk_k = jnp.sum(dkkr * k, axis=0, keepdims=True)
    dza = da_t * a * (1.0 - a)
    da0 = jnp.sum(dza, axis=0, keepdims=True)
    dad = _dotb(dza, a2p, _NT)
    da2p = _dotb(c["ad"], dza, _TN)
    dzw = dlw * lw * jax.nn.sigmoid(-zw)
    dw0 = jnp.sum(dzw, axis=0, keepdims=True)
    dtw = _dotb(dzw, w2p, _NT)
    dw2p = _dotb(tw, dzw, _TN)
    dwd = dtw * (1.0 - tw * tw)
    dxs = jnp.concatenate([dr, dk, dv], axis=1)
    dxl = jnp.concatenate([dwd, dad, dgd], axis=1)
    dmu_rkv = jnp.sum(dxs * (prkv_prev - prkv), axis=0, keepdims=True)
    dmu_lora = jnp.sum(dxl * (plora_prev - plora), axis=0, keepdims=True)
    return (dxs * (1.0 - mu_rkv), dxs * mu_rkv, dxl * (1.0 - mu_lora), dxl * mu_lora,
            dmu_rkv, dmu_lora, dw0, da0, dk_k, dk_a, dw2p, da2p, dg2p)


def _group_alpha(lse):
    l0, l1, l2 = lse[:, :256], lse[:, 256:512], lse[:, 512:]
    m = jnp.maximum(jnp.maximum(l0, l1), l2)
    e0, e1, e2 = jnp.exp(l0 - m), jnp.exp(l1 - m), jnp.exp(l2 - m)
    inv = 1.0 / (e0 + e1 + e2)
    return jnp.concatenate([e0 * inv, e1 * inv, e2 * inv], axis=1)


def _f_combine(o, lse):
    return o * _group_alpha(lse)


def _f_combine_bwd(dyb, o, lse, bd):
    alpha = _group_alpha(lse)
    e = jnp.dot(dyb * o, bd, precision=HI, preferred_element_type=f32)
    ae = alpha * e
    tot = ae[:, :256] + ae[:, 256:512] + ae[:, 512:]
    return dyb * alpha, ae - alpha * jnp.concatenate([tot, tot, tot], axis=1)


def _f_merge(pgate, ta, tb, b_gate):
    gate = jax.nn.sigmoid(pgate + b_gate)
    return gate[:, :D] * ta + gate[:, D:] * tb


def _f_merge_bwd(dm, pgate, ta, tb, b_gate):
    gate = jax.nn.sigmoid(pgate + b_gate)
    ga, gb = gate[:, :D], gate[:, D:]
    dpg = jnp.concatenate([dm * ta * ga * (1.0 - ga), dm * tb * gb * (1.0 - gb)], axis=1)
    return dm * ga, dm * gb, dpg, jnp.sum(dpg, axis=0, keepdims=True)


def _f_mul(a, b):
    return a * b


def _f_adamw(w, g, m, v):
    m2 = ADAM_B1 * m + (1.0 - ADAM_B1) * g
    v2 = ADAM_B2 * v + (1.0 - ADAM_B2) * jnp.square(g)
    m_hat = m2 / (1.0 - ADAM_B1 ** ADAM_STEP)
    v_hat = v2 / (1.0 - ADAM_B2 ** ADAM_STEP)
    delta = -ADAM_LR * (m_hat / (jnp.sqrt(v_hat) + ADAM_EPS) + ADAM_WD * w)
    return delta, m2, v2


def _ffn_fwd(tag, n, WiT, Wo):
    S = n.shape[0]
    u = _mm(f"{tag}_up", n, WiT, "nt")
    (act,) = _rowwise(f"{tag}_swiglu", _f_swiglu, [u], [], [(D_FF, bf16)])
    f = _mm(f"{tag}_down", act, Wo, "nn")
    return u, act, f


def _ffn_bwd(tag, dxo, dxo_b, x_in, n, u, act, g, WiT, Wo):
    dact = _mm(f"{tag}_dact", dxo_b, Wo, "nt")
    dWo = _mm(f"{tag}_dwo", act, dxo_b, "tn", out_dtype=GRAD_WIRE, scale=0.5)
    (du,) = _rowwise(f"{tag}_dswiglu", _f_swiglu_bwd, [dact, u], [], [(2 * D_FF, bf16)], tm=128)
    dn = _mm(f"{tag}_dn", du, WiT, "nn")
    dWiT = _mm(f"{tag}_dwi", du, n, "tn", out_dtype=GRAD_WIRE)
    dx, dx_b, dg = _rowwise(f"{tag}_drms", functools.partial(_f_rms_bwd, 1), [dn, x_in, dxo], [g],
                            [(D, f32), (D, bf16)], [(1, D)])
    return dx, dx_b, dg, dWiT, dWo


def _local_step(x0, tgt, W, P):
    S = x0.shape[0]
    (n1,) = _rowwise("f1_rms", _f_rms, [x0], [P["ffn1_norm"]], [(D, bf16)])
    u1, act1, f1 = _ffn_fwd("f1", n1, W["f1_iT"], W["f1_o"])
    x1, h = _rowwise("mix_rms", functools.partial(_f_resid_rms, 0.5), [x0, f1], [P["mix_norm"]],
                     [(D, f32), (D, bf16)])
    prkv = _mm("p_rkv", h, W["in_rkvT"], "nt")
    plora = _mm("p_lora", h, W["in_loraT"], "nt")
    pqkv = _mm("p_qkv", h, W["in_qkvT"], "nt")
    pgate = _mm("p_gate", h, W["in_gateT"], "nt")
    prkv_prev, plora_prev = _shift_down(prkv), _shift_down(plora)
    pre_params = [P["mu_rkv"], P["mu_lora"], P["w0"], W["w2p"], P["a0"], W["a2p"], W["g2p"], P["k_k"], P["k_a"]]
    r, lw, k2, v, kkr, a, g = _rowwise("rwkv_pre", _f_rwkv_pre, [prkv, prkv_prev, plora, plora_prev], pre_params,
                                       [(D, f32)] * 7, tm=128)
    hm = [r, lw, k2, v, kkr, a]
    hp = [P["r_k"].reshape(RW_HEADS, 1, HEAD), P["ln_w"].reshape(RW_HEADS, 1, HEAD), P["ln_b"].reshape(RW_HEADS, 1, HEAD)]
    yap, wkv_h, U_h, inv_h, S0s = _wkv_fwd(*hm, *hp)
    (ya,) = _rowwise("ya_gate", _f_mul, [yap, g], [], [(D, bf16)])
    ta = _mm("proj_a", ya, W["pr"], "nn")
    qb, kb, vb = (_to_blocks(pqkv[:, i * ATTN_W:(i + 1) * ATTN_W], S) for i in range(3))
    kpb, vpb = _prev_block(kb), _prev_block(vb)
    qg, kg = P["q_norm"].reshape(1, 1, HEAD), P["k_norm"].reshape(1, 1, HEAD)
    o_b, lse_b = _attn_fwd(qb, kb, kpb, vb, vpb, qg, kg, S)
    o_t, lse_t = _from_blocks(o_b, S), _from_blocks(lse_b, S)
    (yb,) = _rowwise("attn_combine", _f_combine, [o_t, lse_t], [], [(ATTN_W, bf16)])
    tb = _mm("proj_b", yb, W["paT"], "nt")
    (merged,) = _rowwise("merge", _f_merge, [pgate, ta, tb], [P["b_gate"]], [(D, bf16)])
    mo = _mm("mix_out", merged, W["out"], "nn")
    x2, n2 = _rowwise("f2_rms", functools.partial(_f_resid_rms, 1.0), [x1, mo], [P["ffn2_norm"]],
                      [(D, f32), (D, bf16)])
    u2, act2, f2 = _ffn_fwd("f2", n2, W["f2_iT"], W["f2_o"])
    dx3, dx3_b, loss = _rowwise("loss", _f_loss, [x2, f2, tgt], [], [(D, f32), (D, bf16)], [(1, 128)])
    G, Gs = {}, {}
    dx2, dx2_b, Gs["ffn2_norm"], G["f2_iT"], G["f2_o"] = _ffn_bwd("f2", dx3, dx3_b, x2, n2, u2, act2, P["ffn2_norm"],
                                                                 W["f2_iT"], W["f2_o"])
    dmerged = _mm("d_merged", dx2_b, W["out"], "nt")
    G["out"] = _mm("dw_out", merged, dx2_b, "tn", out_dtype=GRAD_WIRE)
    dta, dtb, dpgate, Gs["b_gate"] = _rowwise("merge_bwd", _f_merge_bwd, [dmerged, pgate, ta, tb], [P["b_gate"]],
                                              [(D, bf16), (D, bf16), (2 * D, bf16)], [(1, 2 * D)])
    dya = _mm("d_ya", dta, W["pr"], "nt")
    G["pr"] = _mm("dw_pr", ya, dta, "tn", out_dtype=GRAD_WIRE)
    dyb = _mm("d_yb", dtb, W["paT"], "nn")
    G["paT"] = _mm("dw_pa", dtb, yb, "tn", out_dtype=GRAD_WIRE)
    (dyap,) = _rowwise("ya_gate_bwd", _f_mul, [dya, g], [], [(D, f32)])
    hg = _wkv_bwd(dyap, *hm, wkv_h, U_h, inv_h, S0s, *hp)
    dr, dlw, dk2, dv, dkkr, da = hg[:6]
    Gs["r_k"], Gs["ln_w"], Gs["ln_b"] = (t.reshape(1, D) for t in hg[6:])
    lp = sum(LORA_PAD)
    (dprkv_a, dprkv_b, dplora_a, dplora_b, Gs["mu_rkv"], Gs["mu_lora"], Gs["w0"], Gs["a0"], Gs["k_k"], Gs["k_a"],
     dw2p, da2p, dg2p) = _rowwise(
        "rwkv_pre_bwd", _f_rwkv_pre_bwd,
        [prkv, prkv_prev, plora, plora_prev, dr, dlw, dk2, dv, dkkr, da, dya, yap], pre_params,
        [(3 * D, f32), (3 * D, f32), (lp, f32), (lp, f32)],
        [(1, 3 * D), (1, lp), (1, D), (1, D), (1, D), (1, D), (LORA_PAD[0], D), (LORA_PAD[1], D), (LORA_PAD[2], D)],
        tm=128)
    G["w2T"], G["a2T"], G["g2T"] = dw2p[:LORA_W[0]].T, da2p[:LORA_W[1]].T, dg2p[:LORA_W[2]].T
    dprkv = (dprkv_a + _shift_up(dprkv_b)).astype(bf16)
    dplora = (dplora_a + _shift_up(dplora_b)).astype(bf16)
    bd = (jnp.arange(ATTN_W)[:, None] // HEAD == jnp.arange(ATTN_W)[None, :] // HEAD).astype(f32)
    do_t, dlse_t = _rowwise("attn_combine_bwd", _f_combine_bwd, [dyb, o_t, lse_t], [bd], [(ATTN_W, f32)] * 2)
    dq_b, dkc_b, dkp_b, dvc_b, dvp_b, dqg, dkg = _attn_bwd(qb, kb, kpb, vb, vpb, o_b, lse_b, _to_blocks(do_t, S),
                                                            _to_blocks(dlse_t, S), qg, kg, S)
    Gs["q_norm"], Gs["k_norm"] = dqg.reshape(1, HEAD), dkg.reshape(1, HEAD)
    dpqkv = jnp.concatenate([_from_blocks(dq_b, S), _from_blocks(dkc_b + _next_block(dkp_b), S),
                             _from_blocks(dvc_b + _next_block(dvp_b), S)], axis=1).astype(bf16)
    dh = [_mm("dh_rkv", dprkv, W["in_rkvT"], "nn"), _mm("dh_lora", dplora, W["in_loraT"], "nn"),
          _mm("dh_qkv", dpqkv, W["in_qkvT"], "nn"), _mm("dh_gate", dpgate, W["in_gateT"], "nn")]
    dW_rkv = _mm("dw_rkv", dprkv, h, "tn", out_dtype=GRAD_WIRE)
    dW_lora = _mm("dw_lora", dplora, h, "tn", out_dtype=GRAD_WIRE)
    dW_qkv = _mm("dw_qkv", dpqkv, h, "tn", out_dtype=GRAD_WIRE)
    dW_gate = _mm("dw_gate", dpgate, h, "tn", out_dtype=GRAD_WIRE)
    o1, o2 = LORA_PAD[0], LORA_PAD[0] + LORA_PAD[1]
    G["inT"] = jnp.concatenate([dW_rkv, dW_lora[:LORA_W[0]], dW_lora[o1:o1 + LORA_W[1]], dW_lora[o2:o2 + LORA_W[2]],
                                dW_qkv, dW_gate], axis=0)
    dx1, dx1_b, Gs["mix_norm"] = _rowwise("mix_drms", functools.partial(_f_rms_bwd, 4), [*dh, x1, dx2],
                                          [P["mix_norm"]], [(D, f32), (D, bf16)], [(1, D)])
    dx0, _, Gs["ffn1_norm"], G["f1_iT"], G["f1_o"] = _ffn_bwd("f1", dx1, dx1_b, x0, n1, u1, act1, P["ffn1_norm"],
                                                             W["f1_iT"], W["f1_o"])
    return loss[0, 0], dx0, G, Gs


def _peer(k):
    x, y, c = lax.axis_index("x"), lax.axis_index("y"), lax.axis_index("c")
    px = 1 - x if k & 4 else x
    py = 1 - y if k & 2 else y
    pc = 1 - c if k & 1 else c
    return (px, py, pc), 4 * px + 2 * py + pc


def _all_gather(pack):
    R, C = pack.shape

    def body(x_ref, out_ref, send_sems, recv_sems, local_sem):
        x, y, c = lax.axis_index("x"), lax.axis_index("y"), lax.axis_index("c")
        me, sibling = (x, y, c), (x, y, 1 - c)
        chips = [(1 - x, y), (x, 1 - y), (1 - x, 1 - y)]

        def slot(px, py, pc):
            return out_ref.at[4 * px + 2 * py + pc]

        def copy(k, block, to, src=None):
            return pltpu.make_async_remote_copy(
                src_ref=slot(*block) if src is None else src, dst_ref=slot(*block), send_sem=send_sems.at[k],
                recv_sem=recv_sems.at[k], device_id=to, device_id_type=MESH)

        mine = pltpu.make_async_copy(x_ref, slot(*me), local_sem)
        mine.start()
        first = [copy(0, me, sibling, src=x_ref)]
        first += [copy(1 + j, me, (*chip, c), src=x_ref) for j, chip in enumerate(chips)]
        for cp in first:
            cp.start()
        passed = [copy(4 + j, (*chip, c), sibling) for j, chip in enumerate(chips)]
        for j, chip in enumerate(chips):
            copy(1 + j, (*chip, c), me).wait_recv()
            passed[j].start()
        copy(0, sibling, me).wait_recv()
        for j, chip in enumerate(chips):
            copy(4 + j, (*chip, 1 - c), me).wait_recv()
        for cp in first + passed:
            cp.wait_send()
        mine.wait()

    return pl.pallas_call(
        body, name="weight_all_gather", out_shape=jax.ShapeDtypeStruct((N_DEV, R, C), pack.dtype),
        in_specs=[pl.BlockSpec(memory_space=pl.ANY)], out_specs=pl.BlockSpec(memory_space=pl.ANY),
        scratch_shapes=[pltpu.SemaphoreType.DMA((N_DEV - 1,)), pltpu.SemaphoreType.DMA((N_DEV - 1,)),
                        pltpu.SemaphoreType.DMA(())],
    )(pack)


N_CHIP = 4


def _grad_pair(pieces):
    n = len(pieces)
    C = pieces[0].shape[2]
    rows = [p.shape[1] for p in pieces]
    offs = [sum(rows[:i]) for i in range(n)]
    R = sum(rows)

    def body(*refs):
        g_refs, (other_ref, send_sems, recv_sems) = refs[:n], refs[n:]
        x, y, c = lax.axis_index("x"), lax.axis_index("y"), lax.axis_index("c")
        copies = []
        for i, g_ref in enumerate(g_refs):
            for k in range(N_CHIP):
                cp = pltpu.make_async_remote_copy(
                    src_ref=g_ref.at[4 * (k // 2) + 2 * (k % 2) + 1 - c], dst_ref=other_ref.at[k, pl.ds(offs[i], rows[i])],
                    send_sem=send_sems.at[i * N_CHIP + k], recv_sem=recv_sems.at[i * N_CHIP + k],
                    device_id=(x, y, 1 - c), device_id_type=MESH)
                cp.start()
                copies.append(cp)
        for cp in copies:
            cp.wait()

    return pl.pallas_call(
        body, name="grad_pair", out_shape=jax.ShapeDtypeStruct((N_CHIP, R, C), pieces[0].dtype),
        in_specs=[pl.BlockSpec(memory_space=pl.ANY)] * n, out_specs=pl.BlockSpec(memory_space=pl.ANY),
        scratch_shapes=[pltpu.SemaphoreType.DMA((n * N_CHIP,))] * 2,
    )(*pieces)


def _pair_add(pieces, other, c):
    n = len(pieces)
    C = pieces[0].shape[2]
    nblk = [p.shape[1] // PACK_BLOCK for p in pieces]
    lo = [sum(nblk[:i]) for i in range(n)]
    R = sum(nblk) * PACK_BLOCK

    def body(c_ref, *refs):
        g_refs, o_ref, out_ref = refs[:n], refs[n], refs[n + 1]
        rb = pl.program_id(1)
        for i in range(n):
            @pl.when(jnp.logical_and(rb >= lo[i], rb < lo[i] + nblk[i]))
            def _(g_ref=g_refs[i]):
                out_ref[...] = (g_ref[...].astype(f32) + o_ref[...].astype(f32)).astype(out_ref.dtype)

    def piece_spec(i):
        return pl.BlockSpec((1, None, PACK_BLOCK, C),
                            lambda k, rb, c_ref: (k, c_ref[0], jnp.clip(rb - lo[i], 0, nblk[i] - 1), 0))

    blk = pl.BlockSpec((1, PACK_BLOCK, C), lambda k, rb, c_ref: (k, rb, 0))
    return pl.pallas_call(
        body, name="pair_add",
        grid_spec=pltpu.PrefetchScalarGridSpec(
            num_scalar_prefetch=1, grid=(N_CHIP, R // PACK_BLOCK),
            in_specs=[piece_spec(i) for i in range(n)] + [blk], out_specs=blk),
        out_shape=jax.ShapeDtypeStruct((N_CHIP, R, C), other.dtype),
        compiler_params=_cparams(("arbitrary", "arbitrary")),
    )(c, *[p.reshape(N_CHIP, 2, p.shape[1], C) for p in pieces], other)


def _grad_cross(part):
    _, R, C = part.shape

    def body(p_ref, out_ref, send_sems, recv_sems):
        x, y, c = lax.axis_index("x"), lax.axis_index("y"), lax.axis_index("c")
        flips = [(1, 0), (0, 1), (1, 1)]
        copies = []
        for j, (fx, fy) in enumerate(flips):
            px = 1 - x if fx else x
            py = 1 - y if fy else y
            cp = pltpu.make_async_remote_copy(src_ref=p_ref.at[2 * px + py], dst_ref=out_ref.at[j],
                                              send_sem=send_sems.at[j], recv_sem=recv_sems.at[j],
                                              device_id=(px, py, c), device_id_type=MESH)
            cp.start()
            copies.append(cp)
        for cp in copies:
            cp.wait()

    return pl.pallas_call(
        body, name="grad_cross", out_shape=jax.ShapeDtypeStruct((3, R, C), part.dtype),
        in_specs=[pl.BlockSpec(memory_space=pl.ANY)], out_specs=pl.BlockSpec(memory_space=pl.ANY),
        scratch_shapes=[pltpu.SemaphoreType.DMA((3,)), pltpu.SemaphoreType.DMA((3,))],
    )(part)


def _grad_sum(part, recv, my_chip, tr):
    _, R, C = part.shape

    def body(chip_ref, p_ref, r_ref, o_ref):
        acc = p_ref[0].astype(f32)
        for j in range(3):
            acc = acc + r_ref[j].astype(f32)
        o_ref[...] = acc

    return pl.pallas_call(
        body, name="grad_sum",
        grid_spec=pltpu.PrefetchScalarGridSpec(
            num_scalar_prefetch=1, grid=(R // tr,),
            in_specs=[pl.BlockSpec((1, tr, C), lambda i, chip_ref: (chip_ref[0], i, 0)),
                      pl.BlockSpec((3, tr, C), lambda i, chip_ref: (0, i, 0))],
            out_specs=pl.BlockSpec((tr, C), lambda i, chip_ref: (i, 0))),
        out_shape=jax.ShapeDtypeStruct((R, C), f32),
        compiler_params=_cparams(("arbitrary",)),
    )(my_chip, part, recv)


def _small_all_reduce(small):
    R, C = small.shape

    def body(x_ref, o_ref, buf, send_sems, recv_sems):
        _, me = _peer(0)
        buf[me] = x_ref[...]
        sends = []
        for k in range(1, N_DEV):
            dev, _ = _peer(k)
            cp = pltpu.make_async_remote_copy(src_ref=x_ref, dst_ref=buf.at[me], send_sem=send_sems.at[k - 1],
                                              recv_sem=recv_sems.at[k - 1], device_id=dev, device_id_type=MESH)
            cp.start()
            sends.append(cp)
        for k in range(1, N_DEV):
            dev, idx = _peer(k)
            pltpu.make_async_remote_copy(src_ref=x_ref, dst_ref=buf.at[idx], send_sem=send_sems.at[k - 1],
                                         recv_sem=recv_sems.at[k - 1], device_id=dev, device_id_type=MESH).wait_recv()
        for cp in sends:
            cp.wait_send()
        acc = buf[0]
        for i in range(1, N_DEV):
            acc = acc + buf[i]
        o_ref[...] = acc

    return pl.pallas_call(
        body, name="small_all_reduce", out_shape=jax.ShapeDtypeStruct((R, C), f32),
        in_specs=[pl.BlockSpec(memory_space=pltpu.VMEM)], out_specs=pl.BlockSpec(memory_space=pltpu.VMEM),
        scratch_shapes=[pltpu.VMEM((N_DEV, R, C), f32), pltpu.SemaphoreType.DMA((N_DEV - 1,)),
                        pltpu.SemaphoreType.DMA((N_DEV - 1,))],
    )(small)


_BIG_GROUPS = (
    (("ffn1_w_in", True),), (("ffn1_w_out", False),), (("w_in", True),),
    (("rwkv_w2", True), ("rwkv_a2", True), ("rwkv_g2", True)),
    (("w_proj_rwkv", False),), (("w_proj_attn", True),), (("w_out", False),),
    (("ffn2_w_in", True),), (("ffn2_w_out", False),))
_BIG = tuple(item for group in _BIG_GROUPS for item in group)
_SMALL = ("ffn1_norm", "mix_norm", "b_gate", "rwkv_mu", "rwkv_w0", "rwkv_a0", "rwkv_k_k", "rwkv_k_a", "rwkv_r_k",
          "rwkv_ln_w", "rwkv_ln_b", "attn_q_norm", "attn_k_norm", "ffn2_norm")


def _pack_layout(like):
    items, groups, off = {}, [], 0
    for group in _BIG_GROUPS:
        start = off
        for name, _ in group:
            shp = like[name].shape
            n = shp[0] * shp[1] // D
            items[name] = (off, n)
            off += n
        off = -(-off // PACK_BLOCK) * PACK_BLOCK
        groups.append((start, off - start))
    assert off == PACK_ROWS, off
    return items, groups


def _pack_big(shards):
    items, _ = _pack_layout(shards)
    parts, at = [], 0
    for name, tr in _BIG:
        off, n = items[name]
        t = shards[name]
        if off > at:
            parts.append(jnp.zeros((off - at, D), t.dtype))
        parts.append((t.T if tr else t).reshape(n, D))
        at = off + n
    parts.append(jnp.zeros((PACK_ROWS - at, D), parts[0].dtype))
    return jnp.concatenate(parts, axis=0)


def _unpack_big(pack, like):
    items, _ = _pack_layout(like)
    out = {}
    for name, tr in _BIG:
        off, n = items[name]
        shp = like[name].shape
        t = pack[off:off + n]
        out[name] = t.reshape(shp[1], shp[0]).T if tr else t.reshape(shp)
    return out


def _small_rows(name, t):
    flat = t.reshape(-1)
    pad = (-flat.shape[0]) % D
    return jnp.pad(flat, (0, pad)).reshape(-1, D)


def _pack_small(vals):
    parts = [_small_rows(n, vals[n]) for n in _SMALL]
    used = sum(p.shape[0] for p in parts)
    parts.append(jnp.zeros((SMALL_ROWS - used, D), f32))
    return jnp.concatenate(parts, axis=0)


def _unpack_small(pack, like):
    out, off = {}, 0
    for n in _SMALL:
        size = like[n].size
        rows = -(-size // D)
        out[n] = pack[off:off + rows].reshape(-1)[:size].reshape(like[n].shape)
        off += rows
    return out


def _pad_rows(t, n):
    return jnp.concatenate([t, jnp.zeros((n - t.shape[0],) + t.shape[1:], t.dtype)], axis=0)


def _build_W(full):
    inT = full["w_in"]
    z64, z96 = jnp.zeros((64, D), inT.dtype), jnp.zeros((96, D), inT.dtype)
    return {
        "f1_iT": full["ffn1_w_in"], "f1_o": full["ffn1_w_out"], "f2_iT": full["ffn2_w_in"], "f2_o": full["ffn2_w_out"],
        "in_rkvT": inT[:3 * D],
        "in_loraT": jnp.concatenate([inT[3072:3136], z64, inT[3136:3200], z64, inT[3200:3360], z96], axis=0),
        "in_qkvT": inT[3360:3360 + 3 * ATTN_W], "in_gateT": inT[3360 + 3 * ATTN_W:],
        "w2p": jnp.concatenate([full["rwkv_w2"].T, z64], axis=0),
        "a2p": jnp.concatenate([full["rwkv_a2"].T, z64], axis=0),
        "g2p": jnp.concatenate([full["rwkv_g2"].T, z96], axis=0),
        "pr": full["w_proj_rwkv"], "paT": full["w_proj_attn"], "out": full["w_out"],
    }


def _build_P(Wl):
    mu = Wl["rwkv_mu"]
    z64f, z96f = jnp.zeros((1, 64), f32), jnp.zeros((1, 96), f32)
    return {
        "ffn1_norm": Wl["ffn1_norm"][None], "mix_norm": Wl["mix_norm"][None], "ffn2_norm": Wl["ffn2_norm"][None],
        "b_gate": Wl["b_gate"][None], "mu_rkv": mu[None, :3 * D],
        "mu_lora": jnp.concatenate([mu[None, 3072:3136], z64f, mu[None, 3136:3200], z64f, mu[None, 3200:3360], z96f], axis=1),
        "w0": Wl["rwkv_w0"][None], "a0": Wl["rwkv_a0"][None], "k_k": Wl["rwkv_k_k"][None], "k_a": Wl["rwkv_k_a"][None],
        "r_k": Wl["rwkv_r_k"].reshape(1, D), "ln_w": Wl["rwkv_ln_w"][None], "ln_b": Wl["rwkv_ln_b"][None],
        "q_norm": Wl["attn_q_norm"][None], "k_norm": Wl["attn_k_norm"][None],
    }


def kernel(x, ffn1_norm, ffn1_w_in, ffn1_w_out, mix_norm, w_in, b_gate, rwkv_mu, rwkv_w0, rwkv_w2, rwkv_a0, rwkv_a2, rwkv_g2, rwkv_k_k, rwkv_k_a, rwkv_r_k, rwkv_ln_w, rwkv_ln_b, attn_q_norm, attn_k_norm, w_proj_rwkv, w_proj_attn, w_out, ffn2_norm, ffn2_w_in, ffn2_w_out, loss_target, m_ffn1_norm, m_ffn1_w_in, m_ffn1_w_out, m_mix_norm, m_w_in, m_b_gate, m_rwkv_mu, m_rwkv_w0, m_rwkv_w2, m_rwkv_a0, m_rwkv_a2, m_rwkv_g2, m_rwkv_k_k, m_rwkv_k_a, m_rwkv_r_k, m_rwkv_ln_w, m_rwkv_ln_b, m_attn_q_norm, m_attn_k_norm, m_w_proj_rwkv, m_w_proj_attn, m_w_out, m_ffn2_norm, m_ffn2_w_in, m_ffn2_w_out, v_ffn1_norm, v_ffn1_w_in, v_ffn1_w_out, v_mix_norm, v_w_in, v_b_gate, v_rwkv_mu, v_rwkv_w0, v_rwkv_w2, v_rwkv_a0, v_rwkv_a2, v_rwkv_g2, v_rwkv_k_k, v_rwkv_k_a, v_rwkv_r_k, v_rwkv_ln_w, v_rwkv_ln_b, v_attn_q_norm, v_attn_k_norm, v_w_proj_rwkv, v_w_proj_attn, v_w_out, v_ffn2_norm, v_ffn2_w_in, v_ffn2_w_out):
    names = ("ffn1_norm", "ffn1_w_in", "ffn1_w_out", "mix_norm", "w_in", "b_gate", "rwkv_mu", "rwkv_w0", "rwkv_w2",
             "rwkv_a0", "rwkv_a2", "rwkv_g2", "rwkv_k_k", "rwkv_k_a", "rwkv_r_k", "rwkv_ln_w", "rwkv_ln_b",
             "attn_q_norm", "attn_k_norm", "w_proj_rwkv", "w_proj_attn", "w_out", "ffn2_norm", "ffn2_w_in", "ffn2_w_out")
    w_all = (ffn1_norm, ffn1_w_in, ffn1_w_out, mix_norm, w_in, b_gate, rwkv_mu, rwkv_w0, rwkv_w2, rwkv_a0, rwkv_a2,
             rwkv_g2, rwkv_k_k, rwkv_k_a, rwkv_r_k, rwkv_ln_w, rwkv_ln_b, attn_q_norm, attn_k_norm, w_proj_rwkv,
             w_proj_attn, w_out, ffn2_norm, ffn2_w_in, ffn2_w_out)
    m_all = (m_ffn1_norm, m_ffn1_w_in, m_ffn1_w_out, m_mix_norm, m_w_in, m_b_gate, m_rwkv_mu, m_rwkv_w0, m_rwkv_w2,
             m_rwkv_a0, m_rwkv_a2, m_rwkv_g2, m_rwkv_k_k, m_rwkv_k_a, m_rwkv_r_k, m_rwkv_ln_w, m_rwkv_ln_b,
             m_attn_q_norm, m_attn_k_norm, m_w_proj_rwkv, m_w_proj_attn, m_w_out, m_ffn2_norm, m_ffn2_w_in, m_ffn2_w_out)
    v_all = (v_ffn1_norm, v_ffn1_w_in, v_ffn1_w_out, v_mix_norm, v_w_in, v_b_gate, v_rwkv_mu, v_rwkv_w0, v_rwkv_w2,
             v_rwkv_a0, v_rwkv_a2, v_rwkv_g2, v_rwkv_k_k, v_rwkv_k_a, v_rwkv_r_k, v_rwkv_ln_w, v_rwkv_ln_b,
             v_attn_q_norm, v_attn_k_norm, v_w_proj_rwkv, v_w_proj_attn, v_w_out, v_ffn2_norm, v_ffn2_w_in, v_ffn2_w_out)
    Wl = {n: t[0] for n, t in zip(names, w_all)}
    Ml = {n: t[0] for n, t in zip(names, m_all)}
    Vl = {n: t[0] for n, t in zip(names, v_all)}
    big = [n for n, _ in _BIG]

    w_pack = _pack_big({n: Wl[n] for n in big})
    gathered = _all_gather(w_pack.astype(bf16))
    items, groups = _pack_layout(Wl)
    full = {}
    for n, tr in _BIG:
        shp = Wl[n].shape
        off, rows = items[n]
        t = gathered[:, off:off + rows]
        r_loc, c_loc = (shp[1], shp[0]) if tr else shp
        full[n] = t.reshape(N_DEV * r_loc, c_loc)
    W, P = _build_W(full), _build_P(Wl)

    loss_local, dx0, G, Gs = _local_step(x[0], loss_target[0], W, P)

    g_full = {"ffn1_w_in": G["f1_iT"], "ffn1_w_out": G["f1_o"], "w_in": G["inT"], "rwkv_w2": G["w2T"],
              "rwkv_a2": G["a2T"], "rwkv_g2": G["g2T"], "w_proj_rwkv": G["pr"], "w_proj_attn": G["paT"],
              "w_out": G["out"], "ffn2_w_in": G["f2_iT"], "ffn2_w_out": G["f2_o"]}
    pieces = []
    for group, (_, rows_pad) in zip(_BIG_GROUPS, groups):
        parts = [g_full[n].astype(GRAD_WIRE).reshape(N_DEV, items[n][1], D) for n, _ in group]
        piece = parts[0] if len(parts) == 1 else jnp.concatenate(parts, axis=1)
        if rows_pad > piece.shape[1]:
            piece = jnp.pad(piece, ((0, 0), (0, rows_pad - piece.shape[1]), (0, 0)))
        pieces.append(piece)
    my_c = lax.axis_index("c").astype(jnp.int32).reshape(1)
    my_chip = (2 * lax.axis_index("x") + lax.axis_index("y")).astype(jnp.int32).reshape(1)
    chip_part = _pair_add(pieces, _grad_pair(pieces), my_c)
    g_pack = _grad_sum(chip_part, _grad_cross(chip_part), my_chip, 256)

    mu_g = Gs["mu_rkv"], Gs["mu_lora"]
    o1, o2 = LORA_PAD[0], LORA_PAD[0] + LORA_PAD[1]
    g_small_local = {
        "ffn1_norm": Gs["ffn1_norm"], "mix_norm": Gs["mix_norm"], "b_gate": Gs["b_gate"],
        "rwkv_mu": jnp.concatenate([mu_g[0], mu_g[1][:, :64], mu_g[1][:, o1:o1 + 64], mu_g[1][:, o2:o2 + 160]], axis=1),
        "rwkv_w0": Gs["w0"], "rwkv_a0": Gs["a0"], "rwkv_k_k": Gs["k_k"], "rwkv_k_a": Gs["k_a"], "rwkv_r_k": Gs["r_k"],
        "rwkv_ln_w": Gs["ln_w"], "rwkv_ln_b": Gs["ln_b"], "attn_q_norm": Gs["q_norm"], "attn_k_norm": Gs["k_norm"],
        "ffn2_norm": Gs["ffn2_norm"]}
    gs_pack = _small_all_reduce(_pack_small(g_small_local))

    d_pack, m_pack, v_pack = _rowwise(
        "adamw_big", _f_adamw, [w_pack, g_pack, _pack_big({n: Ml[n] for n in big}), _pack_big({n: Vl[n] for n in big})],
        [], [(D, f32)] * 3)
    ds_pack, ms_pack, vs_pack = _rowwise(
        "adamw_small", _f_adamw, [_pack_small(Wl), gs_pack, _pack_small(Ml), _pack_small(Vl)], [], [(D, f32)] * 3)

    def unpack(pb, ps):
        d = _unpack_big(pb, Wl)
        d.update(_unpack_small(ps, Wl))
        return [d[n][None] for n in names]

    loss = lax.psum(loss_local, ("x", "y", "c"))
    return (loss, dx0[None], *unpack(g_pack, gs_pack), *unpack(d_pack, ds_pack), *unpack(m_pack, ms_pack),
            *unpack(v_pack, vs_pack))
```

```python
import functools

import jax
import jax.numpy as jnp
from jax import lax
from jax.experimental import pallas as pl
from jax.experimental.pallas import tpu as pltpu

f32 = jnp.float32
bf16 = jnp.bfloat16
HI = lax.Precision.HIGHEST
MESH = pl.DeviceIdType.MESH

N_DEV = 8
D = 1024
D_FF = 2816
HEAD = 64
RW_HEADS = 16
ATTN_PAIRS = ((128, 1), (512, 4), (2048, 16))
ATTN_BLK = 128
HEADS_PER_GROUP = 4
ATTN_W = 768
LORA_PAD = (128, 128, 256)
LORA_W = (64, 64, 160)
GN_EPS = 64e-5
RMS_EPS = 1e-6
NEG_INF = -1e30
WKV_T = 64
GRAD_WIRE = bf16
PACK_BLOCK = 128
PACK_ROWS = 3840
SMALL_ROWS = 24
VMEM_LIMIT = 56 * 1024 * 1024

ADAM_LR, ADAM_B1, ADAM_B2, ADAM_EPS, ADAM_WD, ADAM_STEP = 0.001, 0.9, 0.999, 1e-08, 0.01, 10


def _cparams(sem):
    return pltpu.CompilerParams(dimension_semantics=sem, vmem_limit_bytes=VMEM_LIMIT)


def _pick(n, cands):
    for c in cands:
        if n % c == 0:
            return c
    return n


HALO = 8


def _rowwise(name, fn, rows, params, outs, accs=(), tm=256, halos=(), carries=(), reverse=False):
    S = rows[0].shape[0]
    tm = min(tm, S)
    while S % tm:
        tm -= 8
    nb = S // tm
    n_in = len(rows) + len(params) + len(halos)
    n_out = len(outs)
    n_acc = len(accs)
    n_car = len(carries)

    def blk_of(i):
        return nb - 1 - i if reverse else i

    def body(*refs):
        step = pl.program_id(0)
        carry_refs = refs[n_in + n_out + n_acc:]
        if n_car:
            @pl.when(step == 0)
            def _():
                for c_ref in carry_refs:
                    c_ref[...] = jnp.zeros(c_ref.shape, f32)
        args = [r[...] for r in refs[:n_in]] + [c[...] for c in carry_refs]
        res = fn(*args, blk=blk_of(step)) if (halos or carries) else fn(*args)
        if not isinstance(res, (tuple, list)):
            res = (res,)
        out_refs = refs[n_in:n_in + n_out + n_acc]
        for j in range(n_out):
            out_refs[j][...] = res[j].astype(out_refs[j].dtype)
        if n_acc:
            @pl.when(step == 0)
            def _():
                for j in range(n_acc):
                    out_refs[n_out + j][...] = jnp.zeros(out_refs[n_out + j].shape, f32)
            for j in range(n_acc):
                out_refs[n_out + j][...] += res[n_out + j]
        for j in range(n_car):
            carry_refs[j][...] = res[n_out + n_acc + j]

    in_specs = [pl.BlockSpec((tm, a.shape[1]), lambda i: (blk_of(i), 0)) for a in rows]
    in_specs += [pl.BlockSpec(p.shape, lambda i, nd=p.ndim: (0,) * nd) for p in params]
    in_specs += [pl.BlockSpec((HALO, rows[h].shape[1]), lambda i: (jnp.maximum(blk_of(i) * (tm // HALO) - 1, 0), 0))
                 for h in halos]
    out_specs = [pl.BlockSpec((tm, w), lambda i: (blk_of(i), 0)) for w, _ in outs]
    out_specs += [pl.BlockSpec(s, lambda i: (0, 0)) for s in accs]
    out_shape = [jax.ShapeDtypeStruct((S, w), dt) for w, dt in outs]
    out_shape += [jax.ShapeDtypeStruct(s, f32) for s in accs]
    res = pl.pallas_call(
        body, name=name, grid=(nb,), in_specs=in_specs, out_specs=out_specs, out_shape=out_shape,
        scratch_shapes=[pltpu.VMEM(s, f32) for s in carries],
        compiler_params=_cparams(("arbitrary",)),
    )(*rows, *params, *[rows[h] for h in halos])
    return res


MM_VMEM_BUDGET = 40 * 1024 * 1024
MM_STEP_US = 0.35
MM_FLOPS_PER_US = 9.0e8
MM_HBM_BYTES_PER_US = 3.0e6


def _tile_options(n, cap):
    opts = [d for d in range(128, min(n, cap) + 1, 128) if n % d == 0]
    return opts or [n]


def _mm_tiles(M, N, K, sa, sb, so):
    best, best_cost = None, None
    for tm in _tile_options(M, 2048):
        for tn in _tile_options(N, 2048):
            for tk in _tile_options(K, 4096):
                vmem = 2 * (tm * tk * sa + tk * tn * sb) + 2 * tm * tn * so + (tm * tn * 4 if tk < K else 0)
                if vmem > MM_VMEM_BUDGET:
                    continue
                steps = (M // tm) * (N // tn) * (K // tk)
                traffic = M * K * sa * (N // tn) + K * N * sb * (M // tm) + M * N * so
                cost = (max(2.0 * M * N * K / MM_FLOPS_PER_US, traffic / MM_HBM_BYTES_PER_US) + steps * MM_STEP_US
                        + (tm * tk * sa + tk * tn * sb) / MM_HBM_BYTES_PER_US)
                if best_cost is None or cost < best_cost:
                    best, best_cost = (tm, tn, tk), cost
    return best


def _mm(name, a, b, mode, out_dtype=f32, scale=None):
    if mode == "nn":
        (M, K), (_, N) = a.shape, b.shape
    elif mode == "nt":
        (M, K), (N, _) = a.shape, b.shape
    else:
        (K, M), (_, N) = a.shape, b.shape
    tm, tn, tk = _mm_tiles(M, N, K, a.dtype.itemsize, b.dtype.itemsize, jnp.dtype(out_dtype).itemsize)
    nk = K // tk
    if mode == "nn":
        a_spec = pl.BlockSpec((tm, tk), lambda i, j, k: (i, k))
        b_spec = pl.BlockSpec((tk, tn), lambda i, j, k: (k, j))
        dims = (((1,), (0,)), ((), ()))
    elif mode == "nt":
        a_spec = pl.BlockSpec((tm, tk), lambda i, j, k: (i, k))
        b_spec = pl.BlockSpec((tn, tk), lambda i, j, k: (j, k))
        dims = (((1,), (1,)), ((), ()))
    else:
        a_spec = pl.BlockSpec((tk, tm), lambda i, j, k: (k, i))
        b_spec = pl.BlockSpec((tk, tn), lambda i, j, k: (k, j))
        dims = (((0,), (0,)), ((), ()))

    def finish(acc):
        return acc if scale is None else acc * scale

    def body(a_ref, b_ref, o_ref, *scratch):
        part = lax.dot_general(a_ref[...].astype(bf16), b_ref[...].astype(bf16), dims,
                               preferred_element_type=f32)
        if nk == 1:
            o_ref[...] = finish(part).astype(o_ref.dtype)
        else:
            acc_ref = scratch[0]
            k = pl.program_id(2)

            @pl.when(k == 0)
            def _():
                acc_ref[...] = part

            @pl.when(k > 0)
            def _():
                acc_ref[...] += part

            @pl.when(k == nk - 1)
            def _():
                o_ref[...] = finish(acc_ref[...]).astype(o_ref.dtype)

    return pl.pallas_call(
        body, name=name, grid=(M // tm, N // tn, nk), in_specs=[a_spec, b_spec],
        out_specs=pl.BlockSpec((tm, tn), lambda i, j, k: (i, j)),
        out_shape=jax.ShapeDtypeStruct((M, N), out_dtype),
        scratch_shapes=[] if nk == 1 else [pltpu.VMEM((tm, tn), f32)],
        compiler_params=_cparams(("parallel", "parallel", "arbitrary")),
    )(a, b)


def _sp(x):
    hi = x.astype(bf16)
    return hi, (x - hi.astype(f32)).astype(bf16)


def _cat(parts):
    return tuple(jnp.concatenate(p, axis=1) for p in zip(*parts))


def _bmm(eq, a, b):
    (ah, al), (bh, bl) = a, b
    dot = functools.partial(jnp.einsum, eq, preferred_element_type=f32)
    return dot(ah, bh) + (dot(ah, bl) + dot(al, bh))


def _tri_dot(eq, tri, x):
    h1 = x.astype(bf16)
    r1 = x - h1.astype(f32)
    h2 = r1.astype(bf16)
    h3 = (r1 - h2.astype(f32)).astype(bf16)
    dot = functools.partial(jnp.einsum, eq, preferred_element_type=f32)
    return dot(tri, h1) + (dot(tri, h2) + dot(tri, h3))


def _tri_masks(T):
    ti = lax.broadcasted_iota(jnp.int32, (T, T), 0)
    si = lax.broadcasted_iota(jnp.int32, (T, T), 1)
    return ti >= si, ti > si


def _wkv_prep(r, lw, k, kkr, a):
    H, T, _ = r.shape
    low_i, low_s = _tri_masks(T)
    nrm = jnp.sqrt(jnp.sum(kkr * kkr, axis=-1, keepdims=True))
    den = jnp.maximum(nrm, 1e-12)
    kk = kkr / den
    tri = jnp.broadcast_to(low_i.astype(bf16)[None], (H, T, T))
    cl = _tri_dot("hts,hsn->htn", tri, lw)
    c = jnp.exp(cl)
    cprev = jnp.exp(cl - lw)
    cinv = jnp.exp(-cl)
    bt, kt = _sp(kk * a * cinv), _sp(k * cinv)
    L = _cat([_sp(r * c), _sp(-kk * cprev)])
    Mb = _bmm("htn,hsn->hts", L, bt)
    Mk = _bmm("htn,hsn->hts", L, kt)
    A_rb = jnp.where(low_i[None], Mb[:, :T], 0.0)
    A_ab = jnp.where(low_s[None], Mb[:, T:], 0.0)
    Mk = jnp.concatenate([jnp.where(low_i[None], Mk[:, :T], 0.0), jnp.where(low_s[None], Mk[:, T:], 0.0)], axis=1)
    return dict(kk=kk, den=den, nrm=nrm, c=c, cprev=cprev, cinv=cinv, L=L, kt=kt, bt=bt,
                A_ab=A_ab, A_rb=A_rb, Mk=Mk, cT=c[:, T - 1:T, :])


def _tri_inverse(A):
    T = A.shape[-1]
    eye = (lax.broadcasted_iota(jnp.int32, (T, T), 0) == lax.broadcasted_iota(jnp.int32, (T, T), 1)).astype(f32)
    inv = eye[None] + A
    X = A
    n = 1
    while 2 * n < T:
        Xs = _sp(X)
        X = _bmm("hts,hsu->htu", Xs, Xs)
        inv = inv + _bmm("hts,hsu->htu", _sp(inv), _sp(X))
        n *= 2
    return inv


def _wkv_chunk_fwd(S0, r, lw, k, v, kkr, a):
    T = r.shape[1]
    q = _wkv_prep(r, lw, k, kkr, a)
    inv = _tri_inverse(q["A_ab"])
    vs = _sp(v)
    P = _bmm("htk,hvk->htv", q["L"], _sp(S0)) + _bmm("hts,hsv->htv", _sp(q["Mk"]), vs)
    U = _bmm("hts,hsv->htv", _sp(inv), _sp(P[:, T:]))
    Us = _sp(U)
    Y = P[:, :T] + _bmm("hts,hsv->htv", _sp(q["A_rb"]), Us)
    S1 = (S0 + _bmm("htv,htk->hvk", _cat([Us, vs]), _cat([q["bt"], q["kt"]]))) * q["cT"]
    return Y, U, inv, S1


def _wkv_chunk_bwd(S0, Hin, Q, r, lw, k, v, kkr, a, U, inv, dY):
    H, T, _ = r.shape
    low_i, low_s = _tri_masks(T)
    q = _wkv_prep(r, lw, k, kkr, a)
    L, kt, bt = q["L"], q["kt"], q["bt"]
    R = _cat([bt, kt])
    Hh = Hin * q["cT"]
    Hs, S0s, dYs, vs, Us = _sp(Hh), _sp(S0), _sp(dY), _sp(v), _sp(U)
    RH = _bmm("htk,hvk->htv", R, Hs)
    Z = _bmm("hst,hsv->htv", _sp(inv), _sp(RH[:, :T] + _bmm("hst,hsv->htv", _sp(q["A_rb"]), dYs)))
    DZ = _cat([dYs, _sp(Z)])
    both = jnp.concatenate([jnp.broadcast_to(low_i[None], (1, T, T)), jnp.broadcast_to(low_s[None], (1, T, T))], axis=1)
    NU = _sp(jnp.where(both, _bmm("htv,hsv->hts", DZ, Us), 0.0))
    NV = _sp(jnp.where(both, _bmm("htv,hsv->hts", DZ, vs), 0.0))
    ra = _bmm("htv,hvk->htk", DZ, S0s) + _bmm("hts,hsk->htk", NU, bt) + _bmm("hts,hsk->htk", NV, kt)
    dr = ra[:, :T] * q["c"]
    da = ra[:, T:] * q["cprev"]
    dv = RH[:, T:] + _bmm("hst,hsv->htv", _sp(q["Mk"]), DZ)
    VH = _bmm("htv,hvk->htk", _cat([vs, Us]), Hs)
    dk = (VH[:, :T] + _bmm("hst,hsk->htk", NV, L)) * q["cinv"]
    db = (VH[:, T:] + _bmm("hst,hsk->htk", NU, L)) * q["cinv"]
    H0 = Hh + _bmm("htv,htk->hvk", DZ, L)
    kk = q["kk"]
    e = r * dr - kk * a * db - k * dk
    f = -kk * da
    tri_i = jnp.broadcast_to(low_i.astype(bf16)[None], (H, T, T))
    tri_s = jnp.broadcast_to(low_s.astype(bf16)[None], (H, T, T))
    dlw = _tri_dot("hst,hsn->htn", tri_i, e) + _tri_dot("hst,hsn->htn", tri_s, f) + Q
    Qn = Q + jnp.sum(e + f, axis=1, keepdims=True)
    dkk = db * a - da
    dasig = db * kk
    proj = jnp.sum(dkk * kk, axis=-1, keepdims=True)
    dkkr = jnp.where(q["nrm"] > 1e-12, dkk - kk * proj, dkk) / q["den"]
    return dr, dlw, dk, dv, dkkr, dasig, H0, Qn


def _heads(ref):
    return jnp.stack([ref[:, h * HEAD:(h + 1) * HEAD] for h in range(RW_HEADS)], axis=0)


def _put_heads(ref, val):
    for h in range(RW_HEADS):
        ref[:, h * HEAD:(h + 1) * HEAD] = val[h]


def _wkv_fwd(r, lw, k, v, kkr, a, g, r_k, ln_w, ln_b):
    S = r.shape[0]
    H, N, T = RW_HEADS, HEAD, WKV_T
    nc = S // T

    def body(r_ref, lw_ref, k_ref, v_ref, kkr_ref, a_ref, g_ref, rk_ref, lnw_ref, lnb_ref,
             y_ref, yg_ref, wkv_ref, u_ref, inv_ref, s0_ref, state):
        @pl.when(pl.program_id(0) == 0)
        def _():
            state[...] = jnp.zeros(state.shape, f32)

        S0 = state[...]
        s0_ref[0] = S0
        rr, kk2, vv = _heads(r_ref), _heads(k_ref), _heads(v_ref)
        Y, U, inv, S1 = _wkv_chunk_fwd(S0, rr, _heads(lw_ref), kk2, vv, _heads(kkr_ref), _heads(a_ref))
        state[...] = S1
        wkv_ref[...] = Y
        u_ref[...] = U
        inv_ref[...] = inv
        mean = jnp.mean(Y, axis=-1, keepdims=True)
        var = jnp.mean(jnp.square(Y - mean), axis=-1, keepdims=True)
        yn = (Y - mean) * lax.rsqrt(var + GN_EPS)
        bonus = jnp.sum(rr * kk2 * rk_ref[...], axis=-1, keepdims=True) * vv
        _put_heads(y_ref, yn * lnw_ref[...] + lnb_ref[...] + bonus)
        yg_ref[...] = (y_ref[...] * g_ref[...]).astype(yg_ref.dtype)

    tok = pl.BlockSpec((T, H * N), lambda i: (i, 0))
    blk = pl.BlockSpec((H, T, N), lambda i: (0, i, 0))
    par = pl.BlockSpec((H, 1, N), lambda i: (0, 0, 0))
    seq = jax.ShapeDtypeStruct((H, S, N), f32)
    return pl.pallas_call(
        body, name="wkv_fwd", grid=(nc,), in_specs=[tok] * 7 + [par] * 3,
        out_specs=[tok, tok, blk, blk, blk, pl.BlockSpec((1, H, N, N), lambda i: (i, 0, 0, 0))],
        out_shape=[jax.ShapeDtypeStruct((S, H * N), f32), jax.ShapeDtypeStruct((S, H * N), bf16), seq, seq, seq,
                   jax.ShapeDtypeStruct((nc, H, N, N), f32)],
        scratch_shapes=[pltpu.VMEM((H, N, N), f32)],
        compiler_params=_cparams(("arbitrary",)),
    )(r, lw, k, v, kkr, a, g, r_k, ln_w, ln_b)


def _wkv_bwd(dy, g, r, lw, k, v, kkr, a, wkv, U, inv, S0s, r_k, ln_w, ln_b):
    S = r.shape[0]
    H, N, T = RW_HEADS, HEAD, WKV_T
    nc = S // T

    def body(dy_ref, g_ref, r_ref, lw_ref, k_ref, v_ref, kkr_ref, a_ref, wkv_ref, u_ref, inv_ref, s0_ref,
             rk_ref, lnw_ref, lnb_ref,
             dr_ref, dlw_ref, dk_ref, dv_ref, dkkr_ref, da_ref, drk_ref, dlnw_ref, dlnb_ref, hst, qst):
        @pl.when(pl.program_id(0) == 0)
        def _():
            hst[...] = jnp.zeros(hst.shape, f32)
            qst[...] = jnp.zeros(qst.shape, f32)
            drk_ref[...] = jnp.zeros(drk_ref.shape, f32)
            dlnw_ref[...] = jnp.zeros(dlnw_ref.shape, f32)
            dlnb_ref[...] = jnp.zeros(dlnb_ref.shape, f32)

        dya = _heads(dy_ref[...] * g_ref[...])
        rr, kk2, vv, Y = _heads(r_ref), _heads(k_ref), _heads(v_ref), wkv_ref[...]
        rk = rk_ref[...]
        s = jnp.sum(rr * kk2 * rk, axis=-1, keepdims=True)
        ds = jnp.sum(dya * vv, axis=-1, keepdims=True)
        mean = jnp.mean(Y, axis=-1, keepdims=True)
        var = jnp.mean(jnp.square(Y - mean), axis=-1, keepdims=True)
        rstd = lax.rsqrt(var + GN_EPS)
        yn = (Y - mean) * rstd
        dyn = dya * lnw_ref[...]
        dY = rstd * (dyn - jnp.mean(dyn, axis=-1, keepdims=True) - yn * jnp.mean(dyn * yn, axis=-1, keepdims=True))
        drk_ref[...] += jnp.sum(ds * rr * kk2, axis=1, keepdims=True)
        dlnw_ref[...] += jnp.sum(dya * yn, axis=1, keepdims=True)
        dlnb_ref[...] += jnp.sum(dya, axis=1, keepdims=True)
        dr, dlw, dk, dv, dkkr, dasig, H0, Qn = _wkv_chunk_bwd(
            s0_ref[0], hst[...], qst[...], rr, _heads(lw_ref), kk2, vv, _heads(kkr_ref), _heads(a_ref), u_ref[...],
            inv_ref[...], dY)
        hst[...] = H0
        qst[...] = Qn
        _put_heads(dr_ref, dr + ds * kk2 * rk)
        _put_heads(dlw_ref, dlw)
        _put_heads(dk_ref, dk + ds * rr * rk)
        _put_heads(dv_ref, dv + dya * s)
        _put_heads(dkkr_ref, dkkr)
        _put_heads(da_ref, dasig)

    tok = pl.BlockSpec((T, H * N), lambda i: (nc - 1 - i, 0))
    blk = pl.BlockSpec((H, T, N), lambda i: (0, nc - 1 - i, 0))
    par = pl.BlockSpec((H, 1, N), lambda i: (0, 0, 0))
    seq = jax.ShapeDtypeStruct((S, H * N), f32)
    pout = jax.ShapeDtypeStruct((H, 1, N), f32)
    return pl.pallas_call(
        body, name="wkv_bwd", grid=(nc,),
        in_specs=[tok] * 8 + [blk] * 3 + [pl.BlockSpec((1, H, N, N), lambda i: (nc - 1 - i, 0, 0, 0))] + [par] * 3,
        out_specs=[tok] * 6 + [par] * 3,
        out_shape=[seq] * 6 + [pout] * 3,
        scratch_shapes=[pltpu.VMEM((H, N, N), f32), pltpu.VMEM((H, 1, N), f32)],
        compiler_params=_cparams(("arbitrary",)),
    )(dy, g, r, lw, k, v, kkr, a, wkv, U, inv, S0s, r_k, ln_w, ln_b)


ATTN_G = 8


def _attn_first_mask(S, G):
    nbg = HEADS_PER_GROUP * S // ATTN_BLK
    b = pl.program_id(0) * G + lax.broadcasted_iota(jnp.int32, (G, 1, 1), 0)
    nbs = [S // (d * ATTN_BLK) for _, d in ATTN_PAIRS]
    per = jnp.where(b < nbg, nbs[0], jnp.where(b < 2 * nbg, nbs[1], nbs[2]))
    return jnp.bitwise_and(b, per - 1) == 0


def _attn_norm(x, gain, scale):
    rs = lax.rsqrt(jnp.mean(x * x, axis=-1, keepdims=True) + RMS_EPS)
    return x * rs * (gain * scale), rs


def _attn_scores(qn, kn_c, kn_p, first):
    s_c = jnp.einsum("gqe,gke->gqk", qn.astype(bf16), kn_c.astype(bf16), preferred_element_type=f32)
    s_p = jnp.einsum("gqe,gke->gqk", qn.astype(bf16), kn_p.astype(bf16), preferred_element_type=f32)
    qi = lax.broadcasted_iota(jnp.int32, (1, ATTN_BLK, ATTN_BLK), 1)
    ki = lax.broadcasted_iota(jnp.int32, (1, ATTN_BLK, ATTN_BLK), 2)
    s_c = jnp.where(qi >= ki, s_c, NEG_INF)
    s_p = jnp.where(jnp.logical_and(ki >= qi, jnp.logical_not(first)), s_p, NEG_INF)
    return s_c, s_p


def _attn_fwd(q, k, kp, v, vp, qg, kg, S):
    NB = q.shape[0]
    G = ATTN_G

    def body(q_ref, k_ref, kp_ref, v_ref, vp_ref, qg_ref, kg_ref, o_ref, lse_ref):
        first = _attn_first_mask(S, G)
        qn, _ = _attn_norm(q_ref[...], qg_ref[...], HEAD ** -0.5)
        kn_c, _ = _attn_norm(k_ref[...], kg_ref[...], 1.0)
        kn_p, _ = _attn_norm(kp_ref[...], kg_ref[...], 1.0)
        s_c, s_p = _attn_scores(qn, kn_c, kn_p, first)
        m = jnp.maximum(jnp.max(s_c, axis=-1, keepdims=True), jnp.max(s_p, axis=-1, keepdims=True))
        p_c = jnp.exp(s_c - m)
        p_p = jnp.exp(s_p - m)
        den = jnp.sum(p_c, axis=-1, keepdims=True) + jnp.sum(p_p, axis=-1, keepdims=True)
        inv = 1.0 / den
        o = jnp.einsum("gqk,gke->gqe", (p_c * inv).astype(bf16), v_ref[...].astype(bf16), preferred_element_type=f32)
        o += jnp.einsum("gqk,gke->gqe", (p_p * inv).astype(bf16), vp_ref[...].astype(bf16), preferred_element_type=f32)
        o_ref[...] = o
        lse_ref[...] = jnp.broadcast_to(m + jnp.log(den), o.shape)

    blk = pl.BlockSpec((G, ATTN_BLK, HEAD), lambda i: (i, 0, 0))
    par = pl.BlockSpec((1, 1, HEAD), lambda i: (0, 0, 0))
    shp = jax.ShapeDtypeStruct((NB, ATTN_BLK, HEAD), f32)
    return pl.pallas_call(
        body, name="attn_fwd", grid=(NB // G,), in_specs=[blk] * 5 + [par] * 2, out_specs=[blk, blk],
        out_shape=[shp, shp], compiler_params=_cparams(("arbitrary",)),
    )(q, k, kp, v, vp, qg, kg)


def _attn_bwd(q, k, kp, v, vp, o, lse, do, dlse, qg, kg, S):
    NB = q.shape[0]
    G = ATTN_G

    def norm_bwd(dxn, x, rs, gain, scale):
        xh = x * rs
        dxh = dxn * (gain * scale)
        dx = rs * (dxh - xh * jnp.mean(dxh * xh, axis=-1, keepdims=True))
        dgain = jnp.sum(jnp.sum(dxn * xh * scale, axis=1, keepdims=True), axis=0, keepdims=True)
        return dx, dgain

    def body(q_ref, k_ref, kp_ref, v_ref, vp_ref, o_ref, lse_ref, do_ref, dlse_ref, qg_ref, kg_ref,
             dq_ref, dkc_ref, dkp_ref, dvc_ref, dvp_ref, dqg_ref, dkg_ref):
        @pl.when(pl.program_id(0) == 0)
        def _():
            dqg_ref[...] = jnp.zeros(dqg_ref.shape, f32)
            dkg_ref[...] = jnp.zeros(dkg_ref.shape, f32)

        first = _attn_first_mask(S, G)
        qx, kx, kpx = q_ref[...], k_ref[...], kp_ref[...]
        qg, kg = qg_ref[...], kg_ref[...]
        qn, rq = _attn_norm(qx, qg, HEAD ** -0.5)
        kn_c, rk_c = _attn_norm(kx, kg, 1.0)
        kn_p, rk_p = _attn_norm(kpx, kg, 1.0)
        s_c, s_p = _attn_scores(qn, kn_c, kn_p, first)
        lse = lse_ref[...][:, :, 0:1]
        p_c = jnp.exp(s_c - lse)
        p_p = jnp.exp(s_p - lse)
        dO = do_ref[...]
        dOb = dO.astype(bf16)
        vb, vpb = v_ref[...].astype(bf16), vp_ref[...].astype(bf16)
        dp_c = jnp.einsum("gqe,gke->gqk", dOb, vb, preferred_element_type=f32)
        dp_p = jnp.einsum("gqe,gke->gqk", dOb, vpb, preferred_element_type=f32)
        corr = dlse_ref[...][:, :, 0:1] - jnp.sum(dO * o_ref[...], axis=-1, keepdims=True)
        ds_c = (p_c * (dp_c + corr)).astype(bf16)
        ds_p = (p_p * (dp_p + corr)).astype(bf16)
        qnb, kcb, kpb = qn.astype(bf16), kn_c.astype(bf16), kn_p.astype(bf16)
        dqn = (jnp.einsum("gqk,gke->gqe", ds_c, kcb, preferred_element_type=f32)
               + jnp.einsum("gqk,gke->gqe", ds_p, kpb, preferred_element_type=f32))
        dkn_c = jnp.einsum("gqk,gqe->gke", ds_c, qnb, preferred_element_type=f32)
        dkn_p = jnp.einsum("gqk,gqe->gke", ds_p, qnb, preferred_element_type=f32)
        dvc_ref[...] = jnp.einsum("gqk,gqe->gke", p_c.astype(bf16), dOb, preferred_element_type=f32)
        dvp_ref[...] = jnp.einsum("gqk,gqe->gke", p_p.astype(bf16), dOb, preferred_element_type=f32)
        dq, dqg = norm_bwd(dqn, qx, rq, qg, HEAD ** -0.5)
        dkc, dkg1 = norm_bwd(dkn_c, kx, rk_c, kg, 1.0)
        dkp, dkg2 = norm_bwd(dkn_p, kpx, rk_p, kg, 1.0)
        dq_ref[...] = dq
        dkc_ref[...] = dkc
        dkp_ref[...] = dkp
        dqg_ref[...] += dqg
        dkg_ref[...] += dkg1 + dkg2

    blk = pl.BlockSpec((G, ATTN_BLK, HEAD), lambda i: (i, 0, 0))
    par = pl.BlockSpec((1, 1, HEAD), lambda i: (0, 0, 0))
    shp = jax.ShapeDtypeStruct((NB, ATTN_BLK, HEAD), f32)
    pshp = jax.ShapeDtypeStruct((1, 1, HEAD), f32)
    return pl.pallas_call(
        body, name="attn_bwd", grid=(NB // G,), in_specs=[blk] * 9 + [par] * 2,
        out_specs=[blk] * 5 + [par] * 2, out_shape=[shp] * 5 + [pshp] * 2,
        compiler_params=_cparams(("arbitrary",)),
    )(q, k, kp, v, vp, o, lse, do, dlse, qg, kg)


def _to_blocks(t, S):
    outs = []
    for gi, (_, d) in enumerate(ATTN_PAIRS):
        tg = t[:, gi * 256:(gi + 1) * 256].reshape(S // d, d, HEADS_PER_GROUP, HEAD)
        outs.append(tg.transpose(2, 1, 0, 3).reshape(-1, ATTN_BLK, HEAD))
    return jnp.concatenate(outs, axis=0)


def _from_blocks(b, S):
    nbg = HEADS_PER_GROUP * S // ATTN_BLK
    outs = []
    for gi, (_, d) in enumerate(ATTN_PAIRS):
        bg = b[gi * nbg:(gi + 1) * nbg].reshape(HEADS_PER_GROUP, d, S // d, HEAD)
        outs.append(bg.transpose(2, 1, 0, 3).reshape(S, HEADS_PER_GROUP * HEAD))
    return jnp.concatenate(outs, axis=1)


def _prev_block(b):
    return jnp.concatenate([jnp.zeros_like(b[:1]), b[:-1]], axis=0)


def _next_block(b):
    return jnp.concatenate([b[1:], jnp.zeros_like(b[:1])], axis=0)


def _rms(x, g):
    rs = lax.rsqrt(jnp.mean(x * x, axis=-1, keepdims=True) + RMS_EPS)
    return x * rs * g


def _f_rms(x, g):
    return _rms(x, g)


def _f_resid_rms(coef, x, f, g):
    xn = x + coef * f
    return xn, _rms(xn, g)


def _f_swiglu(u):
    gate, up = u[:, :D_FF], u[:, D_FF:]
    return gate * jax.nn.sigmoid(gate) * up


def _f_swiglu_bwd(dact, u):
    gate, up = u[:, :D_FF], u[:, D_FF:]
    sg = jax.nn.sigmoid(gate)
    silu = gate * sg
    dact = 0.5 * dact
    return jnp.concatenate([dact * up * (sg * (1.0 + gate * (1.0 - sg))), dact * silu], axis=1)


def _f_rms_bwd(n_parts, *args):
    dns = args[:n_parts]
    x, dres, g = args[n_parts:]
    dn = dns[0]
    for t in dns[1:]:
        dn = dn + t
    rs = lax.rsqrt(jnp.mean(x * x, axis=-1, keepdims=True) + RMS_EPS)
    xh = x * rs
    dxh = dn * g
    dx = dres + rs * (dxh - xh * jnp.mean(dxh * xh, axis=-1, keepdims=True))
    return dx, dx, jnp.sum(dn * xh, axis=0, keepdims=True)


def _f_loss(x, f, tgt):
    y = x + 0.5 * f
    diff = y - tgt
    part = 0.5 * jnp.sum(jnp.mean(diff * diff, axis=-1, keepdims=True), axis=0, keepdims=True)
    dy = diff * (1.0 / D)
    return dy, dy, jnp.broadcast_to(part, (1, 128))


def _dotb(a, b, dims):
    return lax.dot_general(a.astype(bf16), b.astype(bf16), dims, preferred_element_type=f32)


_NN = (((1,), (0,)), ((), ()))
_NT = (((1,), (1,)), ((), ()))
_TN = (((0,), (0,)), ((), ()))


def _rwkv_pre_core(prkv, prkv_prev, plora, plora_prev, mu_rkv, mu_lora, w0, w2p, a0, a2p, g2p, k_k, k_a):
    xs = prkv + (prkv_prev - prkv) * mu_rkv
    xl = plora + (plora_prev - plora) * mu_lora
    r, k, v = xs[:, :D], xs[:, D:2 * D], xs[:, 2 * D:]
    wd, ad, gd = xl[:, :128], xl[:, 128:256], xl[:, 256:]
    tw = jnp.tanh(wd)
    zw = w0 + _dotb(tw, w2p, _NN)
    sp = jnp.maximum(-zw, 0.0) + jnp.log(1.0 + jnp.exp(-jnp.abs(zw)))
    lw = -jnp.exp(-sp - 0.5)
    a = jax.nn.sigmoid(a0 + _dotb(ad, a2p, _NN))
    sg = jax.nn.sigmoid(gd)
    return dict(r=r, k=k, v=v, tw=tw, zw=zw, lw=lw, a=a, sg=sg, ad=ad)


def _rows_down(x, halo, blk):
    before = jnp.where(blk > 0, halo[HALO - 1:HALO, :], 0.0)
    row = lax.broadcasted_iota(jnp.int32, (x.shape[0], 1), 0)
    return jnp.where(row == 0, before, pltpu.roll(x, 1, 0))


def _rows_up(x, after):
    n = x.shape[0]
    row = lax.broadcasted_iota(jnp.int32, (n, 1), 0)
    return jnp.where(row == n - 1, after, pltpu.roll(x, n - 1, 0))


def _f_rwkv_pre(prkv, plora, mu_rkv, mu_lora, w0, w2p, a0, a2p, g2p, k_k, k_a, halo_rkv, halo_lora, blk):
    c = _rwkv_pre_core(prkv, _rows_down(prkv, halo_rkv, blk), plora, _rows_down(plora, halo_lora, blk),
                       mu_rkv, mu_lora, w0, w2p, a0, a2p, g2p, k_k, k_a)
    g = _dotb(c["sg"], g2p, _NN)
    k, a = c["k"], c["a"]
    return c["r"], c["lw"], k * (1.0 + (a - 1.0) * k_a), c["v"], k * k_k, a, g


def _f_rwkv_pre_bwd(prkv, plora, dr, dlw, dk2, dv, dkkr, da, dya, yap,
                    mu_rkv, mu_lora, w0, w2p, a0, a2p, g2p, k_k, k_a, halo_rkv, halo_lora, next_rkv, next_lora, blk):
    prkv_prev, plora_prev = _rows_down(prkv, halo_rkv, blk), _rows_down(plora, halo_lora, blk)
    c = _rwkv_pre_core(prkv, prkv_prev, plora, plora_prev, mu_rkv, mu_lora, w0, w2p, a0, a2p, g2p, k_k, k_a)
    k, a, sg, tw, zw, lw = c["k"], c["a"], c["sg"], c["tw"], c["zw"], c["lw"]
    dg = dya * yap
    dsg = _dotb(dg, g2p, _NT)
    dgd = dsg * sg * (1.0 - sg)
    dg2p = _dotb(sg, dg, _TN)
    dk = dk2 * (1.0 + (a - 1.0) * k_a) + dkkr * k_k
    da_t = da + dk2 * k * k_a
    dk_a = jnp.sum(dk2 * k * (a - 1.0), axis=0, keepdims=True)
    dk_k = jnp.sum(dkkr * k, axis=0, keepdims=True)
    dza = da_t * a * (1.0 - a)
    da0 = jnp.sum(dza, axis=0, keepdims=True)
    dad = _dotb(dza, a2p, _NT)
    da2p = _dotb(c["ad"], dza, _TN)
    dzw = dlw * lw * jax.nn.sigmoid(-zw)
    dw0 = jnp.sum(dzw, axis=0, keepdims=True)
    dtw = _dotb(dzw, w2p, _NT)
    dw2p = _dotb(tw, dzw, _TN)
    dwd = dtw * (1.0 - tw * tw)
    dxs = jnp.concatenate([dr, dk, dv], axis=1)
    dxl = jnp.concatenate([dwd, dad, dgd], axis=1)
    dmu_rkv = jnp.sum(dxs * (prkv_prev - prkv), axis=0, keepdims=True)
    dmu_lora = jnp.sum(dxl * (plora_prev - plora), axis=0, keepdims=True)
    to_next_rkv, to_next_lora = dxs * mu_rkv, dxl * mu_lora
    return (dxs * (1.0 - mu_rkv) + _rows_up(to_next_rkv, next_rkv), dxl * (1.0 - mu_lora) + _rows_up(to_next_lora, next_lora),
            dmu_rkv, dmu_lora, dw0, da0, dk_k, dk_a, dw2p, da2p, dg2p, to_next_rkv[0:1], to_next_lora[0:1])


def _group_alpha(lse):
    l0, l1, l2 = lse[:, :256], lse[:, 256:512], lse[:, 512:]
    m = jnp.maximum(jnp.maximum(l0, l1), l2)
    e0, e1, e2 = jnp.exp(l0 - m), jnp.exp(l1 - m), jnp.exp(l2 - m)
    inv = 1.0 / (e0 + e1 + e2)
    return jnp.concatenate([e0 * inv, e1 * inv, e2 * inv], axis=1)


def _f_combine(o, lse):
    return o * _group_alpha(lse)


def _f_combine_bwd(dyb, o, lse, bd):
    alpha = _group_alpha(lse)
    e = jnp.dot(dyb * o, bd, precision=HI, preferred_element_type=f32)
    ae = alpha * e
    tot = ae[:, :256] + ae[:, 256:512] + ae[:, 512:]
    return dyb * alpha, ae - alpha * jnp.concatenate([tot, tot, tot], axis=1)


def _f_merge(pgate, ta, tb, b_gate):
    gate = jax.nn.sigmoid(pgate + b_gate)
    return gate[:, :D] * ta + gate[:, D:] * tb


def _f_merge_bwd(dm, pgate, ta, tb, b_gate):
    gate = jax.nn.sigmoid(pgate + b_gate)
    ga, gb = gate[:, :D], gate[:, D:]
    dpg = jnp.concatenate([dm * ta * ga * (1.0 - ga), dm * tb * gb * (1.0 - gb)], axis=1)
    return dm * ga, dm * gb, dpg, jnp.sum(dpg, axis=0, keepdims=True)


def _f_adamw(w, g, m, v):
    m2 = ADAM_B1 * m + (1.0 - ADAM_B1) * g
    v2 = ADAM_B2 * v + (1.0 - ADAM_B2) * jnp.square(g)
    m_hat = m2 / (1.0 - ADAM_B1 ** ADAM_STEP)
    v_hat = v2 / (1.0 - ADAM_B2 ** ADAM_STEP)
    delta = -ADAM_LR * (m_hat / (jnp.sqrt(v_hat) + ADAM_EPS) + ADAM_WD * w)
    return delta, m2, v2


def _ffn_fwd(tag, n, WiT, Wo):
    S = n.shape[0]
    u = _mm(f"{tag}_up", n, WiT, "nt")
    (act,) = _rowwise(f"{tag}_swiglu", _f_swiglu, [u], [], [(D_FF, bf16)])
    f = _mm(f"{tag}_down", act, Wo, "nn")
    return u, act, f


def _ffn_bwd(tag, dxo, dxo_b, x_in, n, u, act, g, WiT, Wo):
    dact = _mm(f"{tag}_dact", dxo_b, Wo, "nt")
    dWo = _mm(f"{tag}_dwo", act, dxo_b, "tn", out_dtype=GRAD_WIRE, scale=0.5)
    (du,) = _rowwise(f"{tag}_dswiglu", _f_swiglu_bwd, [dact, u], [], [(2 * D_FF, bf16)], tm=128)
    dn = _mm(f"{tag}_dn", du, WiT, "nn")
    dWiT = _mm(f"{tag}_dwi", du, n, "tn", out_dtype=GRAD_WIRE)
    dx, dx_b, dg = _rowwise(f"{tag}_drms", functools.partial(_f_rms_bwd, 1), [dn, x_in, dxo], [g],
                            [(D, f32), (D, bf16)], [(1, D)])
    return dx, dx_b, dg, dWiT, dWo


def _local_step(x0, tgt, W, P):
    S = x0.shape[0]
    (n1,) = _rowwise("f1_rms", _f_rms, [x0], [P["ffn1_norm"]], [(D, bf16)])
    u1, act1, f1 = _ffn_fwd("f1", n1, W["f1_iT"], W["f1_o"])
    x1, h = _rowwise("mix_rms", functools.partial(_f_resid_rms, 0.5), [x0, f1], [P["mix_norm"]],
                     [(D, f32), (D, bf16)])
    prkv = _mm("p_rkv", h, W["in_rkvT"], "nt")
    plora = _mm("p_lora", h, W["in_loraT"], "nt")
    pqkv = _mm("p_qkv", h, W["in_qkvT"], "nt")
    pgate = _mm("p_gate", h, W["in_gateT"], "nt")
    pre_params = [P["mu_rkv"], P["mu_lora"], P["w0"], W["w2p"], P["a0"], W["a2p"], W["g2p"], P["k_k"], P["k_a"]]
    r, lw, k2, v, kkr, a, g = _rowwise("rwkv_pre", _f_rwkv_pre, [prkv, plora], pre_params, [(D, f32)] * 7, tm=128,
                                       halos=(0, 1))
    hm = [r, lw, k2, v, kkr, a]
    hp = [P["r_k"].reshape(RW_HEADS, 1, HEAD), P["ln_w"].reshape(RW_HEADS, 1, HEAD), P["ln_b"].reshape(RW_HEADS, 1, HEAD)]
    yap, ya, wkv_h, U_h, inv_h, S0s = _wkv_fwd(*hm, g, *hp)
    ta = _mm("proj_a", ya, W["pr"], "nn")
    qb, kb, vb = (_to_blocks(pqkv[:, i * ATTN_W:(i + 1) * ATTN_W], S) for i in range(3))
    kpb, vpb = _prev_block(kb), _prev_block(vb)
    qg, kg = P["q_norm"].reshape(1, 1, HEAD), P["k_norm"].reshape(1, 1, HEAD)
    o_b, lse_b = _attn_fwd(qb, kb, kpb, vb, vpb, qg, kg, S)
    o_t, lse_t = _from_blocks(o_b, S), _from_blocks(lse_b, S)
    (yb,) = _rowwise("attn_combine", _f_combine, [o_t, lse_t], [], [(ATTN_W, bf16)])
    tb = _mm("proj_b", yb, W["paT"], "nt")
    (merged,) = _rowwise("merge", _f_merge, [pgate, ta, tb], [P["b_gate"]], [(D, bf16)])
    mo = _mm("mix_out", merged, W["out"], "nn")
    x2, n2 = _rowwise("f2_rms", functools.partial(_f_resid_rms, 1.0), [x1, mo], [P["ffn2_norm"]],
                      [(D, f32), (D, bf16)])
    u2, act2, f2 = _ffn_fwd("f2", n2, W["f2_iT"], W["f2_o"])
    dx3, dx3_b, loss = _rowwise("loss", _f_loss, [x2, f2, tgt], [], [(D, f32), (D, bf16)], [(1, 128)])
    G, Gs = {}, {}
    dx2, dx2_b, Gs["ffn2_norm"], G["f2_iT"], G["f2_o"] = _ffn_bwd("f2", dx3, dx3_b, x2, n2, u2, act2, P["ffn2_norm"],
                                                                 W["f2_iT"], W["f2_o"])
    dmerged = _mm("d_merged", dx2_b, W["out"], "nt")
    G["out"] = _mm("dw_out", merged, dx2_b, "tn", out_dtype=GRAD_WIRE)
    dta, dtb, dpgate, Gs["b_gate"] = _rowwise("merge_bwd", _f_merge_bwd, [dmerged, pgate, ta, tb], [P["b_gate"]],
                                              [(D, bf16), (D, bf16), (2 * D, bf16)], [(1, 2 * D)])
    dya = _mm("d_ya", dta, W["pr"], "nt")
    G["pr"] = _mm("dw_pr", ya, dta, "tn", out_dtype=GRAD_WIRE)
    dyb = _mm("d_yb", dtb, W["paT"], "nn")
    G["paT"] = _mm("dw_pa", dtb, yb, "tn", out_dtype=GRAD_WIRE)
    hg = _wkv_bwd(dya, g, *hm, wkv_h, U_h, inv_h, S0s, *hp)
    dr, dlw, dk2, dv, dkkr, da = hg[:6]
    Gs["r_k"], Gs["ln_w"], Gs["ln_b"] = (t.reshape(1, D) for t in hg[6:])
    lp = sum(LORA_PAD)
    (dprkv, dplora, Gs["mu_rkv"], Gs["mu_lora"], Gs["w0"], Gs["a0"], Gs["k_k"], Gs["k_a"],
     dw2p, da2p, dg2p) = _rowwise(
        "rwkv_pre_bwd", _f_rwkv_pre_bwd,
        [prkv, plora, dr, dlw, dk2, dv, dkkr, da, dya, yap], pre_params,
        [(3 * D, bf16), (lp, bf16)],
        [(1, 3 * D), (1, lp), (1, D), (1, D), (1, D), (1, D), (LORA_PAD[0], D), (LORA_PAD[1], D), (LORA_PAD[2], D)],
        tm=128, halos=(0, 1), carries=((1, 3 * D), (1, lp)), reverse=True)
    G["w2T"], G["a2T"], G["g2T"] = dw2p[:LORA_W[0]].T, da2p[:LORA_W[1]].T, dg2p[:LORA_W[2]].T
    bd = (jnp.arange(ATTN_W)[:, None] // HEAD == jnp.arange(ATTN_W)[None, :] // HEAD).astype(f32)
    do_t, dlse_t = _rowwise("attn_combine_bwd", _f_combine_bwd, [dyb, o_t, lse_t], [bd], [(ATTN_W, f32)] * 2)
    dq_b, dkc_b, dkp_b, dvc_b, dvp_b, dqg, dkg = _attn_bwd(qb, kb, kpb, vb, vpb, o_b, lse_b, _to_blocks(do_t, S),
                                                            _to_blocks(dlse_t, S), qg, kg, S)
    Gs["q_norm"], Gs["k_norm"] = dqg.reshape(1, HEAD), dkg.reshape(1, HEAD)
    dpqkv = jnp.concatenate([_from_blocks(dq_b, S), _from_blocks(dkc_b + _next_block(dkp_b), S),
                             _from_blocks(dvc_b + _next_block(dvp_b), S)], axis=1).astype(bf16)
    dh = [_mm("dh_rkv", dprkv, W["in_rkvT"], "nn"), _mm("dh_lora", dplora, W["in_loraT"], "nn"),
          _mm("dh_qkv", dpqkv, W["in_qkvT"], "nn"), _mm("dh_gate", dpgate, W["in_gateT"], "nn")]
    dW_rkv = _mm("dw_rkv", dprkv, h, "tn", out_dtype=GRAD_WIRE)
    dW_lora = _mm("dw_lora", dplora, h, "tn", out_dtype=GRAD_WIRE)
    dW_qkv = _mm("dw_qkv", dpqkv, h, "tn", out_dtype=GRAD_WIRE)
    dW_gate = _mm("dw_gate", dpgate, h, "tn", out_dtype=GRAD_WIRE)
    o1, o2 = LORA_PAD[0], LORA_PAD[0] + LORA_PAD[1]
    G["inT"] = jnp.concatenate([dW_rkv, dW_lora[:LORA_W[0]], dW_lora[o1:o1 + LORA_W[1]], dW_lora[o2:o2 + LORA_W[2]],
                                dW_qkv, dW_gate], axis=0)
    dx1, dx1_b, Gs["mix_norm"] = _rowwise("mix_drms", functools.partial(_f_rms_bwd, 4), [*dh, x1, dx2],
                                          [P["mix_norm"]], [(D, f32), (D, bf16)], [(1, D)])
    dx0, _, Gs["ffn1_norm"], G["f1_iT"], G["f1_o"] = _ffn_bwd("f1", dx1, dx1_b, x0, n1, u1, act1, P["ffn1_norm"],
                                                             W["f1_iT"], W["f1_o"])
    return loss[0, 0], dx0, G, Gs


def _peer(k):
    x, y, c = lax.axis_index("x"), lax.axis_index("y"), lax.axis_index("c")
    px = 1 - x if k & 4 else x
    py = 1 - y if k & 2 else y
    pc = 1 - c if k & 1 else c
    return (px, py, pc), 4 * px + 2 * py + pc


def _all_gather(pack):
    R, C = pack.shape

    def body(x_ref, out_ref, send_sems, recv_sems, local_sem):
        x, y, c = lax.axis_index("x"), lax.axis_index("y"), lax.axis_index("c")
        me, sibling = (x, y, c), (x, y, 1 - c)
        chips = [(1 - x, y), (x, 1 - y), (1 - x, 1 - y)]

        def slot(px, py, pc):
            return out_ref.at[4 * px + 2 * py + pc]

        def copy(k, block, to, src=None):
            return pltpu.make_async_remote_copy(
                src_ref=slot(*block) if src is None else src, dst_ref=slot(*block), send_sem=send_sems.at[k],
                recv_sem=recv_sems.at[k], device_id=to, device_id_type=MESH)

        mine = pltpu.make_async_copy(x_ref, slot(*me), local_sem)
        mine.start()
        first = [copy(0, me, sibling, src=x_ref)]
        first += [copy(1 + j, me, (*chip, c), src=x_ref) for j, chip in enumerate(chips)]
        for cp in first:
            cp.start()
        passed = [copy(4 + j, (*chip, c), sibling) for j, chip in enumerate(chips)]
        for j, chip in enumerate(chips):
            copy(1 + j, (*chip, c), me).wait_recv()
            passed[j].start()
        copy(0, sibling, me).wait_recv()
        for j, chip in enumerate(chips):
            copy(4 + j, (*chip, 1 - c), me).wait_recv()
        for cp in first + passed:
            cp.wait_send()
        mine.wait()

    return pl.pallas_call(
        body, name="weight_all_gather", out_shape=jax.ShapeDtypeStruct((N_DEV, R, C), pack.dtype),
        in_specs=[pl.BlockSpec(memory_space=pl.ANY)], out_specs=pl.BlockSpec(memory_space=pl.ANY),
        scratch_shapes=[pltpu.SemaphoreType.DMA((N_DEV - 1,)), pltpu.SemaphoreType.DMA((N_DEV - 1,)),
                        pltpu.SemaphoreType.DMA(())],
    )(pack)


N_CHIP = 4


def _grad_pair(pieces):
    n = len(pieces)
    C = pieces[0].shape[2]
    rows = [p.shape[1] for p in pieces]
    offs = [sum(rows[:i]) for i in range(n)]
    R = sum(rows)

    def body(*refs):
        g_refs, (other_ref, send_sems, recv_sems) = refs[:n], refs[n:]
        x, y, c = lax.axis_index("x"), lax.axis_index("y"), lax.axis_index("c")
        copies = []
        for i, g_ref in enumerate(g_refs):
            for k in range(N_CHIP):
                cp = pltpu.make_async_remote_copy(
                    src_ref=g_ref.at[4 * (k // 2) + 2 * (k % 2) + 1 - c], dst_ref=other_ref.at[k, pl.ds(offs[i], rows[i])],
                    send_sem=send_sems.at[i * N_CHIP + k], recv_sem=recv_sems.at[i * N_CHIP + k],
                    device_id=(x, y, 1 - c), device_id_type=MESH)
                cp.start()
                copies.append(cp)
        for cp in copies:
            cp.wait()

    return pl.pallas_call(
        body, name="grad_pair", out_shape=jax.ShapeDtypeStruct((N_CHIP, R, C), pieces[0].dtype),
        in_specs=[pl.BlockSpec(memory_space=pl.ANY)] * n, out_specs=pl.BlockSpec(memory_space=pl.ANY),
        scratch_shapes=[pltpu.SemaphoreType.DMA((n * N_CHIP,))] * 2,
    )(*pieces)


def _pair_add(pieces, other, c):
    n = len(pieces)
    C = pieces[0].shape[2]
    nblk = [p.shape[1] // PACK_BLOCK for p in pieces]
    lo = [sum(nblk[:i]) for i in range(n)]
    R = sum(nblk) * PACK_BLOCK

    def body(c_ref, *refs):
        g_refs, o_ref, out_ref = refs[:n], refs[n], refs[n + 1]
        rb = pl.program_id(1)
        for i in range(n):
            @pl.when(jnp.logical_and(rb >= lo[i], rb < lo[i] + nblk[i]))
            def _(g_ref=g_refs[i]):
                out_ref[...] = (g_ref[...].astype(f32) + o_ref[...].astype(f32)).astype(out_ref.dtype)

    def piece_spec(i):
        return pl.BlockSpec((1, None, PACK_BLOCK, C),
                            lambda k, rb, c_ref: (k, c_ref[0], jnp.clip(rb - lo[i], 0, nblk[i] - 1), 0))

    blk = pl.BlockSpec((1, PACK_BLOCK, C), lambda k, rb, c_ref: (k, rb, 0))
    return pl.pallas_call(
        body, name="pair_add",
        grid_spec=pltpu.PrefetchScalarGridSpec(
            num_scalar_prefetch=1, grid=(N_CHIP, R // PACK_BLOCK),
            in_specs=[piece_spec(i) for i in range(n)] + [blk], out_specs=blk),
        out_shape=jax.ShapeDtypeStruct((N_CHIP, R, C), other.dtype),
        compiler_params=_cparams(("arbitrary", "arbitrary")),
    )(c, *[p.reshape(N_CHIP, 2, p.shape[1], C) for p in pieces], other)


def _grad_cross(part):
    _, R, C = part.shape

    def body(p_ref, out_ref, send_sems, recv_sems):
        x, y, c = lax.axis_index("x"), lax.axis_index("y"), lax.axis_index("c")
        flips = [(1, 0), (0, 1), (1, 1)]
        copies = []
        for j, (fx, fy) in enumerate(flips):
            px = 1 - x if fx else x
            py = 1 - y if fy else y
            cp = pltpu.make_async_remote_copy(src_ref=p_ref.at[2 * px + py], dst_ref=out_ref.at[j],
                                              send_sem=send_sems.at[j], recv_sem=recv_sems.at[j],
                                              device_id=(px, py, c), device_id_type=MESH)
            cp.start()
            copies.append(cp)
        for cp in copies:
            cp.wait()

    return pl.pallas_call(
        body, name="grad_cross", out_shape=jax.ShapeDtypeStruct((3, R, C), part.dtype),
        in_specs=[pl.BlockSpec(memory_space=pl.ANY)], out_specs=pl.BlockSpec(memory_space=pl.ANY),
        scratch_shapes=[pltpu.SemaphoreType.DMA((3,)), pltpu.SemaphoreType.DMA((3,))],
    )(part)


def _grad_sum(part, recv, my_chip, tr):
    _, R, C = part.shape

    def body(chip_ref, p_ref, r_ref, o_ref):
        acc = p_ref[0].astype(f32)
        for j in range(3):
            acc = acc + r_ref[j].astype(f32)
        o_ref[...] = acc

    return pl.pallas_call(
        body, name="grad_sum",
        grid_spec=pltpu.PrefetchScalarGridSpec(
            num_scalar_prefetch=1, grid=(R // tr,),
            in_specs=[pl.BlockSpec((1, tr, C), lambda i, chip_ref: (chip_ref[0], i, 0)),
                      pl.BlockSpec((3, tr, C), lambda i, chip_ref: (0, i, 0))],
            out_specs=pl.BlockSpec((tr, C), lambda i, chip_ref: (i, 0))),
        out_shape=jax.ShapeDtypeStruct((R, C), f32),
        compiler_params=_cparams(("arbitrary",)),
    )(my_chip, part, recv)


def _small_all_reduce(small):
    R, C = small.shape

    def body(x_ref, o_ref, buf, send_sems, recv_sems):
        _, me = _peer(0)
        buf[me] = x_ref[...]
        sends = []
        for k in range(1, N_DEV):
            dev, _ = _peer(k)
            cp = pltpu.make_async_remote_copy(src_ref=x_ref, dst_ref=buf.at[me], send_sem=send_sems.at[k - 1],
                                              recv_sem=recv_sems.at[k - 1], device_id=dev, device_id_type=MESH)
            cp.start()
            sends.append(cp)
        for k in range(1, N_DEV):
            dev, idx = _peer(k)
            pltpu.make_async_remote_copy(src_ref=x_ref, dst_ref=buf.at[idx], send_sem=send_sems.at[k - 1],
                                         recv_sem=recv_sems.at[k - 1], device_id=dev, device_id_type=MESH).wait_recv()
        for cp in sends:
            cp.wait_send()
        acc = buf[0]
        for i in range(1, N_DEV):
            acc = acc + buf[i]
        o_ref[...] = acc

    return pl.pallas_call(
        body, name="small_all_reduce", out_shape=jax.ShapeDtypeStruct((R, C), f32),
        in_specs=[pl.BlockSpec(memory_space=pltpu.VMEM)], out_specs=pl.BlockSpec(memory_space=pltpu.VMEM),
        scratch_shapes=[pltpu.VMEM((N_DEV, R, C), f32), pltpu.SemaphoreType.DMA((N_DEV - 1,)),
                        pltpu.SemaphoreType.DMA((N_DEV - 1,))],
    )(small)


_BIG_GROUPS = (
    (("ffn1_w_in", True),), (("ffn1_w_out", False),), (("w_in", True),),
    (("rwkv_w2", True), ("rwkv_a2", True), ("rwkv_g2", True)),
    (("w_proj_rwkv", False),), (("w_proj_attn", True),), (("w_out", False),),
    (("ffn2_w_in", True),), (("ffn2_w_out", False),))
_BIG = tuple(item for group in _BIG_GROUPS for item in group)
_SMALL = ("ffn1_norm", "mix_norm", "b_gate", "rwkv_mu", "rwkv_w0", "rwkv_a0", "rwkv_k_k", "rwkv_k_a", "rwkv_r_k",
          "rwkv_ln_w", "rwkv_ln_b", "attn_q_norm", "attn_k_norm", "ffn2_norm")


def _pack_layout(like):
    items, groups, off = {}, [], 0
    for group in _BIG_GROUPS:
        start = off
        for name, _ in group:
            shp = like[name].shape
            n = shp[0] * shp[1] // D
            items[name] = (off, n)
            off += n
        off = -(-off // PACK_BLOCK) * PACK_BLOCK
        groups.append((start, off - start))
    assert off == PACK_ROWS, off
    return items, groups


def _pack_big(shards):
    items, _ = _pack_layout(shards)
    parts, at = [], 0
    for name, tr in _BIG:
        off, n = items[name]
        t = shards[name]
        if off > at:
            parts.append(jnp.zeros((off - at, D), t.dtype))
        parts.append((t.T if tr else t).reshape(n, D))
        at = off + n
    parts.append(jnp.zeros((PACK_ROWS - at, D), parts[0].dtype))
    return jnp.concatenate(parts, axis=0)


def _unpack_big(pack, like):
    items, _ = _pack_layout(like)
    out = {}
    for name, tr in _BIG:
        off, n = items[name]
        shp = like[name].shape
        t = pack[off:off + n]
        out[name] = t.reshape(shp[1], shp[0]).T if tr else t.reshape(shp)
    return out


def _small_rows(name, t):
    flat = t.reshape(-1)
    pad = (-flat.shape[0]) % D
    return jnp.pad(flat, (0, pad)).reshape(-1, D)


def _pack_small(vals):
    parts = [_small_rows(n, vals[n]) for n in _SMALL]
    used = sum(p.shape[0] for p in parts)
    parts.append(jnp.zeros((SMALL_ROWS - used, D), f32))
    return jnp.concatenate(parts, axis=0)


def _unpack_small(pack, like):
    out, off = {}, 0
    for n in _SMALL:
        size = like[n].size
        rows = -(-size // D)
        out[n] = pack[off:off + rows].reshape(-1)[:size].reshape(like[n].shape)
        off += rows
    return out


def _build_W(full):
    inT = full["w_in"]
    z64, z96 = jnp.zeros((64, D), inT.dtype), jnp.zeros((96, D), inT.dtype)
    return {
        "f1_iT": full["ffn1_w_in"], "f1_o": full["ffn1_w_out"], "f2_iT": full["ffn2_w_in"], "f2_o": full["ffn2_w_out"],
        "in_rkvT": inT[:3 * D],
        "in_loraT": jnp.concatenate([inT[3072:3136], z64, inT[3136:3200], z64, inT[3200:3360], z96], axis=0),
        "in_qkvT": inT[3360:3360 + 3 * ATTN_W], "in_gateT": inT[3360 + 3 * ATTN_W:],
        "w2p": jnp.concatenate([full["rwkv_w2"].T, z64], axis=0),
        "a2p": jnp.concatenate([full["rwkv_a2"].T, z64], axis=0),
        "g2p": jnp.concatenate([full["rwkv_g2"].T, z96], axis=0),
        "pr": full["w_proj_rwkv"], "paT": full["w_proj_attn"], "out": full["w_out"],
    }


def _build_P(Wl):
    mu = Wl["rwkv_mu"]
    z64f, z96f = jnp.zeros((1, 64), f32), jnp.zeros((1, 96), f32)
    return {
        "ffn1_norm": Wl["ffn1_norm"][None], "mix_norm": Wl["mix_norm"][None], "ffn2_norm": Wl["ffn2_norm"][None],
        "b_gate": Wl["b_gate"][None], "mu_rkv": mu[None, :3 * D],
        "mu_lora": jnp.concatenate([mu[None, 3072:3136], z64f, mu[None, 3136:3200], z64f, mu[None, 3200:3360], z96f], axis=1),
        "w0": Wl["rwkv_w0"][None], "a0": Wl["rwkv_a0"][None], "k_k": Wl["rwkv_k_k"][None], "k_a": Wl["rwkv_k_a"][None],
        "r_k": Wl["rwkv_r_k"].reshape(1, D), "ln_w": Wl["rwkv_ln_w"][None], "ln_b": Wl["rwkv_ln_b"][None],
        "q_norm": Wl["attn_q_norm"][None], "k_norm": Wl["attn_k_norm"][None],
    }


def kernel(x, ffn1_norm, ffn1_w_in, ffn1_w_out, mix_norm, w_in, b_gate, rwkv_mu, rwkv_w0, rwkv_w2, rwkv_a0, rwkv_a2, rwkv_g2, rwkv_k_k, rwkv_k_a, rwkv_r_k, rwkv_ln_w, rwkv_ln_b, attn_q_norm, attn_k_norm, w_proj_rwkv, w_proj_attn, w_out, ffn2_norm, ffn2_w_in, ffn2_w_out, loss_target, m_ffn1_norm, m_ffn1_w_in, m_ffn1_w_out, m_mix_norm, m_w_in, m_b_gate, m_rwkv_mu, m_rwkv_w0, m_rwkv_w2, m_rwkv_a0, m_rwkv_a2, m_rwkv_g2, m_rwkv_k_k, m_rwkv_k_a, m_rwkv_r_k, m_rwkv_ln_w, m_rwkv_ln_b, m_attn_q_norm, m_attn_k_norm, m_w_proj_rwkv, m_w_proj_attn, m_w_out, m_ffn2_norm, m_ffn2_w_in, m_ffn2_w_out, v_ffn1_norm, v_ffn1_w_in, v_ffn1_w_out, v_mix_norm, v_w_in, v_b_gate, v_rwkv_mu, v_rwkv_w0, v_rwkv_w2, v_rwkv_a0, v_rwkv_a2, v_rwkv_g2, v_rwkv_k_k, v_rwkv_k_a, v_rwkv_r_k, v_rwkv_ln_w, v_rwkv_ln_b, v_attn_q_norm, v_attn_k_norm, v_w_proj_rwkv, v_w_proj_attn, v_w_out, v_ffn2_norm, v_ffn2_w_in, v_ffn2_w_out):
    names = ("ffn1_norm", "ffn1_w_in", "ffn1_w_out", "mix_norm", "w_in", "b_gate", "rwkv_mu", "rwkv_w0", "rwkv_w2",
             "rwkv_a0", "rwkv_a2", "rwkv_g2", "rwkv_k_k", "rwkv_k_a", "rwkv_r_k", "rwkv_ln_w", "rwkv_ln_b",
             "attn_q_norm", "attn_k_norm", "w_proj_rwkv", "w_proj_attn", "w_out", "ffn2_norm", "ffn2_w_in", "ffn2_w_out")
    w_all = (ffn1_norm, ffn1_w_in, ffn1_w_out, mix_norm, w_in, b_gate, rwkv_mu, rwkv_w0, rwkv_w2, rwkv_a0, rwkv_a2,
             rwkv_g2, rwkv_k_k, rwkv_k_a, rwkv_r_k, rwkv_ln_w, rwkv_ln_b, attn_q_norm, attn_k_norm, w_proj_rwkv,
             w_proj_attn, w_out, ffn2_norm, ffn2_w_in, ffn2_w_out)
    m_all = (m_ffn1_norm, m_ffn1_w_in, m_ffn1_w_out, m_mix_norm, m_w_in, m_b_gate, m_rwkv_mu, m_rwkv_w0, m_rwkv_w2,
             m_rwkv_a0, m_rwkv_a2, m_rwkv_g2, m_rwkv_k_k, m_rwkv_k_a, m_rwkv_r_k, m_rwkv_ln_w, m_rwkv_ln_b,
             m_attn_q_norm, m_attn_k_norm, m_w_proj_rwkv, m_w_proj_attn, m_w_out, m_ffn2_norm, m_ffn2_w_in, m_ffn2_w_out)
    v_all = (v_ffn1_norm, v_ffn1_w_in, v_ffn1_w_out, v_mix_norm, v_w_in, v_b_gate, v_rwkv_mu, v_rwkv_w0, v_rwkv_w2,
             v_rwkv_a0, v_rwkv_a2, v_rwkv_g2, v_rwkv_k_k, v_rwkv_k_a, v_rwkv_r_k, v_rwkv_ln_w, v_rwkv_ln_b,
             v_attn_q_norm, v_attn_k_norm, v_w_proj_rwkv, v_w_proj_attn, v_w_out, v_ffn2_norm, v_ffn2_w_in, v_ffn2_w_out)
    Wl = {n: t[0] for n, t in zip(names, w_all)}
    Ml = {n: t[0] for n, t in zip(names, m_all)}
    Vl = {n: t[0] for n, t in zip(names, v_all)}
    big = [n for n, _ in _BIG]

    w_pack = _pack_big({n: Wl[n] for n in big})
    gathered = _all_gather(w_pack.astype(bf16))
    items, groups = _pack_layout(Wl)
    full = {}
    for n, tr in _BIG:
        shp = Wl[n].shape
        off, rows = items[n]
        t = gathered[:, off:off + rows]
        r_loc, c_loc = (shp[1], shp[0]) if tr else shp
        full[n] = t.reshape(N_DEV * r_loc, c_loc)
    W, P = _build_W(full), _build_P(Wl)

    loss_local, dx0, G, Gs = _local_step(x[0], loss_target[0], W, P)

    g_full = {"ffn1_w_in": G["f1_iT"], "ffn1_w_out": G["f1_o"], "w_in": G["inT"], "rwkv_w2": G["w2T"],
              "rwkv_a2": G["a2T"], "rwkv_g2": G["g2T"], "w_proj_rwkv": G["pr"], "w_proj_attn": G["paT"],
              "w_out": G["out"], "ffn2_w_in": G["f2_iT"], "ffn2_w_out": G["f2_o"]}
    pieces = []
    for group, (_, rows_pad) in zip(_BIG_GROUPS, groups):
        parts = [g_full[n].astype(GRAD_WIRE).reshape(N_DEV, items[n][1], D) for n, _ in group]
        piece = parts[0] if len(parts) == 1 else jnp.concatenate(parts, axis=1)
        if rows_pad > piece.shape[1]:
            piece = jnp.pad(piece, ((0, 0), (0, rows_pad - piece.shape[1]), (0, 0)))
        pieces.append(piece)
    my_c = lax.axis_index("c").astype(jnp.int32).reshape(1)
    my_chip = (2 * lax.axis_index("x") + lax.axis_index("y")).astype(jnp.int32).reshape(1)
    chip_part = _pair_add(pieces, _grad_pair(pieces), my_c)
    g_pack = _grad_sum(chip_part, _grad_cross(chip_part), my_chip, 256)

    mu_g = Gs["mu_rkv"], Gs["mu_lora"]
    o1, o2 = LORA_PAD[0], LORA_PAD[0] + LORA_PAD[1]
    g_small_local = {
        "ffn1_norm": Gs["ffn1_norm"], "mix_norm": Gs["mix_norm"], "b_gate": Gs["b_gate"],
        "rwkv_mu": jnp.concatenate([mu_g[0], mu_g[1][:, :64], mu_g[1][:, o1:o1 + 64], mu_g[1][:, o2:o2 + 160]], axis=1),
        "rwkv_w0": Gs["w0"], "rwkv_a0": Gs["a0"], "rwkv_k_k": Gs["k_k"], "rwkv_k_a": Gs["k_a"], "rwkv_r_k": Gs["r_k"],
        "rwkv_ln_w": Gs["ln_w"], "rwkv_ln_b": Gs["ln_b"], "attn_q_norm": Gs["q_norm"], "attn_k_norm": Gs["k_norm"],
        "ffn2_norm": Gs["ffn2_norm"]}
    gs_pack = _small_all_reduce(_pack_small(g_small_local))

    g_big = _unpack_big(g_pack, Wl)
    out_g, out_d, out_m, out_v = dict(g_big), {}, {}, {}
    for n in big:
        cols = Wl[n].shape[1]
        out_d[n], out_m[n], out_v[n] = _rowwise(f"adamw_{n}", _f_adamw, [Wl[n], g_big[n], Ml[n], Vl[n]], [],
                                                 [(cols, f32)] * 3)
    ds_pack, ms_pack, vs_pack = _rowwise(
        "adamw_small", _f_adamw, [_pack_small(Wl), gs_pack, _pack_small(Ml), _pack_small(Vl)], [], [(D, f32)] * 3)
    for out, pack in ((out_g, gs_pack), (out_d, ds_pack), (out_m, ms_pack), (out_v, vs_pack)):
        out.update(_unpack_small(pack, Wl))

    loss = lax.psum(loss_local, ("x", "y", "c"))
    return (loss, dx0[None], *[out_g[n][None] for n in names], *[out_d[n][None] for n in names],
            *[out_m[n][None] for n in names], *[out_v[n][None] for n in names])
```

```python
import functools

import jax
import jax.numpy as jnp
from jax import lax
from jax.experimental import pallas as pl
from jax.experimental.pallas import tpu as pltpu

f32 = jnp.float32
bf16 = jnp.bfloat16
HI = lax.Precision.HIGHEST
MESH = pl.DeviceIdType.MESH

N_DEV = 8
D = 1024
D_FF = 2816
HEAD = 64
RW_HEADS = 16
ATTN_PAIRS = ((128, 1), (512, 4), (2048, 16))
ATTN_BLK = 128
HEADS_PER_GROUP = 4
ATTN_W = 768
LORA_PAD = (128, 128, 256)
LORA_W = (64, 64, 160)
GN_EPS = 64e-5
RMS_EPS = 1e-6
NEG_INF = -1e30
WKV_T = 64
GRAD_WIRE = bf16
PACK_BLOCK = 128
PACK_ROWS = 3840
SMALL_ROWS = 24
VMEM_LIMIT = 56 * 1024 * 1024

ADAM_LR, ADAM_B1, ADAM_B2, ADAM_EPS, ADAM_WD, ADAM_STEP = 0.001, 0.9, 0.999, 1e-08, 0.01, 10


def _cparams(sem):
    return pltpu.CompilerParams(dimension_semantics=sem, vmem_limit_bytes=VMEM_LIMIT)


def _pick(n, cands):
    for c in cands:
        if n % c == 0:
            return c
    return n


HALO = 8


def _rowwise(name, fn, rows, params, outs, accs=(), tm=256, halos=(), carries=(), reverse=False):
    S = rows[0].shape[0]
    tm = min(tm, S)
    while S % tm:
        tm -= 8
    nb = S // tm
    n_in = len(rows) + len(params) + len(halos)
    n_out = len(outs)
    n_acc = len(accs)
    n_car = len(carries)

    def blk_of(i):
        return nb - 1 - i if reverse else i

    def body(*refs):
        step = pl.program_id(0)
        carry_refs = refs[n_in + n_out + n_acc:]
        if n_car:
            @pl.when(step == 0)
            def _():
                for c_ref in carry_refs:
                    c_ref[...] = jnp.zeros(c_ref.shape, f32)
        args = [r[...] for r in refs[:n_in]] + [c[...] for c in carry_refs]
        res = fn(*args, blk=blk_of(step)) if (halos or carries) else fn(*args)
        if not isinstance(res, (tuple, list)):
            res = (res,)
        out_refs = refs[n_in:n_in + n_out + n_acc]
        for j in range(n_out):
            out_refs[j][...] = res[j].astype(out_refs[j].dtype)
        if n_acc:
            @pl.when(step == 0)
            def _():
                for j in range(n_acc):
                    out_refs[n_out + j][...] = jnp.zeros(out_refs[n_out + j].shape, f32)
            for j in range(n_acc):
                out_refs[n_out + j][...] += res[n_out + j]
        for j in range(n_car):
            carry_refs[j][...] = res[n_out + n_acc + j]

    in_specs = [pl.BlockSpec((tm, a.shape[1]), lambda i: (blk_of(i), 0)) for a in rows]
    in_specs += [pl.BlockSpec(p.shape, lambda i, nd=p.ndim: (0,) * nd) for p in params]
    in_specs += [pl.BlockSpec((HALO, rows[h].shape[1]), lambda i: (jnp.maximum(blk_of(i) * (tm // HALO) - 1, 0), 0))
                 for h in halos]
    out_specs = [pl.BlockSpec((tm, w), lambda i: (blk_of(i), 0)) for w, _ in outs]
    out_specs += [pl.BlockSpec(s, lambda i: (0, 0)) for s in accs]
    out_shape = [jax.ShapeDtypeStruct((S, w), dt) for w, dt in outs]
    out_shape += [jax.ShapeDtypeStruct(s, f32) for s in accs]
    res = pl.pallas_call(
        body, name=name, grid=(nb,), in_specs=in_specs, out_specs=out_specs, out_shape=out_shape,
        scratch_shapes=[pltpu.VMEM(s, f32) for s in carries],
        compiler_params=_cparams(("arbitrary",)),
    )(*rows, *params, *[rows[h] for h in halos])
    return res


MM_VMEM_BUDGET = 40 * 1024 * 1024
MM_STEP_US = 0.35
MM_FLOPS_PER_US = 9.0e8
MM_HBM_BYTES_PER_US = 3.0e6


def _tile_options(n, cap):
    opts = [d for d in range(128, min(n, cap) + 1, 128) if n % d == 0]
    return opts or [n]


def _mm_tiles(M, N, K, sa, sb, so):
    best, best_cost = None, None
    for tm in _tile_options(M, 2048):
        for tn in _tile_options(N, 2048):
            for tk in _tile_options(K, 4096):
                vmem = 2 * (tm * tk * sa + tk * tn * sb) + 2 * tm * tn * so + (tm * tn * 4 if tk < K else 0)
                if vmem > MM_VMEM_BUDGET:
                    continue
                steps = (M // tm) * (N // tn) * (K // tk)
                traffic = M * K * sa * (N // tn) + K * N * sb * (M // tm) + M * N * so
                cost = (max(2.0 * M * N * K / MM_FLOPS_PER_US, traffic / MM_HBM_BYTES_PER_US) + steps * MM_STEP_US
                        + (tm * tk * sa + tk * tn * sb) / MM_HBM_BYTES_PER_US)
                if best_cost is None or cost < best_cost:
                    best, best_cost = (tm, tn, tk), cost
    return best


def _mm(name, a, b, mode, out_dtype=f32, scale=None):
    if mode == "nn":
        (M, K), (_, N) = a.shape, b.shape
    elif mode == "nt":
        (M, K), (N, _) = a.shape, b.shape
    else:
        (K, M), (_, N) = a.shape, b.shape
    tm, tn, tk = _mm_tiles(M, N, K, a.dtype.itemsize, b.dtype.itemsize, jnp.dtype(out_dtype).itemsize)
    nk = K // tk
    if mode == "nn":
        a_spec = pl.BlockSpec((tm, tk), lambda i, j, k: (i, k))
        b_spec = pl.BlockSpec((tk, tn), lambda i, j, k: (k, j))
        dims = (((1,), (0,)), ((), ()))
    elif mode == "nt":
        a_spec = pl.BlockSpec((tm, tk), lambda i, j, k: (i, k))
        b_spec = pl.BlockSpec((tn, tk), lambda i, j, k: (j, k))
        dims = (((1,), (1,)), ((), ()))
    else:
        a_spec = pl.BlockSpec((tk, tm), lambda i, j, k: (k, i))
        b_spec = pl.BlockSpec((tk, tn), lambda i, j, k: (k, j))
        dims = (((0,), (0,)), ((), ()))

    def finish(acc):
        return acc if scale is None else acc * scale

    def body(a_ref, b_ref, o_ref, *scratch):
        part = lax.dot_general(a_ref[...].astype(bf16), b_ref[...].astype(bf16), dims,
                               preferred_element_type=f32)
        if nk == 1:
            o_ref[...] = finish(part).astype(o_ref.dtype)
        else:
            acc_ref = scratch[0]
            k = pl.program_id(2)

            @pl.when(k == 0)
            def _():
                acc_ref[...] = part

            @pl.when(k > 0)
            def _():
                acc_ref[...] += part

            @pl.when(k == nk - 1)
            def _():
                o_ref[...] = finish(acc_ref[...]).astype(o_ref.dtype)

    return pl.pallas_call(
        body, name=name, grid=(M // tm, N // tn, nk), in_specs=[a_spec, b_spec],
        out_specs=pl.BlockSpec((tm, tn), lambda i, j, k: (i, j)),
        out_shape=jax.ShapeDtypeStruct((M, N), out_dtype),
        scratch_shapes=[] if nk == 1 else [pltpu.VMEM((tm, tn), f32)],
        compiler_params=_cparams(("parallel", "parallel", "arbitrary")),
    )(a, b)


def _sp(x):
    hi = x.astype(bf16)
    return hi, (x - hi.astype(f32)).astype(bf16)


def _cat(parts):
    return tuple(jnp.concatenate(p, axis=1) for p in zip(*parts))


def _bmm(eq, a, b):
    (ah, al), (bh, bl) = a, b
    dot = functools.partial(jnp.einsum, eq, preferred_element_type=f32)
    return dot(ah, bh) + (dot(ah, bl) + dot(al, bh))


def _tri_dot(eq, tri, x):
    h1 = x.astype(bf16)
    r1 = x - h1.astype(f32)
    h2 = r1.astype(bf16)
    h3 = (r1 - h2.astype(f32)).astype(bf16)
    dot = functools.partial(jnp.einsum, eq, preferred_element_type=f32)
    return dot(tri, h1) + (dot(tri, h2) + dot(tri, h3))


def _tri_masks(T):
    ti = lax.broadcasted_iota(jnp.int32, (T, T), 0)
    si = lax.broadcasted_iota(jnp.int32, (T, T), 1)
    return ti >= si, ti > si


def _wkv_prep(r, lw, k, kkr, a):
    H, T, _ = r.shape
    low_i, low_s = _tri_masks(T)
    nrm = jnp.sqrt(jnp.sum(kkr * kkr, axis=-1, keepdims=True))
    den = jnp.maximum(nrm, 1e-12)
    kk = kkr / den
    tri = jnp.broadcast_to(low_i.astype(bf16)[None], (H, T, T))
    cl = _tri_dot("hts,hsn->htn", tri, lw)
    c = jnp.exp(cl)
    cprev = jnp.exp(cl - lw)
    cinv = jnp.exp(-cl)
    bt, kt = _sp(kk * a * cinv), _sp(k * cinv)
    L = _cat([_sp(r * c), _sp(-kk * cprev)])
    Mb = _bmm("htn,hsn->hts", L, bt)
    Mk = _bmm("htn,hsn->hts", L, kt)
    A_rb = jnp.where(low_i[None], Mb[:, :T], 0.0)
    A_ab = jnp.where(low_s[None], Mb[:, T:], 0.0)
    Mk = jnp.concatenate([jnp.where(low_i[None], Mk[:, :T], 0.0), jnp.where(low_s[None], Mk[:, T:], 0.0)], axis=1)
    return dict(kk=kk, den=den, nrm=nrm, c=c, cprev=cprev, cinv=cinv, L=L, kt=kt, bt=bt,
                A_ab=A_ab, A_rb=A_rb, Mk=Mk, cT=c[:, T - 1:T, :])


def _tri_inverse(A):
    T = A.shape[-1]
    eye = (lax.broadcasted_iota(jnp.int32, (T, T), 0) == lax.broadcasted_iota(jnp.int32, (T, T), 1)).astype(f32)
    inv = eye[None] + A
    X = A
    n = 1
    while 2 * n < T:
        Xs = _sp(X)
        X = _bmm("hts,hsu->htu", Xs, Xs)
        inv = inv + _bmm("hts,hsu->htu", _sp(inv), _sp(X))
        n *= 2
    return inv


def _wkv_chunk_fwd(S0, r, lw, k, v, kkr, a):
    T = r.shape[1]
    q = _wkv_prep(r, lw, k, kkr, a)
    inv = _tri_inverse(q["A_ab"])
    vs = _sp(v)
    P = _bmm("htk,hvk->htv", q["L"], _sp(S0)) + _bmm("hts,hsv->htv", _sp(q["Mk"]), vs)
    U = _bmm("hts,hsv->htv", _sp(inv), _sp(P[:, T:]))
    Us = _sp(U)
    Y = P[:, :T] + _bmm("hts,hsv->htv", _sp(q["A_rb"]), Us)
    S1 = (S0 + _bmm("htv,htk->hvk", _cat([Us, vs]), _cat([q["bt"], q["kt"]]))) * q["cT"]
    return Y, U, inv, S1


def _wkv_chunk_bwd(S0, Hin, Q, r, lw, k, v, kkr, a, U, inv, dY):
    H, T, _ = r.shape
    low_i, low_s = _tri_masks(T)
    q = _wkv_prep(r, lw, k, kkr, a)
    L, kt, bt = q["L"], q["kt"], q["bt"]
    R = _cat([bt, kt])
    Hh = Hin * q["cT"]
    Hs, S0s, dYs, vs, Us = _sp(Hh), _sp(S0), _sp(dY), _sp(v), _sp(U)
    RH = _bmm("htk,hvk->htv", R, Hs)
    Z = _bmm("hst,hsv->htv", _sp(inv), _sp(RH[:, :T] + _bmm("hst,hsv->htv", _sp(q["A_rb"]), dYs)))
    DZ = _cat([dYs, _sp(Z)])
    both = jnp.concatenate([jnp.broadcast_to(low_i[None], (1, T, T)), jnp.broadcast_to(low_s[None], (1, T, T))], axis=1)
    NU = _sp(jnp.where(both, _bmm("htv,hsv->hts", DZ, Us), 0.0))
    NV = _sp(jnp.where(both, _bmm("htv,hsv->hts", DZ, vs), 0.0))
    ra = _bmm("htv,hvk->htk", DZ, S0s) + _bmm("hts,hsk->htk", NU, bt) + _bmm("hts,hsk->htk", NV, kt)
    dr = ra[:, :T] * q["c"]
    da = ra[:, T:] * q["cprev"]
    dv = RH[:, T:] + _bmm("hst,hsv->htv", _sp(q["Mk"]), DZ)
    VH = _bmm("htv,hvk->htk", _cat([vs, Us]), Hs)
    dk = (VH[:, :T] + _bmm("hst,hsk->htk", NV, L)) * q["cinv"]
    db = (VH[:, T:] + _bmm("hst,hsk->htk", NU, L)) * q["cinv"]
    H0 = Hh + _bmm("htv,htk->hvk", DZ, L)
    kk = q["kk"]
    e = r * dr - kk * a * db - k * dk
    f = -kk * da
    tri_i = jnp.broadcast_to(low_i.astype(bf16)[None], (H, T, T))
    tri_s = jnp.broadcast_to(low_s.astype(bf16)[None], (H, T, T))
    dlw = _tri_dot("hst,hsn->htn", tri_i, e) + _tri_dot("hst,hsn->htn", tri_s, f) + Q
    Qn = Q + jnp.sum(e + f, axis=1, keepdims=True)
    dkk = db * a - da
    dasig = db * kk
    proj = jnp.sum(dkk * kk, axis=-1, keepdims=True)
    dkkr = jnp.where(q["nrm"] > 1e-12, dkk - kk * proj, dkk) / q["den"]
    return dr, dlw, dk, dv, dkkr, dasig, H0, Qn


def _heads(ref):
    return jnp.stack([ref[:, h * HEAD:(h + 1) * HEAD] for h in range(RW_HEADS)], axis=0)


def _put_heads(ref, val):
    for h in range(RW_HEADS):
        ref[:, h * HEAD:(h + 1) * HEAD] = val[h]


def _wkv_fwd(r, lw, k, v, kkr, a, g, r_k, ln_w, ln_b):
    S = r.shape[0]
    H, N, T = RW_HEADS, HEAD, WKV_T
    nc = S // T

    def body(r_ref, lw_ref, k_ref, v_ref, kkr_ref, a_ref, g_ref, rk_ref, lnw_ref, lnb_ref,
             y_ref, yg_ref, wkv_ref, u_ref, inv_ref, s0_ref, state):
        @pl.when(pl.program_id(0) == 0)
        def _():
            state[...] = jnp.zeros(state.shape, f32)

        S0 = state[...]
        s0_ref[0] = S0
        rr, kk2, vv = _heads(r_ref), _heads(k_ref), _heads(v_ref)
        Y, U, inv, S1 = _wkv_chunk_fwd(S0, rr, _heads(lw_ref), kk2, vv, _heads(kkr_ref), _heads(a_ref))
        state[...] = S1
        wkv_ref[...] = Y
        u_ref[...] = U
        inv_ref[...] = inv
        mean = jnp.mean(Y, axis=-1, keepdims=True)
        var = jnp.mean(jnp.square(Y - mean), axis=-1, keepdims=True)
        yn = (Y - mean) * lax.rsqrt(var + GN_EPS)
        bonus = jnp.sum(rr * kk2 * rk_ref[...], axis=-1, keepdims=True) * vv
        _put_heads(y_ref, yn * lnw_ref[...] + lnb_ref[...] + bonus)
        yg_ref[...] = (y_ref[...] * g_ref[...]).astype(yg_ref.dtype)

    tok = pl.BlockSpec((T, H * N), lambda i: (i, 0))
    blk = pl.BlockSpec((H, T, N), lambda i: (0, i, 0))
    par = pl.BlockSpec((H, 1, N), lambda i: (0, 0, 0))
    seq = jax.ShapeDtypeStruct((H, S, N), f32)
    return pl.pallas_call(
        body, name="wkv_fwd", grid=(nc,), in_specs=[tok] * 7 + [par] * 3,
        out_specs=[tok, tok, blk, blk, blk, pl.BlockSpec((1, H, N, N), lambda i: (i, 0, 0, 0))],
        out_shape=[jax.ShapeDtypeStruct((S, H * N), f32), jax.ShapeDtypeStruct((S, H * N), bf16), seq, seq, seq,
                   jax.ShapeDtypeStruct((nc, H, N, N), f32)],
        scratch_shapes=[pltpu.VMEM((H, N, N), f32)],
        compiler_params=_cparams(("arbitrary",)),
    )(r, lw, k, v, kkr, a, g, r_k, ln_w, ln_b)


def _wkv_bwd(dy, g, r, lw, k, v, kkr, a, wkv, U, inv, S0s, r_k, ln_w, ln_b):
    S = r.shape[0]
    H, N, T = RW_HEADS, HEAD, WKV_T
    nc = S // T

    def body(dy_ref, g_ref, r_ref, lw_ref, k_ref, v_ref, kkr_ref, a_ref, wkv_ref, u_ref, inv_ref, s0_ref,
             rk_ref, lnw_ref, lnb_ref,
             dr_ref, dlw_ref, dk_ref, dv_ref, dkkr_ref, da_ref, drk_ref, dlnw_ref, dlnb_ref, hst, qst):
        @pl.when(pl.program_id(0) == 0)
        def _():
            hst[...] = jnp.zeros(hst.shape, f32)
            qst[...] = jnp.zeros(qst.shape, f32)
            drk_ref[...] = jnp.zeros(drk_ref.shape, f32)
            dlnw_ref[...] = jnp.zeros(dlnw_ref.shape, f32)
            dlnb_ref[...] = jnp.zeros(dlnb_ref.shape, f32)

        dya = _heads(dy_ref[...] * g_ref[...])
        rr, kk2, vv, Y = _heads(r_ref), _heads(k_ref), _heads(v_ref), wkv_ref[...]
        rk = rk_ref[...]
        s = jnp.sum(rr * kk2 * rk, axis=-1, keepdims=True)
        ds = jnp.sum(dya * vv, axis=-1, keepdims=True)
        mean = jnp.mean(Y, axis=-1, keepdims=True)
        var = jnp.mean(jnp.square(Y - mean), axis=-1, keepdims=True)
        rstd = lax.rsqrt(var + GN_EPS)
        yn = (Y - mean) * rstd
        dyn = dya * lnw_ref[...]
        dY = rstd * (dyn - jnp.mean(dyn, axis=-1, keepdims=True) - yn * jnp.mean(dyn * yn, axis=-1, keepdims=True))
        drk_ref[...] += jnp.sum(ds * rr * kk2, axis=1, keepdims=True)
        dlnw_ref[...] += jnp.sum(dya * yn, axis=1, keepdims=True)
        dlnb_ref[...] += jnp.sum(dya, axis=1, keepdims=True)
        dr, dlw, dk, dv, dkkr, dasig, H0, Qn = _wkv_chunk_bwd(
            s0_ref[0], hst[...], qst[...], rr, _heads(lw_ref), kk2, vv, _heads(kkr_ref), _heads(a_ref), u_ref[...],
            inv_ref[...], dY)
        hst[...] = H0
        qst[...] = Qn
        _put_heads(dr_ref, dr + ds * kk2 * rk)
        _put_heads(dlw_ref, dlw)
        _put_heads(dk_ref, dk + ds * rr * rk)
        _put_heads(dv_ref, dv + dya * s)
        _put_heads(dkkr_ref, dkkr)
        _put_heads(da_ref, dasig)

    tok = pl.BlockSpec((T, H * N), lambda i: (nc - 1 - i, 0))
    blk = pl.BlockSpec((H, T, N), lambda i: (0, nc - 1 - i, 0))
    par = pl.BlockSpec((H, 1, N), lambda i: (0, 0, 0))
    seq = jax.ShapeDtypeStruct((S, H * N), f32)
    pout = jax.ShapeDtypeStruct((H, 1, N), f32)
    return pl.pallas_call(
        body, name="wkv_bwd", grid=(nc,),
        in_specs=[tok] * 8 + [blk] * 3 + [pl.BlockSpec((1, H, N, N), lambda i: (nc - 1 - i, 0, 0, 0))] + [par] * 3,
        out_specs=[tok] * 6 + [par] * 3,
        out_shape=[seq] * 6 + [pout] * 3,
        scratch_shapes=[pltpu.VMEM((H, N, N), f32), pltpu.VMEM((H, 1, N), f32)],
        compiler_params=_cparams(("arbitrary",)),
    )(dy, g, r, lw, k, v, kkr, a, wkv, U, inv, S0s, r_k, ln_w, ln_b)


ATTN_TT = 2048


def _attn_rows(d, i, j):
    return pl.ds(ATTN_BLK * d * i + j, ATTN_BLK, stride=d) if d > 1 else pl.ds(ATTN_BLK * i, ATTN_BLK)


def _attn_take(ref, d, nsub):
    return jnp.stack([ref[_attn_rows(d, i, j), :] for i in range(nsub) for j in range(d)], axis=0)


def _attn_put(ref, val, d):
    for i in range(val.shape[0] // d):
        for j in range(d):
            ref[_attn_rows(d, i, j), :] = val[i * d + j]


def _attn_prev(cur, before, d):
    return before if cur.shape[0] == d else jnp.concatenate([before, cur[:cur.shape[0] - d]], axis=0)


def _attn_specs(gi, d, nt, reverse):
    per_tile = ATTN_TT // (ATTN_BLK * d)

    def tile(n):
        return nt - 1 - n if reverse else n

    def col(kind):
        return lambda hp, n: (tile(n), kind * (ATTN_W // 128) + 2 * gi + hp)

    def col_before(kind):
        return lambda hp, n: (jnp.maximum(tile(n) * per_tile - 1, 0), kind * (ATTN_W // 128) + 2 * gi + hp)

    cur = [pl.BlockSpec((ATTN_TT, 128), col(kind)) for kind in range(3)]
    before = [pl.BlockSpec((ATTN_BLK * d, 128), col_before(kind)) for kind in (1, 2)]
    own = pl.BlockSpec((ATTN_TT, 128), lambda hp, n: (tile(n), hp))
    return cur, before, own, tile


def _attn_norm(x, gain, scale):
    rs = lax.rsqrt(jnp.mean(x * x, axis=-1, keepdims=True) + RMS_EPS)
    return x * rs * (gain * scale), rs


def _attn_scores(qn, kn_c, kn_p, first):
    s_c = jnp.einsum("gqe,gke->gqk", qn.astype(bf16), kn_c.astype(bf16), preferred_element_type=f32)
    s_p = jnp.einsum("gqe,gke->gqk", qn.astype(bf16), kn_p.astype(bf16), preferred_element_type=f32)
    qi = lax.broadcasted_iota(jnp.int32, (1, ATTN_BLK, ATTN_BLK), 1)
    ki = lax.broadcasted_iota(jnp.int32, (1, ATTN_BLK, ATTN_BLK), 2)
    s_c = jnp.where(qi >= ki, s_c, NEG_INF)
    s_p = jnp.where(jnp.logical_and(ki >= qi, jnp.logical_not(first)), s_p, NEG_INF)
    return s_c, s_p


def _attn_fwd(pqkv, qg, kg, gi, S):
    d = ATTN_PAIRS[gi][1]
    nt = S // ATTN_TT
    nsub = ATTN_TT // (ATTN_BLK * d)
    nd = nsub * d

    def body(q_ref, k_ref, v_ref, kb_ref, vb_ref, qg_ref, kg_ref, o_ref, lse_ref):
        Q, K, V = _attn_take(q_ref, d, nsub), _attn_take(k_ref, d, nsub), _attn_take(v_ref, d, nsub)
        KB, VB = _attn_take(kb_ref, d, 1), _attn_take(vb_ref, d, 1)
        first = jnp.logical_and(lax.broadcasted_iota(jnp.int32, (nd, 1, 1), 0) < d, pl.program_id(1) == 0)
        outs, lses = [], []
        for h in range(2):
            sl = slice(h * HEAD, (h + 1) * HEAD)
            kc, vc = K[:, :, sl], V[:, :, sl]
            kp, vp = _attn_prev(kc, KB[:, :, sl], d), _attn_prev(vc, VB[:, :, sl], d)
            qn, _ = _attn_norm(Q[:, :, sl], qg_ref[...], HEAD ** -0.5)
            kn_c, _ = _attn_norm(kc, kg_ref[...], 1.0)
            kn_p, _ = _attn_norm(kp, kg_ref[...], 1.0)
            s_c, s_p = _attn_scores(qn, kn_c, kn_p, first)
            m = jnp.maximum(jnp.max(s_c, axis=-1, keepdims=True), jnp.max(s_p, axis=-1, keepdims=True))
            p_c = jnp.exp(s_c - m)
            p_p = jnp.exp(s_p - m)
            den = jnp.sum(p_c, axis=-1, keepdims=True) + jnp.sum(p_p, axis=-1, keepdims=True)
            inv = 1.0 / den
            o = jnp.einsum("gqk,gke->gqe", (p_c * inv).astype(bf16), vc.astype(bf16), preferred_element_type=f32)
            o += jnp.einsum("gqk,gke->gqe", (p_p * inv).astype(bf16), vp.astype(bf16), preferred_element_type=f32)
            outs.append(o)
            lses.append(jnp.broadcast_to(m + jnp.log(den), o.shape))
        _attn_put(o_ref, jnp.concatenate(outs, axis=-1), d)
        _attn_put(lse_ref, jnp.concatenate(lses, axis=-1), d)

    cur, before, own, _ = _attn_specs(gi, d, nt, False)
    par = pl.BlockSpec((1, HEAD), lambda hp, n: (0, 0))
    shp = jax.ShapeDtypeStruct((S, 2 * 128), f32)
    return pl.pallas_call(
        body, name=f"attn_fwd{gi}", grid=(2, nt), in_specs=cur + before + [par] * 2, out_specs=[own, own],
        out_shape=[shp, shp], compiler_params=_cparams(("arbitrary", "arbitrary")),
    )(pqkv, pqkv, pqkv, pqkv, pqkv, qg, kg)


def _attn_bwd(pqkv, o, lse, do, dlse, qg, kg, gi, S):
    d = ATTN_PAIRS[gi][1]
    nt = S // ATTN_TT
    nsub = ATTN_TT // (ATTN_BLK * d)
    nd = nsub * d

    def norm_bwd(dxn, x, rs, gain, scale):
        xh = x * rs
        dxh = dxn * (gain * scale)
        dx = rs * (dxh - xh * jnp.mean(dxh * xh, axis=-1, keepdims=True))
        dgain = jnp.sum(jnp.sum(dxn * xh * scale, axis=1), axis=0, keepdims=True)
        return dx, dgain

    def to_before(part, carried):
        return carried if nsub == 1 else jnp.concatenate([part[d:], carried], axis=0)

    def body(q_ref, k_ref, v_ref, kb_ref, vb_ref, o_ref, lse_ref, do_ref, dlse_ref, qg_ref, kg_ref,
             dq_ref, dk_ref, dv_ref, dqg_ref, dkg_ref, carry_k, carry_v):
        step = pl.program_id(1)

        @pl.when(jnp.logical_and(pl.program_id(0) == 0, step == 0))
        def _():
            dqg_ref[...] = jnp.zeros(dqg_ref.shape, f32)
            dkg_ref[...] = jnp.zeros(dkg_ref.shape, f32)

        @pl.when(step == 0)
        def _():
            carry_k[...] = jnp.zeros(carry_k.shape, f32)
            carry_v[...] = jnp.zeros(carry_v.shape, f32)

        Q, K, V = _attn_take(q_ref, d, nsub), _attn_take(k_ref, d, nsub), _attn_take(v_ref, d, nsub)
        KB, VB = _attn_take(kb_ref, d, 1), _attn_take(vb_ref, d, 1)
        O, LSE = _attn_take(o_ref, d, nsub), _attn_take(lse_ref, d, nsub)
        DO, DLSE = _attn_take(do_ref, d, nsub), _attn_take(dlse_ref, d, nsub)
        first = jnp.logical_and(lax.broadcasted_iota(jnp.int32, (nd, 1, 1), 0) < d, step == nt - 1)
        qg, kg = qg_ref[...], kg_ref[...]
        dqs, dks, dvs = [], [], []
        for h in range(2):
            sl = slice(h * HEAD, (h + 1) * HEAD)
            qx, kx, vc = Q[:, :, sl], K[:, :, sl], V[:, :, sl]
            kpx, vp = _attn_prev(kx, KB[:, :, sl], d), _attn_prev(vc, VB[:, :, sl], d)
            qn, rq = _attn_norm(qx, qg, HEAD ** -0.5)
            kn_c, rk_c = _attn_norm(kx, kg, 1.0)
            kn_p, _ = _attn_norm(kpx, kg, 1.0)
            s_c, s_p = _attn_scores(qn, kn_c, kn_p, first)
            lse = LSE[:, :, h * HEAD:h * HEAD + 1]
            p_c = jnp.exp(s_c - lse)
            p_p = jnp.exp(s_p - lse)
            dO = DO[:, :, sl]
            dOb = dO.astype(bf16)
            dp_c = jnp.einsum("gqe,gke->gqk", dOb, vc.astype(bf16), preferred_element_type=f32)
            dp_p = jnp.einsum("gqe,gke->gqk", dOb, vp.astype(bf16), preferred_element_type=f32)
            corr = DLSE[:, :, h * HEAD:h * HEAD + 1] - jnp.sum(dO * O[:, :, sl], axis=-1, keepdims=True)
            ds_c = (p_c * (dp_c + corr)).astype(bf16)
            ds_p = (p_p * (dp_p + corr)).astype(bf16)
            qnb = qn.astype(bf16)
            dqn = (jnp.einsum("gqk,gke->gqe", ds_c, kn_c.astype(bf16), preferred_element_type=f32)
                   + jnp.einsum("gqk,gke->gqe", ds_p, kn_p.astype(bf16), preferred_element_type=f32))
            dkn_p = jnp.einsum("gqk,gqe->gke", ds_p, qnb, preferred_element_type=f32)
            dv_p = jnp.einsum("gqk,gqe->gke", p_p.astype(bf16), dOb, preferred_element_type=f32)
            dkn = jnp.einsum("gqk,gqe->gke", ds_c, qnb, preferred_element_type=f32) + to_before(dkn_p, carry_k[h])
            dv = (jnp.einsum("gqk,gqe->gke", p_c.astype(bf16), dOb, preferred_element_type=f32)
                  + to_before(dv_p, carry_v[h]))
            carry_k[h] = dkn_p[:d]
            carry_v[h] = dv_p[:d]
            dq, dqg = norm_bwd(dqn, qx, rq, qg, HEAD ** -0.5)
            dk, dkg = norm_bwd(dkn, kx, rk_c, kg, 1.0)
            dqg_ref[...] += dqg
            dkg_ref[...] += dkg
            dqs.append(dq)
            dks.append(dk)
            dvs.append(dv)
        _attn_put(dq_ref, jnp.concatenate(dqs, axis=-1), d)
        _attn_put(dk_ref, jnp.concatenate(dks, axis=-1), d)
        _attn_put(dv_ref, jnp.concatenate(dvs, axis=-1), d)

    cur, before, own, _ = _attn_specs(gi, d, nt, True)
    par = pl.BlockSpec((1, HEAD), lambda hp, n: (0, 0))
    shp = jax.ShapeDtypeStruct((S, 2 * 128), f32)
    pshp = jax.ShapeDtypeStruct((1, HEAD), f32)
    return pl.pallas_call(
        body, name=f"attn_bwd{gi}", grid=(2, nt), in_specs=cur + before + [own] * 4 + [par] * 2,
        out_specs=[own] * 3 + [par] * 2, out_shape=[shp] * 3 + [pshp] * 2,
        scratch_shapes=[pltpu.VMEM((2, d, ATTN_BLK, HEAD), f32)] * 2,
        compiler_params=_cparams(("arbitrary", "arbitrary")),
    )(pqkv, pqkv, pqkv, pqkv, pqkv, o, lse, do, dlse, qg, kg)


def _rms(x, g):
    rs = lax.rsqrt(jnp.mean(x * x, axis=-1, keepdims=True) + RMS_EPS)
    return x * rs * g


def _f_rms(x, g):
    return _rms(x, g)


def _f_resid_rms(coef, x, f, g):
    xn = x + coef * f
    return xn, _rms(xn, g)


def _f_swiglu(u):
    gate, up = u[:, :D_FF], u[:, D_FF:]
    return gate * jax.nn.sigmoid(gate) * up


def _f_swiglu_bwd(dact, u):
    gate, up = u[:, :D_FF], u[:, D_FF:]
    sg = jax.nn.sigmoid(gate)
    silu = gate * sg
    dact = 0.5 * dact
    return jnp.concatenate([dact * up * (sg * (1.0 + gate * (1.0 - sg))), dact * silu], axis=1)


def _f_rms_bwd(n_parts, *args):
    dns = args[:n_parts]
    x, dres, g = args[n_parts:]
    dn = dns[0]
    for t in dns[1:]:
        dn = dn + t
    rs = lax.rsqrt(jnp.mean(x * x, axis=-1, keepdims=True) + RMS_EPS)
    xh = x * rs
    dxh = dn * g
    dx = dres + rs * (dxh - xh * jnp.mean(dxh * xh, axis=-1, keepdims=True))
    return dx, dx, jnp.sum(dn * xh, axis=0, keepdims=True)


def _f_loss(x, f, tgt):
    y = x + 0.5 * f
    diff = y - tgt
    part = 0.5 * jnp.sum(jnp.mean(diff * diff, axis=-1, keepdims=True), axis=0, keepdims=True)
    dy = diff * (1.0 / D)
    return dy, dy, jnp.broadcast_to(part, (1, 128))


def _dotb(a, b, dims):
    return lax.dot_general(a.astype(bf16), b.astype(bf16), dims, preferred_element_type=f32)


_NN = (((1,), (0,)), ((), ()))
_NT = (((1,), (1,)), ((), ()))
_TN = (((0,), (0,)), ((), ()))


def _rwkv_pre_core(prkv, prkv_prev, plora, plora_prev, mu_rkv, mu_lora, w0, w2p, a0, a2p, g2p, k_k, k_a):
    xs = prkv + (prkv_prev - prkv) * mu_rkv
    xl = plora + (plora_prev - plora) * mu_lora
    r, k, v = xs[:, :D], xs[:, D:2 * D], xs[:, 2 * D:]
    wd, ad, gd = xl[:, :128], xl[:, 128:256], xl[:, 256:]
    tw = jnp.tanh(wd)
    zw = w0 + _dotb(tw, w2p, _NN)
    sp = jnp.maximum(-zw, 0.0) + jnp.log(1.0 + jnp.exp(-jnp.abs(zw)))
    lw = -jnp.exp(-sp - 0.5)
    a = jax.nn.sigmoid(a0 + _dotb(ad, a2p, _NN))
    sg = jax.nn.sigmoid(gd)
    return dict(r=r, k=k, v=v, tw=tw, zw=zw, lw=lw, a=a, sg=sg, ad=ad)


def _rows_down(x, halo, blk):
    before = jnp.where(blk > 0, halo[HALO - 1:HALO, :], 0.0)
    row = lax.broadcasted_iota(jnp.int32, (x.shape[0], 1), 0)
    return jnp.where(row == 0, before, pltpu.roll(x, 1, 0))


def _rows_up(x, after):
    n = x.shape[0]
    row = lax.broadcasted_iota(jnp.int32, (n, 1), 0)
    return jnp.where(row == n - 1, after, pltpu.roll(x, n - 1, 0))


def _f_rwkv_pre(prkv, plora, mu_rkv, mu_lora, w0, w2p, a0, a2p, g2p, k_k, k_a, halo_rkv, halo_lora, blk):
    c = _rwkv_pre_core(prkv, _rows_down(prkv, halo_rkv, blk), plora, _rows_down(plora, halo_lora, blk),
                       mu_rkv, mu_lora, w0, w2p, a0, a2p, g2p, k_k, k_a)
    g = _dotb(c["sg"], g2p, _NN)
    k, a = c["k"], c["a"]
    return c["r"], c["lw"], k * (1.0 + (a - 1.0) * k_a), c["v"], k * k_k, a, g


def _f_rwkv_pre_bwd(prkv, plora, dr, dlw, dk2, dv, dkkr, da, dya, yap,
                    mu_rkv, mu_lora, w0, w2p, a0, a2p, g2p, k_k, k_a, halo_rkv, halo_lora, next_rkv, next_lora, blk):
    prkv_prev, plora_prev = _rows_down(prkv, halo_rkv, blk), _rows_down(plora, halo_lora, blk)
    c = _rwkv_pre_core(prkv, prkv_prev, plora, plora_prev, mu_rkv, mu_lora, w0, w2p, a0, a2p, g2p, k_k, k_a)
    k, a, sg, tw, zw, lw = c["k"], c["a"], c["sg"], c["tw"], c["zw"], c["lw"]
    dg = dya * yap
    dsg = _dotb(dg, g2p, _NT)
    dgd = dsg * sg * (1.0 - sg)
    dg2p = _dotb(sg, dg, _TN)
    dk = dk2 * (1.0 + (a - 1.0) * k_a) + dkkr * k_k
    da_t = da + dk2 * k * k_a
    dk_a = jnp.sum(dk2 * k * (a - 1.0), axis=0, keepdims=True)
    dk_k = jnp.sum(dkkr * k, axis=0, keepdims=True)
    dza = da_t * a * (1.0 - a)
    da0 = jnp.sum(dza, axis=0, keepdims=True)
    dad = _dotb(dza, a2p, _NT)
    da2p = _dotb(c["ad"], dza, _TN)
    dzw = dlw * lw * jax.nn.sigmoid(-zw)
    dw0 = jnp.sum(dzw, axis=0, keepdims=True)
    dtw = _dotb(dzw, w2p, _NT)
    dw2p = _dotb(tw, dzw, _TN)
    dwd = dtw * (1.0 - tw * tw)
    dxs = jnp.concatenate([dr, dk, dv], axis=1)
    dxl = jnp.concatenate([dwd, dad, dgd], axis=1)
    dmu_rkv = jnp.sum(dxs * (prkv_prev - prkv), axis=0, keepdims=True)
    dmu_lora = jnp.sum(dxl * (plora_prev - plora), axis=0, keepdims=True)
    to_next_rkv, to_next_lora = dxs * mu_rkv, dxl * mu_lora
    return (dxs * (1.0 - mu_rkv) + _rows_up(to_next_rkv, next_rkv), dxl * (1.0 - mu_lora) + _rows_up(to_next_lora, next_lora),
            dmu_rkv, dmu_lora, dw0, da0, dk_k, dk_a, dw2p, da2p, dg2p, to_next_rkv[0:1], to_next_lora[0:1])


def _group_alpha(l0, l1, l2):
    m = jnp.maximum(jnp.maximum(l0, l1), l2)
    e0, e1, e2 = jnp.exp(l0 - m), jnp.exp(l1 - m), jnp.exp(l2 - m)
    inv = 1.0 / (e0 + e1 + e2)
    return jnp.concatenate([e0 * inv, e1 * inv, e2 * inv], axis=1)


def _f_combine(o0, o1, o2, l0, l1, l2):
    return jnp.concatenate([o0, o1, o2], axis=1) * _group_alpha(l0, l1, l2)


def _f_combine_bwd(dyb, o0, o1, o2, l0, l1, l2, bd):
    alpha = _group_alpha(l0, l1, l2)
    e = jnp.dot(dyb * jnp.concatenate([o0, o1, o2], axis=1), bd, precision=HI, preferred_element_type=f32)
    ae = alpha * e
    tot = ae[:, :256] + ae[:, 256:512] + ae[:, 512:]
    do = dyb * alpha
    dl = ae - alpha * jnp.concatenate([tot, tot, tot], axis=1)
    return do[:, :256], do[:, 256:512], do[:, 512:], dl[:, :256], dl[:, 256:512], dl[:, 512:]


def _f_merge(pgate, ta, tb, b_gate):
    gate = jax.nn.sigmoid(pgate + b_gate)
    return gate[:, :D] * ta + gate[:, D:] * tb


def _f_merge_bwd(dm, pgate, ta, tb, b_gate):
    gate = jax.nn.sigmoid(pgate + b_gate)
    ga, gb = gate[:, :D], gate[:, D:]
    dpg = jnp.concatenate([dm * ta * ga * (1.0 - ga), dm * tb * gb * (1.0 - gb)], axis=1)
    return dm * ga, dm * gb, dpg, jnp.sum(dpg, axis=0, keepdims=True)


def _f_adamw(w, g, m, v):
    m2 = ADAM_B1 * m + (1.0 - ADAM_B1) * g
    v2 = ADAM_B2 * v + (1.0 - ADAM_B2) * jnp.square(g)
    m_hat = m2 / (1.0 - ADAM_B1 ** ADAM_STEP)
    v_hat = v2 / (1.0 - ADAM_B2 ** ADAM_STEP)
    delta = -ADAM_LR * (m_hat / (jnp.sqrt(v_hat) + ADAM_EPS) + ADAM_WD * w)
    return delta, m2, v2


def _ffn_fwd(tag, n, WiT, Wo):
    S = n.shape[0]
    u = _mm(f"{tag}_up", n, WiT, "nt")
    (act,) = _rowwise(f"{tag}_swiglu", _f_swiglu, [u], [], [(D_FF, bf16)])
    f = _mm(f"{tag}_down", act, Wo, "nn")
    return u, act, f


def _ffn_bwd(tag, dxo, dxo_b, x_in, n, u, act, g, WiT, Wo):
    dact = _mm(f"{tag}_dact", dxo_b, Wo, "nt")
    dWo = _mm(f"{tag}_dwo", act, dxo_b, "tn", out_dtype=GRAD_WIRE, scale=0.5)
    (du,) = _rowwise(f"{tag}_dswiglu", _f_swiglu_bwd, [dact, u], [], [(2 * D_FF, bf16)], tm=128)
    dn = _mm(f"{tag}_dn", du, WiT, "nn")
    dWiT = _mm(f"{tag}_dwi", du, n, "tn", out_dtype=GRAD_WIRE)
    dx, dx_b, dg = _rowwise(f"{tag}_drms", functools.partial(_f_rms_bwd, 1), [dn, x_in, dxo], [g],
                            [(D, f32), (D, bf16)], [(1, D)])
    return dx, dx_b, dg, dWiT, dWo


def _local_step(x0, tgt, W, P):
    S = x0.shape[0]
    (n1,) = _rowwise("f1_rms", _f_rms, [x0], [P["ffn1_norm"]], [(D, bf16)])
    u1, act1, f1 = _ffn_fwd("f1", n1, W["f1_iT"], W["f1_o"])
    x1, h = _rowwise("mix_rms", functools.partial(_f_resid_rms, 0.5), [x0, f1], [P["mix_norm"]],
                     [(D, f32), (D, bf16)])
    prkv = _mm("p_rkv", h, W["in_rkvT"], "nt")
    plora = _mm("p_lora", h, W["in_loraT"], "nt")
    pqkv = _mm("p_qkv", h, W["in_qkvT"], "nt")
    pgate = _mm("p_gate", h, W["in_gateT"], "nt")
    pre_params = [P["mu_rkv"], P["mu_lora"], P["w0"], W["w2p"], P["a0"], W["a2p"], W["g2p"], P["k_k"], P["k_a"]]
    r, lw, k2, v, kkr, a, g = _rowwise("rwkv_pre", _f_rwkv_pre, [prkv, plora], pre_params, [(D, f32)] * 7, tm=128,
                                       halos=(0, 1))
    hm = [r, lw, k2, v, kkr, a]
    hp = [P["r_k"].reshape(RW_HEADS, 1, HEAD), P["ln_w"].reshape(RW_HEADS, 1, HEAD), P["ln_b"].reshape(RW_HEADS, 1, HEAD)]
    yap, ya, wkv_h, U_h, inv_h, S0s = _wkv_fwd(*hm, g, *hp)
    ta = _mm("proj_a", ya, W["pr"], "nn")
    n_grp = len(ATTN_PAIRS)
    attn = [_attn_fwd(pqkv, P["q_norm"], P["k_norm"], gi, S) for gi in range(n_grp)]
    o_g, lse_g = [t[0] for t in attn], [t[1] for t in attn]
    (yb,) = _rowwise("attn_combine", _f_combine, [*o_g, *lse_g], [], [(ATTN_W, bf16)])
    tb = _mm("proj_b", yb, W["paT"], "nt")
    (merged,) = _rowwise("merge", _f_merge, [pgate, ta, tb], [P["b_gate"]], [(D, bf16)])
    mo = _mm("mix_out", merged, W["out"], "nn")
    x2, n2 = _rowwise("f2_rms", functools.partial(_f_resid_rms, 1.0), [x1, mo], [P["ffn2_norm"]],
                      [(D, f32), (D, bf16)])
    u2, act2, f2 = _ffn_fwd("f2", n2, W["f2_iT"], W["f2_o"])
    dx3, dx3_b, loss = _rowwise("loss", _f_loss, [x2, f2, tgt], [], [(D, f32), (D, bf16)], [(1, 128)])
    G, Gs = {}, {}
    dx2, dx2_b, Gs["ffn2_norm"], G["f2_iT"], G["f2_o"] = _ffn_bwd("f2", dx3, dx3_b, x2, n2, u2, act2, P["ffn2_norm"],
                                                                 W["f2_iT"], W["f2_o"])
    dmerged = _mm("d_merged", dx2_b, W["out"], "nt")
    G["out"] = _mm("dw_out", merged, dx2_b, "tn", out_dtype=GRAD_WIRE)
    dta, dtb, dpgate, Gs["b_gate"] = _rowwise("merge_bwd", _f_merge_bwd, [dmerged, pgate, ta, tb], [P["b_gate"]],
                                              [(D, bf16), (D, bf16), (2 * D, bf16)], [(1, 2 * D)])
    dya = _mm("d_ya", dta, W["pr"], "nt")
    G["pr"] = _mm("dw_pr", ya, dta, "tn", out_dtype=GRAD_WIRE)
    dyb = _mm("d_yb", dtb, W["paT"], "nn")
    G["paT"] = _mm("dw_pa", dtb, yb, "tn", out_dtype=GRAD_WIRE)
    hg = _wkv_bwd(dya, g, *hm, wkv_h, U_h, inv_h, S0s, *hp)
    dr, dlw, dk2, dv, dkkr, da = hg[:6]
    Gs["r_k"], Gs["ln_w"], Gs["ln_b"] = (t.reshape(1, D) for t in hg[6:])
    lp = sum(LORA_PAD)
    (dprkv, dplora, Gs["mu_rkv"], Gs["mu_lora"], Gs["w0"], Gs["a0"], Gs["k_k"], Gs["k_a"],
     dw2p, da2p, dg2p) = _rowwise(
        "rwkv_pre_bwd", _f_rwkv_pre_bwd,
        [prkv, plora, dr, dlw, dk2, dv, dkkr, da, dya, yap], pre_params,
        [(3 * D, bf16), (lp, bf16)],
        [(1, 3 * D), (1, lp), (1, D), (1, D), (1, D), (1, D), (LORA_PAD[0], D), (LORA_PAD[1], D), (LORA_PAD[2], D)],
        tm=128, halos=(0, 1), carries=((1, 3 * D), (1, lp)), reverse=True)
    G["w2T"], G["a2T"], G["g2T"] = dw2p[:LORA_W[0]].T, da2p[:LORA_W[1]].T, dg2p[:LORA_W[2]].T
    bd = (jnp.arange(ATTN_W)[:, None] // HEAD == jnp.arange(ATTN_W)[None, :] // HEAD).astype(f32)
    dol = _rowwise("attn_combine_bwd", _f_combine_bwd, [dyb, *o_g, *lse_g], [bd], [(ATTN_W // n_grp, f32)] * (2 * n_grp))
    dattn = [_attn_bwd(pqkv, o_g[gi], lse_g[gi], dol[gi], dol[n_grp + gi], P["q_norm"], P["k_norm"], gi, S)
             for gi in range(n_grp)]
    Gs["q_norm"] = dattn[0][3] + dattn[1][3] + dattn[2][3]
    Gs["k_norm"] = dattn[0][4] + dattn[1][4] + dattn[2][4]
    dpqkv = jnp.concatenate([dattn[gi][kind] for kind in range(3) for gi in range(n_grp)], axis=1).astype(bf16)
    dh = [_mm("dh_rkv", dprkv, W["in_rkvT"], "nn"), _mm("dh_lora", dplora, W["in_loraT"], "nn"),
          _mm("dh_qkv", dpqkv, W["in_qkvT"], "nn"), _mm("dh_gate", dpgate, W["in_gateT"], "nn")]
    dW_rkv = _mm("dw_rkv", dprkv, h, "tn", out_dtype=GRAD_WIRE)
    dW_lora = _mm("dw_lora", dplora, h, "tn", out_dtype=GRAD_WIRE)
    dW_qkv = _mm("dw_qkv", dpqkv, h, "tn", out_dtype=GRAD_WIRE)
    dW_gate = _mm("dw_gate", dpgate, h, "tn", out_dtype=GRAD_WIRE)
    o1, o2 = LORA_PAD[0], LORA_PAD[0] + LORA_PAD[1]
    G["inT"] = jnp.concatenate([dW_rkv, dW_lora[:LORA_W[0]], dW_lora[o1:o1 + LORA_W[1]], dW_lora[o2:o2 + LORA_W[2]],
                                dW_qkv, dW_gate], axis=0)
    dx1, dx1_b, Gs["mix_norm"] = _rowwise("mix_drms", functools.partial(_f_rms_bwd, 4), [*dh, x1, dx2],
                                          [P["mix_norm"]], [(D, f32), (D, bf16)], [(1, D)])
    dx0, _, Gs["ffn1_norm"], G["f1_iT"], G["f1_o"] = _ffn_bwd("f1", dx1, dx1_b, x0, n1, u1, act1, P["ffn1_norm"],
                                                             W["f1_iT"], W["f1_o"])
    return loss[0, 0], dx0, G, Gs


def _peer(k):
    x, y, c = lax.axis_index("x"), lax.axis_index("y"), lax.axis_index("c")
    px = 1 - x if k & 4 else x
    py = 1 - y if k & 2 else y
    pc = 1 - c if k & 1 else c
    return (px, py, pc), 4 * px + 2 * py + pc


def _all_gather(pack):
    R, C = pack.shape

    def body(x_ref, out_ref, send_sems, recv_sems, local_sem):
        x, y, c = lax.axis_index("x"), lax.axis_index("y"), lax.axis_index("c")
        me, sibling = (x, y, c), (x, y, 1 - c)
        chips = [(1 - x, y), (x, 1 - y), (1 - x, 1 - y)]

        def slot(px, py, pc):
            return out_ref.at[4 * px + 2 * py + pc]

        def copy(k, block, to, src=None):
            return pltpu.make_async_remote_copy(
                src_ref=slot(*block) if src is None else src, dst_ref=slot(*block), send_sem=send_sems.at[k],
                recv_sem=recv_sems.at[k], device_id=to, device_id_type=MESH)

        mine = pltpu.make_async_copy(x_ref, slot(*me), local_sem)
        mine.start()
        first = [copy(0, me, sibling, src=x_ref)]
        first += [copy(1 + j, me, (*chip, c), src=x_ref) for j, chip in enumerate(chips)]
        for cp in first:
            cp.start()
        passed = [copy(4 + j, (*chip, c), sibling) for j, chip in enumerate(chips)]
        for j, chip in enumerate(chips):
            copy(1 + j, (*chip, c), me).wait_recv()
            passed[j].start()
        copy(0, sibling, me).wait_recv()
        for j, chip in enumerate(chips):
            copy(4 + j, (*chip, 1 - c), me).wait_recv()
        for cp in first + passed:
            cp.wait_send()
        mine.wait()

    return pl.pallas_call(
        body, name="weight_all_gather", out_shape=jax.ShapeDtypeStruct((N_DEV, R, C), pack.dtype),
        in_specs=[pl.BlockSpec(memory_space=pl.ANY)], out_specs=pl.BlockSpec(memory_space=pl.ANY),
        scratch_shapes=[pltpu.SemaphoreType.DMA((N_DEV - 1,)), pltpu.SemaphoreType.DMA((N_DEV - 1,)),
                        pltpu.SemaphoreType.DMA(())],
    )(pack)


N_CHIP = 4


def _grad_pair(pieces):
    n = len(pieces)
    C = pieces[0].shape[2]
    rows = [p.shape[1] for p in pieces]
    offs = [sum(rows[:i]) for i in range(n)]
    R = sum(rows)

    def body(*refs):
        g_refs, (other_ref, send_sems, recv_sems) = refs[:n], refs[n:]
        x, y, c = lax.axis_index("x"), lax.axis_index("y"), lax.axis_index("c")
        copies = []
        for i, g_ref in enumerate(g_refs):
            for k in range(N_CHIP):
                cp = pltpu.make_async_remote_copy(
                    src_ref=g_ref.at[4 * (k // 2) + 2 * (k % 2) + 1 - c], dst_ref=other_ref.at[k, pl.ds(offs[i], rows[i])],
                    send_sem=send_sems.at[i * N_CHIP + k], recv_sem=recv_sems.at[i * N_CHIP + k],
                    device_id=(x, y, 1 - c), device_id_type=MESH)
                cp.start()
                copies.append(cp)
        for cp in copies:
            cp.wait()

    return pl.pallas_call(
        body, name="grad_pair", out_shape=jax.ShapeDtypeStruct((N_CHIP, R, C), pieces[0].dtype),
        in_specs=[pl.BlockSpec(memory_space=pl.ANY)] * n, out_specs=pl.BlockSpec(memory_space=pl.ANY),
        scratch_shapes=[pltpu.SemaphoreType.DMA((n * N_CHIP,))] * 2,
    )(*pieces)


def _pair_add(pieces, other, c):
    n = len(pieces)
    C = pieces[0].shape[2]
    nblk = [p.shape[1] // PACK_BLOCK for p in pieces]
    lo = [sum(nblk[:i]) for i in range(n)]
    R = sum(nblk) * PACK_BLOCK

    def body(c_ref, *refs):
        g_refs, o_ref, out_ref = refs[:n], refs[n], refs[n + 1]
        rb = pl.program_id(1)
        for i in range(n):
            @pl.when(jnp.logical_and(rb >= lo[i], rb < lo[i] + nblk[i]))
            def _(g_ref=g_refs[i]):
                out_ref[...] = (g_ref[...].astype(f32) + o_ref[...].astype(f32)).astype(out_ref.dtype)

    def piece_spec(i):
        return pl.BlockSpec((1, None, PACK_BLOCK, C),
                            lambda k, rb, c_ref: (k, c_ref[0], jnp.clip(rb - lo[i], 0, nblk[i] - 1), 0))

    blk = pl.BlockSpec((1, PACK_BLOCK, C), lambda k, rb, c_ref: (k, rb, 0))
    return pl.pallas_call(
        body, name="pair_add",
        grid_spec=pltpu.PrefetchScalarGridSpec(
            num_scalar_prefetch=1, grid=(N_CHIP, R // PACK_BLOCK),
            in_specs=[piece_spec(i) for i in range(n)] + [blk], out_specs=blk),
        out_shape=jax.ShapeDtypeStruct((N_CHIP, R, C), other.dtype),
        compiler_params=_cparams(("arbitrary", "arbitrary")),
    )(c, *[p.reshape(N_CHIP, 2, p.shape[1], C) for p in pieces], other)


def _grad_cross(part):
    _, R, C = part.shape

    def body(p_ref, out_ref, send_sems, recv_sems):
        x, y, c = lax.axis_index("x"), lax.axis_index("y"), lax.axis_index("c")
        flips = [(1, 0), (0, 1), (1, 1)]
        copies = []
        for j, (fx, fy) in enumerate(flips):
            px = 1 - x if fx else x
            py = 1 - y if fy else y
            cp = pltpu.make_async_remote_copy(src_ref=p_ref.at[2 * px + py], dst_ref=out_ref.at[j],
                                              send_sem=send_sems.at[j], recv_sem=recv_sems.at[j],
                                              device_id=(px, py, c), device_id_type=MESH)
            cp.start()
            copies.append(cp)
        for cp in copies:
            cp.wait()

    return pl.pallas_call(
        body, name="grad_cross", out_shape=jax.ShapeDtypeStruct((3, R, C), part.dtype),
        in_specs=[pl.BlockSpec(memory_space=pl.ANY)], out_specs=pl.BlockSpec(memory_space=pl.ANY),
        scratch_shapes=[pltpu.SemaphoreType.DMA((3,)), pltpu.SemaphoreType.DMA((3,))],
    )(part)


def _grad_sum(part, recv, my_chip, tr):
    _, R, C = part.shape

    def body(chip_ref, p_ref, r_ref, o_ref):
        acc = p_ref[0].astype(f32)
        for j in range(3):
            acc = acc + r_ref[j].astype(f32)
        o_ref[...] = acc

    return pl.pallas_call(
        body, name="grad_sum",
        grid_spec=pltpu.PrefetchScalarGridSpec(
            num_scalar_prefetch=1, grid=(R // tr,),
            in_specs=[pl.BlockSpec((1, tr, C), lambda i, chip_ref: (chip_ref[0], i, 0)),
                      pl.BlockSpec((3, tr, C), lambda i, chip_ref: (0, i, 0))],
            out_specs=pl.BlockSpec((tr, C), lambda i, chip_ref: (i, 0))),
        out_shape=jax.ShapeDtypeStruct((R, C), f32),
        compiler_params=_cparams(("arbitrary",)),
    )(my_chip, part, recv)


def _small_all_reduce(small):
    R, C = small.shape

    def body(x_ref, o_ref, buf, send_sems, recv_sems):
        _, me = _peer(0)
        buf[me] = x_ref[...]
        sends = []
        for k in range(1, N_DEV):
            dev, _ = _peer(k)
            cp = pltpu.make_async_remote_copy(src_ref=x_ref, dst_ref=buf.at[me], send_sem=send_sems.at[k - 1],
                                              recv_sem=recv_sems.at[k - 1], device_id=dev, device_id_type=MESH)
            cp.start()
            sends.append(cp)
        for k in range(1, N_DEV):
            dev, idx = _peer(k)
            pltpu.make_async_remote_copy(src_ref=x_ref, dst_ref=buf.at[idx], send_sem=send_sems.at[k - 1],
                                         recv_sem=recv_sems.at[k - 1], device_id=dev, device_id_type=MESH).wait_recv()
        for cp in sends:
            cp.wait_send()
        acc = buf[0]
        for i in range(1, N_DEV):
            acc = acc + buf[i]
        o_ref[...] = acc

    return pl.pallas_call(
        body, name="small_all_reduce", out_shape=jax.ShapeDtypeStruct((R, C), f32),
        in_specs=[pl.BlockSpec(memory_space=pltpu.VMEM)], out_specs=pl.BlockSpec(memory_space=pltpu.VMEM),
        scratch_shapes=[pltpu.VMEM((N_DEV, R, C), f32), pltpu.SemaphoreType.DMA((N_DEV - 1,)),
                        pltpu.SemaphoreType.DMA((N_DEV - 1,))],
    )(small)


_BIG_GROUPS = (
    (("ffn1_w_in", True),), (("ffn1_w_out", False),), (("w_in", True),),
    (("rwkv_w2", True), ("rwkv_a2", True), ("rwkv_g2", True)),
    (("w_proj_rwkv", False),), (("w_proj_attn", True),), (("w_out", False),),
    (("ffn2_w_in", True),), (("ffn2_w_out", False),))
_BIG = tuple(item for group in _BIG_GROUPS for item in group)
_SMALL = ("ffn1_norm", "mix_norm", "b_gate", "rwkv_mu", "rwkv_w0", "rwkv_a0", "rwkv_k_k", "rwkv_k_a", "rwkv_r_k",
          "rwkv_ln_w", "rwkv_ln_b", "attn_q_norm", "attn_k_norm", "ffn2_norm")


def _pack_layout(like):
    items, groups, off = {}, [], 0
    for group in _BIG_GROUPS:
        start = off
        for name, _ in group:
            shp = like[name].shape
            n = shp[0] * shp[1] // D
            items[name] = (off, n)
            off += n
        off = -(-off // PACK_BLOCK) * PACK_BLOCK
        groups.append((start, off - start))
    assert off == PACK_ROWS, off
    return items, groups


def _pack_big(shards):
    items, _ = _pack_layout(shards)
    parts, at = [], 0
    for name, tr in _BIG:
        off, n = items[name]
        t = shards[name]
        if off > at:
            parts.append(jnp.zeros((off - at, D), t.dtype))
        parts.append((t.T if tr else t).reshape(n, D))
        at = off + n
    parts.append(jnp.zeros((PACK_ROWS - at, D), parts[0].dtype))
    return jnp.concatenate(parts, axis=0)


def _unpack_big(pack, like):
    items, _ = _pack_layout(like)
    out = {}
    for name, tr in _BIG:
        off, n = items[name]
        shp = like[name].shape
        t = pack[off:off + n]
        out[name] = t.reshape(shp[1], shp[0]).T if tr else t.reshape(shp)
    return out


def _small_rows(name, t):
    flat = t.reshape(-1)
    pad = (-flat.shape[0]) % D
    return jnp.pad(flat, (0, pad)).reshape(-1, D)


def _pack_small(vals):
    parts = [_small_rows(n, vals[n]) for n in _SMALL]
    used = sum(p.shape[0] for p in parts)
    parts.append(jnp.zeros((SMALL_ROWS - used, D), f32))
    return jnp.concatenate(parts, axis=0)


def _unpack_small(pack, like):
    out, off = {}, 0
    for n in _SMALL:
        size = like[n].size
        rows = -(-size // D)
        out[n] = pack[off:off + rows].reshape(-1)[:size].reshape(like[n].shape)
        off += rows
    return out


def _build_W(full):
    inT = full["w_in"]
    z64, z96 = jnp.zeros((64, D), inT.dtype), jnp.zeros((96, D), inT.dtype)
    return {
        "f1_iT": full["ffn1_w_in"], "f1_o": full["ffn1_w_out"], "f2_iT": full["ffn2_w_in"], "f2_o": full["ffn2_w_out"],
        "in_rkvT": inT[:3 * D],
        "in_loraT": jnp.concatenate([inT[3072:3136], z64, inT[3136:3200], z64, inT[3200:3360], z96], axis=0),
        "in_qkvT": inT[3360:3360 + 3 * ATTN_W], "in_gateT": inT[3360 + 3 * ATTN_W:],
        "w2p": jnp.concatenate([full["rwkv_w2"].T, z64], axis=0),
        "a2p": jnp.concatenate([full["rwkv_a2"].T, z64], axis=0),
        "g2p": jnp.concatenate([full["rwkv_g2"].T, z96], axis=0),
        "pr": full["w_proj_rwkv"], "paT": full["w_proj_attn"], "out": full["w_out"],
    }


def _build_P(Wl):
    mu = Wl["rwkv_mu"]
    z64f, z96f = jnp.zeros((1, 64), f32), jnp.zeros((1, 96), f32)
    return {
        "ffn1_norm": Wl["ffn1_norm"][None], "mix_norm": Wl["mix_norm"][None], "ffn2_norm": Wl["ffn2_norm"][None],
        "b_gate": Wl["b_gate"][None], "mu_rkv": mu[None, :3 * D],
        "mu_lora": jnp.concatenate([mu[None, 3072:3136], z64f, mu[None, 3136:3200], z64f, mu[None, 3200:3360], z96f], axis=1),
        "w0": Wl["rwkv_w0"][None], "a0": Wl["rwkv_a0"][None], "k_k": Wl["rwkv_k_k"][None], "k_a": Wl["rwkv_k_a"][None],
        "r_k": Wl["rwkv_r_k"].reshape(1, D), "ln_w": Wl["rwkv_ln_w"][None], "ln_b": Wl["rwkv_ln_b"][None],
        "q_norm": Wl["attn_q_norm"][None], "k_norm": Wl["attn_k_norm"][None],
    }


def kernel(x, ffn1_norm, ffn1_w_in, ffn1_w_out, mix_norm, w_in, b_gate, rwkv_mu, rwkv_w0, rwkv_w2, rwkv_a0, rwkv_a2, rwkv_g2, rwkv_k_k, rwkv_k_a, rwkv_r_k, rwkv_ln_w, rwkv_ln_b, attn_q_norm, attn_k_norm, w_proj_rwkv, w_proj_attn, w_out, ffn2_norm, ffn2_w_in, ffn2_w_out, loss_target, m_ffn1_norm, m_ffn1_w_in, m_ffn1_w_out, m_mix_norm, m_w_in, m_b_gate, m_rwkv_mu, m_rwkv_w0, m_rwkv_w2, m_rwkv_a0, m_rwkv_a2, m_rwkv_g2, m_rwkv_k_k, m_rwkv_k_a, m_rwkv_r_k, m_rwkv_ln_w, m_rwkv_ln_b, m_attn_q_norm, m_attn_k_norm, m_w_proj_rwkv, m_w_proj_attn, m_w_out, m_ffn2_norm, m_ffn2_w_in, m_ffn2_w_out, v_ffn1_norm, v_ffn1_w_in, v_ffn1_w_out, v_mix_norm, v_w_in, v_b_gate, v_rwkv_mu, v_rwkv_w0, v_rwkv_w2, v_rwkv_a0, v_rwkv_a2, v_rwkv_g2, v_rwkv_k_k, v_rwkv_k_a, v_rwkv_r_k, v_rwkv_ln_w, v_rwkv_ln_b, v_attn_q_norm, v_attn_k_norm, v_w_proj_rwkv, v_w_proj_attn, v_w_out, v_ffn2_norm, v_ffn2_w_in, v_ffn2_w_out):
    names = ("ffn1_norm", "ffn1_w_in", "ffn1_w_out", "mix_norm", "w_in", "b_gate", "rwkv_mu", "rwkv_w0", "rwkv_w2",
             "rwkv_a0", "rwkv_a2", "rwkv_g2", "rwkv_k_k", "rwkv_k_a", "rwkv_r_k", "rwkv_ln_w", "rwkv_ln_b",
             "attn_q_norm", "attn_k_norm", "w_proj_rwkv", "w_proj_attn", "w_out", "ffn2_norm", "ffn2_w_in", "ffn2_w_out")
    w_all = (ffn1_norm, ffn1_w_in, ffn1_w_out, mix_norm, w_in, b_gate, rwkv_mu, rwkv_w0, rwkv_w2, rwkv_a0, rwkv_a2,
             rwkv_g2, rwkv_k_k, rwkv_k_a, rwkv_r_k, rwkv_ln_w, rwkv_ln_b, attn_q_norm, attn_k_norm, w_proj_rwkv,
             w_proj_attn, w_out, ffn2_norm, ffn2_w_in, ffn2_w_out)
    m_all = (m_ffn1_norm, m_ffn1_w_in, m_ffn1_w_out, m_mix_norm, m_w_in, m_b_gate, m_rwkv_mu, m_rwkv_w0, m_rwkv_w2,
             m_rwkv_a0, m_rwkv_a2, m_rwkv_g2, m_rwkv_k_k, m_rwkv_k_a, m_rwkv_r_k, m_rwkv_ln_w, m_rwkv_ln_b,
             m_attn_q_norm, m_attn_k_norm, m_w_proj_rwkv, m_w_proj_attn, m_w_out, m_ffn2_norm, m_ffn2_w_in, m_ffn2_w_out)
    v_all = (v_ffn1_norm, v_ffn1_w_in, v_ffn1_w_out, v_mix_norm, v_w_in, v_b_gate, v_rwkv_mu, v_rwkv_w0, v_rwkv_w2,
             v_rwkv_a0, v_rwkv_a2, v_rwkv_g2, v_rwkv_k_k, v_rwkv_k_a, v_rwkv_r_k, v_rwkv_ln_w, v_rwkv_ln_b,
             v_attn_q_norm, v_attn_k_norm, v_w_proj_rwkv, v_w_proj_attn, v_w_out, v_ffn2_norm, v_ffn2_w_in, v_ffn2_w_out)
    Wl = {n: t[0] for n, t in zip(names, w_all)}
    Ml = {n: t[0] for n, t in zip(names, m_all)}
    Vl = {n: t[0] for n, t in zip(names, v_all)}
    big = [n for n, _ in _BIG]

    w_pack = _pack_big({n: Wl[n] for n in big})
    gathered = _all_gather(w_pack.astype(bf16))
    items, groups = _pack_layout(Wl)
    full = {}
    for n, tr in _BIG:
        shp = Wl[n].shape
        off, rows = items[n]
        t = gathered[:, off:off + rows]
        r_loc, c_loc = (shp[1], shp[0]) if tr else shp
        full[n] = t.reshape(N_DEV * r_loc, c_loc)
    W, P = _build_W(full), _build_P(Wl)

    loss_local, dx0, G, Gs = _local_step(x[0], loss_target[0], W, P)

    g_full = {"ffn1_w_in": G["f1_iT"], "ffn1_w_out": G["f1_o"], "w_in": G["inT"], "rwkv_w2": G["w2T"],
              "rwkv_a2": G["a2T"], "rwkv_g2": G["g2T"], "w_proj_rwkv": G["pr"], "w_proj_attn": G["paT"],
              "w_out": G["out"], "ffn2_w_in": G["f2_iT"], "ffn2_w_out": G["f2_o"]}
    pieces = []
    for group, (_, rows_pad) in zip(_BIG_GROUPS, groups):
        parts = [g_full[n].astype(GRAD_WIRE).reshape(N_DEV, items[n][1], D) for n, _ in group]
        piece = parts[0] if len(parts) == 1 else jnp.concatenate(parts, axis=1)
        if rows_pad > piece.shape[1]:
            piece = jnp.pad(piece, ((0, 0), (0, rows_pad - piece.shape[1]), (0, 0)))
        pieces.append(piece)
    my_c = lax.axis_index("c").astype(jnp.int32).reshape(1)
    my_chip = (2 * lax.axis_index("x") + lax.axis_index("y")).astype(jnp.int32).reshape(1)
    chip_part = _pair_add(pieces, _grad_pair(pieces), my_c)
    g_pack = _grad_sum(chip_part, _grad_cross(chip_part), my_chip, 256)

    mu_g = Gs["mu_rkv"], Gs["mu_lora"]
    o1, o2 = LORA_PAD[0], LORA_PAD[0] + LORA_PAD[1]
    g_small_local = {
        "ffn1_norm": Gs["ffn1_norm"], "mix_norm": Gs["mix_norm"], "b_gate": Gs["b_gate"],
        "rwkv_mu": jnp.concatenate([mu_g[0], mu_g[1][:, :64], mu_g[1][:, o1:o1 + 64], mu_g[1][:, o2:o2 + 160]], axis=1),
        "rwkv_w0": Gs["w0"], "rwkv_a0": Gs["a0"], "rwkv_k_k": Gs["k_k"], "rwkv_k_a": Gs["k_a"], "rwkv_r_k": Gs["r_k"],
        "rwkv_ln_w": Gs["ln_w"], "rwkv_ln_b": Gs["ln_b"], "attn_q_norm": Gs["q_norm"], "attn_k_norm": Gs["k_norm"],
        "ffn2_norm": Gs["ffn2_norm"]}
    gs_pack = _small_all_reduce(_pack_small(g_small_local))

    g_big = _unpack_big(g_pack, Wl)
    out_g, out_d, out_m, out_v = dict(g_big), {}, {}, {}
    for n in big:
        cols = Wl[n].shape[1]
        out_d[n], out_m[n], out_v[n] = _rowwise(f"adamw_{n}", _f_adamw, [Wl[n], g_big[n], Ml[n], Vl[n]], [],
                                                 [(cols, f32)] * 3)
    ds_pack, ms_pack, vs_pack = _rowwise(
        "adamw_small", _f_adamw, [_pack_small(Wl), gs_pack, _pack_small(Ml), _pack_small(Vl)], [], [(D, f32)] * 3)
    for out, pack in ((out_g, gs_pack), (out_d, ds_pack), (out_m, ms_pack), (out_v, vs_pack)):
        out.update(_unpack_small(pack, Wl))

    loss = lax.psum(loss_local, ("x", "y", "c"))
    return (loss, dx0[None], *[out_g[n][None] for n in names], *[out_d[n][None] for n in names],
            *[out_m[n][None] for n in names], *[out_v[n][None] for n in names])
```

```python
import functools

import jax
import jax.numpy as jnp
from jax import lax
from jax.experimental import pallas as pl
from jax.experimental.pallas import tpu as pltpu

f32 = jnp.float32
bf16 = jnp.bfloat16
HI = lax.Precision.HIGHEST
MESH = pl.DeviceIdType.MESH

N_DEV = 8
D = 1024
D_FF = 2816
HEAD = 64
RW_HEADS = 16
ATTN_PAIRS = ((128, 1), (512, 4), (2048, 16))
ATTN_BLK = 128
HEADS_PER_GROUP = 4
ATTN_W = 768
LORA_PAD = (128, 128, 256)
LORA_W = (64, 64, 160)
GN_EPS = 64e-5
RMS_EPS = 1e-6
NEG_INF = -1e30
WKV_T = 64
GRAD_WIRE = bf16
PACK_BLOCK = 128
SMALL_ROWS = 24
VMEM_LIMIT = 56 * 1024 * 1024

ADAM_LR, ADAM_B1, ADAM_B2, ADAM_EPS, ADAM_WD, ADAM_STEP = 0.001, 0.9, 0.999, 1e-08, 0.01, 10


def _cparams(sem):
    return pltpu.CompilerParams(dimension_semantics=sem, vmem_limit_bytes=VMEM_LIMIT)


def _pick(n, cands):
    for c in cands:
        if n % c == 0:
            return c
    return n


HALO = 8


def _rowwise(name, fn, rows, params, outs, accs=(), tm=256, halos=(), carries=(), reverse=False):
    S = rows[0].shape[0]
    tm = min(tm, S)
    while S % tm:
        tm -= 8
    nb = S // tm
    n_in = len(rows) + len(params) + len(halos)
    n_out = len(outs)
    n_acc = len(accs)
    n_car = len(carries)

    def blk_of(i):
        return nb - 1 - i if reverse else i

    def body(*refs):
        step = pl.program_id(0)
        carry_refs = refs[n_in + n_out + n_acc:]
        if n_car:
            @pl.when(step == 0)
            def _():
                for c_ref in carry_refs:
                    c_ref[...] = jnp.zeros(c_ref.shape, f32)
        args = [r[...] for r in refs[:n_in]] + [c[...] for c in carry_refs]
        res = fn(*args, blk=blk_of(step)) if (halos or carries) else fn(*args)
        if not isinstance(res, (tuple, list)):
            res = (res,)
        out_refs = refs[n_in:n_in + n_out + n_acc]
        for j in range(n_out):
            out_refs[j][...] = res[j].astype(out_refs[j].dtype)
        if n_acc:
            @pl.when(step == 0)
            def _():
                for j in range(n_acc):
                    out_refs[n_out + j][...] = jnp.zeros(out_refs[n_out + j].shape, f32)
            for j in range(n_acc):
                out_refs[n_out + j][...] += res[n_out + j]
        for j in range(n_car):
            carry_refs[j][...] = res[n_out + n_acc + j]

    in_specs = [pl.BlockSpec((tm, a.shape[1]), lambda i: (blk_of(i), 0)) for a in rows]
    in_specs += [pl.BlockSpec(p.shape, lambda i, nd=p.ndim: (0,) * nd) for p in params]
    in_specs += [pl.BlockSpec((HALO, rows[h].shape[1]), lambda i: (jnp.maximum(blk_of(i) * (tm // HALO) - 1, 0), 0))
                 for h in halos]
    out_specs = [pl.BlockSpec((tm, w), lambda i: (blk_of(i), 0)) for w, _ in outs]
    out_specs += [pl.BlockSpec(s, lambda i: (0, 0)) for s in accs]
    out_shape = [jax.ShapeDtypeStruct((S, w), dt) for w, dt in outs]
    out_shape += [jax.ShapeDtypeStruct(s, f32) for s in accs]
    res = pl.pallas_call(
        body, name=name, grid=(nb,), in_specs=in_specs, out_specs=out_specs, out_shape=out_shape,
        scratch_shapes=[pltpu.VMEM(s, f32) for s in carries],
        compiler_params=_cparams(("arbitrary",)),
    )(*rows, *params, *[rows[h] for h in halos])
    return res


MM_VMEM_BUDGET = 40 * 1024 * 1024
MM_STEP_US = 0.35
MM_FLOPS_PER_US = 9.0e8
MM_HBM_BYTES_PER_US = 3.0e6


def _tile_options(n, cap):
    opts = [d for d in range(128, min(n, cap) + 1, 128) if n % d == 0]
    return opts or [n]


def _mm_tiles(M, N, K, sa, sb, so):
    best, best_cost = None, None
    for tm in _tile_options(M, 2048):
        for tn in _tile_options(N, 2048):
            for tk in _tile_options(K, 4096):
                vmem = 2 * (tm * tk * sa + tk * tn * sb) + 2 * tm * tn * so + (tm * tn * 4 if tk < K else 0)
                if vmem > MM_VMEM_BUDGET:
                    continue
                steps = (M // tm) * (N // tn) * (K // tk)
                traffic = M * K * sa * (N // tn) + K * N * sb * (M // tm) + M * N * so
                cost = (max(2.0 * M * N * K / MM_FLOPS_PER_US, traffic / MM_HBM_BYTES_PER_US) + steps * MM_STEP_US
                        + (tm * tk * sa + tk * tn * sb) / MM_HBM_BYTES_PER_US)
                if best_cost is None or cost < best_cost:
                    best, best_cost = (tm, tn, tk), cost
    return best


def _mm(name, a, b, mode, out_dtype=f32, scale=None):
    if mode == "nn":
        (M, K), (_, N) = a.shape, b.shape
    elif mode == "nt":
        (M, K), (N, _) = a.shape, b.shape
    else:
        (K, M), (_, N) = a.shape, b.shape
    tm, tn, tk = _mm_tiles(M, N, K, a.dtype.itemsize, b.dtype.itemsize, jnp.dtype(out_dtype).itemsize)
    nk = K // tk
    if mode == "nn":
        a_spec = pl.BlockSpec((tm, tk), lambda i, j, k: (i, k))
        b_spec = pl.BlockSpec((tk, tn), lambda i, j, k: (k, j))
        dims = (((1,), (0,)), ((), ()))
    elif mode == "nt":
        a_spec = pl.BlockSpec((tm, tk), lambda i, j, k: (i, k))
        b_spec = pl.BlockSpec((tn, tk), lambda i, j, k: (j, k))
        dims = (((1,), (1,)), ((), ()))
    else:
        a_spec = pl.BlockSpec((tk, tm), lambda i, j, k: (k, i))
        b_spec = pl.BlockSpec((tk, tn), lambda i, j, k: (k, j))
        dims = (((0,), (0,)), ((), ()))

    def finish(acc):
        return acc if scale is None else acc * scale

    def body(a_ref, b_ref, o_ref, *scratch):
        part = lax.dot_general(a_ref[...].astype(bf16), b_ref[...].astype(bf16), dims,
                               preferred_element_type=f32)
        if nk == 1:
            o_ref[...] = finish(part).astype(o_ref.dtype)
        else:
            acc_ref = scratch[0]
            k = pl.program_id(2)

            @pl.when(k == 0)
            def _():
                acc_ref[...] = part

            @pl.when(k > 0)
            def _():
                acc_ref[...] += part

            @pl.when(k == nk - 1)
            def _():
                o_ref[...] = finish(acc_ref[...]).astype(o_ref.dtype)

    return pl.pallas_call(
        body, name=name, grid=(M // tm, N // tn, nk), in_specs=[a_spec, b_spec],
        out_specs=pl.BlockSpec((tm, tn), lambda i, j, k: (i, j)),
        out_shape=jax.ShapeDtypeStruct((M, N), out_dtype),
        scratch_shapes=[] if nk == 1 else [pltpu.VMEM((tm, tn), f32)],
        compiler_params=_cparams(("parallel", "parallel", "arbitrary")),
    )(a, b)


def _sp(x):
    hi = x.astype(bf16)
    return hi, (x - hi.astype(f32)).astype(bf16)


def _cat(parts):
    return tuple(jnp.concatenate(p, axis=1) for p in zip(*parts))


def _bmm(eq, a, b):
    (ah, al), (bh, bl) = a, b
    dot = functools.partial(jnp.einsum, eq, preferred_element_type=f32)
    return dot(ah, bh) + (dot(ah, bl) + dot(al, bh))


def _tri_dot(eq, tri, x):
    h1 = x.astype(bf16)
    r1 = x - h1.astype(f32)
    h2 = r1.astype(bf16)
    h3 = (r1 - h2.astype(f32)).astype(bf16)
    dot = functools.partial(jnp.einsum, eq, preferred_element_type=f32)
    return dot(tri, h1) + (dot(tri, h2) + dot(tri, h3))


def _tri_masks(T):
    ti = lax.broadcasted_iota(jnp.int32, (T, T), 0)
    si = lax.broadcasted_iota(jnp.int32, (T, T), 1)
    return ti >= si, ti > si


def _wkv_prep(r, lw, k, kkr, a):
    H, T, _ = r.shape
    low_i, low_s = _tri_masks(T)
    nrm = jnp.sqrt(jnp.sum(kkr * kkr, axis=-1, keepdims=True))
    den = jnp.maximum(nrm, 1e-12)
    kk = kkr / den
    tri = jnp.broadcast_to(low_i.astype(bf16)[None], (H, T, T))
    cl = _tri_dot("hts,hsn->htn", tri, lw)
    c = jnp.exp(cl)
    cprev = jnp.exp(cl - lw)
    cinv = jnp.exp(-cl)
    bt, kt = _sp(kk * a * cinv), _sp(k * cinv)
    L = _cat([_sp(r * c), _sp(-kk * cprev)])
    Mb = _bmm("htn,hsn->hts", L, bt)
    Mk = _bmm("htn,hsn->hts", L, kt)
    A_rb = jnp.where(low_i[None], Mb[:, :T], 0.0)
    A_ab = jnp.where(low_s[None], Mb[:, T:], 0.0)
    Mk = jnp.concatenate([jnp.where(low_i[None], Mk[:, :T], 0.0), jnp.where(low_s[None], Mk[:, T:], 0.0)], axis=1)
    return dict(kk=kk, den=den, nrm=nrm, c=c, cprev=cprev, cinv=cinv, L=L, kt=kt, bt=bt,
                A_ab=A_ab, A_rb=A_rb, Mk=Mk, cT=c[:, T - 1:T, :])


def _tri_inverse(A):
    T = A.shape[-1]
    eye = (lax.broadcasted_iota(jnp.int32, (T, T), 0) == lax.broadcasted_iota(jnp.int32, (T, T), 1)).astype(f32)
    inv = eye[None] + A
    X = A
    n = 1
    while 2 * n < T:
        Xs = _sp(X)
        X = _bmm("hts,hsu->htu", Xs, Xs)
        inv = inv + _bmm("hts,hsu->htu", _sp(inv), _sp(X))
        n *= 2
    return inv


def _wkv_chunk_fwd(S0, r, lw, k, v, kkr, a):
    T = r.shape[1]
    q = _wkv_prep(r, lw, k, kkr, a)
    inv = _tri_inverse(q["A_ab"])
    vs = _sp(v)
    P = _bmm("htk,hvk->htv", q["L"], _sp(S0)) + _bmm("hts,hsv->htv", _sp(q["Mk"]), vs)
    U = _bmm("hts,hsv->htv", _sp(inv), _sp(P[:, T:]))
    Us = _sp(U)
    Y = P[:, :T] + _bmm("hts,hsv->htv", _sp(q["A_rb"]), Us)
    S1 = (S0 + _bmm("htv,htk->hvk", _cat([Us, vs]), _cat([q["bt"], q["kt"]]))) * q["cT"]
    return Y, U, inv, S1


def _wkv_chunk_bwd(S0, Hin, Q, r, lw, k, v, kkr, a, U, inv, dY):
    H, T, _ = r.shape
    low_i, low_s = _tri_masks(T)
    q = _wkv_prep(r, lw, k, kkr, a)
    L, kt, bt = q["L"], q["kt"], q["bt"]
    R = _cat([bt, kt])
    Hh = Hin * q["cT"]
    Hs, S0s, dYs, vs, Us = _sp(Hh), _sp(S0), _sp(dY), _sp(v), _sp(U)
    RH = _bmm("htk,hvk->htv", R, Hs)
    Z = _bmm("hst,hsv->htv", _sp(inv), _sp(RH[:, :T] + _bmm("hst,hsv->htv", _sp(q["A_rb"]), dYs)))
    DZ = _cat([dYs, _sp(Z)])
    both = jnp.concatenate([jnp.broadcast_to(low_i[None], (1, T, T)), jnp.broadcast_to(low_s[None], (1, T, T))], axis=1)
    NU = _sp(jnp.where(both, _bmm("htv,hsv->hts", DZ, Us), 0.0))
    NV = _sp(jnp.where(both, _bmm("htv,hsv->hts", DZ, vs), 0.0))
    ra = _bmm("htv,hvk->htk", DZ, S0s) + _bmm("hts,hsk->htk", NU, bt) + _bmm("hts,hsk->htk", NV, kt)
    dr = ra[:, :T] * q["c"]
    da = ra[:, T:] * q["cprev"]
    dv = RH[:, T:] + _bmm("hst,hsv->htv", _sp(q["Mk"]), DZ)
    VH = _bmm("htv,hvk->htk", _cat([vs, Us]), Hs)
    dk = (VH[:, :T] + _bmm("hst,hsk->htk", NV, L)) * q["cinv"]
    db = (VH[:, T:] + _bmm("hst,hsk->htk", NU, L)) * q["cinv"]
    H0 = Hh + _bmm("htv,htk->hvk", DZ, L)
    kk = q["kk"]
    e = r * dr - kk * a * db - k * dk
    f = -kk * da
    tri_i = jnp.broadcast_to(low_i.astype(bf16)[None], (H, T, T))
    tri_s = jnp.broadcast_to(low_s.astype(bf16)[None], (H, T, T))
    dlw = _tri_dot("hst,hsn->htn", tri_i, e) + _tri_dot("hst,hsn->htn", tri_s, f) + Q
    Qn = Q + jnp.sum(e + f, axis=1, keepdims=True)
    dkk = db * a - da
    dasig = db * kk
    proj = jnp.sum(dkk * kk, axis=-1, keepdims=True)
    dkkr = jnp.where(q["nrm"] > 1e-12, dkk - kk * proj, dkk) / q["den"]
    return dr, dlw, dk, dv, dkkr, dasig, H0, Qn


def _heads(ref):
    return jnp.stack([ref[:, h * HEAD:(h + 1) * HEAD] for h in range(RW_HEADS)], axis=0)


def _put_heads(ref, val):
    for h in range(RW_HEADS):
        ref[:, h * HEAD:(h + 1) * HEAD] = val[h]


def _wkv_fwd(r, lw, k, v, kkr, a, g, r_k, ln_w, ln_b, late_pack=None):
    S = r.shape[0]
    H, N, T = RW_HEADS, HEAD, WKV_T
    nc = S // T
    hosting = late_pack is not None

    def body(r_ref, lw_ref, k_ref, v_ref, kkr_ref, a_ref, g_ref, rk_ref, lnw_ref, lnb_ref, *rest):
        if hosting:
            pack_ref, y_ref, yg_ref, wkv_ref, u_ref, inv_ref, s0_ref, gathered_ref, state, *sems = rest
            start, forward, finish = _gather_phases(pack_ref, gathered_ref, *sems)
            pl.when(pl.program_id(0) == 0)(start)
            pl.when(pl.program_id(0) == nc // 2)(forward)
        else:
            y_ref, yg_ref, wkv_ref, u_ref, inv_ref, s0_ref, state = rest

        @pl.when(pl.program_id(0) == 0)
        def _():
            state[...] = jnp.zeros(state.shape, f32)

        S0 = state[...]
        s0_ref[0] = S0
        rr, kk2, vv = _heads(r_ref), _heads(k_ref), _heads(v_ref)
        Y, U, inv, S1 = _wkv_chunk_fwd(S0, rr, _heads(lw_ref), kk2, vv, _heads(kkr_ref), _heads(a_ref))
        state[...] = S1
        wkv_ref[...] = Y
        u_ref[...] = U
        inv_ref[...] = inv
        mean = jnp.mean(Y, axis=-1, keepdims=True)
        var = jnp.mean(jnp.square(Y - mean), axis=-1, keepdims=True)
        yn = (Y - mean) * lax.rsqrt(var + GN_EPS)
        bonus = jnp.sum(rr * kk2 * rk_ref[...], axis=-1, keepdims=True) * vv
        _put_heads(y_ref, yn * lnw_ref[...] + lnb_ref[...] + bonus)
        yg_ref[...] = (y_ref[...] * g_ref[...]).astype(yg_ref.dtype)
        if hosting:
            pl.when(pl.program_id(0) == nc - 1)(finish)

    tok = pl.BlockSpec((T, H * N), lambda i: (i, 0))
    blk = pl.BlockSpec((H, T, N), lambda i: (0, i, 0))
    par = pl.BlockSpec((H, 1, N), lambda i: (0, 0, 0))
    hbm = pl.BlockSpec(memory_space=pl.ANY)
    seq = jax.ShapeDtypeStruct((H, S, N), f32)
    out_specs = [tok, tok, blk, blk, blk, pl.BlockSpec((1, H, N, N), lambda i: (i, 0, 0, 0))]
    out_shape = [jax.ShapeDtypeStruct((S, H * N), f32), jax.ShapeDtypeStruct((S, H * N), bf16), seq, seq, seq,
                 jax.ShapeDtypeStruct((nc, H, N, N), f32)]
    if hosting:
        out_specs.append(hbm)
        out_shape.append(jax.ShapeDtypeStruct((N_DEV,) + late_pack.shape, late_pack.dtype))
    return pl.pallas_call(
        body, name="wkv_fwd", grid=(nc,), in_specs=[tok] * 7 + [par] * 3 + [hbm] * hosting,
        out_specs=out_specs, out_shape=out_shape,
        scratch_shapes=[pltpu.VMEM((H, N, N), f32)] + (GATHER_SEMS if hosting else []),
        compiler_params=_cparams(("arbitrary",)),
    )(r, lw, k, v, kkr, a, g, r_k, ln_w, ln_b, *([late_pack] if hosting else []))


def _wkv_bwd(dy, g, r, lw, k, v, kkr, a, wkv, U, inv, S0s, r_k, ln_w, ln_b, late_part=None):
    S = r.shape[0]
    H, N, T = RW_HEADS, HEAD, WKV_T
    nc = S // T
    hosting = late_part is not None

    def body(dy_ref, g_ref, r_ref, lw_ref, k_ref, v_ref, kkr_ref, a_ref, wkv_ref, u_ref, inv_ref, s0_ref,
             rk_ref, lnw_ref, lnb_ref, *rest):
        if hosting:
            (part_ref, dr_ref, dlw_ref, dk_ref, dv_ref, dkkr_ref, da_ref, drk_ref, dlnw_ref, dlnb_ref, recv_ref,
             hst, qst, *sems) = rest
            start, finish = _cross_phases(part_ref, recv_ref, *sems)
            pl.when(pl.program_id(0) == 0)(start)
        else:
            dr_ref, dlw_ref, dk_ref, dv_ref, dkkr_ref, da_ref, drk_ref, dlnw_ref, dlnb_ref, hst, qst = rest

        @pl.when(pl.program_id(0) == 0)
        def _():
            hst[...] = jnp.zeros(hst.shape, f32)
            qst[...] = jnp.zeros(qst.shape, f32)
            drk_ref[...] = jnp.zeros(drk_ref.shape, f32)
            dlnw_ref[...] = jnp.zeros(dlnw_ref.shape, f32)
            dlnb_ref[...] = jnp.zeros(dlnb_ref.shape, f32)

        dya = _heads(dy_ref[...] * g_ref[...])
        rr, kk2, vv, Y = _heads(r_ref), _heads(k_ref), _heads(v_ref), wkv_ref[...]
        rk = rk_ref[...]
        s = jnp.sum(rr * kk2 * rk, axis=-1, keepdims=True)
        ds = jnp.sum(dya * vv, axis=-1, keepdims=True)
        mean = jnp.mean(Y, axis=-1, keepdims=True)
        var = jnp.mean(jnp.square(Y - mean), axis=-1, keepdims=True)
        rstd = lax.rsqrt(var + GN_EPS)
        yn = (Y - mean) * rstd
        dyn = dya * lnw_ref[...]
        dY = rstd * (dyn - jnp.mean(dyn, axis=-1, keepdims=True) - yn * jnp.mean(dyn * yn, axis=-1, keepdims=True))
        drk_ref[...] += jnp.sum(ds * rr * kk2, axis=1, keepdims=True)
        dlnw_ref[...] += jnp.sum(dya * yn, axis=1, keepdims=True)
        dlnb_ref[...] += jnp.sum(dya, axis=1, keepdims=True)
        dr, dlw, dk, dv, dkkr, dasig, H0, Qn = _wkv_chunk_bwd(
            s0_ref[0], hst[...], qst[...], rr, _heads(lw_ref), kk2, vv, _heads(kkr_ref), _heads(a_ref), u_ref[...],
            inv_ref[...], dY)
        hst[...] = H0
        qst[...] = Qn
        _put_heads(dr_ref, dr + ds * kk2 * rk)
        _put_heads(dlw_ref, dlw)
        _put_heads(dk_ref, dk + ds * rr * rk)
        _put_heads(dv_ref, dv + dya * s)
        _put_heads(dkkr_ref, dkkr)
        _put_heads(da_ref, dasig)
        if hosting:
            pl.when(pl.program_id(0) == nc - 1)(finish)

    tok = pl.BlockSpec((T, H * N), lambda i: (nc - 1 - i, 0))
    blk = pl.BlockSpec((H, T, N), lambda i: (0, nc - 1 - i, 0))
    par = pl.BlockSpec((H, 1, N), lambda i: (0, 0, 0))
    hbm = pl.BlockSpec(memory_space=pl.ANY)
    seq = jax.ShapeDtypeStruct((S, H * N), f32)
    pout = jax.ShapeDtypeStruct((H, 1, N), f32)
    out_specs, out_shape = [tok] * 6 + [par] * 3, [seq] * 6 + [pout] * 3
    if hosting:
        out_specs.append(hbm)
        out_shape.append(jax.ShapeDtypeStruct((3,) + late_part.shape[1:], late_part.dtype))
    return pl.pallas_call(
        body, name="wkv_bwd", grid=(nc,),
        in_specs=([tok] * 8 + [blk] * 3 + [pl.BlockSpec((1, H, N, N), lambda i: (nc - 1 - i, 0, 0, 0))] + [par] * 3
                  + [hbm] * hosting),
        out_specs=out_specs, out_shape=out_shape,
        scratch_shapes=[pltpu.VMEM((H, N, N), f32), pltpu.VMEM((H, 1, N), f32)] + (CROSS_SEMS if hosting else []),
        compiler_params=_cparams(("arbitrary",)),
    )(dy, g, r, lw, k, v, kkr, a, wkv, U, inv, S0s, r_k, ln_w, ln_b, *([late_part] if hosting else []))


ATTN_TT = 2048


def _attn_rows(d, i, j):
    return pl.ds(ATTN_BLK * d * i + j, ATTN_BLK, stride=d) if d > 1 else pl.ds(ATTN_BLK * i, ATTN_BLK)


def _attn_take(ref, d, nsub):
    return jnp.stack([ref[_attn_rows(d, i, j), :] for i in range(nsub) for j in range(d)], axis=0)


def _attn_put(ref, val, d):
    for i in range(val.shape[0] // d):
        for j in range(d):
            ref[_attn_rows(d, i, j), :] = val[i * d + j]


def _attn_prev(cur, before, d):
    return before if cur.shape[0] == d else jnp.concatenate([before, cur[:cur.shape[0] - d]], axis=0)


def _attn_specs(gi, d, nt, reverse):
    per_tile = ATTN_TT // (ATTN_BLK * d)

    def tile(n):
        return nt - 1 - n if reverse else n

    def col(kind):
        return lambda hp, n: (tile(n), kind * (ATTN_W // 128) + 2 * gi + hp)

    def col_before(kind):
        return lambda hp, n: (jnp.maximum(tile(n) * per_tile - 1, 0), kind * (ATTN_W // 128) + 2 * gi + hp)

    cur = [pl.BlockSpec((ATTN_TT, 128), col(kind)) for kind in range(3)]
    before = [pl.BlockSpec((ATTN_BLK * d, 128), col_before(kind)) for kind in (1, 2)]
    own = pl.BlockSpec((ATTN_TT, 128), lambda hp, n: (tile(n), hp))
    return cur, before, own, tile


def _attn_norm(x, gain, scale):
    rs = lax.rsqrt(jnp.mean(x * x, axis=-1, keepdims=True) + RMS_EPS)
    return x * rs * (gain * scale), rs


def _attn_scores(qn, kn_c, kn_p, first):
    s_c = jnp.einsum("gqe,gke->gqk", qn.astype(bf16), kn_c.astype(bf16), preferred_element_type=f32)
    s_p = jnp.einsum("gqe,gke->gqk", qn.astype(bf16), kn_p.astype(bf16), preferred_element_type=f32)
    qi = lax.broadcasted_iota(jnp.int32, (1, ATTN_BLK, ATTN_BLK), 1)
    ki = lax.broadcasted_iota(jnp.int32, (1, ATTN_BLK, ATTN_BLK), 2)
    s_c = jnp.where(qi >= ki, s_c, NEG_INF)
    s_p = jnp.where(jnp.logical_and(ki >= qi, jnp.logical_not(first)), s_p, NEG_INF)
    return s_c, s_p


def _attn_fwd(pqkv, qg, kg, gi, S):
    d = ATTN_PAIRS[gi][1]
    nt = S // ATTN_TT
    nsub = ATTN_TT // (ATTN_BLK * d)
    nd = nsub * d

    def body(q_ref, k_ref, v_ref, kb_ref, vb_ref, qg_ref, kg_ref, o_ref, lse_ref):
        Q, K, V = _attn_take(q_ref, d, nsub), _attn_take(k_ref, d, nsub), _attn_take(v_ref, d, nsub)
        KB, VB = _attn_take(kb_ref, d, 1), _attn_take(vb_ref, d, 1)
        first = jnp.logical_and(lax.broadcasted_iota(jnp.int32, (nd, 1, 1), 0) < d, pl.program_id(1) == 0)
        outs, lses = [], []
        for h in range(2):
            sl = slice(h * HEAD, (h + 1) * HEAD)
            kc, vc = K[:, :, sl], V[:, :, sl]
            kp, vp = _attn_prev(kc, KB[:, :, sl], d), _attn_prev(vc, VB[:, :, sl], d)
            qn, _ = _attn_norm(Q[:, :, sl], qg_ref[...], HEAD ** -0.5)
            kn_c, _ = _attn_norm(kc, kg_ref[...], 1.0)
            kn_p, _ = _attn_norm(kp, kg_ref[...], 1.0)
            s_c, s_p = _attn_scores(qn, kn_c, kn_p, first)
            m = jnp.maximum(jnp.max(s_c, axis=-1, keepdims=True), jnp.max(s_p, axis=-1, keepdims=True))
            p_c = jnp.exp(s_c - m)
            p_p = jnp.exp(s_p - m)
            den = jnp.sum(p_c, axis=-1, keepdims=True) + jnp.sum(p_p, axis=-1, keepdims=True)
            inv = 1.0 / den
            o = jnp.einsum("gqk,gke->gqe", (p_c * inv).astype(bf16), vc.astype(bf16), preferred_element_type=f32)
            o += jnp.einsum("gqk,gke->gqe", (p_p * inv).astype(bf16), vp.astype(bf16), preferred_element_type=f32)
            outs.append(o)
            lses.append(jnp.broadcast_to(m + jnp.log(den), o.shape))
        _attn_put(o_ref, jnp.concatenate(outs, axis=-1), d)
        _attn_put(lse_ref, jnp.concatenate(lses, axis=-1), d)

    cur, before, own, _ = _attn_specs(gi, d, nt, False)
    par = pl.BlockSpec((1, HEAD), lambda hp, n: (0, 0))
    shp = jax.ShapeDtypeStruct((S, 2 * 128), f32)
    return pl.pallas_call(
        body, name=f"attn_fwd{gi}", grid=(2, nt), in_specs=cur + before + [par] * 2, out_specs=[own, own],
        out_shape=[shp, shp], compiler_params=_cparams(("arbitrary", "arbitrary")),
    )(pqkv, pqkv, pqkv, pqkv, pqkv, qg, kg)


def _attn_bwd(pqkv, o, lse, do, dlse, qg, kg, gi, S):
    d = ATTN_PAIRS[gi][1]
    nt = S // ATTN_TT
    nsub = ATTN_TT // (ATTN_BLK * d)
    nd = nsub * d

    def norm_bwd(dxn, x, rs, gain, scale):
        xh = x * rs
        dxh = dxn * (gain * scale)
        dx = rs * (dxh - xh * jnp.mean(dxh * xh, axis=-1, keepdims=True))
        dgain = jnp.sum(jnp.sum(dxn * xh * scale, axis=1), axis=0, keepdims=True)
        return dx, dgain

    def to_before(part, carried):
        return carried if nsub == 1 else jnp.concatenate([part[d:], carried], axis=0)

    def body(q_ref, k_ref, v_ref, kb_ref, vb_ref, o_ref, lse_ref, do_ref, dlse_ref, qg_ref, kg_ref,
             dq_ref, dk_ref, dv_ref, dqg_ref, dkg_ref, carry_k, carry_v):
        step = pl.program_id(1)

        @pl.when(jnp.logical_and(pl.program_id(0) == 0, step == 0))
        def _():
            dqg_ref[...] = jnp.zeros(dqg_ref.shape, f32)
            dkg_ref[...] = jnp.zeros(dkg_ref.shape, f32)

        @pl.when(step == 0)
        def _():
            carry_k[...] = jnp.zeros(carry_k.shape, f32)
            carry_v[...] = jnp.zeros(carry_v.shape, f32)

        Q, K, V = _attn_take(q_ref, d, nsub), _attn_take(k_ref, d, nsub), _attn_take(v_ref, d, nsub)
        KB, VB = _attn_take(kb_ref, d, 1), _attn_take(vb_ref, d, 1)
        O, LSE = _attn_take(o_ref, d, nsub), _attn_take(lse_ref, d, nsub)
        DO, DLSE = _attn_take(do_ref, d, nsub), _attn_take(dlse_ref, d, nsub)
        first = jnp.logical_and(lax.broadcasted_iota(jnp.int32, (nd, 1, 1), 0) < d, step == nt - 1)
        qg, kg = qg_ref[...], kg_ref[...]
        dqs, dks, dvs = [], [], []
        for h in range(2):
            sl = slice(h * HEAD, (h + 1) * HEAD)
            qx, kx, vc = Q[:, :, sl], K[:, :, sl], V[:, :, sl]
            kpx, vp = _attn_prev(kx, KB[:, :, sl], d), _attn_prev(vc, VB[:, :, sl], d)
            qn, rq = _attn_norm(qx, qg, HEAD ** -0.5)
            kn_c, rk_c = _attn_norm(kx, kg, 1.0)
            kn_p, _ = _attn_norm(kpx, kg, 1.0)
            s_c, s_p = _attn_scores(qn, kn_c, kn_p, first)
            lse = LSE[:, :, h * HEAD:h * HEAD + 1]
            p_c = jnp.exp(s_c - lse)
            p_p = jnp.exp(s_p - lse)
            dO = DO[:, :, sl]
            dOb = dO.astype(bf16)
            dp_c = jnp.einsum("gqe,gke->gqk", dOb, vc.astype(bf16), preferred_element_type=f32)
            dp_p = jnp.einsum("gqe,gke->gqk", dOb, vp.astype(bf16), preferred_element_type=f32)
            corr = DLSE[:, :, h * HEAD:h * HEAD + 1] - jnp.sum(dO * O[:, :, sl], axis=-1, keepdims=True)
            ds_c = (p_c * (dp_c + corr)).astype(bf16)
            ds_p = (p_p * (dp_p + corr)).astype(bf16)
            qnb = qn.astype(bf16)
            dqn = (jnp.einsum("gqk,gke->gqe", ds_c, kn_c.astype(bf16), preferred_element_type=f32)
                   + jnp.einsum("gqk,gke->gqe", ds_p, kn_p.astype(bf16), preferred_element_type=f32))
            dkn_p = jnp.einsum("gqk,gqe->gke", ds_p, qnb, preferred_element_type=f32)
            dv_p = jnp.einsum("gqk,gqe->gke", p_p.astype(bf16), dOb, preferred_element_type=f32)
            dkn = jnp.einsum("gqk,gqe->gke", ds_c, qnb, preferred_element_type=f32) + to_before(dkn_p, carry_k[h])
            dv = (jnp.einsum("gqk,gqe->gke", p_c.astype(bf16), dOb, preferred_element_type=f32)
                  + to_before(dv_p, carry_v[h]))
            carry_k[h] = dkn_p[:d]
            carry_v[h] = dv_p[:d]
            dq, dqg = norm_bwd(dqn, qx, rq, qg, HEAD ** -0.5)
            dk, dkg = norm_bwd(dkn, kx, rk_c, kg, 1.0)
            dqg_ref[...] += dqg
            dkg_ref[...] += dkg
            dqs.append(dq)
            dks.append(dk)
            dvs.append(dv)
        _attn_put(dq_ref, jnp.concatenate(dqs, axis=-1), d)
        _attn_put(dk_ref, jnp.concatenate(dks, axis=-1), d)
        _attn_put(dv_ref, jnp.concatenate(dvs, axis=-1), d)

    cur, before, own, _ = _attn_specs(gi, d, nt, True)
    par = pl.BlockSpec((1, HEAD), lambda hp, n: (0, 0))
    shp = jax.ShapeDtypeStruct((S, 2 * 128), f32)
    pshp = jax.ShapeDtypeStruct((1, HEAD), f32)
    return pl.pallas_call(
        body, name=f"attn_bwd{gi}", grid=(2, nt), in_specs=cur + before + [own] * 4 + [par] * 2,
        out_specs=[own] * 3 + [par] * 2, out_shape=[shp] * 3 + [pshp] * 2,
        scratch_shapes=[pltpu.VMEM((2, d, ATTN_BLK, HEAD), f32)] * 2,
        compiler_params=_cparams(("arbitrary", "arbitrary")),
    )(pqkv, pqkv, pqkv, pqkv, pqkv, o, lse, do, dlse, qg, kg)


def _rms(x, g):
    rs = lax.rsqrt(jnp.mean(x * x, axis=-1, keepdims=True) + RMS_EPS)
    return x * rs * g


def _f_rms(x, g):
    return _rms(x, g)


def _f_resid_rms(coef, x, f, g):
    xn = x + coef * f
    return xn, _rms(xn, g)


def _f_swiglu(u):
    gate, up = u[:, :D_FF], u[:, D_FF:]
    return gate * jax.nn.sigmoid(gate) * up


def _f_swiglu_bwd(dact, u):
    gate, up = u[:, :D_FF], u[:, D_FF:]
    sg = jax.nn.sigmoid(gate)
    silu = gate * sg
    dact = 0.5 * dact
    return jnp.concatenate([dact * up * (sg * (1.0 + gate * (1.0 - sg))), dact * silu], axis=1)


def _f_rms_bwd(n_parts, *args):
    dns = args[:n_parts]
    x, dres, g = args[n_parts:]
    dn = dns[0]
    for t in dns[1:]:
        dn = dn + t
    rs = lax.rsqrt(jnp.mean(x * x, axis=-1, keepdims=True) + RMS_EPS)
    xh = x * rs
    dxh = dn * g
    dx = dres + rs * (dxh - xh * jnp.mean(dxh * xh, axis=-1, keepdims=True))
    return dx, dx, jnp.sum(dn * xh, axis=0, keepdims=True)


def _f_loss(x, f, tgt):
    y = x + 0.5 * f
    diff = y - tgt
    part = 0.5 * jnp.sum(jnp.mean(diff * diff, axis=-1, keepdims=True), axis=0, keepdims=True)
    dy = diff * (1.0 / D)
    return dy, dy, jnp.broadcast_to(part, (1, 128))


def _dotb(a, b, dims):
    return lax.dot_general(a.astype(bf16), b.astype(bf16), dims, preferred_element_type=f32)


_NN = (((1,), (0,)), ((), ()))
_NT = (((1,), (1,)), ((), ()))
_TN = (((0,), (0,)), ((), ()))


def _rwkv_pre_core(prkv, prkv_prev, plora, plora_prev, mu_rkv, mu_lora, w0, w2p, a0, a2p, g2p, k_k, k_a):
    xs = prkv + (prkv_prev - prkv) * mu_rkv
    xl = plora + (plora_prev - plora) * mu_lora
    r, k, v = xs[:, :D], xs[:, D:2 * D], xs[:, 2 * D:]
    wd, ad, gd = xl[:, :128], xl[:, 128:256], xl[:, 256:]
    tw = jnp.tanh(wd)
    zw = w0 + _dotb(tw, w2p, _NN)
    sp = jnp.maximum(-zw, 0.0) + jnp.log(1.0 + jnp.exp(-jnp.abs(zw)))
    lw = -jnp.exp(-sp - 0.5)
    a = jax.nn.sigmoid(a0 + _dotb(ad, a2p, _NN))
    sg = jax.nn.sigmoid(gd)
    return dict(r=r, k=k, v=v, tw=tw, zw=zw, lw=lw, a=a, sg=sg, ad=ad)


def _rows_down(x, halo, blk):
    before = jnp.where(blk > 0, halo[HALO - 1:HALO, :], 0.0)
    row = lax.broadcasted_iota(jnp.int32, (x.shape[0], 1), 0)
    return jnp.where(row == 0, before, pltpu.roll(x, 1, 0))


def _rows_up(x, after):
    n = x.shape[0]
    row = lax.broadcasted_iota(jnp.int32, (n, 1), 0)
    return jnp.where(row == n - 1, after, pltpu.roll(x, n - 1, 0))


def _f_rwkv_pre(prkv, plora, mu_rkv, mu_lora, w0, w2p, a0, a2p, g2p, k_k, k_a, halo_rkv, halo_lora, blk):
    c = _rwkv_pre_core(prkv, _rows_down(prkv, halo_rkv, blk), plora, _rows_down(plora, halo_lora, blk),
                       mu_rkv, mu_lora, w0, w2p, a0, a2p, g2p, k_k, k_a)
    g = _dotb(c["sg"], g2p, _NN)
    k, a = c["k"], c["a"]
    return c["r"], c["lw"], k * (1.0 + (a - 1.0) * k_a), c["v"], k * k_k, a, g


def _f_rwkv_pre_bwd(prkv, plora, dr, dlw, dk2, dv, dkkr, da, dya, yap,
                    mu_rkv, mu_lora, w0, w2p, a0, a2p, g2p, k_k, k_a, halo_rkv, halo_lora, next_rkv, next_lora, blk):
    prkv_prev, plora_prev = _rows_down(prkv, halo_rkv, blk), _rows_down(plora, halo_lora, blk)
    c = _rwkv_pre_core(prkv, prkv_prev, plora, plora_prev, mu_rkv, mu_lora, w0, w2p, a0, a2p, g2p, k_k, k_a)
    k, a, sg, tw, zw, lw = c["k"], c["a"], c["sg"], c["tw"], c["zw"], c["lw"]
    dg = dya * yap
    dsg = _dotb(dg, g2p, _NT)
    dgd = dsg * sg * (1.0 - sg)
    dg2p = _dotb(sg, dg, _TN)
    dk = dk2 * (1.0 + (a - 1.0) * k_a) + dkkr * k_k
    da_t = da + dk2 * k * k_a
    dk_a = jnp.sum(dk2 * k * (a - 1.0), axis=0, keepdims=True)
    dk_k = jnp.sum(dkkr * k, axis=0, keepdims=True)
    dza = da_t * a * (1.0 - a)
    da0 = jnp.sum(dza, axis=0, keepdims=True)
    dad = _dotb(dza, a2p, _NT)
    da2p = _dotb(c["ad"], dza, _TN)
    dzw = dlw * lw * jax.nn.sigmoid(-zw)
    dw0 = jnp.sum(dzw, axis=0, keepdims=True)
    dtw = _dotb(dzw, w2p, _NT)
    dw2p = _dotb(tw, dzw, _TN)
    dwd = dtw * (1.0 - tw * tw)
    dxs = jnp.concatenate([dr, dk, dv], axis=1)
    dxl = jnp.concatenate([dwd, dad, dgd], axis=1)
    dmu_rkv = jnp.sum(dxs * (prkv_prev - prkv), axis=0, keepdims=True)
    dmu_lora = jnp.sum(dxl * (plora_prev - plora), axis=0, keepdims=True)
    to_next_rkv, to_next_lora = dxs * mu_rkv, dxl * mu_lora
    return (dxs * (1.0 - mu_rkv) + _rows_up(to_next_rkv, next_rkv), dxl * (1.0 - mu_lora) + _rows_up(to_next_lora, next_lora),
            dmu_rkv, dmu_lora, dw0, da0, dk_k, dk_a, dw2p, da2p, dg2p, to_next_rkv[0:1], to_next_lora[0:1])


def _group_alpha(l0, l1, l2):
    m = jnp.maximum(jnp.maximum(l0, l1), l2)
    e0, e1, e2 = jnp.exp(l0 - m), jnp.exp(l1 - m), jnp.exp(l2 - m)
    inv = 1.0 / (e0 + e1 + e2)
    return jnp.concatenate([e0 * inv, e1 * inv, e2 * inv], axis=1)


def _f_combine(o0, o1, o2, l0, l1, l2):
    return jnp.concatenate([o0, o1, o2], axis=1) * _group_alpha(l0, l1, l2)


def _f_combine_bwd(dyb, o0, o1, o2, l0, l1, l2, bd):
    alpha = _group_alpha(l0, l1, l2)
    e = jnp.dot(dyb * jnp.concatenate([o0, o1, o2], axis=1), bd, precision=HI, preferred_element_type=f32)
    ae = alpha * e
    tot = ae[:, :256] + ae[:, 256:512] + ae[:, 512:]
    do = dyb * alpha
    dl = ae - alpha * jnp.concatenate([tot, tot, tot], axis=1)
    return do[:, :256], do[:, 256:512], do[:, 512:], dl[:, :256], dl[:, 256:512], dl[:, 512:]


def _f_merge(pgate, ta, tb, b_gate):
    gate = jax.nn.sigmoid(pgate + b_gate)
    return gate[:, :D] * ta + gate[:, D:] * tb


def _f_merge_bwd(dm, pgate, ta, tb, b_gate):
    gate = jax.nn.sigmoid(pgate + b_gate)
    ga, gb = gate[:, :D], gate[:, D:]
    dpg = jnp.concatenate([dm * ta * ga * (1.0 - ga), dm * tb * gb * (1.0 - gb)], axis=1)
    return dm * ga, dm * gb, dpg, jnp.sum(dpg, axis=0, keepdims=True)


def _f_adamw(w, g, m, v):
    m2 = ADAM_B1 * m + (1.0 - ADAM_B1) * g
    v2 = ADAM_B2 * v + (1.0 - ADAM_B2) * jnp.square(g)
    m_hat = m2 / (1.0 - ADAM_B1 ** ADAM_STEP)
    v_hat = v2 / (1.0 - ADAM_B2 ** ADAM_STEP)
    delta = -ADAM_LR * (m_hat / (jnp.sqrt(v_hat) + ADAM_EPS) + ADAM_WD * w)
    return delta, m2, v2


def _ffn_fwd(tag, n, WiT, Wo):
    S = n.shape[0]
    u = _mm(f"{tag}_up", n, WiT, "nt")
    (act,) = _rowwise(f"{tag}_swiglu", _f_swiglu, [u], [], [(D_FF, bf16)])
    f = _mm(f"{tag}_down", act, Wo, "nn")
    return u, act, f


def _ffn_bwd(tag, dxo, dxo_b, x_in, n, u, act, g, WiT, Wo):
    dact = _mm(f"{tag}_dact", dxo_b, Wo, "nt")
    dWo = _mm(f"{tag}_dwo", act, dxo_b, "tn", out_dtype=GRAD_WIRE, scale=0.5)
    (du,) = _rowwise(f"{tag}_dswiglu", _f_swiglu_bwd, [dact, u], [], [(2 * D_FF, bf16)], tm=128)
    dn = _mm(f"{tag}_dn", du, WiT, "nn")
    dWiT = _mm(f"{tag}_dwi", du, n, "tn", out_dtype=GRAD_WIRE)
    dx, dx_b, dg = _rowwise(f"{tag}_drms", functools.partial(_f_rms_bwd, 1), [dn, x_in, dxo], [g],
                            [(D, f32), (D, bf16)], [(1, D)])
    return dx, dx_b, dg, dWiT, dWo


def _local_step(x0, tgt, W, P, late=None):
    S = x0.shape[0]
    (n1,) = _rowwise("f1_rms", _f_rms, [x0], [P["ffn1_norm"]], [(D, bf16)])
    u1, act1, f1 = _ffn_fwd("f1", n1, W["f1_iT"], W["f1_o"])
    x1, h = _rowwise("mix_rms", functools.partial(_f_resid_rms, 0.5), [x0, f1], [P["mix_norm"]],
                     [(D, f32), (D, bf16)])
    prkv = _mm("p_rkv", h, W["in_rkvT"], "nt")
    plora = _mm("p_lora", h, W["in_loraT"], "nt")
    pqkv = _mm("p_qkv", h, W["in_qkvT"], "nt")
    pgate = _mm("p_gate", h, W["in_gateT"], "nt")
    pre_params = [P["mu_rkv"], P["mu_lora"], P["w0"], W["w2p"], P["a0"], W["a2p"], W["g2p"], P["k_k"], P["k_a"]]
    r, lw, k2, v, kkr, a, g = _rowwise("rwkv_pre", _f_rwkv_pre, [prkv, plora], pre_params, [(D, f32)] * 7, tm=128,
                                       halos=(0, 1))
    hm = [r, lw, k2, v, kkr, a]
    hp = [P["r_k"].reshape(RW_HEADS, 1, HEAD), P["ln_w"].reshape(RW_HEADS, 1, HEAD), P["ln_b"].reshape(RW_HEADS, 1, HEAD)]
    if late is None:
        yap, ya, wkv_h, U_h, inv_h, S0s = _wkv_fwd(*hm, g, *hp)
    else:
        yap, ya, wkv_h, U_h, inv_h, S0s, gathered_late = _wkv_fwd(*hm, g, *hp, late_pack=late["pack"])
        W = {**W, **late["weights"](gathered_late)}
    ta = _mm("proj_a", ya, W["pr"], "nn")
    n_grp = len(ATTN_PAIRS)
    attn = [_attn_fwd(pqkv, P["q_norm"], P["k_norm"], gi, S) for gi in range(n_grp)]
    o_g, lse_g = [t[0] for t in attn], [t[1] for t in attn]
    (yb,) = _rowwise("attn_combine", _f_combine, [*o_g, *lse_g], [], [(ATTN_W, bf16)])
    tb = _mm("proj_b", yb, W["paT"], "nt")
    (merged,) = _rowwise("merge", _f_merge, [pgate, ta, tb], [P["b_gate"]], [(D, bf16)])
    mo = _mm("mix_out", merged, W["out"], "nn")
    x2, n2 = _rowwise("f2_rms", functools.partial(_f_resid_rms, 1.0), [x1, mo], [P["ffn2_norm"]],
                      [(D, f32), (D, bf16)])
    u2, act2, f2 = _ffn_fwd("f2", n2, W["f2_iT"], W["f2_o"])
    dx3, dx3_b, loss = _rowwise("loss", _f_loss, [x2, f2, tgt], [], [(D, f32), (D, bf16)], [(1, 128)])
    G, Gs = {}, {}
    dx2, dx2_b, Gs["ffn2_norm"], G["f2_iT"], G["f2_o"] = _ffn_bwd("f2", dx3, dx3_b, x2, n2, u2, act2, P["ffn2_norm"],
                                                                 W["f2_iT"], W["f2_o"])
    dmerged = _mm("d_merged", dx2_b, W["out"], "nt")
    G["out"] = _mm("dw_out", merged, dx2_b, "tn", out_dtype=GRAD_WIRE)
    dta, dtb, dpgate, Gs["b_gate"] = _rowwise("merge_bwd", _f_merge_bwd, [dmerged, pgate, ta, tb], [P["b_gate"]],
                                              [(D, bf16), (D, bf16), (2 * D, bf16)], [(1, 2 * D)])
    dya = _mm("d_ya", dta, W["pr"], "nt")
    G["pr"] = _mm("dw_pr", ya, dta, "tn", out_dtype=GRAD_WIRE)
    dyb = _mm("d_yb", dtb, W["paT"], "nn")
    G["paT"] = _mm("dw_pa", dtb, yb, "tn", out_dtype=GRAD_WIRE)
    if late is None:
        hg = _wkv_bwd(dya, g, *hm, wkv_h, U_h, inv_h, S0s, *hp)
    else:
        part_late = late["reduce_start"](G)
        hg = _wkv_bwd(dya, g, *hm, wkv_h, U_h, inv_h, S0s, *hp, late_part=part_late)
        G["late"] = (part_late, hg[9])
    dr, dlw, dk2, dv, dkkr, da = hg[:6]
    Gs["r_k"], Gs["ln_w"], Gs["ln_b"] = (t.reshape(1, D) for t in hg[6:9])
    lp = sum(LORA_PAD)
    (dprkv, dplora, Gs["mu_rkv"], Gs["mu_lora"], Gs["w0"], Gs["a0"], Gs["k_k"], Gs["k_a"],
     dw2p, da2p, dg2p) = _rowwise(
        "rwkv_pre_bwd", _f_rwkv_pre_bwd,
        [prkv, plora, dr, dlw, dk2, dv, dkkr, da, dya, yap], pre_params,
        [(3 * D, bf16), (lp, bf16)],
        [(1, 3 * D), (1, lp), (1, D), (1, D), (1, D), (1, D), (LORA_PAD[0], D), (LORA_PAD[1], D), (LORA_PAD[2], D)],
        tm=128, halos=(0, 1), carries=((1, 3 * D), (1, lp)), reverse=True)
    G["w2T"], G["a2T"], G["g2T"] = dw2p[:LORA_W[0]].T, da2p[:LORA_W[1]].T, dg2p[:LORA_W[2]].T
    bd = (jnp.arange(ATTN_W)[:, None] // HEAD == jnp.arange(ATTN_W)[None, :] // HEAD).astype(f32)
    dol = _rowwise("attn_combine_bwd", _f_combine_bwd, [dyb, *o_g, *lse_g], [bd], [(ATTN_W // n_grp, f32)] * (2 * n_grp))
    dattn = [_attn_bwd(pqkv, o_g[gi], lse_g[gi], dol[gi], dol[n_grp + gi], P["q_norm"], P["k_norm"], gi, S)
             for gi in range(n_grp)]
    Gs["q_norm"] = dattn[0][3] + dattn[1][3] + dattn[2][3]
    Gs["k_norm"] = dattn[0][4] + dattn[1][4] + dattn[2][4]
    dpqkv = jnp.concatenate([dattn[gi][kind] for kind in range(3) for gi in range(n_grp)], axis=1).astype(bf16)
    dh = [_mm("dh_rkv", dprkv, W["in_rkvT"], "nn"), _mm("dh_lora", dplora, W["in_loraT"], "nn"),
          _mm("dh_qkv", dpqkv, W["in_qkvT"], "nn"), _mm("dh_gate", dpgate, W["in_gateT"], "nn")]
    dW_rkv = _mm("dw_rkv", dprkv, h, "tn", out_dtype=GRAD_WIRE)
    dW_lora = _mm("dw_lora", dplora, h, "tn", out_dtype=GRAD_WIRE)
    dW_qkv = _mm("dw_qkv", dpqkv, h, "tn", out_dtype=GRAD_WIRE)
    dW_gate = _mm("dw_gate", dpgate, h, "tn", out_dtype=GRAD_WIRE)
    o1, o2 = LORA_PAD[0], LORA_PAD[0] + LORA_PAD[1]
    G["inT"] = jnp.concatenate([dW_rkv, dW_lora[:LORA_W[0]], dW_lora[o1:o1 + LORA_W[1]], dW_lora[o2:o2 + LORA_W[2]],
                                dW_qkv, dW_gate], axis=0)
    dx1, dx1_b, Gs["mix_norm"] = _rowwise("mix_drms", functools.partial(_f_rms_bwd, 4), [*dh, x1, dx2],
                                          [P["mix_norm"]], [(D, f32), (D, bf16)], [(1, D)])
    dx0, _, Gs["ffn1_norm"], G["f1_iT"], G["f1_o"] = _ffn_bwd("f1", dx1, dx1_b, x0, n1, u1, act1, P["ffn1_norm"],
                                                             W["f1_iT"], W["f1_o"])
    return loss[0, 0], dx0, G, Gs


def _peer(k):
    x, y, c = lax.axis_index("x"), lax.axis_index("y"), lax.axis_index("c")
    px = 1 - x if k & 4 else x
    py = 1 - y if k & 2 else y
    pc = 1 - c if k & 1 else c
    return (px, py, pc), 4 * px + 2 * py + pc


def _gather_phases(x_ref, out_ref, send_sems, recv_sems, local_sem):
    x, y, c = lax.axis_index("x"), lax.axis_index("y"), lax.axis_index("c")
    me, sibling = (x, y, c), (x, y, 1 - c)
    chips = [(1 - x, y), (x, 1 - y), (1 - x, 1 - y)]

    def slot(px, py, pc):
        return out_ref.at[4 * px + 2 * py + pc]

    def copy(k, block, to, src=None):
        return pltpu.make_async_remote_copy(
            src_ref=slot(*block) if src is None else src, dst_ref=slot(*block), send_sem=send_sems.at[k],
            recv_sem=recv_sems.at[k], device_id=to, device_id_type=MESH)

    def mine():
        return pltpu.make_async_copy(x_ref, slot(*me), local_sem)

    def first():
        return [copy(0, me, sibling, src=x_ref)] + [copy(1 + j, me, (*chip, c), src=x_ref) for j, chip in enumerate(chips)]

    def passed():
        return [copy(4 + j, (*chip, c), sibling) for j, chip in enumerate(chips)]

    def start():
        mine().start()
        for cp in first():
            cp.start()

    def forward():
        for j, (chip, cp) in enumerate(zip(chips, passed())):
            copy(1 + j, (*chip, c), me).wait_recv()
            cp.start()

    def finish():
        copy(0, sibling, me).wait_recv()
        for j, chip in enumerate(chips):
            copy(4 + j, (*chip, 1 - c), me).wait_recv()
        for cp in first() + passed():
            cp.wait_send()
        mine().wait()

    return start, forward, finish


GATHER_SEMS = [pltpu.SemaphoreType.DMA((N_DEV - 1,)), pltpu.SemaphoreType.DMA((N_DEV - 1,)), pltpu.SemaphoreType.DMA(())]


def _all_gather(pack):
    R, C = pack.shape

    def body(x_ref, out_ref, send_sems, recv_sems, local_sem):
        for phase in _gather_phases(x_ref, out_ref, send_sems, recv_sems, local_sem):
            phase()

    return pl.pallas_call(
        body, name="weight_all_gather", out_shape=jax.ShapeDtypeStruct((N_DEV, R, C), pack.dtype),
        in_specs=[pl.BlockSpec(memory_space=pl.ANY)], out_specs=pl.BlockSpec(memory_space=pl.ANY),
        scratch_shapes=GATHER_SEMS,
    )(pack)


def _cross_phases(p_ref, out_ref, send_sems, recv_sems):
    x, y, c = lax.axis_index("x"), lax.axis_index("y"), lax.axis_index("c")

    def copies():
        out = []
        for j, (fx, fy) in enumerate([(1, 0), (0, 1), (1, 1)]):
            px = 1 - x if fx else x
            py = 1 - y if fy else y
            out.append(pltpu.make_async_remote_copy(src_ref=p_ref.at[2 * px + py], dst_ref=out_ref.at[j],
                                                    send_sem=send_sems.at[j], recv_sem=recv_sems.at[j],
                                                    device_id=(px, py, c), device_id_type=MESH))
        return out

    def start():
        for cp in copies():
            cp.start()

    def finish():
        for cp in copies():
            cp.wait()

    return start, finish


CROSS_SEMS = [pltpu.SemaphoreType.DMA((3,)), pltpu.SemaphoreType.DMA((3,))]


N_CHIP = 4


def _grad_pair(pieces, tag):
    n = len(pieces)
    C = pieces[0].shape[2]
    rows = [p.shape[1] for p in pieces]
    offs = [sum(rows[:i]) for i in range(n)]
    R = sum(rows)

    def body(*refs):
        g_refs, (other_ref, send_sems, recv_sems) = refs[:n], refs[n:]
        x, y, c = lax.axis_index("x"), lax.axis_index("y"), lax.axis_index("c")
        copies = []
        for i, g_ref in enumerate(g_refs):
            for k in range(N_CHIP):
                cp = pltpu.make_async_remote_copy(
                    src_ref=g_ref.at[4 * (k // 2) + 2 * (k % 2) + 1 - c], dst_ref=other_ref.at[k, pl.ds(offs[i], rows[i])],
                    send_sem=send_sems.at[i * N_CHIP + k], recv_sem=recv_sems.at[i * N_CHIP + k],
                    device_id=(x, y, 1 - c), device_id_type=MESH)
                cp.start()
                copies.append(cp)
        for cp in copies:
            cp.wait()

    return pl.pallas_call(
        body, name=f"grad_pair_{tag}", out_shape=jax.ShapeDtypeStruct((N_CHIP, R, C), pieces[0].dtype),
        in_specs=[pl.BlockSpec(memory_space=pl.ANY)] * n, out_specs=pl.BlockSpec(memory_space=pl.ANY),
        scratch_shapes=[pltpu.SemaphoreType.DMA((n * N_CHIP,))] * 2,
    )(*pieces)


def _pair_add(pieces, other, c, tag):
    n = len(pieces)
    C = pieces[0].shape[2]
    nblk = [p.shape[1] // PACK_BLOCK for p in pieces]
    lo = [sum(nblk[:i]) for i in range(n)]
    R = sum(nblk) * PACK_BLOCK

    def body(c_ref, *refs):
        g_refs, o_ref, out_ref = refs[:n], refs[n], refs[n + 1]
        rb = pl.program_id(1)
        for i in range(n):
            @pl.when(jnp.logical_and(rb >= lo[i], rb < lo[i] + nblk[i]))
            def _(g_ref=g_refs[i]):
                out_ref[...] = (g_ref[...].astype(f32) + o_ref[...].astype(f32)).astype(out_ref.dtype)

    def piece_spec(i):
        return pl.BlockSpec((1, None, PACK_BLOCK, C),
                            lambda k, rb, c_ref: (k, c_ref[0], jnp.clip(rb - lo[i], 0, nblk[i] - 1), 0))

    blk = pl.BlockSpec((1, PACK_BLOCK, C), lambda k, rb, c_ref: (k, rb, 0))
    return pl.pallas_call(
        body, name=f"pair_add_{tag}",
        grid_spec=pltpu.PrefetchScalarGridSpec(
            num_scalar_prefetch=1, grid=(N_CHIP, R // PACK_BLOCK),
            in_specs=[piece_spec(i) for i in range(n)] + [blk], out_specs=blk),
        out_shape=jax.ShapeDtypeStruct((N_CHIP, R, C), other.dtype),
        compiler_params=_cparams(("arbitrary", "arbitrary")),
    )(c, *[p.reshape(N_CHIP, 2, p.shape[1], C) for p in pieces], other)


def _grad_cross(part):
    _, R, C = part.shape

    def body(p_ref, out_ref, send_sems, recv_sems):
        for phase in _cross_phases(p_ref, out_ref, send_sems, recv_sems):
            phase()

    return pl.pallas_call(
        body, name="grad_cross", out_shape=jax.ShapeDtypeStruct((3, R, C), part.dtype),
        in_specs=[pl.BlockSpec(memory_space=pl.ANY)], out_specs=pl.BlockSpec(memory_space=pl.ANY),
        scratch_shapes=CROSS_SEMS,
    )(part)


def _grad_sum(part, recv, my_chip, tr, tag):
    _, R, C = part.shape

    def body(chip_ref, p_ref, r_ref, o_ref):
        acc = p_ref[0].astype(f32)
        for j in range(3):
            acc = acc + r_ref[j].astype(f32)
        o_ref[...] = acc

    return pl.pallas_call(
        body, name=f"grad_sum_{tag}",
        grid_spec=pltpu.PrefetchScalarGridSpec(
            num_scalar_prefetch=1, grid=(R // tr,),
            in_specs=[pl.BlockSpec((1, tr, C), lambda i, chip_ref: (chip_ref[0], i, 0)),
                      pl.BlockSpec((3, tr, C), lambda i, chip_ref: (0, i, 0))],
            out_specs=pl.BlockSpec((tr, C), lambda i, chip_ref: (i, 0))),
        out_shape=jax.ShapeDtypeStruct((R, C), f32),
        compiler_params=_cparams(("arbitrary",)),
    )(my_chip, part, recv)


def _small_all_reduce(small):
    R, C = small.shape

    def body(x_ref, o_ref, buf, send_sems, recv_sems):
        _, me = _peer(0)
        buf[me] = x_ref[...]
        sends = []
        for k in range(1, N_DEV):
            dev, _ = _peer(k)
            cp = pltpu.make_async_remote_copy(src_ref=x_ref, dst_ref=buf.at[me], send_sem=send_sems.at[k - 1],
                                              recv_sem=recv_sems.at[k - 1], device_id=dev, device_id_type=MESH)
            cp.start()
            sends.append(cp)
        for k in range(1, N_DEV):
            dev, idx = _peer(k)
            pltpu.make_async_remote_copy(src_ref=x_ref, dst_ref=buf.at[idx], send_sem=send_sems.at[k - 1],
                                         recv_sem=recv_sems.at[k - 1], device_id=dev, device_id_type=MESH).wait_recv()
        for cp in sends:
            cp.wait_send()
        acc = buf[0]
        for i in range(1, N_DEV):
            acc = acc + buf[i]
        o_ref[...] = acc

    return pl.pallas_call(
        body, name="small_all_reduce", out_shape=jax.ShapeDtypeStruct((R, C), f32),
        in_specs=[pl.BlockSpec(memory_space=pltpu.VMEM)], out_specs=pl.BlockSpec(memory_space=pltpu.VMEM),
        scratch_shapes=[pltpu.VMEM((N_DEV, R, C), f32), pltpu.SemaphoreType.DMA((N_DEV - 1,)),
                        pltpu.SemaphoreType.DMA((N_DEV - 1,))],
    )(small)


_GROUPS_EARLY = ((("ffn1_w_in", True),), (("ffn1_w_out", False),), (("w_in", True),),
                 (("rwkv_w2", True), ("rwkv_a2", True), ("rwkv_g2", True)))
_GROUPS_LATE = ((("w_proj_rwkv", False),), (("w_proj_attn", True),), (("w_out", False),),
                (("ffn2_w_in", True),), (("ffn2_w_out", False),))
_BIG = tuple(item for group in _GROUPS_EARLY + _GROUPS_LATE for item in group)
_SMALL = ("ffn1_norm", "mix_norm", "b_gate", "rwkv_mu", "rwkv_w0", "rwkv_a0", "rwkv_k_k", "rwkv_k_a", "rwkv_r_k",
          "rwkv_ln_w", "rwkv_ln_b", "attn_q_norm", "attn_k_norm", "ffn2_norm")


def _pack_layout(like, groups):
    items, spans, off = {}, [], 0
    for group in groups:
        start = off
        for name, _ in group:
            shp = like[name].shape
            n = shp[0] * shp[1] // D
            items[name] = (off, n)
            off += n
        off = -(-off // PACK_BLOCK) * PACK_BLOCK
        spans.append((start, off - start))
    return items, spans, off


def _pack_big(shards, groups):
    items, _, rows = _pack_layout(shards, groups)
    parts, at = [], 0
    for group in groups:
        for name, tr in group:
            off, n = items[name]
            t = shards[name]
            if off > at:
                parts.append(jnp.zeros((off - at, D), t.dtype))
            parts.append((t.T if tr else t).reshape(n, D))
            at = off + n
    if rows > at:
        parts.append(jnp.zeros((rows - at, D), parts[0].dtype))
    return jnp.concatenate(parts, axis=0)


def _unpack_big(pack, like, groups):
    items, _, _ = _pack_layout(like, groups)
    out = {}
    for group in groups:
        for name, tr in group:
            off, n = items[name]
            shp = like[name].shape
            t = pack[off:off + n]
            out[name] = t.reshape(shp[1], shp[0]).T if tr else t.reshape(shp)
    return out


def _unpack_gathered(gathered, like, groups):
    items, _, _ = _pack_layout(like, groups)
    full = {}
    for group in groups:
        for name, tr in group:
            shp = like[name].shape
            off, rows = items[name]
            r_loc, c_loc = (shp[1], shp[0]) if tr else shp
            full[name] = gathered[:, off:off + rows].reshape(N_DEV * r_loc, c_loc)
    return full


def _grad_pieces(g_full, like, groups):
    items, spans, _ = _pack_layout(like, groups)
    pieces = []
    for group, (_, rows_pad) in zip(groups, spans):
        parts = [g_full[n].astype(GRAD_WIRE).reshape(N_DEV, items[n][1], D) for n, _ in group]
        piece = parts[0] if len(parts) == 1 else jnp.concatenate(parts, axis=1)
        if rows_pad > piece.shape[1]:
            piece = jnp.pad(piece, ((0, 0), (0, rows_pad - piece.shape[1]), (0, 0)))
        pieces.append(piece)
    return pieces


def _small_rows(name, t):
    flat = t.reshape(-1)
    pad = (-flat.shape[0]) % D
    return jnp.pad(flat, (0, pad)).reshape(-1, D)


def _pack_small(vals):
    parts = [_small_rows(n, vals[n]) for n in _SMALL]
    used = sum(p.shape[0] for p in parts)
    parts.append(jnp.zeros((SMALL_ROWS - used, D), f32))
    return jnp.concatenate(parts, axis=0)


def _unpack_small(pack, like):
    out, off = {}, 0
    for n in _SMALL:
        size = like[n].size
        rows = -(-size // D)
        out[n] = pack[off:off + rows].reshape(-1)[:size].reshape(like[n].shape)
        off += rows
    return out


def _build_W_early(full):
    inT = full["w_in"]
    z64, z96 = jnp.zeros((64, D), inT.dtype), jnp.zeros((96, D), inT.dtype)
    return {
        "f1_iT": full["ffn1_w_in"], "f1_o": full["ffn1_w_out"],
        "in_rkvT": inT[:3 * D],
        "in_loraT": jnp.concatenate([inT[3072:3136], z64, inT[3136:3200], z64, inT[3200:3360], z96], axis=0),
        "in_qkvT": inT[3360:3360 + 3 * ATTN_W], "in_gateT": inT[3360 + 3 * ATTN_W:],
        "w2p": jnp.concatenate([full["rwkv_w2"].T, z64], axis=0),
        "a2p": jnp.concatenate([full["rwkv_a2"].T, z64], axis=0),
        "g2p": jnp.concatenate([full["rwkv_g2"].T, z96], axis=0),
    }


def _build_W_late(full):
    return {"pr": full["w_proj_rwkv"], "paT": full["w_proj_attn"], "out": full["w_out"],
            "f2_iT": full["ffn2_w_in"], "f2_o": full["ffn2_w_out"]}


def _build_W(full):
    return {**_build_W_early(full), **_build_W_late(full)}


_G_NAMES = {"ffn1_w_in": "f1_iT", "ffn1_w_out": "f1_o", "w_in": "inT", "rwkv_w2": "w2T", "rwkv_a2": "a2T",
            "rwkv_g2": "g2T", "w_proj_rwkv": "pr", "w_proj_attn": "paT", "w_out": "out", "ffn2_w_in": "f2_iT",
            "ffn2_w_out": "f2_o"}


def _reduce_start(G, like, groups, my_c, tag):
    pieces = _grad_pieces({n: G[_G_NAMES[n]] for group in groups for n, _ in group}, like, groups)
    return _pair_add(pieces, _grad_pair(pieces, tag), my_c, tag)


def _build_P(Wl):
    mu = Wl["rwkv_mu"]
    z64f, z96f = jnp.zeros((1, 64), f32), jnp.zeros((1, 96), f32)
    return {
        "ffn1_norm": Wl["ffn1_norm"][None], "mix_norm": Wl["mix_norm"][None], "ffn2_norm": Wl["ffn2_norm"][None],
        "b_gate": Wl["b_gate"][None], "mu_rkv": mu[None, :3 * D],
        "mu_lora": jnp.concatenate([mu[None, 3072:3136], z64f, mu[None, 3136:3200], z64f, mu[None, 3200:3360], z96f], axis=1),
        "w0": Wl["rwkv_w0"][None], "a0": Wl["rwkv_a0"][None], "k_k": Wl["rwkv_k_k"][None], "k_a": Wl["rwkv_k_a"][None],
        "r_k": Wl["rwkv_r_k"].reshape(1, D), "ln_w": Wl["rwkv_ln_w"][None], "ln_b": Wl["rwkv_ln_b"][None],
        "q_norm": Wl["attn_q_norm"][None], "k_norm": Wl["attn_k_norm"][None],
    }


def kernel(x, ffn1_norm, ffn1_w_in, ffn1_w_out, mix_norm, w_in, b_gate, rwkv_mu, rwkv_w0, rwkv_w2, rwkv_a0, rwkv_a2, rwkv_g2, rwkv_k_k, rwkv_k_a, rwkv_r_k, rwkv_ln_w, rwkv_ln_b, attn_q_norm, attn_k_norm, w_proj_rwkv, w_proj_attn, w_out, ffn2_norm, ffn2_w_in, ffn2_w_out, loss_target, m_ffn1_norm, m_ffn1_w_in, m_ffn1_w_out, m_mix_norm, m_w_in, m_b_gate, m_rwkv_mu, m_rwkv_w0, m_rwkv_w2, m_rwkv_a0, m_rwkv_a2, m_rwkv_g2, m_rwkv_k_k, m_rwkv_k_a, m_rwkv_r_k, m_rwkv_ln_w, m_rwkv_ln_b, m_attn_q_norm, m_attn_k_norm, m_w_proj_rwkv, m_w_proj_attn, m_w_out, m_ffn2_norm, m_ffn2_w_in, m_ffn2_w_out, v_ffn1_norm, v_ffn1_w_in, v_ffn1_w_out, v_mix_norm, v_w_in, v_b_gate, v_rwkv_mu, v_rwkv_w0, v_rwkv_w2, v_rwkv_a0, v_rwkv_a2, v_rwkv_g2, v_rwkv_k_k, v_rwkv_k_a, v_rwkv_r_k, v_rwkv_ln_w, v_rwkv_ln_b, v_attn_q_norm, v_attn_k_norm, v_w_proj_rwkv, v_w_proj_attn, v_w_out, v_ffn2_norm, v_ffn2_w_in, v_ffn2_w_out):
    names = ("ffn1_norm", "ffn1_w_in", "ffn1_w_out", "mix_norm", "w_in", "b_gate", "rwkv_mu", "rwkv_w0", "rwkv_w2",
             "rwkv_a0", "rwkv_a2", "rwkv_g2", "rwkv_k_k", "rwkv_k_a", "rwkv_r_k", "rwkv_ln_w", "rwkv_ln_b",
             "attn_q_norm", "attn_k_norm", "w_proj_rwkv", "w_proj_attn", "w_out", "ffn2_norm", "ffn2_w_in", "ffn2_w_out")
    w_all = (ffn1_norm, ffn1_w_in, ffn1_w_out, mix_norm, w_in, b_gate, rwkv_mu, rwkv_w0, rwkv_w2, rwkv_a0, rwkv_a2,
             rwkv_g2, rwkv_k_k, rwkv_k_a, rwkv_r_k, rwkv_ln_w, rwkv_ln_b, attn_q_norm, attn_k_norm, w_proj_rwkv,
             w_proj_attn, w_out, ffn2_norm, ffn2_w_in, ffn2_w_out)
    m_all = (m_ffn1_norm, m_ffn1_w_in, m_ffn1_w_out, m_mix_norm, m_w_in, m_b_gate, m_rwkv_mu, m_rwkv_w0, m_rwkv_w2,
             m_rwkv_a0, m_rwkv_a2, m_rwkv_g2, m_rwkv_k_k, m_rwkv_k_a, m_rwkv_r_k, m_rwkv_ln_w, m_rwkv_ln_b,
             m_attn_q_norm, m_attn_k_norm, m_w_proj_rwkv, m_w_proj_attn, m_w_out, m_ffn2_norm, m_ffn2_w_in, m_ffn2_w_out)
    v_all = (v_ffn1_norm, v_ffn1_w_in, v_ffn1_w_out, v_mix_norm, v_w_in, v_b_gate, v_rwkv_mu, v_rwkv_w0, v_rwkv_w2,
             v_rwkv_a0, v_rwkv_a2, v_rwkv_g2, v_rwkv_k_k, v_rwkv_k_a, v_rwkv_r_k, v_rwkv_ln_w, v_rwkv_ln_b,
             v_attn_q_norm, v_attn_k_norm, v_w_proj_rwkv, v_w_proj_attn, v_w_out, v_ffn2_norm, v_ffn2_w_in, v_ffn2_w_out)
    Wl = {n: t[0] for n, t in zip(names, w_all)}
    Ml = {n: t[0] for n, t in zip(names, m_all)}
    Vl = {n: t[0] for n, t in zip(names, v_all)}
    big = [n for n, _ in _BIG]

    my_c = lax.axis_index("c").astype(jnp.int32).reshape(1)
    my_chip = (2 * lax.axis_index("x") + lax.axis_index("y")).astype(jnp.int32).reshape(1)

    gathered = _all_gather(_pack_big(Wl, _GROUPS_EARLY).astype(bf16))
    W, P = _build_W_early(_unpack_gathered(gathered, Wl, _GROUPS_EARLY)), _build_P(Wl)
    late = {"pack": _pack_big(Wl, _GROUPS_LATE).astype(bf16),
            "weights": lambda g: _build_W_late(_unpack_gathered(g, Wl, _GROUPS_LATE)),
            "reduce_start": lambda G: _reduce_start(G, Wl, _GROUPS_LATE, my_c, "late")}

    loss_local, dx0, G, Gs = _local_step(x[0], loss_target[0], W, P, late)

    part_late, recv_late = G["late"]
    part_early = _reduce_start(G, Wl, _GROUPS_EARLY, my_c, "early")
    g_big = _unpack_big(_grad_sum(part_early, _grad_cross(part_early), my_chip, 256, "early"), Wl, _GROUPS_EARLY)
    g_big.update(_unpack_big(_grad_sum(part_late, recv_late, my_chip, 256, "late"), Wl, _GROUPS_LATE))

    mu_g = Gs["mu_rkv"], Gs["mu_lora"]
    o1, o2 = LORA_PAD[0], LORA_PAD[0] + LORA_PAD[1]
    g_small_local = {
        "ffn1_norm": Gs["ffn1_norm"], "mix_norm": Gs["mix_norm"], "b_gate": Gs["b_gate"],
        "rwkv_mu": jnp.concatenate([mu_g[0], mu_g[1][:, :64], mu_g[1][:, o1:o1 + 64], mu_g[1][:, o2:o2 + 160]], axis=1),
        "rwkv_w0": Gs["w0"], "rwkv_a0": Gs["a0"], "rwkv_k_k": Gs["k_k"], "rwkv_k_a": Gs["k_a"], "rwkv_r_k": Gs["r_k"],
        "rwkv_ln_w": Gs["ln_w"], "rwkv_ln_b": Gs["ln_b"], "attn_q_norm": Gs["q_norm"], "attn_k_norm": Gs["k_norm"],
        "ffn2_norm": Gs["ffn2_norm"]}
    gs_pack = _small_all_reduce(_pack_small(g_small_local))

    out_g, out_d, out_m, out_v = dict(g_big), {}, {}, {}
    for n in big:
        cols = Wl[n].shape[1]
        out_d[n], out_m[n], out_v[n] = _rowwise(f"adamw_{n}", _f_adamw, [Wl[n], g_big[n], Ml[n], Vl[n]], [],
                                                 [(cols, f32)] * 3)
    ds_pack, ms_pack, vs_pack = _rowwise(
        "adamw_small", _f_adamw, [_pack_small(Wl), gs_pack, _pack_small(Ml), _pack_small(Vl)], [], [(D, f32)] * 3)
    for out, pack in ((out_g, gs_pack), (out_d, ds_pack), (out_m, ms_pack), (out_v, vs_pack)):
        out.update(_unpack_small(pack, Wl))

    loss = lax.psum(loss_local, ("x", "y", "c"))
    return (loss, dx0[None], *[out_g[n][None] for n in names], *[out_d[n][None] for n in names],
            *[out_m[n][None] for n in names], *[out_v[n][None] for n in names])
```

```python
import functools

import jax
import jax.numpy as jnp
from jax import lax
from jax.experimental import pallas as pl
from jax.experimental.pallas import tpu as pltpu

f32 = jnp.float32
bf16 = jnp.bfloat16
MESH = pl.DeviceIdType.MESH

N_DEV = 8
D = 1024
D_FF = 2816
HEAD = 64
RW_HEADS = 16
ATTN_PAIRS = ((128, 1), (512, 4), (2048, 16))
ATTN_BLK = 128
HEADS_PER_GROUP = 4
ATTN_W = 768
LORA_PAD = (128, 128, 256)
LORA_W = (64, 64, 160)
GN_EPS = 64e-5
RMS_EPS = 1e-6
NEG_INF = -1e30
WKV_T = 64
WKV_SUB = 2
GRAD_WIRE = bf16
PACK_BLOCK = 128
SMALL_ROWS = 24
VMEM_LIMIT = 56 * 1024 * 1024

ADAM_LR, ADAM_B1, ADAM_B2, ADAM_EPS, ADAM_WD, ADAM_STEP = 0.001, 0.9, 0.999, 1e-08, 0.01, 10


def _cparams(sem):
    return pltpu.CompilerParams(dimension_semantics=sem, vmem_limit_bytes=VMEM_LIMIT)


def _pick(n, cands):
    for c in cands:
        if n % c == 0:
            return c
    return n


HALO = 8


def _rowwise(name, fn, rows, params, outs, accs=(), tm=256, halos=(), carries=(), reverse=False):
    S = rows[0].shape[0]
    tm = min(tm, S)
    while S % tm:
        tm -= 8
    nb = S // tm
    n_in = len(rows) + len(params) + len(halos)
    n_out = len(outs)
    n_acc = len(accs)
    n_car = len(carries)

    def blk_of(i):
        return nb - 1 - i if reverse else i

    def body(*refs):
        step = pl.program_id(0)
        carry_refs = refs[n_in + n_out + n_acc:]
        if n_car:
            @pl.when(step == 0)
            def _():
                for c_ref in carry_refs:
                    c_ref[...] = jnp.zeros(c_ref.shape, f32)
        args = [r[...] for r in refs[:n_in]] + [c[...] for c in carry_refs]
        res = fn(*args, blk=blk_of(step)) if (halos or carries) else fn(*args)
        if not isinstance(res, (tuple, list)):
            res = (res,)
        out_refs = refs[n_in:n_in + n_out + n_acc]
        for j in range(n_out):
            out_refs[j][...] = res[j].astype(out_refs[j].dtype)
        if n_acc:
            @pl.when(step == 0)
            def _():
                for j in range(n_acc):
                    out_refs[n_out + j][...] = jnp.zeros(out_refs[n_out + j].shape, f32)
            for j in range(n_acc):
                out_refs[n_out + j][...] += res[n_out + j]
        for j in range(n_car):
            carry_refs[j][...] = res[n_out + n_acc + j]

    in_specs = [pl.BlockSpec((tm, a.shape[1]), lambda i: (blk_of(i), 0)) for a in rows]
    in_specs += [pl.BlockSpec(p.shape, lambda i, nd=p.ndim: (0,) * nd) for p in params]
    in_specs += [pl.BlockSpec((HALO, rows[h].shape[1]), lambda i: (jnp.maximum(blk_of(i) * (tm // HALO) - 1, 0), 0))
                 for h in halos]
    out_specs = [pl.BlockSpec((tm, w), lambda i: (blk_of(i), 0)) for w, _ in outs]
    out_specs += [pl.BlockSpec(s, lambda i: (0, 0)) for s in accs]
    out_shape = [jax.ShapeDtypeStruct((S, w), dt) for w, dt in outs]
    out_shape += [jax.ShapeDtypeStruct(s, f32) for s in accs]
    res = pl.pallas_call(
        body, name=name, grid=(nb,), in_specs=in_specs, out_specs=out_specs, out_shape=out_shape,
        scratch_shapes=[pltpu.VMEM(s, f32) for s in carries],
        compiler_params=_cparams(("arbitrary",)),
    )(*rows, *params, *[rows[h] for h in halos])
    return res


MM_VMEM_BUDGET = 40 * 1024 * 1024
MM_STEP_US = 0.35
MM_FLOPS_PER_US = 9.0e8
MM_HBM_BYTES_PER_US = 3.0e6


def _tile_options(n, cap):
    opts = [d for d in range(128, min(n, cap) + 1, 128) if n % d == 0]
    return opts or [n]


def _mm_tiles(M, N, K, sa, sb, so):
    best, best_cost = None, None
    for tm in _tile_options(M, 2048):
        for tn in _tile_options(N, 2048):
            for tk in _tile_options(K, 4096):
                vmem = 2 * (tm * tk * sa + tk * tn * sb) + 2 * tm * tn * so + (tm * tn * 4 if tk < K else 0)
                if vmem > MM_VMEM_BUDGET:
                    continue
                steps = (M // tm) * (N // tn) * (K // tk)
                traffic = M * K * sa * (N // tn) + K * N * sb * (M // tm) + M * N * so
                cost = (max(2.0 * M * N * K / MM_FLOPS_PER_US, traffic / MM_HBM_BYTES_PER_US) + steps * MM_STEP_US
                        + (tm * tk * sa + tk * tn * sb) / MM_HBM_BYTES_PER_US)
                if best_cost is None or cost < best_cost:
                    best, best_cost = (tm, tn, tk), cost
    return best


def _mm(name, a, b, mode, out_dtype=f32, scale=None, gather=None, cross=None):
    if mode == "nn":
        (M, K), (_, N) = a.shape, b.shape
    elif mode == "nt":
        (M, K), (N, _) = a.shape, b.shape
    else:
        (K, M), (_, N) = a.shape, b.shape
    tm, tn, tk = _mm_tiles(M, N, K, a.dtype.itemsize, b.dtype.itemsize, jnp.dtype(out_dtype).itemsize)
    nk = K // tk
    if mode == "nn":
        a_spec = pl.BlockSpec((tm, tk), lambda i, j, k: (i, k))
        b_spec = pl.BlockSpec((tk, tn), lambda i, j, k: (k, j))
        dims = (((1,), (0,)), ((), ()))
    elif mode == "nt":
        a_spec = pl.BlockSpec((tm, tk), lambda i, j, k: (i, k))
        b_spec = pl.BlockSpec((tn, tk), lambda i, j, k: (j, k))
        dims = (((1,), (1,)), ((), ()))
    else:
        a_spec = pl.BlockSpec((tk, tm), lambda i, j, k: (k, i))
        b_spec = pl.BlockSpec((tk, tn), lambda i, j, k: (k, j))
        dims = (((0,), (0,)), ((), ()))

    def finish(acc):
        return acc if scale is None else acc * scale

    hosted = gather if gather is not None else cross
    grid = (M // tm, N // tn, nk)
    steps = grid[0] * grid[1] * grid[2]

    def body(a_ref, b_ref, *rest):
        if hosted is None:
            o_ref, *scratch = rest
        else:
            src_ref, o_ref, dst_ref, *scratch = rest
            n_sem = len(GATHER_SEMS if gather is not None else CROSS_SEMS)
            sems, scratch = scratch[len(scratch) - n_sem:], scratch[:len(scratch) - n_sem]
            step = (pl.program_id(0) * grid[1] + pl.program_id(1)) * grid[2] + pl.program_id(2)
            if gather is not None:
                start, forward, done = _gather_phases(src_ref, dst_ref, *sems)
                pl.when(step == steps // 2)(forward)
            else:
                start, done = _cross_phases(src_ref, dst_ref, *sems)
            pl.when(step == 0)(start)
        part = lax.dot_general(a_ref[...].astype(bf16), b_ref[...].astype(bf16), dims,
                               preferred_element_type=f32)
        if nk == 1:
            o_ref[...] = finish(part).astype(o_ref.dtype)
        else:
            acc_ref = scratch[0]
            k = pl.program_id(2)

            @pl.when(k == 0)
            def _():
                acc_ref[...] = part

            @pl.when(k > 0)
            def _():
                acc_ref[...] += part

            @pl.when(k == nk - 1)
            def _():
                o_ref[...] = finish(acc_ref[...]).astype(o_ref.dtype)
        if hosted is not None:
            pl.when(step == steps - 1)(done)

    hbm = pl.BlockSpec(memory_space=pl.ANY)
    out_specs = [pl.BlockSpec((tm, tn), lambda i, j, k: (i, j))]
    out_shape = [jax.ShapeDtypeStruct((M, N), out_dtype)]
    scratch_shapes = [] if nk == 1 else [pltpu.VMEM((tm, tn), f32)]
    if gather is not None:
        out_specs.append(hbm)
        out_shape.append(jax.ShapeDtypeStruct((N_DEV,) + gather.shape, gather.dtype))
        scratch_shapes = scratch_shapes + GATHER_SEMS
    elif cross is not None:
        out_specs.append(hbm)
        out_shape.append(jax.ShapeDtypeStruct((3,) + cross.shape[1:], cross.dtype))
        scratch_shapes = scratch_shapes + CROSS_SEMS
    res = pl.pallas_call(
        body, name=name, grid=grid, in_specs=[a_spec, b_spec] + [hbm] * (hosted is not None),
        out_specs=out_specs, out_shape=out_shape, scratch_shapes=scratch_shapes,
        compiler_params=_cparams(("arbitrary",) * 3 if hosted is not None else ("parallel", "parallel", "arbitrary")),
    )(a, b, *([hosted] if hosted is not None else []))
    return res[0] if hosted is None else res


def _sp(x):
    hi = x.astype(bf16)
    return hi, (x - hi.astype(f32)).astype(bf16)


def _cat(parts):
    return tuple(jnp.concatenate(p, axis=1) for p in zip(*parts))


def _bmm(eq, a, b):
    (ah, al), (bh, bl) = a, b
    dot = functools.partial(jnp.einsum, eq, preferred_element_type=f32)
    return dot(ah, bh) + (dot(ah, bl) + dot(al, bh))


def _tri_dot(eq, tri, x):
    h1 = x.astype(bf16)
    r1 = x - h1.astype(f32)
    h2 = r1.astype(bf16)
    h3 = (r1 - h2.astype(f32)).astype(bf16)
    dot = functools.partial(jnp.einsum, eq, preferred_element_type=f32)
    return dot(tri, h1) + (dot(tri, h2) + dot(tri, h3))


def _tri_masks(T):
    ti = lax.broadcasted_iota(jnp.int32, (T, T), 0)
    si = lax.broadcasted_iota(jnp.int32, (T, T), 1)
    return ti >= si, ti > si


def _wkv_prep(r, lw, k, kkr, a):
    H, T, _ = r.shape
    low_i, low_s = _tri_masks(T)
    nrm = jnp.sqrt(jnp.sum(kkr * kkr, axis=-1, keepdims=True))
    den = jnp.maximum(nrm, 1e-12)
    kk = kkr / den
    tri = jnp.broadcast_to(low_i.astype(bf16)[None], (H, T, T))
    cl = _tri_dot("hts,hsn->htn", tri, lw)
    c = jnp.exp(cl)
    cprev = jnp.exp(cl - lw)
    cinv = jnp.exp(-cl)
    bt, kt = _sp(kk * a * cinv), _sp(k * cinv)
    L = _cat([_sp(r * c), _sp(-kk * cprev)])
    Mb = _bmm("htn,hsn->hts", L, bt)
    Mk = _bmm("htn,hsn->hts", L, kt)
    A_rb = jnp.where(low_i[None], Mb[:, :T], 0.0)
    A_ab = jnp.where(low_s[None], Mb[:, T:], 0.0)
    Mk = jnp.concatenate([jnp.where(low_i[None], Mk[:, :T], 0.0), jnp.where(low_s[None], Mk[:, T:], 0.0)], axis=1)
    return dict(kk=kk, den=den, nrm=nrm, c=c, cprev=cprev, cinv=cinv, L=L, kt=kt, bt=bt,
                A_ab=A_ab, A_rb=A_rb, Mk=Mk, cT=c[:, T - 1:T, :])


def _tri_inverse(A):
    T = A.shape[-1]
    eye = (lax.broadcasted_iota(jnp.int32, (T, T), 0) == lax.broadcasted_iota(jnp.int32, (T, T), 1)).astype(f32)
    inv = eye[None] + A
    X = A
    n = 1
    while 2 * n < T:
        Xs = _sp(X)
        X = _bmm("hts,hsu->htu", Xs, Xs)
        inv = inv + _bmm("hts,hsu->htu", _sp(inv), _sp(X))
        n *= 2
    return inv


def _wkv_chunk_fwd(S0, r, lw, k, v, kkr, a):
    T = r.shape[1]
    q = _wkv_prep(r, lw, k, kkr, a)
    inv = _tri_inverse(q["A_ab"])
    vs = _sp(v)
    P = _bmm("htk,hvk->htv", q["L"], _sp(S0)) + _bmm("hts,hsv->htv", _sp(q["Mk"]), vs)
    U = _bmm("hts,hsv->htv", _sp(inv), _sp(P[:, T:]))
    Us = _sp(U)
    Y = P[:, :T] + _bmm("hts,hsv->htv", _sp(q["A_rb"]), Us)
    S1 = (S0 + _bmm("htv,htk->hvk", _cat([Us, vs]), _cat([q["bt"], q["kt"]]))) * q["cT"]
    return Y, U, inv, S1


def _wkv_chunk_bwd(S0, Hin, Q, r, lw, k, v, kkr, a, U, inv, dY):
    H, T, _ = r.shape
    low_i, low_s = _tri_masks(T)
    q = _wkv_prep(r, lw, k, kkr, a)
    L, kt, bt = q["L"], q["kt"], q["bt"]
    R = _cat([bt, kt])
    Hh = Hin * q["cT"]
    Hs, S0s, dYs, vs, Us = _sp(Hh), _sp(S0), _sp(dY), _sp(v), _sp(U)
    RH = _bmm("htk,hvk->htv", R, Hs)
    Z = _bmm("hst,hsv->htv", _sp(inv), _sp(RH[:, :T] + _bmm("hst,hsv->htv", _sp(q["A_rb"]), dYs)))
    DZ = _cat([dYs, _sp(Z)])
    both = jnp.concatenate([jnp.broadcast_to(low_i[None], (1, T, T)), jnp.broadcast_to(low_s[None], (1, T, T))], axis=1)
    NU = _sp(jnp.where(both, _bmm("htv,hsv->hts", DZ, Us), 0.0))
    NV = _sp(jnp.where(both, _bmm("htv,hsv->hts", DZ, vs), 0.0))
    ra = _bmm("htv,hvk->htk", DZ, S0s) + _bmm("hts,hsk->htk", NU, bt) + _bmm("hts,hsk->htk", NV, kt)
    dr = ra[:, :T] * q["c"]
    da = ra[:, T:] * q["cprev"]
    dv = RH[:, T:] + _bmm("hst,hsv->htv", _sp(q["Mk"]), DZ)
    VH = _bmm("htv,hvk->htk", _cat([vs, Us]), Hs)
    dk = (VH[:, :T] + _bmm("hst,hsk->htk", NV, L)) * q["cinv"]
    db = (VH[:, T:] + _bmm("hst,hsk->htk", NU, L)) * q["cinv"]
    H0 = Hh + _bmm("htv,htk->hvk", DZ, L)
    kk = q["kk"]
    e = r * dr - kk * a * db - k * dk
    f = -kk * da
    tri_i = jnp.broadcast_to(low_i.astype(bf16)[None], (H, T, T))
    tri_s = jnp.broadcast_to(low_s.astype(bf16)[None], (H, T, T))
    dlw = _tri_dot("hst,hsn->htn", tri_i, e) + _tri_dot("hst,hsn->htn", tri_s, f) + Q
    Qn = Q + jnp.sum(e + f, axis=1, keepdims=True)
    dkk = db * a - da
    dasig = db * kk
    proj = jnp.sum(dkk * kk, axis=-1, keepdims=True)
    dkkr = jnp.where(q["nrm"] > 1e-12, dkk - kk * proj, dkk) / q["den"]
    return dr, dlw, dk, dv, dkkr, dasig, H0, Qn


def _heads(ref, rows=slice(None)):
    return jnp.stack([ref[rows, h * HEAD:(h + 1) * HEAD] for h in range(RW_HEADS)], axis=0)


def _put_heads(ref, val, rows=slice(None)):
    for h in range(RW_HEADS):
        ref[rows, h * HEAD:(h + 1) * HEAD] = val[h]


def _wkv_fwd(r, lw, k, v, kkr, a, g, r_k, ln_w, ln_b, late_pack=None):
    S = r.shape[0]
    H, N, T = RW_HEADS, HEAD, WKV_T
    TS = T * WKV_SUB
    nc = S // TS
    hosting = late_pack is not None

    def body(r_ref, lw_ref, k_ref, v_ref, kkr_ref, a_ref, g_ref, rk_ref, lnw_ref, lnb_ref, *rest):
        if hosting:
            pack_ref, y_ref, yg_ref, wkv_ref, u_ref, inv_ref, s0_ref, gathered_ref, state, *sems = rest
            start, forward, finish = _gather_phases(pack_ref, gathered_ref, *sems)
            pl.when(pl.program_id(0) == 0)(start)
            pl.when(pl.program_id(0) == nc // 2)(forward)
        else:
            y_ref, yg_ref, wkv_ref, u_ref, inv_ref, s0_ref, state = rest

        @pl.when(pl.program_id(0) == 0)
        def _():
            state[...] = jnp.zeros(state.shape, f32)

        S0 = state[...]
        for c in range(WKV_SUB):
            rows = slice(c * T, (c + 1) * T)
            s0_ref[c] = S0
            rr, kk2, vv = _heads(r_ref, rows), _heads(k_ref, rows), _heads(v_ref, rows)
            Y, U, inv, S0 = _wkv_chunk_fwd(S0, rr, _heads(lw_ref, rows), kk2, vv, _heads(kkr_ref, rows),
                                           _heads(a_ref, rows))
            wkv_ref[:, rows, :] = Y
            u_ref[:, rows, :] = U
            inv_ref[:, rows, :] = inv
            mean = jnp.mean(Y, axis=-1, keepdims=True)
            var = jnp.mean(jnp.square(Y - mean), axis=-1, keepdims=True)
            yn = (Y - mean) * lax.rsqrt(var + GN_EPS)
            bonus = jnp.sum(rr * kk2 * rk_ref[...], axis=-1, keepdims=True) * vv
            _put_heads(y_ref, yn * lnw_ref[...] + lnb_ref[...] + bonus, rows)
        state[...] = S0
        yg_ref[...] = (y_ref[...] * g_ref[...]).astype(yg_ref.dtype)
        if hosting:
            pl.when(pl.program_id(0) == nc - 1)(finish)

    tok = pl.BlockSpec((TS, H * N), lambda i: (i, 0))
    blk = pl.BlockSpec((H, TS, N), lambda i: (0, i, 0))
    par = pl.BlockSpec((H, 1, N), lambda i: (0, 0, 0))
    hbm = pl.BlockSpec(memory_space=pl.ANY)
    seq = jax.ShapeDtypeStruct((H, S, N), f32)
    out_specs = [tok, tok, blk, blk, blk, pl.BlockSpec((WKV_SUB, H, N, N), lambda i: (i, 0, 0, 0))]
    out_shape = [jax.ShapeDtypeStruct((S, H * N), f32), jax.ShapeDtypeStruct((S, H * N), bf16), seq, seq, seq,
                 jax.ShapeDtypeStruct((S // T, H, N, N), f32)]
    if hosting:
        out_specs.append(hbm)
        out_shape.append(jax.ShapeDtypeStruct((N_DEV,) + late_pack.shape, late_pack.dtype))
    return pl.pallas_call(
        body, name="wkv_fwd", grid=(nc,), in_specs=[tok] * 7 + [par] * 3 + [hbm] * hosting,
        out_specs=out_specs, out_shape=out_shape,
        scratch_shapes=[pltpu.VMEM((H, N, N), f32)] + (GATHER_SEMS if hosting else []),
        compiler_params=_cparams(("arbitrary",)),
    )(r, lw, k, v, kkr, a, g, r_k, ln_w, ln_b, *([late_pack] if hosting else []))


def _wkv_bwd(dy, g, r, lw, k, v, kkr, a, wkv, U, inv, S0s, r_k, ln_w, ln_b, late_part=None):
    S = r.shape[0]
    H, N, T = RW_HEADS, HEAD, WKV_T
    TS = T * WKV_SUB
    nc = S // TS
    hosting = late_part is not None

    def body(dy_ref, g_ref, r_ref, lw_ref, k_ref, v_ref, kkr_ref, a_ref, wkv_ref, u_ref, inv_ref, s0_ref,
             rk_ref, lnw_ref, lnb_ref, *rest):
        if hosting:
            (part_ref, dr_ref, dlw_ref, dk_ref, dv_ref, dkkr_ref, da_ref, drk_ref, dlnw_ref, dlnb_ref, recv_ref,
             hst, qst, *sems) = rest
            start, finish = _cross_phases(part_ref, recv_ref, *sems)
            pl.when(pl.program_id(0) == 0)(start)
        else:
            dr_ref, dlw_ref, dk_ref, dv_ref, dkkr_ref, da_ref, drk_ref, dlnw_ref, dlnb_ref, hst, qst = rest

        @pl.when(pl.program_id(0) == 0)
        def _():
            hst[...] = jnp.zeros(hst.shape, f32)
            qst[...] = jnp.zeros(qst.shape, f32)
            drk_ref[...] = jnp.zeros(drk_ref.shape, f32)
            dlnw_ref[...] = jnp.zeros(dlnw_ref.shape, f32)
            dlnb_ref[...] = jnp.zeros(dlnb_ref.shape, f32)

        dyg = dy_ref[...] * g_ref[...]
        rk = rk_ref[...]
        Hst, Qst = hst[...], qst[...]
        for c in reversed(range(WKV_SUB)):
            rows = slice(c * T, (c + 1) * T)
            dya = _heads(dyg, rows)
            rr, kk2, vv, Y = _heads(r_ref, rows), _heads(k_ref, rows), _heads(v_ref, rows), wkv_ref[:, rows, :]
            s = jnp.sum(rr * kk2 * rk, axis=-1, keepdims=True)
            ds = jnp.sum(dya * vv, axis=-1, keepdims=True)
            mean = jnp.mean(Y, axis=-1, keepdims=True)
            var = jnp.mean(jnp.square(Y - mean), axis=-1, keepdims=True)
            rstd = lax.rsqrt(var + GN_EPS)
            yn = (Y - mean) * rstd
            dyn = dya * lnw_ref[...]
            dY = rstd * (dyn - jnp.mean(dyn, axis=-1, keepdims=True) - yn * jnp.mean(dyn * yn, axis=-1, keepdims=True))
            drk_ref[...] += jnp.sum(ds * rr * kk2, axis=1, keepdims=True)
            dlnw_ref[...] += jnp.sum(dya * yn, axis=1, keepdims=True)
            dlnb_ref[...] += jnp.sum(dya, axis=1, keepdims=True)
            dr, dlw, dk, dv, dkkr, dasig, Hst, Qst = _wkv_chunk_bwd(
                s0_ref[c], Hst, Qst, rr, _heads(lw_ref, rows), kk2, vv, _heads(kkr_ref, rows), _heads(a_ref, rows),
                u_ref[:, rows, :], inv_ref[:, rows, :], dY)
            _put_heads(dr_ref, dr + ds * kk2 * rk, rows)
            _put_heads(dlw_ref, dlw, rows)
            _put_heads(dk_ref, dk + ds * rr * rk, rows)
            _put_heads(dv_ref, dv + dya * s, rows)
            _put_heads(dkkr_ref, dkkr, rows)
            _put_heads(da_ref, dasig, rows)
        hst[...] = Hst
        qst[...] = Qst
        if hosting:
            pl.when(pl.program_id(0) == nc - 1)(finish)

    tok = pl.BlockSpec((TS, H * N), lambda i: (nc - 1 - i, 0))
    blk = pl.BlockSpec((H, TS, N), lambda i: (0, nc - 1 - i, 0))
    par = pl.BlockSpec((H, 1, N), lambda i: (0, 0, 0))
    hbm = pl.BlockSpec(memory_space=pl.ANY)
    seq = jax.ShapeDtypeStruct((S, H * N), f32)
    pout = jax.ShapeDtypeStruct((H, 1, N), f32)
    out_specs, out_shape = [tok] * 6 + [par] * 3, [seq] * 6 + [pout] * 3
    if hosting:
        out_specs.append(hbm)
        out_shape.append(jax.ShapeDtypeStruct((3,) + late_part.shape[1:], late_part.dtype))
    return pl.pallas_call(
        body, name="wkv_bwd", grid=(nc,),
        in_specs=([tok] * 8 + [blk] * 3 + [pl.BlockSpec((WKV_SUB, H, N, N), lambda i: (nc - 1 - i, 0, 0, 0))]
                  + [par] * 3 + [hbm] * hosting),
        out_specs=out_specs, out_shape=out_shape,
        scratch_shapes=[pltpu.VMEM((H, N, N), f32), pltpu.VMEM((H, 1, N), f32)] + (CROSS_SEMS if hosting else []),
        compiler_params=_cparams(("arbitrary",)),
    )(dy, g, r, lw, k, v, kkr, a, wkv, U, inv, S0s, r_k, ln_w, ln_b, *([late_part] if hosting else []))


ATTN_TT = 2048


def _attn_rows(d, i, j):
    return pl.ds(ATTN_BLK * d * i + j, ATTN_BLK, stride=d) if d > 1 else pl.ds(ATTN_BLK * i, ATTN_BLK)


def _attn_take(ref, d, nsub):
    return jnp.stack([ref[_attn_rows(d, i, j), :] for i in range(nsub) for j in range(d)], axis=0)


def _attn_put(ref, val, d):
    for i in range(val.shape[0] // d):
        for j in range(d):
            ref[_attn_rows(d, i, j), :] = val[i * d + j]


def _attn_prev(cur, before, d):
    return before if cur.shape[0] == d else jnp.concatenate([before, cur[:cur.shape[0] - d]], axis=0)


def _attn_specs(gi, d, nt, reverse):
    per_tile = ATTN_TT // (ATTN_BLK * d)

    def tile(n):
        return nt - 1 - n if reverse else n

    def col(kind):
        return lambda hp, n: (tile(n), kind * (ATTN_W // 128) + 2 * gi + hp)

    def col_before(kind):
        return lambda hp, n: (jnp.maximum(tile(n) * per_tile - 1, 0), kind * (ATTN_W // 128) + 2 * gi + hp)

    cur = [pl.BlockSpec((ATTN_TT, 128), col(kind)) for kind in range(3)]
    before = [pl.BlockSpec((ATTN_BLK * d, 128), col_before(kind)) for kind in (1, 2)]
    own = pl.BlockSpec((ATTN_TT, 128), lambda hp, n: (tile(n), hp))
    return cur, before, own, tile


def _attn_norm(x, gain, scale):
    rs = lax.rsqrt(jnp.mean(x * x, axis=-1, keepdims=True) + RMS_EPS)
    return x * rs * (gain * scale), rs


def _attn_scores(qn, kn_c, kn_p, first):
    s_c = jnp.einsum("gqe,gke->gqk", qn.astype(bf16), kn_c.astype(bf16), preferred_element_type=f32)
    s_p = jnp.einsum("gqe,gke->gqk", qn.astype(bf16), kn_p.astype(bf16), preferred_element_type=f32)
    qi = lax.broadcasted_iota(jnp.int32, (1, ATTN_BLK, ATTN_BLK), 1)
    ki = lax.broadcasted_iota(jnp.int32, (1, ATTN_BLK, ATTN_BLK), 2)
    s_c = jnp.where(qi >= ki, s_c, NEG_INF)
    s_p = jnp.where(jnp.logical_and(ki >= qi, jnp.logical_not(first)), s_p, NEG_INF)
    return s_c, s_p


def _attn_fwd(pqkv, qg, kg, gi, S):
    d = ATTN_PAIRS[gi][1]
    nt = S // ATTN_TT
    nsub = ATTN_TT // (ATTN_BLK * d)
    nd = nsub * d

    def body(q_ref, k_ref, v_ref, kb_ref, vb_ref, qg_ref, kg_ref, o_ref, lse_ref):
        Q, K, V = _attn_take(q_ref, d, nsub), _attn_take(k_ref, d, nsub), _attn_take(v_ref, d, nsub)
        KB, VB = _attn_take(kb_ref, d, 1), _attn_take(vb_ref, d, 1)
        first = jnp.logical_and(lax.broadcasted_iota(jnp.int32, (nd, 1, 1), 0) < d, pl.program_id(1) == 0)
        outs, lses = [], []
        for h in range(2):
            sl = slice(h * HEAD, (h + 1) * HEAD)
            kc, vc = K[:, :, sl], V[:, :, sl]
            kp, vp = _attn_prev(kc, KB[:, :, sl], d), _attn_prev(vc, VB[:, :, sl], d)
            qn, _ = _attn_norm(Q[:, :, sl], qg_ref[...], HEAD ** -0.5)
            kn_c, _ = _attn_norm(kc, kg_ref[...], 1.0)
            kn_p, _ = _attn_norm(kp, kg_ref[...], 1.0)
            s_c, s_p = _attn_scores(qn, kn_c, kn_p, first)
            m = jnp.maximum(jnp.max(s_c, axis=-1, keepdims=True), jnp.max(s_p, axis=-1, keepdims=True))
            p_c = jnp.exp(s_c - m)
            p_p = jnp.exp(s_p - m)
            den = jnp.sum(p_c, axis=-1, keepdims=True) + jnp.sum(p_p, axis=-1, keepdims=True)
            inv = 1.0 / den
            o = jnp.einsum("gqk,gke->gqe", (p_c * inv).astype(bf16), vc.astype(bf16), preferred_element_type=f32)
            o += jnp.einsum("gqk,gke->gqe", (p_p * inv).astype(bf16), vp.astype(bf16), preferred_element_type=f32)
            outs.append(o)
            lses.append(jnp.broadcast_to(m + jnp.log(den), o.shape))
        _attn_put(o_ref, jnp.concatenate(outs, axis=-1), d)
        _attn_put(lse_ref, jnp.concatenate(lses, axis=-1), d)

    cur, before, own, _ = _attn_specs(gi, d, nt, False)
    par = pl.BlockSpec((1, HEAD), lambda hp, n: (0, 0))
    shp = jax.ShapeDtypeStruct((S, 2 * 128), f32)
    return pl.pallas_call(
        body, name=f"attn_fwd{gi}", grid=(2, nt), in_specs=cur + before + [par] * 2, out_specs=[own, own],
        out_shape=[shp, shp], compiler_params=_cparams(("arbitrary", "arbitrary")),
    )(pqkv, pqkv, pqkv, pqkv, pqkv, qg, kg)


def _attn_bwd(pqkv, o, lse, do, dlse, qg, kg, gi, S):
    d = ATTN_PAIRS[gi][1]
    nt = S // ATTN_TT
    nsub = ATTN_TT // (ATTN_BLK * d)
    nd = nsub * d

    def norm_bwd(dxn, x, rs, gain, scale):
        xh = x * rs
        dxh = dxn * (gain * scale)
        dx = rs * (dxh - xh * jnp.mean(dxh * xh, axis=-1, keepdims=True))
        dgain = jnp.sum(jnp.sum(dxn * xh * scale, axis=1), axis=0, keepdims=True)
        return dx, dgain

    def to_before(part, carried):
        return carried if nsub == 1 else jnp.concatenate([part[d:], carried], axis=0)

    def body(q_ref, k_ref, v_ref, kb_ref, vb_ref, o_ref, lse_ref, do_ref, dlse_ref, qg_ref, kg_ref,
             dq_ref, dk_ref, dv_ref, dqg_ref, dkg_ref, carry_k, carry_v):
        step = pl.program_id(1)

        @pl.when(jnp.logical_and(pl.program_id(0) == 0, step == 0))
        def _():
            dqg_ref[...] = jnp.zeros(dqg_ref.shape, f32)
            dkg_ref[...] = jnp.zeros(dkg_ref.shape, f32)

        @pl.when(step == 0)
        def _():
            carry_k[...] = jnp.zeros(carry_k.shape, f32)
            carry_v[...] = jnp.zeros(carry_v.shape, f32)

        Q, K, V = _attn_take(q_ref, d, nsub), _attn_take(k_ref, d, nsub), _attn_take(v_ref, d, nsub)
        KB, VB = _attn_take(kb_ref, d, 1), _attn_take(vb_ref, d, 1)
        O, LSE = _attn_take(o_ref, d, nsub), _attn_take(lse_ref, d, nsub)
        DO, DLSE = _attn_take(do_ref, d, nsub), _attn_take(dlse_ref, d, nsub)
        first = jnp.logical_and(lax.broadcasted_iota(jnp.int32, (nd, 1, 1), 0) < d, step == nt - 1)
        qg, kg = qg_ref[...], kg_ref[...]
        dqs, dks, dvs = [], [], []
        for h in range(2):
            sl = slice(h * HEAD, (h + 1) * HEAD)
            qx, kx, vc = Q[:, :, sl], K[:, :, sl], V[:, :, sl]
            kpx, vp = _attn_prev(kx, KB[:, :, sl], d), _attn_prev(vc, VB[:, :, sl], d)
            qn, rq = _attn_norm(qx, qg, HEAD ** -0.5)
            kn_c, rk_c = _attn_norm(kx, kg, 1.0)
            kn_p, _ = _attn_norm(kpx, kg, 1.0)
            s_c, s_p = _attn_scores(qn, kn_c, kn_p, first)
            lse = LSE[:, :, h * HEAD:h * HEAD + 1]
            p_c = jnp.exp(s_c - lse)
            p_p = jnp.exp(s_p - lse)
            dO = DO[:, :, sl]
            dOb = dO.astype(bf16)
            dp_c = jnp.einsum("gqe,gke->gqk", dOb, vc.astype(bf16), preferred_element_type=f32)
            dp_p = jnp.einsum("gqe,gke->gqk", dOb, vp.astype(bf16), preferred_element_type=f32)
            corr = DLSE[:, :, h * HEAD:h * HEAD + 1] - jnp.sum(dO * O[:, :, sl], axis=-1, keepdims=True)
            ds_c = (p_c * (dp_c + corr)).astype(bf16)
            ds_p = (p_p * (dp_p + corr)).astype(bf16)
            qnb = qn.astype(bf16)
            dqn = (jnp.einsum("gqk,gke->gqe", ds_c, kn_c.astype(bf16), preferred_element_type=f32)
                   + jnp.einsum("gqk,gke->gqe", ds_p, kn_p.astype(bf16), preferred_element_type=f32))
            dkn_p = jnp.einsum("gqk,gqe->gke", ds_p, qnb, preferred_element_type=f32)
            dv_p = jnp.einsum("gqk,gqe->gke", p_p.astype(bf16), dOb, preferred_element_type=f32)
            dkn = jnp.einsum("gqk,gqe->gke", ds_c, qnb, preferred_element_type=f32) + to_before(dkn_p, carry_k[h])
            dv = (jnp.einsum("gqk,gqe->gke", p_c.astype(bf16), dOb, preferred_element_type=f32)
                  + to_before(dv_p, carry_v[h]))
            carry_k[h] = dkn_p[:d]
            carry_v[h] = dv_p[:d]
            dq, dqg = norm_bwd(dqn, qx, rq, qg, HEAD ** -0.5)
            dk, dkg = norm_bwd(dkn, kx, rk_c, kg, 1.0)
            dqg_ref[...] += dqg
            dkg_ref[...] += dkg
            dqs.append(dq)
            dks.append(dk)
            dvs.append(dv)
        _attn_put(dq_ref, jnp.concatenate(dqs, axis=-1), d)
        _attn_put(dk_ref, jnp.concatenate(dks, axis=-1), d)
        _attn_put(dv_ref, jnp.concatenate(dvs, axis=-1), d)

    cur, before, own, _ = _attn_specs(gi, d, nt, True)
    par = pl.BlockSpec((1, HEAD), lambda hp, n: (0, 0))
    shp = jax.ShapeDtypeStruct((S, 2 * 128), f32)
    pshp = jax.ShapeDtypeStruct((1, HEAD), f32)
    return pl.pallas_call(
        body, name=f"attn_bwd{gi}", grid=(2, nt), in_specs=cur + before + [own] * 4 + [par] * 2,
        out_specs=[own] * 3 + [par] * 2, out_shape=[shp] * 3 + [pshp] * 2,
        scratch_shapes=[pltpu.VMEM((2, d, ATTN_BLK, HEAD), f32)] * 2,
        compiler_params=_cparams(("arbitrary", "arbitrary")),
    )(pqkv, pqkv, pqkv, pqkv, pqkv, o, lse, do, dlse, qg, kg)


def _rms(x, g):
    rs = lax.rsqrt(jnp.mean(x * x, axis=-1, keepdims=True) + RMS_EPS)
    return x * rs * g


def _f_rms(x, g):
    return _rms(x, g)


def _f_resid_rms(coef, x, f, g):
    xn = x + coef * f
    return xn, _rms(xn, g)


def _f_swiglu(u):
    gate, up = u[:, :D_FF], u[:, D_FF:]
    return gate * jax.nn.sigmoid(gate) * up


def _f_swiglu_bwd(dact, u):
    gate, up = u[:, :D_FF], u[:, D_FF:]
    sg = jax.nn.sigmoid(gate)
    silu = gate * sg
    dact = 0.5 * dact
    return jnp.concatenate([dact * up * (sg * (1.0 + gate * (1.0 - sg))), dact * silu], axis=1)


def _f_rms_bwd(n_parts, *args):
    dns = args[:n_parts]
    x, dres, g = args[n_parts:]
    dn = dns[0]
    for t in dns[1:]:
        dn = dn + t
    rs = lax.rsqrt(jnp.mean(x * x, axis=-1, keepdims=True) + RMS_EPS)
    xh = x * rs
    dxh = dn * g
    dx = dres + rs * (dxh - xh * jnp.mean(dxh * xh, axis=-1, keepdims=True))
    return dx, dx, jnp.sum(dn * xh, axis=0, keepdims=True)


def _f_loss(x, f, tgt):
    y = x + 0.5 * f
    diff = y - tgt
    part = 0.5 * jnp.sum(jnp.mean(diff * diff, axis=-1, keepdims=True), axis=0, keepdims=True)
    dy = diff * (1.0 / D)
    return dy, dy, jnp.broadcast_to(part, (1, 128))


def _dotb(a, b, dims):
    return lax.dot_general(a.astype(bf16), b.astype(bf16), dims, preferred_element_type=f32)


_NN = (((1,), (0,)), ((), ()))
_NT = (((1,), (1,)), ((), ()))
_TN = (((0,), (0,)), ((), ()))


def _rwkv_pre_core(prkv, prkv_prev, plora, plora_prev, mu_rkv, mu_lora, w0, w2p, a0, a2p, g2p, k_k, k_a):
    xs = prkv + (prkv_prev - prkv) * mu_rkv
    xl = plora + (plora_prev - plora) * mu_lora
    r, k, v = xs[:, :D], xs[:, D:2 * D], xs[:, 2 * D:]
    wd, ad, gd = xl[:, :128], xl[:, 128:256], xl[:, 256:]
    tw = jnp.tanh(wd)
    zw = w0 + _dotb(tw, w2p, _NN)
    sp = jnp.maximum(-zw, 0.0) + jnp.log(1.0 + jnp.exp(-jnp.abs(zw)))
    lw = -jnp.exp(-sp - 0.5)
    a = jax.nn.sigmoid(a0 + _dotb(ad, a2p, _NN))
    sg = jax.nn.sigmoid(gd)
    return dict(r=r, k=k, v=v, tw=tw, zw=zw, lw=lw, a=a, sg=sg, ad=ad)


def _rows_down(x, halo, blk):
    before = jnp.where(blk > 0, halo[HALO - 1:HALO, :], 0.0)
    row = lax.broadcasted_iota(jnp.int32, (x.shape[0], 1), 0)
    return jnp.where(row == 0, before, pltpu.roll(x, 1, 0))


def _rows_up(x, after):
    n = x.shape[0]
    row = lax.broadcasted_iota(jnp.int32, (n, 1), 0)
    return jnp.where(row == n - 1, after, pltpu.roll(x, n - 1, 0))


def _f_rwkv_pre(prkv, plora, mu_rkv, mu_lora, w0, w2p, a0, a2p, g2p, k_k, k_a, halo_rkv, halo_lora, blk):
    c = _rwkv_pre_core(prkv, _rows_down(prkv, halo_rkv, blk), plora, _rows_down(plora, halo_lora, blk),
                       mu_rkv, mu_lora, w0, w2p, a0, a2p, g2p, k_k, k_a)
    g = _dotb(c["sg"], g2p, _NN)
    k, a = c["k"], c["a"]
    return c["r"], c["lw"], k * (1.0 + (a - 1.0) * k_a), c["v"], k * k_k, a, g


def _f_rwkv_pre_bwd(prkv, plora, dr, dlw, dk2, dv, dkkr, da, dya, yap,
                    mu_rkv, mu_lora, w0, w2p, a0, a2p, g2p, k_k, k_a, halo_rkv, halo_lora, next_rkv, next_lora, blk):
    prkv_prev, plora_prev = _rows_down(prkv, halo_rkv, blk), _rows_down(plora, halo_lora, blk)
    c = _rwkv_pre_core(prkv, prkv_prev, plora, plora_prev, mu_rkv, mu_lora, w0, w2p, a0, a2p, g2p, k_k, k_a)
    k, a, sg, tw, zw, lw = c["k"], c["a"], c["sg"], c["tw"], c["zw"], c["lw"]
    dg = dya * yap
    dsg = _dotb(dg, g2p, _NT)
    dgd = dsg * sg * (1.0 - sg)
    dg2p = _dotb(sg, dg, _TN)
    dk = dk2 * (1.0 + (a - 1.0) * k_a) + dkkr * k_k
    da_t = da + dk2 * k * k_a
    dk_a = jnp.sum(dk2 * k * (a - 1.0), axis=0, keepdims=True)
    dk_k = jnp.sum(dkkr * k, axis=0, keepdims=True)
    dza = da_t * a * (1.0 - a)
    da0 = jnp.sum(dza, axis=0, keepdims=True)
    dad = _dotb(dza, a2p, _NT)
    da2p = _dotb(c["ad"], dza, _TN)
    dzw = dlw * lw * jax.nn.sigmoid(-zw)
    dw0 = jnp.sum(dzw, axis=0, keepdims=True)
    dtw = _dotb(dzw, w2p, _NT)
    dw2p = _dotb(tw, dzw, _TN)
    dwd = dtw * (1.0 - tw * tw)
    dxs = jnp.concatenate([dr, dk, dv], axis=1)
    dxl = jnp.concatenate([dwd, dad, dgd], axis=1)
    dmu_rkv = jnp.sum(dxs * (prkv_prev - prkv), axis=0, keepdims=True)
    dmu_lora = jnp.sum(dxl * (plora_prev - plora), axis=0, keepdims=True)
    to_next_rkv, to_next_lora = dxs * mu_rkv, dxl * mu_lora
    return (dxs * (1.0 - mu_rkv) + _rows_up(to_next_rkv, next_rkv), dxl * (1.0 - mu_lora) + _rows_up(to_next_lora, next_lora),
            dmu_rkv, dmu_lora, dw0, da0, dk_k, dk_a, dw2p, da2p, dg2p, to_next_rkv[0:1], to_next_lora[0:1])


def _group_alpha(l0, l1, l2):
    m = jnp.maximum(jnp.maximum(l0, l1), l2)
    e0, e1, e2 = jnp.exp(l0 - m), jnp.exp(l1 - m), jnp.exp(l2 - m)
    inv = 1.0 / (e0 + e1 + e2)
    return jnp.concatenate([e0 * inv, e1 * inv, e2 * inv], axis=1)


def _f_combine(o0, o1, o2, l0, l1, l2):
    return jnp.concatenate([o0, o1, o2], axis=1) * _group_alpha(l0, l1, l2)


def _f_combine_bwd(dyb, o0, o1, o2, l0, l1, l2, bd):
    alpha = _group_alpha(l0, l1, l2)
    hi, lo = _sp(dyb * jnp.concatenate([o0, o1, o2], axis=1))
    ones = bd.astype(bf16)
    e = jnp.dot(hi, ones, preferred_element_type=f32) + jnp.dot(lo, ones, preferred_element_type=f32)
    ae = alpha * e
    tot = ae[:, :256] + ae[:, 256:512] + ae[:, 512:]
    do = dyb * alpha
    dl = ae - alpha * jnp.concatenate([tot, tot, tot], axis=1)
    return do[:, :256], do[:, 256:512], do[:, 512:], dl[:, :256], dl[:, 256:512], dl[:, 512:]


def _f_merge(pgate, ta, tb, b_gate):
    gate = jax.nn.sigmoid(pgate + b_gate)
    return gate[:, :D] * ta + gate[:, D:] * tb


def _f_merge_bwd(dm, pgate, ta, tb, b_gate):
    gate = jax.nn.sigmoid(pgate + b_gate)
    ga, gb = gate[:, :D], gate[:, D:]
    dpg = jnp.concatenate([dm * ta * ga * (1.0 - ga), dm * tb * gb * (1.0 - gb)], axis=1)
    return dm * ga, dm * gb, dpg, jnp.sum(dpg, axis=0, keepdims=True)


def _f_adamw(w, g, m, v):
    m2 = ADAM_B1 * m + (1.0 - ADAM_B1) * g
    v2 = ADAM_B2 * v + (1.0 - ADAM_B2) * jnp.square(g)
    m_hat = m2 / (1.0 - ADAM_B1 ** ADAM_STEP)
    v_hat = v2 / (1.0 - ADAM_B2 ** ADAM_STEP)
    delta = -ADAM_LR * (m_hat / (jnp.sqrt(v_hat) + ADAM_EPS) + ADAM_WD * w)
    return delta, m2, v2


def _ffn_bwd(tag, dxo, dxo_b, x_in, n, u, act, g, WiT, Wo, cross=None):
    dact = _mm(f"{tag}_dact", dxo_b, Wo, "nt")
    dWo = _mm(f"{tag}_dwo", act, dxo_b, "tn", out_dtype=GRAD_WIRE, scale=0.5)
    (du,) = _rowwise(f"{tag}_dswiglu", _f_swiglu_bwd, [dact, u], [], [(2 * D_FF, bf16)], tm=128)
    if cross is None:
        dn, recv = _mm(f"{tag}_dn", du, WiT, "nn"), None
    else:
        dn, recv = _mm(f"{tag}_dn", du, WiT, "nn", cross=cross)
    dWiT = _mm(f"{tag}_dwi", du, n, "tn", out_dtype=GRAD_WIRE)
    dx, dx_b, dg = _rowwise(f"{tag}_drms", functools.partial(_f_rms_bwd, 1), [dn, x_in, dxo], [g],
                            [(D, f32), (D, bf16)], [(1, D)])
    return dx, dx_b, dg, dWiT, dWo, recv


def _local_step(x0, tgt, W, P, hooks=None):
    S = x0.shape[0]
    (n1,) = _rowwise("f1_rms", _f_rms, [x0], [P["ffn1_norm"]], [(D, bf16)])
    hooks = hooks or {}
    if "gather_mid" in hooks:
        pack, weights = hooks["gather_mid"]
        u1, gathered = _mm("f1_up", n1, W["f1_iT"], "nt", gather=pack)
        W = {**W, **weights(gathered)}
    else:
        u1 = _mm("f1_up", n1, W["f1_iT"], "nt")
    (act1,) = _rowwise("f1_swiglu", _f_swiglu, [u1], [], [(D_FF, bf16)])
    f1 = _mm("f1_down", act1, W["f1_o"], "nn")
    x1, h = _rowwise("mix_rms", functools.partial(_f_resid_rms, 0.5), [x0, f1], [P["mix_norm"]],
                     [(D, f32), (D, bf16)])
    prkv = _mm("p_rkv", h, W["in_rkvT"], "nt")
    plora = _mm("p_lora", h, W["in_loraT"], "nt")
    pqkv = _mm("p_qkv", h, W["in_qkvT"], "nt")
    pgate = _mm("p_gate", h, W["in_gateT"], "nt")
    pre_params = [P["mu_rkv"], P["mu_lora"], P["w0"], W["w2p"], P["a0"], W["a2p"], W["g2p"], P["k_k"], P["k_a"]]
    r, lw, k2, v, kkr, a, g = _rowwise("rwkv_pre", _f_rwkv_pre, [prkv, plora], pre_params, [(D, f32)] * 7, tm=128,
                                       halos=(0, 1))
    hm = [r, lw, k2, v, kkr, a]
    hp = [P["r_k"].reshape(RW_HEADS, 1, HEAD), P["ln_w"].reshape(RW_HEADS, 1, HEAD), P["ln_b"].reshape(RW_HEADS, 1, HEAD)]
    if "gather_late" in hooks:
        pack, weights = hooks["gather_late"]
        yap, ya, wkv_h, U_h, inv_h, S0s, gathered = _wkv_fwd(*hm, g, *hp, late_pack=pack)
        W = {**W, **weights(gathered)}
    else:
        yap, ya, wkv_h, U_h, inv_h, S0s = _wkv_fwd(*hm, g, *hp)
    ta = _mm("proj_a", ya, W["pr"], "nn")
    n_grp = len(ATTN_PAIRS)
    attn = [_attn_fwd(pqkv, P["q_norm"], P["k_norm"], gi, S) for gi in range(n_grp)]
    o_g, lse_g = [t[0] for t in attn], [t[1] for t in attn]
    (yb,) = _rowwise("attn_combine", _f_combine, [*o_g, *lse_g], [], [(ATTN_W, bf16)])
    tb = _mm("proj_b", yb, W["paT"], "nt")
    (merged,) = _rowwise("merge", _f_merge, [pgate, ta, tb], [P["b_gate"]], [(D, bf16)])
    mo = _mm("mix_out", merged, W["out"], "nn")
    x2, n2 = _rowwise("f2_rms", functools.partial(_f_resid_rms, 1.0), [x1, mo], [P["ffn2_norm"]],
                      [(D, f32), (D, bf16)])
    u2 = _mm("f2_up", n2, W["f2_iT"], "nt")
    (act2,) = _rowwise("f2_swiglu", _f_swiglu, [u2], [], [(D_FF, bf16)])
    f2 = _mm("f2_down", act2, W["f2_o"], "nn")
    dx3, dx3_b, loss = _rowwise("loss", _f_loss, [x2, f2, tgt], [], [(D, f32), (D, bf16)], [(1, 128)])
    G, Gs = {}, {}
    dx2, dx2_b, Gs["ffn2_norm"], G["f2_iT"], G["f2_o"], _ = _ffn_bwd("f2", dx3, dx3_b, x2, n2, u2, act2,
                                                                    P["ffn2_norm"], W["f2_iT"], W["f2_o"])
    dmerged = _mm("d_merged", dx2_b, W["out"], "nt")
    G["out"] = _mm("dw_out", merged, dx2_b, "tn", out_dtype=GRAD_WIRE)
    dta, dtb, dpgate, Gs["b_gate"] = _rowwise("merge_bwd", _f_merge_bwd, [dmerged, pgate, ta, tb], [P["b_gate"]],
                                              [(D, bf16), (D, bf16), (2 * D, bf16)], [(1, 2 * D)])
    dya = _mm("d_ya", dta, W["pr"], "nt")
    G["pr"] = _mm("dw_pr", ya, dta, "tn", out_dtype=GRAD_WIRE)
    dyb = _mm("d_yb", dtb, W["paT"], "nn")
    G["paT"] = _mm("dw_pa", dtb, yb, "tn", out_dtype=GRAD_WIRE)
    if "reduce_late" in hooks:
        part_late = hooks["reduce_late"](G)
        hg = _wkv_bwd(dya, g, *hm, wkv_h, U_h, inv_h, S0s, *hp, late_part=part_late)
        G["late"] = (part_late, hg[9])
    else:
        hg = _wkv_bwd(dya, g, *hm, wkv_h, U_h, inv_h, S0s, *hp)
    dr, dlw, dk2, dv, dkkr, da = hg[:6]
    Gs["r_k"], Gs["ln_w"], Gs["ln_b"] = (t.reshape(1, D) for t in hg[6:9])
    lp = sum(LORA_PAD)
    (dprkv, dplora, Gs["mu_rkv"], Gs["mu_lora"], Gs["w0"], Gs["a0"], Gs["k_k"], Gs["k_a"],
     dw2p, da2p, dg2p) = _rowwise(
        "rwkv_pre_bwd", _f_rwkv_pre_bwd,
        [prkv, plora, dr, dlw, dk2, dv, dkkr, da, dya, yap], pre_params,
        [(3 * D, bf16), (lp, bf16)],
        [(1, 3 * D), (1, lp), (1, D), (1, D), (1, D), (1, D), (LORA_PAD[0], D), (LORA_PAD[1], D), (LORA_PAD[2], D)],
        tm=128, halos=(0, 1), carries=((1, 3 * D), (1, lp)), reverse=True)
    G["w2T"], G["a2T"], G["g2T"] = dw2p[:LORA_W[0]].T, da2p[:LORA_W[1]].T, dg2p[:LORA_W[2]].T
    bd = (jnp.arange(ATTN_W)[:, None] // HEAD == jnp.arange(ATTN_W)[None, :] // HEAD).astype(f32)
    dol = _rowwise("attn_combine_bwd", _f_combine_bwd, [dyb, *o_g, *lse_g], [bd], [(ATTN_W // n_grp, f32)] * (2 * n_grp))
    dattn = [_attn_bwd(pqkv, o_g[gi], lse_g[gi], dol[gi], dol[n_grp + gi], P["q_norm"], P["k_norm"], gi, S)
             for gi in range(n_grp)]
    Gs["q_norm"] = dattn[0][3] + dattn[1][3] + dattn[2][3]
    Gs["k_norm"] = dattn[0][4] + dattn[1][4] + dattn[2][4]
    dpqkv = jnp.concatenate([dattn[gi][kind] for kind in range(3) for gi in range(n_grp)], axis=1).astype(bf16)
    dh = [_mm("dh_rkv", dprkv, W["in_rkvT"], "nn"), _mm("dh_lora", dplora, W["in_loraT"], "nn"),
          _mm("dh_qkv", dpqkv, W["in_qkvT"], "nn"), _mm("dh_gate", dpgate, W["in_gateT"], "nn")]
    dW_rkv = _mm("dw_rkv", dprkv, h, "tn", out_dtype=GRAD_WIRE)
    dW_lora = _mm("dw_lora", dplora, h, "tn", out_dtype=GRAD_WIRE)
    dW_qkv = _mm("dw_qkv", dpqkv, h, "tn", out_dtype=GRAD_WIRE)
    dW_gate = _mm("dw_gate", dpgate, h, "tn", out_dtype=GRAD_WIRE)
    o1, o2 = LORA_PAD[0], LORA_PAD[0] + LORA_PAD[1]
    G["inT"] = jnp.concatenate([dW_rkv, dW_lora[:LORA_W[0]], dW_lora[o1:o1 + LORA_W[1]], dW_lora[o2:o2 + LORA_W[2]],
                                dW_qkv, dW_gate], axis=0)
    dx1, dx1_b, Gs["mix_norm"] = _rowwise("mix_drms", functools.partial(_f_rms_bwd, 4), [*dh, x1, dx2],
                                          [P["mix_norm"]], [(D, f32), (D, bf16)], [(1, D)])
    part_mid = hooks["reduce_mid"](G) if "reduce_mid" in hooks else None
    dx0, _, Gs["ffn1_norm"], G["f1_iT"], G["f1_o"], recv_mid = _ffn_bwd(
        "f1", dx1, dx1_b, x0, n1, u1, act1, P["ffn1_norm"], W["f1_iT"], W["f1_o"], cross=part_mid)
    G["mid"] = (part_mid, recv_mid)
    return loss[0, 0], dx0, G, Gs


def _peer(k):
    x, y, c = lax.axis_index("x"), lax.axis_index("y"), lax.axis_index("c")
    px = 1 - x if k & 4 else x
    py = 1 - y if k & 2 else y
    pc = 1 - c if k & 1 else c
    return (px, py, pc), 4 * px + 2 * py + pc


def _gather_phases(x_ref, out_ref, send_sems, recv_sems, local_sem):
    x, y, c = lax.axis_index("x"), lax.axis_index("y"), lax.axis_index("c")
    me, sibling = (x, y, c), (x, y, 1 - c)
    chips = [(1 - x, y), (x, 1 - y), (1 - x, 1 - y)]

    def slot(px, py, pc):
        return out_ref.at[4 * px + 2 * py + pc]

    def copy(k, block, to, src=None):
        return pltpu.make_async_remote_copy(
            src_ref=slot(*block) if src is None else src, dst_ref=slot(*block), send_sem=send_sems.at[k],
            recv_sem=recv_sems.at[k], device_id=to, device_id_type=MESH)

    def mine():
        return pltpu.make_async_copy(x_ref, slot(*me), local_sem)

    def first():
        return [copy(0, me, sibling, src=x_ref)] + [copy(1 + j, me, (*chip, c), src=x_ref) for j, chip in enumerate(chips)]

    def passed():
        return [copy(4 + j, (*chip, c), sibling) for j, chip in enumerate(chips)]

    def start():
        mine().start()
        for cp in first():
            cp.start()

    def forward():
        for j, (chip, cp) in enumerate(zip(chips, passed())):
            copy(1 + j, (*chip, c), me).wait_recv()
            cp.start()

    def finish():
        copy(0, sibling, me).wait_recv()
        for j, chip in enumerate(chips):
            copy(4 + j, (*chip, 1 - c), me).wait_recv()
        for cp in first() + passed():
            cp.wait_send()
        mine().wait()

    return start, forward, finish


GATHER_SEMS = [pltpu.SemaphoreType.DMA((N_DEV - 1,)), pltpu.SemaphoreType.DMA((N_DEV - 1,)), pltpu.SemaphoreType.DMA(())]


def _all_gather(pack):
    R, C = pack.shape

    def body(x_ref, out_ref, send_sems, recv_sems, local_sem):
        for phase in _gather_phases(x_ref, out_ref, send_sems, recv_sems, local_sem):
            phase()

    return pl.pallas_call(
        body, name="weight_all_gather", out_shape=jax.ShapeDtypeStruct((N_DEV, R, C), pack.dtype),
        in_specs=[pl.BlockSpec(memory_space=pl.ANY)], out_specs=pl.BlockSpec(memory_space=pl.ANY),
        scratch_shapes=GATHER_SEMS,
    )(pack)


def _cross_phases(p_ref, out_ref, send_sems, recv_sems):
    x, y, c = lax.axis_index("x"), lax.axis_index("y"), lax.axis_index("c")

    def copies():
        out = []
        for j, (fx, fy) in enumerate([(1, 0), (0, 1), (1, 1)]):
            px = 1 - x if fx else x
            py = 1 - y if fy else y
            out.append(pltpu.make_async_remote_copy(src_ref=p_ref.at[2 * px + py], dst_ref=out_ref.at[j],
                                                    send_sem=send_sems.at[j], recv_sem=recv_sems.at[j],
                                                    device_id=(px, py, c), device_id_type=MESH))
        return out

    def start():
        for cp in copies():
            cp.start()

    def finish():
        for cp in copies():
            cp.wait()

    return start, finish


CROSS_SEMS = [pltpu.SemaphoreType.DMA((3,)), pltpu.SemaphoreType.DMA((3,))]


N_CHIP = 4


def _grad_pair(pieces, tag):
    n = len(pieces)
    C = pieces[0].shape[2]
    rows = [p.shape[1] for p in pieces]
    offs = [sum(rows[:i]) for i in range(n)]
    R = sum(rows)

    def body(*refs):
        g_refs, (other_ref, send_sems, recv_sems) = refs[:n], refs[n:]
        x, y, c = lax.axis_index("x"), lax.axis_index("y"), lax.axis_index("c")
        copies = []
        for i, g_ref in enumerate(g_refs):
            for k in range(N_CHIP):
                cp = pltpu.make_async_remote_copy(
                    src_ref=g_ref.at[4 * (k // 2) + 2 * (k % 2) + 1 - c], dst_ref=other_ref.at[k, pl.ds(offs[i], rows[i])],
                    send_sem=send_sems.at[i * N_CHIP + k], recv_sem=recv_sems.at[i * N_CHIP + k],
                    device_id=(x, y, 1 - c), device_id_type=MESH)
                cp.start()
                copies.append(cp)
        for cp in copies:
            cp.wait()

    return pl.pallas_call(
        body, name=f"grad_pair_{tag}", out_shape=jax.ShapeDtypeStruct((N_CHIP, R, C), pieces[0].dtype),
        in_specs=[pl.BlockSpec(memory_space=pl.ANY)] * n, out_specs=pl.BlockSpec(memory_space=pl.ANY),
        scratch_shapes=[pltpu.SemaphoreType.DMA((n * N_CHIP,))] * 2,
    )(*pieces)


def _pair_add(pieces, other, c, tag):
    n = len(pieces)
    C = pieces[0].shape[2]
    nblk = [p.shape[1] // PACK_BLOCK for p in pieces]
    lo = [sum(nblk[:i]) for i in range(n)]
    R = sum(nblk) * PACK_BLOCK

    def body(c_ref, *refs):
        g_refs, o_ref, out_ref = refs[:n], refs[n], refs[n + 1]
        rb = pl.program_id(1)
        for i in range(n):
            @pl.when(jnp.logical_and(rb >= lo[i], rb < lo[i] + nblk[i]))
            def _(g_ref=g_refs[i]):
                out_ref[...] = (g_ref[...].astype(f32) + o_ref[...].astype(f32)).astype(out_ref.dtype)

    def piece_spec(i):
        return pl.BlockSpec((1, None, PACK_BLOCK, C),
                            lambda k, rb, c_ref: (k, c_ref[0], jnp.clip(rb - lo[i], 0, nblk[i] - 1), 0))

    blk = pl.BlockSpec((1, PACK_BLOCK, C), lambda k, rb, c_ref: (k, rb, 0))
    return pl.pallas_call(
        body, name=f"pair_add_{tag}",
        grid_spec=pltpu.PrefetchScalarGridSpec(
            num_scalar_prefetch=1, grid=(N_CHIP, R // PACK_BLOCK),
            in_specs=[piece_spec(i) for i in range(n)] + [blk], out_specs=blk),
        out_shape=jax.ShapeDtypeStruct((N_CHIP, R, C), other.dtype),
        compiler_params=_cparams(("arbitrary", "arbitrary")),
    )(c, *[p.reshape(N_CHIP, 2, p.shape[1], C) for p in pieces], other)


def _grad_cross(part):
    _, R, C = part.shape

    def body(p_ref, out_ref, send_sems, recv_sems):
        for phase in _cross_phases(p_ref, out_ref, send_sems, recv_sems):
            phase()

    return pl.pallas_call(
        body, name="grad_cross", out_shape=jax.ShapeDtypeStruct((3, R, C), part.dtype),
        in_specs=[pl.BlockSpec(memory_space=pl.ANY)], out_specs=pl.BlockSpec(memory_space=pl.ANY),
        scratch_shapes=CROSS_SEMS,
    )(part)


def _grad_sum(part, recv, my_chip, tr, tag):
    _, R, C = part.shape

    def body(chip_ref, p_ref, r_ref, o_ref):
        acc = p_ref[0].astype(f32)
        for j in range(3):
            acc = acc + r_ref[j].astype(f32)
        o_ref[...] = acc

    return pl.pallas_call(
        body, name=f"grad_sum_{tag}",
        grid_spec=pltpu.PrefetchScalarGridSpec(
            num_scalar_prefetch=1, grid=(R // tr,),
            in_specs=[pl.BlockSpec((1, tr, C), lambda i, chip_ref: (chip_ref[0], i, 0)),
                      pl.BlockSpec((3, tr, C), lambda i, chip_ref: (0, i, 0))],
            out_specs=pl.BlockSpec((tr, C), lambda i, chip_ref: (i, 0))),
        out_shape=jax.ShapeDtypeStruct((R, C), f32),
        compiler_params=_cparams(("arbitrary",)),
    )(my_chip, part, recv)


def _small_all_reduce(small):
    R, C = small.shape

    def body(x_ref, o_ref, buf, send_sems, recv_sems):
        _, me = _peer(0)
        buf[me] = x_ref[...]
        sends = []
        for k in range(1, N_DEV):
            dev, _ = _peer(k)
            cp = pltpu.make_async_remote_copy(src_ref=x_ref, dst_ref=buf.at[me], send_sem=send_sems.at[k - 1],
                                              recv_sem=recv_sems.at[k - 1], device_id=dev, device_id_type=MESH)
            cp.start()
            sends.append(cp)
        for k in range(1, N_DEV):
            dev, idx = _peer(k)
            pltpu.make_async_remote_copy(src_ref=x_ref, dst_ref=buf.at[idx], send_sem=send_sems.at[k - 1],
                                         recv_sem=recv_sems.at[k - 1], device_id=dev, device_id_type=MESH).wait_recv()
        for cp in sends:
            cp.wait_send()
        acc = buf[0]
        for i in range(1, N_DEV):
            acc = acc + buf[i]
        o_ref[...] = acc

    return pl.pallas_call(
        body, name="small_all_reduce", out_shape=jax.ShapeDtypeStruct((R, C), f32),
        in_specs=[pl.BlockSpec(memory_space=pltpu.VMEM)], out_specs=pl.BlockSpec(memory_space=pltpu.VMEM),
        scratch_shapes=[pltpu.VMEM((N_DEV, R, C), f32), pltpu.SemaphoreType.DMA((N_DEV - 1,)),
                        pltpu.SemaphoreType.DMA((N_DEV - 1,))],
    )(small)


_LORA = (("rwkv_w2", True), ("rwkv_a2", True), ("rwkv_g2", True))
_GROUPS_FIRST = ((("ffn1_w_in", True),),)
_GROUPS_MID = ((("ffn1_w_out", False),), (("w_in", True),), _LORA)
_GROUPS_LATE = ((("w_proj_rwkv", False),), (("w_proj_attn", True),), (("w_out", False),),
                (("ffn2_w_in", True),), (("ffn2_w_out", False),))
_GRADS_MID = ((("w_in", True),), _LORA)
_GRADS_LAST = ((("ffn1_w_in", True),), (("ffn1_w_out", False),))
_BIG = tuple(item for group in _GROUPS_FIRST + _GROUPS_MID + _GROUPS_LATE for item in group)
_SMALL = ("ffn1_norm", "mix_norm", "b_gate", "rwkv_mu", "rwkv_w0", "rwkv_a0", "rwkv_k_k", "rwkv_k_a", "rwkv_r_k",
          "rwkv_ln_w", "rwkv_ln_b", "attn_q_norm", "attn_k_norm", "ffn2_norm")


def _pack_layout(like, groups):
    items, spans, off = {}, [], 0
    for group in groups:
        start = off
        for name, _ in group:
            shp = like[name].shape
            n = shp[0] * shp[1] // D
            items[name] = (off, n)
            off += n
        off = -(-off // PACK_BLOCK) * PACK_BLOCK
        spans.append((start, off - start))
    return items, spans, off


def _pack_big(shards, groups):
    items, _, rows = _pack_layout(shards, groups)
    parts, at = [], 0
    for group in groups:
        for name, tr in group:
            off, n = items[name]
            t = shards[name]
            if off > at:
                parts.append(jnp.zeros((off - at, D), t.dtype))
            parts.append((t.T if tr else t).reshape(n, D))
            at = off + n
    if rows > at:
        parts.append(jnp.zeros((rows - at, D), parts[0].dtype))
    return jnp.concatenate(parts, axis=0)


def _unpack_big(pack, like, groups):
    items, _, _ = _pack_layout(like, groups)
    out = {}
    for group in groups:
        for name, tr in group:
            off, n = items[name]
            shp = like[name].shape
            t = pack[off:off + n]
            out[name] = t.reshape(shp[1], shp[0]).T if tr else t.reshape(shp)
    return out


def _unpack_gathered(gathered, like, groups):
    items, _, _ = _pack_layout(like, groups)
    full = {}
    for group in groups:
        for name, tr in group:
            shp = like[name].shape
            off, rows = items[name]
            r_loc, c_loc = (shp[1], shp[0]) if tr else shp
            full[name] = gathered[:, off:off + rows].reshape(N_DEV * r_loc, c_loc)
    return full


def _grad_pieces(g_full, like, groups):
    items, spans, _ = _pack_layout(like, groups)
    pieces = []
    for group, (_, rows_pad) in zip(groups, spans):
        parts = [g_full[n].astype(GRAD_WIRE).reshape(N_DEV, items[n][1], D) for n, _ in group]
        piece = parts[0] if len(parts) == 1 else jnp.concatenate(parts, axis=1)
        if rows_pad > piece.shape[1]:
            piece = jnp.pad(piece, ((0, 0), (0, rows_pad - piece.shape[1]), (0, 0)))
        pieces.append(piece)
    return pieces


def _small_rows(name, t):
    flat = t.reshape(-1)
    pad = (-flat.shape[0]) % D
    return jnp.pad(flat, (0, pad)).reshape(-1, D)


def _pack_small(vals):
    parts = [_small_rows(n, vals[n]) for n in _SMALL]
    used = sum(p.shape[0] for p in parts)
    parts.append(jnp.zeros((SMALL_ROWS - used, D), f32))
    return jnp.concatenate(parts, axis=0)


def _unpack_small(pack, like):
    out, off = {}, 0
    for n in _SMALL:
        size = like[n].size
        rows = -(-size // D)
        out[n] = pack[off:off + rows].reshape(-1)[:size].reshape(like[n].shape)
        off += rows
    return out


def _build_W_mid(full):
    inT = full["w_in"]
    z64, z96 = jnp.zeros((64, D), inT.dtype), jnp.zeros((96, D), inT.dtype)
    return {
        "f1_o": full["ffn1_w_out"],
        "in_rkvT": inT[:3 * D],
        "in_loraT": jnp.concatenate([inT[3072:3136], z64, inT[3136:3200], z64, inT[3200:3360], z96], axis=0),
        "in_qkvT": inT[3360:3360 + 3 * ATTN_W], "in_gateT": inT[3360 + 3 * ATTN_W:],
        "w2p": jnp.concatenate([full["rwkv_w2"].T, z64], axis=0),
        "a2p": jnp.concatenate([full["rwkv_a2"].T, z64], axis=0),
        "g2p": jnp.concatenate([full["rwkv_g2"].T, z96], axis=0),
    }


def _build_W_late(full):
    return {"pr": full["w_proj_rwkv"], "paT": full["w_proj_attn"], "out": full["w_out"],
            "f2_iT": full["ffn2_w_in"], "f2_o": full["ffn2_w_out"]}


def _build_W_first(full):
    return {"f1_iT": full["ffn1_w_in"]}


def _build_W(full):
    return {**_build_W_first(full), **_build_W_mid(full), **_build_W_late(full)}


_G_NAMES = {"ffn1_w_in": "f1_iT", "ffn1_w_out": "f1_o", "w_in": "inT", "rwkv_w2": "w2T", "rwkv_a2": "a2T",
            "rwkv_g2": "g2T", "w_proj_rwkv": "pr", "w_proj_attn": "paT", "w_out": "out", "ffn2_w_in": "f2_iT",
            "ffn2_w_out": "f2_o"}


def _reduce_start(G, like, groups, my_c, tag):
    pieces = _grad_pieces({n: G[_G_NAMES[n]] for group in groups for n, _ in group}, like, groups)
    return _pair_add(pieces, _grad_pair(pieces, tag), my_c, tag)


def _build_P(Wl):
    mu = Wl["rwkv_mu"]
    z64f, z96f = jnp.zeros((1, 64), f32), jnp.zeros((1, 96), f32)
    return {
        "ffn1_norm": Wl["ffn1_norm"][None], "mix_norm": Wl["mix_norm"][None], "ffn2_norm": Wl["ffn2_norm"][None],
        "b_gate": Wl["b_gate"][None], "mu_rkv": mu[None, :3 * D],
        "mu_lora": jnp.concatenate([mu[None, 3072:3136], z64f, mu[None, 3136:3200], z64f, mu[None, 3200:3360], z96f], axis=1),
        "w0": Wl["rwkv_w0"][None], "a0": Wl["rwkv_a0"][None], "k_k": Wl["rwkv_k_k"][None], "k_a": Wl["rwkv_k_a"][None],
        "r_k": Wl["rwkv_r_k"].reshape(1, D), "ln_w": Wl["rwkv_ln_w"][None], "ln_b": Wl["rwkv_ln_b"][None],
        "q_norm": Wl["attn_q_norm"][None], "k_norm": Wl["attn_k_norm"][None],
    }


def kernel(x, ffn1_norm, ffn1_w_in, ffn1_w_out, mix_norm, w_in, b_gate, rwkv_mu, rwkv_w0, rwkv_w2, rwkv_a0, rwkv_a2, rwkv_g2, rwkv_k_k, rwkv_k_a, rwkv_r_k, rwkv_ln_w, rwkv_ln_b, attn_q_norm, attn_k_norm, w_proj_rwkv, w_proj_attn, w_out, ffn2_norm, ffn2_w_in, ffn2_w_out, loss_target, m_ffn1_norm, m_ffn1_w_in, m_ffn1_w_out, m_mix_norm, m_w_in, m_b_gate, m_rwkv_mu, m_rwkv_w0, m_rwkv_w2, m_rwkv_a0, m_rwkv_a2, m_rwkv_g2, m_rwkv_k_k, m_rwkv_k_a, m_rwkv_r_k, m_rwkv_ln_w, m_rwkv_ln_b, m_attn_q_norm, m_attn_k_norm, m_w_proj_rwkv, m_w_proj_attn, m_w_out, m_ffn2_norm, m_ffn2_w_in, m_ffn2_w_out, v_ffn1_norm, v_ffn1_w_in, v_ffn1_w_out, v_mix_norm, v_w_in, v_b_gate, v_rwkv_mu, v_rwkv_w0, v_rwkv_w2, v_rwkv_a0, v_rwkv_a2, v_rwkv_g2, v_rwkv_k_k, v_rwkv_k_a, v_rwkv_r_k, v_rwkv_ln_w, v_rwkv_ln_b, v_attn_q_norm, v_attn_k_norm, v_w_proj_rwkv, v_w_proj_attn, v_w_out, v_ffn2_norm, v_ffn2_w_in, v_ffn2_w_out):
    names = ("ffn1_norm", "ffn1_w_in", "ffn1_w_out", "mix_norm", "w_in", "b_gate", "rwkv_mu", "rwkv_w0", "rwkv_w2",
             "rwkv_a0", "rwkv_a2", "rwkv_g2", "rwkv_k_k", "rwkv_k_a", "rwkv_r_k", "rwkv_ln_w", "rwkv_ln_b",
             "attn_q_norm", "attn_k_norm", "w_proj_rwkv", "w_proj_attn", "w_out", "ffn2_norm", "ffn2_w_in", "ffn2_w_out")
    w_all = (ffn1_norm, ffn1_w_in, ffn1_w_out, mix_norm, w_in, b_gate, rwkv_mu, rwkv_w0, rwkv_w2, rwkv_a0, rwkv_a2,
             rwkv_g2, rwkv_k_k, rwkv_k_a, rwkv_r_k, rwkv_ln_w, rwkv_ln_b, attn_q_norm, attn_k_norm, w_proj_rwkv,
             w_proj_attn, w_out, ffn2_norm, ffn2_w_in, ffn2_w_out)
    m_all = (m_ffn1_norm, m_ffn1_w_in, m_ffn1_w_out, m_mix_norm, m_w_in, m_b_gate, m_rwkv_mu, m_rwkv_w0, m_rwkv_w2,
             m_rwkv_a0, m_rwkv_a2, m_rwkv_g2, m_rwkv_k_k, m_rwkv_k_a, m_rwkv_r_k, m_rwkv_ln_w, m_rwkv_ln_b,
             m_attn_q_norm, m_attn_k_norm, m_w_proj_rwkv, m_w_proj_attn, m_w_out, m_ffn2_norm, m_ffn2_w_in, m_ffn2_w_out)
    v_all = (v_ffn1_norm, v_ffn1_w_in, v_ffn1_w_out, v_mix_norm, v_w_in, v_b_gate, v_rwkv_mu, v_rwkv_w0, v_rwkv_w2,
             v_rwkv_a0, v_rwkv_a2, v_rwkv_g2, v_rwkv_k_k, v_rwkv_k_a, v_rwkv_r_k, v_rwkv_ln_w, v_rwkv_ln_b,
             v_attn_q_norm, v_attn_k_norm, v_w_proj_rwkv, v_w_proj_attn, v_w_out, v_ffn2_norm, v_ffn2_w_in, v_ffn2_w_out)
    Wl = {n: t[0] for n, t in zip(names, w_all)}
    Ml = {n: t[0] for n, t in zip(names, m_all)}
    Vl = {n: t[0] for n, t in zip(names, v_all)}
    big = [n for n, _ in _BIG]

    my_c = lax.axis_index("c").astype(jnp.int32).reshape(1)
    my_chip = (2 * lax.axis_index("x") + lax.axis_index("y")).astype(jnp.int32).reshape(1)

    def pack(groups):
        return _pack_big(Wl, groups).astype(bf16)

    gathered = _all_gather(pack(_GROUPS_FIRST))
    W, P = _build_W_first(_unpack_gathered(gathered, Wl, _GROUPS_FIRST)), _build_P(Wl)
    hooks = {"gather_mid": (pack(_GROUPS_MID), lambda g: _build_W_mid(_unpack_gathered(g, Wl, _GROUPS_MID))),
             "gather_late": (pack(_GROUPS_LATE), lambda g: _build_W_late(_unpack_gathered(g, Wl, _GROUPS_LATE))),
             "reduce_mid": lambda G: _reduce_start(G, Wl, _GRADS_MID, my_c, "mid"),
             "reduce_late": lambda G: _reduce_start(G, Wl, _GROUPS_LATE, my_c, "late")}

    loss_local, dx0, G, Gs = _local_step(x[0], loss_target[0], W, P, hooks)

    part_last = _reduce_start(G, Wl, _GRADS_LAST, my_c, "last")
    g_big = _unpack_big(_grad_sum(part_last, _grad_cross(part_last), my_chip, 128, "last"), Wl, _GRADS_LAST)
    g_big.update(_unpack_big(_grad_sum(*G["mid"], my_chip, 128, "mid"), Wl, _GRADS_MID))
    g_big.update(_unpack_big(_grad_sum(*G["late"], my_chip, 128, "late"), Wl, _GROUPS_LATE))

    mu_g = Gs["mu_rkv"], Gs["mu_lora"]
    o1, o2 = LORA_PAD[0], LORA_PAD[0] + LORA_PAD[1]
    g_small_local = {
        "ffn1_norm": Gs["ffn1_norm"], "mix_norm": Gs["mix_norm"], "b_gate": Gs["b_gate"],
        "rwkv_mu": jnp.concatenate([mu_g[0], mu_g[1][:, :64], mu_g[1][:, o1:o1 + 64], mu_g[1][:, o2:o2 + 160]], axis=1),
        "rwkv_w0": Gs["w0"], "rwkv_a0": Gs["a0"], "rwkv_k_k": Gs["k_k"], "rwkv_k_a": Gs["k_a"], "rwkv_r_k": Gs["r_k"],
        "rwkv_ln_w": Gs["ln_w"], "rwkv_ln_b": Gs["ln_b"], "attn_q_norm": Gs["q_norm"], "attn_k_norm": Gs["k_norm"],
        "ffn2_norm": Gs["ffn2_norm"]}
    gs_pack = _small_all_reduce(_pack_small(g_small_local))

    out_g, out_d, out_m, out_v = dict(g_big), {}, {}, {}
    for n in big:
        cols = Wl[n].shape[1]
        out_d[n], out_m[n], out_v[n] = _rowwise(f"adamw_{n}", _f_adamw, [Wl[n], g_big[n], Ml[n], Vl[n]], [],
                                                 [(cols, f32)] * 3)
    ds_pack, ms_pack, vs_pack = _rowwise(
        "adamw_small", _f_adamw, [_pack_small(Wl), gs_pack, _pack_small(Ml), _pack_small(Vl)], [], [(D, f32)] * 3)
    for out, pack in ((out_g, gs_pack), (out_d, ds_pack), (out_m, ms_pack), (out_v, vs_pack)):
        out.update(_unpack_small(pack, Wl))

    loss = lax.psum(loss_local, ("x", "y", "c"))
    return (loss, dx0[None], *[out_g[n][None] for n in names], *[out_d[n][None] for n in names],
            *[out_m[n][None] for n in names], *[out_v[n][None] for n in names])
```

```python
import functools

import jax
import jax.numpy as jnp
from jax import lax
from jax.experimental import pallas as pl
from jax.experimental.pallas import tpu as pltpu

f32 = jnp.float32
bf16 = jnp.bfloat16
MESH = pl.DeviceIdType.MESH

N_DEV = 8
D = 1024
D_FF = 2816
HEAD = 64
RW_HEADS = 16
ATTN_PAIRS = ((128, 1), (512, 4), (2048, 16))
ATTN_BLK = 128
HEADS_PER_GROUP = 4
ATTN_W = 768
LORA_PAD = (128, 128, 256)
LORA_W = (64, 64, 160)
GN_EPS = 64e-5
RMS_EPS = 1e-6
NEG_INF = -1e30
WKV_T = 64
WKV_SUB = 2
GRAD_WIRE = bf16
PACK_BLOCK = 128
SMALL_ROWS = 24
VMEM_LIMIT = 56 * 1024 * 1024

ADAM_LR, ADAM_B1, ADAM_B2, ADAM_EPS, ADAM_WD, ADAM_STEP = 0.001, 0.9, 0.999, 1e-08, 0.01, 10


def _cparams(sem):
    return pltpu.CompilerParams(dimension_semantics=sem, vmem_limit_bytes=VMEM_LIMIT)


def _pick(n, cands):
    for c in cands:
        if n % c == 0:
            return c
    return n


HALO = 8


def _rowwise(name, fn, rows, params, outs, accs=(), tm=256, halos=(), carries=(), reverse=False):
    S = rows[0].shape[0]
    tm = min(tm, S)
    while S % tm:
        tm -= 8
    nb = S // tm
    n_in = len(rows) + len(params) + len(halos)
    n_out = len(outs)
    n_acc = len(accs)
    n_car = len(carries)

    def blk_of(i):
        return nb - 1 - i if reverse else i

    def body(*refs):
        step = pl.program_id(0)
        carry_refs = refs[n_in + n_out + n_acc:]
        if n_car:
            @pl.when(step == 0)
            def _():
                for c_ref in carry_refs:
                    c_ref[...] = jnp.zeros(c_ref.shape, f32)
        args = [r[...] for r in refs[:n_in]] + [c[...] for c in carry_refs]
        res = fn(*args, blk=blk_of(step)) if (halos or carries) else fn(*args)
        if not isinstance(res, (tuple, list)):
            res = (res,)
        out_refs = refs[n_in:n_in + n_out + n_acc]
        for j in range(n_out):
            out_refs[j][...] = res[j].astype(out_refs[j].dtype)
        if n_acc:
            @pl.when(step == 0)
            def _():
                for j in range(n_acc):
                    out_refs[n_out + j][...] = jnp.zeros(out_refs[n_out + j].shape, f32)
            for j in range(n_acc):
                out_refs[n_out + j][...] += res[n_out + j]
        for j in range(n_car):
            carry_refs[j][...] = res[n_out + n_acc + j]

    in_specs = [pl.BlockSpec((tm, a.shape[1]), lambda i: (blk_of(i), 0)) for a in rows]
    in_specs += [pl.BlockSpec(p.shape, lambda i, nd=p.ndim: (0,) * nd) for p in params]
    in_specs += [pl.BlockSpec((HALO, rows[h].shape[1]), lambda i: (jnp.maximum(blk_of(i) * (tm // HALO) - 1, 0), 0))
                 for h in halos]
    out_specs = [pl.BlockSpec((tm, w), lambda i: (blk_of(i), 0)) for w, _ in outs]
    out_specs += [pl.BlockSpec(s, lambda i: (0, 0)) for s in accs]
    out_shape = [jax.ShapeDtypeStruct((S, w), dt) for w, dt in outs]
    out_shape += [jax.ShapeDtypeStruct(s, f32) for s in accs]
    res = pl.pallas_call(
        body, name=name, grid=(nb,), in_specs=in_specs, out_specs=out_specs, out_shape=out_shape,
        scratch_shapes=[pltpu.VMEM(s, f32) for s in carries],
        compiler_params=_cparams(("arbitrary",)),
    )(*rows, *params, *[rows[h] for h in halos])
    return res


MM_VMEM_BUDGET = 40 * 1024 * 1024
MM_STEP_US = 0.35
MM_FLOPS_PER_US = 9.0e8
MM_HBM_BYTES_PER_US = 3.0e6


def _tile_options(n, cap):
    opts = [d for d in range(128, min(n, cap) + 1, 128) if n % d == 0]
    return opts or [n]


def _mm_tiles(M, N, K, sa, sb, so):
    best, best_cost = None, None
    for tm in _tile_options(M, 2048):
        for tn in _tile_options(N, 2048):
            for tk in _tile_options(K, 4096):
                vmem = 2 * (tm * tk * sa + tk * tn * sb) + 2 * tm * tn * so + (tm * tn * 4 if tk < K else 0)
                if vmem > MM_VMEM_BUDGET:
                    continue
                steps = (M // tm) * (N // tn) * (K // tk)
                traffic = M * K * sa * (N // tn) + K * N * sb * (M // tm) + M * N * so
                cost = (max(2.0 * M * N * K / MM_FLOPS_PER_US, traffic / MM_HBM_BYTES_PER_US) + steps * MM_STEP_US
                        + (tm * tk * sa + tk * tn * sb) / MM_HBM_BYTES_PER_US)
                if best_cost is None or cost < best_cost:
                    best, best_cost = (tm, tn, tk), cost
    return best


def _mm(name, a, b, mode, out_dtype=f32, scale=None, gather=None, cross=None):
    if mode == "nn":
        (M, K), (_, N) = a.shape, b.shape
    elif mode == "nt":
        (M, K), (N, _) = a.shape, b.shape
    else:
        (K, M), (_, N) = a.shape, b.shape
    tm, tn, tk = _mm_tiles(M, N, K, a.dtype.itemsize, b.dtype.itemsize, jnp.dtype(out_dtype).itemsize)
    nk = K // tk
    if mode == "nn":
        a_spec = pl.BlockSpec((tm, tk), lambda i, j, k: (i, k))
        b_spec = pl.BlockSpec((tk, tn), lambda i, j, k: (k, j))
        dims = (((1,), (0,)), ((), ()))
    elif mode == "nt":
        a_spec = pl.BlockSpec((tm, tk), lambda i, j, k: (i, k))
        b_spec = pl.BlockSpec((tn, tk), lambda i, j, k: (j, k))
        dims = (((1,), (1,)), ((), ()))
    else:
        a_spec = pl.BlockSpec((tk, tm), lambda i, j, k: (k, i))
        b_spec = pl.BlockSpec((tk, tn), lambda i, j, k: (k, j))
        dims = (((0,), (0,)), ((), ()))

    def finish(acc):
        return acc if scale is None else acc * scale

    hosted = gather if gather is not None else cross
    grid = (M // tm, N // tn, nk)
    steps = grid[0] * grid[1] * grid[2]

    def body(a_ref, b_ref, *rest):
        if hosted is None:
            o_ref, *scratch = rest
        else:
            src_ref, o_ref, dst_ref, *scratch = rest
            n_sem = len(GATHER_SEMS if gather is not None else CROSS_SEMS)
            sems, scratch = scratch[len(scratch) - n_sem:], scratch[:len(scratch) - n_sem]
            step = (pl.program_id(0) * grid[1] + pl.program_id(1)) * grid[2] + pl.program_id(2)
            if gather is not None:
                start, forward, done = _gather_phases(src_ref, dst_ref, *sems)
                pl.when(step == steps // 2)(forward)
            else:
                start, done = _cross_phases(src_ref, dst_ref, *sems)
            pl.when(step == 0)(start)
        part = lax.dot_general(a_ref[...].astype(bf16), b_ref[...].astype(bf16), dims,
                               preferred_element_type=f32)
        if nk == 1:
            o_ref[...] = finish(part).astype(o_ref.dtype)
        else:
            acc_ref = scratch[0]
            k = pl.program_id(2)

            @pl.when(k == 0)
            def _():
                acc_ref[...] = part

            @pl.when(k > 0)
            def _():
                acc_ref[...] += part

            @pl.when(k == nk - 1)
            def _():
                o_ref[...] = finish(acc_ref[...]).astype(o_ref.dtype)
        if hosted is not None:
            pl.when(step == steps - 1)(done)

    hbm = pl.BlockSpec(memory_space=pl.ANY)
    out_specs = [pl.BlockSpec((tm, tn), lambda i, j, k: (i, j))]
    out_shape = [jax.ShapeDtypeStruct((M, N), out_dtype)]
    scratch_shapes = [] if nk == 1 else [pltpu.VMEM((tm, tn), f32)]
    if gather is not None:
        out_specs.append(hbm)
        out_shape.append(jax.ShapeDtypeStruct((N_DEV,) + gather.shape, gather.dtype))
        scratch_shapes = scratch_shapes + GATHER_SEMS
    elif cross is not None:
        out_specs.append(hbm)
        out_shape.append(jax.ShapeDtypeStruct((3,) + cross.shape[1:], cross.dtype))
        scratch_shapes = scratch_shapes + CROSS_SEMS
    res = pl.pallas_call(
        body, name=name, grid=grid, in_specs=[a_spec, b_spec] + [hbm] * (hosted is not None),
        out_specs=out_specs, out_shape=out_shape, scratch_shapes=scratch_shapes,
        compiler_params=_cparams(("arbitrary",) * 3 if hosted is not None else ("parallel", "parallel", "arbitrary")),
    )(a, b, *([hosted] if hosted is not None else []))
    return res[0] if hosted is None else res


def _sp(x):
    hi = x.astype(bf16)
    return hi, (x - hi.astype(f32)).astype(bf16)


def _cat(parts):
    return tuple(jnp.concatenate(p, axis=1) for p in zip(*parts))


def _bmm(eq, a, b):
    (ah, al), (bh, bl) = a, b
    dot = functools.partial(jnp.einsum, eq, preferred_element_type=f32)
    return dot(ah, bh) + (dot(ah, bl) + dot(al, bh))


def _tri_dot(eq, tri, x):
    h1 = x.astype(bf16)
    r1 = x - h1.astype(f32)
    h2 = r1.astype(bf16)
    h3 = (r1 - h2.astype(f32)).astype(bf16)
    dot = functools.partial(jnp.einsum, eq, preferred_element_type=f32)
    return dot(tri, h1) + (dot(tri, h2) + dot(tri, h3))


def _tri_masks(T):
    ti = lax.broadcasted_iota(jnp.int32, (T, T), 0)
    si = lax.broadcasted_iota(jnp.int32, (T, T), 1)
    return ti >= si, ti > si


def _wkv_prep(r, lw, k, kkr, a):
    H, T, _ = r.shape
    low_i, low_s = _tri_masks(T)
    nrm = jnp.sqrt(jnp.sum(kkr * kkr, axis=-1, keepdims=True))
    den = jnp.maximum(nrm, 1e-12)
    kk = kkr / den
    tri = jnp.broadcast_to(low_i.astype(bf16)[None], (H, T, T))
    cl = _tri_dot("hts,hsn->htn", tri, lw)
    c = jnp.exp(cl)
    cprev = jnp.exp(cl - lw)
    cinv = jnp.exp(-cl)
    bt, kt = _sp(kk * a * cinv), _sp(k * cinv)
    L = _cat([_sp(r * c), _sp(-kk * cprev)])
    Mb = _bmm("htn,hsn->hts", L, bt)
    Mk = _bmm("htn,hsn->hts", L, kt)
    A_rb = jnp.where(low_i[None], Mb[:, :T], 0.0)
    A_ab = jnp.where(low_s[None], Mb[:, T:], 0.0)
    Mk = jnp.concatenate([jnp.where(low_i[None], Mk[:, :T], 0.0), jnp.where(low_s[None], Mk[:, T:], 0.0)], axis=1)
    return dict(kk=kk, den=den, nrm=nrm, c=c, cprev=cprev, cinv=cinv, L=L, kt=kt, bt=bt,
                A_ab=A_ab, A_rb=A_rb, Mk=Mk, cT=c[:, T - 1:T, :])


def _tri_inverse(A):
    T = A.shape[-1]
    eye = (lax.broadcasted_iota(jnp.int32, (T, T), 0) == lax.broadcasted_iota(jnp.int32, (T, T), 1)).astype(f32)
    inv = eye[None] + A
    X = A
    n = 1
    while 2 * n < T:
        Xs = _sp(X)
        X = _bmm("hts,hsu->htu", Xs, Xs)
        inv = inv + _bmm("hts,hsu->htu", _sp(inv), _sp(X))
        n *= 2
    return inv


def _wkv_chunk_fwd(S0, r, lw, k, v, kkr, a):
    T = r.shape[1]
    q = _wkv_prep(r, lw, k, kkr, a)
    inv = _tri_inverse(q["A_ab"])
    vs = _sp(v)
    P = _bmm("htk,hvk->htv", q["L"], _sp(S0)) + _bmm("hts,hsv->htv", _sp(q["Mk"]), vs)
    U = _bmm("hts,hsv->htv", _sp(inv), _sp(P[:, T:]))
    Us = _sp(U)
    Y = P[:, :T] + _bmm("hts,hsv->htv", _sp(q["A_rb"]), Us)
    S1 = (S0 + _bmm("htv,htk->hvk", _cat([Us, vs]), _cat([q["bt"], q["kt"]]))) * q["cT"]
    return Y, U, inv, S1


def _wkv_chunk_bwd(S0, Hin, Q, r, lw, k, v, kkr, a, U, inv, dY):
    H, T, _ = r.shape
    low_i, low_s = _tri_masks(T)
    q = _wkv_prep(r, lw, k, kkr, a)
    L, kt, bt = q["L"], q["kt"], q["bt"]
    R = _cat([bt, kt])
    Hh = Hin * q["cT"]
    Hs, S0s, dYs, vs, Us = _sp(Hh), _sp(S0), _sp(dY), _sp(v), _sp(U)
    RH = _bmm("htk,hvk->htv", R, Hs)
    Z = _bmm("hst,hsv->htv", _sp(inv), _sp(RH[:, :T] + _bmm("hst,hsv->htv", _sp(q["A_rb"]), dYs)))
    DZ = _cat([dYs, _sp(Z)])
    both = jnp.concatenate([jnp.broadcast_to(low_i[None], (1, T, T)), jnp.broadcast_to(low_s[None], (1, T, T))], axis=1)
    NU = _sp(jnp.where(both, _bmm("htv,hsv->hts", DZ, Us), 0.0))
    NV = _sp(jnp.where(both, _bmm("htv,hsv->hts", DZ, vs), 0.0))
    ra = _bmm("htv,hvk->htk", DZ, S0s) + _bmm("hts,hsk->htk", NU, bt) + _bmm("hts,hsk->htk", NV, kt)
    dr = ra[:, :T] * q["c"]
    da = ra[:, T:] * q["cprev"]
    dv = RH[:, T:] + _bmm("hst,hsv->htv", _sp(q["Mk"]), DZ)
    VH = _bmm("htv,hvk->htk", _cat([vs, Us]), Hs)
    dk = (VH[:, :T] + _bmm("hst,hsk->htk", NV, L)) * q["cinv"]
    db = (VH[:, T:] + _bmm("hst,hsk->htk", NU, L)) * q["cinv"]
    H0 = Hh + _bmm("htv,htk->hvk", DZ, L)
    kk = q["kk"]
    e = r * dr - kk * a * db - k * dk
    f = -kk * da
    tri_i = jnp.broadcast_to(low_i.astype(bf16)[None], (H, T, T))
    tri_s = jnp.broadcast_to(low_s.astype(bf16)[None], (H, T, T))
    dlw = _tri_dot("hst,hsn->htn", tri_i, e) + _tri_dot("hst,hsn->htn", tri_s, f) + Q
    Qn = Q + jnp.sum(e + f, axis=1, keepdims=True)
    dkk = db * a - da
    dasig = db * kk
    proj = jnp.sum(dkk * kk, axis=-1, keepdims=True)
    dkkr = jnp.where(q["nrm"] > 1e-12, dkk - kk * proj, dkk) / q["den"]
    return dr, dlw, dk, dv, dkkr, dasig, H0, Qn


def _heads(ref, rows=slice(None)):
    return jnp.stack([ref[rows, h * HEAD:(h + 1) * HEAD] for h in range(RW_HEADS)], axis=0)


def _put_heads(ref, val, rows=slice(None)):
    for h in range(RW_HEADS):
        ref[rows, h * HEAD:(h + 1) * HEAD] = val[h]


def _wkv_fwd(r, lw, k, v, kkr, a, g, r_k, ln_w, ln_b, late_pack=None):
    S = r.shape[0]
    H, N, T = RW_HEADS, HEAD, WKV_T
    TS = T * WKV_SUB
    nc = S // TS
    hosting = late_pack is not None

    def body(r_ref, lw_ref, k_ref, v_ref, kkr_ref, a_ref, g_ref, rk_ref, lnw_ref, lnb_ref, *rest):
        if hosting:
            pack_ref, y_ref, yg_ref, wkv_ref, u_ref, inv_ref, s0_ref, gathered_ref, state, *sems = rest
            start, forward, finish = _gather_phases(pack_ref, gathered_ref, *sems)
            pl.when(pl.program_id(0) == 0)(start)
            pl.when(pl.program_id(0) == nc // 2)(forward)
        else:
            y_ref, yg_ref, wkv_ref, u_ref, inv_ref, s0_ref, state = rest

        @pl.when(pl.program_id(0) == 0)
        def _():
            state[...] = jnp.zeros(state.shape, f32)

        S0 = state[...]
        for c in range(WKV_SUB):
            rows = slice(c * T, (c + 1) * T)
            s0_ref[c] = S0
            rr, kk2, vv = _heads(r_ref, rows), _heads(k_ref, rows), _heads(v_ref, rows)
            Y, U, inv, S0 = _wkv_chunk_fwd(S0, rr, _heads(lw_ref, rows), kk2, vv, _heads(kkr_ref, rows),
                                           _heads(a_ref, rows))
            wkv_ref[:, rows, :] = Y
            u_ref[:, rows, :] = U
            inv_ref[:, rows, :] = inv
            mean = jnp.mean(Y, axis=-1, keepdims=True)
            var = jnp.mean(jnp.square(Y - mean), axis=-1, keepdims=True)
            yn = (Y - mean) * lax.rsqrt(var + GN_EPS)
            bonus = jnp.sum(rr * kk2 * rk_ref[...], axis=-1, keepdims=True) * vv
            _put_heads(y_ref, yn * lnw_ref[...] + lnb_ref[...] + bonus, rows)
        state[...] = S0
        yg_ref[...] = (y_ref[...] * g_ref[...]).astype(yg_ref.dtype)
        if hosting:
            pl.when(pl.program_id(0) == nc - 1)(finish)

    tok = pl.BlockSpec((TS, H * N), lambda i: (i, 0))
    blk = pl.BlockSpec((H, TS, N), lambda i: (0, i, 0))
    par = pl.BlockSpec((H, 1, N), lambda i: (0, 0, 0))
    hbm = pl.BlockSpec(memory_space=pl.ANY)
    seq = jax.ShapeDtypeStruct((H, S, N), f32)
    out_specs = [tok, tok, blk, blk, blk, pl.BlockSpec((WKV_SUB, H, N, N), lambda i: (i, 0, 0, 0))]
    out_shape = [jax.ShapeDtypeStruct((S, H * N), f32), jax.ShapeDtypeStruct((S, H * N), bf16), seq, seq, seq,
                 jax.ShapeDtypeStruct((S // T, H, N, N), f32)]
    if hosting:
        out_specs.append(hbm)
        out_shape.append(jax.ShapeDtypeStruct((N_DEV,) + late_pack.shape, late_pack.dtype))
    return pl.pallas_call(
        body, name="wkv_fwd", grid=(nc,), in_specs=[tok] * 7 + [par] * 3 + [hbm] * hosting,
        out_specs=out_specs, out_shape=out_shape,
        scratch_shapes=[pltpu.VMEM((H, N, N), f32)] + (GATHER_SEMS if hosting else []),
        compiler_params=_cparams(("arbitrary",)),
    )(r, lw, k, v, kkr, a, g, r_k, ln_w, ln_b, *([late_pack] if hosting else []))


def _wkv_bwd(dy, g, r, lw, k, v, kkr, a, wkv, U, inv, S0s, r_k, ln_w, ln_b, late_pieces=None):
    S = r.shape[0]
    H, N, T = RW_HEADS, HEAD, WKV_T
    TS = T * WKV_SUB
    nc = S // TS
    hosting = late_pieces is not None
    n_late = len(late_pieces) if hosting else 0

    def body(dy_ref, g_ref, r_ref, lw_ref, k_ref, v_ref, kkr_ref, a_ref, wkv_ref, u_ref, inv_ref, s0_ref,
             rk_ref, lnw_ref, lnb_ref, *rest):
        if hosting:
            piece_refs, rest = rest[:n_late], rest[n_late:]
            (dr_ref, dlw_ref, dk_ref, dv_ref, dkkr_ref, da_ref, drk_ref, dlnw_ref, dlnb_ref, recv_ref,
             hst, qst, *sems) = rest
            start, finish = _direct_phases(piece_refs, [p.shape[1] for p in late_pieces], recv_ref, *sems)
            pl.when(pl.program_id(0) == 0)(start)
        else:
            dr_ref, dlw_ref, dk_ref, dv_ref, dkkr_ref, da_ref, drk_ref, dlnw_ref, dlnb_ref, hst, qst = rest

        @pl.when(pl.program_id(0) == 0)
        def _():
            hst[...] = jnp.zeros(hst.shape, f32)
            qst[...] = jnp.zeros(qst.shape, f32)
            drk_ref[...] = jnp.zeros(drk_ref.shape, f32)
            dlnw_ref[...] = jnp.zeros(dlnw_ref.shape, f32)
            dlnb_ref[...] = jnp.zeros(dlnb_ref.shape, f32)

        dyg = dy_ref[...] * g_ref[...]
        rk = rk_ref[...]
        Hst, Qst = hst[...], qst[...]
        for c in reversed(range(WKV_SUB)):
            rows = slice(c * T, (c + 1) * T)
            dya = _heads(dyg, rows)
            rr, kk2, vv, Y = _heads(r_ref, rows), _heads(k_ref, rows), _heads(v_ref, rows), wkv_ref[:, rows, :]
            s = jnp.sum(rr * kk2 * rk, axis=-1, keepdims=True)
            ds = jnp.sum(dya * vv, axis=-1, keepdims=True)
            mean = jnp.mean(Y, axis=-1, keepdims=True)
            var = jnp.mean(jnp.square(Y - mean), axis=-1, keepdims=True)
            rstd = lax.rsqrt(var + GN_EPS)
            yn = (Y - mean) * rstd
            dyn = dya * lnw_ref[...]
            dY = rstd * (dyn - jnp.mean(dyn, axis=-1, keepdims=True) - yn * jnp.mean(dyn * yn, axis=-1, keepdims=True))
            drk_ref[...] += jnp.sum(ds * rr * kk2, axis=1, keepdims=True)
            dlnw_ref[...] += jnp.sum(dya * yn, axis=1, keepdims=True)
            dlnb_ref[...] += jnp.sum(dya, axis=1, keepdims=True)
            dr, dlw, dk, dv, dkkr, dasig, Hst, Qst = _wkv_chunk_bwd(
                s0_ref[c], Hst, Qst, rr, _heads(lw_ref, rows), kk2, vv, _heads(kkr_ref, rows), _heads(a_ref, rows),
                u_ref[:, rows, :], inv_ref[:, rows, :], dY)
            _put_heads(dr_ref, dr + ds * kk2 * rk, rows)
            _put_heads(dlw_ref, dlw, rows)
            _put_heads(dk_ref, dk + ds * rr * rk, rows)
            _put_heads(dv_ref, dv + dya * s, rows)
            _put_heads(dkkr_ref, dkkr, rows)
            _put_heads(da_ref, dasig, rows)
        hst[...] = Hst
        qst[...] = Qst
        if hosting:
            pl.when(pl.program_id(0) == nc - 1)(finish)

    tok = pl.BlockSpec((TS, H * N), lambda i: (nc - 1 - i, 0))
    blk = pl.BlockSpec((H, TS, N), lambda i: (0, nc - 1 - i, 0))
    par = pl.BlockSpec((H, 1, N), lambda i: (0, 0, 0))
    hbm = pl.BlockSpec(memory_space=pl.ANY)
    seq = jax.ShapeDtypeStruct((S, H * N), f32)
    pout = jax.ShapeDtypeStruct((H, 1, N), f32)
    out_specs, out_shape = [tok] * 6 + [par] * 3, [seq] * 6 + [pout] * 3
    sems = []
    if hosting:
        rows_late = sum(p.shape[1] for p in late_pieces)
        out_specs.append(hbm)
        out_shape.append(jax.ShapeDtypeStruct((N_DEV - 1, rows_late, late_pieces[0].shape[2]), late_pieces[0].dtype))
        sems = [pltpu.SemaphoreType.DMA((n_late * (N_DEV - 1),))] * 2
    return pl.pallas_call(
        body, name="wkv_bwd", grid=(nc,),
        in_specs=([tok] * 8 + [blk] * 3 + [pl.BlockSpec((WKV_SUB, H, N, N), lambda i: (nc - 1 - i, 0, 0, 0))]
                  + [par] * 3 + [hbm] * n_late),
        out_specs=out_specs, out_shape=out_shape,
        scratch_shapes=[pltpu.VMEM((H, N, N), f32), pltpu.VMEM((H, 1, N), f32)] + sems,
        compiler_params=_cparams(("arbitrary",)),
    )(dy, g, r, lw, k, v, kkr, a, wkv, U, inv, S0s, r_k, ln_w, ln_b, *(late_pieces if hosting else []))


ATTN_TT = 2048


def _attn_rows(d, i, j):
    return pl.ds(ATTN_BLK * d * i + j, ATTN_BLK, stride=d) if d > 1 else pl.ds(ATTN_BLK * i, ATTN_BLK)


def _attn_take(ref, d, nsub):
    return jnp.stack([ref[_attn_rows(d, i, j), :] for i in range(nsub) for j in range(d)], axis=0)


def _attn_put(ref, val, d):
    for i in range(val.shape[0] // d):
        for j in range(d):
            ref[_attn_rows(d, i, j), :] = val[i * d + j]


def _attn_prev(cur, before, d):
    return before if cur.shape[0] == d else jnp.concatenate([before, cur[:cur.shape[0] - d]], axis=0)


def _attn_specs(gi, d, nt, reverse):
    per_tile = ATTN_TT // (ATTN_BLK * d)

    def tile(n):
        return nt - 1 - n if reverse else n

    def col(kind):
        return lambda hp, n: (tile(n), kind * (ATTN_W // 128) + 2 * gi + hp)

    def col_before(kind):
        return lambda hp, n: (jnp.maximum(tile(n) * per_tile - 1, 0), kind * (ATTN_W // 128) + 2 * gi + hp)

    cur = [pl.BlockSpec((ATTN_TT, 128), col(kind)) for kind in range(3)]
    before = [pl.BlockSpec((ATTN_BLK * d, 128), col_before(kind)) for kind in (1, 2)]
    own = pl.BlockSpec((ATTN_TT, 128), lambda hp, n: (tile(n), hp))
    return cur, before, own, tile


def _attn_norm(x, gain, scale):
    rs = lax.rsqrt(jnp.mean(x * x, axis=-1, keepdims=True) + RMS_EPS)
    return x * rs * (gain * scale), rs


def _attn_scores(qn, kn_c, kn_p, first):
    s_c = jnp.einsum("gqe,gke->gqk", qn.astype(bf16), kn_c.astype(bf16), preferred_element_type=f32)
    s_p = jnp.einsum("gqe,gke->gqk", qn.astype(bf16), kn_p.astype(bf16), preferred_element_type=f32)
    qi = lax.broadcasted_iota(jnp.int32, (1, ATTN_BLK, ATTN_BLK), 1)
    ki = lax.broadcasted_iota(jnp.int32, (1, ATTN_BLK, ATTN_BLK), 2)
    s_c = jnp.where(qi >= ki, s_c, NEG_INF)
    s_p = jnp.where(jnp.logical_and(ki >= qi, jnp.logical_not(first)), s_p, NEG_INF)
    return s_c, s_p


def _attn_fwd(pqkv, qg, kg, gi, S):
    d = ATTN_PAIRS[gi][1]
    nt = S // ATTN_TT
    nsub = ATTN_TT // (ATTN_BLK * d)
    nd = nsub * d

    def body(q_ref, k_ref, v_ref, kb_ref, vb_ref, qg_ref, kg_ref, o_ref, lse_ref):
        Q, K, V = _attn_take(q_ref, d, nsub), _attn_take(k_ref, d, nsub), _attn_take(v_ref, d, nsub)
        KB, VB = _attn_take(kb_ref, d, 1), _attn_take(vb_ref, d, 1)
        first = jnp.logical_and(lax.broadcasted_iota(jnp.int32, (nd, 1, 1), 0) < d, pl.program_id(1) == 0)
        outs, lses = [], []
        for h in range(2):
            sl = slice(h * HEAD, (h + 1) * HEAD)
            kc, vc = K[:, :, sl], V[:, :, sl]
            kp, vp = _attn_prev(kc, KB[:, :, sl], d), _attn_prev(vc, VB[:, :, sl], d)
            qn, _ = _attn_norm(Q[:, :, sl], qg_ref[...], HEAD ** -0.5)
            kn_c, _ = _attn_norm(kc, kg_ref[...], 1.0)
            kn_p, _ = _attn_norm(kp, kg_ref[...], 1.0)
            s_c, s_p = _attn_scores(qn, kn_c, kn_p, first)
            m = jnp.maximum(jnp.max(s_c, axis=-1, keepdims=True), jnp.max(s_p, axis=-1, keepdims=True))
            p_c = jnp.exp(s_c - m)
            p_p = jnp.exp(s_p - m)
            den = jnp.sum(p_c, axis=-1, keepdims=True) + jnp.sum(p_p, axis=-1, keepdims=True)
            inv = 1.0 / den
            o = jnp.einsum("gqk,gke->gqe", (p_c * inv).astype(bf16), vc.astype(bf16), preferred_element_type=f32)
            o += jnp.einsum("gqk,gke->gqe", (p_p * inv).astype(bf16), vp.astype(bf16), preferred_element_type=f32)
            outs.append(o)
            lses.append(jnp.broadcast_to(m + jnp.log(den), o.shape))
        _attn_put(o_ref, jnp.concatenate(outs, axis=-1), d)
        _attn_put(lse_ref, jnp.concatenate(lses, axis=-1), d)

    cur, before, own, _ = _attn_specs(gi, d, nt, False)
    par = pl.BlockSpec((1, HEAD), lambda hp, n: (0, 0))
    shp = jax.ShapeDtypeStruct((S, 2 * 128), f32)
    return pl.pallas_call(
        body, name=f"attn_fwd{gi}", grid=(2, nt), in_specs=cur + before + [par] * 2, out_specs=[own, own],
        out_shape=[shp, shp], compiler_params=_cparams(("arbitrary", "arbitrary")),
    )(pqkv, pqkv, pqkv, pqkv, pqkv, qg, kg)


def _attn_bwd(pqkv, o, lse, do, dlse, qg, kg, gi, S):
    d = ATTN_PAIRS[gi][1]
    nt = S // ATTN_TT
    nsub = ATTN_TT // (ATTN_BLK * d)
    nd = nsub * d

    def norm_bwd(dxn, x, rs, gain, scale):
        xh = x * rs
        dxh = dxn * (gain * scale)
        dx = rs * (dxh - xh * jnp.mean(dxh * xh, axis=-1, keepdims=True))
        dgain = jnp.sum(jnp.sum(dxn * xh * scale, axis=1), axis=0, keepdims=True)
        return dx, dgain

    def to_before(part, carried):
        return carried if nsub == 1 else jnp.concatenate([part[d:], carried], axis=0)

    def body(q_ref, k_ref, v_ref, kb_ref, vb_ref, o_ref, lse_ref, do_ref, dlse_ref, qg_ref, kg_ref,
             dq_ref, dk_ref, dv_ref, dqg_ref, dkg_ref, carry_k, carry_v):
        step = pl.program_id(1)

        @pl.when(jnp.logical_and(pl.program_id(0) == 0, step == 0))
        def _():
            dqg_ref[...] = jnp.zeros(dqg_ref.shape, f32)
            dkg_ref[...] = jnp.zeros(dkg_ref.shape, f32)

        @pl.when(step == 0)
        def _():
            carry_k[...] = jnp.zeros(carry_k.shape, f32)
            carry_v[...] = jnp.zeros(carry_v.shape, f32)

        Q, K, V = _attn_take(q_ref, d, nsub), _attn_take(k_ref, d, nsub), _attn_take(v_ref, d, nsub)
        KB, VB = _attn_take(kb_ref, d, 1), _attn_take(vb_ref, d, 1)
        O, LSE = _attn_take(o_ref, d, nsub), _attn_take(lse_ref, d, nsub)
        DO, DLSE = _attn_take(do_ref, d, nsub), _attn_take(dlse_ref, d, nsub)
        first = jnp.logical_and(lax.broadcasted_iota(jnp.int32, (nd, 1, 1), 0) < d, step == nt - 1)
        qg, kg = qg_ref[...], kg_ref[...]
        dqs, dks, dvs = [], [], []
        for h in range(2):
            sl = slice(h * HEAD, (h + 1) * HEAD)
            qx, kx, vc = Q[:, :, sl], K[:, :, sl], V[:, :, sl]
            kpx, vp = _attn_prev(kx, KB[:, :, sl], d), _attn_prev(vc, VB[:, :, sl], d)
            qn, rq = _attn_norm(qx, qg, HEAD ** -0.5)
            kn_c, rk_c = _attn_norm(kx, kg, 1.0)
            kn_p, _ = _attn_norm(kpx, kg, 1.0)
            s_c, s_p = _attn_scores(qn, kn_c, kn_p, first)
            lse = LSE[:, :, h * HEAD:h * HEAD + 1]
            p_c = jnp.exp(s_c - lse)
            p_p = jnp.exp(s_p - lse)
            dO = DO[:, :, sl]
            dOb = dO.astype(bf16)
            dp_c = jnp.einsum("gqe,gke->gqk", dOb, vc.astype(bf16), preferred_element_type=f32)
            dp_p = jnp.einsum("gqe,gke->gqk", dOb, vp.astype(bf16), preferred_element_type=f32)
            corr = DLSE[:, :, h * HEAD:h * HEAD + 1] - jnp.sum(dO * O[:, :, sl], axis=-1, keepdims=True)
            ds_c = (p_c * (dp_c + corr)).astype(bf16)
            ds_p = (p_p * (dp_p + corr)).astype(bf16)
            qnb = qn.astype(bf16)
            dqn = (jnp.einsum("gqk,gke->gqe", ds_c, kn_c.astype(bf16), preferred_element_type=f32)
                   + jnp.einsum("gqk,gke->gqe", ds_p, kn_p.astype(bf16), preferred_element_type=f32))
            dkn_p = jnp.einsum("gqk,gqe->gke", ds_p, qnb, preferred_element_type=f32)
            dv_p = jnp.einsum("gqk,gqe->gke", p_p.astype(bf16), dOb, preferred_element_type=f32)
            dkn = jnp.einsum("gqk,gqe->gke", ds_c, qnb, preferred_element_type=f32) + to_before(dkn_p, carry_k[h])
            dv = (jnp.einsum("gqk,gqe->gke", p_c.astype(bf16), dOb, preferred_element_type=f32)
                  + to_before(dv_p, carry_v[h]))
            carry_k[h] = dkn_p[:d]
            carry_v[h] = dv_p[:d]
            dq, dqg = norm_bwd(dqn, qx, rq, qg, HEAD ** -0.5)
            dk, dkg = norm_bwd(dkn, kx, rk_c, kg, 1.0)
            dqg_ref[...] += dqg
            dkg_ref[...] += dkg
            dqs.append(dq)
            dks.append(dk)
            dvs.append(dv)
        _attn_put(dq_ref, jnp.concatenate(dqs, axis=-1), d)
        _attn_put(dk_ref, jnp.concatenate(dks, axis=-1), d)
        _attn_put(dv_ref, jnp.concatenate(dvs, axis=-1), d)

    cur, before, own, _ = _attn_specs(gi, d, nt, True)
    par = pl.BlockSpec((1, HEAD), lambda hp, n: (0, 0))
    shp = jax.ShapeDtypeStruct((S, 2 * 128), f32)
    pshp = jax.ShapeDtypeStruct((1, HEAD), f32)
    return pl.pallas_call(
        body, name=f"attn_bwd{gi}", grid=(2, nt), in_specs=cur + before + [own] * 4 + [par] * 2,
        out_specs=[own] * 3 + [par] * 2, out_shape=[shp] * 3 + [pshp] * 2,
        scratch_shapes=[pltpu.VMEM((2, d, ATTN_BLK, HEAD), f32)] * 2,
        compiler_params=_cparams(("arbitrary", "arbitrary")),
    )(pqkv, pqkv, pqkv, pqkv, pqkv, o, lse, do, dlse, qg, kg)


def _rms(x, g):
    rs = lax.rsqrt(jnp.mean(x * x, axis=-1, keepdims=True) + RMS_EPS)
    return x * rs * g


def _f_rms(x, g):
    return _rms(x, g)


def _f_resid_rms(coef, x, f, g):
    xn = x + coef * f
    return xn, _rms(xn, g)


def _f_swiglu(u):
    gate, up = u[:, :D_FF], u[:, D_FF:]
    return gate * jax.nn.sigmoid(gate) * up


def _f_swiglu_bwd(dact, u):
    gate, up = u[:, :D_FF], u[:, D_FF:]
    sg = jax.nn.sigmoid(gate)
    silu = gate * sg
    dact = 0.5 * dact
    return jnp.concatenate([dact * up * (sg * (1.0 + gate * (1.0 - sg))), dact * silu], axis=1)


def _f_rms_bwd(n_parts, *args):
    dns = args[:n_parts]
    x, dres, g = args[n_parts:]
    dn = dns[0]
    for t in dns[1:]:
        dn = dn + t
    rs = lax.rsqrt(jnp.mean(x * x, axis=-1, keepdims=True) + RMS_EPS)
    xh = x * rs
    dxh = dn * g
    dx = dres + rs * (dxh - xh * jnp.mean(dxh * xh, axis=-1, keepdims=True))
    return dx, dx, jnp.sum(dn * xh, axis=0, keepdims=True)


def _f_loss(x, f, tgt):
    y = x + 0.5 * f
    diff = y - tgt
    part = 0.5 * jnp.sum(jnp.mean(diff * diff, axis=-1, keepdims=True), axis=0, keepdims=True)
    dy = diff * (1.0 / D)
    return dy, dy, jnp.broadcast_to(part, (1, 128))


def _dotb(a, b, dims):
    return lax.dot_general(a.astype(bf16), b.astype(bf16), dims, preferred_element_type=f32)


_NN = (((1,), (0,)), ((), ()))
_NT = (((1,), (1,)), ((), ()))
_TN = (((0,), (0,)), ((), ()))


def _rwkv_pre_core(prkv, prkv_prev, plora, plora_prev, mu_rkv, mu_lora, w0, w2p, a0, a2p, g2p, k_k, k_a):
    xs = prkv + (prkv_prev - prkv) * mu_rkv
    xl = plora + (plora_prev - plora) * mu_lora
    r, k, v = xs[:, :D], xs[:, D:2 * D], xs[:, 2 * D:]
    wd, ad, gd = xl[:, :128], xl[:, 128:256], xl[:, 256:]
    tw = jnp.tanh(wd)
    zw = w0 + _dotb(tw, w2p, _NN)
    sp = jnp.maximum(-zw, 0.0) + jnp.log(1.0 + jnp.exp(-jnp.abs(zw)))
    lw = -jnp.exp(-sp - 0.5)
    a = jax.nn.sigmoid(a0 + _dotb(ad, a2p, _NN))
    sg = jax.nn.sigmoid(gd)
    return dict(r=r, k=k, v=v, tw=tw, zw=zw, lw=lw, a=a, sg=sg, ad=ad)


def _rows_down(x, halo, blk):
    before = jnp.where(blk > 0, halo[HALO - 1:HALO, :], 0.0)
    row = lax.broadcasted_iota(jnp.int32, (x.shape[0], 1), 0)
    return jnp.where(row == 0, before, pltpu.roll(x, 1, 0))


def _rows_up(x, after):
    n = x.shape[0]
    row = lax.broadcasted_iota(jnp.int32, (n, 1), 0)
    return jnp.where(row == n - 1, after, pltpu.roll(x, n - 1, 0))


def _f_rwkv_pre(prkv, plora, mu_rkv, mu_lora, w0, w2p, a0, a2p, g2p, k_k, k_a, halo_rkv, halo_lora, blk):
    c = _rwkv_pre_core(prkv, _rows_down(prkv, halo_rkv, blk), plora, _rows_down(plora, halo_lora, blk),
                       mu_rkv, mu_lora, w0, w2p, a0, a2p, g2p, k_k, k_a)
    g = _dotb(c["sg"], g2p, _NN)
    k, a = c["k"], c["a"]
    return c["r"], c["lw"], k * (1.0 + (a - 1.0) * k_a), c["v"], k * k_k, a, g


def _f_rwkv_pre_bwd(prkv, plora, dr, dlw, dk2, dv, dkkr, da, dya, yap,
                    mu_rkv, mu_lora, w0, w2p, a0, a2p, g2p, k_k, k_a, halo_rkv, halo_lora, next_rkv, next_lora, blk):
    prkv_prev, plora_prev = _rows_down(prkv, halo_rkv, blk), _rows_down(plora, halo_lora, blk)
    c = _rwkv_pre_core(prkv, prkv_prev, plora, plora_prev, mu_rkv, mu_lora, w0, w2p, a0, a2p, g2p, k_k, k_a)
    k, a, sg, tw, zw, lw = c["k"], c["a"], c["sg"], c["tw"], c["zw"], c["lw"]
    dg = dya * yap
    dsg = _dotb(dg, g2p, _NT)
    dgd = dsg * sg * (1.0 - sg)
    dg2p = _dotb(sg, dg, _TN)
    dk = dk2 * (1.0 + (a - 1.0) * k_a) + dkkr * k_k
    da_t = da + dk2 * k * k_a
    dk_a = jnp.sum(dk2 * k * (a - 1.0), axis=0, keepdims=True)
    dk_k = jnp.sum(dkkr * k, axis=0, keepdims=True)
    dza = da_t * a * (1.0 - a)
    da0 = jnp.sum(dza, axis=0, keepdims=True)
    dad = _dotb(dza, a2p, _NT)
    da2p = _dotb(c["ad"], dza, _TN)
    dzw = dlw * lw * jax.nn.sigmoid(-zw)
    dw0 = jnp.sum(dzw, axis=0, keepdims=True)
    dtw = _dotb(dzw, w2p, _NT)
    dw2p = _dotb(tw, dzw, _TN)
    dwd = dtw * (1.0 - tw * tw)
    dxs = jnp.concatenate([dr, dk, dv], axis=1)
    dxl = jnp.concatenate([dwd, dad, dgd], axis=1)
    dmu_rkv = jnp.sum(dxs * (prkv_prev - prkv), axis=0, keepdims=True)
    dmu_lora = jnp.sum(dxl * (plora_prev - plora), axis=0, keepdims=True)
    to_next_rkv, to_next_lora = dxs * mu_rkv, dxl * mu_lora
    return (dxs * (1.0 - mu_rkv) + _rows_up(to_next_rkv, next_rkv), dxl * (1.0 - mu_lora) + _rows_up(to_next_lora, next_lora),
            dmu_rkv, dmu_lora, dw0, da0, dk_k, dk_a, dw2p, da2p, dg2p, to_next_rkv[0:1], to_next_lora[0:1])


def _group_alpha(l0, l1, l2):
    m = jnp.maximum(jnp.maximum(l0, l1), l2)
    e0, e1, e2 = jnp.exp(l0 - m), jnp.exp(l1 - m), jnp.exp(l2 - m)
    inv = 1.0 / (e0 + e1 + e2)
    return jnp.concatenate([e0 * inv, e1 * inv, e2 * inv], axis=1)


def _f_combine(o0, o1, o2, l0, l1, l2):
    return jnp.concatenate([o0, o1, o2], axis=1) * _group_alpha(l0, l1, l2)


def _f_combine_bwd(dyb, o0, o1, o2, l0, l1, l2, bd):
    alpha = _group_alpha(l0, l1, l2)
    hi, lo = _sp(dyb * jnp.concatenate([o0, o1, o2], axis=1))
    ones = bd.astype(bf16)
    e = jnp.dot(hi, ones, preferred_element_type=f32) + jnp.dot(lo, ones, preferred_element_type=f32)
    ae = alpha * e
    tot = ae[:, :256] + ae[:, 256:512] + ae[:, 512:]
    do = dyb * alpha
    dl = ae - alpha * jnp.concatenate([tot, tot, tot], axis=1)
    return do[:, :256], do[:, 256:512], do[:, 512:], dl[:, :256], dl[:, 256:512], dl[:, 512:]


def _f_merge(pgate, ta, tb, b_gate):
    gate = jax.nn.sigmoid(pgate + b_gate)
    return gate[:, :D] * ta + gate[:, D:] * tb


def _f_merge_bwd(dm, pgate, ta, tb, b_gate):
    gate = jax.nn.sigmoid(pgate + b_gate)
    ga, gb = gate[:, :D], gate[:, D:]
    dpg = jnp.concatenate([dm * ta * ga * (1.0 - ga), dm * tb * gb * (1.0 - gb)], axis=1)
    return dm * ga, dm * gb, dpg, jnp.sum(dpg, axis=0, keepdims=True)


def _f_adamw(w, g, m, v):
    m2 = ADAM_B1 * m + (1.0 - ADAM_B1) * g
    v2 = ADAM_B2 * v + (1.0 - ADAM_B2) * jnp.square(g)
    m_hat = m2 / (1.0 - ADAM_B1 ** ADAM_STEP)
    v_hat = v2 / (1.0 - ADAM_B2 ** ADAM_STEP)
    delta = -ADAM_LR * (m_hat / (jnp.sqrt(v_hat) + ADAM_EPS) + ADAM_WD * w)
    return delta, m2, v2


def _ffn_bwd(tag, dxo, dxo_b, x_in, n, u, act, g, WiT, Wo, cross=None):
    dact = _mm(f"{tag}_dact", dxo_b, Wo, "nt")
    dWo = _mm(f"{tag}_dwo", act, dxo_b, "tn", out_dtype=GRAD_WIRE, scale=0.5)
    (du,) = _rowwise(f"{tag}_dswiglu", _f_swiglu_bwd, [dact, u], [], [(2 * D_FF, bf16)], tm=128)
    if cross is None:
        dn, recv = _mm(f"{tag}_dn", du, WiT, "nn"), None
    else:
        dn, recv = _mm(f"{tag}_dn", du, WiT, "nn", cross=cross)
    dWiT = _mm(f"{tag}_dwi", du, n, "tn", out_dtype=GRAD_WIRE)
    dx, dx_b, dg = _rowwise(f"{tag}_drms", functools.partial(_f_rms_bwd, 1), [dn, x_in, dxo], [g],
                            [(D, f32), (D, bf16)], [(1, D)])
    return dx, dx_b, dg, dWiT, dWo, recv


def _local_step(x0, tgt, W, P, hooks=None):
    S = x0.shape[0]
    (n1,) = _rowwise("f1_rms", _f_rms, [x0], [P["ffn1_norm"]], [(D, bf16)])
    hooks = hooks or {}
    if "gather_mid" in hooks:
        pack, weights = hooks["gather_mid"]
        u1, gathered = _mm("f1_up", n1, W["f1_iT"], "nt", gather=pack)
        W = {**W, **weights(gathered)}
    else:
        u1 = _mm("f1_up", n1, W["f1_iT"], "nt")
    (act1,) = _rowwise("f1_swiglu", _f_swiglu, [u1], [], [(D_FF, bf16)])
    if "gather_in" in hooks:
        pack, weights = hooks["gather_in"]
        f1, gathered = _mm("f1_down", act1, W["f1_o"], "nn", gather=pack)
        W = {**W, **weights(gathered)}
    else:
        f1 = _mm("f1_down", act1, W["f1_o"], "nn")
    x1, h = _rowwise("mix_rms", functools.partial(_f_resid_rms, 0.5), [x0, f1], [P["mix_norm"]],
                     [(D, f32), (D, bf16)])
    prkv = _mm("p_rkv", h, W["in_rkvT"], "nt")
    plora = _mm("p_lora", h, W["in_loraT"], "nt")
    pqkv = _mm("p_qkv", h, W["in_qkvT"], "nt")
    pgate = _mm("p_gate", h, W["in_gateT"], "nt")
    pre_params = [P["mu_rkv"], P["mu_lora"], P["w0"], W["w2p"], P["a0"], W["a2p"], W["g2p"], P["k_k"], P["k_a"]]
    r, lw, k2, v, kkr, a, g = _rowwise("rwkv_pre", _f_rwkv_pre, [prkv, plora], pre_params, [(D, f32)] * 7, tm=128,
                                       halos=(0, 1))
    hm = [r, lw, k2, v, kkr, a]
    hp = [P["r_k"].reshape(RW_HEADS, 1, HEAD), P["ln_w"].reshape(RW_HEADS, 1, HEAD), P["ln_b"].reshape(RW_HEADS, 1, HEAD)]
    if "gather_late" in hooks:
        pack, weights = hooks["gather_late"]
        yap, ya, wkv_h, U_h, inv_h, S0s, gathered = _wkv_fwd(*hm, g, *hp, late_pack=pack)
        W = {**W, **weights(gathered)}
    else:
        yap, ya, wkv_h, U_h, inv_h, S0s = _wkv_fwd(*hm, g, *hp)
    ta = _mm("proj_a", ya, W["pr"], "nn")
    n_grp = len(ATTN_PAIRS)
    attn = [_attn_fwd(pqkv, P["q_norm"], P["k_norm"], gi, S) for gi in range(n_grp)]
    o_g, lse_g = [t[0] for t in attn], [t[1] for t in attn]
    (yb,) = _rowwise("attn_combine", _f_combine, [*o_g, *lse_g], [], [(ATTN_W, bf16)])
    tb = _mm("proj_b", yb, W["paT"], "nt")
    (merged,) = _rowwise("merge", _f_merge, [pgate, ta, tb], [P["b_gate"]], [(D, bf16)])
    mo = _mm("mix_out", merged, W["out"], "nn")
    x2, n2 = _rowwise("f2_rms", functools.partial(_f_resid_rms, 1.0), [x1, mo], [P["ffn2_norm"]],
                      [(D, f32), (D, bf16)])
    u2 = _mm("f2_up", n2, W["f2_iT"], "nt")
    (act2,) = _rowwise("f2_swiglu", _f_swiglu, [u2], [], [(D_FF, bf16)])
    f2 = _mm("f2_down", act2, W["f2_o"], "nn")
    dx3, dx3_b, loss = _rowwise("loss", _f_loss, [x2, f2, tgt], [], [(D, f32), (D, bf16)], [(1, 128)])
    G, Gs = {}, {}
    dx2, dx2_b, Gs["ffn2_norm"], G["f2_iT"], G["f2_o"], _ = _ffn_bwd("f2", dx3, dx3_b, x2, n2, u2, act2,
                                                                    P["ffn2_norm"], W["f2_iT"], W["f2_o"])
    dmerged = _mm("d_merged", dx2_b, W["out"], "nt")
    G["out"] = _mm("dw_out", merged, dx2_b, "tn", out_dtype=GRAD_WIRE)
    dta, dtb, dpgate, Gs["b_gate"] = _rowwise("merge_bwd", _f_merge_bwd, [dmerged, pgate, ta, tb], [P["b_gate"]],
                                              [(D, bf16), (D, bf16), (2 * D, bf16)], [(1, 2 * D)])
    dya = _mm("d_ya", dta, W["pr"], "nt")
    G["pr"] = _mm("dw_pr", ya, dta, "tn", out_dtype=GRAD_WIRE)
    dyb = _mm("d_yb", dtb, W["paT"], "nn")
    G["paT"] = _mm("dw_pa", dtb, yb, "tn", out_dtype=GRAD_WIRE)
    if "reduce_late" in hooks:
        pieces_late = hooks["reduce_late"](G)
        hg = _wkv_bwd(dya, g, *hm, wkv_h, U_h, inv_h, S0s, *hp, late_pieces=pieces_late)
        G["late"] = (pieces_late, hg[9])
    else:
        hg = _wkv_bwd(dya, g, *hm, wkv_h, U_h, inv_h, S0s, *hp)
    dr, dlw, dk2, dv, dkkr, da = hg[:6]
    Gs["r_k"], Gs["ln_w"], Gs["ln_b"] = (t.reshape(1, D) for t in hg[6:9])
    lp = sum(LORA_PAD)
    (dprkv, dplora, Gs["mu_rkv"], Gs["mu_lora"], Gs["w0"], Gs["a0"], Gs["k_k"], Gs["k_a"],
     dw2p, da2p, dg2p) = _rowwise(
        "rwkv_pre_bwd", _f_rwkv_pre_bwd,
        [prkv, plora, dr, dlw, dk2, dv, dkkr, da, dya, yap], pre_params,
        [(3 * D, bf16), (lp, bf16)],
        [(1, 3 * D), (1, lp), (1, D), (1, D), (1, D), (1, D), (LORA_PAD[0], D), (LORA_PAD[1], D), (LORA_PAD[2], D)],
        tm=128, halos=(0, 1), carries=((1, 3 * D), (1, lp)), reverse=True)
    G["w2T"], G["a2T"], G["g2T"] = dw2p[:LORA_W[0]].T, da2p[:LORA_W[1]].T, dg2p[:LORA_W[2]].T
    bd = (jnp.arange(ATTN_W)[:, None] // HEAD == jnp.arange(ATTN_W)[None, :] // HEAD).astype(f32)
    dol = _rowwise("attn_combine_bwd", _f_combine_bwd, [dyb, *o_g, *lse_g], [bd], [(ATTN_W // n_grp, f32)] * (2 * n_grp))
    dattn = [_attn_bwd(pqkv, o_g[gi], lse_g[gi], dol[gi], dol[n_grp + gi], P["q_norm"], P["k_norm"], gi, S)
             for gi in range(n_grp)]
    Gs["q_norm"] = dattn[0][3] + dattn[1][3] + dattn[2][3]
    Gs["k_norm"] = dattn[0][4] + dattn[1][4] + dattn[2][4]
    dpqkv = jnp.concatenate([dattn[gi][kind] for kind in range(3) for gi in range(n_grp)], axis=1).astype(bf16)
    dh = [_mm("dh_rkv", dprkv, W["in_rkvT"], "nn"), _mm("dh_lora", dplora, W["in_loraT"], "nn"),
          _mm("dh_qkv", dpqkv, W["in_qkvT"], "nn"), _mm("dh_gate", dpgate, W["in_gateT"], "nn")]
    dW_rkv = _mm("dw_rkv", dprkv, h, "tn", out_dtype=GRAD_WIRE)
    dW_lora = _mm("dw_lora", dplora, h, "tn", out_dtype=GRAD_WIRE)
    dW_qkv = _mm("dw_qkv", dpqkv, h, "tn", out_dtype=GRAD_WIRE)
    dW_gate = _mm("dw_gate", dpgate, h, "tn", out_dtype=GRAD_WIRE)
    o1, o2 = LORA_PAD[0], LORA_PAD[0] + LORA_PAD[1]
    G["inT"] = jnp.concatenate([dW_rkv, dW_lora[:LORA_W[0]], dW_lora[o1:o1 + LORA_W[1]], dW_lora[o2:o2 + LORA_W[2]],
                                dW_qkv, dW_gate], axis=0)
    dx1, dx1_b, Gs["mix_norm"] = _rowwise("mix_drms", functools.partial(_f_rms_bwd, 4), [*dh, x1, dx2],
                                          [P["mix_norm"]], [(D, f32), (D, bf16)], [(1, D)])
    part_mid = hooks["reduce_mid"](G) if "reduce_mid" in hooks else None
    dx0, _, Gs["ffn1_norm"], G["f1_iT"], G["f1_o"], recv_mid = _ffn_bwd(
        "f1", dx1, dx1_b, x0, n1, u1, act1, P["ffn1_norm"], W["f1_iT"], W["f1_o"], cross=part_mid)
    G["mid"] = (part_mid, recv_mid)
    return loss[0, 0], dx0, G, Gs


def _peer(k):
    x, y, c = lax.axis_index("x"), lax.axis_index("y"), lax.axis_index("c")
    px = 1 - x if k & 4 else x
    py = 1 - y if k & 2 else y
    pc = 1 - c if k & 1 else c
    return (px, py, pc), 4 * px + 2 * py + pc


def _gather_phases(x_ref, out_ref, send_sems, recv_sems, local_sem):
    x, y, c = lax.axis_index("x"), lax.axis_index("y"), lax.axis_index("c")
    me, sibling = (x, y, c), (x, y, 1 - c)
    chips = [(1 - x, y), (x, 1 - y), (1 - x, 1 - y)]

    def slot(px, py, pc):
        return out_ref.at[4 * px + 2 * py + pc]

    def copy(k, block, to, src=None):
        return pltpu.make_async_remote_copy(
            src_ref=slot(*block) if src is None else src, dst_ref=slot(*block), send_sem=send_sems.at[k],
            recv_sem=recv_sems.at[k], device_id=to, device_id_type=MESH)

    def mine():
        return pltpu.make_async_copy(x_ref, slot(*me), local_sem)

    def first():
        return [copy(0, me, sibling, src=x_ref)] + [copy(1 + j, me, (*chip, c), src=x_ref) for j, chip in enumerate(chips)]

    def passed():
        return [copy(4 + j, (*chip, c), sibling) for j, chip in enumerate(chips)]

    def start():
        mine().start()
        for cp in first():
            cp.start()

    def forward():
        for j, (chip, cp) in enumerate(zip(chips, passed())):
            copy(1 + j, (*chip, c), me).wait_recv()
            cp.start()

    def finish():
        copy(0, sibling, me).wait_recv()
        for j, chip in enumerate(chips):
            copy(4 + j, (*chip, 1 - c), me).wait_recv()
        for cp in first() + passed():
            cp.wait_send()
        mine().wait()

    return start, forward, finish


GATHER_SEMS = [pltpu.SemaphoreType.DMA((N_DEV - 1,)), pltpu.SemaphoreType.DMA((N_DEV - 1,)), pltpu.SemaphoreType.DMA(())]


def _all_gather(pack):
    R, C = pack.shape

    def body(x_ref, out_ref, send_sems, recv_sems, local_sem):
        for phase in _gather_phases(x_ref, out_ref, send_sems, recv_sems, local_sem):
            phase()

    return pl.pallas_call(
        body, name="weight_all_gather", out_shape=jax.ShapeDtypeStruct((N_DEV, R, C), pack.dtype),
        in_specs=[pl.BlockSpec(memory_space=pl.ANY)], out_specs=pl.BlockSpec(memory_space=pl.ANY),
        scratch_shapes=GATHER_SEMS,
    )(pack)


def _cross_phases(p_ref, out_ref, send_sems, recv_sems):
    x, y, c = lax.axis_index("x"), lax.axis_index("y"), lax.axis_index("c")

    def copies():
        out = []
        for j, (fx, fy) in enumerate([(1, 0), (0, 1), (1, 1)]):
            px = 1 - x if fx else x
            py = 1 - y if fy else y
            out.append(pltpu.make_async_remote_copy(src_ref=p_ref.at[2 * px + py], dst_ref=out_ref.at[j],
                                                    send_sem=send_sems.at[j], recv_sem=recv_sems.at[j],
                                                    device_id=(px, py, c), device_id_type=MESH))
        return out

    def start():
        for cp in copies():
            cp.start()

    def finish():
        for cp in copies():
            cp.wait()

    return start, finish


CROSS_SEMS = [pltpu.SemaphoreType.DMA((3,)), pltpu.SemaphoreType.DMA((3,))]


def _direct_phases(piece_refs, rows, out_ref, send_sems, recv_sems):
    offs = [sum(rows[:i]) for i in range(len(rows))]

    def copies():
        out = []
        for i, g_ref in enumerate(piece_refs):
            for k in range(1, N_DEV):
                dev, idx = _peer(k)
                out.append(pltpu.make_async_remote_copy(
                    src_ref=g_ref.at[idx], dst_ref=out_ref.at[k - 1, pl.ds(offs[i], rows[i])],
                    send_sem=send_sems.at[i * (N_DEV - 1) + k - 1], recv_sem=recv_sems.at[i * (N_DEV - 1) + k - 1],
                    device_id=dev, device_id_type=MESH))
        return out

    def start():
        for cp in copies():
            cp.start()

    def finish():
        for cp in copies():
            cp.wait()

    return start, finish


def _sum_direct(pieces, recv, me, tag):
    n = len(pieces)
    C = pieces[0].shape[2]
    nblk = [p.shape[1] // PACK_BLOCK for p in pieces]
    lo = [sum(nblk[:i]) for i in range(n)]
    R = sum(nblk) * PACK_BLOCK

    def body(me_ref, *refs):
        g_refs, r_ref, o_ref = refs[:n], refs[n], refs[n + 1]
        rb = pl.program_id(0)
        for i in range(n):
            @pl.when(jnp.logical_and(rb >= lo[i], rb < lo[i] + nblk[i]))
            def _(g_ref=g_refs[i]):
                acc = g_ref[...].astype(f32)
                for k in range(N_DEV - 1):
                    acc = acc + r_ref[k].astype(f32)
                o_ref[...] = acc

    def piece_spec(i):
        return pl.BlockSpec((None, PACK_BLOCK, C), lambda rb, me_ref: (me_ref[0], jnp.clip(rb - lo[i], 0, nblk[i] - 1), 0))

    return pl.pallas_call(
        body, name=f"grad_sum_{tag}",
        grid_spec=pltpu.PrefetchScalarGridSpec(
            num_scalar_prefetch=1, grid=(R // PACK_BLOCK,),
            in_specs=[piece_spec(i) for i in range(n)] + [pl.BlockSpec((N_DEV - 1, PACK_BLOCK, C), lambda rb, me_ref: (0, rb, 0))],
            out_specs=pl.BlockSpec((PACK_BLOCK, C), lambda rb, me_ref: (rb, 0))),
        out_shape=jax.ShapeDtypeStruct((R, C), f32),
        compiler_params=_cparams(("arbitrary",)),
    )(me, *pieces, recv)


N_CHIP = 4


def _grad_pair(pieces, tag):
    n = len(pieces)
    C = pieces[0].shape[2]
    rows = [p.shape[1] for p in pieces]
    offs = [sum(rows[:i]) for i in range(n)]
    R = sum(rows)

    def body(*refs):
        g_refs, (other_ref, send_sems, recv_sems) = refs[:n], refs[n:]
        x, y, c = lax.axis_index("x"), lax.axis_index("y"), lax.axis_index("c")
        copies = []
        for i, g_ref in enumerate(g_refs):
            for k in range(N_CHIP):
                cp = pltpu.make_async_remote_copy(
                    src_ref=g_ref.at[4 * (k // 2) + 2 * (k % 2) + 1 - c], dst_ref=other_ref.at[k, pl.ds(offs[i], rows[i])],
                    send_sem=send_sems.at[i * N_CHIP + k], recv_sem=recv_sems.at[i * N_CHIP + k],
                    device_id=(x, y, 1 - c), device_id_type=MESH)
                cp.start()
                copies.append(cp)
        for cp in copies:
            cp.wait()

    return pl.pallas_call(
        body, name=f"grad_pair_{tag}", out_shape=jax.ShapeDtypeStruct((N_CHIP, R, C), pieces[0].dtype),
        in_specs=[pl.BlockSpec(memory_space=pl.ANY)] * n, out_specs=pl.BlockSpec(memory_space=pl.ANY),
        scratch_shapes=[pltpu.SemaphoreType.DMA((n * N_CHIP,))] * 2,
    )(*pieces)


def _pair_add(pieces, other, c, tag):
    n = len(pieces)
    C = pieces[0].shape[2]
    nblk = [p.shape[1] // PACK_BLOCK for p in pieces]
    lo = [sum(nblk[:i]) for i in range(n)]
    R = sum(nblk) * PACK_BLOCK

    def body(c_ref, *refs):
        g_refs, o_ref, out_ref = refs[:n], refs[n], refs[n + 1]
        rb = pl.program_id(1)
        for i in range(n):
            @pl.when(jnp.logical_and(rb >= lo[i], rb < lo[i] + nblk[i]))
            def _(g_ref=g_refs[i]):
                out_ref[...] = (g_ref[...].astype(f32) + o_ref[...].astype(f32)).astype(out_ref.dtype)

    def piece_spec(i):
        return pl.BlockSpec((1, None, PACK_BLOCK, C),
                            lambda k, rb, c_ref: (k, c_ref[0], jnp.clip(rb - lo[i], 0, nblk[i] - 1), 0))

    blk = pl.BlockSpec((1, PACK_BLOCK, C), lambda k, rb, c_ref: (k, rb, 0))
    return pl.pallas_call(
        body, name=f"pair_add_{tag}",
        grid_spec=pltpu.PrefetchScalarGridSpec(
            num_scalar_prefetch=1, grid=(N_CHIP, R // PACK_BLOCK),
            in_specs=[piece_spec(i) for i in range(n)] + [blk], out_specs=blk),
        out_shape=jax.ShapeDtypeStruct((N_CHIP, R, C), other.dtype),
        compiler_params=_cparams(("arbitrary", "arbitrary")),
    )(c, *[p.reshape(N_CHIP, 2, p.shape[1], C) for p in pieces], other)


def _grad_cross(part):
    _, R, C = part.shape

    def body(p_ref, out_ref, send_sems, recv_sems):
        for phase in _cross_phases(p_ref, out_ref, send_sems, recv_sems):
            phase()

    return pl.pallas_call(
        body, name="grad_cross", out_shape=jax.ShapeDtypeStruct((3, R, C), part.dtype),
        in_specs=[pl.BlockSpec(memory_space=pl.ANY)], out_specs=pl.BlockSpec(memory_space=pl.ANY),
        scratch_shapes=CROSS_SEMS,
    )(part)


def _grad_sum(part, recv, my_chip, tr, tag):
    _, R, C = part.shape

    def body(chip_ref, p_ref, r_ref, o_ref):
        acc = p_ref[0].astype(f32)
        for j in range(3):
            acc = acc + r_ref[j].astype(f32)
        o_ref[...] = acc

    return pl.pallas_call(
        body, name=f"grad_sum_{tag}",
        grid_spec=pltpu.PrefetchScalarGridSpec(
            num_scalar_prefetch=1, grid=(R // tr,),
            in_specs=[pl.BlockSpec((1, tr, C), lambda i, chip_ref: (chip_ref[0], i, 0)),
                      pl.BlockSpec((3, tr, C), lambda i, chip_ref: (0, i, 0))],
            out_specs=pl.BlockSpec((tr, C), lambda i, chip_ref: (i, 0))),
        out_shape=jax.ShapeDtypeStruct((R, C), f32),
        compiler_params=_cparams(("arbitrary",)),
    )(my_chip, part, recv)


def _small_all_reduce(small):
    R, C = small.shape

    def body(x_ref, o_ref, buf, send_sems, recv_sems):
        _, me = _peer(0)
        buf[me] = x_ref[...]
        sends = []
        for k in range(1, N_DEV):
            dev, _ = _peer(k)
            cp = pltpu.make_async_remote_copy(src_ref=x_ref, dst_ref=buf.at[me], send_sem=send_sems.at[k - 1],
                                              recv_sem=recv_sems.at[k - 1], device_id=dev, device_id_type=MESH)
            cp.start()
            sends.append(cp)
        for k in range(1, N_DEV):
            dev, idx = _peer(k)
            pltpu.make_async_remote_copy(src_ref=x_ref, dst_ref=buf.at[idx], send_sem=send_sems.at[k - 1],
                                         recv_sem=recv_sems.at[k - 1], device_id=dev, device_id_type=MESH).wait_recv()
        for cp in sends:
            cp.wait_send()
        acc = buf[0]
        for i in range(1, N_DEV):
            acc = acc + buf[i]
        o_ref[...] = acc

    return pl.pallas_call(
        body, name="small_all_reduce", out_shape=jax.ShapeDtypeStruct((R, C), f32),
        in_specs=[pl.BlockSpec(memory_space=pltpu.VMEM)], out_specs=pl.BlockSpec(memory_space=pltpu.VMEM),
        scratch_shapes=[pltpu.VMEM((N_DEV, R, C), f32), pltpu.SemaphoreType.DMA((N_DEV - 1,)),
                        pltpu.SemaphoreType.DMA((N_DEV - 1,))],
    )(small)


_LORA = (("rwkv_w2", True), ("rwkv_a2", True), ("rwkv_g2", True))
_GROUPS_FIRST = ((("ffn1_w_in", True),),)
_GROUPS_MID = ((("ffn1_w_out", False),), _LORA)
_GROUPS_IN = ((("w_in", True),),)
_GROUPS_LATE = ((("w_proj_rwkv", False),), (("w_proj_attn", True),), (("w_out", False),),
                (("ffn2_w_in", True),), (("ffn2_w_out", False),))
_GRADS_MID = ((("w_in", True),), _LORA)
_GRADS_LAST = ((("ffn1_w_in", True),), (("ffn1_w_out", False),))
_BIG = tuple(item for group in _GROUPS_FIRST + _GROUPS_MID + _GROUPS_IN + _GROUPS_LATE for item in group)
_SMALL = ("ffn1_norm", "mix_norm", "b_gate", "rwkv_mu", "rwkv_w0", "rwkv_a0", "rwkv_k_k", "rwkv_k_a", "rwkv_r_k",
          "rwkv_ln_w", "rwkv_ln_b", "attn_q_norm", "attn_k_norm", "ffn2_norm")


def _pack_layout(like, groups):
    items, spans, off = {}, [], 0
    for group in groups:
        start = off
        for name, _ in group:
            shp = like[name].shape
            n = shp[0] * shp[1] // D
            items[name] = (off, n)
            off += n
        off = -(-off // PACK_BLOCK) * PACK_BLOCK
        spans.append((start, off - start))
    return items, spans, off


def _pack_big(shards, groups):
    items, _, rows = _pack_layout(shards, groups)
    parts, at = [], 0
    for group in groups:
        for name, tr in group:
            off, n = items[name]
            t = shards[name]
            if off > at:
                parts.append(jnp.zeros((off - at, D), t.dtype))
            parts.append((t.T if tr else t).reshape(n, D))
            at = off + n
    if rows > at:
        parts.append(jnp.zeros((rows - at, D), parts[0].dtype))
    return jnp.concatenate(parts, axis=0)


def _unpack_big(pack, like, groups):
    items, _, _ = _pack_layout(like, groups)
    out = {}
    for group in groups:
        for name, tr in group:
            off, n = items[name]
            shp = like[name].shape
            t = pack[off:off + n]
            out[name] = t.reshape(shp[1], shp[0]).T if tr else t.reshape(shp)
    return out


def _unpack_gathered(gathered, like, groups):
    items, _, _ = _pack_layout(like, groups)
    full = {}
    for group in groups:
        for name, tr in group:
            shp = like[name].shape
            off, rows = items[name]
            r_loc, c_loc = (shp[1], shp[0]) if tr else shp
            full[name] = gathered[:, off:off + rows].reshape(N_DEV * r_loc, c_loc)
    return full


def _grad_pieces(g_full, like, groups):
    items, spans, _ = _pack_layout(like, groups)
    pieces = []
    for group, (_, rows_pad) in zip(groups, spans):
        parts = [g_full[n].astype(GRAD_WIRE).reshape(N_DEV, items[n][1], D) for n, _ in group]
        piece = parts[0] if len(parts) == 1 else jnp.concatenate(parts, axis=1)
        if rows_pad > piece.shape[1]:
            piece = jnp.pad(piece, ((0, 0), (0, rows_pad - piece.shape[1]), (0, 0)))
        pieces.append(piece)
    return pieces


def _small_rows(name, t):
    flat = t.reshape(-1)
    pad = (-flat.shape[0]) % D
    return jnp.pad(flat, (0, pad)).reshape(-1, D)


def _pack_small(vals):
    parts = [_small_rows(n, vals[n]) for n in _SMALL]
    used = sum(p.shape[0] for p in parts)
    parts.append(jnp.zeros((SMALL_ROWS - used, D), f32))
    return jnp.concatenate(parts, axis=0)


def _unpack_small(pack, like):
    out, off = {}, 0
    for n in _SMALL:
        size = like[n].size
        rows = -(-size // D)
        out[n] = pack[off:off + rows].reshape(-1)[:size].reshape(like[n].shape)
        off += rows
    return out


def _build_W_mid(full):
    dt = full["rwkv_w2"].dtype
    z64, z96 = jnp.zeros((64, D), dt), jnp.zeros((96, D), dt)
    return {
        "f1_o": full["ffn1_w_out"],
        "w2p": jnp.concatenate([full["rwkv_w2"].T, z64], axis=0),
        "a2p": jnp.concatenate([full["rwkv_a2"].T, z64], axis=0),
        "g2p": jnp.concatenate([full["rwkv_g2"].T, z96], axis=0),
    }


def _build_W_in(full):
    inT = full["w_in"]
    z64, z96 = jnp.zeros((64, D), inT.dtype), jnp.zeros((96, D), inT.dtype)
    return {
        "in_rkvT": inT[:3 * D],
        "in_loraT": jnp.concatenate([inT[3072:3136], z64, inT[3136:3200], z64, inT[3200:3360], z96], axis=0),
        "in_qkvT": inT[3360:3360 + 3 * ATTN_W], "in_gateT": inT[3360 + 3 * ATTN_W:],
    }


def _build_W_late(full):
    return {"pr": full["w_proj_rwkv"], "paT": full["w_proj_attn"], "out": full["w_out"],
            "f2_iT": full["ffn2_w_in"], "f2_o": full["ffn2_w_out"]}


def _build_W_first(full):
    return {"f1_iT": full["ffn1_w_in"]}


def _build_W(full):
    return {**_build_W_first(full), **_build_W_mid(full), **_build_W_in(full), **_build_W_late(full)}


_G_NAMES = {"ffn1_w_in": "f1_iT", "ffn1_w_out": "f1_o", "w_in": "inT", "rwkv_w2": "w2T", "rwkv_a2": "a2T",
            "rwkv_g2": "g2T", "w_proj_rwkv": "pr", "w_proj_attn": "paT", "w_out": "out", "ffn2_w_in": "f2_iT",
            "ffn2_w_out": "f2_o"}


def _named_grads(G, groups):
    return {n: G[_G_NAMES[n]] for group in groups for n, _ in group}


def _reduce_start(G, like, groups, my_c, tag):
    pieces = _grad_pieces(_named_grads(G, groups), like, groups)
    return _pair_add(pieces, _grad_pair(pieces, tag), my_c, tag)


def _build_P(Wl):
    mu = Wl["rwkv_mu"]
    z64f, z96f = jnp.zeros((1, 64), f32), jnp.zeros((1, 96), f32)
    return {
        "ffn1_norm": Wl["ffn1_norm"][None], "mix_norm": Wl["mix_norm"][None], "ffn2_norm": Wl["ffn2_norm"][None],
        "b_gate": Wl["b_gate"][None], "mu_rkv": mu[None, :3 * D],
        "mu_lora": jnp.concatenate([mu[None, 3072:3136], z64f, mu[None, 3136:3200], z64f, mu[None, 3200:3360], z96f], axis=1),
        "w0": Wl["rwkv_w0"][None], "a0": Wl["rwkv_a0"][None], "k_k": Wl["rwkv_k_k"][None], "k_a": Wl["rwkv_k_a"][None],
        "r_k": Wl["rwkv_r_k"].reshape(1, D), "ln_w": Wl["rwkv_ln_w"][None], "ln_b": Wl["rwkv_ln_b"][None],
        "q_norm": Wl["attn_q_norm"][None], "k_norm": Wl["attn_k_norm"][None],
    }


def kernel(x, ffn1_norm, ffn1_w_in, ffn1_w_out, mix_norm, w_in, b_gate, rwkv_mu, rwkv_w0, rwkv_w2, rwkv_a0, rwkv_a2, rwkv_g2, rwkv_k_k, rwkv_k_a, rwkv_r_k, rwkv_ln_w, rwkv_ln_b, attn_q_norm, attn_k_norm, w_proj_rwkv, w_proj_attn, w_out, ffn2_norm, ffn2_w_in, ffn2_w_out, loss_target, m_ffn1_norm, m_ffn1_w_in, m_ffn1_w_out, m_mix_norm, m_w_in, m_b_gate, m_rwkv_mu, m_rwkv_w0, m_rwkv_w2, m_rwkv_a0, m_rwkv_a2, m_rwkv_g2, m_rwkv_k_k, m_rwkv_k_a, m_rwkv_r_k, m_rwkv_ln_w, m_rwkv_ln_b, m_attn_q_norm, m_attn_k_norm, m_w_proj_rwkv, m_w_proj_attn, m_w_out, m_ffn2_norm, m_ffn2_w_in, m_ffn2_w_out, v_ffn1_norm, v_ffn1_w_in, v_ffn1_w_out, v_mix_norm, v_w_in, v_b_gate, v_rwkv_mu, v_rwkv_w0, v_rwkv_w2, v_rwkv_a0, v_rwkv_a2, v_rwkv_g2, v_rwkv_k_k, v_rwkv_k_a, v_rwkv_r_k, v_rwkv_ln_w, v_rwkv_ln_b, v_attn_q_norm, v_attn_k_norm, v_w_proj_rwkv, v_w_proj_attn, v_w_out, v_ffn2_norm, v_ffn2_w_in, v_ffn2_w_out):
    names = ("ffn1_norm", "ffn1_w_in", "ffn1_w_out", "mix_norm", "w_in", "b_gate", "rwkv_mu", "rwkv_w0", "rwkv_w2",
             "rwkv_a0", "rwkv_a2", "rwkv_g2", "rwkv_k_k", "rwkv_k_a", "rwkv_r_k", "rwkv_ln_w", "rwkv_ln_b",
             "attn_q_norm", "attn_k_norm", "w_proj_rwkv", "w_proj_attn", "w_out", "ffn2_norm", "ffn2_w_in", "ffn2_w_out")
    w_all = (ffn1_norm, ffn1_w_in, ffn1_w_out, mix_norm, w_in, b_gate, rwkv_mu, rwkv_w0, rwkv_w2, rwkv_a0, rwkv_a2,
             rwkv_g2, rwkv_k_k, rwkv_k_a, rwkv_r_k, rwkv_ln_w, rwkv_ln_b, attn_q_norm, attn_k_norm, w_proj_rwkv,
             w_proj_attn, w_out, ffn2_norm, ffn2_w_in, ffn2_w_out)
    m_all = (m_ffn1_norm, m_ffn1_w_in, m_ffn1_w_out, m_mix_norm, m_w_in, m_b_gate, m_rwkv_mu, m_rwkv_w0, m_rwkv_w2,
             m_rwkv_a0, m_rwkv_a2, m_rwkv_g2, m_rwkv_k_k, m_rwkv_k_a, m_rwkv_r_k, m_rwkv_ln_w, m_rwkv_ln_b,
             m_attn_q_norm, m_attn_k_norm, m_w_proj_rwkv, m_w_proj_attn, m_w_out, m_ffn2_norm, m_ffn2_w_in, m_ffn2_w_out)
    v_all = (v_ffn1_norm, v_ffn1_w_in, v_ffn1_w_out, v_mix_norm, v_w_in, v_b_gate, v_rwkv_mu, v_rwkv_w0, v_rwkv_w2,
             v_rwkv_a0, v_rwkv_a2, v_rwkv_g2, v_rwkv_k_k, v_rwkv_k_a, v_rwkv_r_k, v_rwkv_ln_w, v_rwkv_ln_b,
             v_attn_q_norm, v_attn_k_norm, v_w_proj_rwkv, v_w_proj_attn, v_w_out, v_ffn2_norm, v_ffn2_w_in, v_ffn2_w_out)
    Wl = {n: t[0] for n, t in zip(names, w_all)}
    Ml = {n: t[0] for n, t in zip(names, m_all)}
    Vl = {n: t[0] for n, t in zip(names, v_all)}
    big = [n for n, _ in _BIG]

    my_c = lax.axis_index("c").astype(jnp.int32).reshape(1)
    my_chip = (2 * lax.axis_index("x") + lax.axis_index("y")).astype(jnp.int32).reshape(1)

    def pack(groups):
        return _pack_big(Wl, groups).astype(bf16)

    gathered = _all_gather(pack(_GROUPS_FIRST))
    W, P = _build_W_first(_unpack_gathered(gathered, Wl, _GROUPS_FIRST)), _build_P(Wl)
    hooks = {"gather_mid": (pack(_GROUPS_MID), lambda g: _build_W_mid(_unpack_gathered(g, Wl, _GROUPS_MID))),
             "gather_in": (pack(_GROUPS_IN), lambda g: _build_W_in(_unpack_gathered(g, Wl, _GROUPS_IN))),
             "gather_late": (pack(_GROUPS_LATE), lambda g: _build_W_late(_unpack_gathered(g, Wl, _GROUPS_LATE))),
             "reduce_mid": lambda G: _reduce_start(G, Wl, _GRADS_MID, my_c, "mid"),
             "reduce_late": lambda G: _grad_pieces(_named_grads(G, _GROUPS_LATE), Wl, _GROUPS_LATE)}

    loss_local, dx0, G, Gs = _local_step(x[0], loss_target[0], W, P, hooks)

    part_last = _reduce_start(G, Wl, _GRADS_LAST, my_c, "last")
    g_big = _unpack_big(_grad_sum(part_last, _grad_cross(part_last), my_chip, 128, "last"), Wl, _GRADS_LAST)
    g_big.update(_unpack_big(_grad_sum(*G["mid"], my_chip, 128, "mid"), Wl, _GRADS_MID))
    me = (4 * lax.axis_index("x") + 2 * lax.axis_index("y") + lax.axis_index("c")).astype(jnp.int32).reshape(1)
    g_big.update(_unpack_big(_sum_direct(*G["late"], me, "late"), Wl, _GROUPS_LATE))

    mu_g = Gs["mu_rkv"], Gs["mu_lora"]
    o1, o2 = LORA_PAD[0], LORA_PAD[0] + LORA_PAD[1]
    g_small_local = {
        "ffn1_norm": Gs["ffn1_norm"], "mix_norm": Gs["mix_norm"], "b_gate": Gs["b_gate"],
        "rwkv_mu": jnp.concatenate([mu_g[0], mu_g[1][:, :64], mu_g[1][:, o1:o1 + 64], mu_g[1][:, o2:o2 + 160]], axis=1),
        "rwkv_w0": Gs["w0"], "rwkv_a0": Gs["a0"], "rwkv_k_k": Gs["k_k"], "rwkv_k_a": Gs["k_a"], "rwkv_r_k": Gs["r_k"],
        "rwkv_ln_w": Gs["ln_w"], "rwkv_ln_b": Gs["ln_b"], "attn_q_norm": Gs["q_norm"], "attn_k_norm": Gs["k_norm"],
        "ffn2_norm": Gs["ffn2_norm"]}
    gs_pack = _small_all_reduce(_pack_small(g_small_local))

    out_g, out_d, out_m, out_v = dict(g_big), {}, {}, {}
    for n in big:
        cols = Wl[n].shape[1]
        out_d[n], out_m[n], out_v[n] = _rowwise(f"adamw_{n}", _f_adamw, [Wl[n], g_big[n], Ml[n], Vl[n]], [],
                                                 [(cols, f32)] * 3)
    ds_pack, ms_pack, vs_pack = _rowwise(
        "adamw_small", _f_adamw, [_pack_small(Wl), gs_pack, _pack_small(Ml), _pack_small(Vl)], [], [(D, f32)] * 3)
    for out, pack in ((out_g, gs_pack), (out_d, ds_pack), (out_m, ms_pack), (out_v, vs_pack)):
        out.update(_unpack_small(pack, Wl))

    loss = lax.psum(loss_local, ("x", "y", "c"))
    return (loss, dx0[None], *[out_g[n][None] for n in names], *[out_d[n][None] for n in names],
            *[out_m[n][None] for n in names], *[out_v[n][None] for n in names])
```

```python
import functools

import jax
import jax.numpy as jnp
from jax import lax
from jax.experimental import pallas as pl
from jax.experimental.pallas import tpu as pltpu

f32 = jnp.float32
bf16 = jnp.bfloat16
MESH = pl.DeviceIdType.MESH

N_DEV = 8
D = 1024
D_FF = 2816
HEAD = 64
RW_HEADS = 16
ATTN_PAIRS = ((128, 1), (512, 4), (2048, 16))
ATTN_BLK = 128
HEADS_PER_GROUP = 4
ATTN_W = 768
LORA_PAD = (128, 128, 256)
LORA_W = (64, 64, 160)
GN_EPS = 64e-5
RMS_EPS = 1e-6
NEG_INF = -1e30
WKV_T = 64
WKV_SUB = 2
GRAD_WIRE = bf16
PACK_BLOCK = 128
SMALL_ROWS = 24
VMEM_LIMIT = 56 * 1024 * 1024

ADAM_LR, ADAM_B1, ADAM_B2, ADAM_EPS, ADAM_WD, ADAM_STEP = 0.001, 0.9, 0.999, 1e-08, 0.01, 10


def _cparams(sem):
    return pltpu.CompilerParams(dimension_semantics=sem, vmem_limit_bytes=VMEM_LIMIT)


def _pick(n, cands):
    for c in cands:
        if n % c == 0:
            return c
    return n


HALO = 8


def _rowwise(name, fn, rows, params, outs, accs=(), tm=256, halos=(), carries=(), reverse=False):
    S = rows[0].shape[0]
    tm = min(tm, S)
    while S % tm:
        tm -= 8
    nb = S // tm
    n_in = len(rows) + len(params) + len(halos)
    n_out = len(outs)
    n_acc = len(accs)
    n_car = len(carries)

    def blk_of(i):
        return nb - 1 - i if reverse else i

    def body(*refs):
        step = pl.program_id(0)
        carry_refs = refs[n_in + n_out + n_acc:]
        if n_car:
            @pl.when(step == 0)
            def _():
                for c_ref in carry_refs:
                    c_ref[...] = jnp.zeros(c_ref.shape, f32)
        args = [r[...] for r in refs[:n_in]] + [c[...] for c in carry_refs]
        res = fn(*args, blk=blk_of(step)) if (halos or carries) else fn(*args)
        if not isinstance(res, (tuple, list)):
            res = (res,)
        out_refs = refs[n_in:n_in + n_out + n_acc]
        for j in range(n_out):
            out_refs[j][...] = res[j].astype(out_refs[j].dtype)
        if n_acc:
            @pl.when(step == 0)
            def _():
                for j in range(n_acc):
                    out_refs[n_out + j][...] = jnp.zeros(out_refs[n_out + j].shape, f32)
            for j in range(n_acc):
                out_refs[n_out + j][...] += res[n_out + j]
        for j in range(n_car):
            carry_refs[j][...] = res[n_out + n_acc + j]

    in_specs = [pl.BlockSpec((tm, a.shape[1]), lambda i: (blk_of(i), 0)) for a in rows]
    in_specs += [pl.BlockSpec(p.shape, lambda i, nd=p.ndim: (0,) * nd) for p in params]
    in_specs += [pl.BlockSpec((HALO, rows[h].shape[1]), lambda i: (jnp.maximum(blk_of(i) * (tm // HALO) - 1, 0), 0))
                 for h in halos]
    out_specs = [pl.BlockSpec((tm, w), lambda i: (blk_of(i), 0)) for w, _ in outs]
    out_specs += [pl.BlockSpec(s, lambda i: (0, 0)) for s in accs]
    out_shape = [jax.ShapeDtypeStruct((S, w), dt) for w, dt in outs]
    out_shape += [jax.ShapeDtypeStruct(s, f32) for s in accs]
    res = pl.pallas_call(
        body, name=name, grid=(nb,), in_specs=in_specs, out_specs=out_specs, out_shape=out_shape,
        scratch_shapes=[pltpu.VMEM(s, f32) for s in carries],
        compiler_params=_cparams(("arbitrary",)),
    )(*rows, *params, *[rows[h] for h in halos])
    return res


MM_VMEM_BUDGET = 40 * 1024 * 1024
MM_STEP_US = 0.35
MM_FLOPS_PER_US = 9.0e8
MM_HBM_BYTES_PER_US = 3.0e6


def _tile_options(n, cap):
    opts = [d for d in range(128, min(n, cap) + 1, 128) if n % d == 0]
    return opts or [n]


def _mm_tiles(M, N, K, sa, sb, so):
    best, best_cost = None, None
    for tm in _tile_options(M, 2048):
        for tn in _tile_options(N, 2048):
            for tk in _tile_options(K, 4096):
                vmem = 2 * (tm * tk * sa + tk * tn * sb) + 2 * tm * tn * so + (tm * tn * 4 if tk < K else 0)
                if vmem > MM_VMEM_BUDGET:
                    continue
                steps = (M // tm) * (N // tn) * (K // tk)
                traffic = M * K * sa * (N // tn) + K * N * sb * (M // tm) + M * N * so
                cost = (max(2.0 * M * N * K / MM_FLOPS_PER_US, traffic / MM_HBM_BYTES_PER_US) + steps * MM_STEP_US
                        + (tm * tk * sa + tk * tn * sb) / MM_HBM_BYTES_PER_US)
                if best_cost is None or cost < best_cost:
                    best, best_cost = (tm, tn, tk), cost
    return best


def _mm(name, a, b, mode, out_dtype=f32, scale=None, gather=None, cross=None):
    if mode == "nn":
        (M, K), (_, N) = a.shape, b.shape
    elif mode == "nt":
        (M, K), (N, _) = a.shape, b.shape
    else:
        (K, M), (_, N) = a.shape, b.shape
    tm, tn, tk = _mm_tiles(M, N, K, a.dtype.itemsize, b.dtype.itemsize, jnp.dtype(out_dtype).itemsize)
    nk = K // tk
    if mode == "nn":
        a_spec = pl.BlockSpec((tm, tk), lambda i, j, k: (i, k))
        b_spec = pl.BlockSpec((tk, tn), lambda i, j, k: (k, j))
        dims = (((1,), (0,)), ((), ()))
    elif mode == "nt":
        a_spec = pl.BlockSpec((tm, tk), lambda i, j, k: (i, k))
        b_spec = pl.BlockSpec((tn, tk), lambda i, j, k: (j, k))
        dims = (((1,), (1,)), ((), ()))
    else:
        a_spec = pl.BlockSpec((tk, tm), lambda i, j, k: (k, i))
        b_spec = pl.BlockSpec((tk, tn), lambda i, j, k: (k, j))
        dims = (((0,), (0,)), ((), ()))

    def finish(acc):
        return acc if scale is None else acc * scale

    hosted = gather if gather is not None else cross
    grid = (M // tm, N // tn, nk)
    steps = grid[0] * grid[1] * grid[2]

    def body(a_ref, b_ref, *rest):
        if hosted is None:
            o_ref, *scratch = rest
        else:
            src_ref, o_ref, dst_ref, *scratch = rest
            n_sem = len(GATHER_SEMS if gather is not None else CROSS_SEMS)
            sems, scratch = scratch[len(scratch) - n_sem:], scratch[:len(scratch) - n_sem]
            step = (pl.program_id(0) * grid[1] + pl.program_id(1)) * grid[2] + pl.program_id(2)
            if gather is not None:
                start, forward, done = _gather_phases(src_ref, dst_ref, *sems)
                pl.when(step == steps // 2)(forward)
            else:
                start, done = _cross_phases(src_ref, dst_ref, *sems)
            pl.when(step == 0)(start)
        part = lax.dot_general(a_ref[...].astype(bf16), b_ref[...].astype(bf16), dims,
                               preferred_element_type=f32)
        if nk == 1:
            o_ref[...] = finish(part).astype(o_ref.dtype)
        else:
            acc_ref = scratch[0]
            k = pl.program_id(2)

            @pl.when(k == 0)
            def _():
                acc_ref[...] = part

            @pl.when(k > 0)
            def _():
                acc_ref[...] += part

            @pl.when(k == nk - 1)
            def _():
                o_ref[...] = finish(acc_ref[...]).astype(o_ref.dtype)
        if hosted is not None:
            pl.when(step == steps - 1)(done)

    hbm = pl.BlockSpec(memory_space=pl.ANY)
    out_specs = [pl.BlockSpec((tm, tn), lambda i, j, k: (i, j))]
    out_shape = [jax.ShapeDtypeStruct((M, N), out_dtype)]
    scratch_shapes = [] if nk == 1 else [pltpu.VMEM((tm, tn), f32)]
    if gather is not None:
        out_specs.append(hbm)
        out_shape.append(jax.ShapeDtypeStruct((N_DEV,) + gather.shape, gather.dtype))
        scratch_shapes = scratch_shapes + GATHER_SEMS
    elif cross is not None:
        out_specs.append(hbm)
        out_shape.append(jax.ShapeDtypeStruct((3,) + cross.shape[1:], cross.dtype))
        scratch_shapes = scratch_shapes + CROSS_SEMS
    res = pl.pallas_call(
        body, name=name, grid=grid, in_specs=[a_spec, b_spec] + [hbm] * (hosted is not None),
        out_specs=out_specs, out_shape=out_shape, scratch_shapes=scratch_shapes,
        compiler_params=_cparams(("arbitrary",) * 3 if hosted is not None else ("parallel", "parallel", "arbitrary")),
    )(a, b, *([hosted] if hosted is not None else []))
    return res[0] if hosted is None else res


FFN_TM, FFN_TN = 512, 1408


def _ffn_up(name, n, WiT, gather=None):
    S = n.shape[0]
    grid = (S // FFN_TM, D_FF // FFN_TN)
    steps = grid[0] * grid[1]

    def body(n_ref, wg_ref, wu_ref, *rest):
        if gather is None:
            g_ref, u_ref, act_ref = rest
        else:
            src_ref, g_ref, u_ref, act_ref, dst_ref, *sems = rest
            step = pl.program_id(0) * grid[1] + pl.program_id(1)
            start, forward, done = _gather_phases(src_ref, dst_ref, *sems)
            pl.when(step == 0)(start)
            pl.when(step == steps // 2)(forward)
        x = n_ref[...]
        gate = lax.dot_general(x, wg_ref[...], _NT, preferred_element_type=f32)
        up = lax.dot_general(x, wu_ref[...], _NT, preferred_element_type=f32)
        g_ref[...] = gate
        u_ref[...] = up
        act_ref[...] = (gate * jax.nn.sigmoid(gate) * up).astype(act_ref.dtype)
        if gather is not None:
            pl.when(step == steps - 1)(done)

    hbm = pl.BlockSpec(memory_space=pl.ANY)
    tile = pl.BlockSpec((FFN_TM, FFN_TN), lambda i, j: (i, j))
    in_specs = [pl.BlockSpec((FFN_TM, D), lambda i, j: (i, 0)), pl.BlockSpec((FFN_TN, D), lambda i, j: (j, 0)),
                pl.BlockSpec((FFN_TN, D), lambda i, j: (j + D_FF // FFN_TN, 0))]
    out_specs = [tile, tile, tile]
    out_shape = [jax.ShapeDtypeStruct((S, D_FF), f32), jax.ShapeDtypeStruct((S, D_FF), f32),
                 jax.ShapeDtypeStruct((S, D_FF), bf16)]
    if gather is not None:
        in_specs.append(hbm)
        out_specs.append(hbm)
        out_shape.append(jax.ShapeDtypeStruct((N_DEV,) + gather.shape, gather.dtype))
    return pl.pallas_call(
        body, name=name, grid=grid, in_specs=in_specs, out_specs=out_specs, out_shape=out_shape,
        scratch_shapes=GATHER_SEMS if gather is not None else [],
        compiler_params=_cparams(("arbitrary", "arbitrary")),
    )(n, WiT, WiT, *([gather] if gather is not None else []))


def _ffn_dact(name, dy, Wo, gate, up):
    S = dy.shape[0]

    def body(dy_ref, wo_ref, g_ref, u_ref, dg_ref, du_ref):
        dact = 0.5 * lax.dot_general(dy_ref[...], wo_ref[...], _NT, preferred_element_type=f32)
        gate, up = g_ref[...], u_ref[...]
        sg = jax.nn.sigmoid(gate)
        dg_ref[...] = (dact * up * (sg * (1.0 + gate * (1.0 - sg)))).astype(dg_ref.dtype)
        du_ref[...] = (dact * gate * sg).astype(du_ref.dtype)

    tile = pl.BlockSpec((FFN_TM, FFN_TN), lambda i, j: (i, j))
    shp = jax.ShapeDtypeStruct((S, D_FF), bf16)
    return pl.pallas_call(
        body, name=name, grid=(S // FFN_TM, D_FF // FFN_TN),
        in_specs=[pl.BlockSpec((FFN_TM, D), lambda i, j: (i, 0)), pl.BlockSpec((FFN_TN, D), lambda i, j: (j, 0)), tile, tile],
        out_specs=[tile, tile], out_shape=[shp, shp],
        compiler_params=_cparams(("parallel", "parallel")),
    )(dy, Wo, gate, up)


def _sp(x):
    hi = x.astype(bf16)
    return hi, (x - hi.astype(f32)).astype(bf16)


def _cat(parts):
    return tuple(jnp.concatenate(p, axis=1) for p in zip(*parts))


def _bmm(eq, a, b):
    (ah, al), (bh, bl) = a, b
    dot = functools.partial(jnp.einsum, eq, preferred_element_type=f32)
    return dot(ah, bh) + (dot(ah, bl) + dot(al, bh))


def _tri_dot(eq, tri, x):
    h1 = x.astype(bf16)
    r1 = x - h1.astype(f32)
    h2 = r1.astype(bf16)
    h3 = (r1 - h2.astype(f32)).astype(bf16)
    dot = functools.partial(jnp.einsum, eq, preferred_element_type=f32)
    return dot(tri, h1) + (dot(tri, h2) + dot(tri, h3))


def _tri_masks(T):
    ti = lax.broadcasted_iota(jnp.int32, (T, T), 0)
    si = lax.broadcasted_iota(jnp.int32, (T, T), 1)
    return ti >= si, ti > si


def _wkv_prep(r, lw, k, kkr, a):
    H, T, _ = r.shape
    low_i, low_s = _tri_masks(T)
    nrm = jnp.sqrt(jnp.sum(kkr * kkr, axis=-1, keepdims=True))
    den = jnp.maximum(nrm, 1e-12)
    kk = kkr / den
    tri = jnp.broadcast_to(low_i.astype(bf16)[None], (H, T, T))
    cl = _tri_dot("hts,hsn->htn", tri, lw)
    c = jnp.exp(cl)
    cprev = jnp.exp(cl - lw)
    cinv = jnp.exp(-cl)
    bt, kt = _sp(kk * a * cinv), _sp(k * cinv)
    L = _cat([_sp(r * c), _sp(-kk * cprev)])
    Mb = _bmm("htn,hsn->hts", L, bt)
    Mk = _bmm("htn,hsn->hts", L, kt)
    A_rb = jnp.where(low_i[None], Mb[:, :T], 0.0)
    A_ab = jnp.where(low_s[None], Mb[:, T:], 0.0)
    Mk = jnp.concatenate([jnp.where(low_i[None], Mk[:, :T], 0.0), jnp.where(low_s[None], Mk[:, T:], 0.0)], axis=1)
    return dict(kk=kk, den=den, nrm=nrm, c=c, cprev=cprev, cinv=cinv, L=L, kt=kt, bt=bt,
                A_ab=A_ab, A_rb=A_rb, Mk=Mk, cT=c[:, T - 1:T, :])


def _tri_inverse(A):
    T = A.shape[-1]
    eye = (lax.broadcasted_iota(jnp.int32, (T, T), 0) == lax.broadcasted_iota(jnp.int32, (T, T), 1)).astype(f32)
    inv = eye[None] + A
    X = A
    n = 1
    while 2 * n < T:
        Xs = _sp(X)
        X = _bmm("hts,hsu->htu", Xs, Xs)
        inv = inv + _bmm("hts,hsu->htu", _sp(inv), _sp(X))
        n *= 2
    return inv


def _wkv_chunk_fwd(S0, r, lw, k, v, kkr, a):
    T = r.shape[1]
    q = _wkv_prep(r, lw, k, kkr, a)
    inv = _tri_inverse(q["A_ab"])
    vs = _sp(v)
    P = _bmm("htk,hvk->htv", q["L"], _sp(S0)) + _bmm("hts,hsv->htv", _sp(q["Mk"]), vs)
    U = _bmm("hts,hsv->htv", _sp(inv), _sp(P[:, T:]))
    Us = _sp(U)
    Y = P[:, :T] + _bmm("hts,hsv->htv", _sp(q["A_rb"]), Us)
    S1 = (S0 + _bmm("htv,htk->hvk", _cat([Us, vs]), _cat([q["bt"], q["kt"]]))) * q["cT"]
    return Y, U, inv, S1


def _wkv_chunk_bwd(S0, Hin, Q, r, lw, k, v, kkr, a, U, inv, dY):
    H, T, _ = r.shape
    low_i, low_s = _tri_masks(T)
    q = _wkv_prep(r, lw, k, kkr, a)
    L, kt, bt = q["L"], q["kt"], q["bt"]
    R = _cat([bt, kt])
    Hh = Hin * q["cT"]
    Hs, S0s, dYs, vs, Us = _sp(Hh), _sp(S0), _sp(dY), _sp(v), _sp(U)
    RH = _bmm("htk,hvk->htv", R, Hs)
    Z = _bmm("hst,hsv->htv", _sp(inv), _sp(RH[:, :T] + _bmm("hst,hsv->htv", _sp(q["A_rb"]), dYs)))
    DZ = _cat([dYs, _sp(Z)])
    both = jnp.concatenate([jnp.broadcast_to(low_i[None], (1, T, T)), jnp.broadcast_to(low_s[None], (1, T, T))], axis=1)
    NU = _sp(jnp.where(both, _bmm("htv,hsv->hts", DZ, Us), 0.0))
    NV = _sp(jnp.where(both, _bmm("htv,hsv->hts", DZ, vs), 0.0))
    ra = _bmm("htv,hvk->htk", DZ, S0s) + _bmm("hts,hsk->htk", NU, bt) + _bmm("hts,hsk->htk", NV, kt)
    dr = ra[:, :T] * q["c"]
    da = ra[:, T:] * q["cprev"]
    dv = RH[:, T:] + _bmm("hst,hsv->htv", _sp(q["Mk"]), DZ)
    VH = _bmm("htv,hvk->htk", _cat([vs, Us]), Hs)
    dk = (VH[:, :T] + _bmm("hst,hsk->htk", NV, L)) * q["cinv"]
    db = (VH[:, T:] + _bmm("hst,hsk->htk", NU, L)) * q["cinv"]
    H0 = Hh + _bmm("htv,htk->hvk", DZ, L)
    kk = q["kk"]
    e = r * dr - kk * a * db - k * dk
    f = -kk * da
    tri_i = jnp.broadcast_to(low_i.astype(bf16)[None], (H, T, T))
    tri_s = jnp.broadcast_to(low_s.astype(bf16)[None], (H, T, T))
    dlw = _tri_dot("hst,hsn->htn", tri_i, e) + _tri_dot("hst,hsn->htn", tri_s, f) + Q
    Qn = Q + jnp.sum(e + f, axis=1, keepdims=True)
    dkk = db * a - da
    dasig = db * kk
    proj = jnp.sum(dkk * kk, axis=-1, keepdims=True)
    dkkr = jnp.where(q["nrm"] > 1e-12, dkk - kk * proj, dkk) / q["den"]
    return dr, dlw, dk, dv, dkkr, dasig, H0, Qn


def _heads(ref, rows=slice(None)):
    return jnp.stack([ref[rows, h * HEAD:(h + 1) * HEAD] for h in range(RW_HEADS)], axis=0)


def _put_heads(ref, val, rows=slice(None)):
    for h in range(RW_HEADS):
        ref[rows, h * HEAD:(h + 1) * HEAD] = val[h]


def _wkv_fwd(r, lw, k, v, kkr, a, g, r_k, ln_w, ln_b, late_pack=None):
    S = r.shape[0]
    H, N, T = RW_HEADS, HEAD, WKV_T
    TS = T * WKV_SUB
    nc = S // TS
    hosting = late_pack is not None

    def body(r_ref, lw_ref, k_ref, v_ref, kkr_ref, a_ref, g_ref, rk_ref, lnw_ref, lnb_ref, *rest):
        if hosting:
            pack_ref, y_ref, yg_ref, wkv_ref, u_ref, inv_ref, s0_ref, gathered_ref, state, *sems = rest
            start, forward, finish = _gather_phases(pack_ref, gathered_ref, *sems)
            pl.when(pl.program_id(0) == 0)(start)
            pl.when(pl.program_id(0) == nc // 2)(forward)
        else:
            y_ref, yg_ref, wkv_ref, u_ref, inv_ref, s0_ref, state = rest

        @pl.when(pl.program_id(0) == 0)
        def _():
            state[...] = jnp.zeros(state.shape, f32)

        S0 = state[...]
        for c in range(WKV_SUB):
            rows = slice(c * T, (c + 1) * T)
            s0_ref[c] = S0
            rr, kk2, vv = _heads(r_ref, rows), _heads(k_ref, rows), _heads(v_ref, rows)
            Y, U, inv, S0 = _wkv_chunk_fwd(S0, rr, _heads(lw_ref, rows), kk2, vv, _heads(kkr_ref, rows),
                                           _heads(a_ref, rows))
            wkv_ref[:, rows, :] = Y
            u_ref[:, rows, :] = U
            inv_ref[:, rows, :] = inv
            mean = jnp.mean(Y, axis=-1, keepdims=True)
            var = jnp.mean(jnp.square(Y - mean), axis=-1, keepdims=True)
            yn = (Y - mean) * lax.rsqrt(var + GN_EPS)
            bonus = jnp.sum(rr * kk2 * rk_ref[...], axis=-1, keepdims=True) * vv
            _put_heads(y_ref, yn * lnw_ref[...] + lnb_ref[...] + bonus, rows)
        state[...] = S0
        yg_ref[...] = (y_ref[...] * g_ref[...]).astype(yg_ref.dtype)
        if hosting:
            pl.when(pl.program_id(0) == nc - 1)(finish)

    tok = pl.BlockSpec((TS, H * N), lambda i: (i, 0))
    blk = pl.BlockSpec((H, TS, N), lambda i: (0, i, 0))
    par = pl.BlockSpec((H, 1, N), lambda i: (0, 0, 0))
    hbm = pl.BlockSpec(memory_space=pl.ANY)
    seq = jax.ShapeDtypeStruct((H, S, N), f32)
    out_specs = [tok, tok, blk, blk, blk, pl.BlockSpec((WKV_SUB, H, N, N), lambda i: (i, 0, 0, 0))]
    out_shape = [jax.ShapeDtypeStruct((S, H * N), f32), jax.ShapeDtypeStruct((S, H * N), bf16), seq, seq, seq,
                 jax.ShapeDtypeStruct((S // T, H, N, N), f32)]
    if hosting:
        out_specs.append(hbm)
        out_shape.append(jax.ShapeDtypeStruct((N_DEV,) + late_pack.shape, late_pack.dtype))
    return pl.pallas_call(
        body, name="wkv_fwd", grid=(nc,), in_specs=[tok] * 7 + [par] * 3 + [hbm] * hosting,
        out_specs=out_specs, out_shape=out_shape,
        scratch_shapes=[pltpu.VMEM((H, N, N), f32)] + (GATHER_SEMS if hosting else []),
        compiler_params=_cparams(("arbitrary",)),
    )(r, lw, k, v, kkr, a, g, r_k, ln_w, ln_b, *([late_pack] if hosting else []))


def _wkv_bwd(dy, g, r, lw, k, v, kkr, a, wkv, U, inv, S0s, r_k, ln_w, ln_b, late_pieces=None):
    S = r.shape[0]
    H, N, T = RW_HEADS, HEAD, WKV_T
    TS = T * WKV_SUB
    nc = S // TS
    hosting = late_pieces is not None
    n_late = len(late_pieces) if hosting else 0

    def body(dy_ref, g_ref, r_ref, lw_ref, k_ref, v_ref, kkr_ref, a_ref, wkv_ref, u_ref, inv_ref, s0_ref,
             rk_ref, lnw_ref, lnb_ref, *rest):
        if hosting:
            piece_refs, rest = rest[:n_late], rest[n_late:]
            (dr_ref, dlw_ref, dk_ref, dv_ref, dkkr_ref, da_ref, drk_ref, dlnw_ref, dlnb_ref, recv_ref,
             hst, qst, *sems) = rest
            start, finish = _direct_phases(piece_refs, [p.shape[1] for p in late_pieces], recv_ref, *sems)
            pl.when(pl.program_id(0) == 0)(start)
        else:
            dr_ref, dlw_ref, dk_ref, dv_ref, dkkr_ref, da_ref, drk_ref, dlnw_ref, dlnb_ref, hst, qst = rest

        @pl.when(pl.program_id(0) == 0)
        def _():
            hst[...] = jnp.zeros(hst.shape, f32)
            qst[...] = jnp.zeros(qst.shape, f32)
            drk_ref[...] = jnp.zeros(drk_ref.shape, f32)
            dlnw_ref[...] = jnp.zeros(dlnw_ref.shape, f32)
            dlnb_ref[...] = jnp.zeros(dlnb_ref.shape, f32)

        dyg = dy_ref[...] * g_ref[...]
        rk = rk_ref[...]
        Hst, Qst = hst[...], qst[...]
        for c in reversed(range(WKV_SUB)):
            rows = slice(c * T, (c + 1) * T)
            dya = _heads(dyg, rows)
            rr, kk2, vv, Y = _heads(r_ref, rows), _heads(k_ref, rows), _heads(v_ref, rows), wkv_ref[:, rows, :]
            s = jnp.sum(rr * kk2 * rk, axis=-1, keepdims=True)
            ds = jnp.sum(dya * vv, axis=-1, keepdims=True)
            mean = jnp.mean(Y, axis=-1, keepdims=True)
            var = jnp.mean(jnp.square(Y - mean), axis=-1, keepdims=True)
            rstd = lax.rsqrt(var + GN_EPS)
            yn = (Y - mean) * rstd
            dyn = dya * lnw_ref[...]
            dY = rstd * (dyn - jnp.mean(dyn, axis=-1, keepdims=True) - yn * jnp.mean(dyn * yn, axis=-1, keepdims=True))
            drk_ref[...] += jnp.sum(ds * rr * kk2, axis=1, keepdims=True)
            dlnw_ref[...] += jnp.sum(dya * yn, axis=1, keepdims=True)
            dlnb_ref[...] += jnp.sum(dya, axis=1, keepdims=True)
            dr, dlw, dk, dv, dkkr, dasig, Hst, Qst = _wkv_chunk_bwd(
                s0_ref[c], Hst, Qst, rr, _heads(lw_ref, rows), kk2, vv, _heads(kkr_ref, rows), _heads(a_ref, rows),
                u_ref[:, rows, :], inv_ref[:, rows, :], dY)
            _put_heads(dr_ref, dr + ds * kk2 * rk, rows)
            _put_heads(dlw_ref, dlw, rows)
            _put_heads(dk_ref, dk + ds * rr * rk, rows)
            _put_heads(dv_ref, dv + dya * s, rows)
            _put_heads(dkkr_ref, dkkr, rows)
            _put_heads(da_ref, dasig, rows)
        hst[...] = Hst
        qst[...] = Qst
        if hosting:
            pl.when(pl.program_id(0) == nc - 1)(finish)

    tok = pl.BlockSpec((TS, H * N), lambda i: (nc - 1 - i, 0))
    blk = pl.BlockSpec((H, TS, N), lambda i: (0, nc - 1 - i, 0))
    par = pl.BlockSpec((H, 1, N), lambda i: (0, 0, 0))
    hbm = pl.BlockSpec(memory_space=pl.ANY)
    seq = jax.ShapeDtypeStruct((S, H * N), f32)
    pout = jax.ShapeDtypeStruct((H, 1, N), f32)
    out_specs, out_shape = [tok] * 6 + [par] * 3, [seq] * 6 + [pout] * 3
    sems = []
    if hosting:
        rows_late = sum(p.shape[1] for p in late_pieces)
        out_specs.append(hbm)
        out_shape.append(jax.ShapeDtypeStruct((N_DEV - 1, rows_late, late_pieces[0].shape[2]), late_pieces[0].dtype))
        sems = [pltpu.SemaphoreType.DMA((n_late * (N_DEV - 1),))] * 2
    return pl.pallas_call(
        body, name="wkv_bwd", grid=(nc,),
        in_specs=([tok] * 8 + [blk] * 3 + [pl.BlockSpec((WKV_SUB, H, N, N), lambda i: (nc - 1 - i, 0, 0, 0))]
                  + [par] * 3 + [hbm] * n_late),
        out_specs=out_specs, out_shape=out_shape,
        scratch_shapes=[pltpu.VMEM((H, N, N), f32), pltpu.VMEM((H, 1, N), f32)] + sems,
        compiler_params=_cparams(("arbitrary",)),
    )(dy, g, r, lw, k, v, kkr, a, wkv, U, inv, S0s, r_k, ln_w, ln_b, *(late_pieces if hosting else []))


ATTN_TT = 2048


def _attn_rows(d, i, j):
    return pl.ds(ATTN_BLK * d * i + j, ATTN_BLK, stride=d) if d > 1 else pl.ds(ATTN_BLK * i, ATTN_BLK)


def _attn_take(ref, d, nsub):
    return jnp.stack([ref[_attn_rows(d, i, j), :] for i in range(nsub) for j in range(d)], axis=0)


def _attn_put(ref, val, d):
    for i in range(val.shape[0] // d):
        for j in range(d):
            ref[_attn_rows(d, i, j), :] = val[i * d + j]


def _attn_prev(cur, before, d):
    return before if cur.shape[0] == d else jnp.concatenate([before, cur[:cur.shape[0] - d]], axis=0)


def _attn_specs(gi, d, nt, reverse):
    per_tile = ATTN_TT // (ATTN_BLK * d)

    def tile(n):
        return nt - 1 - n if reverse else n

    def col(kind):
        return lambda hp, n: (tile(n), kind * (ATTN_W // 128) + 2 * gi + hp)

    def col_before(kind):
        return lambda hp, n: (jnp.maximum(tile(n) * per_tile - 1, 0), kind * (ATTN_W // 128) + 2 * gi + hp)

    cur = [pl.BlockSpec((ATTN_TT, 128), col(kind)) for kind in range(3)]
    before = [pl.BlockSpec((ATTN_BLK * d, 128), col_before(kind)) for kind in (1, 2)]
    own = pl.BlockSpec((ATTN_TT, 128), lambda hp, n: (tile(n), hp))
    return cur, before, own, tile


def _attn_norm(x, gain, scale):
    rs = lax.rsqrt(jnp.mean(x * x, axis=-1, keepdims=True) + RMS_EPS)
    return x * rs * (gain * scale), rs


def _attn_scores(qn, kn_c, kn_p, first):
    s_c = jnp.einsum("gqe,gke->gqk", qn.astype(bf16), kn_c.astype(bf16), preferred_element_type=f32)
    s_p = jnp.einsum("gqe,gke->gqk", qn.astype(bf16), kn_p.astype(bf16), preferred_element_type=f32)
    qi = lax.broadcasted_iota(jnp.int32, (1, ATTN_BLK, ATTN_BLK), 1)
    ki = lax.broadcasted_iota(jnp.int32, (1, ATTN_BLK, ATTN_BLK), 2)
    s_c = jnp.where(qi >= ki, s_c, NEG_INF)
    s_p = jnp.where(jnp.logical_and(ki >= qi, jnp.logical_not(first)), s_p, NEG_INF)
    return s_c, s_p


def _attn_fwd(pqkv, qg, kg, gi, S):
    d = ATTN_PAIRS[gi][1]
    nt = S // ATTN_TT
    nsub = ATTN_TT // (ATTN_BLK * d)
    nd = nsub * d

    def body(q_ref, k_ref, v_ref, kb_ref, vb_ref, qg_ref, kg_ref, o_ref, lse_ref):
        Q, K, V = _attn_take(q_ref, d, nsub), _attn_take(k_ref, d, nsub), _attn_take(v_ref, d, nsub)
        KB, VB = _attn_take(kb_ref, d, 1), _attn_take(vb_ref, d, 1)
        first = jnp.logical_and(lax.broadcasted_iota(jnp.int32, (nd, 1, 1), 0) < d, pl.program_id(1) == 0)
        outs, lses = [], []
        for h in range(2):
            sl = slice(h * HEAD, (h + 1) * HEAD)
            kc, vc = K[:, :, sl], V[:, :, sl]
            kp, vp = _attn_prev(kc, KB[:, :, sl], d), _attn_prev(vc, VB[:, :, sl], d)
            qn, _ = _attn_norm(Q[:, :, sl], qg_ref[...], HEAD ** -0.5)
            kn_c, _ = _attn_norm(kc, kg_ref[...], 1.0)
            kn_p, _ = _attn_norm(kp, kg_ref[...], 1.0)
            s_c, s_p = _attn_scores(qn, kn_c, kn_p, first)
            m = jnp.maximum(jnp.max(s_c, axis=-1, keepdims=True), jnp.max(s_p, axis=-1, keepdims=True))
            p_c = jnp.exp(s_c - m)
            p_p = jnp.exp(s_p - m)
            den = jnp.sum(p_c, axis=-1, keepdims=True) + jnp.sum(p_p, axis=-1, keepdims=True)
            inv = 1.0 / den
            o = jnp.einsum("gqk,gke->gqe", (p_c * inv).astype(bf16), vc.astype(bf16), preferred_element_type=f32)
            o += jnp.einsum("gqk,gke->gqe", (p_p * inv).astype(bf16), vp.astype(bf16), preferred_element_type=f32)
            outs.append(o)
            lses.append(jnp.broadcast_to(m + jnp.log(den), o.shape))
        _attn_put(o_ref, jnp.concatenate(outs, axis=-1), d)
        _attn_put(lse_ref, jnp.concatenate(lses, axis=-1), d)

    cur, before, own, _ = _attn_specs(gi, d, nt, False)
    par = pl.BlockSpec((1, HEAD), lambda hp, n: (0, 0))
    shp = jax.ShapeDtypeStruct((S, 2 * 128), f32)
    return pl.pallas_call(
        body, name=f"attn_fwd{gi}", grid=(2, nt), in_specs=cur + before + [par] * 2, out_specs=[own, own],
        out_shape=[shp, shp], compiler_params=_cparams(("arbitrary", "arbitrary")),
    )(pqkv, pqkv, pqkv, pqkv, pqkv, qg, kg)


def _attn_bwd(pqkv, o, lse, do, dlse, qg, kg, gi, S):
    d = ATTN_PAIRS[gi][1]
    nt = S // ATTN_TT
    nsub = ATTN_TT // (ATTN_BLK * d)
    nd = nsub * d

    def norm_bwd(dxn, x, rs, gain, scale):
        xh = x * rs
        dxh = dxn * (gain * scale)
        dx = rs * (dxh - xh * jnp.mean(dxh * xh, axis=-1, keepdims=True))
        dgain = jnp.sum(jnp.sum(dxn * xh * scale, axis=1), axis=0, keepdims=True)
        return dx, dgain

    def to_before(part, carried):
        return carried if nsub == 1 else jnp.concatenate([part[d:], carried], axis=0)

    def body(q_ref, k_ref, v_ref, kb_ref, vb_ref, o_ref, lse_ref, do_ref, dlse_ref, qg_ref, kg_ref,
             dq_ref, dk_ref, dv_ref, dqg_ref, dkg_ref, carry_k, carry_v):
        step = pl.program_id(1)

        @pl.when(jnp.logical_and(pl.program_id(0) == 0, step == 0))
        def _():
            dqg_ref[...] = jnp.zeros(dqg_ref.shape, f32)
            dkg_ref[...] = jnp.zeros(dkg_ref.shape, f32)

        @pl.when(step == 0)
        def _():
            carry_k[...] = jnp.zeros(carry_k.shape, f32)
            carry_v[...] = jnp.zeros(carry_v.shape, f32)

        Q, K, V = _attn_take(q_ref, d, nsub), _attn_take(k_ref, d, nsub), _attn_take(v_ref, d, nsub)
        KB, VB = _attn_take(kb_ref, d, 1), _attn_take(vb_ref, d, 1)
        O, LSE = _attn_take(o_ref, d, nsub), _attn_take(lse_ref, d, nsub)
        DO, DLSE = _attn_take(do_ref, d, nsub), _attn_take(dlse_ref, d, nsub)
        first = jnp.logical_and(lax.broadcasted_iota(jnp.int32, (nd, 1, 1), 0) < d, step == nt - 1)
        qg, kg = qg_ref[...], kg_ref[...]
        dqs, dks, dvs = [], [], []
        for h in range(2):
            sl = slice(h * HEAD, (h + 1) * HEAD)
            qx, kx, vc = Q[:, :, sl], K[:, :, sl], V[:, :, sl]
            kpx, vp = _attn_prev(kx, KB[:, :, sl], d), _attn_prev(vc, VB[:, :, sl], d)
            qn, rq = _attn_norm(qx, qg, HEAD ** -0.5)
            kn_c, rk_c = _attn_norm(kx, kg, 1.0)
            kn_p, _ = _attn_norm(kpx, kg, 1.0)
            s_c, s_p = _attn_scores(qn, kn_c, kn_p, first)
            lse = LSE[:, :, h * HEAD:h * HEAD + 1]
            p_c = jnp.exp(s_c - lse)
            p_p = jnp.exp(s_p - lse)
            dO = DO[:, :, sl]
            dOb = dO.astype(bf16)
            dp_c = jnp.einsum("gqe,gke->gqk", dOb, vc.astype(bf16), preferred_element_type=f32)
            dp_p = jnp.einsum("gqe,gke->gqk", dOb, vp.astype(bf16), preferred_element_type=f32)
            corr = DLSE[:, :, h * HEAD:h * HEAD + 1] - jnp.sum(dO * O[:, :, sl], axis=-1, keepdims=True)
            ds_c = (p_c * (dp_c + corr)).astype(bf16)
            ds_p = (p_p * (dp_p + corr)).astype(bf16)
            qnb = qn.astype(bf16)
            dqn = (jnp.einsum("gqk,gke->gqe", ds_c, kn_c.astype(bf16), preferred_element_type=f32)
                   + jnp.einsum("gqk,gke->gqe", ds_p, kn_p.astype(bf16), preferred_element_type=f32))
            dkn_p = jnp.einsum("gqk,gqe->gke", ds_p, qnb, preferred_element_type=f32)
            dv_p = jnp.einsum("gqk,gqe->gke", p_p.astype(bf16), dOb, preferred_element_type=f32)
            dkn = jnp.einsum("gqk,gqe->gke", ds_c, qnb, preferred_element_type=f32) + to_before(dkn_p, carry_k[h])
            dv = (jnp.einsum("gqk,gqe->gke", p_c.astype(bf16), dOb, preferred_element_type=f32)
                  + to_before(dv_p, carry_v[h]))
            carry_k[h] = dkn_p[:d]
            carry_v[h] = dv_p[:d]
            dq, dqg = norm_bwd(dqn, qx, rq, qg, HEAD ** -0.5)
            dk, dkg = norm_bwd(dkn, kx, rk_c, kg, 1.0)
            dqg_ref[...] += dqg
            dkg_ref[...] += dkg
            dqs.append(dq)
            dks.append(dk)
            dvs.append(dv)
        _attn_put(dq_ref, jnp.concatenate(dqs, axis=-1), d)
        _attn_put(dk_ref, jnp.concatenate(dks, axis=-1), d)
        _attn_put(dv_ref, jnp.concatenate(dvs, axis=-1), d)

    cur, before, own, _ = _attn_specs(gi, d, nt, True)
    par = pl.BlockSpec((1, HEAD), lambda hp, n: (0, 0))
    shp = jax.ShapeDtypeStruct((S, 2 * 128), f32)
    pshp = jax.ShapeDtypeStruct((1, HEAD), f32)
    return pl.pallas_call(
        body, name=f"attn_bwd{gi}", grid=(2, nt), in_specs=cur + before + [own] * 4 + [par] * 2,
        out_specs=[own] * 3 + [par] * 2, out_shape=[shp] * 3 + [pshp] * 2,
        scratch_shapes=[pltpu.VMEM((2, d, ATTN_BLK, HEAD), f32)] * 2,
        compiler_params=_cparams(("arbitrary", "arbitrary")),
    )(pqkv, pqkv, pqkv, pqkv, pqkv, o, lse, do, dlse, qg, kg)


def _rms(x, g):
    rs = lax.rsqrt(jnp.mean(x * x, axis=-1, keepdims=True) + RMS_EPS)
    return x * rs * g


def _f_rms(x, g):
    return _rms(x, g)


def _f_resid_rms(coef, x, f, g):
    xn = x + coef * f
    return xn, _rms(xn, g)


def _f_rms_bwd(n_parts, *args):
    dns = args[:n_parts]
    x, dres, g = args[n_parts:]
    dn = dns[0]
    for t in dns[1:]:
        dn = dn + t
    rs = lax.rsqrt(jnp.mean(x * x, axis=-1, keepdims=True) + RMS_EPS)
    xh = x * rs
    dxh = dn * g
    dx = dres + rs * (dxh - xh * jnp.mean(dxh * xh, axis=-1, keepdims=True))
    return dx, dx, jnp.sum(dn * xh, axis=0, keepdims=True)


def _f_loss(x, f, tgt):
    y = x + 0.5 * f
    diff = y - tgt
    part = 0.5 * jnp.sum(jnp.mean(diff * diff, axis=-1, keepdims=True), axis=0, keepdims=True)
    dy = diff * (1.0 / D)
    return dy, dy, jnp.broadcast_to(part, (1, 128))


def _dotb(a, b, dims):
    return lax.dot_general(a.astype(bf16), b.astype(bf16), dims, preferred_element_type=f32)


_NN = (((1,), (0,)), ((), ()))
_NT = (((1,), (1,)), ((), ()))
_TN = (((0,), (0,)), ((), ()))


def _rwkv_pre_core(prkv, prkv_prev, plora, plora_prev, mu_rkv, mu_lora, w0, w2p, a0, a2p, g2p, k_k, k_a):
    xs = prkv + (prkv_prev - prkv) * mu_rkv
    xl = plora + (plora_prev - plora) * mu_lora
    r, k, v = xs[:, :D], xs[:, D:2 * D], xs[:, 2 * D:]
    wd, ad, gd = xl[:, :128], xl[:, 128:256], xl[:, 256:]
    tw = jnp.tanh(wd)
    zw = w0 + _dotb(tw, w2p, _NN)
    sp = jnp.maximum(-zw, 0.0) + jnp.log(1.0 + jnp.exp(-jnp.abs(zw)))
    lw = -jnp.exp(-sp - 0.5)
    a = jax.nn.sigmoid(a0 + _dotb(ad, a2p, _NN))
    sg = jax.nn.sigmoid(gd)
    return dict(r=r, k=k, v=v, tw=tw, zw=zw, lw=lw, a=a, sg=sg, ad=ad)


def _rows_down(x, halo, blk):
    before = jnp.where(blk > 0, halo[HALO - 1:HALO, :], 0.0)
    row = lax.broadcasted_iota(jnp.int32, (x.shape[0], 1), 0)
    return jnp.where(row == 0, before, pltpu.roll(x, 1, 0))


def _rows_up(x, after):
    n = x.shape[0]
    row = lax.broadcasted_iota(jnp.int32, (n, 1), 0)
    return jnp.where(row == n - 1, after, pltpu.roll(x, n - 1, 0))


def _f_rwkv_pre(prkv, plora, mu_rkv, mu_lora, w0, w2p, a0, a2p, g2p, k_k, k_a, halo_rkv, halo_lora, blk):
    c = _rwkv_pre_core(prkv, _rows_down(prkv, halo_rkv, blk), plora, _rows_down(plora, halo_lora, blk),
                       mu_rkv, mu_lora, w0, w2p, a0, a2p, g2p, k_k, k_a)
    g = _dotb(c["sg"], g2p, _NN)
    k, a = c["k"], c["a"]
    return c["r"], c["lw"], k * (1.0 + (a - 1.0) * k_a), c["v"], k * k_k, a, g


def _f_rwkv_pre_bwd(prkv, plora, dr, dlw, dk2, dv, dkkr, da, dya, yap,
                    mu_rkv, mu_lora, w0, w2p, a0, a2p, g2p, k_k, k_a, halo_rkv, halo_lora, next_rkv, next_lora, blk):
    prkv_prev, plora_prev = _rows_down(prkv, halo_rkv, blk), _rows_down(plora, halo_lora, blk)
    c = _rwkv_pre_core(prkv, prkv_prev, plora, plora_prev, mu_rkv, mu_lora, w0, w2p, a0, a2p, g2p, k_k, k_a)
    k, a, sg, tw, zw, lw = c["k"], c["a"], c["sg"], c["tw"], c["zw"], c["lw"]
    dg = dya * yap
    dsg = _dotb(dg, g2p, _NT)
    dgd = dsg * sg * (1.0 - sg)
    dg2p = _dotb(sg, dg, _TN)
    dk = dk2 * (1.0 + (a - 1.0) * k_a) + dkkr * k_k
    da_t = da + dk2 * k * k_a
    dk_a = jnp.sum(dk2 * k * (a - 1.0), axis=0, keepdims=True)
    dk_k = jnp.sum(dkkr * k, axis=0, keepdims=True)
    dza = da_t * a * (1.0 - a)
    da0 = jnp.sum(dza, axis=0, keepdims=True)
    dad = _dotb(dza, a2p, _NT)
    da2p = _dotb(c["ad"], dza, _TN)
    dzw = dlw * lw * jax.nn.sigmoid(-zw)
    dw0 = jnp.sum(dzw, axis=0, keepdims=True)
    dtw = _dotb(dzw, w2p, _NT)
    dw2p = _dotb(tw, dzw, _TN)
    dwd = dtw * (1.0 - tw * tw)
    dxs = jnp.concatenate([dr, dk, dv], axis=1)
    dxl = jnp.concatenate([dwd, dad, dgd], axis=1)
    dmu_rkv = jnp.sum(dxs * (prkv_prev - prkv), axis=0, keepdims=True)
    dmu_lora = jnp.sum(dxl * (plora_prev - plora), axis=0, keepdims=True)
    to_next_rkv, to_next_lora = dxs * mu_rkv, dxl * mu_lora
    return (dxs * (1.0 - mu_rkv) + _rows_up(to_next_rkv, next_rkv), dxl * (1.0 - mu_lora) + _rows_up(to_next_lora, next_lora),
            dmu_rkv, dmu_lora, dw0, da0, dk_k, dk_a, dw2p, da2p, dg2p, to_next_rkv[0:1], to_next_lora[0:1])


def _group_alpha(l0, l1, l2):
    m = jnp.maximum(jnp.maximum(l0, l1), l2)
    e0, e1, e2 = jnp.exp(l0 - m), jnp.exp(l1 - m), jnp.exp(l2 - m)
    inv = 1.0 / (e0 + e1 + e2)
    return jnp.concatenate([e0 * inv, e1 * inv, e2 * inv], axis=1)


def _f_combine(o0, o1, o2, l0, l1, l2):
    return jnp.concatenate([o0, o1, o2], axis=1) * _group_alpha(l0, l1, l2)


def _f_combine_bwd(dyb, o0, o1, o2, l0, l1, l2, bd):
    alpha = _group_alpha(l0, l1, l2)
    hi, lo = _sp(dyb * jnp.concatenate([o0, o1, o2], axis=1))
    ones = bd.astype(bf16)
    e = jnp.dot(hi, ones, preferred_element_type=f32) + jnp.dot(lo, ones, preferred_element_type=f32)
    ae = alpha * e
    tot = ae[:, :256] + ae[:, 256:512] + ae[:, 512:]
    do = dyb * alpha
    dl = ae - alpha * jnp.concatenate([tot, tot, tot], axis=1)
    return do[:, :256], do[:, 256:512], do[:, 512:], dl[:, :256], dl[:, 256:512], dl[:, 512:]


def _f_merge(pgate, ta, tb, b_gate):
    gate = jax.nn.sigmoid(pgate + b_gate)
    return gate[:, :D] * ta + gate[:, D:] * tb


def _f_merge_bwd(dm, pgate, ta, tb, b_gate):
    gate = jax.nn.sigmoid(pgate + b_gate)
    ga, gb = gate[:, :D], gate[:, D:]
    dpg = jnp.concatenate([dm * ta * ga * (1.0 - ga), dm * tb * gb * (1.0 - gb)], axis=1)
    return dm * ga, dm * gb, dpg, jnp.sum(dpg, axis=0, keepdims=True)


def _f_adamw(w, g, m, v):
    m2 = ADAM_B1 * m + (1.0 - ADAM_B1) * g
    v2 = ADAM_B2 * v + (1.0 - ADAM_B2) * jnp.square(g)
    m_hat = m2 / (1.0 - ADAM_B1 ** ADAM_STEP)
    v_hat = v2 / (1.0 - ADAM_B2 ** ADAM_STEP)
    delta = -ADAM_LR * (m_hat / (jnp.sqrt(v_hat) + ADAM_EPS) + ADAM_WD * w)
    return delta, m2, v2


def _ffn_bwd(tag, dxo, dxo_b, x_in, n, gate, up, act, g, WiT, Wo, cross=None):
    dgate, dup = _ffn_dact(f"{tag}_dact", dxo_b, Wo, gate, up)
    dWo = _mm(f"{tag}_dwo", act, dxo_b, "tn", out_dtype=GRAD_WIRE, scale=0.5)
    if cross is None:
        dn_g, recv = _mm(f"{tag}_dn_gate", dgate, WiT[:D_FF], "nn"), None
    else:
        dn_g, recv = _mm(f"{tag}_dn_gate", dgate, WiT[:D_FF], "nn", cross=cross)
    dn_u = _mm(f"{tag}_dn_up", dup, WiT[D_FF:], "nn")
    dWiT = jnp.concatenate([_mm(f"{tag}_dwi_gate", dgate, n, "tn", out_dtype=GRAD_WIRE),
                            _mm(f"{tag}_dwi_up", dup, n, "tn", out_dtype=GRAD_WIRE)], axis=0)
    dx, dx_b, dg = _rowwise(f"{tag}_drms", functools.partial(_f_rms_bwd, 2), [dn_g, dn_u, x_in, dxo], [g],
                            [(D, f32), (D, bf16)], [(1, D)])
    return dx, dx_b, dg, dWiT, dWo, recv


def _local_step(x0, tgt, W, P, hooks=None):
    S = x0.shape[0]
    (n1,) = _rowwise("f1_rms", _f_rms, [x0], [P["ffn1_norm"]], [(D, bf16)])
    hooks = hooks or {}
    if "gather_mid" in hooks:
        pack, weights = hooks["gather_mid"]
        gate1, up1, act1, gathered = _ffn_up("f1_up", n1, W["f1_iT"], gather=pack)
        W = {**W, **weights(gathered)}
    else:
        gate1, up1, act1 = _ffn_up("f1_up", n1, W["f1_iT"])
    if "gather_in" in hooks:
        pack, weights = hooks["gather_in"]
        f1, gathered = _mm("f1_down", act1, W["f1_o"], "nn", gather=pack)
        W = {**W, **weights(gathered)}
    else:
        f1 = _mm("f1_down", act1, W["f1_o"], "nn")
    x1, h = _rowwise("mix_rms", functools.partial(_f_resid_rms, 0.5), [x0, f1], [P["mix_norm"]],
                     [(D, f32), (D, bf16)])
    prkv = _mm("p_rkv", h, W["in_rkvT"], "nt")
    plora = _mm("p_lora", h, W["in_loraT"], "nt")
    pqkv = _mm("p_qkv", h, W["in_qkvT"], "nt")
    pgate = _mm("p_gate", h, W["in_gateT"], "nt")
    pre_params = [P["mu_rkv"], P["mu_lora"], P["w0"], W["w2p"], P["a0"], W["a2p"], W["g2p"], P["k_k"], P["k_a"]]
    r, lw, k2, v, kkr, a, g = _rowwise("rwkv_pre", _f_rwkv_pre, [prkv, plora], pre_params, [(D, f32)] * 7, tm=128,
                                       halos=(0, 1))
    hm = [r, lw, k2, v, kkr, a]
    hp = [P["r_k"].reshape(RW_HEADS, 1, HEAD), P["ln_w"].reshape(RW_HEADS, 1, HEAD), P["ln_b"].reshape(RW_HEADS, 1, HEAD)]
    if "gather_late" in hooks:
        pack, weights = hooks["gather_late"]
        yap, ya, wkv_h, U_h, inv_h, S0s, gathered = _wkv_fwd(*hm, g, *hp, late_pack=pack)
        W = {**W, **weights(gathered)}
    else:
        yap, ya, wkv_h, U_h, inv_h, S0s = _wkv_fwd(*hm, g, *hp)
    ta = _mm("proj_a", ya, W["pr"], "nn")
    n_grp = len(ATTN_PAIRS)
    attn = [_attn_fwd(pqkv, P["q_norm"], P["k_norm"], gi, S) for gi in range(n_grp)]
    o_g, lse_g = [t[0] for t in attn], [t[1] for t in attn]
    (yb,) = _rowwise("attn_combine", _f_combine, [*o_g, *lse_g], [], [(ATTN_W, bf16)])
    tb = _mm("proj_b", yb, W["paT"], "nt")
    (merged,) = _rowwise("merge", _f_merge, [pgate, ta, tb], [P["b_gate"]], [(D, bf16)])
    mo = _mm("mix_out", merged, W["out"], "nn")
    x2, n2 = _rowwise("f2_rms", functools.partial(_f_resid_rms, 1.0), [x1, mo], [P["ffn2_norm"]],
                      [(D, f32), (D, bf16)])
    gate2, up2, act2 = _ffn_up("f2_up", n2, W["f2_iT"])
    f2 = _mm("f2_down", act2, W["f2_o"], "nn")
    dx3, dx3_b, loss = _rowwise("loss", _f_loss, [x2, f2, tgt], [], [(D, f32), (D, bf16)], [(1, 128)])
    G, Gs = {}, {}
    dx2, dx2_b, Gs["ffn2_norm"], G["f2_iT"], G["f2_o"], _ = _ffn_bwd("f2", dx3, dx3_b, x2, n2, gate2, up2, act2,
                                                                    P["ffn2_norm"], W["f2_iT"], W["f2_o"])
    dmerged = _mm("d_merged", dx2_b, W["out"], "nt")
    G["out"] = _mm("dw_out", merged, dx2_b, "tn", out_dtype=GRAD_WIRE)
    dta, dtb, dpgate, Gs["b_gate"] = _rowwise("merge_bwd", _f_merge_bwd, [dmerged, pgate, ta, tb], [P["b_gate"]],
                                              [(D, bf16), (D, bf16), (2 * D, bf16)], [(1, 2 * D)])
    dya = _mm("d_ya", dta, W["pr"], "nt")
    G["pr"] = _mm("dw_pr", ya, dta, "tn", out_dtype=GRAD_WIRE)
    dyb = _mm("d_yb", dtb, W["paT"], "nn")
    G["paT"] = _mm("dw_pa", dtb, yb, "tn", out_dtype=GRAD_WIRE)
    if "reduce_late" in hooks:
        pieces_late = hooks["reduce_late"](G)
        hg = _wkv_bwd(dya, g, *hm, wkv_h, U_h, inv_h, S0s, *hp, late_pieces=pieces_late)
        G["late"] = (pieces_late, hg[9])
    else:
        hg = _wkv_bwd(dya, g, *hm, wkv_h, U_h, inv_h, S0s, *hp)
    dr, dlw, dk2, dv, dkkr, da = hg[:6]
    Gs["r_k"], Gs["ln_w"], Gs["ln_b"] = (t.reshape(1, D) for t in hg[6:9])
    lp = sum(LORA_PAD)
    (dprkv, dplora, Gs["mu_rkv"], Gs["mu_lora"], Gs["w0"], Gs["a0"], Gs["k_k"], Gs["k_a"],
     dw2p, da2p, dg2p) = _rowwise(
        "rwkv_pre_bwd", _f_rwkv_pre_bwd,
        [prkv, plora, dr, dlw, dk2, dv, dkkr, da, dya, yap], pre_params,
        [(3 * D, bf16), (lp, bf16)],
        [(1, 3 * D), (1, lp), (1, D), (1, D), (1, D), (1, D), (LORA_PAD[0], D), (LORA_PAD[1], D), (LORA_PAD[2], D)],
        tm=128, halos=(0, 1), carries=((1, 3 * D), (1, lp)), reverse=True)
    G["w2T"], G["a2T"], G["g2T"] = dw2p[:LORA_W[0]].T, da2p[:LORA_W[1]].T, dg2p[:LORA_W[2]].T
    bd = (jnp.arange(ATTN_W)[:, None] // HEAD == jnp.arange(ATTN_W)[None, :] // HEAD).astype(f32)
    dol = _rowwise("attn_combine_bwd", _f_combine_bwd, [dyb, *o_g, *lse_g], [bd], [(ATTN_W // n_grp, f32)] * (2 * n_grp))
    dattn = [_attn_bwd(pqkv, o_g[gi], lse_g[gi], dol[gi], dol[n_grp + gi], P["q_norm"], P["k_norm"], gi, S)
             for gi in range(n_grp)]
    Gs["q_norm"] = dattn[0][3] + dattn[1][3] + dattn[2][3]
    Gs["k_norm"] = dattn[0][4] + dattn[1][4] + dattn[2][4]
    dpqkv = jnp.concatenate([dattn[gi][kind] for kind in range(3) for gi in range(n_grp)], axis=1).astype(bf16)
    dh = [_mm("dh_rkv", dprkv, W["in_rkvT"], "nn"), _mm("dh_lora", dplora, W["in_loraT"], "nn"),
          _mm("dh_qkv", dpqkv, W["in_qkvT"], "nn"), _mm("dh_gate", dpgate, W["in_gateT"], "nn")]
    dW_rkv = _mm("dw_rkv", dprkv, h, "tn", out_dtype=GRAD_WIRE)
    dW_lora = _mm("dw_lora", dplora, h, "tn", out_dtype=GRAD_WIRE)
    dW_qkv = _mm("dw_qkv", dpqkv, h, "tn", out_dtype=GRAD_WIRE)
    dW_gate = _mm("dw_gate", dpgate, h, "tn", out_dtype=GRAD_WIRE)
    o1, o2 = LORA_PAD[0], LORA_PAD[0] + LORA_PAD[1]
    G["inT"] = jnp.concatenate([dW_rkv, dW_lora[:LORA_W[0]], dW_lora[o1:o1 + LORA_W[1]], dW_lora[o2:o2 + LORA_W[2]],
                                dW_qkv, dW_gate], axis=0)
    dx1, dx1_b, Gs["mix_norm"] = _rowwise("mix_drms", functools.partial(_f_rms_bwd, 4), [*dh, x1, dx2],
                                          [P["mix_norm"]], [(D, f32), (D, bf16)], [(1, D)])
    part_mid = hooks["reduce_mid"](G) if "reduce_mid" in hooks else None
    dx0, _, Gs["ffn1_norm"], G["f1_iT"], G["f1_o"], recv_mid = _ffn_bwd(
        "f1", dx1, dx1_b, x0, n1, gate1, up1, act1, P["ffn1_norm"], W["f1_iT"], W["f1_o"], cross=part_mid)
    G["mid"] = (part_mid, recv_mid)
    return loss[0, 0], dx0, G, Gs


def _peer(k):
    x, y, c = lax.axis_index("x"), lax.axis_index("y"), lax.axis_index("c")
    px = 1 - x if k & 4 else x
    py = 1 - y if k & 2 else y
    pc = 1 - c if k & 1 else c
    return (px, py, pc), 4 * px + 2 * py + pc


def _gather_phases(x_ref, out_ref, send_sems, recv_sems, local_sem):
    x, y, c = lax.axis_index("x"), lax.axis_index("y"), lax.axis_index("c")
    me, sibling = (x, y, c), (x, y, 1 - c)
    chips = [(1 - x, y), (x, 1 - y), (1 - x, 1 - y)]

    def slot(px, py, pc):
        return out_ref.at[4 * px + 2 * py + pc]

    def copy(k, block, to, src=None):
        return pltpu.make_async_remote_copy(
            src_ref=slot(*block) if src is None else src, dst_ref=slot(*block), send_sem=send_sems.at[k],
            recv_sem=recv_sems.at[k], device_id=to, device_id_type=MESH)

    def mine():
        return pltpu.make_async_copy(x_ref, slot(*me), local_sem)

    def first():
        return [copy(0, me, sibling, src=x_ref)] + [copy(1 + j, me, (*chip, c), src=x_ref) for j, chip in enumerate(chips)]

    def passed():
        return [copy(4 + j, (*chip, c), sibling) for j, chip in enumerate(chips)]

    def start():
        mine().start()
        for cp in first():
            cp.start()

    def forward():
        for j, (chip, cp) in enumerate(zip(chips, passed())):
            copy(1 + j, (*chip, c), me).wait_recv()
            cp.start()

    def finish():
        copy(0, sibling, me).wait_recv()
        for j, chip in enumerate(chips):
            copy(4 + j, (*chip, 1 - c), me).wait_recv()
        for cp in first() + passed():
            cp.wait_send()
        mine().wait()

    return start, forward, finish


GATHER_SEMS = [pltpu.SemaphoreType.DMA((N_DEV - 1,)), pltpu.SemaphoreType.DMA((N_DEV - 1,)), pltpu.SemaphoreType.DMA(())]


def _all_gather(pack):
    R, C = pack.shape

    def body(x_ref, out_ref, send_sems, recv_sems, local_sem):
        for phase in _gather_phases(x_ref, out_ref, send_sems, recv_sems, local_sem):
            phase()

    return pl.pallas_call(
        body, name="weight_all_gather", out_shape=jax.ShapeDtypeStruct((N_DEV, R, C), pack.dtype),
        in_specs=[pl.BlockSpec(memory_space=pl.ANY)], out_specs=pl.BlockSpec(memory_space=pl.ANY),
        scratch_shapes=GATHER_SEMS,
    )(pack)


def _cross_phases(p_ref, out_ref, send_sems, recv_sems):
    x, y, c = lax.axis_index("x"), lax.axis_index("y"), lax.axis_index("c")

    def copies():
        out = []
        for j, (fx, fy) in enumerate([(1, 0), (0, 1), (1, 1)]):
            px = 1 - x if fx else x
            py = 1 - y if fy else y
            out.append(pltpu.make_async_remote_copy(src_ref=p_ref.at[2 * px + py], dst_ref=out_ref.at[j],
                                                    send_sem=send_sems.at[j], recv_sem=recv_sems.at[j],
                                                    device_id=(px, py, c), device_id_type=MESH))
        return out

    def start():
        for cp in copies():
            cp.start()

    def finish():
        for cp in copies():
            cp.wait()

    return start, finish


CROSS_SEMS = [pltpu.SemaphoreType.DMA((3,)), pltpu.SemaphoreType.DMA((3,))]


def _direct_phases(piece_refs, rows, out_ref, send_sems, recv_sems):
    offs = [sum(rows[:i]) for i in range(len(rows))]

    def copies():
        out = []
        for i, g_ref in enumerate(piece_refs):
            for k in range(1, N_DEV):
                dev, idx = _peer(k)
                out.append(pltpu.make_async_remote_copy(
                    src_ref=g_ref.at[idx], dst_ref=out_ref.at[k - 1, pl.ds(offs[i], rows[i])],
                    send_sem=send_sems.at[i * (N_DEV - 1) + k - 1], recv_sem=recv_sems.at[i * (N_DEV - 1) + k - 1],
                    device_id=dev, device_id_type=MESH))
        return out

    def start():
        for cp in copies():
            cp.start()

    def finish():
        for cp in copies():
            cp.wait()

    return start, finish


def _sum_direct(pieces, recv, me, tag):
    n = len(pieces)
    C = pieces[0].shape[2]
    nblk = [p.shape[1] // PACK_BLOCK for p in pieces]
    lo = [sum(nblk[:i]) for i in range(n)]
    R = sum(nblk) * PACK_BLOCK

    def body(me_ref, *refs):
        g_refs, r_ref, o_ref = refs[:n], refs[n], refs[n + 1]
        rb = pl.program_id(0)
        for i in range(n):
            @pl.when(jnp.logical_and(rb >= lo[i], rb < lo[i] + nblk[i]))
            def _(g_ref=g_refs[i]):
                acc = g_ref[...].astype(f32)
                for k in range(N_DEV - 1):
                    acc = acc + r_ref[k].astype(f32)
                o_ref[...] = acc

    def piece_spec(i):
        return pl.BlockSpec((None, PACK_BLOCK, C), lambda rb, me_ref: (me_ref[0], jnp.clip(rb - lo[i], 0, nblk[i] - 1), 0))

    return pl.pallas_call(
        body, name=f"grad_sum_{tag}",
        grid_spec=pltpu.PrefetchScalarGridSpec(
            num_scalar_prefetch=1, grid=(R // PACK_BLOCK,),
            in_specs=[piece_spec(i) for i in range(n)] + [pl.BlockSpec((N_DEV - 1, PACK_BLOCK, C), lambda rb, me_ref: (0, rb, 0))],
            out_specs=pl.BlockSpec((PACK_BLOCK, C), lambda rb, me_ref: (rb, 0))),
        out_shape=jax.ShapeDtypeStruct((R, C), f32),
        compiler_params=_cparams(("arbitrary",)),
    )(me, *pieces, recv)


N_CHIP = 4


def _grad_pair(pieces, tag):
    n = len(pieces)
    C = pieces[0].shape[2]
    rows = [p.shape[1] for p in pieces]
    offs = [sum(rows[:i]) for i in range(n)]
    R = sum(rows)

    def body(*refs):
        g_refs, (other_ref, send_sems, recv_sems) = refs[:n], refs[n:]
        x, y, c = lax.axis_index("x"), lax.axis_index("y"), lax.axis_index("c")
        copies = []
        for i, g_ref in enumerate(g_refs):
            for k in range(N_CHIP):
                cp = pltpu.make_async_remote_copy(
                    src_ref=g_ref.at[4 * (k // 2) + 2 * (k % 2) + 1 - c], dst_ref=other_ref.at[k, pl.ds(offs[i], rows[i])],
                    send_sem=send_sems.at[i * N_CHIP + k], recv_sem=recv_sems.at[i * N_CHIP + k],
                    device_id=(x, y, 1 - c), device_id_type=MESH)
                cp.start()
                copies.append(cp)
        for cp in copies:
            cp.wait()

    return pl.pallas_call(
        body, name=f"grad_pair_{tag}", out_shape=jax.ShapeDtypeStruct((N_CHIP, R, C), pieces[0].dtype),
        in_specs=[pl.BlockSpec(memory_space=pl.ANY)] * n, out_specs=pl.BlockSpec(memory_space=pl.ANY),
        scratch_shapes=[pltpu.SemaphoreType.DMA((n * N_CHIP,))] * 2,
    )(*pieces)


def _pair_add(pieces, other, c, tag):
    n = len(pieces)
    C = pieces[0].shape[2]
    nblk = [p.shape[1] // PACK_BLOCK for p in pieces]
    lo = [sum(nblk[:i]) for i in range(n)]
    R = sum(nblk) * PACK_BLOCK

    def body(c_ref, *refs):
        g_refs, o_ref, out_ref = refs[:n], refs[n], refs[n + 1]
        rb = pl.program_id(1)
        for i in range(n):
            @pl.when(jnp.logical_and(rb >= lo[i], rb < lo[i] + nblk[i]))
            def _(g_ref=g_refs[i]):
                out_ref[...] = (g_ref[...].astype(f32) + o_ref[...].astype(f32)).astype(out_ref.dtype)

    def piece_spec(i):
        return pl.BlockSpec((1, None, PACK_BLOCK, C),
                            lambda k, rb, c_ref: (k, c_ref[0], jnp.clip(rb - lo[i], 0, nblk[i] - 1), 0))

    blk = pl.BlockSpec((1, PACK_BLOCK, C), lambda k, rb, c_ref: (k, rb, 0))
    return pl.pallas_call(
        body, name=f"pair_add_{tag}",
        grid_spec=pltpu.PrefetchScalarGridSpec(
            num_scalar_prefetch=1, grid=(N_CHIP, R // PACK_BLOCK),
            in_specs=[piece_spec(i) for i in range(n)] + [blk], out_specs=blk),
        out_shape=jax.ShapeDtypeStruct((N_CHIP, R, C), other.dtype),
        compiler_params=_cparams(("arbitrary", "arbitrary")),
    )(c, *[p.reshape(N_CHIP, 2, p.shape[1], C) for p in pieces], other)


def _grad_cross(part):
    _, R, C = part.shape

    def body(p_ref, out_ref, send_sems, recv_sems):
        for phase in _cross_phases(p_ref, out_ref, send_sems, recv_sems):
            phase()

    return pl.pallas_call(
        body, name="grad_cross", out_shape=jax.ShapeDtypeStruct((3, R, C), part.dtype),
        in_specs=[pl.BlockSpec(memory_space=pl.ANY)], out_specs=pl.BlockSpec(memory_space=pl.ANY),
        scratch_shapes=CROSS_SEMS,
    )(part)


def _grad_sum(part, recv, my_chip, tr, tag):
    _, R, C = part.shape

    def body(chip_ref, p_ref, r_ref, o_ref):
        acc = p_ref[0].astype(f32)
        for j in range(3):
            acc = acc + r_ref[j].astype(f32)
        o_ref[...] = acc

    return pl.pallas_call(
        body, name=f"grad_sum_{tag}",
        grid_spec=pltpu.PrefetchScalarGridSpec(
            num_scalar_prefetch=1, grid=(R // tr,),
            in_specs=[pl.BlockSpec((1, tr, C), lambda i, chip_ref: (chip_ref[0], i, 0)),
                      pl.BlockSpec((3, tr, C), lambda i, chip_ref: (0, i, 0))],
            out_specs=pl.BlockSpec((tr, C), lambda i, chip_ref: (i, 0))),
        out_shape=jax.ShapeDtypeStruct((R, C), f32),
        compiler_params=_cparams(("arbitrary",)),
    )(my_chip, part, recv)


def _small_all_reduce(small):
    R, C = small.shape

    def body(x_ref, o_ref, buf, send_sems, recv_sems):
        _, me = _peer(0)
        buf[me] = x_ref[...]
        sends = []
        for k in range(1, N_DEV):
            dev, _ = _peer(k)
            cp = pltpu.make_async_remote_copy(src_ref=x_ref, dst_ref=buf.at[me], send_sem=send_sems.at[k - 1],
                                              recv_sem=recv_sems.at[k - 1], device_id=dev, device_id_type=MESH)
            cp.start()
            sends.append(cp)
        for k in range(1, N_DEV):
            dev, idx = _peer(k)
            pltpu.make_async_remote_copy(src_ref=x_ref, dst_ref=buf.at[idx], send_sem=send_sems.at[k - 1],
                                         recv_sem=recv_sems.at[k - 1], device_id=dev, device_id_type=MESH).wait_recv()
        for cp in sends:
            cp.wait_send()
        acc = buf[0]
        for i in range(1, N_DEV):
            acc = acc + buf[i]
        o_ref[...] = acc

    return pl.pallas_call(
        body, name="small_all_reduce", out_shape=jax.ShapeDtypeStruct((R, C), f32),
        in_specs=[pl.BlockSpec(memory_space=pltpu.VMEM)], out_specs=pl.BlockSpec(memory_space=pltpu.VMEM),
        scratch_shapes=[pltpu.VMEM((N_DEV, R, C), f32), pltpu.SemaphoreType.DMA((N_DEV - 1,)),
                        pltpu.SemaphoreType.DMA((N_DEV - 1,))],
    )(small)


_LORA = (("rwkv_w2", True), ("rwkv_a2", True), ("rwkv_g2", True))
_GROUPS_FIRST = ((("ffn1_w_in", True),),)
_GROUPS_MID = ((("ffn1_w_out", False),), _LORA)
_GROUPS_IN = ((("w_in", True),),)
_GROUPS_LATE = ((("w_proj_rwkv", False),), (("w_proj_attn", True),), (("w_out", False),),
                (("ffn2_w_in", True),), (("ffn2_w_out", False),))
_GRADS_MID = ((("w_in", True),), _LORA)
_GRADS_LAST = ((("ffn1_w_in", True),), (("ffn1_w_out", False),))
_BIG = tuple(item for group in _GROUPS_FIRST + _GROUPS_MID + _GROUPS_IN + _GROUPS_LATE for item in group)
_SMALL = ("ffn1_norm", "mix_norm", "b_gate", "rwkv_mu", "rwkv_w0", "rwkv_a0", "rwkv_k_k", "rwkv_k_a", "rwkv_r_k",
          "rwkv_ln_w", "rwkv_ln_b", "attn_q_norm", "attn_k_norm", "ffn2_norm")


def _pack_layout(like, groups):
    items, spans, off = {}, [], 0
    for group in groups:
        start = off
        for name, _ in group:
            shp = like[name].shape
            n = shp[0] * shp[1] // D
            items[name] = (off, n)
            off += n
        off = -(-off // PACK_BLOCK) * PACK_BLOCK
        spans.append((start, off - start))
    return items, spans, off


def _pack_big(shards, groups):
    items, _, rows = _pack_layout(shards, groups)
    parts, at = [], 0
    for group in groups:
        for name, tr in group:
            off, n = items[name]
            t = shards[name]
            if off > at:
                parts.append(jnp.zeros((off - at, D), t.dtype))
            parts.append((t.T if tr else t).reshape(n, D))
            at = off + n
    if rows > at:
        parts.append(jnp.zeros((rows - at, D), parts[0].dtype))
    return jnp.concatenate(parts, axis=0)


def _unpack_big(pack, like, groups):
    items, _, _ = _pack_layout(like, groups)
    out = {}
    for group in groups:
        for name, tr in group:
            off, n = items[name]
            shp = like[name].shape
            t = pack[off:off + n]
            out[name] = t.reshape(shp[1], shp[0]).T if tr else t.reshape(shp)
    return out


def _unpack_gathered(gathered, like, groups):
    items, _, _ = _pack_layout(like, groups)
    full = {}
    for group in groups:
        for name, tr in group:
            shp = like[name].shape
            off, rows = items[name]
            r_loc, c_loc = (shp[1], shp[0]) if tr else shp
            full[name] = gathered[:, off:off + rows].reshape(N_DEV * r_loc, c_loc)
    return full


def _grad_pieces(g_full, like, groups):
    items, spans, _ = _pack_layout(like, groups)
    pieces = []
    for group, (_, rows_pad) in zip(groups, spans):
        parts = [g_full[n].astype(GRAD_WIRE).reshape(N_DEV, items[n][1], D) for n, _ in group]
        piece = parts[0] if len(parts) == 1 else jnp.concatenate(parts, axis=1)
        if rows_pad > piece.shape[1]:
            piece = jnp.pad(piece, ((0, 0), (0, rows_pad - piece.shape[1]), (0, 0)))
        pieces.append(piece)
    return pieces


def _small_rows(name, t):
    flat = t.reshape(-1)
    pad = (-flat.shape[0]) % D
    return jnp.pad(flat, (0, pad)).reshape(-1, D)


def _pack_small(vals):
    parts = [_small_rows(n, vals[n]) for n in _SMALL]
    used = sum(p.shape[0] for p in parts)
    parts.append(jnp.zeros((SMALL_ROWS - used, D), f32))
    return jnp.concatenate(parts, axis=0)


def _unpack_small(pack, like):
    out, off = {}, 0
    for n in _SMALL:
        size = like[n].size
        rows = -(-size // D)
        out[n] = pack[off:off + rows].reshape(-1)[:size].reshape(like[n].shape)
        off += rows
    return out


def _build_W_mid(full):
    dt = full["rwkv_w2"].dtype
    z64, z96 = jnp.zeros((64, D), dt), jnp.zeros((96, D), dt)
    return {
        "f1_o": full["ffn1_w_out"],
        "w2p": jnp.concatenate([full["rwkv_w2"].T, z64], axis=0),
        "a2p": jnp.concatenate([full["rwkv_a2"].T, z64], axis=0),
        "g2p": jnp.concatenate([full["rwkv_g2"].T, z96], axis=0),
    }


def _build_W_in(full):
    inT = full["w_in"]
    z64, z96 = jnp.zeros((64, D), inT.dtype), jnp.zeros((96, D), inT.dtype)
    return {
        "in_rkvT": inT[:3 * D],
        "in_loraT": jnp.concatenate([inT[3072:3136], z64, inT[3136:3200], z64, inT[3200:3360], z96], axis=0),
        "in_qkvT": inT[3360:3360 + 3 * ATTN_W], "in_gateT": inT[3360 + 3 * ATTN_W:],
    }


def _build_W_late(full):
    return {"pr": full["w_proj_rwkv"], "paT": full["w_proj_attn"], "out": full["w_out"],
            "f2_iT": full["ffn2_w_in"], "f2_o": full["ffn2_w_out"]}


def _build_W_first(full):
    return {"f1_iT": full["ffn1_w_in"]}


def _build_W(full):
    return {**_build_W_first(full), **_build_W_mid(full), **_build_W_in(full), **_build_W_late(full)}


_G_NAMES = {"ffn1_w_in": "f1_iT", "ffn1_w_out": "f1_o", "w_in": "inT", "rwkv_w2": "w2T", "rwkv_a2": "a2T",
            "rwkv_g2": "g2T", "w_proj_rwkv": "pr", "w_proj_attn": "paT", "w_out": "out", "ffn2_w_in": "f2_iT",
            "ffn2_w_out": "f2_o"}


def _named_grads(G, groups):
    return {n: G[_G_NAMES[n]] for group in groups for n, _ in group}


def _reduce_start(G, like, groups, my_c, tag):
    pieces = _grad_pieces(_named_grads(G, groups), like, groups)
    return _pair_add(pieces, _grad_pair(pieces, tag), my_c, tag)


def _build_P(Wl):
    mu = Wl["rwkv_mu"]
    z64f, z96f = jnp.zeros((1, 64), f32), jnp.zeros((1, 96), f32)
    return {
        "ffn1_norm": Wl["ffn1_norm"][None], "mix_norm": Wl["mix_norm"][None], "ffn2_norm": Wl["ffn2_norm"][None],
        "b_gate": Wl["b_gate"][None], "mu_rkv": mu[None, :3 * D],
        "mu_lora": jnp.concatenate([mu[None, 3072:3136], z64f, mu[None, 3136:3200], z64f, mu[None, 3200:3360], z96f], axis=1),
        "w0": Wl["rwkv_w0"][None], "a0": Wl["rwkv_a0"][None], "k_k": Wl["rwkv_k_k"][None], "k_a": Wl["rwkv_k_a"][None],
        "r_k": Wl["rwkv_r_k"].reshape(1, D), "ln_w": Wl["rwkv_ln_w"][None], "ln_b": Wl["rwkv_ln_b"][None],
        "q_norm": Wl["attn_q_norm"][None], "k_norm": Wl["attn_k_norm"][None],
    }


def kernel(x, ffn1_norm, ffn1_w_in, ffn1_w_out, mix_norm, w_in, b_gate, rwkv_mu, rwkv_w0, rwkv_w2, rwkv_a0, rwkv_a2, rwkv_g2, rwkv_k_k, rwkv_k_a, rwkv_r_k, rwkv_ln_w, rwkv_ln_b, attn_q_norm, attn_k_norm, w_proj_rwkv, w_proj_attn, w_out, ffn2_norm, ffn2_w_in, ffn2_w_out, loss_target, m_ffn1_norm, m_ffn1_w_in, m_ffn1_w_out, m_mix_norm, m_w_in, m_b_gate, m_rwkv_mu, m_rwkv_w0, m_rwkv_w2, m_rwkv_a0, m_rwkv_a2, m_rwkv_g2, m_rwkv_k_k, m_rwkv_k_a, m_rwkv_r_k, m_rwkv_ln_w, m_rwkv_ln_b, m_attn_q_norm, m_attn_k_norm, m_w_proj_rwkv, m_w_proj_attn, m_w_out, m_ffn2_norm, m_ffn2_w_in, m_ffn2_w_out, v_ffn1_norm, v_ffn1_w_in, v_ffn1_w_out, v_mix_norm, v_w_in, v_b_gate, v_rwkv_mu, v_rwkv_w0, v_rwkv_w2, v_rwkv_a0, v_rwkv_a2, v_rwkv_g2, v_rwkv_k_k, v_rwkv_k_a, v_rwkv_r_k, v_rwkv_ln_w, v_rwkv_ln_b, v_attn_q_norm, v_attn_k_norm, v_w_proj_rwkv, v_w_proj_attn, v_w_out, v_ffn2_norm, v_ffn2_w_in, v_ffn2_w_out):
    names = ("ffn1_norm", "ffn1_w_in", "ffn1_w_out", "mix_norm", "w_in", "b_gate", "rwkv_mu", "rwkv_w0", "rwkv_w2",
             "rwkv_a0", "rwkv_a2", "rwkv_g2", "rwkv_k_k", "rwkv_k_a", "rwkv_r_k", "rwkv_ln_w", "rwkv_ln_b",
             "attn_q_norm", "attn_k_norm", "w_proj_rwkv", "w_proj_attn", "w_out", "ffn2_norm", "ffn2_w_in", "ffn2_w_out")
    w_all = (ffn1_norm, ffn1_w_in, ffn1_w_out, mix_norm, w_in, b_gate, rwkv_mu, rwkv_w0, rwkv_w2, rwkv_a0, rwkv_a2,
             rwkv_g2, rwkv_k_k, rwkv_k_a, rwkv_r_k, rwkv_ln_w, rwkv_ln_b, attn_q_norm, attn_k_norm, w_proj_rwkv,
             w_proj_attn, w_out, ffn2_norm, ffn2_w_in, ffn2_w_out)
    m_all = (m_ffn1_norm, m_ffn1_w_in, m_ffn1_w_out, m_mix_norm, m_w_in, m_b_gate, m_rwkv_mu, m_rwkv_w0, m_rwkv_w2,
             m_rwkv_a0, m_rwkv_a2, m_rwkv_g2, m_rwkv_k_k, m_rwkv_k_a, m_rwkv_r_k, m_rwkv_ln_w, m_rwkv_ln_b,
             m_attn_q_norm, m_attn_k_norm, m_w_proj_rwkv, m_w_proj_attn, m_w_out, m_ffn2_norm, m_ffn2_w_in, m_ffn2_w_out)
    v_all = (v_ffn1_norm, v_ffn1_w_in, v_ffn1_w_out, v_mix_norm, v_w_in, v_b_gate, v_rwkv_mu, v_rwkv_w0, v_rwkv_w2,
             v_rwkv_a0, v_rwkv_a2, v_rwkv_g2, v_rwkv_k_k, v_rwkv_k_a, v_rwkv_r_k, v_rwkv_ln_w, v_rwkv_ln_b,
             v_attn_q_norm, v_attn_k_norm, v_w_proj_rwkv, v_w_proj_attn, v_w_out, v_ffn2_norm, v_ffn2_w_in, v_ffn2_w_out)
    Wl = {n: t[0] for n, t in zip(names, w_all)}
    Ml = {n: t[0] for n, t in zip(names, m_all)}
    Vl = {n: t[0] for n, t in zip(names, v_all)}
    big = [n for n, _ in _BIG]

    my_c = lax.axis_index("c").astype(jnp.int32).reshape(1)
    my_chip = (2 * lax.axis_index("x") + lax.axis_index("y")).astype(jnp.int32).reshape(1)

    def pack(groups):
        return _pack_big(Wl, groups).astype(bf16)

    gathered = _all_gather(pack(_GROUPS_FIRST))
    W, P = _build_W_first(_unpack_gathered(gathered, Wl, _GROUPS_FIRST)), _build_P(Wl)
    hooks = {"gather_mid": (pack(_GROUPS_MID), lambda g: _build_W_mid(_unpack_gathered(g, Wl, _GROUPS_MID))),
             "gather_in": (pack(_GROUPS_IN), lambda g: _build_W_in(_unpack_gathered(g, Wl, _GROUPS_IN))),
             "gather_late": (pack(_GROUPS_LATE), lambda g: _build_W_late(_unpack_gathered(g, Wl, _GROUPS_LATE))),
             "reduce_mid": lambda G: _reduce_start(G, Wl, _GRADS_MID, my_c, "mid"),
             "reduce_late": lambda G: _grad_pieces(_named_grads(G, _GROUPS_LATE), Wl, _GROUPS_LATE)}

    loss_local, dx0, G, Gs = _local_step(x[0], loss_target[0], W, P, hooks)

    part_last = _reduce_start(G, Wl, _GRADS_LAST, my_c, "last")
    g_big = _unpack_big(_grad_sum(part_last, _grad_cross(part_last), my_chip, 128, "last"), Wl, _GRADS_LAST)
    g_big.update(_unpack_big(_grad_sum(*G["mid"], my_chip, 128, "mid"), Wl, _GRADS_MID))
    me = (4 * lax.axis_index("x") + 2 * lax.axis_index("y") + lax.axis_index("c")).astype(jnp.int32).reshape(1)
    g_big.update(_unpack_big(_sum_direct(*G["late"], me, "late"), Wl, _GROUPS_LATE))

    mu_g = Gs["mu_rkv"], Gs["mu_lora"]
    o1, o2 = LORA_PAD[0], LORA_PAD[0] + LORA_PAD[1]
    g_small_local = {
        "ffn1_norm": Gs["ffn1_norm"], "mix_norm": Gs["mix_norm"], "b_gate": Gs["b_gate"],
        "rwkv_mu": jnp.concatenate([mu_g[0], mu_g[1][:, :64], mu_g[1][:, o1:o1 + 64], mu_g[1][:, o2:o2 + 160]], axis=1),
        "rwkv_w0": Gs["w0"], "rwkv_a0": Gs["a0"], "rwkv_k_k": Gs["k_k"], "rwkv_k_a": Gs["k_a"], "rwkv_r_k": Gs["r_k"],
        "rwkv_ln_w": Gs["ln_w"], "rwkv_ln_b": Gs["ln_b"], "attn_q_norm": Gs["q_norm"], "attn_k_norm": Gs["k_norm"],
        "ffn2_norm": Gs["ffn2_norm"]}
    gs_pack = _small_all_reduce(_pack_small(g_small_local))

    out_g, out_d, out_m, out_v = dict(g_big), {}, {}, {}
    for n in big:
        cols = Wl[n].shape[1]
        out_d[n], out_m[n], out_v[n] = _rowwise(f"adamw_{n}", _f_adamw, [Wl[n], g_big[n], Ml[n], Vl[n]], [],
                                                 [(cols, f32)] * 3)
    ds_pack, ms_pack, vs_pack = _rowwise(
        "adamw_small", _f_adamw, [_pack_small(Wl), gs_pack, _pack_small(Ml), _pack_small(Vl)], [], [(D, f32)] * 3)
    for out, pack in ((out_g, gs_pack), (out_d, ds_pack), (out_m, ms_pack), (out_v, vs_pack)):
        out.update(_unpack_small(pack, Wl))

    loss = lax.psum(loss_local, ("x", "y", "c"))
    return (loss, dx0[None], *[out_g[n][None] for n in names], *[out_d[n][None] for n in names],
            *[out_m[n][None] for n in names], *[out_v[n][None] for n in names])
```

```python
import functools

import jax
import jax.numpy as jnp
from jax import lax
from jax.experimental import pallas as pl
from jax.experimental.pallas import tpu as pltpu

f32 = jnp.float32
bf16 = jnp.bfloat16
MESH = pl.DeviceIdType.MESH

N_DEV = 8
D = 1024
D_FF = 2816
HEAD = 64
RW_HEADS = 16
ATTN_PAIRS = ((128, 1), (512, 4), (2048, 16))
ATTN_BLK = 128
HEADS_PER_GROUP = 4
ATTN_W = 768
LORA_PAD = (128, 128, 256)
LORA_W = (64, 64, 160)
GN_EPS = 64e-5
RMS_EPS = 1e-6
NEG_INF = -1e30
WKV_T = 64
WKV_SUB = 2
GRAD_WIRE = bf16
PACK_BLOCK = 128
SMALL_ROWS = 24
VMEM_LIMIT = 56 * 1024 * 1024

ADAM_LR, ADAM_B1, ADAM_B2, ADAM_EPS, ADAM_WD, ADAM_STEP = 0.001, 0.9, 0.999, 1e-08, 0.01, 10


def _cparams(sem):
    return pltpu.CompilerParams(dimension_semantics=sem, vmem_limit_bytes=VMEM_LIMIT)


def _pick(n, cands):
    for c in cands:
        if n % c == 0:
            return c
    return n


HALO = 8


def _rowwise(name, fn, rows, params, outs, accs=(), tm=256, halos=(), carries=(), reverse=False):
    S = rows[0].shape[0]
    tm = min(tm, S)
    while S % tm:
        tm -= 8
    nb = S // tm
    n_in = len(rows) + len(params) + len(halos)
    n_out = len(outs)
    n_acc = len(accs)
    n_car = len(carries)

    def blk_of(i):
        return nb - 1 - i if reverse else i

    def body(*refs):
        step = pl.program_id(0)
        carry_refs = refs[n_in + n_out + n_acc:]
        if n_car:
            @pl.when(step == 0)
            def _():
                for c_ref in carry_refs:
                    c_ref[...] = jnp.zeros(c_ref.shape, f32)
        args = [r[...] for r in refs[:n_in]] + [c[...] for c in carry_refs]
        res = fn(*args, blk=blk_of(step)) if (halos or carries) else fn(*args)
        if not isinstance(res, (tuple, list)):
            res = (res,)
        out_refs = refs[n_in:n_in + n_out + n_acc]
        for j in range(n_out):
            out_refs[j][...] = res[j].astype(out_refs[j].dtype)
        if n_acc:
            @pl.when(step == 0)
            def _():
                for j in range(n_acc):
                    out_refs[n_out + j][...] = jnp.zeros(out_refs[n_out + j].shape, f32)
            for j in range(n_acc):
                out_refs[n_out + j][...] += res[n_out + j]
        for j in range(n_car):
            carry_refs[j][...] = res[n_out + n_acc + j]

    in_specs = [pl.BlockSpec((tm, a.shape[1]), lambda i: (blk_of(i), 0)) for a in rows]
    in_specs += [pl.BlockSpec(p.shape, lambda i, nd=p.ndim: (0,) * nd) for p in params]
    in_specs += [pl.BlockSpec((HALO, rows[h].shape[1]), lambda i: (jnp.maximum(blk_of(i) * (tm // HALO) - 1, 0), 0))
                 for h in halos]
    out_specs = [pl.BlockSpec((tm, w), lambda i: (blk_of(i), 0)) for w, _ in outs]
    out_specs += [pl.BlockSpec(s, lambda i: (0, 0)) for s in accs]
    out_shape = [jax.ShapeDtypeStruct((S, w), dt) for w, dt in outs]
    out_shape += [jax.ShapeDtypeStruct(s, f32) for s in accs]
    res = pl.pallas_call(
        body, name=name, grid=(nb,), in_specs=in_specs, out_specs=out_specs, out_shape=out_shape,
        scratch_shapes=[pltpu.VMEM(s, f32) for s in carries],
        compiler_params=_cparams(("arbitrary",)),
    )(*rows, *params, *[rows[h] for h in halos])
    return res


MM_VMEM_BUDGET = 40 * 1024 * 1024
MM_STEP_US = 0.35
MM_FLOPS_PER_US = 9.0e8
MM_HBM_BYTES_PER_US = 3.0e6


def _tile_options(n, cap):
    opts = [d for d in range(128, min(n, cap) + 1, 128) if n % d == 0]
    return opts or [n]


def _mm_tiles(M, N, K, sa, sb, so):
    best, best_cost = None, None
    for tm in _tile_options(M, 2048):
        for tn in _tile_options(N, 2048):
            for tk in _tile_options(K, 4096):
                vmem = 2 * (tm * tk * sa + tk * tn * sb) + 2 * tm * tn * so + (tm * tn * 4 if tk < K else 0)
                if vmem > MM_VMEM_BUDGET:
                    continue
                steps = (M // tm) * (N // tn) * (K // tk)
                traffic = M * K * sa * (N // tn) + K * N * sb * (M // tm) + M * N * so
                cost = (max(2.0 * M * N * K / MM_FLOPS_PER_US, traffic / MM_HBM_BYTES_PER_US) + steps * MM_STEP_US
                        + (tm * tk * sa + tk * tn * sb) / MM_HBM_BYTES_PER_US)
                if best_cost is None or cost < best_cost:
                    best, best_cost = (tm, tn, tk), cost
    return best


def _mm(name, a, b, mode, out_dtype=f32, scale=None, gather=None, cross=None):
    halves = a.ndim == 3
    sizes = (jnp.dtype(a.dtype).itemsize, jnp.dtype(b.dtype).itemsize, jnp.dtype(out_dtype).itemsize)
    if mode == "nn":
        (M, K), N = (a.shape[1], 2 * a.shape[2]) if halves else a.shape, b.shape[1]
        tm, tn, tk = _mm_tiles(M, N, K // 2 if halves else K, *sizes)
    elif mode == "nt":
        (M, K), N = a.shape, b.shape[0]
        tm, tn, tk = _mm_tiles(M, N, K, *sizes)
    else:
        (K, M), N = (a.shape[1], 2 * a.shape[2]) if halves else a.shape, b.shape[1]
        tm, tn, tk = _mm_tiles(M // 2 if halves else M, N, K, *sizes)
    nk = K // tk
    if mode == "nn":
        per = K // 2 // tk
        a_spec = (pl.BlockSpec((None, tm, tk), lambda i, j, k: (k // per, i, k % per)) if halves
                  else pl.BlockSpec((tm, tk), lambda i, j, k: (i, k)))
        b_spec = pl.BlockSpec((tk, tn), lambda i, j, k: (k, j))
        dims = (((1,), (0,)), ((), ()))
    elif mode == "nt":
        a_spec = pl.BlockSpec((tm, tk), lambda i, j, k: (i, k))
        b_spec = pl.BlockSpec((tn, tk), lambda i, j, k: (j, k))
        dims = (((1,), (1,)), ((), ()))
    else:
        per = M // 2 // tm
        a_spec = (pl.BlockSpec((None, tk, tm), lambda i, j, k: (i // per, k, i % per)) if halves
                  else pl.BlockSpec((tk, tm), lambda i, j, k: (k, i)))
        b_spec = pl.BlockSpec((tk, tn), lambda i, j, k: (k, j))
        dims = (((0,), (0,)), ((), ()))

    def finish(acc):
        return acc if scale is None else acc * scale

    hosted = gather if gather is not None else cross
    grid = (M // tm, N // tn, nk)
    steps = grid[0] * grid[1] * grid[2]

    def body(a_ref, b_ref, *rest):
        if hosted is None:
            o_ref, *scratch = rest
        else:
            src_ref, o_ref, dst_ref, *scratch = rest
            n_sem = len(GATHER_SEMS if gather is not None else CROSS_SEMS)
            sems, scratch = scratch[len(scratch) - n_sem:], scratch[:len(scratch) - n_sem]
            step = (pl.program_id(0) * grid[1] + pl.program_id(1)) * grid[2] + pl.program_id(2)
            if gather is not None:
                start, forward, done = _gather_phases(src_ref, dst_ref, *sems)
                pl.when(step == steps // 2)(forward)
            else:
                start, done = _cross_phases(src_ref, dst_ref, *sems)
            pl.when(step == 0)(start)
        part = lax.dot_general(a_ref[...].astype(bf16), b_ref[...].astype(bf16), dims,
                               preferred_element_type=f32)
        if nk == 1:
            o_ref[...] = finish(part).astype(o_ref.dtype)
        else:
            acc_ref = scratch[0]
            k = pl.program_id(2)

            @pl.when(k == 0)
            def _():
                acc_ref[...] = part

            @pl.when(k > 0)
            def _():
                acc_ref[...] += part

            @pl.when(k == nk - 1)
            def _():
                o_ref[...] = finish(acc_ref[...]).astype(o_ref.dtype)
        if hosted is not None:
            pl.when(step == steps - 1)(done)

    hbm = pl.BlockSpec(memory_space=pl.ANY)
    out_specs = [pl.BlockSpec((tm, tn), lambda i, j, k: (i, j))]
    out_shape = [jax.ShapeDtypeStruct((M, N), out_dtype)]
    scratch_shapes = [] if nk == 1 else [pltpu.VMEM((tm, tn), f32)]
    if gather is not None:
        out_specs.append(hbm)
        out_shape.append(jax.ShapeDtypeStruct((N_DEV,) + gather.shape, gather.dtype))
        scratch_shapes = scratch_shapes + GATHER_SEMS
    elif cross is not None:
        out_specs.append(hbm)
        out_shape.append(jax.ShapeDtypeStruct((3,) + cross.shape[1:], cross.dtype))
        scratch_shapes = scratch_shapes + CROSS_SEMS
    res = pl.pallas_call(
        body, name=name, grid=grid, in_specs=[a_spec, b_spec] + [hbm] * (hosted is not None),
        out_specs=out_specs, out_shape=out_shape, scratch_shapes=scratch_shapes,
        compiler_params=_cparams(("arbitrary",) * 3 if hosted is not None else ("parallel", "parallel", "arbitrary")),
    )(a, b, *([hosted] if hosted is not None else []))
    return res[0] if hosted is None else res


FFN_TM, FFN_TN = 512, 1408


def _ffn_up(name, n, WiT, gather=None):
    S = n.shape[0]
    grid = (S // FFN_TM, D_FF // FFN_TN)
    steps = grid[0] * grid[1]

    def body(n_ref, wg_ref, wu_ref, *rest):
        if gather is None:
            g_ref, u_ref, act_ref = rest
        else:
            src_ref, g_ref, u_ref, act_ref, dst_ref, *sems = rest
            step = pl.program_id(0) * grid[1] + pl.program_id(1)
            start, forward, done = _gather_phases(src_ref, dst_ref, *sems)
            pl.when(step == 0)(start)
            pl.when(step == steps // 2)(forward)
        x = n_ref[...]
        gate = lax.dot_general(x, wg_ref[...], _NT, preferred_element_type=f32)
        up = lax.dot_general(x, wu_ref[...], _NT, preferred_element_type=f32)
        g_ref[...] = gate
        u_ref[...] = up
        act_ref[...] = (gate * jax.nn.sigmoid(gate) * up).astype(act_ref.dtype)
        if gather is not None:
            pl.when(step == steps - 1)(done)

    hbm = pl.BlockSpec(memory_space=pl.ANY)
    tile = pl.BlockSpec((FFN_TM, FFN_TN), lambda i, j: (i, j))
    in_specs = [pl.BlockSpec((FFN_TM, D), lambda i, j: (i, 0)), pl.BlockSpec((FFN_TN, D), lambda i, j: (j, 0)),
                pl.BlockSpec((FFN_TN, D), lambda i, j: (j + D_FF // FFN_TN, 0))]
    out_specs = [tile, tile, tile]
    out_shape = [jax.ShapeDtypeStruct((S, D_FF), f32), jax.ShapeDtypeStruct((S, D_FF), f32),
                 jax.ShapeDtypeStruct((S, D_FF), bf16)]
    if gather is not None:
        in_specs.append(hbm)
        out_specs.append(hbm)
        out_shape.append(jax.ShapeDtypeStruct((N_DEV,) + gather.shape, gather.dtype))
    return pl.pallas_call(
        body, name=name, grid=grid, in_specs=in_specs, out_specs=out_specs, out_shape=out_shape,
        scratch_shapes=GATHER_SEMS if gather is not None else [],
        compiler_params=_cparams(("arbitrary", "arbitrary")),
    )(n, WiT, WiT, *([gather] if gather is not None else []))


def _ffn_dact(name, dy, Wo, gate, up):
    S = dy.shape[0]

    def body(dy_ref, wo_ref, g_ref, u_ref, d_ref):
        dact = 0.5 * lax.dot_general(dy_ref[...], wo_ref[...], _NT, preferred_element_type=f32)
        gate, up = g_ref[...], u_ref[...]
        sg = jax.nn.sigmoid(gate)
        d_ref[0] = (dact * up * (sg * (1.0 + gate * (1.0 - sg)))).astype(d_ref.dtype)
        d_ref[1] = (dact * gate * sg).astype(d_ref.dtype)

    tile = pl.BlockSpec((FFN_TM, FFN_TN), lambda i, j: (i, j))
    return pl.pallas_call(
        body, name=name, grid=(S // FFN_TM, D_FF // FFN_TN),
        in_specs=[pl.BlockSpec((FFN_TM, D), lambda i, j: (i, 0)), pl.BlockSpec((FFN_TN, D), lambda i, j: (j, 0)), tile, tile],
        out_specs=pl.BlockSpec((2, FFN_TM, FFN_TN), lambda i, j: (0, i, j)),
        out_shape=jax.ShapeDtypeStruct((2, S, D_FF), bf16),
        compiler_params=_cparams(("parallel", "parallel")),
    )(dy, Wo, gate, up)


def _sp(x):
    hi = x.astype(bf16)
    return hi, (x - hi.astype(f32)).astype(bf16)


def _cat(parts):
    return tuple(jnp.concatenate(p, axis=1) for p in zip(*parts))


def _bmm(eq, a, b):
    (ah, al), (bh, bl) = a, b
    dot = functools.partial(jnp.einsum, eq, preferred_element_type=f32)
    return dot(ah, bh) + (dot(ah, bl) + dot(al, bh))


def _tri_dot(eq, tri, x):
    h1 = x.astype(bf16)
    r1 = x - h1.astype(f32)
    h2 = r1.astype(bf16)
    h3 = (r1 - h2.astype(f32)).astype(bf16)
    dot = functools.partial(jnp.einsum, eq, preferred_element_type=f32)
    return dot(tri, h1) + (dot(tri, h2) + dot(tri, h3))


def _tri_masks(T):
    ti = lax.broadcasted_iota(jnp.int32, (T, T), 0)
    si = lax.broadcasted_iota(jnp.int32, (T, T), 1)
    return ti >= si, ti > si


def _wkv_prep(r, lw, k, kkr, a):
    H, T, _ = r.shape
    low_i, low_s = _tri_masks(T)
    nrm = jnp.sqrt(jnp.sum(kkr * kkr, axis=-1, keepdims=True))
    den = jnp.maximum(nrm, 1e-12)
    kk = kkr / den
    tri = jnp.broadcast_to(low_i.astype(bf16)[None], (H, T, T))
    cl = _tri_dot("hts,hsn->htn", tri, lw)
    c = jnp.exp(cl)
    cprev = jnp.exp(cl - lw)
    cinv = jnp.exp(-cl)
    bt, kt = _sp(kk * a * cinv), _sp(k * cinv)
    L = _cat([_sp(r * c), _sp(-kk * cprev)])
    Mb = _bmm("htn,hsn->hts", L, bt)
    Mk = _bmm("htn,hsn->hts", L, kt)
    A_rb = jnp.where(low_i[None], Mb[:, :T], 0.0)
    A_ab = jnp.where(low_s[None], Mb[:, T:], 0.0)
    Mk = jnp.concatenate([jnp.where(low_i[None], Mk[:, :T], 0.0), jnp.where(low_s[None], Mk[:, T:], 0.0)], axis=1)
    return dict(kk=kk, den=den, nrm=nrm, c=c, cprev=cprev, cinv=cinv, L=L, kt=kt, bt=bt,
                A_ab=A_ab, A_rb=A_rb, Mk=Mk, cT=c[:, T - 1:T, :])


def _tri_inverse(A):
    T = A.shape[-1]
    eye = (lax.broadcasted_iota(jnp.int32, (T, T), 0) == lax.broadcasted_iota(jnp.int32, (T, T), 1)).astype(f32)
    inv = eye[None] + A
    X = A
    n = 1
    while 2 * n < T:
        Xs = _sp(X)
        X = _bmm("hts,hsu->htu", Xs, Xs)
        inv = inv + _bmm("hts,hsu->htu", _sp(inv), _sp(X))
        n *= 2
    return inv


def _wkv_chunk_fwd(S0, r, lw, k, v, kkr, a):
    T = r.shape[1]
    q = _wkv_prep(r, lw, k, kkr, a)
    inv = _tri_inverse(q["A_ab"])
    vs = _sp(v)
    P = _bmm("htk,hvk->htv", q["L"], _sp(S0)) + _bmm("hts,hsv->htv", _sp(q["Mk"]), vs)
    U = _bmm("hts,hsv->htv", _sp(inv), _sp(P[:, T:]))
    Us = _sp(U)
    Y = P[:, :T] + _bmm("hts,hsv->htv", _sp(q["A_rb"]), Us)
    S1 = (S0 + _bmm("htv,htk->hvk", _cat([Us, vs]), _cat([q["bt"], q["kt"]]))) * q["cT"]
    return Y, U, inv, S1


def _wkv_chunk_bwd(S0, Hin, Q, r, lw, k, v, kkr, a, U, inv, dY):
    H, T, _ = r.shape
    low_i, low_s = _tri_masks(T)
    q = _wkv_prep(r, lw, k, kkr, a)
    L, kt, bt = q["L"], q["kt"], q["bt"]
    R = _cat([bt, kt])
    Hh = Hin * q["cT"]
    Hs, S0s, dYs, vs, Us = _sp(Hh), _sp(S0), _sp(dY), _sp(v), _sp(U)
    RH = _bmm("htk,hvk->htv", R, Hs)
    Z = _bmm("hst,hsv->htv", _sp(inv), _sp(RH[:, :T] + _bmm("hst,hsv->htv", _sp(q["A_rb"]), dYs)))
    DZ = _cat([dYs, _sp(Z)])
    both = jnp.concatenate([jnp.broadcast_to(low_i[None], (1, T, T)), jnp.broadcast_to(low_s[None], (1, T, T))], axis=1)
    NU = _sp(jnp.where(both, _bmm("htv,hsv->hts", DZ, Us), 0.0))
    NV = _sp(jnp.where(both, _bmm("htv,hsv->hts", DZ, vs), 0.0))
    ra = _bmm("htv,hvk->htk", DZ, S0s) + _bmm("hts,hsk->htk", NU, bt) + _bmm("hts,hsk->htk", NV, kt)
    dr = ra[:, :T] * q["c"]
    da = ra[:, T:] * q["cprev"]
    dv = RH[:, T:] + _bmm("hst,hsv->htv", _sp(q["Mk"]), DZ)
    VH = _bmm("htv,hvk->htk", _cat([vs, Us]), Hs)
    dk = (VH[:, :T] + _bmm("hst,hsk->htk", NV, L)) * q["cinv"]
    db = (VH[:, T:] + _bmm("hst,hsk->htk", NU, L)) * q["cinv"]
    H0 = Hh + _bmm("htv,htk->hvk", DZ, L)
    kk = q["kk"]
    e = r * dr - kk * a * db - k * dk
    f = -kk * da
    tri_i = jnp.broadcast_to(low_i.astype(bf16)[None], (H, T, T))
    tri_s = jnp.broadcast_to(low_s.astype(bf16)[None], (H, T, T))
    dlw = _tri_dot("hst,hsn->htn", tri_i, e) + _tri_dot("hst,hsn->htn", tri_s, f) + Q
    Qn = Q + jnp.sum(e + f, axis=1, keepdims=True)
    dkk = db * a - da
    dasig = db * kk
    proj = jnp.sum(dkk * kk, axis=-1, keepdims=True)
    dkkr = jnp.where(q["nrm"] > 1e-12, dkk - kk * proj, dkk) / q["den"]
    return dr, dlw, dk, dv, dkkr, dasig, H0, Qn


def _heads(ref, rows=slice(None)):
    return jnp.stack([ref[rows, h * HEAD:(h + 1) * HEAD] for h in range(RW_HEADS)], axis=0)


def _put_heads(ref, val, rows=slice(None)):
    for h in range(RW_HEADS):
        ref[rows, h * HEAD:(h + 1) * HEAD] = val[h]


def _wkv_fwd(r, lw, k, v, kkr, a, g, r_k, ln_w, ln_b, late_pack=None):
    S = r.shape[0]
    H, N, T = RW_HEADS, HEAD, WKV_T
    TS = T * WKV_SUB
    nc = S // TS
    hosting = late_pack is not None

    def body(r_ref, lw_ref, k_ref, v_ref, kkr_ref, a_ref, g_ref, rk_ref, lnw_ref, lnb_ref, *rest):
        if hosting:
            pack_ref, y_ref, yg_ref, wkv_ref, u_ref, inv_ref, s0_ref, gathered_ref, state, *sems = rest
            start, forward, finish = _gather_phases(pack_ref, gathered_ref, *sems)
            pl.when(pl.program_id(0) == 0)(start)
            pl.when(pl.program_id(0) == nc // 2)(forward)
        else:
            y_ref, yg_ref, wkv_ref, u_ref, inv_ref, s0_ref, state = rest

        @pl.when(pl.program_id(0) == 0)
        def _():
            state[...] = jnp.zeros(state.shape, f32)

        S0 = state[...]
        for c in range(WKV_SUB):
            rows = slice(c * T, (c + 1) * T)
            s0_ref[c] = S0
            rr, kk2, vv = _heads(r_ref, rows), _heads(k_ref, rows), _heads(v_ref, rows)
            Y, U, inv, S0 = _wkv_chunk_fwd(S0, rr, _heads(lw_ref, rows), kk2, vv, _heads(kkr_ref, rows),
                                           _heads(a_ref, rows))
            wkv_ref[:, rows, :] = Y
            u_ref[:, rows, :] = U
            inv_ref[:, rows, :] = inv
            mean = jnp.mean(Y, axis=-1, keepdims=True)
            var = jnp.mean(jnp.square(Y - mean), axis=-1, keepdims=True)
            yn = (Y - mean) * lax.rsqrt(var + GN_EPS)
            bonus = jnp.sum(rr * kk2 * rk_ref[...], axis=-1, keepdims=True) * vv
            _put_heads(y_ref, yn * lnw_ref[...] + lnb_ref[...] + bonus, rows)
        state[...] = S0
        yg_ref[...] = (y_ref[...] * g_ref[...]).astype(yg_ref.dtype)
        if hosting:
            pl.when(pl.program_id(0) == nc - 1)(finish)

    tok = pl.BlockSpec((TS, H * N), lambda i: (i, 0))
    blk = pl.BlockSpec((H, TS, N), lambda i: (0, i, 0))
    par = pl.BlockSpec((H, 1, N), lambda i: (0, 0, 0))
    hbm = pl.BlockSpec(memory_space=pl.ANY)
    seq = jax.ShapeDtypeStruct((H, S, N), f32)
    out_specs = [tok, tok, blk, blk, pl.BlockSpec((H, TS, T), lambda i: (0, i, 0)),
                 pl.BlockSpec((WKV_SUB, H, N, N), lambda i: (i, 0, 0, 0))]
    out_shape = [jax.ShapeDtypeStruct((S, H * N), f32), jax.ShapeDtypeStruct((S, H * N), bf16), seq, seq,
                 jax.ShapeDtypeStruct((H, S, T), f32),
                 jax.ShapeDtypeStruct((S // T, H, N, N), f32)]
    if hosting:
        out_specs.append(hbm)
        out_shape.append(jax.ShapeDtypeStruct((N_DEV,) + late_pack.shape, late_pack.dtype))
    return pl.pallas_call(
        body, name="wkv_fwd", grid=(nc,), in_specs=[tok] * 7 + [par] * 3 + [hbm] * hosting,
        out_specs=out_specs, out_shape=out_shape,
        scratch_shapes=[pltpu.VMEM((H, N, N), f32)] + (GATHER_SEMS if hosting else []),
        compiler_params=_cparams(("arbitrary",)),
    )(r, lw, k, v, kkr, a, g, r_k, ln_w, ln_b, *([late_pack] if hosting else []))


def _wkv_bwd(dy, g, r, lw, k, v, kkr, a, wkv, U, inv, S0s, r_k, ln_w, ln_b, late_pieces=None):
    S = r.shape[0]
    H, N, T = RW_HEADS, HEAD, WKV_T
    TS = T * WKV_SUB
    nc = S // TS
    hosting = late_pieces is not None
    n_late = len(late_pieces) if hosting else 0

    def body(dy_ref, g_ref, r_ref, lw_ref, k_ref, v_ref, kkr_ref, a_ref, wkv_ref, u_ref, inv_ref, s0_ref,
             rk_ref, lnw_ref, lnb_ref, *rest):
        if hosting:
            piece_refs, rest = rest[:n_late], rest[n_late:]
            (dr_ref, dlw_ref, dk_ref, dv_ref, dkkr_ref, da_ref, drk_ref, dlnw_ref, dlnb_ref, recv_ref,
             hst, qst, *sems) = rest
            start, finish = _direct_phases(piece_refs, [p.shape[1] for p in late_pieces], recv_ref, *sems)
            pl.when(pl.program_id(0) == 0)(start)
        else:
            dr_ref, dlw_ref, dk_ref, dv_ref, dkkr_ref, da_ref, drk_ref, dlnw_ref, dlnb_ref, hst, qst = rest

        @pl.when(pl.program_id(0) == 0)
        def _():
            hst[...] = jnp.zeros(hst.shape, f32)
            qst[...] = jnp.zeros(qst.shape, f32)
            drk_ref[...] = jnp.zeros(drk_ref.shape, f32)
            dlnw_ref[...] = jnp.zeros(dlnw_ref.shape, f32)
            dlnb_ref[...] = jnp.zeros(dlnb_ref.shape, f32)

        dyg = dy_ref[...] * g_ref[...]
        rk = rk_ref[...]
        Hst, Qst = hst[...], qst[...]
        for c in reversed(range(WKV_SUB)):
            rows = slice(c * T, (c + 1) * T)
            dya = _heads(dyg, rows)
            rr, kk2, vv, Y = _heads(r_ref, rows), _heads(k_ref, rows), _heads(v_ref, rows), wkv_ref[:, rows, :]
            s = jnp.sum(rr * kk2 * rk, axis=-1, keepdims=True)
            ds = jnp.sum(dya * vv, axis=-1, keepdims=True)
            mean = jnp.mean(Y, axis=-1, keepdims=True)
            var = jnp.mean(jnp.square(Y - mean), axis=-1, keepdims=True)
            rstd = lax.rsqrt(var + GN_EPS)
            yn = (Y - mean) * rstd
            dyn = dya * lnw_ref[...]
            dY = rstd * (dyn - jnp.mean(dyn, axis=-1, keepdims=True) - yn * jnp.mean(dyn * yn, axis=-1, keepdims=True))
            drk_ref[...] += jnp.sum(ds * rr * kk2, axis=1, keepdims=True)
            dlnw_ref[...] += jnp.sum(dya * yn, axis=1, keepdims=True)
            dlnb_ref[...] += jnp.sum(dya, axis=1, keepdims=True)
            dr, dlw, dk, dv, dkkr, dasig, Hst, Qst = _wkv_chunk_bwd(
                s0_ref[c], Hst, Qst, rr, _heads(lw_ref, rows), kk2, vv, _heads(kkr_ref, rows), _heads(a_ref, rows),
                u_ref[:, rows, :], inv_ref[:, rows, :], dY)
            _put_heads(dr_ref, dr + ds * kk2 * rk, rows)
            _put_heads(dlw_ref, dlw, rows)
            _put_heads(dk_ref, dk + ds * rr * rk, rows)
            _put_heads(dv_ref, dv + dya * s, rows)
            _put_heads(dkkr_ref, dkkr, rows)
            _put_heads(da_ref, dasig, rows)
        hst[...] = Hst
        qst[...] = Qst
        if hosting:
            pl.when(pl.program_id(0) == nc - 1)(finish)

    tok = pl.BlockSpec((TS, H * N), lambda i: (nc - 1 - i, 0))
    blk = pl.BlockSpec((H, TS, N), lambda i: (0, nc - 1 - i, 0))
    par = pl.BlockSpec((H, 1, N), lambda i: (0, 0, 0))
    hbm = pl.BlockSpec(memory_space=pl.ANY)
    seq = jax.ShapeDtypeStruct((S, H * N), f32)
    pout = jax.ShapeDtypeStruct((H, 1, N), f32)
    out_specs, out_shape = [tok] * 6 + [par] * 3, [seq] * 6 + [pout] * 3
    sems = []
    if hosting:
        rows_late = sum(p.shape[1] for p in late_pieces)
        out_specs.append(hbm)
        out_shape.append(jax.ShapeDtypeStruct((N_DEV - 1, rows_late, late_pieces[0].shape[2]), late_pieces[0].dtype))
        sems = [pltpu.SemaphoreType.DMA((n_late * (N_DEV - 1),))] * 2
    return pl.pallas_call(
        body, name="wkv_bwd", grid=(nc,),
        in_specs=([tok] * 8 + [blk] * 2 + [pl.BlockSpec((H, TS, T), lambda i: (0, nc - 1 - i, 0))]
                  + [pl.BlockSpec((WKV_SUB, H, N, N), lambda i: (nc - 1 - i, 0, 0, 0))]
                  + [par] * 3 + [hbm] * n_late),
        out_specs=out_specs, out_shape=out_shape,
        scratch_shapes=[pltpu.VMEM((H, N, N), f32), pltpu.VMEM((H, 1, N), f32)] + sems,
        compiler_params=_cparams(("arbitrary",)),
    )(dy, g, r, lw, k, v, kkr, a, wkv, U, inv, S0s, r_k, ln_w, ln_b, *(late_pieces if hosting else []))


ATTN_TT = 2048


def _attn_rows(d, i, j):
    return pl.ds(ATTN_BLK * d * i + j, ATTN_BLK, stride=d) if d > 1 else pl.ds(ATTN_BLK * i, ATTN_BLK)


def _attn_take(ref, d, nsub):
    return jnp.stack([ref[_attn_rows(d, i, j), :] for i in range(nsub) for j in range(d)], axis=0)


def _attn_put(ref, val, d):
    for i in range(val.shape[0] // d):
        for j in range(d):
            ref[_attn_rows(d, i, j), :] = val[i * d + j]


def _attn_prev(cur, before, d):
    return before if cur.shape[0] == d else jnp.concatenate([before, cur[:cur.shape[0] - d]], axis=0)


def _attn_specs(gi, d, nt, reverse):
    per_tile = ATTN_TT // (ATTN_BLK * d)

    def tile(n):
        return nt - 1 - n if reverse else n

    def col(kind):
        return lambda hp, n: (tile(n), kind * (ATTN_W // 128) + 2 * gi + hp)

    def col_before(kind):
        return lambda hp, n: (jnp.maximum(tile(n) * per_tile - 1, 0), kind * (ATTN_W // 128) + 2 * gi + hp)

    cur = [pl.BlockSpec((ATTN_TT, 128), col(kind)) for kind in range(3)]
    before = [pl.BlockSpec((ATTN_BLK * d, 128), col_before(kind)) for kind in (1, 2)]
    own = pl.BlockSpec((ATTN_TT, 128), lambda hp, n: (tile(n), hp))
    return cur, before, own, tile


def _attn_norm(x, gain, scale):
    rs = lax.rsqrt(jnp.mean(x * x, axis=-1, keepdims=True) + RMS_EPS)
    return x * rs * (gain * scale), rs


def _attn_scores(qn, kn_c, kn_p, first):
    s_c = jnp.einsum("gqe,gke->gqk", qn.astype(bf16), kn_c.astype(bf16), preferred_element_type=f32)
    s_p = jnp.einsum("gqe,gke->gqk", qn.astype(bf16), kn_p.astype(bf16), preferred_element_type=f32)
    qi = lax.broadcasted_iota(jnp.int32, (1, ATTN_BLK, ATTN_BLK), 1)
    ki = lax.broadcasted_iota(jnp.int32, (1, ATTN_BLK, ATTN_BLK), 2)
    s_c = jnp.where(qi >= ki, s_c, NEG_INF)
    s_p = jnp.where(jnp.logical_and(ki >= qi, jnp.logical_not(first)), s_p, NEG_INF)
    return s_c, s_p


def _attn_fwd(pqkv, qg, kg, gi, S):
    d = ATTN_PAIRS[gi][1]
    nt = S // ATTN_TT
    nsub = ATTN_TT // (ATTN_BLK * d)
    nd = nsub * d

    def body(q_ref, k_ref, v_ref, kb_ref, vb_ref, qg_ref, kg_ref, o_ref, lse_ref):
        Q, K, V = _attn_take(q_ref, d, nsub), _attn_take(k_ref, d, nsub), _attn_take(v_ref, d, nsub)
        KB, VB = _attn_take(kb_ref, d, 1), _attn_take(vb_ref, d, 1)
        first = jnp.logical_and(lax.broadcasted_iota(jnp.int32, (nd, 1, 1), 0) < d, pl.program_id(1) == 0)
        outs, lses = [], []
        for h in range(2):
            sl = slice(h * HEAD, (h + 1) * HEAD)
            kc, vc = K[:, :, sl], V[:, :, sl]
            kp, vp = _attn_prev(kc, KB[:, :, sl], d), _attn_prev(vc, VB[:, :, sl], d)
            qn, _ = _attn_norm(Q[:, :, sl], qg_ref[...], HEAD ** -0.5)
            kn_c, _ = _attn_norm(kc, kg_ref[...], 1.0)
            kn_p, _ = _attn_norm(kp, kg_ref[...], 1.0)
            s_c, s_p = _attn_scores(qn, kn_c, kn_p, first)
            m = jnp.maximum(jnp.max(s_c, axis=-1, keepdims=True), jnp.max(s_p, axis=-1, keepdims=True))
            p_c = jnp.exp(s_c - m)
            p_p = jnp.exp(s_p - m)
            den = jnp.sum(p_c, axis=-1, keepdims=True) + jnp.sum(p_p, axis=-1, keepdims=True)
            inv = 1.0 / den
            o = jnp.einsum("gqk,gke->gqe", (p_c * inv).astype(bf16), vc.astype(bf16), preferred_element_type=f32)
            o += jnp.einsum("gqk,gke->gqe", (p_p * inv).astype(bf16), vp.astype(bf16), preferred_element_type=f32)
            outs.append(o)
            lses.append(jnp.broadcast_to(m + jnp.log(den), o.shape))
        _attn_put(o_ref, jnp.concatenate(outs, axis=-1), d)
        _attn_put(lse_ref, jnp.concatenate(lses, axis=-1), d)

    cur, before, own, _ = _attn_specs(gi, d, nt, False)
    par = pl.BlockSpec((1, HEAD), lambda hp, n: (0, 0))
    shp = jax.ShapeDtypeStruct((S, 2 * 128), f32)
    return pl.pallas_call(
        body, name=f"attn_fwd{gi}", grid=(2, nt), in_specs=cur + before + [par] * 2, out_specs=[own, own],
        out_shape=[shp, shp], compiler_params=_cparams(("arbitrary", "arbitrary")),
    )(pqkv, pqkv, pqkv, pqkv, pqkv, qg, kg)


def _attn_bwd(pqkv, o, lse, do, dlse, qg, kg, gi, S):
    d = ATTN_PAIRS[gi][1]
    nt = S // ATTN_TT
    nsub = ATTN_TT // (ATTN_BLK * d)
    nd = nsub * d

    def norm_bwd(dxn, x, rs, gain, scale):
        xh = x * rs
        dxh = dxn * (gain * scale)
        dx = rs * (dxh - xh * jnp.mean(dxh * xh, axis=-1, keepdims=True))
        dgain = jnp.sum(jnp.sum(dxn * xh * scale, axis=1), axis=0, keepdims=True)
        return dx, dgain

    def to_before(part, carried):
        return carried if nsub == 1 else jnp.concatenate([part[d:], carried], axis=0)

    def body(q_ref, k_ref, v_ref, kb_ref, vb_ref, o_ref, lse_ref, do_ref, dlse_ref, qg_ref, kg_ref,
             dq_ref, dk_ref, dv_ref, dqg_ref, dkg_ref, carry_k, carry_v):
        step = pl.program_id(1)

        @pl.when(jnp.logical_and(pl.program_id(0) == 0, step == 0))
        def _():
            dqg_ref[...] = jnp.zeros(dqg_ref.shape, f32)
            dkg_ref[...] = jnp.zeros(dkg_ref.shape, f32)

        @pl.when(step == 0)
        def _():
            carry_k[...] = jnp.zeros(carry_k.shape, f32)
            carry_v[...] = jnp.zeros(carry_v.shape, f32)

        Q, K, V = _attn_take(q_ref, d, nsub), _attn_take(k_ref, d, nsub), _attn_take(v_ref, d, nsub)
        KB, VB = _attn_take(kb_ref, d, 1), _attn_take(vb_ref, d, 1)
        O, LSE = _attn_take(o_ref, d, nsub), _attn_take(lse_ref, d, nsub)
        DO, DLSE = _attn_take(do_ref, d, nsub), _attn_take(dlse_ref, d, nsub)
        first = jnp.logical_and(lax.broadcasted_iota(jnp.int32, (nd, 1, 1), 0) < d, step == nt - 1)
        qg, kg = qg_ref[...], kg_ref[...]
        dqs, dks, dvs = [], [], []
        for h in range(2):
            sl = slice(h * HEAD, (h + 1) * HEAD)
            qx, kx, vc = Q[:, :, sl], K[:, :, sl], V[:, :, sl]
            kpx, vp = _attn_prev(kx, KB[:, :, sl], d), _attn_prev(vc, VB[:, :, sl], d)
            qn, rq = _attn_norm(qx, qg, HEAD ** -0.5)
            kn_c, rk_c = _attn_norm(kx, kg, 1.0)
            kn_p, _ = _attn_norm(kpx, kg, 1.0)
            s_c, s_p = _attn_scores(qn, kn_c, kn_p, first)
            lse = LSE[:, :, h * HEAD:h * HEAD + 1]
            p_c = jnp.exp(s_c - lse)
            p_p = jnp.exp(s_p - lse)
            dO = DO[:, :, sl]
            dOb = dO.astype(bf16)
            dp_c = jnp.einsum("gqe,gke->gqk", dOb, vc.astype(bf16), preferred_element_type=f32)
            dp_p = jnp.einsum("gqe,gke->gqk", dOb, vp.astype(bf16), preferred_element_type=f32)
            corr = DLSE[:, :, h * HEAD:h * HEAD + 1] - jnp.sum(dO * O[:, :, sl], axis=-1, keepdims=True)
            ds_c = (p_c * (dp_c + corr)).astype(bf16)
            ds_p = (p_p * (dp_p + corr)).astype(bf16)
            qnb = qn.astype(bf16)
            dqn = (jnp.einsum("gqk,gke->gqe", ds_c, kn_c.astype(bf16), preferred_element_type=f32)
                   + jnp.einsum("gqk,gke->gqe", ds_p, kn_p.astype(bf16), preferred_element_type=f32))
            dkn_p = jnp.einsum("gqk,gqe->gke", ds_p, qnb, preferred_element_type=f32)
            dv_p = jnp.einsum("gqk,gqe->gke", p_p.astype(bf16), dOb, preferred_element_type=f32)
            dkn = jnp.einsum("gqk,gqe->gke", ds_c, qnb, preferred_element_type=f32) + to_before(dkn_p, carry_k[h])
            dv = (jnp.einsum("gqk,gqe->gke", p_c.astype(bf16), dOb, preferred_element_type=f32)
                  + to_before(dv_p, carry_v[h]))
            carry_k[h] = dkn_p[:d]
            carry_v[h] = dv_p[:d]
            dq, dqg = norm_bwd(dqn, qx, rq, qg, HEAD ** -0.5)
            dk, dkg = norm_bwd(dkn, kx, rk_c, kg, 1.0)
            dqg_ref[...] += dqg
            dkg_ref[...] += dkg
            dqs.append(dq)
            dks.append(dk)
            dvs.append(dv)
        _attn_put(dq_ref, jnp.concatenate(dqs, axis=-1), d)
        _attn_put(dk_ref, jnp.concatenate(dks, axis=-1), d)
        _attn_put(dv_ref, jnp.concatenate(dvs, axis=-1), d)

    cur, before, own, _ = _attn_specs(gi, d, nt, True)
    par = pl.BlockSpec((1, HEAD), lambda hp, n: (0, 0))
    shp = jax.ShapeDtypeStruct((S, 2 * 128), f32)
    pshp = jax.ShapeDtypeStruct((1, HEAD), f32)
    return pl.pallas_call(
        body, name=f"attn_bwd{gi}", grid=(2, nt), in_specs=cur + before + [own] * 4 + [par] * 2,
        out_specs=[own] * 3 + [par] * 2, out_shape=[shp] * 3 + [pshp] * 2,
        scratch_shapes=[pltpu.VMEM((2, d, ATTN_BLK, HEAD), f32)] * 2,
        compiler_params=_cparams(("arbitrary", "arbitrary")),
    )(pqkv, pqkv, pqkv, pqkv, pqkv, o, lse, do, dlse, qg, kg)


def _rms(x, g):
    rs = lax.rsqrt(jnp.mean(x * x, axis=-1, keepdims=True) + RMS_EPS)
    return x * rs * g


def _f_rms(x, g):
    return _rms(x, g)


def _f_resid_rms(coef, x, f, g):
    xn = x + coef * f
    return xn, _rms(xn, g)


def _f_rms_bwd(n_parts, *args):
    dns = args[:n_parts]
    x, dres, g = args[n_parts:]
    dn = dns[0]
    for t in dns[1:]:
        dn = dn + t
    rs = lax.rsqrt(jnp.mean(x * x, axis=-1, keepdims=True) + RMS_EPS)
    xh = x * rs
    dxh = dn * g
    dx = dres + rs * (dxh - xh * jnp.mean(dxh * xh, axis=-1, keepdims=True))
    return dx, dx, jnp.sum(dn * xh, axis=0, keepdims=True)


def _f_loss(x, f, tgt):
    y = x + 0.5 * f
    diff = y - tgt
    part = 0.5 * jnp.sum(jnp.mean(diff * diff, axis=-1, keepdims=True), axis=0, keepdims=True)
    dy = diff * (1.0 / D)
    return dy, dy, jnp.broadcast_to(part, (1, 128))


def _dotb(a, b, dims):
    return lax.dot_general(a.astype(bf16), b.astype(bf16), dims, preferred_element_type=f32)


_NN = (((1,), (0,)), ((), ()))
_NT = (((1,), (1,)), ((), ()))
_TN = (((0,), (0,)), ((), ()))


def _rwkv_pre_core(prkv, prkv_prev, plora, plora_prev, mu_rkv, mu_lora, w0, w2p, a0, a2p, g2p, k_k, k_a):
    xs = prkv + (prkv_prev - prkv) * mu_rkv
    xl = plora + (plora_prev - plora) * mu_lora
    r, k, v = xs[:, :D], xs[:, D:2 * D], xs[:, 2 * D:]
    wd, ad, gd = xl[:, :128], xl[:, 128:256], xl[:, 256:]
    tw = jnp.tanh(wd)
    zw = w0 + _dotb(tw, w2p, _NN)
    sp = jnp.maximum(-zw, 0.0) + jnp.log(1.0 + jnp.exp(-jnp.abs(zw)))
    lw = -jnp.exp(-sp - 0.5)
    a = jax.nn.sigmoid(a0 + _dotb(ad, a2p, _NN))
    sg = jax.nn.sigmoid(gd)
    return dict(r=r, k=k, v=v, tw=tw, zw=zw, lw=lw, a=a, sg=sg, ad=ad)


def _rows_down(x, halo, blk):
    before = jnp.where(blk > 0, halo[HALO - 1:HALO, :], 0.0)
    row = lax.broadcasted_iota(jnp.int32, (x.shape[0], 1), 0)
    return jnp.where(row == 0, before, pltpu.roll(x, 1, 0))


def _rows_up(x, after):
    n = x.shape[0]
    row = lax.broadcasted_iota(jnp.int32, (n, 1), 0)
    return jnp.where(row == n - 1, after, pltpu.roll(x, n - 1, 0))


def _f_rwkv_pre(prkv, plora, mu_rkv, mu_lora, w0, w2p, a0, a2p, g2p, k_k, k_a, halo_rkv, halo_lora, blk):
    c = _rwkv_pre_core(prkv, _rows_down(prkv, halo_rkv, blk), plora, _rows_down(plora, halo_lora, blk),
                       mu_rkv, mu_lora, w0, w2p, a0, a2p, g2p, k_k, k_a)
    g = _dotb(c["sg"], g2p, _NN)
    k, a = c["k"], c["a"]
    return c["r"], c["lw"], k * (1.0 + (a - 1.0) * k_a), c["v"], k * k_k, a, g


def _f_rwkv_pre_bwd(prkv, plora, dr, dlw, dk2, dv, dkkr, da, dya, yap,
                    mu_rkv, mu_lora, w0, w2p, a0, a2p, g2p, k_k, k_a, halo_rkv, halo_lora, next_rkv, next_lora, blk):
    prkv_prev, plora_prev = _rows_down(prkv, halo_rkv, blk), _rows_down(plora, halo_lora, blk)
    c = _rwkv_pre_core(prkv, prkv_prev, plora, plora_prev, mu_rkv, mu_lora, w0, w2p, a0, a2p, g2p, k_k, k_a)
    k, a, sg, tw, zw, lw = c["k"], c["a"], c["sg"], c["tw"], c["zw"], c["lw"]
    dg = dya * yap
    dsg = _dotb(dg, g2p, _NT)
    dgd = dsg * sg * (1.0 - sg)
    dg2p = _dotb(sg, dg, _TN)
    dk = dk2 * (1.0 + (a - 1.0) * k_a) + dkkr * k_k
    da_t = da + dk2 * k * k_a
    dk_a = jnp.sum(dk2 * k * (a - 1.0), axis=0, keepdims=True)
    dk_k = jnp.sum(dkkr * k, axis=0, keepdims=True)
    dza = da_t * a * (1.0 - a)
    da0 = jnp.sum(dza, axis=0, keepdims=True)
    dad = _dotb(dza, a2p, _NT)
    da2p = _dotb(c["ad"], dza, _TN)
    dzw = dlw * lw * jax.nn.sigmoid(-zw)
    dw0 = jnp.sum(dzw, axis=0, keepdims=True)
    dtw = _dotb(dzw, w2p, _NT)
    dw2p = _dotb(tw, dzw, _TN)
    dwd = dtw * (1.0 - tw * tw)
    dxs = jnp.concatenate([dr, dk, dv], axis=1)
    dxl = jnp.concatenate([dwd, dad, dgd], axis=1)
    dmu_rkv = jnp.sum(dxs * (prkv_prev - prkv), axis=0, keepdims=True)
    dmu_lora = jnp.sum(dxl * (plora_prev - plora), axis=0, keepdims=True)
    to_next_rkv, to_next_lora = dxs * mu_rkv, dxl * mu_lora
    return (dxs * (1.0 - mu_rkv) + _rows_up(to_next_rkv, next_rkv), dxl * (1.0 - mu_lora) + _rows_up(to_next_lora, next_lora),
            dmu_rkv, dmu_lora, dw0, da0, dk_k, dk_a, dw2p, da2p, dg2p, to_next_rkv[0:1], to_next_lora[0:1])


def _group_alpha(l0, l1, l2):
    m = jnp.maximum(jnp.maximum(l0, l1), l2)
    e0, e1, e2 = jnp.exp(l0 - m), jnp.exp(l1 - m), jnp.exp(l2 - m)
    inv = 1.0 / (e0 + e1 + e2)
    return jnp.concatenate([e0 * inv, e1 * inv, e2 * inv], axis=1)


def _f_combine(o0, o1, o2, l0, l1, l2):
    return jnp.concatenate([o0, o1, o2], axis=1) * _group_alpha(l0, l1, l2)


def _f_combine_bwd(dyb, o0, o1, o2, l0, l1, l2, bd):
    alpha = _group_alpha(l0, l1, l2)
    hi, lo = _sp(dyb * jnp.concatenate([o0, o1, o2], axis=1))
    ones = bd.astype(bf16)
    e = jnp.dot(hi, ones, preferred_element_type=f32) + jnp.dot(lo, ones, preferred_element_type=f32)
    ae = alpha * e
    tot = ae[:, :256] + ae[:, 256:512] + ae[:, 512:]
    do = dyb * alpha
    dl = ae - alpha * jnp.concatenate([tot, tot, tot], axis=1)
    return do[:, :256], do[:, 256:512], do[:, 512:], dl[:, :256], dl[:, 256:512], dl[:, 512:]


def _f_merge(pgate, ta, tb, b_gate):
    gate = jax.nn.sigmoid(pgate + b_gate)
    return gate[:, :D] * ta + gate[:, D:] * tb


def _f_merge_bwd(dm, pgate, ta, tb, b_gate):
    gate = jax.nn.sigmoid(pgate + b_gate)
    ga, gb = gate[:, :D], gate[:, D:]
    dpg = jnp.concatenate([dm * ta * ga * (1.0 - ga), dm * tb * gb * (1.0 - gb)], axis=1)
    return dm * ga, dm * gb, dpg, jnp.sum(dpg, axis=0, keepdims=True)


def _f_adamw(w, g, m, v):
    m2 = ADAM_B1 * m + (1.0 - ADAM_B1) * g
    v2 = ADAM_B2 * v + (1.0 - ADAM_B2) * jnp.square(g)
    m_hat = m2 / (1.0 - ADAM_B1 ** ADAM_STEP)
    v_hat = v2 / (1.0 - ADAM_B2 ** ADAM_STEP)
    delta = -ADAM_LR * (m_hat / (jnp.sqrt(v_hat) + ADAM_EPS) + ADAM_WD * w)
    return delta, m2, v2


def _ffn_bwd(tag, dxo, dxo_b, x_in, n, gate, up, act, g, WiT, Wo, cross=None):
    du = _ffn_dact(f"{tag}_dact", dxo_b, Wo, gate, up)
    dWo = _mm(f"{tag}_dwo", act, dxo_b, "tn", out_dtype=GRAD_WIRE, scale=0.5)
    if cross is None:
        dn, recv = _mm(f"{tag}_dn", du, WiT, "nn"), None
    else:
        dn, recv = _mm(f"{tag}_dn", du, WiT, "nn", cross=cross)
    dWiT = _mm(f"{tag}_dwi", du, n, "tn", out_dtype=GRAD_WIRE)
    dx, dx_b, dg = _rowwise(f"{tag}_drms", functools.partial(_f_rms_bwd, 1), [dn, x_in, dxo], [g],
                            [(D, f32), (D, bf16)], [(1, D)])
    return dx, dx_b, dg, dWiT, dWo, recv


def _local_step(x0, tgt, W, P, hooks=None):
    S = x0.shape[0]
    (n1,) = _rowwise("f1_rms", _f_rms, [x0], [P["ffn1_norm"]], [(D, bf16)])
    hooks = hooks or {}
    if "gather_mid" in hooks:
        pack, weights = hooks["gather_mid"]
        gate1, up1, act1, gathered = _ffn_up("f1_up", n1, W["f1_iT"], gather=pack)
        W = {**W, **weights(gathered)}
    else:
        gate1, up1, act1 = _ffn_up("f1_up", n1, W["f1_iT"])
    if "gather_in" in hooks:
        pack, weights = hooks["gather_in"]
        f1, gathered = _mm("f1_down", act1, W["f1_o"], "nn", gather=pack)
        W = {**W, **weights(gathered)}
    else:
        f1 = _mm("f1_down", act1, W["f1_o"], "nn")
    x1, h = _rowwise("mix_rms", functools.partial(_f_resid_rms, 0.5), [x0, f1], [P["mix_norm"]],
                     [(D, f32), (D, bf16)])
    prkv = _mm("p_rkv", h, W["in_rkvT"], "nt")
    plora = _mm("p_lora", h, W["in_loraT"], "nt")
    pqkv = _mm("p_qkv", h, W["in_qkvT"], "nt")
    pgate = _mm("p_gate", h, W["in_gateT"], "nt")
    pre_params = [P["mu_rkv"], P["mu_lora"], P["w0"], W["w2p"], P["a0"], W["a2p"], W["g2p"], P["k_k"], P["k_a"]]
    r, lw, k2, v, kkr, a, g = _rowwise("rwkv_pre", _f_rwkv_pre, [prkv, plora], pre_params, [(D, f32)] * 7, tm=128,
                                       halos=(0, 1))
    hm = [r, lw, k2, v, kkr, a]
    hp = [P["r_k"].reshape(RW_HEADS, 1, HEAD), P["ln_w"].reshape(RW_HEADS, 1, HEAD), P["ln_b"].reshape(RW_HEADS, 1, HEAD)]
    if "gather_late" in hooks:
        pack, weights = hooks["gather_late"]
        yap, ya, wkv_h, U_h, inv_h, S0s, gathered = _wkv_fwd(*hm, g, *hp, late_pack=pack)
        W = {**W, **weights(gathered)}
    else:
        yap, ya, wkv_h, U_h, inv_h, S0s = _wkv_fwd(*hm, g, *hp)
    ta = _mm("proj_a", ya, W["pr"], "nn")
    n_grp = len(ATTN_PAIRS)
    attn = [_attn_fwd(pqkv, P["q_norm"], P["k_norm"], gi, S) for gi in range(n_grp)]
    o_g, lse_g = [t[0] for t in attn], [t[1] for t in attn]
    (yb,) = _rowwise("attn_combine", _f_combine, [*o_g, *lse_g], [], [(ATTN_W, bf16)])
    tb = _mm("proj_b", yb, W["paT"], "nt")
    (merged,) = _rowwise("merge", _f_merge, [pgate, ta, tb], [P["b_gate"]], [(D, bf16)])
    mo = _mm("mix_out", merged, W["out"], "nn")
    x2, n2 = _rowwise("f2_rms", functools.partial(_f_resid_rms, 1.0), [x1, mo], [P["ffn2_norm"]],
                      [(D, f32), (D, bf16)])
    gate2, up2, act2 = _ffn_up("f2_up", n2, W["f2_iT"])
    f2 = _mm("f2_down", act2, W["f2_o"], "nn")
    dx3, dx3_b, loss = _rowwise("loss", _f_loss, [x2, f2, tgt], [], [(D, f32), (D, bf16)], [(1, 128)])
    G, Gs = {}, {}
    dx2, dx2_b, Gs["ffn2_norm"], G["f2_iT"], G["f2_o"], _ = _ffn_bwd("f2", dx3, dx3_b, x2, n2, gate2, up2, act2,
                                                                    P["ffn2_norm"], W["f2_iT"], W["f2_o"])
    dmerged = _mm("d_merged", dx2_b, W["out"], "nt")
    G["out"] = _mm("dw_out", merged, dx2_b, "tn", out_dtype=GRAD_WIRE)
    dta, dtb, dpgate, Gs["b_gate"] = _rowwise("merge_bwd", _f_merge_bwd, [dmerged, pgate, ta, tb], [P["b_gate"]],
                                              [(D, bf16), (D, bf16), (2 * D, bf16)], [(1, 2 * D)])
    dya = _mm("d_ya", dta, W["pr"], "nt")
    G["pr"] = _mm("dw_pr", ya, dta, "tn", out_dtype=GRAD_WIRE)
    dyb = _mm("d_yb", dtb, W["paT"], "nn")
    G["paT"] = _mm("dw_pa", dtb, yb, "tn", out_dtype=GRAD_WIRE)
    if "reduce_late" in hooks:
        pieces_late = hooks["reduce_late"](G)
        hg = _wkv_bwd(dya, g, *hm, wkv_h, U_h, inv_h, S0s, *hp, late_pieces=pieces_late)
        G["late"] = (pieces_late, hg[9])
    else:
        hg = _wkv_bwd(dya, g, *hm, wkv_h, U_h, inv_h, S0s, *hp)
    dr, dlw, dk2, dv, dkkr, da = hg[:6]
    Gs["r_k"], Gs["ln_w"], Gs["ln_b"] = (t.reshape(1, D) for t in hg[6:9])
    lp = sum(LORA_PAD)
    (dprkv, dplora, Gs["mu_rkv"], Gs["mu_lora"], Gs["w0"], Gs["a0"], Gs["k_k"], Gs["k_a"],
     dw2p, da2p, dg2p) = _rowwise(
        "rwkv_pre_bwd", _f_rwkv_pre_bwd,
        [prkv, plora, dr, dlw, dk2, dv, dkkr, da, dya, yap], pre_params,
        [(3 * D, bf16), (lp, bf16)],
        [(1, 3 * D), (1, lp), (1, D), (1, D), (1, D), (1, D), (LORA_PAD[0], D), (LORA_PAD[1], D), (LORA_PAD[2], D)],
        tm=128, halos=(0, 1), carries=((1, 3 * D), (1, lp)), reverse=True)
    G["w2T"], G["a2T"], G["g2T"] = dw2p[:LORA_W[0]].T, da2p[:LORA_W[1]].T, dg2p[:LORA_W[2]].T
    bd = (jnp.arange(ATTN_W)[:, None] // HEAD == jnp.arange(ATTN_W)[None, :] // HEAD).astype(f32)
    dol = _rowwise("attn_combine_bwd", _f_combine_bwd, [dyb, *o_g, *lse_g], [bd], [(ATTN_W // n_grp, f32)] * (2 * n_grp))
    dattn = [_attn_bwd(pqkv, o_g[gi], lse_g[gi], dol[gi], dol[n_grp + gi], P["q_norm"], P["k_norm"], gi, S)
             for gi in range(n_grp)]
    Gs["q_norm"] = dattn[0][3] + dattn[1][3] + dattn[2][3]
    Gs["k_norm"] = dattn[0][4] + dattn[1][4] + dattn[2][4]
    dpqkv = jnp.concatenate([dattn[gi][kind] for kind in range(3) for gi in range(n_grp)], axis=1).astype(bf16)
    dh = [_mm("dh_rkv", dprkv, W["in_rkvT"], "nn"), _mm("dh_lora", dplora, W["in_loraT"], "nn"),
          _mm("dh_qkv", dpqkv, W["in_qkvT"], "nn"), _mm("dh_gate", dpgate, W["in_gateT"], "nn")]
    dW_rkv = _mm("dw_rkv", dprkv, h, "tn", out_dtype=GRAD_WIRE)
    dW_lora = _mm("dw_lora", dplora, h, "tn", out_dtype=GRAD_WIRE)
    dW_qkv = _mm("dw_qkv", dpqkv, h, "tn", out_dtype=GRAD_WIRE)
    dW_gate = _mm("dw_gate", dpgate, h, "tn", out_dtype=GRAD_WIRE)
    o1, o2 = LORA_PAD[0], LORA_PAD[0] + LORA_PAD[1]
    G["inT"] = jnp.concatenate([dW_rkv, dW_lora[:LORA_W[0]], dW_lora[o1:o1 + LORA_W[1]], dW_lora[o2:o2 + LORA_W[2]],
                                dW_qkv, dW_gate], axis=0)
    dx1, dx1_b, Gs["mix_norm"] = _rowwise("mix_drms", functools.partial(_f_rms_bwd, 4), [*dh, x1, dx2],
                                          [P["mix_norm"]], [(D, f32), (D, bf16)], [(1, D)])
    part_mid = hooks["reduce_mid"](G) if "reduce_mid" in hooks else None
    dx0, _, Gs["ffn1_norm"], G["f1_iT"], G["f1_o"], recv_mid = _ffn_bwd(
        "f1", dx1, dx1_b, x0, n1, gate1, up1, act1, P["ffn1_norm"], W["f1_iT"], W["f1_o"], cross=part_mid)
    G["mid"] = (part_mid, recv_mid)
    return loss[0, 0], dx0, G, Gs


def _peer(k):
    x, y, c = lax.axis_index("x"), lax.axis_index("y"), lax.axis_index("c")
    px = 1 - x if k & 4 else x
    py = 1 - y if k & 2 else y
    pc = 1 - c if k & 1 else c
    return (px, py, pc), 4 * px + 2 * py + pc


def _gather_phases(x_ref, out_ref, send_sems, recv_sems, local_sem):
    x, y, c = lax.axis_index("x"), lax.axis_index("y"), lax.axis_index("c")
    me, sibling = (x, y, c), (x, y, 1 - c)
    chips = [(1 - x, y), (x, 1 - y), (1 - x, 1 - y)]

    def slot(px, py, pc):
        return out_ref.at[4 * px + 2 * py + pc]

    def copy(k, block, to, src=None):
        return pltpu.make_async_remote_copy(
            src_ref=slot(*block) if src is None else src, dst_ref=slot(*block), send_sem=send_sems.at[k],
            recv_sem=recv_sems.at[k], device_id=to, device_id_type=MESH)

    def mine():
        return pltpu.make_async_copy(x_ref, slot(*me), local_sem)

    def first():
        return [copy(0, me, sibling, src=x_ref)] + [copy(1 + j, me, (*chip, c), src=x_ref) for j, chip in enumerate(chips)]

    def passed():
        return [copy(4 + j, (*chip, c), sibling) for j, chip in enumerate(chips)]

    def start():
        mine().start()
        for cp in first():
            cp.start()

    def forward():
        for j, (chip, cp) in enumerate(zip(chips, passed())):
            copy(1 + j, (*chip, c), me).wait_recv()
            cp.start()

    def finish():
        copy(0, sibling, me).wait_recv()
        for j, chip in enumerate(chips):
            copy(4 + j, (*chip, 1 - c), me).wait_recv()
        for cp in first() + passed():
            cp.wait_send()
        mine().wait()

    return start, forward, finish


GATHER_SEMS = [pltpu.SemaphoreType.DMA((N_DEV - 1,)), pltpu.SemaphoreType.DMA((N_DEV - 1,)), pltpu.SemaphoreType.DMA(())]


def _all_gather(pack):
    R, C = pack.shape

    def body(x_ref, out_ref, send_sems, recv_sems, local_sem):
        for phase in _gather_phases(x_ref, out_ref, send_sems, recv_sems, local_sem):
            phase()

    return pl.pallas_call(
        body, name="weight_all_gather", out_shape=jax.ShapeDtypeStruct((N_DEV, R, C), pack.dtype),
        in_specs=[pl.BlockSpec(memory_space=pl.ANY)], out_specs=pl.BlockSpec(memory_space=pl.ANY),
        scratch_shapes=GATHER_SEMS,
    )(pack)


def _cross_phases(p_ref, out_ref, send_sems, recv_sems):
    x, y, c = lax.axis_index("x"), lax.axis_index("y"), lax.axis_index("c")

    def copies():
        out = []
        for j, (fx, fy) in enumerate([(1, 0), (0, 1), (1, 1)]):
            px = 1 - x if fx else x
            py = 1 - y if fy else y
            out.append(pltpu.make_async_remote_copy(src_ref=p_ref.at[2 * px + py], dst_ref=out_ref.at[j],
                                                    send_sem=send_sems.at[j], recv_sem=recv_sems.at[j],
                                                    device_id=(px, py, c), device_id_type=MESH))
        return out

    def start():
        for cp in copies():
            cp.start()

    def finish():
        for cp in copies():
            cp.wait()

    return start, finish


CROSS_SEMS = [pltpu.SemaphoreType.DMA((3,)), pltpu.SemaphoreType.DMA((3,))]


def _direct_phases(piece_refs, rows, out_ref, send_sems, recv_sems):
    offs = [sum(rows[:i]) for i in range(len(rows))]

    def copies():
        out = []
        for i, g_ref in enumerate(piece_refs):
            for k in range(1, N_DEV):
                dev, idx = _peer(k)
                out.append(pltpu.make_async_remote_copy(
                    src_ref=g_ref.at[idx], dst_ref=out_ref.at[k - 1, pl.ds(offs[i], rows[i])],
                    send_sem=send_sems.at[i * (N_DEV - 1) + k - 1], recv_sem=recv_sems.at[i * (N_DEV - 1) + k - 1],
                    device_id=dev, device_id_type=MESH))
        return out

    def start():
        for cp in copies():
            cp.start()

    def finish():
        for cp in copies():
            cp.wait()

    return start, finish


def _sum_direct(pieces, recv, me, tag):
    n = len(pieces)
    C = pieces[0].shape[2]
    nblk = [p.shape[1] // PACK_BLOCK for p in pieces]
    lo = [sum(nblk[:i]) for i in range(n)]
    R = sum(nblk) * PACK_BLOCK

    def body(me_ref, *refs):
        g_refs, r_ref, o_ref = refs[:n], refs[n], refs[n + 1]
        rb = pl.program_id(0)
        for i in range(n):
            @pl.when(jnp.logical_and(rb >= lo[i], rb < lo[i] + nblk[i]))
            def _(g_ref=g_refs[i]):
                acc = g_ref[...].astype(f32)
                for k in range(N_DEV - 1):
                    acc = acc + r_ref[k].astype(f32)
                o_ref[...] = acc

    def piece_spec(i):
        return pl.BlockSpec((None, PACK_BLOCK, C), lambda rb, me_ref: (me_ref[0], jnp.clip(rb - lo[i], 0, nblk[i] - 1), 0))

    return pl.pallas_call(
        body, name=f"grad_sum_{tag}",
        grid_spec=pltpu.PrefetchScalarGridSpec(
            num_scalar_prefetch=1, grid=(R // PACK_BLOCK,),
            in_specs=[piece_spec(i) for i in range(n)] + [pl.BlockSpec((N_DEV - 1, PACK_BLOCK, C), lambda rb, me_ref: (0, rb, 0))],
            out_specs=pl.BlockSpec((PACK_BLOCK, C), lambda rb, me_ref: (rb, 0))),
        out_shape=jax.ShapeDtypeStruct((R, C), f32),
        compiler_params=_cparams(("arbitrary",)),
    )(me, *pieces, recv)


N_CHIP = 4


def _grad_pair(pieces, tag):
    n = len(pieces)
    C = pieces[0].shape[2]
    rows = [p.shape[1] for p in pieces]
    offs = [sum(rows[:i]) for i in range(n)]
    R = sum(rows)

    def body(*refs):
        g_refs, (other_ref, send_sems, recv_sems) = refs[:n], refs[n:]
        x, y, c = lax.axis_index("x"), lax.axis_index("y"), lax.axis_index("c")
        copies = []
        for i, g_ref in enumerate(g_refs):
            for k in range(N_CHIP):
                cp = pltpu.make_async_remote_copy(
                    src_ref=g_ref.at[4 * (k // 2) + 2 * (k % 2) + 1 - c], dst_ref=other_ref.at[k, pl.ds(offs[i], rows[i])],
                    send_sem=send_sems.at[i * N_CHIP + k], recv_sem=recv_sems.at[i * N_CHIP + k],
                    device_id=(x, y, 1 - c), device_id_type=MESH)
                cp.start()
                copies.append(cp)
        for cp in copies:
            cp.wait()

    return pl.pallas_call(
        body, name=f"grad_pair_{tag}", out_shape=jax.ShapeDtypeStruct((N_CHIP, R, C), pieces[0].dtype),
        in_specs=[pl.BlockSpec(memory_space=pl.ANY)] * n, out_specs=pl.BlockSpec(memory_space=pl.ANY),
        scratch_shapes=[pltpu.SemaphoreType.DMA((n * N_CHIP,))] * 2,
    )(*pieces)


def _pair_add(pieces, other, c, tag):
    n = len(pieces)
    C = pieces[0].shape[2]
    nblk = [p.shape[1] // PACK_BLOCK for p in pieces]
    lo = [sum(nblk[:i]) for i in range(n)]
    R = sum(nblk) * PACK_BLOCK

    def body(c_ref, *refs):
        g_refs, o_ref, out_ref = refs[:n], refs[n], refs[n + 1]
        rb = pl.program_id(1)
        for i in range(n):
            @pl.when(jnp.logical_and(rb >= lo[i], rb < lo[i] + nblk[i]))
            def _(g_ref=g_refs[i]):
                out_ref[...] = (g_ref[...].astype(f32) + o_ref[...].astype(f32)).astype(out_ref.dtype)

    def piece_spec(i):
        return pl.BlockSpec((1, None, PACK_BLOCK, C),
                            lambda k, rb, c_ref: (k, c_ref[0], jnp.clip(rb - lo[i], 0, nblk[i] - 1), 0))

    blk = pl.BlockSpec((1, PACK_BLOCK, C), lambda k, rb, c_ref: (k, rb, 0))
    return pl.pallas_call(
        body, name=f"pair_add_{tag}",
        grid_spec=pltpu.PrefetchScalarGridSpec(
            num_scalar_prefetch=1, grid=(N_CHIP, R // PACK_BLOCK),
            in_specs=[piece_spec(i) for i in range(n)] + [blk], out_specs=blk),
        out_shape=jax.ShapeDtypeStruct((N_CHIP, R, C), other.dtype),
        compiler_params=_cparams(("arbitrary", "arbitrary")),
    )(c, *[p.reshape(N_CHIP, 2, p.shape[1], C) for p in pieces], other)


def _grad_cross(part):
    _, R, C = part.shape

    def body(p_ref, out_ref, send_sems, recv_sems):
        for phase in _cross_phases(p_ref, out_ref, send_sems, recv_sems):
            phase()

    return pl.pallas_call(
        body, name="grad_cross", out_shape=jax.ShapeDtypeStruct((3, R, C), part.dtype),
        in_specs=[pl.BlockSpec(memory_space=pl.ANY)], out_specs=pl.BlockSpec(memory_space=pl.ANY),
        scratch_shapes=CROSS_SEMS,
    )(part)


def _grad_sum(part, recv, my_chip, tr, tag):
    _, R, C = part.shape

    def body(chip_ref, p_ref, r_ref, o_ref):
        acc = p_ref[0].astype(f32)
        for j in range(3):
            acc = acc + r_ref[j].astype(f32)
        o_ref[...] = acc

    return pl.pallas_call(
        body, name=f"grad_sum_{tag}",
        grid_spec=pltpu.PrefetchScalarGridSpec(
            num_scalar_prefetch=1, grid=(R // tr,),
            in_specs=[pl.BlockSpec((1, tr, C), lambda i, chip_ref: (chip_ref[0], i, 0)),
                      pl.BlockSpec((3, tr, C), lambda i, chip_ref: (0, i, 0))],
            out_specs=pl.BlockSpec((tr, C), lambda i, chip_ref: (i, 0))),
        out_shape=jax.ShapeDtypeStruct((R, C), f32),
        compiler_params=_cparams(("arbitrary",)),
    )(my_chip, part, recv)


def _small_all_reduce(small):
    R, C = small.shape

    def body(x_ref, o_ref, buf, send_sems, recv_sems):
        _, me = _peer(0)
        buf[me] = x_ref[...]
        sends = []
        for k in range(1, N_DEV):
            dev, _ = _peer(k)
            cp = pltpu.make_async_remote_copy(src_ref=x_ref, dst_ref=buf.at[me], send_sem=send_sems.at[k - 1],
                                              recv_sem=recv_sems.at[k - 1], device_id=dev, device_id_type=MESH)
            cp.start()
            sends.append(cp)
        for k in range(1, N_DEV):
            dev, idx = _peer(k)
            pltpu.make_async_remote_copy(src_ref=x_ref, dst_ref=buf.at[idx], send_sem=send_sems.at[k - 1],
                                         recv_sem=recv_sems.at[k - 1], device_id=dev, device_id_type=MESH).wait_recv()
        for cp in sends:
            cp.wait_send()
        acc = buf[0]
        for i in range(1, N_DEV):
            acc = acc + buf[i]
        o_ref[...] = acc

    return pl.pallas_call(
        body, name="small_all_reduce", out_shape=jax.ShapeDtypeStruct((R, C), f32),
        in_specs=[pl.BlockSpec(memory_space=pltpu.VMEM)], out_specs=pl.BlockSpec(memory_space=pltpu.VMEM),
        scratch_shapes=[pltpu.VMEM((N_DEV, R, C), f32), pltpu.SemaphoreType.DMA((N_DEV - 1,)),
                        pltpu.SemaphoreType.DMA((N_DEV - 1,))],
    )(small)


_LORA = (("rwkv_w2", True), ("rwkv_a2", True), ("rwkv_g2", True))
_GROUPS_FIRST = ((("ffn1_w_in", True),),)
_GROUPS_MID = ((("ffn1_w_out", False),), _LORA)
_GROUPS_IN = ((("w_in", True),),)
_GROUPS_LATE = ((("w_proj_rwkv", False),), (("w_proj_attn", True),), (("w_out", False),),
                (("ffn2_w_in", True),), (("ffn2_w_out", False),))
_GRADS_MID = ((("w_in", True),), _LORA)
_GRADS_LAST = ((("ffn1_w_in", True),), (("ffn1_w_out", False),))
_BIG = tuple(item for group in _GROUPS_FIRST + _GROUPS_MID + _GROUPS_IN + _GROUPS_LATE for item in group)
_SMALL = ("ffn1_norm", "mix_norm", "b_gate", "rwkv_mu", "rwkv_w0", "rwkv_a0", "rwkv_k_k", "rwkv_k_a", "rwkv_r_k",
          "rwkv_ln_w", "rwkv_ln_b", "attn_q_norm", "attn_k_norm", "ffn2_norm")


def _pack_layout(like, groups):
    items, spans, off = {}, [], 0
    for group in groups:
        start = off
        for name, _ in group:
            shp = like[name].shape
            n = shp[0] * shp[1] // D
            items[name] = (off, n)
            off += n
        off = -(-off // PACK_BLOCK) * PACK_BLOCK
        spans.append((start, off - start))
    return items, spans, off


def _pack_big(shards, groups):
    items, _, rows = _pack_layout(shards, groups)
    parts, at = [], 0
    for group in groups:
        for name, tr in group:
            off, n = items[name]
            t = shards[name]
            if off > at:
                parts.append(jnp.zeros((off - at, D), t.dtype))
            parts.append((t.T if tr else t).reshape(n, D))
            at = off + n
    if rows > at:
        parts.append(jnp.zeros((rows - at, D), parts[0].dtype))
    return jnp.concatenate(parts, axis=0)


def _unpack_big(pack, like, groups):
    items, _, _ = _pack_layout(like, groups)
    out = {}
    for group in groups:
        for name, tr in group:
            off, n = items[name]
            shp = like[name].shape
            t = pack[off:off + n]
            out[name] = t.reshape(shp[1], shp[0]).T if tr else t.reshape(shp)
    return out


def _unpack_gathered(gathered, like, groups):
    items, _, _ = _pack_layout(like, groups)
    full = {}
    for group in groups:
        for name, tr in group:
            shp = like[name].shape
            off, rows = items[name]
            r_loc, c_loc = (shp[1], shp[0]) if tr else shp
            full[name] = gathered[:, off:off + rows].reshape(N_DEV * r_loc, c_loc)
    return full


def _grad_pieces(g_full, like, groups):
    items, spans, _ = _pack_layout(like, groups)
    pieces = []
    for group, (_, rows_pad) in zip(groups, spans):
        parts = [g_full[n].astype(GRAD_WIRE).reshape(N_DEV, items[n][1], D) for n, _ in group]
        piece = parts[0] if len(parts) == 1 else jnp.concatenate(parts, axis=1)
        if rows_pad > piece.shape[1]:
            piece = jnp.pad(piece, ((0, 0), (0, rows_pad - piece.shape[1]), (0, 0)))
        pieces.append(piece)
    return pieces


def _small_rows(name, t):
    flat = t.reshape(-1)
    pad = (-flat.shape[0]) % D
    return jnp.pad(flat, (0, pad)).reshape(-1, D)


def _pack_small(vals):
    parts = [_small_rows(n, vals[n]) for n in _SMALL]
    used = sum(p.shape[0] for p in parts)
    parts.append(jnp.zeros((SMALL_ROWS - used, D), f32))
    return jnp.concatenate(parts, axis=0)


def _unpack_small(pack, like):
    out, off = {}, 0
    for n in _SMALL:
        size = like[n].size
        rows = -(-size // D)
        out[n] = pack[off:off + rows].reshape(-1)[:size].reshape(like[n].shape)
        off += rows
    return out


def _build_W_mid(full):
    dt = full["rwkv_w2"].dtype
    z64, z96 = jnp.zeros((64, D), dt), jnp.zeros((96, D), dt)
    return {
        "f1_o": full["ffn1_w_out"],
        "w2p": jnp.concatenate([full["rwkv_w2"].T, z64], axis=0),
        "a2p": jnp.concatenate([full["rwkv_a2"].T, z64], axis=0),
        "g2p": jnp.concatenate([full["rwkv_g2"].T, z96], axis=0),
    }


def _build_W_in(full):
    inT = full["w_in"]
    z64, z96 = jnp.zeros((64, D), inT.dtype), jnp.zeros((96, D), inT.dtype)
    return {
        "in_rkvT": inT[:3 * D],
        "in_loraT": jnp.concatenate([inT[3072:3136], z64, inT[3136:3200], z64, inT[3200:3360], z96], axis=0),
        "in_qkvT": inT[3360:3360 + 3 * ATTN_W], "in_gateT": inT[3360 + 3 * ATTN_W:],
    }


def _build_W_late(full):
    return {"pr": full["w_proj_rwkv"], "paT": full["w_proj_attn"], "out": full["w_out"],
            "f2_iT": full["ffn2_w_in"], "f2_o": full["ffn2_w_out"]}


def _build_W_first(full):
    return {"f1_iT": full["ffn1_w_in"]}


def _build_W(full):
    return {**_build_W_first(full), **_build_W_mid(full), **_build_W_in(full), **_build_W_late(full)}


_G_NAMES = {"ffn1_w_in": "f1_iT", "ffn1_w_out": "f1_o", "w_in": "inT", "rwkv_w2": "w2T", "rwkv_a2": "a2T",
            "rwkv_g2": "g2T", "w_proj_rwkv": "pr", "w_proj_attn": "paT", "w_out": "out", "ffn2_w_in": "f2_iT",
            "ffn2_w_out": "f2_o"}


def _named_grads(G, groups):
    return {n: G[_G_NAMES[n]] for group in groups for n, _ in group}


def _reduce_start(G, like, groups, my_c, tag):
    pieces = _grad_pieces(_named_grads(G, groups), like, groups)
    return _pair_add(pieces, _grad_pair(pieces, tag), my_c, tag)


def _build_P(Wl):
    mu = Wl["rwkv_mu"]
    z64f, z96f = jnp.zeros((1, 64), f32), jnp.zeros((1, 96), f32)
    return {
        "ffn1_norm": Wl["ffn1_norm"][None], "mix_norm": Wl["mix_norm"][None], "ffn2_norm": Wl["ffn2_norm"][None],
        "b_gate": Wl["b_gate"][None], "mu_rkv": mu[None, :3 * D],
        "mu_lora": jnp.concatenate([mu[None, 3072:3136], z64f, mu[None, 3136:3200], z64f, mu[None, 3200:3360], z96f], axis=1),
        "w0": Wl["rwkv_w0"][None], "a0": Wl["rwkv_a0"][None], "k_k": Wl["rwkv_k_k"][None], "k_a": Wl["rwkv_k_a"][None],
        "r_k": Wl["rwkv_r_k"].reshape(1, D), "ln_w": Wl["rwkv_ln_w"][None], "ln_b": Wl["rwkv_ln_b"][None],
        "q_norm": Wl["attn_q_norm"][None], "k_norm": Wl["attn_k_norm"][None],
    }


def kernel(x, ffn1_norm, ffn1_w_in, ffn1_w_out, mix_norm, w_in, b_gate, rwkv_mu, rwkv_w0, rwkv_w2, rwkv_a0, rwkv_a2, rwkv_g2, rwkv_k_k, rwkv_k_a, rwkv_r_k, rwkv_ln_w, rwkv_ln_b, attn_q_norm, attn_k_norm, w_proj_rwkv, w_proj_attn, w_out, ffn2_norm, ffn2_w_in, ffn2_w_out, loss_target, m_ffn1_norm, m_ffn1_w_in, m_ffn1_w_out, m_mix_norm, m_w_in, m_b_gate, m_rwkv_mu, m_rwkv_w0, m_rwkv_w2, m_rwkv_a0, m_rwkv_a2, m_rwkv_g2, m_rwkv_k_k, m_rwkv_k_a, m_rwkv_r_k, m_rwkv_ln_w, m_rwkv_ln_b, m_attn_q_norm, m_attn_k_norm, m_w_proj_rwkv, m_w_proj_attn, m_w_out, m_ffn2_norm, m_ffn2_w_in, m_ffn2_w_out, v_ffn1_norm, v_ffn1_w_in, v_ffn1_w_out, v_mix_norm, v_w_in, v_b_gate, v_rwkv_mu, v_rwkv_w0, v_rwkv_w2, v_rwkv_a0, v_rwkv_a2, v_rwkv_g2, v_rwkv_k_k, v_rwkv_k_a, v_rwkv_r_k, v_rwkv_ln_w, v_rwkv_ln_b, v_attn_q_norm, v_attn_k_norm, v_w_proj_rwkv, v_w_proj_attn, v_w_out, v_ffn2_norm, v_ffn2_w_in, v_ffn2_w_out):
    names = ("ffn1_norm", "ffn1_w_in", "ffn1_w_out", "mix_norm", "w_in", "b_gate", "rwkv_mu", "rwkv_w0", "rwkv_w2",
             "rwkv_a0", "rwkv_a2", "rwkv_g2", "rwkv_k_k", "rwkv_k_a", "rwkv_r_k", "rwkv_ln_w", "rwkv_ln_b",
             "attn_q_norm", "attn_k_norm", "w_proj_rwkv", "w_proj_attn", "w_out", "ffn2_norm", "ffn2_w_in", "ffn2_w_out")
    w_all = (ffn1_norm, ffn1_w_in, ffn1_w_out, mix_norm, w_in, b_gate, rwkv_mu, rwkv_w0, rwkv_w2, rwkv_a0, rwkv_a2,
             rwkv_g2, rwkv_k_k, rwkv_k_a, rwkv_r_k, rwkv_ln_w, rwkv_ln_b, attn_q_norm, attn_k_norm, w_proj_rwkv,
             w_proj_attn, w_out, ffn2_norm, ffn2_w_in, ffn2_w_out)
    m_all = (m_ffn1_norm, m_ffn1_w_in, m_ffn1_w_out, m_mix_norm, m_w_in, m_b_gate, m_rwkv_mu, m_rwkv_w0, m_rwkv_w2,
             m_rwkv_a0, m_rwkv_a2, m_rwkv_g2, m_rwkv_k_k, m_rwkv_k_a, m_rwkv_r_k, m_rwkv_ln_w, m_rwkv_ln_b,
             m_attn_q_norm, m_attn_k_norm, m_w_proj_rwkv, m_w_proj_attn, m_w_out, m_ffn2_norm, m_ffn2_w_in, m_ffn2_w_out)
    v_all = (v_ffn1_norm, v_ffn1_w_in, v_ffn1_w_out, v_mix_norm, v_w_in, v_b_gate, v_rwkv_mu, v_rwkv_w0, v_rwkv_w2,
             v_rwkv_a0, v_rwkv_a2, v_rwkv_g2, v_rwkv_k_k, v_rwkv_k_a, v_rwkv_r_k, v_rwkv_ln_w, v_rwkv_ln_b,
             v_attn_q_norm, v_attn_k_norm, v_w_proj_rwkv, v_w_proj_attn, v_w_out, v_ffn2_norm, v_ffn2_w_in, v_ffn2_w_out)
    Wl = {n: t[0] for n, t in zip(names, w_all)}
    Ml = {n: t[0] for n, t in zip(names, m_all)}
    Vl = {n: t[0] for n, t in zip(names, v_all)}
    big = [n for n, _ in _BIG]

    my_c = lax.axis_index("c").astype(jnp.int32).reshape(1)
    my_chip = (2 * lax.axis_index("x") + lax.axis_index("y")).astype(jnp.int32).reshape(1)

    def pack(groups):
        return _pack_big(Wl, groups).astype(bf16)

    gathered = _all_gather(pack(_GROUPS_FIRST))
    W, P = _build_W_first(_unpack_gathered(gathered, Wl, _GROUPS_FIRST)), _build_P(Wl)
    hooks = {"gather_mid": (pack(_GROUPS_MID), lambda g: _build_W_mid(_unpack_gathered(g, Wl, _GROUPS_MID))),
             "gather_in": (pack(_GROUPS_IN), lambda g: _build_W_in(_unpack_gathered(g, Wl, _GROUPS_IN))),
             "gather_late": (pack(_GROUPS_LATE), lambda g: _build_W_late(_unpack_gathered(g, Wl, _GROUPS_LATE))),
             "reduce_mid": lambda G: _reduce_start(G, Wl, _GRADS_MID, my_c, "mid"),
             "reduce_late": lambda G: _grad_pieces(_named_grads(G, _GROUPS_LATE), Wl, _GROUPS_LATE)}

    loss_local, dx0, G, Gs = _local_step(x[0], loss_target[0], W, P, hooks)

    part_last = _reduce_start(G, Wl, _GRADS_LAST, my_c, "last")
    g_big = _unpack_big(_grad_sum(part_last, _grad_cross(part_last), my_chip, 128, "last"), Wl, _GRADS_LAST)
    g_big.update(_unpack_big(_grad_sum(*G["mid"], my_chip, 128, "mid"), Wl, _GRADS_MID))
    me = (4 * lax.axis_index("x") + 2 * lax.axis_index("y") + lax.axis_index("c")).astype(jnp.int32).reshape(1)
    g_big.update(_unpack_big(_sum_direct(*G["late"], me, "late"), Wl, _GROUPS_LATE))

    mu_g = Gs["mu_rkv"], Gs["mu_lora"]
    o1, o2 = LORA_PAD[0], LORA_PAD[0] + LORA_PAD[1]
    g_small_local = {
        "ffn1_norm": Gs["ffn1_norm"], "mix_norm": Gs["mix_norm"], "b_gate": Gs["b_gate"],
        "rwkv_mu": jnp.concatenate([mu_g[0], mu_g[1][:, :64], mu_g[1][:, o1:o1 + 64], mu_g[1][:, o2:o2 + 160]], axis=1),
        "rwkv_w0": Gs["w0"], "rwkv_a0": Gs["a0"], "rwkv_k_k": Gs["k_k"], "rwkv_k_a": Gs["k_a"], "rwkv_r_k": Gs["r_k"],
        "rwkv_ln_w": Gs["ln_w"], "rwkv_ln_b": Gs["ln_b"], "attn_q_norm": Gs["q_norm"], "attn_k_norm": Gs["k_norm"],
        "ffn2_norm": Gs["ffn2_norm"]}
    gs_pack = _small_all_reduce(_pack_small(g_small_local))

    out_g, out_d, out_m, out_v = dict(g_big), {}, {}, {}
    for n in big:
        cols = Wl[n].shape[1]
        out_d[n], out_m[n], out_v[n] = _rowwise(f"adamw_{n}", _f_adamw, [Wl[n], g_big[n], Ml[n], Vl[n]], [],
                                                 [(cols, f32)] * 3)
    ds_pack, ms_pack, vs_pack = _rowwise(
        "adamw_small", _f_adamw, [_pack_small(Wl), gs_pack, _pack_small(Ml), _pack_small(Vl)], [], [(D, f32)] * 3)
    for out, pack in ((out_g, gs_pack), (out_d, ds_pack), (out_m, ms_pack), (out_v, vs_pack)):
        out.update(_unpack_small(pack, Wl))

    loss = lax.psum(loss_local, ("x", "y", "c"))
    return (loss, dx0[None], *[out_g[n][None] for n in names], *[out_d[n][None] for n in names],
            *[out_m[n][None] for n in names], *[out_v[n][None] for n in names])
```

```python
import functools

import jax
import jax.numpy as jnp
from jax import lax
from jax.experimental import pallas as pl
from jax.experimental.pallas import tpu as pltpu

f32 = jnp.float32
bf16 = jnp.bfloat16
MESH = pl.DeviceIdType.MESH

N_DEV = 8
D = 1024
D_FF = 2816
HEAD = 64
RW_HEADS = 16
ATTN_PAIRS = ((128, 1), (512, 4), (2048, 16))
ATTN_BLK = 128
HEADS_PER_GROUP = 4
ATTN_W = 768
LORA_PAD = (128, 128, 256)
LORA_W = (64, 64, 160)
GN_EPS = 64e-5
RMS_EPS = 1e-6
NEG_INF = -1e30
WKV_T = 64
WKV_SUB = 2
GRAD_WIRE = bf16
PACK_BLOCK = 128
SMALL_ROWS = 24
VMEM_LIMIT = 56 * 1024 * 1024

ADAM_LR, ADAM_B1, ADAM_B2, ADAM_EPS, ADAM_WD, ADAM_STEP = 0.001, 0.9, 0.999, 1e-08, 0.01, 10


def _cparams(sem):
    return pltpu.CompilerParams(dimension_semantics=sem, vmem_limit_bytes=VMEM_LIMIT)


def _pick(n, cands):
    for c in cands:
        if n % c == 0:
            return c
    return n


HALO = 8


def _rowwise(name, fn, rows, params, outs, accs=(), tm=256, halos=(), carries=(), reverse=False):
    S = rows[0].shape[0]
    tm = min(tm, S)
    while S % tm:
        tm -= 8
    nb = S // tm
    n_in = len(rows) + len(params) + len(halos)
    n_out = len(outs)
    n_acc = len(accs)
    n_car = len(carries)

    def blk_of(i):
        return nb - 1 - i if reverse else i

    def body(*refs):
        step = pl.program_id(0)
        carry_refs = refs[n_in + n_out + n_acc:]
        if n_car:
            @pl.when(step == 0)
            def _():
                for c_ref in carry_refs:
                    c_ref[...] = jnp.zeros(c_ref.shape, f32)
        args = [r[...] for r in refs[:n_in]] + [c[...] for c in carry_refs]
        res = fn(*args, blk=blk_of(step)) if (halos or carries) else fn(*args)
        if not isinstance(res, (tuple, list)):
            res = (res,)
        out_refs = refs[n_in:n_in + n_out + n_acc]
        for j in range(n_out):
            out_refs[j][...] = res[j].astype(out_refs[j].dtype)
        if n_acc:
            @pl.when(step == 0)
            def _():
                for j in range(n_acc):
                    out_refs[n_out + j][...] = jnp.zeros(out_refs[n_out + j].shape, f32)
            for j in range(n_acc):
                out_refs[n_out + j][...] += res[n_out + j]
        for j in range(n_car):
            carry_refs[j][...] = res[n_out + n_acc + j]

    in_specs = [pl.BlockSpec((tm, a.shape[1]), lambda i: (blk_of(i), 0)) for a in rows]
    in_specs += [pl.BlockSpec(p.shape, lambda i, nd=p.ndim: (0,) * nd) for p in params]
    in_specs += [pl.BlockSpec((HALO, rows[h].shape[1]), lambda i: (jnp.maximum(blk_of(i) * (tm // HALO) - 1, 0), 0))
                 for h in halos]
    out_specs = [pl.BlockSpec((tm, w), lambda i: (blk_of(i), 0)) for w, _ in outs]
    out_specs += [pl.BlockSpec(s, lambda i: (0, 0)) for s in accs]
    out_shape = [jax.ShapeDtypeStruct((S, w), dt) for w, dt in outs]
    out_shape += [jax.ShapeDtypeStruct(s, f32) for s in accs]
    res = pl.pallas_call(
        body, name=name, grid=(nb,), in_specs=in_specs, out_specs=out_specs, out_shape=out_shape,
        scratch_shapes=[pltpu.VMEM(s, f32) for s in carries],
        compiler_params=_cparams(("arbitrary",)),
    )(*rows, *params, *[rows[h] for h in halos])
    return res


MM_VMEM_BUDGET = 40 * 1024 * 1024
MM_STEP_US = 0.35
MM_FLOPS_PER_US = 9.0e8
MM_HBM_BYTES_PER_US = 3.0e6


def _tile_options(n, cap):
    opts = [d for d in range(128, min(n, cap) + 1, 128) if n % d == 0]
    return opts or [n]


def _mm_tiles(M, N, K, sa, sb, so, whole_rows=False):
    best, best_cost = None, None
    for tm in _tile_options(M, 512 if whole_rows else 2048):
        for tn in ([N] if whole_rows else _tile_options(N, 2048)):
            for tk in _tile_options(K, 4096):
                vmem = 2 * (tm * tk * sa + tk * tn * sb) + 2 * tm * tn * so + (tm * tn * 4 if tk < K else 0)
                if vmem > MM_VMEM_BUDGET:
                    continue
                steps = (M // tm) * (N // tn) * (K // tk)
                traffic = M * K * sa * (N // tn) + K * N * sb * (M // tm) + M * N * so
                cost = (max(2.0 * M * N * K / MM_FLOPS_PER_US, traffic / MM_HBM_BYTES_PER_US) + steps * MM_STEP_US
                        + (tm * tk * sa + tk * tn * sb) / MM_HBM_BYTES_PER_US)
                if best_cost is None or cost < best_cost:
                    best, best_cost = (tm, tn, tk), cost
    return best


def _mm(name, a, b, mode, out_dtype=f32, scale=None, gather=None, cross=None, epilogue=None):
    halves = a.ndim == 3
    sizes = (jnp.dtype(a.dtype).itemsize, jnp.dtype(b.dtype).itemsize, jnp.dtype(out_dtype).itemsize)
    whole = epilogue is not None
    if mode == "nn":
        (M, K), N = (a.shape[1], 2 * a.shape[2]) if halves else a.shape, b.shape[1]
        tm, tn, tk = _mm_tiles(M, N, K // 2 if halves else K, *sizes, whole_rows=whole)
    elif mode == "nt":
        (M, K), N = a.shape, b.shape[0]
        tm, tn, tk = _mm_tiles(M, N, K, *sizes, whole_rows=whole)
    else:
        (K, M), N = (a.shape[1], 2 * a.shape[2]) if halves else a.shape, b.shape[1]
        tm, tn, tk = _mm_tiles(M // 2 if halves else M, N, K, *sizes, whole_rows=whole)
    nk = K // tk
    if mode == "nn":
        per = K // 2 // tk
        a_spec = (pl.BlockSpec((None, tm, tk), lambda i, j, k: (k // per, i, k % per)) if halves
                  else pl.BlockSpec((tm, tk), lambda i, j, k: (i, k)))
        b_spec = pl.BlockSpec((tk, tn), lambda i, j, k: (k, j))
        dims = (((1,), (0,)), ((), ()))
    elif mode == "nt":
        a_spec = pl.BlockSpec((tm, tk), lambda i, j, k: (i, k))
        b_spec = pl.BlockSpec((tn, tk), lambda i, j, k: (j, k))
        dims = (((1,), (1,)), ((), ()))
    else:
        per = M // 2 // tm
        a_spec = (pl.BlockSpec((None, tk, tm), lambda i, j, k: (i // per, k, i % per)) if halves
                  else pl.BlockSpec((tk, tm), lambda i, j, k: (k, i)))
        b_spec = pl.BlockSpec((tk, tn), lambda i, j, k: (k, j))
        dims = (((0,), (0,)), ((), ()))

    hosted = gather if gather is not None else cross
    grid = (M // tm, N // tn, nk)
    steps = grid[0] * grid[1] * grid[2]
    ep_rows, ep_params, ep_outs, ep_accs = ([], [], [], []) if epilogue is None else epilogue[1:]
    n_ep_in, n_ep_out = len(ep_rows) + len(ep_params), len(ep_outs) + len(ep_accs)
    assert epilogue is None or tn == N

    def body(a_ref, b_ref, *rest):
        rest = list(rest)
        src_ref = rest.pop(0) if hosted is not None else None
        ep_in, rest = rest[:n_ep_in], rest[n_ep_in:]
        if epilogue is None:
            o_ref = rest.pop(0)
        else:
            out_refs, rest = rest[:n_ep_out], rest[n_ep_out:]
        dst_ref = rest.pop(0) if hosted is not None else None
        scratch = rest
        step = (pl.program_id(0) * grid[1] + pl.program_id(1)) * grid[2] + pl.program_id(2)
        if hosted is not None:
            n_sem = len(GATHER_SEMS if gather is not None else CROSS_SEMS)
            sems, scratch = scratch[len(scratch) - n_sem:], scratch[:len(scratch) - n_sem]
            if gather is not None:
                start, forward, done = _gather_phases(src_ref, dst_ref, *sems)
                pl.when(step == steps // 2)(forward)
            else:
                start, done = _cross_phases(src_ref, dst_ref, *sems)
            pl.when(step == 0)(start)
        part = lax.dot_general(a_ref[...].astype(bf16), b_ref[...].astype(bf16), dims,
                               preferred_element_type=f32)

        def finish(acc):
            if epilogue is None:
                o_ref[...] = (acc if scale is None else acc * scale).astype(o_ref.dtype)
                return
            res = epilogue[0](acc, *[r[...] for r in ep_in])
            for j in range(len(ep_outs)):
                out_refs[j][...] = res[j].astype(out_refs[j].dtype)
            for j in range(len(ep_accs)):
                acc_out = out_refs[len(ep_outs) + j]

                @pl.when(step == nk - 1)
                def _(acc_out=acc_out):
                    acc_out[...] = jnp.zeros(acc_out.shape, f32)
                acc_out[...] += res[len(ep_outs) + j]

        if nk == 1:
            finish(part)
        else:
            acc_ref = scratch[0]
            k = pl.program_id(2)

            @pl.when(k == 0)
            def _():
                acc_ref[...] = part

            @pl.when(k > 0)
            def _():
                acc_ref[...] += part

            @pl.when(k == nk - 1)
            def _():
                finish(acc_ref[...])
        if hosted is not None:
            pl.when(step == steps - 1)(done)

    hbm = pl.BlockSpec(memory_space=pl.ANY)
    in_specs = [a_spec, b_spec] + [hbm] * (hosted is not None)
    in_specs += [pl.BlockSpec((tm, r.shape[1]), lambda i, j, k: (i, 0)) for r in ep_rows]
    in_specs += [pl.BlockSpec(p.shape, lambda i, j, k: (0, 0)) for p in ep_params]
    if epilogue is None:
        out_specs = [pl.BlockSpec((tm, tn), lambda i, j, k: (i, j))]
        out_shape = [jax.ShapeDtypeStruct((M, N), out_dtype)]
    else:
        out_specs = [pl.BlockSpec((tm, w), lambda i, j, k: (i, 0)) for w, _ in ep_outs]
        out_specs += [pl.BlockSpec(s, lambda i, j, k: (0, 0)) for s in ep_accs]
        out_shape = [jax.ShapeDtypeStruct((M, w), dt) for w, dt in ep_outs]
        out_shape += [jax.ShapeDtypeStruct(s, f32) for s in ep_accs]
    scratch_shapes = [] if nk == 1 else [pltpu.VMEM((tm, tn), f32)]
    if gather is not None:
        out_specs.append(hbm)
        out_shape.append(jax.ShapeDtypeStruct((N_DEV,) + gather.shape, gather.dtype))
        scratch_shapes = scratch_shapes + GATHER_SEMS
    elif cross is not None:
        out_specs.append(hbm)
        out_shape.append(jax.ShapeDtypeStruct((3,) + cross.shape[1:], cross.dtype))
        scratch_shapes = scratch_shapes + CROSS_SEMS
    sequential = hosted is not None or ep_accs
    res = pl.pallas_call(
        body, name=name, grid=grid, in_specs=in_specs,
        out_specs=out_specs, out_shape=out_shape, scratch_shapes=scratch_shapes,
        compiler_params=_cparams(("arbitrary",) * 3 if sequential else ("parallel", "parallel", "arbitrary")),
    )(a, b, *([hosted] if hosted is not None else []), *ep_rows, *ep_params)
    return res[0] if (hosted is None and epilogue is None) else res


FFN_TM, FFN_TN = 512, 1408


def _ffn_up(name, n, WiT, gather=None):
    S = n.shape[0]
    grid = (S // FFN_TM, D_FF // FFN_TN)
    steps = grid[0] * grid[1]

    def body(n_ref, wg_ref, wu_ref, *rest):
        if gather is None:
            g_ref, u_ref, act_ref = rest
        else:
            src_ref, g_ref, u_ref, act_ref, dst_ref, *sems = rest
            step = pl.program_id(0) * grid[1] + pl.program_id(1)
            start, forward, done = _gather_phases(src_ref, dst_ref, *sems)
            pl.when(step == 0)(start)
            pl.when(step == steps // 2)(forward)
        x = n_ref[...]
        gate = lax.dot_general(x, wg_ref[...], _NT, preferred_element_type=f32)
        up = lax.dot_general(x, wu_ref[...], _NT, preferred_element_type=f32)
        g_ref[...] = gate
        u_ref[...] = up
        act_ref[...] = (gate * jax.nn.sigmoid(gate) * up).astype(act_ref.dtype)
        if gather is not None:
            pl.when(step == steps - 1)(done)

    hbm = pl.BlockSpec(memory_space=pl.ANY)
    tile = pl.BlockSpec((FFN_TM, FFN_TN), lambda i, j: (i, j))
    in_specs = [pl.BlockSpec((FFN_TM, D), lambda i, j: (i, 0)), pl.BlockSpec((FFN_TN, D), lambda i, j: (j, 0)),
                pl.BlockSpec((FFN_TN, D), lambda i, j: (j + D_FF // FFN_TN, 0))]
    out_specs = [tile, tile, tile]
    out_shape = [jax.ShapeDtypeStruct((S, D_FF), f32), jax.ShapeDtypeStruct((S, D_FF), f32),
                 jax.ShapeDtypeStruct((S, D_FF), bf16)]
    if gather is not None:
        in_specs.append(hbm)
        out_specs.append(hbm)
        out_shape.append(jax.ShapeDtypeStruct((N_DEV,) + gather.shape, gather.dtype))
    return pl.pallas_call(
        body, name=name, grid=grid, in_specs=in_specs, out_specs=out_specs, out_shape=out_shape,
        scratch_shapes=GATHER_SEMS if gather is not None else [],
        compiler_params=_cparams(("arbitrary", "arbitrary")),
    )(n, WiT, WiT, *([gather] if gather is not None else []))


def _ffn_dact(name, dy, Wo, gate, up):
    S = dy.shape[0]

    def body(dy_ref, wo_ref, g_ref, u_ref, d_ref):
        dact = 0.5 * lax.dot_general(dy_ref[...], wo_ref[...], _NT, preferred_element_type=f32)
        gate, up = g_ref[...], u_ref[...]
        sg = jax.nn.sigmoid(gate)
        d_ref[0] = (dact * up * (sg * (1.0 + gate * (1.0 - sg)))).astype(d_ref.dtype)
        d_ref[1] = (dact * gate * sg).astype(d_ref.dtype)

    tile = pl.BlockSpec((FFN_TM, FFN_TN), lambda i, j: (i, j))
    return pl.pallas_call(
        body, name=name, grid=(S // FFN_TM, D_FF // FFN_TN),
        in_specs=[pl.BlockSpec((FFN_TM, D), lambda i, j: (i, 0)), pl.BlockSpec((FFN_TN, D), lambda i, j: (j, 0)), tile, tile],
        out_specs=pl.BlockSpec((2, FFN_TM, FFN_TN), lambda i, j: (0, i, j)),
        out_shape=jax.ShapeDtypeStruct((2, S, D_FF), bf16),
        compiler_params=_cparams(("parallel", "parallel")),
    )(dy, Wo, gate, up)


def _sp(x):
    hi = x.astype(bf16)
    return hi, (x - hi.astype(f32)).astype(bf16)


def _cat(parts):
    return tuple(jnp.concatenate(p, axis=1) for p in zip(*parts))


def _bmm(eq, a, b):
    (ah, al), (bh, bl) = a, b
    dot = functools.partial(jnp.einsum, eq, preferred_element_type=f32)
    return dot(ah, bh) + (dot(ah, bl) + dot(al, bh))


def _tri_dot(eq, tri, x):
    h1 = x.astype(bf16)
    r1 = x - h1.astype(f32)
    h2 = r1.astype(bf16)
    h3 = (r1 - h2.astype(f32)).astype(bf16)
    dot = functools.partial(jnp.einsum, eq, preferred_element_type=f32)
    return dot(tri, h1) + (dot(tri, h2) + dot(tri, h3))


def _tri_masks(T):
    ti = lax.broadcasted_iota(jnp.int32, (T, T), 0)
    si = lax.broadcasted_iota(jnp.int32, (T, T), 1)
    return ti >= si, ti > si


def _wkv_prep(r, lw, k, kkr, a):
    H, T, _ = r.shape
    low_i, low_s = _tri_masks(T)
    nrm = jnp.sqrt(jnp.sum(kkr * kkr, axis=-1, keepdims=True))
    den = jnp.maximum(nrm, 1e-12)
    kk = kkr / den
    tri = jnp.broadcast_to(low_i.astype(bf16)[None], (H, T, T))
    cl = _tri_dot("hts,hsn->htn", tri, lw)
    c = jnp.exp(cl)
    cprev = jnp.exp(cl - lw)
    cinv = jnp.exp(-cl)
    bt, kt = _sp(kk * a * cinv), _sp(k * cinv)
    L = _cat([_sp(r * c), _sp(-kk * cprev)])
    Mb = _bmm("htn,hsn->hts", L, bt)
    Mk = _bmm("htn,hsn->hts", L, kt)
    A_rb = jnp.where(low_i[None], Mb[:, :T], 0.0)
    A_ab = jnp.where(low_s[None], Mb[:, T:], 0.0)
    Mk = jnp.concatenate([jnp.where(low_i[None], Mk[:, :T], 0.0), jnp.where(low_s[None], Mk[:, T:], 0.0)], axis=1)
    return dict(kk=kk, den=den, nrm=nrm, c=c, cprev=cprev, cinv=cinv, L=L, kt=kt, bt=bt,
                A_ab=A_ab, A_rb=A_rb, Mk=Mk, cT=c[:, T - 1:T, :])


def _tri_inverse(A):
    T = A.shape[-1]
    eye = (lax.broadcasted_iota(jnp.int32, (T, T), 0) == lax.broadcasted_iota(jnp.int32, (T, T), 1)).astype(f32)
    inv = eye[None] + A
    X = A
    n = 1
    while 2 * n < T:
        Xs = _sp(X)
        X = _bmm("hts,hsu->htu", Xs, Xs)
        inv = inv + _bmm("hts,hsu->htu", _sp(inv), _sp(X))
        n *= 2
    return inv


def _wkv_chunk_fwd(S0, r, lw, k, v, kkr, a):
    T = r.shape[1]
    q = _wkv_prep(r, lw, k, kkr, a)
    inv = _tri_inverse(q["A_ab"])
    vs = _sp(v)
    P = _bmm("htk,hvk->htv", q["L"], _sp(S0)) + _bmm("hts,hsv->htv", _sp(q["Mk"]), vs)
    U = _bmm("hts,hsv->htv", _sp(inv), _sp(P[:, T:]))
    Us = _sp(U)
    Y = P[:, :T] + _bmm("hts,hsv->htv", _sp(q["A_rb"]), Us)
    S1 = (S0 + _bmm("htv,htk->hvk", _cat([Us, vs]), _cat([q["bt"], q["kt"]]))) * q["cT"]
    return Y, U, inv, S1


def _wkv_chunk_bwd(S0, Hin, Q, r, lw, k, v, kkr, a, U, inv, dY):
    H, T, _ = r.shape
    low_i, low_s = _tri_masks(T)
    q = _wkv_prep(r, lw, k, kkr, a)
    L, kt, bt = q["L"], q["kt"], q["bt"]
    R = _cat([bt, kt])
    Hh = Hin * q["cT"]
    Hs, S0s, dYs, vs, Us = _sp(Hh), _sp(S0), _sp(dY), _sp(v), _sp(U)
    RH = _bmm("htk,hvk->htv", R, Hs)
    Z = _bmm("hst,hsv->htv", _sp(inv), _sp(RH[:, :T] + _bmm("hst,hsv->htv", _sp(q["A_rb"]), dYs)))
    DZ = _cat([dYs, _sp(Z)])
    both = jnp.concatenate([jnp.broadcast_to(low_i[None], (1, T, T)), jnp.broadcast_to(low_s[None], (1, T, T))], axis=1)
    NU = _sp(jnp.where(both, _bmm("htv,hsv->hts", DZ, Us), 0.0))
    NV = _sp(jnp.where(both, _bmm("htv,hsv->hts", DZ, vs), 0.0))
    ra = _bmm("htv,hvk->htk", DZ, S0s) + _bmm("hts,hsk->htk", NU, bt) + _bmm("hts,hsk->htk", NV, kt)
    dr = ra[:, :T] * q["c"]
    da = ra[:, T:] * q["cprev"]
    dv = RH[:, T:] + _bmm("hst,hsv->htv", _sp(q["Mk"]), DZ)
    VH = _bmm("htv,hvk->htk", _cat([vs, Us]), Hs)
    dk = (VH[:, :T] + _bmm("hst,hsk->htk", NV, L)) * q["cinv"]
    db = (VH[:, T:] + _bmm("hst,hsk->htk", NU, L)) * q["cinv"]
    H0 = Hh + _bmm("htv,htk->hvk", DZ, L)
    kk = q["kk"]
    e = r * dr - kk * a * db - k * dk
    f = -kk * da
    tri_i = jnp.broadcast_to(low_i.astype(bf16)[None], (H, T, T))
    tri_s = jnp.broadcast_to(low_s.astype(bf16)[None], (H, T, T))
    dlw = _tri_dot("hst,hsn->htn", tri_i, e) + _tri_dot("hst,hsn->htn", tri_s, f) + Q
    Qn = Q + jnp.sum(e + f, axis=1, keepdims=True)
    dkk = db * a - da
    dasig = db * kk
    proj = jnp.sum(dkk * kk, axis=-1, keepdims=True)
    dkkr = jnp.where(q["nrm"] > 1e-12, dkk - kk * proj, dkk) / q["den"]
    return dr, dlw, dk, dv, dkkr, dasig, H0, Qn


def _heads(ref, rows=slice(None)):
    return jnp.stack([ref[rows, h * HEAD:(h + 1) * HEAD] for h in range(RW_HEADS)], axis=0)


def _put_heads(ref, val, rows=slice(None)):
    for h in range(RW_HEADS):
        ref[rows, h * HEAD:(h + 1) * HEAD] = val[h]


def _wkv_fwd(r, lw, k, v, kkr, a, g, r_k, ln_w, ln_b, late_pack=None):
    S = r.shape[0]
    H, N, T = RW_HEADS, HEAD, WKV_T
    TS = T * WKV_SUB
    nc = S // TS
    hosting = late_pack is not None

    def body(r_ref, lw_ref, k_ref, v_ref, kkr_ref, a_ref, g_ref, rk_ref, lnw_ref, lnb_ref, *rest):
        if hosting:
            pack_ref, y_ref, yg_ref, wkv_ref, u_ref, inv_ref, s0_ref, gathered_ref, state, *sems = rest
            start, forward, finish = _gather_phases(pack_ref, gathered_ref, *sems)
            pl.when(pl.program_id(0) == 0)(start)
            pl.when(pl.program_id(0) == nc // 2)(forward)
        else:
            y_ref, yg_ref, wkv_ref, u_ref, inv_ref, s0_ref, state = rest

        @pl.when(pl.program_id(0) == 0)
        def _():
            state[...] = jnp.zeros(state.shape, f32)

        S0 = state[...]
        for c in range(WKV_SUB):
            rows = slice(c * T, (c + 1) * T)
            s0_ref[c] = S0
            rr, kk2, vv = _heads(r_ref, rows), _heads(k_ref, rows), _heads(v_ref, rows)
            Y, U, inv, S0 = _wkv_chunk_fwd(S0, rr, _heads(lw_ref, rows), kk2, vv, _heads(kkr_ref, rows),
                                           _heads(a_ref, rows))
            wkv_ref[:, rows, :] = Y
            u_ref[:, rows, :] = U
            inv_ref[:, rows, :] = inv
            mean = jnp.mean(Y, axis=-1, keepdims=True)
            var = jnp.mean(jnp.square(Y - mean), axis=-1, keepdims=True)
            yn = (Y - mean) * lax.rsqrt(var + GN_EPS)
            bonus = jnp.sum(rr * kk2 * rk_ref[...], axis=-1, keepdims=True) * vv
            _put_heads(y_ref, yn * lnw_ref[...] + lnb_ref[...] + bonus, rows)
        state[...] = S0
        yg_ref[...] = (y_ref[...] * g_ref[...]).astype(yg_ref.dtype)
        if hosting:
            pl.when(pl.program_id(0) == nc - 1)(finish)

    tok = pl.BlockSpec((TS, H * N), lambda i: (i, 0))
    blk = pl.BlockSpec((H, TS, N), lambda i: (0, i, 0))
    par = pl.BlockSpec((H, 1, N), lambda i: (0, 0, 0))
    hbm = pl.BlockSpec(memory_space=pl.ANY)
    seq = jax.ShapeDtypeStruct((H, S, N), f32)
    out_specs = [tok, tok, blk, blk, pl.BlockSpec((H, TS, T), lambda i: (0, i, 0)),
                 pl.BlockSpec((WKV_SUB, H, N, N), lambda i: (i, 0, 0, 0))]
    out_shape = [jax.ShapeDtypeStruct((S, H * N), f32), jax.ShapeDtypeStruct((S, H * N), bf16), seq, seq,
                 jax.ShapeDtypeStruct((H, S, T), f32),
                 jax.ShapeDtypeStruct((S // T, H, N, N), f32)]
    if hosting:
        out_specs.append(hbm)
        out_shape.append(jax.ShapeDtypeStruct((N_DEV,) + late_pack.shape, late_pack.dtype))
    return pl.pallas_call(
        body, name="wkv_fwd", grid=(nc,), in_specs=[tok] * 7 + [par] * 3 + [hbm] * hosting,
        out_specs=out_specs, out_shape=out_shape,
        scratch_shapes=[pltpu.VMEM((H, N, N), f32)] + (GATHER_SEMS if hosting else []),
        compiler_params=_cparams(("arbitrary",)),
    )(r, lw, k, v, kkr, a, g, r_k, ln_w, ln_b, *([late_pack] if hosting else []))


def _wkv_bwd(dy, g, r, lw, k, v, kkr, a, wkv, U, inv, S0s, r_k, ln_w, ln_b, late_pieces=None):
    S = r.shape[0]
    H, N, T = RW_HEADS, HEAD, WKV_T
    TS = T * WKV_SUB
    nc = S // TS
    hosting = late_pieces is not None
    n_late = len(late_pieces) if hosting else 0

    def body(dy_ref, g_ref, r_ref, lw_ref, k_ref, v_ref, kkr_ref, a_ref, wkv_ref, u_ref, inv_ref, s0_ref,
             rk_ref, lnw_ref, lnb_ref, *rest):
        if hosting:
            piece_refs, rest = rest[:n_late], rest[n_late:]
            (dr_ref, dlw_ref, dk_ref, dv_ref, dkkr_ref, da_ref, drk_ref, dlnw_ref, dlnb_ref, recv_ref,
             hst, qst, *sems) = rest
            start, finish = _direct_phases(piece_refs, [p.shape[1] for p in late_pieces], recv_ref, *sems)
            pl.when(pl.program_id(0) == 0)(start)
        else:
            dr_ref, dlw_ref, dk_ref, dv_ref, dkkr_ref, da_ref, drk_ref, dlnw_ref, dlnb_ref, hst, qst = rest

        @pl.when(pl.program_id(0) == 0)
        def _():
            hst[...] = jnp.zeros(hst.shape, f32)
            qst[...] = jnp.zeros(qst.shape, f32)
            drk_ref[...] = jnp.zeros(drk_ref.shape, f32)
            dlnw_ref[...] = jnp.zeros(dlnw_ref.shape, f32)
            dlnb_ref[...] = jnp.zeros(dlnb_ref.shape, f32)

        dyg = dy_ref[...] * g_ref[...]
        rk = rk_ref[...]
        Hst, Qst = hst[...], qst[...]
        for c in reversed(range(WKV_SUB)):
            rows = slice(c * T, (c + 1) * T)
            dya = _heads(dyg, rows)
            rr, kk2, vv, Y = _heads(r_ref, rows), _heads(k_ref, rows), _heads(v_ref, rows), wkv_ref[:, rows, :]
            s = jnp.sum(rr * kk2 * rk, axis=-1, keepdims=True)
            ds = jnp.sum(dya * vv, axis=-1, keepdims=True)
            mean = jnp.mean(Y, axis=-1, keepdims=True)
            var = jnp.mean(jnp.square(Y - mean), axis=-1, keepdims=True)
            rstd = lax.rsqrt(var + GN_EPS)
            yn = (Y - mean) * rstd
            dyn = dya * lnw_ref[...]
            dY = rstd * (dyn - jnp.mean(dyn, axis=-1, keepdims=True) - yn * jnp.mean(dyn * yn, axis=-1, keepdims=True))
            drk_ref[...] += jnp.sum(ds * rr * kk2, axis=1, keepdims=True)
            dlnw_ref[...] += jnp.sum(dya * yn, axis=1, keepdims=True)
            dlnb_ref[...] += jnp.sum(dya, axis=1, keepdims=True)
            dr, dlw, dk, dv, dkkr, dasig, Hst, Qst = _wkv_chunk_bwd(
                s0_ref[c], Hst, Qst, rr, _heads(lw_ref, rows), kk2, vv, _heads(kkr_ref, rows), _heads(a_ref, rows),
                u_ref[:, rows, :], inv_ref[:, rows, :], dY)
            _put_heads(dr_ref, dr + ds * kk2 * rk, rows)
            _put_heads(dlw_ref, dlw, rows)
            _put_heads(dk_ref, dk + ds * rr * rk, rows)
            _put_heads(dv_ref, dv + dya * s, rows)
            _put_heads(dkkr_ref, dkkr, rows)
            _put_heads(da_ref, dasig, rows)
        hst[...] = Hst
        qst[...] = Qst
        if hosting:
            pl.when(pl.program_id(0) == nc - 1)(finish)

    tok = pl.BlockSpec((TS, H * N), lambda i: (nc - 1 - i, 0))
    blk = pl.BlockSpec((H, TS, N), lambda i: (0, nc - 1 - i, 0))
    par = pl.BlockSpec((H, 1, N), lambda i: (0, 0, 0))
    hbm = pl.BlockSpec(memory_space=pl.ANY)
    seq = jax.ShapeDtypeStruct((S, H * N), f32)
    pout = jax.ShapeDtypeStruct((H, 1, N), f32)
    out_specs, out_shape = [tok] * 6 + [par] * 3, [seq] * 6 + [pout] * 3
    sems = []
    if hosting:
        rows_late = sum(p.shape[1] for p in late_pieces)
        out_specs.append(hbm)
        out_shape.append(jax.ShapeDtypeStruct((N_DEV - 1, rows_late, late_pieces[0].shape[2]), late_pieces[0].dtype))
        sems = [pltpu.SemaphoreType.DMA((n_late * (N_DEV - 1),))] * 2
    return pl.pallas_call(
        body, name="wkv_bwd", grid=(nc,),
        in_specs=([tok] * 8 + [blk] * 2 + [pl.BlockSpec((H, TS, T), lambda i: (0, nc - 1 - i, 0))]
                  + [pl.BlockSpec((WKV_SUB, H, N, N), lambda i: (nc - 1 - i, 0, 0, 0))]
                  + [par] * 3 + [hbm] * n_late),
        out_specs=out_specs, out_shape=out_shape,
        scratch_shapes=[pltpu.VMEM((H, N, N), f32), pltpu.VMEM((H, 1, N), f32)] + sems,
        compiler_params=_cparams(("arbitrary",)),
    )(dy, g, r, lw, k, v, kkr, a, wkv, U, inv, S0s, r_k, ln_w, ln_b, *(late_pieces if hosting else []))


ATTN_TT = 2048


def _attn_rows(d, i, j):
    return pl.ds(ATTN_BLK * d * i + j, ATTN_BLK, stride=d) if d > 1 else pl.ds(ATTN_BLK * i, ATTN_BLK)


def _attn_take(ref, d, nsub):
    return jnp.stack([ref[_attn_rows(d, i, j), :] for i in range(nsub) for j in range(d)], axis=0)


def _attn_put(ref, val, d):
    for i in range(val.shape[0] // d):
        for j in range(d):
            ref[_attn_rows(d, i, j), :] = val[i * d + j]


def _attn_prev(cur, before, d):
    return before if cur.shape[0] == d else jnp.concatenate([before, cur[:cur.shape[0] - d]], axis=0)


def _attn_specs(gi, d, nt, reverse):
    per_tile = ATTN_TT // (ATTN_BLK * d)

    def tile(n):
        return nt - 1 - n if reverse else n

    def col(kind):
        return lambda hp, n: (tile(n), kind * (ATTN_W // 128) + 2 * gi + hp)

    def col_before(kind):
        return lambda hp, n: (jnp.maximum(tile(n) * per_tile - 1, 0), kind * (ATTN_W // 128) + 2 * gi + hp)

    cur = [pl.BlockSpec((ATTN_TT, 128), col(kind)) for kind in range(3)]
    before = [pl.BlockSpec((ATTN_BLK * d, 128), col_before(kind)) for kind in (1, 2)]
    own = pl.BlockSpec((ATTN_TT, 128), lambda hp, n: (tile(n), hp))
    return cur, before, own, tile


def _attn_norm(x, gain, scale):
    rs = lax.rsqrt(jnp.mean(x * x, axis=-1, keepdims=True) + RMS_EPS)
    return x * rs * (gain * scale), rs


def _attn_scores(qn, kn_c, kn_p, first):
    s_c = jnp.einsum("gqe,gke->gqk", qn.astype(bf16), kn_c.astype(bf16), preferred_element_type=f32)
    s_p = jnp.einsum("gqe,gke->gqk", qn.astype(bf16), kn_p.astype(bf16), preferred_element_type=f32)
    qi = lax.broadcasted_iota(jnp.int32, (1, ATTN_BLK, ATTN_BLK), 1)
    ki = lax.broadcasted_iota(jnp.int32, (1, ATTN_BLK, ATTN_BLK), 2)
    s_c = jnp.where(qi >= ki, s_c, NEG_INF)
    s_p = jnp.where(jnp.logical_and(ki >= qi, jnp.logical_not(first)), s_p, NEG_INF)
    return s_c, s_p


def _attn_fwd(pqkv, qg, kg, gi, S):
    d = ATTN_PAIRS[gi][1]
    nt = S // ATTN_TT
    nsub = ATTN_TT // (ATTN_BLK * d)
    nd = nsub * d

    def body(q_ref, k_ref, v_ref, kb_ref, vb_ref, qg_ref, kg_ref, o_ref, lse_ref):
        Q, K, V = _attn_take(q_ref, d, nsub), _attn_take(k_ref, d, nsub), _attn_take(v_ref, d, nsub)
        KB, VB = _attn_take(kb_ref, d, 1), _attn_take(vb_ref, d, 1)
        first = jnp.logical_and(lax.broadcasted_iota(jnp.int32, (nd, 1, 1), 0) < d, pl.program_id(1) == 0)
        outs, lses = [], []
        for h in range(2):
            sl = slice(h * HEAD, (h + 1) * HEAD)
            kc, vc = K[:, :, sl], V[:, :, sl]
            kp, vp = _attn_prev(kc, KB[:, :, sl], d), _attn_prev(vc, VB[:, :, sl], d)
            qn, _ = _attn_norm(Q[:, :, sl], qg_ref[...], HEAD ** -0.5)
            kn_c, _ = _attn_norm(kc, kg_ref[...], 1.0)
            kn_p, _ = _attn_norm(kp, kg_ref[...], 1.0)
            s_c, s_p = _attn_scores(qn, kn_c, kn_p, first)
            m = jnp.maximum(jnp.max(s_c, axis=-1, keepdims=True), jnp.max(s_p, axis=-1, keepdims=True))
            p_c = jnp.exp(s_c - m)
            p_p = jnp.exp(s_p - m)
            den = jnp.sum(p_c, axis=-1, keepdims=True) + jnp.sum(p_p, axis=-1, keepdims=True)
            inv = 1.0 / den
            o = jnp.einsum("gqk,gke->gqe", (p_c * inv).astype(bf16), vc.astype(bf16), preferred_element_type=f32)
            o += jnp.einsum("gqk,gke->gqe", (p_p * inv).astype(bf16), vp.astype(bf16), preferred_element_type=f32)
            outs.append(o)
            lses.append(jnp.broadcast_to(m + jnp.log(den), o.shape))
        _attn_put(o_ref, jnp.concatenate(outs, axis=-1), d)
        _attn_put(lse_ref, jnp.concatenate(lses, axis=-1), d)

    cur, before, own, _ = _attn_specs(gi, d, nt, False)
    par = pl.BlockSpec((1, HEAD), lambda hp, n: (0, 0))
    shp = jax.ShapeDtypeStruct((S, 2 * 128), f32)
    return pl.pallas_call(
        body, name=f"attn_fwd{gi}", grid=(2, nt), in_specs=cur + before + [par] * 2, out_specs=[own, own],
        out_shape=[shp, shp], compiler_params=_cparams(("arbitrary", "arbitrary")),
    )(pqkv, pqkv, pqkv, pqkv, pqkv, qg, kg)


def _attn_bwd(pqkv, o, lse, do, dlse, qg, kg, gi, S):
    d = ATTN_PAIRS[gi][1]
    nt = S // ATTN_TT
    nsub = ATTN_TT // (ATTN_BLK * d)
    nd = nsub * d

    def norm_bwd(dxn, x, rs, gain, scale):
        xh = x * rs
        dxh = dxn * (gain * scale)
        dx = rs * (dxh - xh * jnp.mean(dxh * xh, axis=-1, keepdims=True))
        dgain = jnp.sum(jnp.sum(dxn * xh * scale, axis=1), axis=0, keepdims=True)
        return dx, dgain

    def to_before(part, carried):
        return carried if nsub == 1 else jnp.concatenate([part[d:], carried], axis=0)

    def body(q_ref, k_ref, v_ref, kb_ref, vb_ref, o_ref, lse_ref, do_ref, dlse_ref, qg_ref, kg_ref,
             dq_ref, dk_ref, dv_ref, dqg_ref, dkg_ref, carry_k, carry_v):
        step = pl.program_id(1)

        @pl.when(jnp.logical_and(pl.program_id(0) == 0, step == 0))
        def _():
            dqg_ref[...] = jnp.zeros(dqg_ref.shape, f32)
            dkg_ref[...] = jnp.zeros(dkg_ref.shape, f32)

        @pl.when(step == 0)
        def _():
            carry_k[...] = jnp.zeros(carry_k.shape, f32)
            carry_v[...] = jnp.zeros(carry_v.shape, f32)

        Q, K, V = _attn_take(q_ref, d, nsub), _attn_take(k_ref, d, nsub), _attn_take(v_ref, d, nsub)
        KB, VB = _attn_take(kb_ref, d, 1), _attn_take(vb_ref, d, 1)
        O, LSE = _attn_take(o_ref, d, nsub), _attn_take(lse_ref, d, nsub)
        DO, DLSE = _attn_take(do_ref, d, nsub), _attn_take(dlse_ref, d, nsub)
        first = jnp.logical_and(lax.broadcasted_iota(jnp.int32, (nd, 1, 1), 0) < d, step == nt - 1)
        qg, kg = qg_ref[...], kg_ref[...]
        dqs, dks, dvs = [], [], []
        for h in range(2):
            sl = slice(h * HEAD, (h + 1) * HEAD)
            qx, kx, vc = Q[:, :, sl], K[:, :, sl], V[:, :, sl]
            kpx, vp = _attn_prev(kx, KB[:, :, sl], d), _attn_prev(vc, VB[:, :, sl], d)
            qn, rq = _attn_norm(qx, qg, HEAD ** -0.5)
            kn_c, rk_c = _attn_norm(kx, kg, 1.0)
            kn_p, _ = _attn_norm(kpx, kg, 1.0)
            s_c, s_p = _attn_scores(qn, kn_c, kn_p, first)
            lse = LSE[:, :, h * HEAD:h * HEAD + 1]
            p_c = jnp.exp(s_c - lse)
            p_p = jnp.exp(s_p - lse)
            dO = DO[:, :, sl]
            dOb = dO.astype(bf16)
            dp_c = jnp.einsum("gqe,gke->gqk", dOb, vc.astype(bf16), preferred_element_type=f32)
            dp_p = jnp.einsum("gqe,gke->gqk", dOb, vp.astype(bf16), preferred_element_type=f32)
            corr = DLSE[:, :, h * HEAD:h * HEAD + 1] - jnp.sum(dO * O[:, :, sl], axis=-1, keepdims=True)
            ds_c = (p_c * (dp_c + corr)).astype(bf16)
            ds_p = (p_p * (dp_p + corr)).astype(bf16)
            qnb = qn.astype(bf16)
            dqn = (jnp.einsum("gqk,gke->gqe", ds_c, kn_c.astype(bf16), preferred_element_type=f32)
                   + jnp.einsum("gqk,gke->gqe", ds_p, kn_p.astype(bf16), preferred_element_type=f32))
            dkn_p = jnp.einsum("gqk,gqe->gke", ds_p, qnb, preferred_element_type=f32)
            dv_p = jnp.einsum("gqk,gqe->gke", p_p.astype(bf16), dOb, preferred_element_type=f32)
            dkn = jnp.einsum("gqk,gqe->gke", ds_c, qnb, preferred_element_type=f32) + to_before(dkn_p, carry_k[h])
            dv = (jnp.einsum("gqk,gqe->gke", p_c.astype(bf16), dOb, preferred_element_type=f32)
                  + to_before(dv_p, carry_v[h]))
            carry_k[h] = dkn_p[:d]
            carry_v[h] = dv_p[:d]
            dq, dqg = norm_bwd(dqn, qx, rq, qg, HEAD ** -0.5)
            dk, dkg = norm_bwd(dkn, kx, rk_c, kg, 1.0)
            dqg_ref[...] += dqg
            dkg_ref[...] += dkg
            dqs.append(dq)
            dks.append(dk)
            dvs.append(dv)
        _attn_put(dq_ref, jnp.concatenate(dqs, axis=-1), d)
        _attn_put(dk_ref, jnp.concatenate(dks, axis=-1), d)
        _attn_put(dv_ref, jnp.concatenate(dvs, axis=-1), d)

    cur, before, own, _ = _attn_specs(gi, d, nt, True)
    par = pl.BlockSpec((1, HEAD), lambda hp, n: (0, 0))
    shp = jax.ShapeDtypeStruct((S, 2 * 128), f32)
    pshp = jax.ShapeDtypeStruct((1, HEAD), f32)
    return pl.pallas_call(
        body, name=f"attn_bwd{gi}", grid=(2, nt), in_specs=cur + before + [own] * 4 + [par] * 2,
        out_specs=[own] * 3 + [par] * 2, out_shape=[shp] * 3 + [pshp] * 2,
        scratch_shapes=[pltpu.VMEM((2, d, ATTN_BLK, HEAD), f32)] * 2,
        compiler_params=_cparams(("arbitrary", "arbitrary")),
    )(pqkv, pqkv, pqkv, pqkv, pqkv, o, lse, do, dlse, qg, kg)


def _rms(x, g):
    rs = lax.rsqrt(jnp.mean(x * x, axis=-1, keepdims=True) + RMS_EPS)
    return x * rs * g


def _f_rms(x, g):
    return _rms(x, g)


def _f_resid_rms(coef, x, f, g):
    xn = x + coef * f
    return xn, _rms(xn, g)


def _f_rms_bwd(n_parts, *args):
    dns = args[:n_parts]
    x, dres, g = args[n_parts:]
    dn = dns[0]
    for t in dns[1:]:
        dn = dn + t
    rs = lax.rsqrt(jnp.mean(x * x, axis=-1, keepdims=True) + RMS_EPS)
    xh = x * rs
    dxh = dn * g
    dx = dres + rs * (dxh - xh * jnp.mean(dxh * xh, axis=-1, keepdims=True))
    return dx, dx, jnp.sum(dn * xh, axis=0, keepdims=True)


def _f_loss(x, f, tgt):
    y = x + 0.5 * f
    diff = y - tgt
    part = 0.5 * jnp.sum(jnp.mean(diff * diff, axis=-1, keepdims=True), axis=0, keepdims=True)
    dy = diff * (1.0 / D)
    return dy, dy, jnp.broadcast_to(part, (1, 128))


def _dotb(a, b, dims):
    return lax.dot_general(a.astype(bf16), b.astype(bf16), dims, preferred_element_type=f32)


_NN = (((1,), (0,)), ((), ()))
_NT = (((1,), (1,)), ((), ()))
_TN = (((0,), (0,)), ((), ()))


def _rwkv_pre_core(prkv, prkv_prev, plora, plora_prev, mu_rkv, mu_lora, w0, w2p, a0, a2p, g2p, k_k, k_a):
    xs = prkv + (prkv_prev - prkv) * mu_rkv
    xl = plora + (plora_prev - plora) * mu_lora
    r, k, v = xs[:, :D], xs[:, D:2 * D], xs[:, 2 * D:]
    wd, ad, gd = xl[:, :128], xl[:, 128:256], xl[:, 256:]
    tw = jnp.tanh(wd)
    zw = w0 + _dotb(tw, w2p, _NN)
    sp = jnp.maximum(-zw, 0.0) + jnp.log(1.0 + jnp.exp(-jnp.abs(zw)))
    lw = -jnp.exp(-sp - 0.5)
    a = jax.nn.sigmoid(a0 + _dotb(ad, a2p, _NN))
    sg = jax.nn.sigmoid(gd)
    return dict(r=r, k=k, v=v, tw=tw, zw=zw, lw=lw, a=a, sg=sg, ad=ad)


def _rows_down(x, halo, blk):
    before = jnp.where(blk > 0, halo[HALO - 1:HALO, :], 0.0)
    row = lax.broadcasted_iota(jnp.int32, (x.shape[0], 1), 0)
    return jnp.where(row == 0, before, pltpu.roll(x, 1, 0))


def _rows_up(x, after):
    n = x.shape[0]
    row = lax.broadcasted_iota(jnp.int32, (n, 1), 0)
    return jnp.where(row == n - 1, after, pltpu.roll(x, n - 1, 0))


def _f_rwkv_pre(prkv, plora, mu_rkv, mu_lora, w0, w2p, a0, a2p, g2p, k_k, k_a, halo_rkv, halo_lora, blk):
    c = _rwkv_pre_core(prkv, _rows_down(prkv, halo_rkv, blk), plora, _rows_down(plora, halo_lora, blk),
                       mu_rkv, mu_lora, w0, w2p, a0, a2p, g2p, k_k, k_a)
    g = _dotb(c["sg"], g2p, _NN)
    k, a = c["k"], c["a"]
    return c["r"], c["lw"], k * (1.0 + (a - 1.0) * k_a), c["v"], k * k_k, a, g


def _f_rwkv_pre_bwd(prkv, plora, dr, dlw, dk2, dv, dkkr, da, dya, yap,
                    mu_rkv, mu_lora, w0, w2p, a0, a2p, g2p, k_k, k_a, halo_rkv, halo_lora, next_rkv, next_lora, blk):
    prkv_prev, plora_prev = _rows_down(prkv, halo_rkv, blk), _rows_down(plora, halo_lora, blk)
    c = _rwkv_pre_core(prkv, prkv_prev, plora, plora_prev, mu_rkv, mu_lora, w0, w2p, a0, a2p, g2p, k_k, k_a)
    k, a, sg, tw, zw, lw = c["k"], c["a"], c["sg"], c["tw"], c["zw"], c["lw"]
    dg = dya * yap
    dsg = _dotb(dg, g2p, _NT)
    dgd = dsg * sg * (1.0 - sg)
    dg2p = _dotb(sg, dg, _TN)
    dk = dk2 * (1.0 + (a - 1.0) * k_a) + dkkr * k_k
    da_t = da + dk2 * k * k_a
    dk_a = jnp.sum(dk2 * k * (a - 1.0), axis=0, keepdims=True)
    dk_k = jnp.sum(dkkr * k, axis=0, keepdims=True)
    dza = da_t * a * (1.0 - a)
    da0 = jnp.sum(dza, axis=0, keepdims=True)
    dad = _dotb(dza, a2p, _NT)
    da2p = _dotb(c["ad"], dza, _TN)
    dzw = dlw * lw * jax.nn.sigmoid(-zw)
    dw0 = jnp.sum(dzw, axis=0, keepdims=True)
    dtw = _dotb(dzw, w2p, _NT)
    dw2p = _dotb(tw, dzw, _TN)
    dwd = dtw * (1.0 - tw * tw)
    dxs = jnp.concatenate([dr, dk, dv], axis=1)
    dxl = jnp.concatenate([dwd, dad, dgd], axis=1)
    dmu_rkv = jnp.sum(dxs * (prkv_prev - prkv), axis=0, keepdims=True)
    dmu_lora = jnp.sum(dxl * (plora_prev - plora), axis=0, keepdims=True)
    to_next_rkv, to_next_lora = dxs * mu_rkv, dxl * mu_lora
    return (dxs * (1.0 - mu_rkv) + _rows_up(to_next_rkv, next_rkv), dxl * (1.0 - mu_lora) + _rows_up(to_next_lora, next_lora),
            dmu_rkv, dmu_lora, dw0, da0, dk_k, dk_a, dw2p, da2p, dg2p, to_next_rkv[0:1], to_next_lora[0:1])


def _group_alpha(l0, l1, l2):
    m = jnp.maximum(jnp.maximum(l0, l1), l2)
    e0, e1, e2 = jnp.exp(l0 - m), jnp.exp(l1 - m), jnp.exp(l2 - m)
    inv = 1.0 / (e0 + e1 + e2)
    return jnp.concatenate([e0 * inv, e1 * inv, e2 * inv], axis=1)


def _f_combine(o0, o1, o2, l0, l1, l2):
    return jnp.concatenate([o0, o1, o2], axis=1) * _group_alpha(l0, l1, l2)


def _f_combine_bwd(dyb, o0, o1, o2, l0, l1, l2, bd):
    alpha = _group_alpha(l0, l1, l2)
    hi, lo = _sp(dyb * jnp.concatenate([o0, o1, o2], axis=1))
    ones = bd.astype(bf16)
    e = jnp.dot(hi, ones, preferred_element_type=f32) + jnp.dot(lo, ones, preferred_element_type=f32)
    ae = alpha * e
    tot = ae[:, :256] + ae[:, 256:512] + ae[:, 512:]
    do = dyb * alpha
    dl = ae - alpha * jnp.concatenate([tot, tot, tot], axis=1)
    return do[:, :256], do[:, 256:512], do[:, 512:], dl[:, :256], dl[:, 256:512], dl[:, 512:]


def _f_merge(pgate, ta, tb, b_gate):
    gate = jax.nn.sigmoid(pgate + b_gate)
    return gate[:, :D] * ta + gate[:, D:] * tb


def _f_merge_bwd(dm, pgate, ta, tb, b_gate):
    gate = jax.nn.sigmoid(pgate + b_gate)
    ga, gb = gate[:, :D], gate[:, D:]
    dpg = jnp.concatenate([dm * ta * ga * (1.0 - ga), dm * tb * gb * (1.0 - gb)], axis=1)
    return dm * ga, dm * gb, dpg, jnp.sum(dpg, axis=0, keepdims=True)


def _f_adamw(w, g, m, v):
    m2 = ADAM_B1 * m + (1.0 - ADAM_B1) * g
    v2 = ADAM_B2 * v + (1.0 - ADAM_B2) * jnp.square(g)
    m_hat = m2 / (1.0 - ADAM_B1 ** ADAM_STEP)
    v_hat = v2 / (1.0 - ADAM_B2 ** ADAM_STEP)
    delta = -ADAM_LR * (m_hat / (jnp.sqrt(v_hat) + ADAM_EPS) + ADAM_WD * w)
    return delta, m2, v2


def _ffn_bwd(tag, dxo, dxo_b, x_in, n, gate, up, act, g, WiT, Wo, cross=None):
    du = _ffn_dact(f"{tag}_dact", dxo_b, Wo, gate, up)
    dWo = _mm(f"{tag}_dwo", act, dxo_b, "tn", out_dtype=GRAD_WIRE, scale=0.5)
    drms = (functools.partial(_f_rms_bwd, 1), [x_in, dxo], [g], [(D, f32), (D, bf16)], [(1, D)])
    dx, dx_b, dg, *recv = _mm(f"{tag}_dn", du, WiT, "nn", cross=cross, epilogue=drms)
    dWiT = _mm(f"{tag}_dwi", du, n, "tn", out_dtype=GRAD_WIRE)
    return dx, dx_b, dg, dWiT, dWo, (recv[0] if recv else None)


def _local_step(x0, tgt, W, P, hooks=None):
    S = x0.shape[0]
    (n1,) = _rowwise("f1_rms", _f_rms, [x0], [P["ffn1_norm"]], [(D, bf16)])
    hooks = hooks or {}
    if "gather_mid" in hooks:
        pack, weights = hooks["gather_mid"]
        gate1, up1, act1, gathered = _ffn_up("f1_up", n1, W["f1_iT"], gather=pack)
        W = {**W, **weights(gathered)}
    else:
        gate1, up1, act1 = _ffn_up("f1_up", n1, W["f1_iT"])
    mix_rms = (lambda f, x, g: _f_resid_rms(0.5, x, f, g), [x0], [P["mix_norm"]], [(D, f32), (D, bf16)], [])
    if "gather_in" in hooks:
        pack, weights = hooks["gather_in"]
        x1, h, gathered = _mm("f1_down", act1, W["f1_o"], "nn", gather=pack, epilogue=mix_rms)
        W = {**W, **weights(gathered)}
    else:
        x1, h = _mm("f1_down", act1, W["f1_o"], "nn", epilogue=mix_rms)
    prkv = _mm("p_rkv", h, W["in_rkvT"], "nt")
    plora = _mm("p_lora", h, W["in_loraT"], "nt")
    pqkv = _mm("p_qkv", h, W["in_qkvT"], "nt")
    pgate = _mm("p_gate", h, W["in_gateT"], "nt")
    pre_params = [P["mu_rkv"], P["mu_lora"], P["w0"], W["w2p"], P["a0"], W["a2p"], W["g2p"], P["k_k"], P["k_a"]]
    r, lw, k2, v, kkr, a, g = _rowwise("rwkv_pre", _f_rwkv_pre, [prkv, plora], pre_params, [(D, f32)] * 7, tm=128,
                                       halos=(0, 1))
    hm = [r, lw, k2, v, kkr, a]
    hp = [P["r_k"].reshape(RW_HEADS, 1, HEAD), P["ln_w"].reshape(RW_HEADS, 1, HEAD), P["ln_b"].reshape(RW_HEADS, 1, HEAD)]
    if "gather_late" in hooks:
        pack, weights = hooks["gather_late"]
        yap, ya, wkv_h, U_h, inv_h, S0s, gathered = _wkv_fwd(*hm, g, *hp, late_pack=pack)
        W = {**W, **weights(gathered)}
    else:
        yap, ya, wkv_h, U_h, inv_h, S0s = _wkv_fwd(*hm, g, *hp)
    ta = _mm("proj_a", ya, W["pr"], "nn")
    n_grp = len(ATTN_PAIRS)
    attn = [_attn_fwd(pqkv, P["q_norm"], P["k_norm"], gi, S) for gi in range(n_grp)]
    o_g, lse_g = [t[0] for t in attn], [t[1] for t in attn]
    (yb,) = _rowwise("attn_combine", _f_combine, [*o_g, *lse_g], [], [(ATTN_W, bf16)])
    tb = _mm("proj_b", yb, W["paT"], "nt")
    (merged,) = _rowwise("merge", _f_merge, [pgate, ta, tb], [P["b_gate"]], [(D, bf16)])
    f2_rms = (lambda f, x, g: _f_resid_rms(1.0, x, f, g), [x1], [P["ffn2_norm"]], [(D, f32), (D, bf16)], [])
    x2, n2 = _mm("mix_out", merged, W["out"], "nn", epilogue=f2_rms)
    gate2, up2, act2 = _ffn_up("f2_up", n2, W["f2_iT"])
    loss_head = (lambda f, x, t: _f_loss(x, f, t), [x2, tgt], [], [(D, f32), (D, bf16)], [(1, 128)])
    dx3, dx3_b, loss = _mm("f2_down", act2, W["f2_o"], "nn", epilogue=loss_head)
    G, Gs = {}, {}
    dx2, dx2_b, Gs["ffn2_norm"], G["f2_iT"], G["f2_o"], _ = _ffn_bwd("f2", dx3, dx3_b, x2, n2, gate2, up2, act2,
                                                                    P["ffn2_norm"], W["f2_iT"], W["f2_o"])
    dmerged = _mm("d_merged", dx2_b, W["out"], "nt")
    G["out"] = _mm("dw_out", merged, dx2_b, "tn", out_dtype=GRAD_WIRE)
    dta, dtb, dpgate, Gs["b_gate"] = _rowwise("merge_bwd", _f_merge_bwd, [dmerged, pgate, ta, tb], [P["b_gate"]],
                                              [(D, bf16), (D, bf16), (2 * D, bf16)], [(1, 2 * D)])
    dya = _mm("d_ya", dta, W["pr"], "nt")
    G["pr"] = _mm("dw_pr", ya, dta, "tn", out_dtype=GRAD_WIRE)
    dyb = _mm("d_yb", dtb, W["paT"], "nn")
    G["paT"] = _mm("dw_pa", dtb, yb, "tn", out_dtype=GRAD_WIRE)
    if "reduce_late" in hooks:
        pieces_late = hooks["reduce_late"](G)
        hg = _wkv_bwd(dya, g, *hm, wkv_h, U_h, inv_h, S0s, *hp, late_pieces=pieces_late)
        G["late"] = (pieces_late, hg[9])
    else:
        hg = _wkv_bwd(dya, g, *hm, wkv_h, U_h, inv_h, S0s, *hp)
    dr, dlw, dk2, dv, dkkr, da = hg[:6]
    Gs["r_k"], Gs["ln_w"], Gs["ln_b"] = (t.reshape(1, D) for t in hg[6:9])
    lp = sum(LORA_PAD)
    (dprkv, dplora, Gs["mu_rkv"], Gs["mu_lora"], Gs["w0"], Gs["a0"], Gs["k_k"], Gs["k_a"],
     dw2p, da2p, dg2p) = _rowwise(
        "rwkv_pre_bwd", _f_rwkv_pre_bwd,
        [prkv, plora, dr, dlw, dk2, dv, dkkr, da, dya, yap], pre_params,
        [(3 * D, bf16), (lp, bf16)],
        [(1, 3 * D), (1, lp), (1, D), (1, D), (1, D), (1, D), (LORA_PAD[0], D), (LORA_PAD[1], D), (LORA_PAD[2], D)],
        tm=128, halos=(0, 1), carries=((1, 3 * D), (1, lp)), reverse=True)
    G["w2T"], G["a2T"], G["g2T"] = dw2p[:LORA_W[0]].T, da2p[:LORA_W[1]].T, dg2p[:LORA_W[2]].T
    bd = (jnp.arange(ATTN_W)[:, None] // HEAD == jnp.arange(ATTN_W)[None, :] // HEAD).astype(f32)
    dol = _rowwise("attn_combine_bwd", _f_combine_bwd, [dyb, *o_g, *lse_g], [bd], [(ATTN_W // n_grp, f32)] * (2 * n_grp))
    dattn = [_attn_bwd(pqkv, o_g[gi], lse_g[gi], dol[gi], dol[n_grp + gi], P["q_norm"], P["k_norm"], gi, S)
             for gi in range(n_grp)]
    Gs["q_norm"] = dattn[0][3] + dattn[1][3] + dattn[2][3]
    Gs["k_norm"] = dattn[0][4] + dattn[1][4] + dattn[2][4]
    dpqkv = jnp.concatenate([dattn[gi][kind] for kind in range(3) for gi in range(n_grp)], axis=1).astype(bf16)
    dh = [_mm("dh_rkv", dprkv, W["in_rkvT"], "nn"), _mm("dh_lora", dplora, W["in_loraT"], "nn"),
          _mm("dh_qkv", dpqkv, W["in_qkvT"], "nn"), _mm("dh_gate", dpgate, W["in_gateT"], "nn")]
    dW_rkv = _mm("dw_rkv", dprkv, h, "tn", out_dtype=GRAD_WIRE)
    dW_lora = _mm("dw_lora", dplora, h, "tn", out_dtype=GRAD_WIRE)
    dW_qkv = _mm("dw_qkv", dpqkv, h, "tn", out_dtype=GRAD_WIRE)
    dW_gate = _mm("dw_gate", dpgate, h, "tn", out_dtype=GRAD_WIRE)
    o1, o2 = LORA_PAD[0], LORA_PAD[0] + LORA_PAD[1]
    G["inT"] = jnp.concatenate([dW_rkv, dW_lora[:LORA_W[0]], dW_lora[o1:o1 + LORA_W[1]], dW_lora[o2:o2 + LORA_W[2]],
                                dW_qkv, dW_gate], axis=0)
    dx1, dx1_b, Gs["mix_norm"] = _rowwise("mix_drms", functools.partial(_f_rms_bwd, 4), [*dh, x1, dx2],
                                          [P["mix_norm"]], [(D, f32), (D, bf16)], [(1, D)])
    part_mid = hooks["reduce_mid"](G) if "reduce_mid" in hooks else None
    dx0, _, Gs["ffn1_norm"], G["f1_iT"], G["f1_o"], recv_mid = _ffn_bwd(
        "f1", dx1, dx1_b, x0, n1, gate1, up1, act1, P["ffn1_norm"], W["f1_iT"], W["f1_o"], cross=part_mid)
    G["mid"] = (part_mid, recv_mid)
    return loss[0, 0], dx0, G, Gs


def _peer(k):
    x, y, c = lax.axis_index("x"), lax.axis_index("y"), lax.axis_index("c")
    px = 1 - x if k & 4 else x
    py = 1 - y if k & 2 else y
    pc = 1 - c if k & 1 else c
    return (px, py, pc), 4 * px + 2 * py + pc


def _gather_phases(x_ref, out_ref, send_sems, recv_sems, local_sem):
    x, y, c = lax.axis_index("x"), lax.axis_index("y"), lax.axis_index("c")
    me, sibling = (x, y, c), (x, y, 1 - c)
    chips = [(1 - x, y), (x, 1 - y), (1 - x, 1 - y)]

    def slot(px, py, pc):
        return out_ref.at[4 * px + 2 * py + pc]

    def copy(k, block, to, src=None):
        return pltpu.make_async_remote_copy(
            src_ref=slot(*block) if src is None else src, dst_ref=slot(*block), send_sem=send_sems.at[k],
            recv_sem=recv_sems.at[k], device_id=to, device_id_type=MESH)

    def mine():
        return pltpu.make_async_copy(x_ref, slot(*me), local_sem)

    def first():
        return [copy(0, me, sibling, src=x_ref)] + [copy(1 + j, me, (*chip, c), src=x_ref) for j, chip in enumerate(chips)]

    def passed():
        return [copy(4 + j, (*chip, c), sibling) for j, chip in enumerate(chips)]

    def start():
        mine().start()
        for cp in first():
            cp.start()

    def forward():
        for j, (chip, cp) in enumerate(zip(chips, passed())):
            copy(1 + j, (*chip, c), me).wait_recv()
            cp.start()

    def finish():
        copy(0, sibling, me).wait_recv()
        for j, chip in enumerate(chips):
            copy(4 + j, (*chip, 1 - c), me).wait_recv()
        for cp in first() + passed():
            cp.wait_send()
        mine().wait()

    return start, forward, finish


GATHER_SEMS = [pltpu.SemaphoreType.DMA((N_DEV - 1,)), pltpu.SemaphoreType.DMA((N_DEV - 1,)), pltpu.SemaphoreType.DMA(())]


def _all_gather(pack):
    R, C = pack.shape

    def body(x_ref, out_ref, send_sems, recv_sems, local_sem):
        for phase in _gather_phases(x_ref, out_ref, send_sems, recv_sems, local_sem):
            phase()

    return pl.pallas_call(
        body, name="weight_all_gather", out_shape=jax.ShapeDtypeStruct((N_DEV, R, C), pack.dtype),
        in_specs=[pl.BlockSpec(memory_space=pl.ANY)], out_specs=pl.BlockSpec(memory_space=pl.ANY),
        scratch_shapes=GATHER_SEMS,
    )(pack)


def _cross_phases(p_ref, out_ref, send_sems, recv_sems):
    x, y, c = lax.axis_index("x"), lax.axis_index("y"), lax.axis_index("c")

    def copies():
        out = []
        for j, (fx, fy) in enumerate([(1, 0), (0, 1), (1, 1)]):
            px = 1 - x if fx else x
            py = 1 - y if fy else y
            out.append(pltpu.make_async_remote_copy(src_ref=p_ref.at[2 * px + py], dst_ref=out_ref.at[j],
                                                    send_sem=send_sems.at[j], recv_sem=recv_sems.at[j],
                                                    device_id=(px, py, c), device_id_type=MESH))
        return out

    def start():
        for cp in copies():
            cp.start()

    def finish():
        for cp in copies():
            cp.wait()

    return start, finish


CROSS_SEMS = [pltpu.SemaphoreType.DMA((3,)), pltpu.SemaphoreType.DMA((3,))]


def _direct_phases(piece_refs, rows, out_ref, send_sems, recv_sems):
    offs = [sum(rows[:i]) for i in range(len(rows))]

    def copies():
        out = []
        for i, g_ref in enumerate(piece_refs):
            for k in range(1, N_DEV):
                dev, idx = _peer(k)
                out.append(pltpu.make_async_remote_copy(
                    src_ref=g_ref.at[idx], dst_ref=out_ref.at[k - 1, pl.ds(offs[i], rows[i])],
                    send_sem=send_sems.at[i * (N_DEV - 1) + k - 1], recv_sem=recv_sems.at[i * (N_DEV - 1) + k - 1],
                    device_id=dev, device_id_type=MESH))
        return out

    def start():
        for cp in copies():
            cp.start()

    def finish():
        for cp in copies():
            cp.wait()

    return start, finish


def _sum_direct(pieces, recv, me, tag):
    n = len(pieces)
    C = pieces[0].shape[2]
    nblk = [p.shape[1] // PACK_BLOCK for p in pieces]
    lo = [sum(nblk[:i]) for i in range(n)]
    R = sum(nblk) * PACK_BLOCK

    def body(me_ref, *refs):
        g_refs, r_ref, o_ref = refs[:n], refs[n], refs[n + 1]
        rb = pl.program_id(0)
        for i in range(n):
            @pl.when(jnp.logical_and(rb >= lo[i], rb < lo[i] + nblk[i]))
            def _(g_ref=g_refs[i]):
                acc = g_ref[...].astype(f32)
                for k in range(N_DEV - 1):
                    acc = acc + r_ref[k].astype(f32)
                o_ref[...] = acc

    def piece_spec(i):
        return pl.BlockSpec((None, PACK_BLOCK, C), lambda rb, me_ref: (me_ref[0], jnp.clip(rb - lo[i], 0, nblk[i] - 1), 0))

    return pl.pallas_call(
        body, name=f"grad_sum_{tag}",
        grid_spec=pltpu.PrefetchScalarGridSpec(
            num_scalar_prefetch=1, grid=(R // PACK_BLOCK,),
            in_specs=[piece_spec(i) for i in range(n)] + [pl.BlockSpec((N_DEV - 1, PACK_BLOCK, C), lambda rb, me_ref: (0, rb, 0))],
            out_specs=pl.BlockSpec((PACK_BLOCK, C), lambda rb, me_ref: (rb, 0))),
        out_shape=jax.ShapeDtypeStruct((R, C), f32),
        compiler_params=_cparams(("arbitrary",)),
    )(me, *pieces, recv)


N_CHIP = 4


def _grad_pair(pieces, tag):
    n = len(pieces)
    C = pieces[0].shape[2]
    rows = [p.shape[1] for p in pieces]
    offs = [sum(rows[:i]) for i in range(n)]
    R = sum(rows)

    def body(*refs):
        g_refs, (other_ref, send_sems, recv_sems) = refs[:n], refs[n:]
        x, y, c = lax.axis_index("x"), lax.axis_index("y"), lax.axis_index("c")
        copies = []
        for i, g_ref in enumerate(g_refs):
            for k in range(N_CHIP):
                cp = pltpu.make_async_remote_copy(
                    src_ref=g_ref.at[4 * (k // 2) + 2 * (k % 2) + 1 - c], dst_ref=other_ref.at[k, pl.ds(offs[i], rows[i])],
                    send_sem=send_sems.at[i * N_CHIP + k], recv_sem=recv_sems.at[i * N_CHIP + k],
                    device_id=(x, y, 1 - c), device_id_type=MESH)
                cp.start()
                copies.append(cp)
        for cp in copies:
            cp.wait()

    return pl.pallas_call(
        body, name=f"grad_pair_{tag}", out_shape=jax.ShapeDtypeStruct((N_CHIP, R, C), pieces[0].dtype),
        in_specs=[pl.BlockSpec(memory_space=pl.ANY)] * n, out_specs=pl.BlockSpec(memory_space=pl.ANY),
        scratch_shapes=[pltpu.SemaphoreType.DMA((n * N_CHIP,))] * 2,
    )(*pieces)


def _pair_add(pieces, other, c, tag):
    n = len(pieces)
    C = pieces[0].shape[2]
    nblk = [p.shape[1] // PACK_BLOCK for p in pieces]
    lo = [sum(nblk[:i]) for i in range(n)]
    R = sum(nblk) * PACK_BLOCK

    def body(c_ref, *refs):
        g_refs, o_ref, out_ref = refs[:n], refs[n], refs[n + 1]
        rb = pl.program_id(1)
        for i in range(n):
            @pl.when(jnp.logical_and(rb >= lo[i], rb < lo[i] + nblk[i]))
            def _(g_ref=g_refs[i]):
                out_ref[...] = (g_ref[...].astype(f32) + o_ref[...].astype(f32)).astype(out_ref.dtype)

    def piece_spec(i):
        return pl.BlockSpec((1, None, PACK_BLOCK, C),
                            lambda k, rb, c_ref: (k, c_ref[0], jnp.clip(rb - lo[i], 0, nblk[i] - 1), 0))

    blk = pl.BlockSpec((1, PACK_BLOCK, C), lambda k, rb, c_ref: (k, rb, 0))
    return pl.pallas_call(
        body, name=f"pair_add_{tag}",
        grid_spec=pltpu.PrefetchScalarGridSpec(
            num_scalar_prefetch=1, grid=(N_CHIP, R // PACK_BLOCK),
            in_specs=[piece_spec(i) for i in range(n)] + [blk], out_specs=blk),
        out_shape=jax.ShapeDtypeStruct((N_CHIP, R, C), other.dtype),
        compiler_params=_cparams(("arbitrary", "arbitrary")),
    )(c, *[p.reshape(N_CHIP, 2, p.shape[1], C) for p in pieces], other)


def _grad_cross(part):
    _, R, C = part.shape

    def body(p_ref, out_ref, send_sems, recv_sems):
        for phase in _cross_phases(p_ref, out_ref, send_sems, recv_sems):
            phase()

    return pl.pallas_call(
        body, name="grad_cross", out_shape=jax.ShapeDtypeStruct((3, R, C), part.dtype),
        in_specs=[pl.BlockSpec(memory_space=pl.ANY)], out_specs=pl.BlockSpec(memory_space=pl.ANY),
        scratch_shapes=CROSS_SEMS,
    )(part)


def _grad_sum(part, recv, my_chip, tr, tag):
    _, R, C = part.shape

    def body(chip_ref, p_ref, r_ref, o_ref):
        acc = p_ref[0].astype(f32)
        for j in range(3):
            acc = acc + r_ref[j].astype(f32)
        o_ref[...] = acc

    return pl.pallas_call(
        body, name=f"grad_sum_{tag}",
        grid_spec=pltpu.PrefetchScalarGridSpec(
            num_scalar_prefetch=1, grid=(R // tr,),
            in_specs=[pl.BlockSpec((1, tr, C), lambda i, chip_ref: (chip_ref[0], i, 0)),
                      pl.BlockSpec((3, tr, C), lambda i, chip_ref: (0, i, 0))],
            out_specs=pl.BlockSpec((tr, C), lambda i, chip_ref: (i, 0))),
        out_shape=jax.ShapeDtypeStruct((R, C), f32),
        compiler_params=_cparams(("arbitrary",)),
    )(my_chip, part, recv)


def _small_all_reduce(small):
    R, C = small.shape

    def body(x_ref, o_ref, buf, send_sems, recv_sems):
        _, me = _peer(0)
        buf[me] = x_ref[...]
        sends = []
        for k in range(1, N_DEV):
            dev, _ = _peer(k)
            cp = pltpu.make_async_remote_copy(src_ref=x_ref, dst_ref=buf.at[me], send_sem=send_sems.at[k - 1],
                                              recv_sem=recv_sems.at[k - 1], device_id=dev, device_id_type=MESH)
            cp.start()
            sends.append(cp)
        for k in range(1, N_DEV):
            dev, idx = _peer(k)
            pltpu.make_async_remote_copy(src_ref=x_ref, dst_ref=buf.at[idx], send_sem=send_sems.at[k - 1],
                                         recv_sem=recv_sems.at[k - 1], device_id=dev, device_id_type=MESH).wait_recv()
        for cp in sends:
            cp.wait_send()
        acc = buf[0]
        for i in range(1, N_DEV):
            acc = acc + buf[i]
        o_ref[...] = acc

    return pl.pallas_call(
        body, name="small_all_reduce", out_shape=jax.ShapeDtypeStruct((R, C), f32),
        in_specs=[pl.BlockSpec(memory_space=pltpu.VMEM)], out_specs=pl.BlockSpec(memory_space=pltpu.VMEM),
        scratch_shapes=[pltpu.VMEM((N_DEV, R, C), f32), pltpu.SemaphoreType.DMA((N_DEV - 1,)),
                        pltpu.SemaphoreType.DMA((N_DEV - 1,))],
    )(small)


_LORA = (("rwkv_w2", True), ("rwkv_a2", True), ("rwkv_g2", True))
_GROUPS_FIRST = ((("ffn1_w_in", True),),)
_GROUPS_MID = ((("ffn1_w_out", False),), _LORA)
_GROUPS_IN = ((("w_in", True),),)
_GROUPS_LATE = ((("w_proj_rwkv", False),), (("w_proj_attn", True),), (("w_out", False),),
                (("ffn2_w_in", True),), (("ffn2_w_out", False),))
_GRADS_MID = ((("w_in", True),), _LORA)
_GRADS_LAST = ((("ffn1_w_in", True),), (("ffn1_w_out", False),))
_BIG = tuple(item for group in _GROUPS_FIRST + _GROUPS_MID + _GROUPS_IN + _GROUPS_LATE for item in group)
_SMALL = ("ffn1_norm", "mix_norm", "b_gate", "rwkv_mu", "rwkv_w0", "rwkv_a0", "rwkv_k_k", "rwkv_k_a", "rwkv_r_k",
          "rwkv_ln_w", "rwkv_ln_b", "attn_q_norm", "attn_k_norm", "ffn2_norm")


def _pack_layout(like, groups):
    items, spans, off = {}, [], 0
    for group in groups:
        start = off
        for name, _ in group:
            shp = like[name].shape
            n = shp[0] * shp[1] // D
            items[name] = (off, n)
            off += n
        off = -(-off // PACK_BLOCK) * PACK_BLOCK
        spans.append((start, off - start))
    return items, spans, off


def _pack_big(shards, groups):
    items, _, rows = _pack_layout(shards, groups)
    parts, at = [], 0
    for group in groups:
        for name, tr in group:
            off, n = items[name]
            t = shards[name]
            if off > at:
                parts.append(jnp.zeros((off - at, D), t.dtype))
            parts.append((t.T if tr else t).reshape(n, D))
            at = off + n
    if rows > at:
        parts.append(jnp.zeros((rows - at, D), parts[0].dtype))
    return jnp.concatenate(parts, axis=0)


def _unpack_big(pack, like, groups):
    items, _, _ = _pack_layout(like, groups)
    out = {}
    for group in groups:
        for name, tr in group:
            off, n = items[name]
            shp = like[name].shape
            t = pack[off:off + n]
            out[name] = t.reshape(shp[1], shp[0]).T if tr else t.reshape(shp)
    return out


def _unpack_gathered(gathered, like, groups):
    items, _, _ = _pack_layout(like, groups)
    full = {}
    for group in groups:
        for name, tr in group:
            shp = like[name].shape
            off, rows = items[name]
            r_loc, c_loc = (shp[1], shp[0]) if tr else shp
            full[name] = gathered[:, off:off + rows].reshape(N_DEV * r_loc, c_loc)
    return full


def _grad_pieces(g_full, like, groups):
    items, spans, _ = _pack_layout(like, groups)
    pieces = []
    for group, (_, rows_pad) in zip(groups, spans):
        parts = [g_full[n].astype(GRAD_WIRE).reshape(N_DEV, items[n][1], D) for n, _ in group]
        piece = parts[0] if len(parts) == 1 else jnp.concatenate(parts, axis=1)
        if rows_pad > piece.shape[1]:
            piece = jnp.pad(piece, ((0, 0), (0, rows_pad - piece.shape[1]), (0, 0)))
        pieces.append(piece)
    return pieces


def _small_rows(name, t):
    flat = t.reshape(-1)
    pad = (-flat.shape[0]) % D
    return jnp.pad(flat, (0, pad)).reshape(-1, D)


def _pack_small(vals):
    parts = [_small_rows(n, vals[n]) for n in _SMALL]
    used = sum(p.shape[0] for p in parts)
    parts.append(jnp.zeros((SMALL_ROWS - used, D), f32))
    return jnp.concatenate(parts, axis=0)


def _unpack_small(pack, like):
    out, off = {}, 0
    for n in _SMALL:
        size = like[n].size
        rows = -(-size // D)
        out[n] = pack[off:off + rows].reshape(-1)[:size].reshape(like[n].shape)
        off += rows
    return out


def _build_W_mid(full):
    dt = full["rwkv_w2"].dtype
    z64, z96 = jnp.zeros((64, D), dt), jnp.zeros((96, D), dt)
    return {
        "f1_o": full["ffn1_w_out"],
        "w2p": jnp.concatenate([full["rwkv_w2"].T, z64], axis=0),
        "a2p": jnp.concatenate([full["rwkv_a2"].T, z64], axis=0),
        "g2p": jnp.concatenate([full["rwkv_g2"].T, z96], axis=0),
    }


def _build_W_in(full):
    inT = full["w_in"]
    z64, z96 = jnp.zeros((64, D), inT.dtype), jnp.zeros((96, D), inT.dtype)
    return {
        "in_rkvT": inT[:3 * D],
        "in_loraT": jnp.concatenate([inT[3072:3136], z64, inT[3136:3200], z64, inT[3200:3360], z96], axis=0),
        "in_qkvT": inT[3360:3360 + 3 * ATTN_W], "in_gateT": inT[3360 + 3 * ATTN_W:],
    }


def _build_W_late(full):
    return {"pr": full["w_proj_rwkv"], "paT": full["w_proj_attn"], "out": full["w_out"],
            "f2_iT": full["ffn2_w_in"], "f2_o": full["ffn2_w_out"]}


def _build_W_first(full):
    return {"f1_iT": full["ffn1_w_in"]}


def _build_W(full):
    return {**_build_W_first(full), **_build_W_mid(full), **_build_W_in(full), **_build_W_late(full)}


_G_NAMES = {"ffn1_w_in": "f1_iT", "ffn1_w_out": "f1_o", "w_in": "inT", "rwkv_w2": "w2T", "rwkv_a2": "a2T",
            "rwkv_g2": "g2T", "w_proj_rwkv": "pr", "w_proj_attn": "paT", "w_out": "out", "ffn2_w_in": "f2_iT",
            "ffn2_w_out": "f2_o"}


def _named_grads(G, groups):
    return {n: G[_G_NAMES[n]] for group in groups for n, _ in group}


def _reduce_start(G, like, groups, my_c, tag):
    pieces = _grad_pieces(_named_grads(G, groups), like, groups)
    return _pair_add(pieces, _grad_pair(pieces, tag), my_c, tag)


def _build_P(Wl):
    mu = Wl["rwkv_mu"]
    z64f, z96f = jnp.zeros((1, 64), f32), jnp.zeros((1, 96), f32)
    return {
        "ffn1_norm": Wl["ffn1_norm"][None], "mix_norm": Wl["mix_norm"][None], "ffn2_norm": Wl["ffn2_norm"][None],
        "b_gate": Wl["b_gate"][None], "mu_rkv": mu[None, :3 * D],
        "mu_lora": jnp.concatenate([mu[None, 3072:3136], z64f, mu[None, 3136:3200], z64f, mu[None, 3200:3360], z96f], axis=1),
        "w0": Wl["rwkv_w0"][None], "a0": Wl["rwkv_a0"][None], "k_k": Wl["rwkv_k_k"][None], "k_a": Wl["rwkv_k_a"][None],
        "r_k": Wl["rwkv_r_k"].reshape(1, D), "ln_w": Wl["rwkv_ln_w"][None], "ln_b": Wl["rwkv_ln_b"][None],
        "q_norm": Wl["attn_q_norm"][None], "k_norm": Wl["attn_k_norm"][None],
    }


def kernel(x, ffn1_norm, ffn1_w_in, ffn1_w_out, mix_norm, w_in, b_gate, rwkv_mu, rwkv_w0, rwkv_w2, rwkv_a0, rwkv_a2, rwkv_g2, rwkv_k_k, rwkv_k_a, rwkv_r_k, rwkv_ln_w, rwkv_ln_b, attn_q_norm, attn_k_norm, w_proj_rwkv, w_proj_attn, w_out, ffn2_norm, ffn2_w_in, ffn2_w_out, loss_target, m_ffn1_norm, m_ffn1_w_in, m_ffn1_w_out, m_mix_norm, m_w_in, m_b_gate, m_rwkv_mu, m_rwkv_w0, m_rwkv_w2, m_rwkv_a0, m_rwkv_a2, m_rwkv_g2, m_rwkv_k_k, m_rwkv_k_a, m_rwkv_r_k, m_rwkv_ln_w, m_rwkv_ln_b, m_attn_q_norm, m_attn_k_norm, m_w_proj_rwkv, m_w_proj_attn, m_w_out, m_ffn2_norm, m_ffn2_w_in, m_ffn2_w_out, v_ffn1_norm, v_ffn1_w_in, v_ffn1_w_out, v_mix_norm, v_w_in, v_b_gate, v_rwkv_mu, v_rwkv_w0, v_rwkv_w2, v_rwkv_a0, v_rwkv_a2, v_rwkv_g2, v_rwkv_k_k, v_rwkv_k_a, v_rwkv_r_k, v_rwkv_ln_w, v_rwkv_ln_b, v_attn_q_norm, v_attn_k_norm, v_w_proj_rwkv, v_w_proj_attn, v_w_out, v_ffn2_norm, v_ffn2_w_in, v_ffn2_w_out):
    names = ("ffn1_norm", "ffn1_w_in", "ffn1_w_out", "mix_norm", "w_in", "b_gate", "rwkv_mu", "rwkv_w0", "rwkv_w2",
             "rwkv_a0", "rwkv_a2", "rwkv_g2", "rwkv_k_k", "rwkv_k_a", "rwkv_r_k", "rwkv_ln_w", "rwkv_ln_b",
             "attn_q_norm", "attn_k_norm", "w_proj_rwkv", "w_proj_attn", "w_out", "ffn2_norm", "ffn2_w_in", "ffn2_w_out")
    w_all = (ffn1_norm, ffn1_w_in, ffn1_w_out, mix_norm, w_in, b_gate, rwkv_mu, rwkv_w0, rwkv_w2, rwkv_a0, rwkv_a2,
             rwkv_g2, rwkv_k_k, rwkv_k_a, rwkv_r_k, rwkv_ln_w, rwkv_ln_b, attn_q_norm, attn_k_norm, w_proj_rwkv,
             w_proj_attn, w_out, ffn2_norm, ffn2_w_in, ffn2_w_out)
    m_all = (m_ffn1_norm, m_ffn1_w_in, m_ffn1_w_out, m_mix_norm, m_w_in, m_b_gate, m_rwkv_mu, m_rwkv_w0, m_rwkv_w2,
             m_rwkv_a0, m_rwkv_a2, m_rwkv_g2, m_rwkv_k_k, m_rwkv_k_a, m_rwkv_r_k, m_rwkv_ln_w, m_rwkv_ln_b,
             m_attn_q_norm, m_attn_k_norm, m_w_proj_rwkv, m_w_proj_attn, m_w_out, m_ffn2_norm, m_ffn2_w_in, m_ffn2_w_out)
    v_all = (v_ffn1_norm, v_ffn1_w_in, v_ffn1_w_out, v_mix_norm, v_w_in, v_b_gate, v_rwkv_mu, v_rwkv_w0, v_rwkv_w2,
             v_rwkv_a0, v_rwkv_a2, v_rwkv_g2, v_rwkv_k_k, v_rwkv_k_a, v_rwkv_r_k, v_rwkv_ln_w, v_rwkv_ln_b,
             v_attn_q_norm, v_attn_k_norm, v_w_proj_rwkv, v_w_proj_attn, v_w_out, v_ffn2_norm, v_ffn2_w_in, v_ffn2_w_out)
    Wl = {n: t[0] for n, t in zip(names, w_all)}
    Ml = {n: t[0] for n, t in zip(names, m_all)}
    Vl = {n: t[0] for n, t in zip(names, v_all)}
    big = [n for n, _ in _BIG]

    my_c = lax.axis_index("c").astype(jnp.int32).reshape(1)
    my_chip = (2 * lax.axis_index("x") + lax.axis_index("y")).astype(jnp.int32).reshape(1)

    def pack(groups):
        return _pack_big(Wl, groups).astype(bf16)

    gathered = _all_gather(pack(_GROUPS_FIRST))
    W, P = _build_W_first(_unpack_gathered(gathered, Wl, _GROUPS_FIRST)), _build_P(Wl)
    hooks = {"gather_mid": (pack(_GROUPS_MID), lambda g: _build_W_mid(_unpack_gathered(g, Wl, _GROUPS_MID))),
             "gather_in": (pack(_GROUPS_IN), lambda g: _build_W_in(_unpack_gathered(g, Wl, _GROUPS_IN))),
             "gather_late": (pack(_GROUPS_LATE), lambda g: _build_W_late(_unpack_gathered(g, Wl, _GROUPS_LATE))),
             "reduce_mid": lambda G: _reduce_start(G, Wl, _GRADS_MID, my_c, "mid"),
             "reduce_late": lambda G: _grad_pieces(_named_grads(G, _GROUPS_LATE), Wl, _GROUPS_LATE)}

    loss_local, dx0, G, Gs = _local_step(x[0], loss_target[0], W, P, hooks)

    part_last = _reduce_start(G, Wl, _GRADS_LAST, my_c, "last")
    g_big = _unpack_big(_grad_sum(part_last, _grad_cross(part_last), my_chip, 128, "last"), Wl, _GRADS_LAST)
    g_big.update(_unpack_big(_grad_sum(*G["mid"], my_chip, 128, "mid"), Wl, _GRADS_MID))
    me = (4 * lax.axis_index("x") + 2 * lax.axis_index("y") + lax.axis_index("c")).astype(jnp.int32).reshape(1)
    g_big.update(_unpack_big(_sum_direct(*G["late"], me, "late"), Wl, _GROUPS_LATE))

    mu_g = Gs["mu_rkv"], Gs["mu_lora"]
    o1, o2 = LORA_PAD[0], LORA_PAD[0] + LORA_PAD[1]
    g_small_local = {
        "ffn1_norm": Gs["ffn1_norm"], "mix_norm": Gs["mix_norm"], "b_gate": Gs["b_gate"],
        "rwkv_mu": jnp.concatenate([mu_g[0], mu_g[1][:, :64], mu_g[1][:, o1:o1 + 64], mu_g[1][:, o2:o2 + 160]], axis=1),
        "rwkv_w0": Gs["w0"], "rwkv_a0": Gs["a0"], "rwkv_k_k": Gs["k_k"], "rwkv_k_a": Gs["k_a"], "rwkv_r_k": Gs["r_k"],
        "rwkv_ln_w": Gs["ln_w"], "rwkv_ln_b": Gs["ln_b"], "attn_q_norm": Gs["q_norm"], "attn_k_norm": Gs["k_norm"],
        "ffn2_norm": Gs["ffn2_norm"]}
    gs_pack = _small_all_reduce(_pack_small(g_small_local))

    out_g, out_d, out_m, out_v = dict(g_big), {}, {}, {}
    for n in big:
        cols = Wl[n].shape[1]
        out_d[n], out_m[n], out_v[n] = _rowwise(f"adamw_{n}", _f_adamw, [Wl[n], g_big[n], Ml[n], Vl[n]], [],
                                                 [(cols, f32)] * 3)
    ds_pack, ms_pack, vs_pack = _rowwise(
        "adamw_small", _f_adamw, [_pack_small(Wl), gs_pack, _pack_small(Ml), _pack_small(Vl)], [], [(D, f32)] * 3)
    for out, pack in ((out_g, gs_pack), (out_d, ds_pack), (out_m, ms_pack), (out_v, vs_pack)):
        out.update(_unpack_small(pack, Wl))

    loss = lax.psum(loss_local, ("x", "y", "c"))
    return (loss, dx0[None], *[out_g[n][None] for n in names], *[out_d[n][None] for n in names],
            *[out_m[n][None] for n in names], *[out_v[n][None] for n in names])
```

```python
import functools

import jax
import jax.numpy as jnp
from jax import lax
from jax.experimental import pallas as pl
from jax.experimental.pallas import tpu as pltpu

f32 = jnp.float32
bf16 = jnp.bfloat16
MESH = pl.DeviceIdType.MESH

N_DEV = 8
D = 1024
D_FF = 2816
HEAD = 64
RW_HEADS = 16
ATTN_PAIRS = ((128, 1), (512, 4), (2048, 16))
ATTN_BLK = 128
HEADS_PER_GROUP = 4
ATTN_W = 768
LORA_PAD = (128, 128, 256)
LORA_W = (64, 64, 160)
GN_EPS = 64e-5
RMS_EPS = 1e-6
NEG_INF = -1e30
WKV_T = 64
WKV_SUB = 2
GRAD_WIRE = bf16
PACK_BLOCK = 128
SMALL_ROWS = 24
VMEM_LIMIT = 56 * 1024 * 1024

ADAM_LR, ADAM_B1, ADAM_B2, ADAM_EPS, ADAM_WD, ADAM_STEP = 0.001, 0.9, 0.999, 1e-08, 0.01, 10


def _cparams(sem):
    return pltpu.CompilerParams(dimension_semantics=sem, vmem_limit_bytes=VMEM_LIMIT)


def _pick(n, cands):
    for c in cands:
        if n % c == 0:
            return c
    return n


HALO = 8


def _rowwise(name, fn, rows, params, outs, accs=(), tm=256, halos=(), carries=(), reverse=False):
    S = rows[0].shape[0]
    tm = min(tm, S)
    while S % tm:
        tm -= 8
    nb = S // tm
    n_in = len(rows) + len(params) + len(halos)
    n_out = len(outs)
    n_acc = len(accs)
    n_car = len(carries)

    def blk_of(i):
        return nb - 1 - i if reverse else i

    def body(*refs):
        step = pl.program_id(0)
        carry_refs = refs[n_in + n_out + n_acc:]
        if n_car:
            @pl.when(step == 0)
            def _():
                for c_ref in carry_refs:
                    c_ref[...] = jnp.zeros(c_ref.shape, f32)
        args = [r[...] for r in refs[:n_in]] + [c[...] for c in carry_refs]
        res = fn(*args, blk=blk_of(step)) if (halos or carries) else fn(*args)
        if not isinstance(res, (tuple, list)):
            res = (res,)
        out_refs = refs[n_in:n_in + n_out + n_acc]
        for j in range(n_out):
            out_refs[j][...] = res[j].astype(out_refs[j].dtype)
        if n_acc:
            @pl.when(step == 0)
            def _():
                for j in range(n_acc):
                    out_refs[n_out + j][...] = jnp.zeros(out_refs[n_out + j].shape, f32)
            for j in range(n_acc):
                out_refs[n_out + j][...] += res[n_out + j]
        for j in range(n_car):
            carry_refs[j][...] = res[n_out + n_acc + j]

    in_specs = [pl.BlockSpec((tm, a.shape[1]), lambda i: (blk_of(i), 0)) for a in rows]
    in_specs += [pl.BlockSpec(p.shape, lambda i, nd=p.ndim: (0,) * nd) for p in params]
    in_specs += [pl.BlockSpec((HALO, rows[h].shape[1]), lambda i: (jnp.maximum(blk_of(i) * (tm // HALO) - 1, 0), 0))
                 for h in halos]
    out_specs = [pl.BlockSpec((tm, w), lambda i: (blk_of(i), 0)) for w, _ in outs]
    out_specs += [pl.BlockSpec(s, lambda i: (0, 0)) for s in accs]
    out_shape = [jax.ShapeDtypeStruct((S, w), dt) for w, dt in outs]
    out_shape += [jax.ShapeDtypeStruct(s, f32) for s in accs]
    res = pl.pallas_call(
        body, name=name, grid=(nb,), in_specs=in_specs, out_specs=out_specs, out_shape=out_shape,
        scratch_shapes=[pltpu.VMEM(s, f32) for s in carries],
        compiler_params=_cparams(("arbitrary",)),
    )(*rows, *params, *[rows[h] for h in halos])
    return res


MM_VMEM_BUDGET = 40 * 1024 * 1024
MM_STEP_US = 0.35
MM_FLOPS_PER_US = 9.0e8
MM_HBM_BYTES_PER_US = 3.0e6


def _tile_options(n, cap):
    opts = [d for d in range(128, min(n, cap) + 1, 128) if n % d == 0]
    return opts or [n]


def _mm_tiles(M, N, K, sa, sb, so, whole_rows=False):
    best, best_cost = None, None
    for tm in _tile_options(M, 512 if whole_rows else 2048):
        for tn in ([N] if whole_rows else _tile_options(N, 2048)):
            for tk in _tile_options(K, 4096):
                vmem = 2 * (tm * tk * sa + tk * tn * sb) + 2 * tm * tn * so + (tm * tn * 4 if tk < K else 0)
                if vmem > MM_VMEM_BUDGET:
                    continue
                steps = (M // tm) * (N // tn) * (K // tk)
                traffic = M * K * sa * (N // tn) + K * N * sb * (M // tm) + M * N * so
                cost = (max(2.0 * M * N * K / MM_FLOPS_PER_US, traffic / MM_HBM_BYTES_PER_US) + steps * MM_STEP_US
                        + (tm * tk * sa + tk * tn * sb) / MM_HBM_BYTES_PER_US)
                if best_cost is None or cost < best_cost:
                    best, best_cost = (tm, tn, tk), cost
    return best


def _mm(name, a, b, mode, out_dtype=f32, scale=None, gather=None, cross=None, epilogue=None):
    halves = a.ndim == 3
    sizes = (jnp.dtype(a.dtype).itemsize, jnp.dtype(b.dtype).itemsize, jnp.dtype(out_dtype).itemsize)
    whole = epilogue is not None
    if mode == "nn":
        (M, K), N = (a.shape[1], 2 * a.shape[2]) if halves else a.shape, b.shape[1]
        tm, tn, tk = _mm_tiles(M, N, K // 2 if halves else K, *sizes, whole_rows=whole)
    elif mode == "nt":
        (M, K), N = a.shape, b.shape[0]
        tm, tn, tk = _mm_tiles(M, N, K, *sizes, whole_rows=whole)
    else:
        (K, M), N = (a.shape[1], 2 * a.shape[2]) if halves else a.shape, b.shape[1]
        tm, tn, tk = _mm_tiles(M // 2 if halves else M, N, K, *sizes, whole_rows=whole)
    nk = K // tk
    if mode == "nn":
        per = K // 2 // tk
        a_spec = (pl.BlockSpec((None, tm, tk), lambda i, j, k: (k // per, i, k % per)) if halves
                  else pl.BlockSpec((tm, tk), lambda i, j, k: (i, k)))
        b_spec = pl.BlockSpec((tk, tn), lambda i, j, k: (k, j))
        dims = (((1,), (0,)), ((), ()))
    elif mode == "nt":
        a_spec = pl.BlockSpec((tm, tk), lambda i, j, k: (i, k))
        b_spec = pl.BlockSpec((tn, tk), lambda i, j, k: (j, k))
        dims = (((1,), (1,)), ((), ()))
    else:
        per = M // 2 // tm
        a_spec = (pl.BlockSpec((None, tk, tm), lambda i, j, k: (i // per, k, i % per)) if halves
                  else pl.BlockSpec((tk, tm), lambda i, j, k: (k, i)))
        b_spec = pl.BlockSpec((tk, tn), lambda i, j, k: (k, j))
        dims = (((0,), (0,)), ((), ()))

    hosted = gather if gather is not None else cross
    grid = (M // tm, N // tn, nk)
    steps = grid[0] * grid[1] * grid[2]
    ep_rows, ep_params, ep_outs, ep_accs = ([], [], [], []) if epilogue is None else epilogue[1:]
    n_ep_in, n_ep_out = len(ep_rows) + len(ep_params), len(ep_outs) + len(ep_accs)
    assert epilogue is None or tn == N

    def body(a_ref, b_ref, *rest):
        rest = list(rest)
        src_ref = rest.pop(0) if hosted is not None else None
        ep_in, rest = rest[:n_ep_in], rest[n_ep_in:]
        if epilogue is None:
            o_ref = rest.pop(0)
        else:
            out_refs, rest = rest[:n_ep_out], rest[n_ep_out:]
        dst_ref = rest.pop(0) if hosted is not None else None
        scratch = rest
        step = (pl.program_id(0) * grid[1] + pl.program_id(1)) * grid[2] + pl.program_id(2)
        if hosted is not None:
            n_sem = len(GATHER_SEMS if gather is not None else CROSS_SEMS)
            sems, scratch = scratch[len(scratch) - n_sem:], scratch[:len(scratch) - n_sem]
            if gather is not None:
                start, forward, done = _gather_phases(src_ref, dst_ref, *sems)
                pl.when(step == steps // 2)(forward)
            else:
                start, done = _cross_phases(src_ref, dst_ref, *sems)
            pl.when(step == 0)(start)
        part = lax.dot_general(a_ref[...].astype(bf16), b_ref[...].astype(bf16), dims,
                               preferred_element_type=f32)

        def finish(acc):
            if epilogue is None:
                o_ref[...] = (acc if scale is None else acc * scale).astype(o_ref.dtype)
                return
            res = epilogue[0](acc, *[r[...] for r in ep_in])
            for j in range(len(ep_outs)):
                out_refs[j][...] = res[j].astype(out_refs[j].dtype)
            for j in range(len(ep_accs)):
                acc_out = out_refs[len(ep_outs) + j]

                @pl.when(step == nk - 1)
                def _(acc_out=acc_out):
                    acc_out[...] = jnp.zeros(acc_out.shape, f32)
                acc_out[...] += res[len(ep_outs) + j]

        if nk == 1:
            finish(part)
        else:
            acc_ref = scratch[0]
            k = pl.program_id(2)

            @pl.when(k == 0)
            def _():
                acc_ref[...] = part

            @pl.when(k > 0)
            def _():
                acc_ref[...] += part

            @pl.when(k == nk - 1)
            def _():
                finish(acc_ref[...])
        if hosted is not None:
            pl.when(step == steps - 1)(done)

    hbm = pl.BlockSpec(memory_space=pl.ANY)
    in_specs = [a_spec, b_spec] + [hbm] * (hosted is not None)
    in_specs += [pl.BlockSpec((tm, r.shape[1]), lambda i, j, k: (i, 0)) for r in ep_rows]
    in_specs += [pl.BlockSpec(p.shape, lambda i, j, k: (0, 0)) for p in ep_params]
    if epilogue is None:
        out_specs = [pl.BlockSpec((tm, tn), lambda i, j, k: (i, j))]
        out_shape = [jax.ShapeDtypeStruct((M, N), out_dtype)]
    else:
        out_specs = [pl.BlockSpec((tm, w), lambda i, j, k: (i, 0)) for w, _ in ep_outs]
        out_specs += [pl.BlockSpec(s, lambda i, j, k: (0, 0)) for s in ep_accs]
        out_shape = [jax.ShapeDtypeStruct((M, w), dt) for w, dt in ep_outs]
        out_shape += [jax.ShapeDtypeStruct(s, f32) for s in ep_accs]
    scratch_shapes = [] if nk == 1 else [pltpu.VMEM((tm, tn), f32)]
    if gather is not None:
        out_specs.append(hbm)
        out_shape.append(jax.ShapeDtypeStruct((N_DEV,) + gather.shape, gather.dtype))
        scratch_shapes = scratch_shapes + GATHER_SEMS
    elif cross is not None:
        out_specs.append(hbm)
        out_shape.append(jax.ShapeDtypeStruct((3,) + cross.shape[1:], cross.dtype))
        scratch_shapes = scratch_shapes + CROSS_SEMS
    sequential = hosted is not None or ep_accs
    res = pl.pallas_call(
        body, name=name, grid=grid, in_specs=in_specs,
        out_specs=out_specs, out_shape=out_shape, scratch_shapes=scratch_shapes,
        compiler_params=_cparams(("arbitrary",) * 3 if sequential else ("parallel", "parallel", "arbitrary")),
    )(a, b, *([hosted] if hosted is not None else []), *ep_rows, *ep_params)
    return res[0] if (hosted is None and epilogue is None) else res


FFN_TM, FFN_TN = 512, 1408


def _ffn_up(name, n, WiT, gather=None):
    S = n.shape[0]
    grid = (S // FFN_TM, D_FF // FFN_TN)
    steps = grid[0] * grid[1]

    def body(n_ref, wg_ref, wu_ref, *rest):
        if gather is None:
            g_ref, u_ref, act_ref = rest
        else:
            src_ref, g_ref, u_ref, act_ref, dst_ref, *sems = rest
            step = pl.program_id(0) * grid[1] + pl.program_id(1)
            start, forward, done = _gather_phases(src_ref, dst_ref, *sems)
            pl.when(step == 0)(start)
            pl.when(step == steps // 2)(forward)
        x = n_ref[...]
        gate = lax.dot_general(x, wg_ref[...], _NT, preferred_element_type=f32)
        up = lax.dot_general(x, wu_ref[...], _NT, preferred_element_type=f32)
        g_ref[...] = gate
        u_ref[...] = up
        act_ref[...] = (gate * jax.nn.sigmoid(gate) * up).astype(act_ref.dtype)
        if gather is not None:
            pl.when(step == steps - 1)(done)

    hbm = pl.BlockSpec(memory_space=pl.ANY)
    tile = pl.BlockSpec((FFN_TM, FFN_TN), lambda i, j: (i, j))
    in_specs = [pl.BlockSpec((FFN_TM, D), lambda i, j: (i, 0)), pl.BlockSpec((FFN_TN, D), lambda i, j: (j, 0)),
                pl.BlockSpec((FFN_TN, D), lambda i, j: (j + D_FF // FFN_TN, 0))]
    out_specs = [tile, tile, tile]
    out_shape = [jax.ShapeDtypeStruct((S, D_FF), f32), jax.ShapeDtypeStruct((S, D_FF), f32),
                 jax.ShapeDtypeStruct((S, D_FF), bf16)]
    if gather is not None:
        in_specs.append(hbm)
        out_specs.append(hbm)
        out_shape.append(jax.ShapeDtypeStruct((N_DEV,) + gather.shape, gather.dtype))
    return pl.pallas_call(
        body, name=name, grid=grid, in_specs=in_specs, out_specs=out_specs, out_shape=out_shape,
        scratch_shapes=GATHER_SEMS if gather is not None else [],
        compiler_params=_cparams(("arbitrary", "arbitrary")),
    )(n, WiT, WiT, *([gather] if gather is not None else []))


def _ffn_dact(name, dy, Wo, gate, up):
    S = dy.shape[0]

    def body(dy_ref, wo_ref, g_ref, u_ref, d_ref):
        dact = 0.5 * lax.dot_general(dy_ref[...], wo_ref[...], _NT, preferred_element_type=f32)
        gate, up = g_ref[...], u_ref[...]
        sg = jax.nn.sigmoid(gate)
        d_ref[0] = (dact * up * (sg * (1.0 + gate * (1.0 - sg)))).astype(d_ref.dtype)
        d_ref[1] = (dact * gate * sg).astype(d_ref.dtype)

    tile = pl.BlockSpec((FFN_TM, FFN_TN), lambda i, j: (i, j))
    return pl.pallas_call(
        body, name=name, grid=(S // FFN_TM, D_FF // FFN_TN),
        in_specs=[pl.BlockSpec((FFN_TM, D), lambda i, j: (i, 0)), pl.BlockSpec((FFN_TN, D), lambda i, j: (j, 0)), tile, tile],
        out_specs=pl.BlockSpec((2, FFN_TM, FFN_TN), lambda i, j: (0, i, j)),
        out_shape=jax.ShapeDtypeStruct((2, S, D_FF), bf16),
        compiler_params=_cparams(("parallel", "parallel")),
    )(dy, Wo, gate, up)


def _sp(x):
    hi = x.astype(bf16)
    return hi, (x - hi.astype(f32)).astype(bf16)


def _cat(parts):
    return tuple(jnp.concatenate(p, axis=1) for p in zip(*parts))


def _bmm(eq, a, b):
    (ah, al), (bh, bl) = a, b
    dot = functools.partial(jnp.einsum, eq, preferred_element_type=f32)
    return dot(ah, bh) + (dot(ah, bl) + dot(al, bh))


def _tri_dot(eq, tri, x):
    h1 = x.astype(bf16)
    r1 = x - h1.astype(f32)
    h2 = r1.astype(bf16)
    h3 = (r1 - h2.astype(f32)).astype(bf16)
    dot = functools.partial(jnp.einsum, eq, preferred_element_type=f32)
    return dot(tri, h1) + (dot(tri, h2) + dot(tri, h3))


def _tri_masks(T):
    ti = lax.broadcasted_iota(jnp.int32, (T, T), 0)
    si = lax.broadcasted_iota(jnp.int32, (T, T), 1)
    return ti >= si, ti > si


def _wkv_prep(r, lw, k, kkr, a):
    H, T, _ = r.shape
    low_i, low_s = _tri_masks(T)
    nrm = jnp.sqrt(jnp.sum(kkr * kkr, axis=-1, keepdims=True))
    den = jnp.maximum(nrm, 1e-12)
    kk = kkr / den
    tri = jnp.broadcast_to(low_i.astype(bf16)[None], (H, T, T))
    cl = _tri_dot("hts,hsn->htn", tri, lw)
    c = jnp.exp(cl)
    cprev = jnp.exp(cl - lw)
    cinv = jnp.exp(-cl)
    bt, kt = _sp(kk * a * cinv), _sp(k * cinv)
    L = _cat([_sp(r * c), _sp(-kk * cprev)])
    Mb = _bmm("htn,hsn->hts", L, bt)
    Mk = _bmm("htn,hsn->hts", L, kt)
    A_rb = jnp.where(low_i[None], Mb[:, :T], 0.0)
    A_ab = jnp.where(low_s[None], Mb[:, T:], 0.0)
    Mk = jnp.concatenate([jnp.where(low_i[None], Mk[:, :T], 0.0), jnp.where(low_s[None], Mk[:, T:], 0.0)], axis=1)
    return dict(kk=kk, den=den, nrm=nrm, c=c, cprev=cprev, cinv=cinv, L=L, kt=kt, bt=bt,
                A_ab=A_ab, A_rb=A_rb, Mk=Mk, cT=c[:, T - 1:T, :])


def _tri_inverse(A):
    T = A.shape[-1]
    eye = (lax.broadcasted_iota(jnp.int32, (T, T), 0) == lax.broadcasted_iota(jnp.int32, (T, T), 1)).astype(f32)
    inv = eye[None] + A
    X = A
    n = 1
    while 2 * n < T:
        Xs = _sp(X)
        X = _bmm("hts,hsu->htu", Xs, Xs)
        inv = inv + _bmm("hts,hsu->htu", _sp(inv), _sp(X))
        n *= 2
    return inv


def _wkv_chunk_fwd(S0, r, lw, k, v, kkr, a):
    T = r.shape[1]
    q = _wkv_prep(r, lw, k, kkr, a)
    inv = _tri_inverse(q["A_ab"])
    vs = _sp(v)
    P = _bmm("htk,hvk->htv", q["L"], _sp(S0)) + _bmm("hts,hsv->htv", _sp(q["Mk"]), vs)
    U = _bmm("hts,hsv->htv", _sp(inv), _sp(P[:, T:]))
    Us = _sp(U)
    Y = P[:, :T] + _bmm("hts,hsv->htv", _sp(q["A_rb"]), Us)
    S1 = (S0 + _bmm("htv,htk->hvk", _cat([Us, vs]), _cat([q["bt"], q["kt"]]))) * q["cT"]
    return Y, U, inv, S1


def _wkv_chunk_bwd(S0, Hin, Q, r, lw, k, v, kkr, a, U, inv, dY):
    H, T, _ = r.shape
    low_i, low_s = _tri_masks(T)
    q = _wkv_prep(r, lw, k, kkr, a)
    L, kt, bt = q["L"], q["kt"], q["bt"]
    R = _cat([bt, kt])
    Hh = Hin * q["cT"]
    Hs, S0s, dYs, vs, Us = _sp(Hh), _sp(S0), _sp(dY), _sp(v), _sp(U)
    RH = _bmm("htk,hvk->htv", R, Hs)
    Z = _bmm("hst,hsv->htv", _sp(inv), _sp(RH[:, :T] + _bmm("hst,hsv->htv", _sp(q["A_rb"]), dYs)))
    DZ = _cat([dYs, _sp(Z)])
    both = jnp.concatenate([jnp.broadcast_to(low_i[None], (1, T, T)), jnp.broadcast_to(low_s[None], (1, T, T))], axis=1)
    NU = _sp(jnp.where(both, _bmm("htv,hsv->hts", DZ, Us), 0.0))
    NV = _sp(jnp.where(both, _bmm("htv,hsv->hts", DZ, vs), 0.0))
    ra = _bmm("htv,hvk->htk", DZ, S0s) + _bmm("hts,hsk->htk", NU, bt) + _bmm("hts,hsk->htk", NV, kt)
    dr = ra[:, :T] * q["c"]
    da = ra[:, T:] * q["cprev"]
    dv = RH[:, T:] + _bmm("hst,hsv->htv", _sp(q["Mk"]), DZ)
    VH = _bmm("htv,hvk->htk", _cat([vs, Us]), Hs)
    dk = (VH[:, :T] + _bmm("hst,hsk->htk", NV, L)) * q["cinv"]
    db = (VH[:, T:] + _bmm("hst,hsk->htk", NU, L)) * q["cinv"]
    H0 = Hh + _bmm("htv,htk->hvk", DZ, L)
    kk = q["kk"]
    e = r * dr - kk * a * db - k * dk
    f = -kk * da
    tri_i = jnp.broadcast_to(low_i.astype(bf16)[None], (H, T, T))
    tri_s = jnp.broadcast_to(low_s.astype(bf16)[None], (H, T, T))
    dlw = _tri_dot("hst,hsn->htn", tri_i, e) + _tri_dot("hst,hsn->htn", tri_s, f) + Q
    Qn = Q + jnp.sum(e + f, axis=1, keepdims=True)
    dkk = db * a - da
    dasig = db * kk
    proj = jnp.sum(dkk * kk, axis=-1, keepdims=True)
    dkkr = jnp.where(q["nrm"] > 1e-12, dkk - kk * proj, dkk) / q["den"]
    return dr, dlw, dk, dv, dkkr, dasig, H0, Qn


def _heads(ref, rows=slice(None)):
    return jnp.stack([ref[rows, h * HEAD:(h + 1) * HEAD] for h in range(RW_HEADS)], axis=0)


def _put_heads(ref, val, rows=slice(None)):
    for h in range(RW_HEADS):
        ref[rows, h * HEAD:(h + 1) * HEAD] = val[h]


def _wkv_fwd(r, lw, k, v, kkr, a, g, r_k, ln_w, ln_b, late_pack=None):
    S = r.shape[0]
    H, N, T = RW_HEADS, HEAD, WKV_T
    TS = T * WKV_SUB
    nc = S // TS
    hosting = late_pack is not None

    def body(r_ref, lw_ref, k_ref, v_ref, kkr_ref, a_ref, g_ref, rk_ref, lnw_ref, lnb_ref, *rest):
        if hosting:
            pack_ref, y_ref, yg_ref, wkv_ref, u_ref, inv_ref, s0_ref, gathered_ref, state, *sems = rest
            start, forward, finish = _gather_phases(pack_ref, gathered_ref, *sems)
            pl.when(pl.program_id(0) == 0)(start)
            pl.when(pl.program_id(0) == nc // 2)(forward)
        else:
            y_ref, yg_ref, wkv_ref, u_ref, inv_ref, s0_ref, state = rest

        @pl.when(pl.program_id(0) == 0)
        def _():
            state[...] = jnp.zeros(state.shape, f32)

        S0 = state[...]
        for c in range(WKV_SUB):
            rows = slice(c * T, (c + 1) * T)
            s0_ref[c] = S0
            rr, kk2, vv = _heads(r_ref, rows), _heads(k_ref, rows), _heads(v_ref, rows)
            Y, U, inv, S0 = _wkv_chunk_fwd(S0, rr, _heads(lw_ref, rows), kk2, vv, _heads(kkr_ref, rows),
                                           _heads(a_ref, rows))
            wkv_ref[:, rows, :] = Y
            u_ref[:, rows, :] = U
            inv_ref[:, rows, :] = inv
            mean = jnp.mean(Y, axis=-1, keepdims=True)
            var = jnp.mean(jnp.square(Y - mean), axis=-1, keepdims=True)
            yn = (Y - mean) * lax.rsqrt(var + GN_EPS)
            bonus = jnp.sum(rr * kk2 * rk_ref[...], axis=-1, keepdims=True) * vv
            _put_heads(y_ref, yn * lnw_ref[...] + lnb_ref[...] + bonus, rows)
        state[...] = S0
        yg_ref[...] = (y_ref[...] * g_ref[...]).astype(yg_ref.dtype)
        if hosting:
            pl.when(pl.program_id(0) == nc - 1)(finish)

    tok = pl.BlockSpec((TS, H * N), lambda i: (i, 0))
    blk = pl.BlockSpec((H, TS, N), lambda i: (0, i, 0))
    par = pl.BlockSpec((H, 1, N), lambda i: (0, 0, 0))
    hbm = pl.BlockSpec(memory_space=pl.ANY)
    seq = jax.ShapeDtypeStruct((H, S, N), f32)
    out_specs = [tok, tok, blk, blk, pl.BlockSpec((H, TS, T), lambda i: (0, i, 0)),
                 pl.BlockSpec((WKV_SUB, H, N, N), lambda i: (i, 0, 0, 0))]
    out_shape = [jax.ShapeDtypeStruct((S, H * N), f32), jax.ShapeDtypeStruct((S, H * N), bf16), seq, seq,
                 jax.ShapeDtypeStruct((H, S, T), f32),
                 jax.ShapeDtypeStruct((S // T, H, N, N), f32)]
    if hosting:
        out_specs.append(hbm)
        out_shape.append(jax.ShapeDtypeStruct((N_DEV,) + late_pack.shape, late_pack.dtype))
    return pl.pallas_call(
        body, name="wkv_fwd", grid=(nc,), in_specs=[tok] * 7 + [par] * 3 + [hbm] * hosting,
        out_specs=out_specs, out_shape=out_shape,
        scratch_shapes=[pltpu.VMEM((H, N, N), f32)] + (GATHER_SEMS if hosting else []),
        compiler_params=_cparams(("arbitrary",)),
    )(r, lw, k, v, kkr, a, g, r_k, ln_w, ln_b, *([late_pack] if hosting else []))


def _wkv_bwd(dy, g, r, lw, k, v, kkr, a, wkv, U, inv, S0s, r_k, ln_w, ln_b, late_pieces=None):
    S = r.shape[0]
    H, N, T = RW_HEADS, HEAD, WKV_T
    TS = T * WKV_SUB
    nc = S // TS
    hosting = late_pieces is not None
    n_late = len(late_pieces) if hosting else 0

    def body(dy_ref, g_ref, r_ref, lw_ref, k_ref, v_ref, kkr_ref, a_ref, wkv_ref, u_ref, inv_ref, s0_ref,
             rk_ref, lnw_ref, lnb_ref, *rest):
        if hosting:
            piece_refs, rest = rest[:n_late], rest[n_late:]
            (dr_ref, dlw_ref, dk_ref, dv_ref, dkkr_ref, da_ref, drk_ref, dlnw_ref, dlnb_ref, recv_ref,
             hst, qst, *sems) = rest
            start, finish = _direct_phases(piece_refs, [p.shape[1] for p in late_pieces], recv_ref, *sems)
            pl.when(pl.program_id(0) == 0)(start)
        else:
            dr_ref, dlw_ref, dk_ref, dv_ref, dkkr_ref, da_ref, drk_ref, dlnw_ref, dlnb_ref, hst, qst = rest

        @pl.when(pl.program_id(0) == 0)
        def _():
            hst[...] = jnp.zeros(hst.shape, f32)
            qst[...] = jnp.zeros(qst.shape, f32)
            drk_ref[...] = jnp.zeros(drk_ref.shape, f32)
            dlnw_ref[...] = jnp.zeros(dlnw_ref.shape, f32)
            dlnb_ref[...] = jnp.zeros(dlnb_ref.shape, f32)

        dyg = dy_ref[...] * g_ref[...]
        rk = rk_ref[...]
        Hst, Qst = hst[...], qst[...]
        for c in reversed(range(WKV_SUB)):
            rows = slice(c * T, (c + 1) * T)
            dya = _heads(dyg, rows)
            rr, kk2, vv, Y = _heads(r_ref, rows), _heads(k_ref, rows), _heads(v_ref, rows), wkv_ref[:, rows, :]
            s = jnp.sum(rr * kk2 * rk, axis=-1, keepdims=True)
            ds = jnp.sum(dya * vv, axis=-1, keepdims=True)
            mean = jnp.mean(Y, axis=-1, keepdims=True)
            var = jnp.mean(jnp.square(Y - mean), axis=-1, keepdims=True)
            rstd = lax.rsqrt(var + GN_EPS)
            yn = (Y - mean) * rstd
            dyn = dya * lnw_ref[...]
            dY = rstd * (dyn - jnp.mean(dyn, axis=-1, keepdims=True) - yn * jnp.mean(dyn * yn, axis=-1, keepdims=True))
            drk_ref[...] += jnp.sum(ds * rr * kk2, axis=1, keepdims=True)
            dlnw_ref[...] += jnp.sum(dya * yn, axis=1, keepdims=True)
            dlnb_ref[...] += jnp.sum(dya, axis=1, keepdims=True)
            dr, dlw, dk, dv, dkkr, dasig, Hst, Qst = _wkv_chunk_bwd(
                s0_ref[c], Hst, Qst, rr, _heads(lw_ref, rows), kk2, vv, _heads(kkr_ref, rows), _heads(a_ref, rows),
                u_ref[:, rows, :], inv_ref[:, rows, :], dY)
            _put_heads(dr_ref, dr + ds * kk2 * rk, rows)
            _put_heads(dlw_ref, dlw, rows)
            _put_heads(dk_ref, dk + ds * rr * rk, rows)
            _put_heads(dv_ref, dv + dya * s, rows)
            _put_heads(dkkr_ref, dkkr, rows)
            _put_heads(da_ref, dasig, rows)
        hst[...] = Hst
        qst[...] = Qst
        if hosting:
            pl.when(pl.program_id(0) == nc - 1)(finish)

    tok = pl.BlockSpec((TS, H * N), lambda i: (nc - 1 - i, 0))
    blk = pl.BlockSpec((H, TS, N), lambda i: (0, nc - 1 - i, 0))
    par = pl.BlockSpec((H, 1, N), lambda i: (0, 0, 0))
    hbm = pl.BlockSpec(memory_space=pl.ANY)
    seq = jax.ShapeDtypeStruct((S, H * N), f32)
    pout = jax.ShapeDtypeStruct((H, 1, N), f32)
    out_specs, out_shape = [tok] * 6 + [par] * 3, [seq] * 6 + [pout] * 3
    sems = []
    if hosting:
        rows_late = sum(p.shape[1] for p in late_pieces)
        out_specs.append(hbm)
        out_shape.append(jax.ShapeDtypeStruct((N_DEV - 1, rows_late, late_pieces[0].shape[2]), late_pieces[0].dtype))
        sems = [pltpu.SemaphoreType.DMA((n_late * (N_DEV - 1),))] * 2
    return pl.pallas_call(
        body, name="wkv_bwd", grid=(nc,),
        in_specs=([tok] * 8 + [blk] * 2 + [pl.BlockSpec((H, TS, T), lambda i: (0, nc - 1 - i, 0))]
                  + [pl.BlockSpec((WKV_SUB, H, N, N), lambda i: (nc - 1 - i, 0, 0, 0))]
                  + [par] * 3 + [hbm] * n_late),
        out_specs=out_specs, out_shape=out_shape,
        scratch_shapes=[pltpu.VMEM((H, N, N), f32), pltpu.VMEM((H, 1, N), f32)] + sems,
        compiler_params=_cparams(("arbitrary",)),
    )(dy, g, r, lw, k, v, kkr, a, wkv, U, inv, S0s, r_k, ln_w, ln_b, *(late_pieces if hosting else []))


ATTN_TT = 2048


def _attn_rows(d, i, j):
    return pl.ds(ATTN_BLK * d * i + j, ATTN_BLK, stride=d) if d > 1 else pl.ds(ATTN_BLK * i, ATTN_BLK)


def _attn_take(ref, d, nsub):
    return jnp.stack([ref[_attn_rows(d, i, j), :] for i in range(nsub) for j in range(d)], axis=0)


def _attn_put(ref, val, d):
    for i in range(val.shape[0] // d):
        for j in range(d):
            ref[_attn_rows(d, i, j), :] = val[i * d + j]


def _attn_prev(cur, before, d):
    return before if cur.shape[0] == d else jnp.concatenate([before, cur[:cur.shape[0] - d]], axis=0)


def _attn_specs(gi, d, nt, reverse):
    per_tile = ATTN_TT // (ATTN_BLK * d)

    def tile(n):
        return nt - 1 - n if reverse else n

    def col(kind):
        return lambda hp, n: (tile(n), kind * (ATTN_W // 128) + 2 * gi + hp)

    def col_before(kind):
        return lambda hp, n: (jnp.maximum(tile(n) * per_tile - 1, 0), kind * (ATTN_W // 128) + 2 * gi + hp)

    cur = [pl.BlockSpec((ATTN_TT, 128), col(kind)) for kind in range(3)]
    before = [pl.BlockSpec((ATTN_BLK * d, 128), col_before(kind)) for kind in (1, 2)]
    own = pl.BlockSpec((ATTN_TT, 128), lambda hp, n: (tile(n), hp))
    return cur, before, own, tile


def _attn_norm(x, gain, scale):
    rs = lax.rsqrt(jnp.mean(x * x, axis=-1, keepdims=True) + RMS_EPS)
    return x * rs * (gain * scale), rs


def _attn_scores(qn, kn_c, kn_p, first):
    s_c = jnp.einsum("gqe,gke->gqk", qn.astype(bf16), kn_c.astype(bf16), preferred_element_type=f32)
    s_p = jnp.einsum("gqe,gke->gqk", qn.astype(bf16), kn_p.astype(bf16), preferred_element_type=f32)
    qi = lax.broadcasted_iota(jnp.int32, (1, ATTN_BLK, ATTN_BLK), 1)
    ki = lax.broadcasted_iota(jnp.int32, (1, ATTN_BLK, ATTN_BLK), 2)
    s_c = jnp.where(qi >= ki, s_c, NEG_INF)
    s_p = jnp.where(jnp.logical_and(ki >= qi, jnp.logical_not(first)), s_p, NEG_INF)
    return s_c, s_p


def _attn_fwd(pqkv, qg, kg, gi, S):
    d = ATTN_PAIRS[gi][1]
    nt = S // ATTN_TT
    nsub = ATTN_TT // (ATTN_BLK * d)
    nd = nsub * d

    def body(q_ref, k_ref, v_ref, kb_ref, vb_ref, qg_ref, kg_ref, o_ref, lse_ref):
        Q, K, V = _attn_take(q_ref, d, nsub), _attn_take(k_ref, d, nsub), _attn_take(v_ref, d, nsub)
        KB, VB = _attn_take(kb_ref, d, 1), _attn_take(vb_ref, d, 1)
        first = jnp.logical_and(lax.broadcasted_iota(jnp.int32, (nd, 1, 1), 0) < d, pl.program_id(1) == 0)
        outs, lses = [], []
        for h in range(2):
            sl = slice(h * HEAD, (h + 1) * HEAD)
            kc, vc = K[:, :, sl], V[:, :, sl]
            kp, vp = _attn_prev(kc, KB[:, :, sl], d), _attn_prev(vc, VB[:, :, sl], d)
            qn, _ = _attn_norm(Q[:, :, sl], qg_ref[...], HEAD ** -0.5)
            kn_c, _ = _attn_norm(kc, kg_ref[...], 1.0)
            kn_p, _ = _attn_norm(kp, kg_ref[...], 1.0)
            s_c, s_p = _attn_scores(qn, kn_c, kn_p, first)
            m = jnp.maximum(jnp.max(s_c, axis=-1, keepdims=True), jnp.max(s_p, axis=-1, keepdims=True))
            p_c = jnp.exp(s_c - m)
            p_p = jnp.exp(s_p - m)
            den = jnp.sum(p_c, axis=-1, keepdims=True) + jnp.sum(p_p, axis=-1, keepdims=True)
            inv = 1.0 / den
            o = jnp.einsum("gqk,gke->gqe", (p_c * inv).astype(bf16), vc.astype(bf16), preferred_element_type=f32)
            o += jnp.einsum("gqk,gke->gqe", (p_p * inv).astype(bf16), vp.astype(bf16), preferred_element_type=f32)
            outs.append(o)
            lses.append(jnp.broadcast_to(m + jnp.log(den), o.shape))
        _attn_put(o_ref, jnp.concatenate(outs, axis=-1), d)
        _attn_put(lse_ref, jnp.concatenate(lses, axis=-1), d)

    cur, before, own, _ = _attn_specs(gi, d, nt, False)
    par = pl.BlockSpec((1, HEAD), lambda hp, n: (0, 0))
    shp = jax.ShapeDtypeStruct((S, 2 * 128), f32)
    return pl.pallas_call(
        body, name=f"attn_fwd{gi}", grid=(2, nt), in_specs=cur + before + [par] * 2, out_specs=[own, own],
        out_shape=[shp, shp], compiler_params=_cparams(("arbitrary", "arbitrary")),
    )(pqkv, pqkv, pqkv, pqkv, pqkv, qg, kg)


def _attn_bwd(pqkv, o, lse, do, dlse, qg, kg, gi, S):
    d = ATTN_PAIRS[gi][1]
    nt = S // ATTN_TT
    nsub = ATTN_TT // (ATTN_BLK * d)
    nd = nsub * d

    def norm_bwd(dxn, x, rs, gain, scale):
        xh = x * rs
        dxh = dxn * (gain * scale)
        dx = rs * (dxh - xh * jnp.mean(dxh * xh, axis=-1, keepdims=True))
        dgain = jnp.sum(jnp.sum(dxn * xh * scale, axis=1), axis=0, keepdims=True)
        return dx, dgain

    def to_before(part, carried):
        return carried if nsub == 1 else jnp.concatenate([part[d:], carried], axis=0)

    def body(q_ref, k_ref, v_ref, kb_ref, vb_ref, o_ref, lse_ref, do_ref, dlse_ref, qg_ref, kg_ref,
             dq_ref, dk_ref, dv_ref, dqg_ref, dkg_ref, carry_k, carry_v):
        step = pl.program_id(1)

        @pl.when(jnp.logical_and(pl.program_id(0) == 0, step == 0))
        def _():
            dqg_ref[...] = jnp.zeros(dqg_ref.shape, f32)
            dkg_ref[...] = jnp.zeros(dkg_ref.shape, f32)

        @pl.when(step == 0)
        def _():
            carry_k[...] = jnp.zeros(carry_k.shape, f32)
            carry_v[...] = jnp.zeros(carry_v.shape, f32)

        Q, K, V = _attn_take(q_ref, d, nsub), _attn_take(k_ref, d, nsub), _attn_take(v_ref, d, nsub)
        KB, VB = _attn_take(kb_ref, d, 1), _attn_take(vb_ref, d, 1)
        O, LSE = _attn_take(o_ref, d, nsub), _attn_take(lse_ref, d, nsub)
        DO, DLSE = _attn_take(do_ref, d, nsub), _attn_take(dlse_ref, d, nsub)
        first = jnp.logical_and(lax.broadcasted_iota(jnp.int32, (nd, 1, 1), 0) < d, step == nt - 1)
        qg, kg = qg_ref[...], kg_ref[...]
        dqs, dks, dvs = [], [], []
        for h in range(2):
            sl = slice(h * HEAD, (h + 1) * HEAD)
            qx, kx, vc = Q[:, :, sl], K[:, :, sl], V[:, :, sl]
            kpx, vp = _attn_prev(kx, KB[:, :, sl], d), _attn_prev(vc, VB[:, :, sl], d)
            qn, rq = _attn_norm(qx, qg, HEAD ** -0.5)
            kn_c, rk_c = _attn_norm(kx, kg, 1.0)
            kn_p, _ = _attn_norm(kpx, kg, 1.0)
            s_c, s_p = _attn_scores(qn, kn_c, kn_p, first)
            lse = LSE[:, :, h * HEAD:h * HEAD + 1]
            p_c = jnp.exp(s_c - lse)
            p_p = jnp.exp(s_p - lse)
            dO = DO[:, :, sl]
            dOb = dO.astype(bf16)
            dp_c = jnp.einsum("gqe,gke->gqk", dOb, vc.astype(bf16), preferred_element_type=f32)
            dp_p = jnp.einsum("gqe,gke->gqk", dOb, vp.astype(bf16), preferred_element_type=f32)
            corr = DLSE[:, :, h * HEAD:h * HEAD + 1] - jnp.sum(dO * O[:, :, sl], axis=-1, keepdims=True)
            ds_c = (p_c * (dp_c + corr)).astype(bf16)
            ds_p = (p_p * (dp_p + corr)).astype(bf16)
            qnb = qn.astype(bf16)
            dqn = (jnp.einsum("gqk,gke->gqe", ds_c, kn_c.astype(bf16), preferred_element_type=f32)
                   + jnp.einsum("gqk,gke->gqe", ds_p, kn_p.astype(bf16), preferred_element_type=f32))
            dkn_p = jnp.einsum("gqk,gqe->gke", ds_p, qnb, preferred_element_type=f32)
            dv_p = jnp.einsum("gqk,gqe->gke", p_p.astype(bf16), dOb, preferred_element_type=f32)
            dkn = jnp.einsum("gqk,gqe->gke", ds_c, qnb, preferred_element_type=f32) + to_before(dkn_p, carry_k[h])
            dv = (jnp.einsum("gqk,gqe->gke", p_c.astype(bf16), dOb, preferred_element_type=f32)
                  + to_before(dv_p, carry_v[h]))
            carry_k[h] = dkn_p[:d]
            carry_v[h] = dv_p[:d]
            dq, dqg = norm_bwd(dqn, qx, rq, qg, HEAD ** -0.5)
            dk, dkg = norm_bwd(dkn, kx, rk_c, kg, 1.0)
            dqg_ref[...] += dqg
            dkg_ref[...] += dkg
            dqs.append(dq)
            dks.append(dk)
            dvs.append(dv)
        _attn_put(dq_ref, jnp.concatenate(dqs, axis=-1), d)
        _attn_put(dk_ref, jnp.concatenate(dks, axis=-1), d)
        _attn_put(dv_ref, jnp.concatenate(dvs, axis=-1), d)

    cur, before, own, _ = _attn_specs(gi, d, nt, True)
    par = pl.BlockSpec((1, HEAD), lambda hp, n: (0, 0))
    shp = jax.ShapeDtypeStruct((S, 2 * 128), f32)
    pshp = jax.ShapeDtypeStruct((1, HEAD), f32)
    return pl.pallas_call(
        body, name=f"attn_bwd{gi}", grid=(2, nt), in_specs=cur + before + [own] * 4 + [par] * 2,
        out_specs=[own] * 3 + [par] * 2, out_shape=[shp] * 3 + [pshp] * 2,
        scratch_shapes=[pltpu.VMEM((2, d, ATTN_BLK, HEAD), f32)] * 2,
        compiler_params=_cparams(("arbitrary", "arbitrary")),
    )(pqkv, pqkv, pqkv, pqkv, pqkv, o, lse, do, dlse, qg, kg)


def _rms(x, g):
    rs = lax.rsqrt(jnp.mean(x * x, axis=-1, keepdims=True) + RMS_EPS)
    return x * rs * g


def _f_rms(x, g):
    return _rms(x, g)


def _f_resid_rms(coef, x, f, g):
    xn = x + coef * f
    return xn, _rms(xn, g)


def _f_rms_bwd(n_parts, *args):
    dns = args[:n_parts]
    x, dres, g = args[n_parts:]
    dn = dns[0]
    for t in dns[1:]:
        dn = dn + t
    rs = lax.rsqrt(jnp.mean(x * x, axis=-1, keepdims=True) + RMS_EPS)
    xh = x * rs
    dxh = dn * g
    dx = dres + rs * (dxh - xh * jnp.mean(dxh * xh, axis=-1, keepdims=True))
    return dx, dx, jnp.sum(dn * xh, axis=0, keepdims=True)


def _f_loss(x, f, tgt):
    y = x + 0.5 * f
    diff = y - tgt
    part = 0.5 * jnp.sum(jnp.mean(diff * diff, axis=-1, keepdims=True), axis=0, keepdims=True)
    dy = diff * (1.0 / D)
    return dy, dy, jnp.broadcast_to(part, (1, 128))


def _dotb(a, b, dims):
    return lax.dot_general(a.astype(bf16), b.astype(bf16), dims, preferred_element_type=f32)


_NN = (((1,), (0,)), ((), ()))
_NT = (((1,), (1,)), ((), ()))
_TN = (((0,), (0,)), ((), ()))


def _rwkv_pre_core(prkv, prkv_prev, plora, plora_prev, mu_rkv, mu_lora, w0, w2p, a0, a2p, g2p, k_k, k_a):
    xs = prkv + (prkv_prev - prkv) * mu_rkv
    xl = plora + (plora_prev - plora) * mu_lora
    r, k, v = xs[:, :D], xs[:, D:2 * D], xs[:, 2 * D:]
    wd, ad, gd = xl[:, :128], xl[:, 128:256], xl[:, 256:]
    tw = jnp.tanh(wd)
    zw = w0 + _dotb(tw, w2p, _NN)
    sp = jnp.maximum(-zw, 0.0) + jnp.log(1.0 + jnp.exp(-jnp.abs(zw)))
    lw = -jnp.exp(-sp - 0.5)
    a = jax.nn.sigmoid(a0 + _dotb(ad, a2p, _NN))
    sg = jax.nn.sigmoid(gd)
    return dict(r=r, k=k, v=v, tw=tw, zw=zw, lw=lw, a=a, sg=sg, ad=ad)


def _rows_down(x, halo, blk):
    before = jnp.where(blk > 0, halo[HALO - 1:HALO, :], 0.0)
    row = lax.broadcasted_iota(jnp.int32, (x.shape[0], 1), 0)
    return jnp.where(row == 0, before, pltpu.roll(x, 1, 0))


def _rows_up(x, after):
    n = x.shape[0]
    row = lax.broadcasted_iota(jnp.int32, (n, 1), 0)
    return jnp.where(row == n - 1, after, pltpu.roll(x, n - 1, 0))


def _f_rwkv_pre(prkv, plora, mu_rkv, mu_lora, w0, w2p, a0, a2p, g2p, k_k, k_a, halo_rkv, halo_lora, blk):
    c = _rwkv_pre_core(prkv, _rows_down(prkv, halo_rkv, blk), plora, _rows_down(plora, halo_lora, blk),
                       mu_rkv, mu_lora, w0, w2p, a0, a2p, g2p, k_k, k_a)
    g = _dotb(c["sg"], g2p, _NN)
    k, a = c["k"], c["a"]
    return c["r"], c["lw"], k * (1.0 + (a - 1.0) * k_a), c["v"], k * k_k, a, g


def _f_rwkv_pre_bwd(prkv, plora, dr, dlw, dk2, dv, dkkr, da, dya, yap,
                    mu_rkv, mu_lora, w0, w2p, a0, a2p, g2p, k_k, k_a, halo_rkv, halo_lora, next_rkv, next_lora, blk):
    prkv_prev, plora_prev = _rows_down(prkv, halo_rkv, blk), _rows_down(plora, halo_lora, blk)
    c = _rwkv_pre_core(prkv, prkv_prev, plora, plora_prev, mu_rkv, mu_lora, w0, w2p, a0, a2p, g2p, k_k, k_a)
    k, a, sg, tw, zw, lw = c["k"], c["a"], c["sg"], c["tw"], c["zw"], c["lw"]
    dg = dya * yap
    dsg = _dotb(dg, g2p, _NT)
    dgd = dsg * sg * (1.0 - sg)
    dg2p = _dotb(sg, dg, _TN)
    dk = dk2 * (1.0 + (a - 1.0) * k_a) + dkkr * k_k
    da_t = da + dk2 * k * k_a
    dk_a = jnp.sum(dk2 * k * (a - 1.0), axis=0, keepdims=True)
    dk_k = jnp.sum(dkkr * k, axis=0, keepdims=True)
    dza = da_t * a * (1.0 - a)
    da0 = jnp.sum(dza, axis=0, keepdims=True)
    dad = _dotb(dza, a2p, _NT)
    da2p = _dotb(c["ad"], dza, _TN)
    dzw = dlw * lw * jax.nn.sigmoid(-zw)
    dw0 = jnp.sum(dzw, axis=0, keepdims=True)
    dtw = _dotb(dzw, w2p, _NT)
    dw2p = _dotb(tw, dzw, _TN)
    dwd = dtw * (1.0 - tw * tw)
    dxs = jnp.concatenate([dr, dk, dv], axis=1)
    dxl = jnp.concatenate([dwd, dad, dgd], axis=1)
    dmu_rkv = jnp.sum(dxs * (prkv_prev - prkv), axis=0, keepdims=True)
    dmu_lora = jnp.sum(dxl * (plora_prev - plora), axis=0, keepdims=True)
    to_next_rkv, to_next_lora = dxs * mu_rkv, dxl * mu_lora
    return (dxs * (1.0 - mu_rkv) + _rows_up(to_next_rkv, next_rkv), dxl * (1.0 - mu_lora) + _rows_up(to_next_lora, next_lora),
            dmu_rkv, dmu_lora, dw0, da0, dk_k, dk_a, dw2p, da2p, dg2p, to_next_rkv[0:1], to_next_lora[0:1])


def _group_alpha(l0, l1, l2):
    m = jnp.maximum(jnp.maximum(l0, l1), l2)
    e0, e1, e2 = jnp.exp(l0 - m), jnp.exp(l1 - m), jnp.exp(l2 - m)
    inv = 1.0 / (e0 + e1 + e2)
    return jnp.concatenate([e0 * inv, e1 * inv, e2 * inv], axis=1)


def _f_combine(o0, o1, o2, l0, l1, l2):
    return jnp.concatenate([o0, o1, o2], axis=1) * _group_alpha(l0, l1, l2)


def _f_combine_bwd(dyb, o0, o1, o2, l0, l1, l2, bd):
    alpha = _group_alpha(l0, l1, l2)
    hi, lo = _sp(dyb * jnp.concatenate([o0, o1, o2], axis=1))
    ones = bd.astype(bf16)
    e = jnp.dot(hi, ones, preferred_element_type=f32) + jnp.dot(lo, ones, preferred_element_type=f32)
    ae = alpha * e
    tot = ae[:, :256] + ae[:, 256:512] + ae[:, 512:]
    do = dyb * alpha
    dl = ae - alpha * jnp.concatenate([tot, tot, tot], axis=1)
    return do[:, :256], do[:, 256:512], do[:, 512:], dl[:, :256], dl[:, 256:512], dl[:, 512:]


def _f_merge(pgate, ta, tb, b_gate):
    gate = jax.nn.sigmoid(pgate + b_gate)
    return gate[:, :D] * ta + gate[:, D:] * tb


def _f_merge_bwd(dm, pgate, ta, tb, b_gate):
    gate = jax.nn.sigmoid(pgate + b_gate)
    ga, gb = gate[:, :D], gate[:, D:]
    dpg = jnp.concatenate([dm * ta * ga * (1.0 - ga), dm * tb * gb * (1.0 - gb)], axis=1)
    return dm * ga, dm * gb, dpg, jnp.sum(dpg, axis=0, keepdims=True)


def _f_adamw(w, g, m, v):
    m2 = ADAM_B1 * m + (1.0 - ADAM_B1) * g
    v2 = ADAM_B2 * v + (1.0 - ADAM_B2) * jnp.square(g)
    m_hat = m2 / (1.0 - ADAM_B1 ** ADAM_STEP)
    v_hat = v2 / (1.0 - ADAM_B2 ** ADAM_STEP)
    delta = -ADAM_LR * (m_hat / (jnp.sqrt(v_hat) + ADAM_EPS) + ADAM_WD * w)
    return delta, m2, v2


def _ffn_bwd(tag, dxo, dxo_b, x_in, n, gate, up, act, g, WiT, Wo, cross=None):
    du = _ffn_dact(f"{tag}_dact", dxo_b, Wo, gate, up)
    dWo = _mm(f"{tag}_dwo", act, dxo_b, "tn", out_dtype=GRAD_WIRE, scale=0.5)
    drms = (functools.partial(_f_rms_bwd, 1), [x_in, dxo], [g], [(D, f32), (D, bf16)], [(1, D)])
    dx, dx_b, dg, *recv = _mm(f"{tag}_dn", du, WiT, "nn", cross=cross, epilogue=drms)
    dWiT = _mm(f"{tag}_dwi", du, n, "tn", out_dtype=GRAD_WIRE)
    return dx, dx_b, dg, dWiT, dWo, (recv[0] if recv else None)


def _local_step(x0, tgt, W, P, hooks=None):
    S = x0.shape[0]
    (n1,) = _rowwise("f1_rms", _f_rms, [x0], [P["ffn1_norm"]], [(D, bf16)])
    hooks = hooks or {}
    if "gather_mid" in hooks:
        pack, weights = hooks["gather_mid"]
        gate1, up1, act1, gathered = _ffn_up("f1_up", n1, W["f1_iT"], gather=pack)
        W = {**W, **weights(gathered)}
    else:
        gate1, up1, act1 = _ffn_up("f1_up", n1, W["f1_iT"])
    mix_rms = (lambda f, x, g: _f_resid_rms(0.5, x, f, g), [x0], [P["mix_norm"]], [(D, f32), (D, bf16)], [])
    if "gather_in" in hooks:
        pack, weights = hooks["gather_in"]
        x1, h, gathered = _mm("f1_down", act1, W["f1_o"], "nn", gather=pack, epilogue=mix_rms)
        W = {**W, **weights(gathered)}
    else:
        x1, h = _mm("f1_down", act1, W["f1_o"], "nn", epilogue=mix_rms)
    prkv = _mm("p_rkv", h, W["in_rkvT"], "nt")
    plora = _mm("p_lora", h, W["in_loraT"], "nt")
    pqkv = _mm("p_qkv", h, W["in_qkvT"], "nt")
    pgate = _mm("p_gate", h, W["in_gateT"], "nt")
    pre_params = [P["mu_rkv"], P["mu_lora"], P["w0"], W["w2p"], P["a0"], W["a2p"], W["g2p"], P["k_k"], P["k_a"]]
    r, lw, k2, v, kkr, a, g = _rowwise("rwkv_pre", _f_rwkv_pre, [prkv, plora], pre_params, [(D, f32)] * 7, tm=128,
                                       halos=(0, 1))
    hm = [r, lw, k2, v, kkr, a]
    hp = [P["r_k"].reshape(RW_HEADS, 1, HEAD), P["ln_w"].reshape(RW_HEADS, 1, HEAD), P["ln_b"].reshape(RW_HEADS, 1, HEAD)]
    if "gather_late" in hooks:
        pack, weights = hooks["gather_late"]
        yap, ya, wkv_h, U_h, inv_h, S0s, gathered = _wkv_fwd(*hm, g, *hp, late_pack=pack)
        W = {**W, **weights(gathered)}
    else:
        yap, ya, wkv_h, U_h, inv_h, S0s = _wkv_fwd(*hm, g, *hp)
    ta = _mm("proj_a", ya, W["pr"], "nn")
    n_grp = len(ATTN_PAIRS)
    attn = [_attn_fwd(pqkv, P["q_norm"], P["k_norm"], gi, S) for gi in range(n_grp)]
    o_g, lse_g = [t[0] for t in attn], [t[1] for t in attn]
    (yb,) = _rowwise("attn_combine", _f_combine, [*o_g, *lse_g], [], [(ATTN_W, bf16)])
    merge = (lambda tb, pg, ta, bg: (tb, _f_merge(pg, ta, tb, bg)), [pgate, ta], [P["b_gate"]], [(D, f32), (D, bf16)], [])
    tb, merged = _mm("proj_b", yb, W["paT"], "nt", epilogue=merge)
    f2_rms = (lambda f, x, g: _f_resid_rms(1.0, x, f, g), [x1], [P["ffn2_norm"]], [(D, f32), (D, bf16)], [])
    x2, n2 = _mm("mix_out", merged, W["out"], "nn", epilogue=f2_rms)
    gate2, up2, act2 = _ffn_up("f2_up", n2, W["f2_iT"])
    loss_head = (lambda f, x, t: _f_loss(x, f, t), [x2, tgt], [], [(D, f32), (D, bf16)], [(1, 128)])
    dx3, dx3_b, loss = _mm("f2_down", act2, W["f2_o"], "nn", epilogue=loss_head)
    G, Gs = {}, {}
    dx2, dx2_b, Gs["ffn2_norm"], G["f2_iT"], G["f2_o"], _ = _ffn_bwd("f2", dx3, dx3_b, x2, n2, gate2, up2, act2,
                                                                    P["ffn2_norm"], W["f2_iT"], W["f2_o"])
    merge_bwd = (_f_merge_bwd, [pgate, ta, tb], [P["b_gate"]], [(D, bf16), (D, bf16), (2 * D, bf16)], [(1, 2 * D)])
    dta, dtb, dpgate, Gs["b_gate"] = _mm("d_merged", dx2_b, W["out"], "nt", epilogue=merge_bwd)
    G["out"] = _mm("dw_out", merged, dx2_b, "tn", out_dtype=GRAD_WIRE)
    dya = _mm("d_ya", dta, W["pr"], "nt")
    G["pr"] = _mm("dw_pr", ya, dta, "tn", out_dtype=GRAD_WIRE)
    dyb = _mm("d_yb", dtb, W["paT"], "nn")
    G["paT"] = _mm("dw_pa", dtb, yb, "tn", out_dtype=GRAD_WIRE)
    if "reduce_late" in hooks:
        pieces_late = hooks["reduce_late"](G)
        hg = _wkv_bwd(dya, g, *hm, wkv_h, U_h, inv_h, S0s, *hp, late_pieces=pieces_late)
        G["late"] = (pieces_late, hg[9])
    else:
        hg = _wkv_bwd(dya, g, *hm, wkv_h, U_h, inv_h, S0s, *hp)
    dr, dlw, dk2, dv, dkkr, da = hg[:6]
    Gs["r_k"], Gs["ln_w"], Gs["ln_b"] = (t.reshape(1, D) for t in hg[6:9])
    lp = sum(LORA_PAD)
    (dprkv, dplora, Gs["mu_rkv"], Gs["mu_lora"], Gs["w0"], Gs["a0"], Gs["k_k"], Gs["k_a"],
     dw2p, da2p, dg2p) = _rowwise(
        "rwkv_pre_bwd", _f_rwkv_pre_bwd,
        [prkv, plora, dr, dlw, dk2, dv, dkkr, da, dya, yap], pre_params,
        [(3 * D, bf16), (lp, bf16)],
        [(1, 3 * D), (1, lp), (1, D), (1, D), (1, D), (1, D), (LORA_PAD[0], D), (LORA_PAD[1], D), (LORA_PAD[2], D)],
        tm=128, halos=(0, 1), carries=((1, 3 * D), (1, lp)), reverse=True)
    G["w2T"], G["a2T"], G["g2T"] = dw2p[:LORA_W[0]].T, da2p[:LORA_W[1]].T, dg2p[:LORA_W[2]].T
    bd = (jnp.arange(ATTN_W)[:, None] // HEAD == jnp.arange(ATTN_W)[None, :] // HEAD).astype(f32)
    dol = _rowwise("attn_combine_bwd", _f_combine_bwd, [dyb, *o_g, *lse_g], [bd], [(ATTN_W // n_grp, f32)] * (2 * n_grp))
    dattn = [_attn_bwd(pqkv, o_g[gi], lse_g[gi], dol[gi], dol[n_grp + gi], P["q_norm"], P["k_norm"], gi, S)
             for gi in range(n_grp)]
    Gs["q_norm"] = dattn[0][3] + dattn[1][3] + dattn[2][3]
    Gs["k_norm"] = dattn[0][4] + dattn[1][4] + dattn[2][4]
    dpqkv = jnp.concatenate([dattn[gi][kind] for kind in range(3) for gi in range(n_grp)], axis=1).astype(bf16)
    dh = [_mm("dh_rkv", dprkv, W["in_rkvT"], "nn"), _mm("dh_lora", dplora, W["in_loraT"], "nn"),
          _mm("dh_gate", dpgate, W["in_gateT"], "nn")]
    mix_drms = (functools.partial(_f_rms_bwd, 4), [*dh, x1, dx2], [P["mix_norm"]], [(D, f32), (D, bf16)], [(1, D)])
    dx1, dx1_b, Gs["mix_norm"] = _mm("dh_qkv", dpqkv, W["in_qkvT"], "nn", epilogue=mix_drms)
    dW_rkv = _mm("dw_rkv", dprkv, h, "tn", out_dtype=GRAD_WIRE)
    dW_lora = _mm("dw_lora", dplora, h, "tn", out_dtype=GRAD_WIRE)
    dW_qkv = _mm("dw_qkv", dpqkv, h, "tn", out_dtype=GRAD_WIRE)
    dW_gate = _mm("dw_gate", dpgate, h, "tn", out_dtype=GRAD_WIRE)
    o1, o2 = LORA_PAD[0], LORA_PAD[0] + LORA_PAD[1]
    G["inT"] = jnp.concatenate([dW_rkv, dW_lora[:LORA_W[0]], dW_lora[o1:o1 + LORA_W[1]], dW_lora[o2:o2 + LORA_W[2]],
                                dW_qkv, dW_gate], axis=0)
    part_mid = hooks["reduce_mid"](G) if "reduce_mid" in hooks else None
    dx0, _, Gs["ffn1_norm"], G["f1_iT"], G["f1_o"], recv_mid = _ffn_bwd(
        "f1", dx1, dx1_b, x0, n1, gate1, up1, act1, P["ffn1_norm"], W["f1_iT"], W["f1_o"], cross=part_mid)
    G["mid"] = (part_mid, recv_mid)
    return loss[0, 0], dx0, G, Gs


def _peer(k):
    x, y, c = lax.axis_index("x"), lax.axis_index("y"), lax.axis_index("c")
    px = 1 - x if k & 4 else x
    py = 1 - y if k & 2 else y
    pc = 1 - c if k & 1 else c
    return (px, py, pc), 4 * px + 2 * py + pc


def _gather_phases(x_ref, out_ref, send_sems, recv_sems, local_sem):
    x, y, c = lax.axis_index("x"), lax.axis_index("y"), lax.axis_index("c")
    me, sibling = (x, y, c), (x, y, 1 - c)
    chips = [(1 - x, y), (x, 1 - y), (1 - x, 1 - y)]

    def slot(px, py, pc):
        return out_ref.at[4 * px + 2 * py + pc]

    def copy(k, block, to, src=None):
        return pltpu.make_async_remote_copy(
            src_ref=slot(*block) if src is None else src, dst_ref=slot(*block), send_sem=send_sems.at[k],
            recv_sem=recv_sems.at[k], device_id=to, device_id_type=MESH)

    def mine():
        return pltpu.make_async_copy(x_ref, slot(*me), local_sem)

    def first():
        return [copy(0, me, sibling, src=x_ref)] + [copy(1 + j, me, (*chip, c), src=x_ref) for j, chip in enumerate(chips)]

    def passed():
        return [copy(4 + j, (*chip, c), sibling) for j, chip in enumerate(chips)]

    def start():
        mine().start()
        for cp in first():
            cp.start()

    def forward():
        for j, (chip, cp) in enumerate(zip(chips, passed())):
            copy(1 + j, (*chip, c), me).wait_recv()
            cp.start()

    def finish():
        copy(0, sibling, me).wait_recv()
        for j, chip in enumerate(chips):
            copy(4 + j, (*chip, 1 - c), me).wait_recv()
        for cp in first() + passed():
            cp.wait_send()
        mine().wait()

    return start, forward, finish


GATHER_SEMS = [pltpu.SemaphoreType.DMA((N_DEV - 1,)), pltpu.SemaphoreType.DMA((N_DEV - 1,)), pltpu.SemaphoreType.DMA(())]


def _all_gather(pack):
    R, C = pack.shape

    def body(x_ref, out_ref, send_sems, recv_sems, local_sem):
        for phase in _gather_phases(x_ref, out_ref, send_sems, recv_sems, local_sem):
            phase()

    return pl.pallas_call(
        body, name="weight_all_gather", out_shape=jax.ShapeDtypeStruct((N_DEV, R, C), pack.dtype),
        in_specs=[pl.BlockSpec(memory_space=pl.ANY)], out_specs=pl.BlockSpec(memory_space=pl.ANY),
        scratch_shapes=GATHER_SEMS,
    )(pack)


def _cross_phases(p_ref, out_ref, send_sems, recv_sems):
    x, y, c = lax.axis_index("x"), lax.axis_index("y"), lax.axis_index("c")

    def copies():
        out = []
        for j, (fx, fy) in enumerate([(1, 0), (0, 1), (1, 1)]):
            px = 1 - x if fx else x
            py = 1 - y if fy else y
            out.append(pltpu.make_async_remote_copy(src_ref=p_ref.at[2 * px + py], dst_ref=out_ref.at[j],
                                                    send_sem=send_sems.at[j], recv_sem=recv_sems.at[j],
                                                    device_id=(px, py, c), device_id_type=MESH))
        return out

    def start():
        for cp in copies():
            cp.start()

    def finish():
        for cp in copies():
            cp.wait()

    return start, finish


CROSS_SEMS = [pltpu.SemaphoreType.DMA((3,)), pltpu.SemaphoreType.DMA((3,))]


def _direct_phases(piece_refs, rows, out_ref, send_sems, recv_sems):
    offs = [sum(rows[:i]) for i in range(len(rows))]

    def copies():
        out = []
        for i, g_ref in enumerate(piece_refs):
            for k in range(1, N_DEV):
                dev, idx = _peer(k)
                out.append(pltpu.make_async_remote_copy(
                    src_ref=g_ref.at[idx], dst_ref=out_ref.at[k - 1, pl.ds(offs[i], rows[i])],
                    send_sem=send_sems.at[i * (N_DEV - 1) + k - 1], recv_sem=recv_sems.at[i * (N_DEV - 1) + k - 1],
                    device_id=dev, device_id_type=MESH))
        return out

    def start():
        for cp in copies():
            cp.start()

    def finish():
        for cp in copies():
            cp.wait()

    return start, finish


def _sum_direct(pieces, recv, me, tag):
    n = len(pieces)
    C = pieces[0].shape[2]
    nblk = [p.shape[1] // PACK_BLOCK for p in pieces]
    lo = [sum(nblk[:i]) for i in range(n)]
    R = sum(nblk) * PACK_BLOCK

    def body(me_ref, *refs):
        g_refs, r_ref, o_ref = refs[:n], refs[n], refs[n + 1]
        rb = pl.program_id(0)
        for i in range(n):
            @pl.when(jnp.logical_and(rb >= lo[i], rb < lo[i] + nblk[i]))
            def _(g_ref=g_refs[i]):
                acc = g_ref[...].astype(f32)
                for k in range(N_DEV - 1):
                    acc = acc + r_ref[k].astype(f32)
                o_ref[...] = acc

    def piece_spec(i):
        return pl.BlockSpec((None, PACK_BLOCK, C), lambda rb, me_ref: (me_ref[0], jnp.clip(rb - lo[i], 0, nblk[i] - 1), 0))

    return pl.pallas_call(
        body, name=f"grad_sum_{tag}",
        grid_spec=pltpu.PrefetchScalarGridSpec(
            num_scalar_prefetch=1, grid=(R // PACK_BLOCK,),
            in_specs=[piece_spec(i) for i in range(n)] + [pl.BlockSpec((N_DEV - 1, PACK_BLOCK, C), lambda rb, me_ref: (0, rb, 0))],
            out_specs=pl.BlockSpec((PACK_BLOCK, C), lambda rb, me_ref: (rb, 0))),
        out_shape=jax.ShapeDtypeStruct((R, C), f32),
        compiler_params=_cparams(("arbitrary",)),
    )(me, *pieces, recv)


N_CHIP = 4


def _grad_pair(pieces, tag):
    n = len(pieces)
    C = pieces[0].shape[2]
    rows = [p.shape[1] for p in pieces]
    offs = [sum(rows[:i]) for i in range(n)]
    R = sum(rows)

    def body(*refs):
        g_refs, (other_ref, send_sems, recv_sems) = refs[:n], refs[n:]
        x, y, c = lax.axis_index("x"), lax.axis_index("y"), lax.axis_index("c")
        copies = []
        for i, g_ref in enumerate(g_refs):
            for k in range(N_CHIP):
                cp = pltpu.make_async_remote_copy(
                    src_ref=g_ref.at[4 * (k // 2) + 2 * (k % 2) + 1 - c], dst_ref=other_ref.at[k, pl.ds(offs[i], rows[i])],
                    send_sem=send_sems.at[i * N_CHIP + k], recv_sem=recv_sems.at[i * N_CHIP + k],
                    device_id=(x, y, 1 - c), device_id_type=MESH)
                cp.start()
                copies.append(cp)
        for cp in copies:
            cp.wait()

    return pl.pallas_call(
        body, name=f"grad_pair_{tag}", out_shape=jax.ShapeDtypeStruct((N_CHIP, R, C), pieces[0].dtype),
        in_specs=[pl.BlockSpec(memory_space=pl.ANY)] * n, out_specs=pl.BlockSpec(memory_space=pl.ANY),
        scratch_shapes=[pltpu.SemaphoreType.DMA((n * N_CHIP,))] * 2,
    )(*pieces)


def _pair_add(pieces, other, c, tag):
    n = len(pieces)
    C = pieces[0].shape[2]
    nblk = [p.shape[1] // PACK_BLOCK for p in pieces]
    lo = [sum(nblk[:i]) for i in range(n)]
    R = sum(nblk) * PACK_BLOCK

    def body(c_ref, *refs):
        g_refs, o_ref, out_ref = refs[:n], refs[n], refs[n + 1]
        rb = pl.program_id(1)
        for i in range(n):
            @pl.when(jnp.logical_and(rb >= lo[i], rb < lo[i] + nblk[i]))
            def _(g_ref=g_refs[i]):
                out_ref[...] = (g_ref[...].astype(f32) + o_ref[...].astype(f32)).astype(out_ref.dtype)

    def piece_spec(i):
        return pl.BlockSpec((1, None, PACK_BLOCK, C),
                            lambda k, rb, c_ref: (k, c_ref[0], jnp.clip(rb - lo[i], 0, nblk[i] - 1), 0))

    blk = pl.BlockSpec((1, PACK_BLOCK, C), lambda k, rb, c_ref: (k, rb, 0))
    return pl.pallas_call(
        body, name=f"pair_add_{tag}",
        grid_spec=pltpu.PrefetchScalarGridSpec(
            num_scalar_prefetch=1, grid=(N_CHIP, R // PACK_BLOCK),
            in_specs=[piece_spec(i) for i in range(n)] + [blk], out_specs=blk),
        out_shape=jax.ShapeDtypeStruct((N_CHIP, R, C), other.dtype),
        compiler_params=_cparams(("arbitrary", "arbitrary")),
    )(c, *[p.reshape(N_CHIP, 2, p.shape[1], C) for p in pieces], other)


def _grad_cross(part):
    _, R, C = part.shape

    def body(p_ref, out_ref, send_sems, recv_sems):
        for phase in _cross_phases(p_ref, out_ref, send_sems, recv_sems):
            phase()

    return pl.pallas_call(
        body, name="grad_cross", out_shape=jax.ShapeDtypeStruct((3, R, C), part.dtype),
        in_specs=[pl.BlockSpec(memory_space=pl.ANY)], out_specs=pl.BlockSpec(memory_space=pl.ANY),
        scratch_shapes=CROSS_SEMS,
    )(part)


def _grad_sum(part, recv, my_chip, tr, tag):
    _, R, C = part.shape

    def body(chip_ref, p_ref, r_ref, o_ref):
        acc = p_ref[0].astype(f32)
        for j in range(3):
            acc = acc + r_ref[j].astype(f32)
        o_ref[...] = acc

    return pl.pallas_call(
        body, name=f"grad_sum_{tag}",
        grid_spec=pltpu.PrefetchScalarGridSpec(
            num_scalar_prefetch=1, grid=(R // tr,),
            in_specs=[pl.BlockSpec((1, tr, C), lambda i, chip_ref: (chip_ref[0], i, 0)),
                      pl.BlockSpec((3, tr, C), lambda i, chip_ref: (0, i, 0))],
            out_specs=pl.BlockSpec((tr, C), lambda i, chip_ref: (i, 0))),
        out_shape=jax.ShapeDtypeStruct((R, C), f32),
        compiler_params=_cparams(("arbitrary",)),
    )(my_chip, part, recv)


def _small_all_reduce(small):
    R, C = small.shape

    def body(x_ref, o_ref, buf, send_sems, recv_sems):
        _, me = _peer(0)
        buf[me] = x_ref[...]
        sends = []
        for k in range(1, N_DEV):
            dev, _ = _peer(k)
            cp = pltpu.make_async_remote_copy(src_ref=x_ref, dst_ref=buf.at[me], send_sem=send_sems.at[k - 1],
                                              recv_sem=recv_sems.at[k - 1], device_id=dev, device_id_type=MESH)
            cp.start()
            sends.append(cp)
        for k in range(1, N_DEV):
            dev, idx = _peer(k)
            pltpu.make_async_remote_copy(src_ref=x_ref, dst_ref=buf.at[idx], send_sem=send_sems.at[k - 1],
                                         recv_sem=recv_sems.at[k - 1], device_id=dev, device_id_type=MESH).wait_recv()
        for cp in sends:
            cp.wait_send()
        acc = buf[0]
        for i in range(1, N_DEV):
            acc = acc + buf[i]
        o_ref[...] = acc

    return pl.pallas_call(
        body, name="small_all_reduce", out_shape=jax.ShapeDtypeStruct((R, C), f32),
        in_specs=[pl.BlockSpec(memory_space=pltpu.VMEM)], out_specs=pl.BlockSpec(memory_space=pltpu.VMEM),
        scratch_shapes=[pltpu.VMEM((N_DEV, R, C), f32), pltpu.SemaphoreType.DMA((N_DEV - 1,)),
                        pltpu.SemaphoreType.DMA((N_DEV - 1,))],
    )(small)


_LORA = (("rwkv_w2", True), ("rwkv_a2", True), ("rwkv_g2", True))
_GROUPS_FIRST = ((("ffn1_w_in", True),),)
_GROUPS_MID = ((("ffn1_w_out", False),), _LORA)
_GROUPS_IN = ((("w_in", True),),)
_GROUPS_LATE = ((("w_proj_rwkv", False),), (("w_proj_attn", True),), (("w_out", False),),
                (("ffn2_w_in", True),), (("ffn2_w_out", False),))
_GRADS_MID = ((("w_in", True),), _LORA)
_GRADS_LAST = ((("ffn1_w_in", True),), (("ffn1_w_out", False),))
_BIG = tuple(item for group in _GROUPS_FIRST + _GROUPS_MID + _GROUPS_IN + _GROUPS_LATE for item in group)
_SMALL = ("ffn1_norm", "mix_norm", "b_gate", "rwkv_mu", "rwkv_w0", "rwkv_a0", "rwkv_k_k", "rwkv_k_a", "rwkv_r_k",
          "rwkv_ln_w", "rwkv_ln_b", "attn_q_norm", "attn_k_norm", "ffn2_norm")


def _pack_layout(like, groups):
    items, spans, off = {}, [], 0
    for group in groups:
        start = off
        for name, _ in group:
            shp = like[name].shape
            n = shp[0] * shp[1] // D
            items[name] = (off, n)
            off += n
        off = -(-off // PACK_BLOCK) * PACK_BLOCK
        spans.append((start, off - start))
    return items, spans, off


def _pack_big(shards, groups):
    items, _, rows = _pack_layout(shards, groups)
    parts, at = [], 0
    for group in groups:
        for name, tr in group:
            off, n = items[name]
            t = shards[name]
            if off > at:
                parts.append(jnp.zeros((off - at, D), t.dtype))
            parts.append((t.T if tr else t).reshape(n, D))
            at = off + n
    if rows > at:
        parts.append(jnp.zeros((rows - at, D), parts[0].dtype))
    return jnp.concatenate(parts, axis=0)


def _unpack_big(pack, like, groups):
    items, _, _ = _pack_layout(like, groups)
    out = {}
    for group in groups:
        for name, tr in group:
            off, n = items[name]
            shp = like[name].shape
            t = pack[off:off + n]
            out[name] = t.reshape(shp[1], shp[0]).T if tr else t.reshape(shp)
    return out


def _unpack_gathered(gathered, like, groups):
    items, _, _ = _pack_layout(like, groups)
    full = {}
    for group in groups:
        for name, tr in group:
            shp = like[name].shape
            off, rows = items[name]
            r_loc, c_loc = (shp[1], shp[0]) if tr else shp
            full[name] = gathered[:, off:off + rows].reshape(N_DEV * r_loc, c_loc)
    return full


def _grad_pieces(g_full, like, groups):
    items, spans, _ = _pack_layout(like, groups)
    pieces = []
    for group, (_, rows_pad) in zip(groups, spans):
        parts = [g_full[n].astype(GRAD_WIRE).reshape(N_DEV, items[n][1], D) for n, _ in group]
        piece = parts[0] if len(parts) == 1 else jnp.concatenate(parts, axis=1)
        if rows_pad > piece.shape[1]:
            piece = jnp.pad(piece, ((0, 0), (0, rows_pad - piece.shape[1]), (0, 0)))
        pieces.append(piece)
    return pieces


def _small_rows(name, t):
    flat = t.reshape(-1)
    pad = (-flat.shape[0]) % D
    return jnp.pad(flat, (0, pad)).reshape(-1, D)


def _pack_small(vals):
    parts = [_small_rows(n, vals[n]) for n in _SMALL]
    used = sum(p.shape[0] for p in parts)
    parts.append(jnp.zeros((SMALL_ROWS - used, D), f32))
    return jnp.concatenate(parts, axis=0)


def _unpack_small(pack, like):
    out, off = {}, 0
    for n in _SMALL:
        size = like[n].size
        rows = -(-size // D)
        out[n] = pack[off:off + rows].reshape(-1)[:size].reshape(like[n].shape)
        off += rows
    return out


def _build_W_mid(full):
    dt = full["rwkv_w2"].dtype
    z64, z96 = jnp.zeros((64, D), dt), jnp.zeros((96, D), dt)
    return {
        "f1_o": full["ffn1_w_out"],
        "w2p": jnp.concatenate([full["rwkv_w2"].T, z64], axis=0),
        "a2p": jnp.concatenate([full["rwkv_a2"].T, z64], axis=0),
        "g2p": jnp.concatenate([full["rwkv_g2"].T, z96], axis=0),
    }


def _build_W_in(full):
    inT = full["w_in"]
    z64, z96 = jnp.zeros((64, D), inT.dtype), jnp.zeros((96, D), inT.dtype)
    return {
        "in_rkvT": inT[:3 * D],
        "in_loraT": jnp.concatenate([inT[3072:3136], z64, inT[3136:3200], z64, inT[3200:3360], z96], axis=0),
        "in_qkvT": inT[3360:3360 + 3 * ATTN_W], "in_gateT": inT[3360 + 3 * ATTN_W:],
    }


def _build_W_late(full):
    return {"pr": full["w_proj_rwkv"], "paT": full["w_proj_attn"], "out": full["w_out"],
            "f2_iT": full["ffn2_w_in"], "f2_o": full["ffn2_w_out"]}


def _build_W_first(full):
    return {"f1_iT": full["ffn1_w_in"]}


def _build_W(full):
    return {**_build_W_first(full), **_build_W_mid(full), **_build_W_in(full), **_build_W_late(full)}


_G_NAMES = {"ffn1_w_in": "f1_iT", "ffn1_w_out": "f1_o", "w_in": "inT", "rwkv_w2": "w2T", "rwkv_a2": "a2T",
            "rwkv_g2": "g2T", "w_proj_rwkv": "pr", "w_proj_attn": "paT", "w_out": "out", "ffn2_w_in": "f2_iT",
            "ffn2_w_out": "f2_o"}


def _named_grads(G, groups):
    return {n: G[_G_NAMES[n]] for group in groups for n, _ in group}


def _reduce_start(G, like, groups, my_c, tag):
    pieces = _grad_pieces(_named_grads(G, groups), like, groups)
    return _pair_add(pieces, _grad_pair(pieces, tag), my_c, tag)


def _build_P(Wl):
    mu = Wl["rwkv_mu"]
    z64f, z96f = jnp.zeros((1, 64), f32), jnp.zeros((1, 96), f32)
    return {
        "ffn1_norm": Wl["ffn1_norm"][None], "mix_norm": Wl["mix_norm"][None], "ffn2_norm": Wl["ffn2_norm"][None],
        "b_gate": Wl["b_gate"][None], "mu_rkv": mu[None, :3 * D],
        "mu_lora": jnp.concatenate([mu[None, 3072:3136], z64f, mu[None, 3136:3200], z64f, mu[None, 3200:3360], z96f], axis=1),
        "w0": Wl["rwkv_w0"][None], "a0": Wl["rwkv_a0"][None], "k_k": Wl["rwkv_k_k"][None], "k_a": Wl["rwkv_k_a"][None],
        "r_k": Wl["rwkv_r_k"].reshape(1, D), "ln_w": Wl["rwkv_ln_w"][None], "ln_b": Wl["rwkv_ln_b"][None],
        "q_norm": Wl["attn_q_norm"][None], "k_norm": Wl["attn_k_norm"][None],
    }


def kernel(x, ffn1_norm, ffn1_w_in, ffn1_w_out, mix_norm, w_in, b_gate, rwkv_mu, rwkv_w0, rwkv_w2, rwkv_a0, rwkv_a2, rwkv_g2, rwkv_k_k, rwkv_k_a, rwkv_r_k, rwkv_ln_w, rwkv_ln_b, attn_q_norm, attn_k_norm, w_proj_rwkv, w_proj_attn, w_out, ffn2_norm, ffn2_w_in, ffn2_w_out, loss_target, m_ffn1_norm, m_ffn1_w_in, m_ffn1_w_out, m_mix_norm, m_w_in, m_b_gate, m_rwkv_mu, m_rwkv_w0, m_rwkv_w2, m_rwkv_a0, m_rwkv_a2, m_rwkv_g2, m_rwkv_k_k, m_rwkv_k_a, m_rwkv_r_k, m_rwkv_ln_w, m_rwkv_ln_b, m_attn_q_norm, m_attn_k_norm, m_w_proj_rwkv, m_w_proj_attn, m_w_out, m_ffn2_norm, m_ffn2_w_in, m_ffn2_w_out, v_ffn1_norm, v_ffn1_w_in, v_ffn1_w_out, v_mix_norm, v_w_in, v_b_gate, v_rwkv_mu, v_rwkv_w0, v_rwkv_w2, v_rwkv_a0, v_rwkv_a2, v_rwkv_g2, v_rwkv_k_k, v_rwkv_k_a, v_rwkv_r_k, v_rwkv_ln_w, v_rwkv_ln_b, v_attn_q_norm, v_attn_k_norm, v_w_proj_rwkv, v_w_proj_attn, v_w_out, v_ffn2_norm, v_ffn2_w_in, v_ffn2_w_out):
    names = ("ffn1_norm", "ffn1_w_in", "ffn1_w_out", "mix_norm", "w_in", "b_gate", "rwkv_mu", "rwkv_w0", "rwkv_w2",
             "rwkv_a0", "rwkv_a2", "rwkv_g2", "rwkv_k_k", "rwkv_k_a", "rwkv_r_k", "rwkv_ln_w", "rwkv_ln_b",
             "attn_q_norm", "attn_k_norm", "w_proj_rwkv", "w_proj_attn", "w_out", "ffn2_norm", "ffn2_w_in", "ffn2_w_out")
    w_all = (ffn1_norm, ffn1_w_in, ffn1_w_out, mix_norm, w_in, b_gate, rwkv_mu, rwkv_w0, rwkv_w2, rwkv_a0, rwkv_a2,
             rwkv_g2, rwkv_k_k, rwkv_k_a, rwkv_r_k, rwkv_ln_w, rwkv_ln_b, attn_q_norm, attn_k_norm, w_proj_rwkv,
             w_proj_attn, w_out, ffn2_norm, ffn2_w_in, ffn2_w_out)
    m_all = (m_ffn1_norm, m_ffn1_w_in, m_ffn1_w_out, m_mix_norm, m_w_in, m_b_gate, m_rwkv_mu, m_rwkv_w0, m_rwkv_w2,
             m_rwkv_a0, m_rwkv_a2, m_rwkv_g2, m_rwkv_k_k, m_rwkv_k_a, m_rwkv_r_k, m_rwkv_ln_w, m_rwkv_ln_b,
             m_attn_q_norm, m_attn_k_norm, m_w_proj_rwkv, m_w_proj_attn, m_w_out, m_ffn2_norm, m_ffn2_w_in, m_ffn2_w_out)
    v_all = (v_ffn1_norm, v_ffn1_w_in, v_ffn1_w_out, v_mix_norm, v_w_in, v_b_gate, v_rwkv_mu, v_rwkv_w0, v_rwkv_w2,
             v_rwkv_a0, v_rwkv_a2, v_rwkv_g2, v_rwkv_k_k, v_rwkv_k_a, v_rwkv_r_k, v_rwkv_ln_w, v_rwkv_ln_b,
             v_attn_q_norm, v_attn_k_norm, v_w_proj_rwkv, v_w_proj_attn, v_w_out, v_ffn2_norm, v_ffn2_w_in, v_ffn2_w_out)
    Wl = {n: t[0] for n, t in zip(names, w_all)}
    Ml = {n: t[0] for n, t in zip(names, m_all)}
    Vl = {n: t[0] for n, t in zip(names, v_all)}
    big = [n for n, _ in _BIG]

    my_c = lax.axis_index("c").astype(jnp.int32).reshape(1)
    my_chip = (2 * lax.axis_index("x") + lax.axis_index("y")).astype(jnp.int32).reshape(1)

    def pack(groups):
        return _pack_big(Wl, groups).astype(bf16)

    gathered = _all_gather(pack(_GROUPS_FIRST))
    W, P = _build_W_first(_unpack_gathered(gathered, Wl, _GROUPS_FIRST)), _build_P(Wl)
    hooks = {"gather_mid": (pack(_GROUPS_MID), lambda g: _build_W_mid(_unpack_gathered(g, Wl, _GROUPS_MID))),
             "gather_in": (pack(_GROUPS_IN), lambda g: _build_W_in(_unpack_gathered(g, Wl, _GROUPS_IN))),
             "gather_late": (pack(_GROUPS_LATE), lambda g: _build_W_late(_unpack_gathered(g, Wl, _GROUPS_LATE))),
             "reduce_mid": lambda G: _reduce_start(G, Wl, _GRADS_MID, my_c, "mid"),
             "reduce_late": lambda G: _grad_pieces(_named_grads(G, _GROUPS_LATE), Wl, _GROUPS_LATE)}

    loss_local, dx0, G, Gs = _local_step(x[0], loss_target[0], W, P, hooks)

    part_last = _reduce_start(G, Wl, _GRADS_LAST, my_c, "last")
    g_big = _unpack_big(_grad_sum(part_last, _grad_cross(part_last), my_chip, 128, "last"), Wl, _GRADS_LAST)
    g_big.update(_unpack_big(_grad_sum(*G["mid"], my_chip, 128, "mid"), Wl, _GRADS_MID))
    me = (4 * lax.axis_index("x") + 2 * lax.axis_index("y") + lax.axis_index("c")).astype(jnp.int32).reshape(1)
    g_big.update(_unpack_big(_sum_direct(*G["late"], me, "late"), Wl, _GROUPS_LATE))

    mu_g = Gs["mu_rkv"], Gs["mu_lora"]
    o1, o2 = LORA_PAD[0], LORA_PAD[0] + LORA_PAD[1]
    g_small_local = {
        "ffn1_norm": Gs["ffn1_norm"], "mix_norm": Gs["mix_norm"], "b_gate": Gs["b_gate"],
        "rwkv_mu": jnp.concatenate([mu_g[0], mu_g[1][:, :64], mu_g[1][:, o1:o1 + 64], mu_g[1][:, o2:o2 + 160]], axis=1),
        "rwkv_w0": Gs["w0"], "rwkv_a0": Gs["a0"], "rwkv_k_k": Gs["k_k"], "rwkv_k_a": Gs["k_a"], "rwkv_r_k": Gs["r_k"],
        "rwkv_ln_w": Gs["ln_w"], "rwkv_ln_b": Gs["ln_b"], "attn_q_norm": Gs["q_norm"], "attn_k_norm": Gs["k_norm"],
        "ffn2_norm": Gs["ffn2_norm"]}
    gs_pack = _small_all_reduce(_pack_small(g_small_local))

    out_g, out_d, out_m, out_v = dict(g_big), {}, {}, {}
    for n in big:
        cols = Wl[n].shape[1]
        out_d[n], out_m[n], out_v[n] = _rowwise(f"adamw_{n}", _f_adamw, [Wl[n], g_big[n], Ml[n], Vl[n]], [],
                                                 [(cols, f32)] * 3)
    ds_pack, ms_pack, vs_pack = _rowwise(
        "adamw_small", _f_adamw, [_pack_small(Wl), gs_pack, _pack_small(Ml), _pack_small(Vl)], [], [(D, f32)] * 3)
    for out, pack in ((out_g, gs_pack), (out_d, ds_pack), (out_m, ms_pack), (out_v, vs_pack)):
        out.update(_unpack_small(pack, Wl))

    loss = lax.psum(loss_local, ("x", "y", "c"))
    return (loss, dx0[None], *[out_g[n][None] for n in names], *[out_d[n][None] for n in names],
            *[out_m[n][None] for n in names], *[out_v[n][None] for n in names])
```

```python
import functools

import jax
import jax.numpy as jnp
from jax import lax
from jax.experimental import pallas as pl
from jax.experimental.pallas import tpu as pltpu

f32 = jnp.float32
bf16 = jnp.bfloat16
MESH = pl.DeviceIdType.MESH

N_DEV = 8
D = 1024
D_FF = 2816
HEAD = 64
RW_HEADS = 16
ATTN_PAIRS = ((128, 1), (512, 4), (2048, 16))
ATTN_BLK = 128
HEADS_PER_GROUP = 4
ATTN_W = 768
LORA_PAD = (128, 128, 256)
LORA_W = (64, 64, 160)
GN_EPS = 64e-5
RMS_EPS = 1e-6
NEG_INF = -1e30
WKV_T = 64
WKV_SUB = 2
GRAD_WIRE = bf16
PACK_BLOCK = 128
SMALL_ROWS = 24
VMEM_LIMIT = 56 * 1024 * 1024

ADAM_LR, ADAM_B1, ADAM_B2, ADAM_EPS, ADAM_WD, ADAM_STEP = 0.001, 0.9, 0.999, 1e-08, 0.01, 10


def _cparams(sem):
    return pltpu.CompilerParams(dimension_semantics=sem, vmem_limit_bytes=VMEM_LIMIT)


def _pick(n, cands):
    for c in cands:
        if n % c == 0:
            return c
    return n


HALO = 8


def _rowwise(name, fn, rows, params, outs, accs=(), tm=256, halos=(), carries=(), reverse=False):
    S = rows[0].shape[0]
    tm = min(tm, S)
    while S % tm:
        tm -= 8
    nb = S // tm
    n_in = len(rows) + len(params) + len(halos)
    n_out = len(outs)
    n_acc = len(accs)
    n_car = len(carries)

    def blk_of(i):
        return nb - 1 - i if reverse else i

    def body(*refs):
        step = pl.program_id(0)
        carry_refs = refs[n_in + n_out + n_acc:]
        if n_car:
            @pl.when(step == 0)
            def _():
                for c_ref in carry_refs:
                    c_ref[...] = jnp.zeros(c_ref.shape, f32)
        args = [r[...] for r in refs[:n_in]] + [c[...] for c in carry_refs]
        res = fn(*args, blk=blk_of(step)) if (halos or carries) else fn(*args)
        if not isinstance(res, (tuple, list)):
            res = (res,)
        out_refs = refs[n_in:n_in + n_out + n_acc]
        for j in range(n_out):
            out_refs[j][...] = res[j].astype(out_refs[j].dtype)
        if n_acc:
            @pl.when(step == 0)
            def _():
                for j in range(n_acc):
                    out_refs[n_out + j][...] = jnp.zeros(out_refs[n_out + j].shape, f32)
            for j in range(n_acc):
                out_refs[n_out + j][...] += res[n_out + j]
        for j in range(n_car):
            carry_refs[j][...] = res[n_out + n_acc + j]

    in_specs = [pl.BlockSpec((tm, a.shape[1]), lambda i: (blk_of(i), 0)) for a in rows]
    in_specs += [pl.BlockSpec(p.shape, lambda i, nd=p.ndim: (0,) * nd) for p in params]
    in_specs += [pl.BlockSpec((HALO, rows[h].shape[1]), lambda i: (jnp.maximum(blk_of(i) * (tm // HALO) - 1, 0), 0))
                 for h in halos]
    out_specs = [pl.BlockSpec((tm, w), lambda i: (blk_of(i), 0)) for w, _ in outs]
    out_specs += [pl.BlockSpec(s, lambda i: (0, 0)) for s in accs]
    out_shape = [jax.ShapeDtypeStruct((S, w), dt) for w, dt in outs]
    out_shape += [jax.ShapeDtypeStruct(s, f32) for s in accs]
    res = pl.pallas_call(
        body, name=name, grid=(nb,), in_specs=in_specs, out_specs=out_specs, out_shape=out_shape,
        scratch_shapes=[pltpu.VMEM(s, f32) for s in carries],
        compiler_params=_cparams(("arbitrary",)),
    )(*rows, *params, *[rows[h] for h in halos])
    return res


MM_VMEM_BUDGET = 40 * 1024 * 1024
MM_STEP_US = 0.35
MM_FLOPS_PER_US = 9.0e8
MM_HBM_BYTES_PER_US = 3.0e6


def _tile_options(n, cap):
    opts = [d for d in range(128, min(n, cap) + 1, 128) if n % d == 0]
    return opts or [n]


def _mm_tiles(M, N, K, sa, sb, so, whole_rows=False):
    best, best_cost = None, None
    for tm in _tile_options(M, 512 if whole_rows else 2048):
        for tn in ([N] if whole_rows else _tile_options(N, 2048)):
            for tk in _tile_options(K, 4096):
                vmem = 2 * (tm * tk * sa + tk * tn * sb) + 2 * tm * tn * so + (tm * tn * 4 if tk < K else 0)
                if vmem > MM_VMEM_BUDGET:
                    continue
                steps = (M // tm) * (N // tn) * (K // tk)
                traffic = M * K * sa * (N // tn) + K * N * sb * (M // tm) + M * N * so
                cost = (max(2.0 * M * N * K / MM_FLOPS_PER_US, traffic / MM_HBM_BYTES_PER_US) + steps * MM_STEP_US
                        + (tm * tk * sa + tk * tn * sb) / MM_HBM_BYTES_PER_US)
                if best_cost is None or cost < best_cost:
                    best, best_cost = (tm, tn, tk), cost
    return best


def _mm(name, a, b, mode, out_dtype=f32, scale=None, gather=None, cross=None, epilogue=None):
    halves = a.ndim == 3
    sizes = (jnp.dtype(a.dtype).itemsize, jnp.dtype(b.dtype).itemsize, jnp.dtype(out_dtype).itemsize)
    whole = epilogue is not None
    if mode == "nn":
        (M, K), N = (a.shape[1], 2 * a.shape[2]) if halves else a.shape, b.shape[1]
        tm, tn, tk = _mm_tiles(M, N, K // 2 if halves else K, *sizes, whole_rows=whole)
    elif mode == "nt":
        (M, K), N = a.shape, b.shape[0]
        tm, tn, tk = _mm_tiles(M, N, K, *sizes, whole_rows=whole)
    else:
        (K, M), N = (a.shape[1], 2 * a.shape[2]) if halves else a.shape, b.shape[1]
        tm, tn, tk = _mm_tiles(M // 2 if halves else M, N, K, *sizes, whole_rows=whole)
    nk = K // tk
    if mode == "nn":
        per = K // 2 // tk
        a_spec = (pl.BlockSpec((None, tm, tk), lambda i, j, k: (k // per, i, k % per)) if halves
                  else pl.BlockSpec((tm, tk), lambda i, j, k: (i, k)))
        b_spec = pl.BlockSpec((tk, tn), lambda i, j, k: (k, j))
        dims = (((1,), (0,)), ((), ()))
    elif mode == "nt":
        a_spec = pl.BlockSpec((tm, tk), lambda i, j, k: (i, k))
        b_spec = pl.BlockSpec((tn, tk), lambda i, j, k: (j, k))
        dims = (((1,), (1,)), ((), ()))
    else:
        per = M // 2 // tm
        a_spec = (pl.BlockSpec((None, tk, tm), lambda i, j, k: (i // per, k, i % per)) if halves
                  else pl.BlockSpec((tk, tm), lambda i, j, k: (k, i)))
        b_spec = pl.BlockSpec((tk, tn), lambda i, j, k: (k, j))
        dims = (((0,), (0,)), ((), ()))

    hosted = gather if gather is not None else cross
    grid = (M // tm, N // tn, nk)
    steps = grid[0] * grid[1] * grid[2]
    ep_rows, ep_params, ep_outs, ep_accs = ([], [], [], []) if epilogue is None else epilogue[1:]
    n_ep_in, n_ep_out = len(ep_rows) + len(ep_params), len(ep_outs) + len(ep_accs)
    assert epilogue is None or tn == N

    def body(a_ref, b_ref, *rest):
        rest = list(rest)
        src_ref = rest.pop(0) if hosted is not None else None
        ep_in, rest = rest[:n_ep_in], rest[n_ep_in:]
        if epilogue is None:
            o_ref = rest.pop(0)
        else:
            out_refs, rest = rest[:n_ep_out], rest[n_ep_out:]
        dst_ref = rest.pop(0) if hosted is not None else None
        scratch = rest
        step = (pl.program_id(0) * grid[1] + pl.program_id(1)) * grid[2] + pl.program_id(2)
        if hosted is not None:
            n_sem = len(GATHER_SEMS if gather is not None else CROSS_SEMS)
            sems, scratch = scratch[len(scratch) - n_sem:], scratch[:len(scratch) - n_sem]
            if gather is not None:
                start, forward, done = _gather_phases(src_ref, dst_ref, *sems)
                pl.when(step == steps // 2)(forward)
            else:
                start, done = _cross_phases(src_ref, dst_ref, *sems)
            pl.when(step == 0)(start)
        part = lax.dot_general(a_ref[...].astype(bf16), b_ref[...].astype(bf16), dims,
                               preferred_element_type=f32)

        def finish(acc):
            if epilogue is None:
                o_ref[...] = (acc if scale is None else acc * scale).astype(o_ref.dtype)
                return
            res = epilogue[0](acc, *[r[...] for r in ep_in])
            for j in range(len(ep_outs)):
                out_refs[j][...] = res[j].astype(out_refs[j].dtype)
            for j in range(len(ep_accs)):
                acc_out = out_refs[len(ep_outs) + j]

                @pl.when(step == nk - 1)
                def _(acc_out=acc_out):
                    acc_out[...] = jnp.zeros(acc_out.shape, f32)
                acc_out[...] += res[len(ep_outs) + j]

        if nk == 1:
            finish(part)
        else:
            acc_ref = scratch[0]
            k = pl.program_id(2)

            @pl.when(k == 0)
            def _():
                acc_ref[...] = part

            @pl.when(k > 0)
            def _():
                acc_ref[...] += part

            @pl.when(k == nk - 1)
            def _():
                finish(acc_ref[...])
        if hosted is not None:
            pl.when(step == steps - 1)(done)

    hbm = pl.BlockSpec(memory_space=pl.ANY)
    in_specs = [a_spec, b_spec] + [hbm] * (hosted is not None)
    in_specs += [pl.BlockSpec((tm, r.shape[1]), lambda i, j, k: (i, 0)) for r in ep_rows]
    in_specs += [pl.BlockSpec(p.shape, lambda i, j, k: (0, 0)) for p in ep_params]
    if epilogue is None:
        out_specs = [pl.BlockSpec((tm, tn), lambda i, j, k: (i, j))]
        out_shape = [jax.ShapeDtypeStruct((M, N), out_dtype)]
    else:
        out_specs = [pl.BlockSpec((tm, w), lambda i, j, k: (i, 0)) for w, _ in ep_outs]
        out_specs += [pl.BlockSpec(s, lambda i, j, k: (0, 0)) for s in ep_accs]
        out_shape = [jax.ShapeDtypeStruct((M, w), dt) for w, dt in ep_outs]
        out_shape += [jax.ShapeDtypeStruct(s, f32) for s in ep_accs]
    scratch_shapes = [] if nk == 1 else [pltpu.VMEM((tm, tn), f32)]
    if gather is not None:
        out_specs.append(hbm)
        out_shape.append(jax.ShapeDtypeStruct((N_DEV,) + gather.shape, gather.dtype))
        scratch_shapes = scratch_shapes + GATHER_SEMS
    elif cross is not None:
        out_specs.append(hbm)
        out_shape.append(jax.ShapeDtypeStruct((3,) + cross.shape[1:], cross.dtype))
        scratch_shapes = scratch_shapes + CROSS_SEMS
    sequential = hosted is not None or ep_accs
    res = pl.pallas_call(
        body, name=name, grid=grid, in_specs=in_specs,
        out_specs=out_specs, out_shape=out_shape, scratch_shapes=scratch_shapes,
        compiler_params=_cparams(("arbitrary",) * 3 if sequential else ("parallel", "parallel", "arbitrary")),
    )(a, b, *([hosted] if hosted is not None else []), *ep_rows, *ep_params)
    return res[0] if (hosted is None and epilogue is None) else res


FFN_TM, FFN_TN = 512, 1408


def _ffn_up(name, n, WiT, gather=None):
    S = n.shape[0]
    grid = (S // FFN_TM, D_FF // FFN_TN)
    steps = grid[0] * grid[1]

    def body(n_ref, wg_ref, wu_ref, *rest):
        if gather is None:
            g_ref, u_ref, act_ref = rest
        else:
            src_ref, g_ref, u_ref, act_ref, dst_ref, *sems = rest
            step = pl.program_id(0) * grid[1] + pl.program_id(1)
            start, forward, done = _gather_phases(src_ref, dst_ref, *sems)
            pl.when(step == 0)(start)
            pl.when(step == steps // 2)(forward)
        x = n_ref[...]
        gate = lax.dot_general(x, wg_ref[...], _NT, preferred_element_type=f32)
        up = lax.dot_general(x, wu_ref[...], _NT, preferred_element_type=f32)
        g_ref[...] = gate
        u_ref[...] = up
        act_ref[...] = (gate * jax.nn.sigmoid(gate) * up).astype(act_ref.dtype)
        if gather is not None:
            pl.when(step == steps - 1)(done)

    hbm = pl.BlockSpec(memory_space=pl.ANY)
    tile = pl.BlockSpec((FFN_TM, FFN_TN), lambda i, j: (i, j))
    in_specs = [pl.BlockSpec((FFN_TM, D), lambda i, j: (i, 0)), pl.BlockSpec((FFN_TN, D), lambda i, j: (j, 0)),
                pl.BlockSpec((FFN_TN, D), lambda i, j: (j + D_FF // FFN_TN, 0))]
    out_specs = [tile, tile, tile]
    out_shape = [jax.ShapeDtypeStruct((S, D_FF), f32), jax.ShapeDtypeStruct((S, D_FF), f32),
                 jax.ShapeDtypeStruct((S, D_FF), bf16)]
    if gather is not None:
        in_specs.append(hbm)
        out_specs.append(hbm)
        out_shape.append(jax.ShapeDtypeStruct((N_DEV,) + gather.shape, gather.dtype))
    return pl.pallas_call(
        body, name=name, grid=grid, in_specs=in_specs, out_specs=out_specs, out_shape=out_shape,
        scratch_shapes=GATHER_SEMS if gather is not None else [],
        compiler_params=_cparams(("arbitrary", "arbitrary")),
    )(n, WiT, WiT, *([gather] if gather is not None else []))


def _ffn_dact(name, dy, Wo, gate, up):
    S = dy.shape[0]

    def body(dy_ref, wo_ref, g_ref, u_ref, d_ref):
        dact = 0.5 * lax.dot_general(dy_ref[...], wo_ref[...], _NT, preferred_element_type=f32)
        gate, up = g_ref[...], u_ref[...]
        sg = jax.nn.sigmoid(gate)
        d_ref[0] = (dact * up * (sg * (1.0 + gate * (1.0 - sg)))).astype(d_ref.dtype)
        d_ref[1] = (dact * gate * sg).astype(d_ref.dtype)

    tile = pl.BlockSpec((FFN_TM, FFN_TN), lambda i, j: (i, j))
    return pl.pallas_call(
        body, name=name, grid=(S // FFN_TM, D_FF // FFN_TN),
        in_specs=[pl.BlockSpec((FFN_TM, D), lambda i, j: (i, 0)), pl.BlockSpec((FFN_TN, D), lambda i, j: (j, 0)), tile, tile],
        out_specs=pl.BlockSpec((2, FFN_TM, FFN_TN), lambda i, j: (0, i, j)),
        out_shape=jax.ShapeDtypeStruct((2, S, D_FF), bf16),
        compiler_params=_cparams(("parallel", "parallel")),
    )(dy, Wo, gate, up)


def _sp(x):
    hi = x.astype(bf16)
    return hi, (x - hi.astype(f32)).astype(bf16)


def _cat(parts):
    return tuple(jnp.concatenate(p, axis=1) for p in zip(*parts))


def _bmm(eq, a, b):
    (ah, al), (bh, bl) = a, b
    dot = functools.partial(jnp.einsum, eq, preferred_element_type=f32)
    return dot(ah, bh) + (dot(ah, bl) + dot(al, bh))


def _tri_dot(eq, tri, x):
    h1 = x.astype(bf16)
    r1 = x - h1.astype(f32)
    h2 = r1.astype(bf16)
    h3 = (r1 - h2.astype(f32)).astype(bf16)
    dot = functools.partial(jnp.einsum, eq, preferred_element_type=f32)
    return dot(tri, h1) + (dot(tri, h2) + dot(tri, h3))


def _tri_masks(T):
    ti = lax.broadcasted_iota(jnp.int32, (T, T), 0)
    si = lax.broadcasted_iota(jnp.int32, (T, T), 1)
    return ti >= si, ti > si


def _wkv_prep(r, lw, k, kkr, a):
    H, T, _ = r.shape
    low_i, low_s = _tri_masks(T)
    nrm = jnp.sqrt(jnp.sum(kkr * kkr, axis=-1, keepdims=True))
    den = jnp.maximum(nrm, 1e-12)
    kk = kkr / den
    tri = jnp.broadcast_to(low_i.astype(bf16)[None], (H, T, T))
    cl = _tri_dot("hts,hsn->htn", tri, lw)
    c = jnp.exp(cl)
    cprev = jnp.exp(cl - lw)
    cinv = jnp.exp(-cl)
    bt, kt = _sp(kk * a * cinv), _sp(k * cinv)
    L = _cat([_sp(r * c), _sp(-kk * cprev)])
    Mb = _bmm("htn,hsn->hts", L, bt)
    Mk = _bmm("htn,hsn->hts", L, kt)
    A_rb = jnp.where(low_i[None], Mb[:, :T], 0.0)
    A_ab = jnp.where(low_s[None], Mb[:, T:], 0.0)
    Mk = jnp.concatenate([jnp.where(low_i[None], Mk[:, :T], 0.0), jnp.where(low_s[None], Mk[:, T:], 0.0)], axis=1)
    return dict(kk=kk, den=den, nrm=nrm, c=c, cprev=cprev, cinv=cinv, L=L, kt=kt, bt=bt,
                A_ab=A_ab, A_rb=A_rb, Mk=Mk, cT=c[:, T - 1:T, :])


def _tri_inverse(A):
    T = A.shape[-1]
    eye = (lax.broadcasted_iota(jnp.int32, (T, T), 0) == lax.broadcasted_iota(jnp.int32, (T, T), 1)).astype(f32)
    inv = eye[None] + A
    X = A
    n = 1
    while 2 * n < T:
        Xs = _sp(X)
        X = _bmm("hts,hsu->htu", Xs, Xs)
        inv = inv + _bmm("hts,hsu->htu", _sp(inv), _sp(X))
        n *= 2
    return inv


def _wkv_chunk_fwd(S0, r, lw, k, v, kkr, a):
    T = r.shape[1]
    q = _wkv_prep(r, lw, k, kkr, a)
    inv = _tri_inverse(q["A_ab"])
    vs = _sp(v)
    P = _bmm("htk,hvk->htv", q["L"], _sp(S0)) + _bmm("hts,hsv->htv", _sp(q["Mk"]), vs)
    U = _bmm("hts,hsv->htv", _sp(inv), _sp(P[:, T:]))
    Us = _sp(U)
    Y = P[:, :T] + _bmm("hts,hsv->htv", _sp(q["A_rb"]), Us)
    S1 = (S0 + _bmm("htv,htk->hvk", _cat([Us, vs]), _cat([q["bt"], q["kt"]]))) * q["cT"]
    return Y, U, inv, S1


def _wkv_chunk_bwd(S0, Hin, Q, r, lw, k, v, kkr, a, U, inv, dY):
    H, T, _ = r.shape
    low_i, low_s = _tri_masks(T)
    q = _wkv_prep(r, lw, k, kkr, a)
    L, kt, bt = q["L"], q["kt"], q["bt"]
    R = _cat([bt, kt])
    Hh = Hin * q["cT"]
    Hs, S0s, dYs, vs, Us = _sp(Hh), _sp(S0), _sp(dY), _sp(v), _sp(U)
    RH = _bmm("htk,hvk->htv", R, Hs)
    Z = _bmm("hst,hsv->htv", _sp(inv), _sp(RH[:, :T] + _bmm("hst,hsv->htv", _sp(q["A_rb"]), dYs)))
    DZ = _cat([dYs, _sp(Z)])
    both = jnp.concatenate([jnp.broadcast_to(low_i[None], (1, T, T)), jnp.broadcast_to(low_s[None], (1, T, T))], axis=1)
    NU = _sp(jnp.where(both, _bmm("htv,hsv->hts", DZ, Us), 0.0))
    NV = _sp(jnp.where(both, _bmm("htv,hsv->hts", DZ, vs), 0.0))
    ra = _bmm("htv,hvk->htk", DZ, S0s) + _bmm("hts,hsk->htk", NU, bt) + _bmm("hts,hsk->htk", NV, kt)
    dr = ra[:, :T] * q["c"]
    da = ra[:, T:] * q["cprev"]
    dv = RH[:, T:] + _bmm("hst,hsv->htv", _sp(q["Mk"]), DZ)
    VH = _bmm("htv,hvk->htk", _cat([vs, Us]), Hs)
    dk = (VH[:, :T] + _bmm("hst,hsk->htk", NV, L)) * q["cinv"]
    db = (VH[:, T:] + _bmm("hst,hsk->htk", NU, L)) * q["cinv"]
    H0 = Hh + _bmm("htv,htk->hvk", DZ, L)
    kk = q["kk"]
    e = r * dr - kk * a * db - k * dk
    f = -kk * da
    tri_i = jnp.broadcast_to(low_i.astype(bf16)[None], (H, T, T))
    tri_s = jnp.broadcast_to(low_s.astype(bf16)[None], (H, T, T))
    dlw = _tri_dot("hst,hsn->htn", tri_i, e) + _tri_dot("hst,hsn->htn", tri_s, f) + Q
    Qn = Q + jnp.sum(e + f, axis=1, keepdims=True)
    dkk = db * a - da
    dasig = db * kk
    proj = jnp.sum(dkk * kk, axis=-1, keepdims=True)
    dkkr = jnp.where(q["nrm"] > 1e-12, dkk - kk * proj, dkk) / q["den"]
    return dr, dlw, dk, dv, dkkr, dasig, H0, Qn


def _heads(ref, rows=slice(None)):
    return jnp.stack([ref[rows, h * HEAD:(h + 1) * HEAD] for h in range(RW_HEADS)], axis=0)


def _put_heads(ref, val, rows=slice(None)):
    for h in range(RW_HEADS):
        ref[rows, h * HEAD:(h + 1) * HEAD] = val[h]


def _wkv_fwd(r, lw, k, v, kkr, a, g, r_k, ln_w, ln_b, late_pack=None):
    S = r.shape[0]
    H, N, T = RW_HEADS, HEAD, WKV_T
    TS = T * WKV_SUB
    nc = S // TS
    hosting = late_pack is not None

    def body(r_ref, lw_ref, k_ref, v_ref, kkr_ref, a_ref, g_ref, rk_ref, lnw_ref, lnb_ref, *rest):
        if hosting:
            pack_ref, y_ref, yg_ref, wkv_ref, u_ref, inv_ref, s0_ref, gathered_ref, state, *sems = rest
            start, forward, finish = _gather_phases(pack_ref, gathered_ref, *sems)
            pl.when(pl.program_id(0) == 0)(start)
            pl.when(pl.program_id(0) == nc // 2)(forward)
        else:
            y_ref, yg_ref, wkv_ref, u_ref, inv_ref, s0_ref, state = rest

        @pl.when(pl.program_id(0) == 0)
        def _():
            state[...] = jnp.zeros(state.shape, f32)

        S0 = state[...]
        for c in range(WKV_SUB):
            rows = slice(c * T, (c + 1) * T)
            s0_ref[c] = S0
            rr, kk2, vv = _heads(r_ref, rows), _heads(k_ref, rows), _heads(v_ref, rows)
            Y, U, inv, S0 = _wkv_chunk_fwd(S0, rr, _heads(lw_ref, rows), kk2, vv, _heads(kkr_ref, rows),
                                           _heads(a_ref, rows))
            wkv_ref[:, rows, :] = Y
            u_ref[:, rows, :] = U
            inv_ref[:, rows, :] = inv
            mean = jnp.mean(Y, axis=-1, keepdims=True)
            var = jnp.mean(jnp.square(Y - mean), axis=-1, keepdims=True)
            yn = (Y - mean) * lax.rsqrt(var + GN_EPS)
            bonus = jnp.sum(rr * kk2 * rk_ref[...], axis=-1, keepdims=True) * vv
            _put_heads(y_ref, yn * lnw_ref[...] + lnb_ref[...] + bonus, rows)
        state[...] = S0
        yg_ref[...] = (y_ref[...] * g_ref[...]).astype(yg_ref.dtype)
        if hosting:
            pl.when(pl.program_id(0) == nc - 1)(finish)

    tok = pl.BlockSpec((TS, H * N), lambda i: (i, 0))
    blk = pl.BlockSpec((H, TS, N), lambda i: (0, i, 0))
    par = pl.BlockSpec((H, 1, N), lambda i: (0, 0, 0))
    hbm = pl.BlockSpec(memory_space=pl.ANY)
    seq = jax.ShapeDtypeStruct((H, S, N), f32)
    out_specs = [tok, tok, blk, blk, pl.BlockSpec((H, TS, T), lambda i: (0, i, 0)),
                 pl.BlockSpec((WKV_SUB, H, N, N), lambda i: (i, 0, 0, 0))]
    out_shape = [jax.ShapeDtypeStruct((S, H * N), f32), jax.ShapeDtypeStruct((S, H * N), bf16), seq, seq,
                 jax.ShapeDtypeStruct((H, S, T), f32),
                 jax.ShapeDtypeStruct((S // T, H, N, N), f32)]
    if hosting:
        out_specs.append(hbm)
        out_shape.append(jax.ShapeDtypeStruct((N_DEV,) + late_pack.shape, late_pack.dtype))
    return pl.pallas_call(
        body, name="wkv_fwd", grid=(nc,), in_specs=[tok] * 7 + [par] * 3 + [hbm] * hosting,
        out_specs=out_specs, out_shape=out_shape,
        scratch_shapes=[pltpu.VMEM((H, N, N), f32)] + (GATHER_SEMS if hosting else []),
        compiler_params=_cparams(("arbitrary",)),
    )(r, lw, k, v, kkr, a, g, r_k, ln_w, ln_b, *([late_pack] if hosting else []))


def _wkv_bwd(dy, g, r, lw, k, v, kkr, a, wkv, U, inv, S0s, r_k, ln_w, ln_b, late_pieces=None):
    S = r.shape[0]
    H, N, T = RW_HEADS, HEAD, WKV_T
    TS = T * WKV_SUB
    nc = S // TS
    hosting = late_pieces is not None
    n_late = len(late_pieces) if hosting else 0

    def body(dy_ref, g_ref, r_ref, lw_ref, k_ref, v_ref, kkr_ref, a_ref, wkv_ref, u_ref, inv_ref, s0_ref,
             rk_ref, lnw_ref, lnb_ref, *rest):
        if hosting:
            piece_refs, rest = rest[:n_late], rest[n_late:]
            (dr_ref, dlw_ref, dk_ref, dv_ref, dkkr_ref, da_ref, drk_ref, dlnw_ref, dlnb_ref, recv_ref,
             hst, qst, *sems) = rest
            start, finish = _direct_phases(piece_refs, [p.shape[1] for p in late_pieces], recv_ref, *sems)
            pl.when(pl.program_id(0) == 0)(start)
        else:
            dr_ref, dlw_ref, dk_ref, dv_ref, dkkr_ref, da_ref, drk_ref, dlnw_ref, dlnb_ref, hst, qst = rest

        @pl.when(pl.program_id(0) == 0)
        def _():
            hst[...] = jnp.zeros(hst.shape, f32)
            qst[...] = jnp.zeros(qst.shape, f32)
            drk_ref[...] = jnp.zeros(drk_ref.shape, f32)
            dlnw_ref[...] = jnp.zeros(dlnw_ref.shape, f32)
            dlnb_ref[...] = jnp.zeros(dlnb_ref.shape, f32)

        dyg = dy_ref[...] * g_ref[...]
        rk = rk_ref[...]
        Hst, Qst = hst[...], qst[...]
        for c in reversed(range(WKV_SUB)):
            rows = slice(c * T, (c + 1) * T)
            dya = _heads(dyg, rows)
            rr, kk2, vv, Y = _heads(r_ref, rows), _heads(k_ref, rows), _heads(v_ref, rows), wkv_ref[:, rows, :]
            s = jnp.sum(rr * kk2 * rk, axis=-1, keepdims=True)
            ds = jnp.sum(dya * vv, axis=-1, keepdims=True)
            mean = jnp.mean(Y, axis=-1, keepdims=True)
            var = jnp.mean(jnp.square(Y - mean), axis=-1, keepdims=True)
            rstd = lax.rsqrt(var + GN_EPS)
            yn = (Y - mean) * rstd
            dyn = dya * lnw_ref[...]
            dY = rstd * (dyn - jnp.mean(dyn, axis=-1, keepdims=True) - yn * jnp.mean(dyn * yn, axis=-1, keepdims=True))
            drk_ref[...] += jnp.sum(ds * rr * kk2, axis=1, keepdims=True)
            dlnw_ref[...] += jnp.sum(dya * yn, axis=1, keepdims=True)
            dlnb_ref[...] += jnp.sum(dya, axis=1, keepdims=True)
            dr, dlw, dk, dv, dkkr, dasig, Hst, Qst = _wkv_chunk_bwd(
                s0_ref[c], Hst, Qst, rr, _heads(lw_ref, rows), kk2, vv, _heads(kkr_ref, rows), _heads(a_ref, rows),
                u_ref[:, rows, :], inv_ref[:, rows, :], dY)
            _put_heads(dr_ref, dr + ds * kk2 * rk, rows)
            _put_heads(dlw_ref, dlw, rows)
            _put_heads(dk_ref, dk + ds * rr * rk, rows)
            _put_heads(dv_ref, dv + dya * s, rows)
            _put_heads(dkkr_ref, dkkr, rows)
            _put_heads(da_ref, dasig, rows)
        hst[...] = Hst
        qst[...] = Qst
        if hosting:
            pl.when(pl.program_id(0) == nc - 1)(finish)

    tok = pl.BlockSpec((TS, H * N), lambda i: (nc - 1 - i, 0))
    blk = pl.BlockSpec((H, TS, N), lambda i: (0, nc - 1 - i, 0))
    par = pl.BlockSpec((H, 1, N), lambda i: (0, 0, 0))
    hbm = pl.BlockSpec(memory_space=pl.ANY)
    seq = jax.ShapeDtypeStruct((S, H * N), f32)
    pout = jax.ShapeDtypeStruct((H, 1, N), f32)
    out_specs, out_shape = [tok] * 6 + [par] * 3, [seq] * 6 + [pout] * 3
    sems = []
    if hosting:
        rows_late = sum(p.shape[1] for p in late_pieces)
        out_specs.append(hbm)
        out_shape.append(jax.ShapeDtypeStruct((N_DEV - 1, rows_late, late_pieces[0].shape[2]), late_pieces[0].dtype))
        sems = [pltpu.SemaphoreType.DMA((n_late * (N_DEV - 1),))] * 2
    return pl.pallas_call(
        body, name="wkv_bwd", grid=(nc,),
        in_specs=([tok] * 8 + [blk] * 2 + [pl.BlockSpec((H, TS, T), lambda i: (0, nc - 1 - i, 0))]
                  + [pl.BlockSpec((WKV_SUB, H, N, N), lambda i: (nc - 1 - i, 0, 0, 0))]
                  + [par] * 3 + [hbm] * n_late),
        out_specs=out_specs, out_shape=out_shape,
        scratch_shapes=[pltpu.VMEM((H, N, N), f32), pltpu.VMEM((H, 1, N), f32)] + sems,
        compiler_params=_cparams(("arbitrary",)),
    )(dy, g, r, lw, k, v, kkr, a, wkv, U, inv, S0s, r_k, ln_w, ln_b, *(late_pieces if hosting else []))


ATTN_TT = 2048


def _attn_rows(d, i, j):
    return pl.ds(ATTN_BLK * d * i + j, ATTN_BLK, stride=d) if d > 1 else pl.ds(ATTN_BLK * i, ATTN_BLK)


def _attn_take(ref, d, nsub):
    return jnp.stack([ref[_attn_rows(d, i, j), :] for i in range(nsub) for j in range(d)], axis=0)


def _attn_put(ref, val, d):
    for i in range(val.shape[0] // d):
        for j in range(d):
            ref[_attn_rows(d, i, j), :] = val[i * d + j]


def _attn_prev(cur, before, d):
    return before if cur.shape[0] == d else jnp.concatenate([before, cur[:cur.shape[0] - d]], axis=0)


def _attn_specs(gi, d, nt, reverse):
    per_tile = ATTN_TT // (ATTN_BLK * d)

    def tile(n):
        return nt - 1 - n if reverse else n

    def col(kind):
        return lambda hp, n: (tile(n), kind * (ATTN_W // 128) + 2 * gi + hp)

    def col_before(kind):
        return lambda hp, n: (jnp.maximum(tile(n) * per_tile - 1, 0), kind * (ATTN_W // 128) + 2 * gi + hp)

    cur = [pl.BlockSpec((ATTN_TT, 128), col(kind)) for kind in range(3)]
    before = [pl.BlockSpec((ATTN_BLK * d, 128), col_before(kind)) for kind in (1, 2)]
    own = pl.BlockSpec((ATTN_TT, 128), lambda hp, n: (tile(n), hp))
    return cur, before, own, tile


def _attn_norm(x, gain, scale):
    rs = lax.rsqrt(jnp.mean(x * x, axis=-1, keepdims=True) + RMS_EPS)
    return x * rs * (gain * scale), rs


def _attn_scores(qn, kn_c, kn_p, first):
    s_c = jnp.einsum("gqe,gke->gqk", qn.astype(bf16), kn_c.astype(bf16), preferred_element_type=f32)
    s_p = jnp.einsum("gqe,gke->gqk", qn.astype(bf16), kn_p.astype(bf16), preferred_element_type=f32)
    qi = lax.broadcasted_iota(jnp.int32, (1, ATTN_BLK, ATTN_BLK), 1)
    ki = lax.broadcasted_iota(jnp.int32, (1, ATTN_BLK, ATTN_BLK), 2)
    s_c = jnp.where(qi >= ki, s_c, NEG_INF)
    s_p = jnp.where(jnp.logical_and(ki >= qi, jnp.logical_not(first)), s_p, NEG_INF)
    return s_c, s_p


def _attn_fwd(pqkv, qg, kg, gi, S):
    d = ATTN_PAIRS[gi][1]
    nt = S // ATTN_TT
    nsub = ATTN_TT // (ATTN_BLK * d)
    nd = nsub * d

    def body(q_ref, k_ref, v_ref, kb_ref, vb_ref, qg_ref, kg_ref, o_ref, lse_ref):
        Q, K, V = _attn_take(q_ref, d, nsub), _attn_take(k_ref, d, nsub), _attn_take(v_ref, d, nsub)
        KB, VB = _attn_take(kb_ref, d, 1), _attn_take(vb_ref, d, 1)
        first = jnp.logical_and(lax.broadcasted_iota(jnp.int32, (nd, 1, 1), 0) < d, pl.program_id(1) == 0)
        outs, lses = [], []
        for h in range(2):
            sl = slice(h * HEAD, (h + 1) * HEAD)
            kc, vc = K[:, :, sl], V[:, :, sl]
            kp, vp = _attn_prev(kc, KB[:, :, sl], d), _attn_prev(vc, VB[:, :, sl], d)
            qn, _ = _attn_norm(Q[:, :, sl], qg_ref[...], HEAD ** -0.5)
            kn_c, _ = _attn_norm(kc, kg_ref[...], 1.0)
            kn_p, _ = _attn_norm(kp, kg_ref[...], 1.0)
            s_c, s_p = _attn_scores(qn, kn_c, kn_p, first)
            m = jnp.maximum(jnp.max(s_c, axis=-1, keepdims=True), jnp.max(s_p, axis=-1, keepdims=True))
            p_c = jnp.exp(s_c - m)
            p_p = jnp.exp(s_p - m)
            den = jnp.sum(p_c, axis=-1, keepdims=True) + jnp.sum(p_p, axis=-1, keepdims=True)
            inv = 1.0 / den
            o = jnp.einsum("gqk,gke->gqe", (p_c * inv).astype(bf16), vc.astype(bf16), preferred_element_type=f32)
            o += jnp.einsum("gqk,gke->gqe", (p_p * inv).astype(bf16), vp.astype(bf16), preferred_element_type=f32)
            outs.append(o)
            lses.append(jnp.broadcast_to(m + jnp.log(den), o.shape))
        _attn_put(o_ref, jnp.concatenate(outs, axis=-1), d)
        _attn_put(lse_ref, jnp.concatenate(lses, axis=-1), d)

    cur, before, own, _ = _attn_specs(gi, d, nt, False)
    par = pl.BlockSpec((1, HEAD), lambda hp, n: (0, 0))
    shp = jax.ShapeDtypeStruct((S, 2 * 128), f32)
    return pl.pallas_call(
        body, name=f"attn_fwd{gi}", grid=(2, nt), in_specs=cur + before + [par] * 2, out_specs=[own, own],
        out_shape=[shp, shp], compiler_params=_cparams(("arbitrary", "arbitrary")),
    )(pqkv, pqkv, pqkv, pqkv, pqkv, qg, kg)


def _attn_bwd(pqkv, o, lse, do, dlse, qg, kg, gi, S):
    d = ATTN_PAIRS[gi][1]
    nt = S // ATTN_TT
    nsub = ATTN_TT // (ATTN_BLK * d)
    nd = nsub * d

    def norm_bwd(dxn, x, rs, gain, scale):
        xh = x * rs
        dxh = dxn * (gain * scale)
        dx = rs * (dxh - xh * jnp.mean(dxh * xh, axis=-1, keepdims=True))
        dgain = jnp.sum(jnp.sum(dxn * xh * scale, axis=1), axis=0, keepdims=True)
        return dx, dgain

    def to_before(part, carried):
        return carried if nsub == 1 else jnp.concatenate([part[d:], carried], axis=0)

    def body(q_ref, k_ref, v_ref, kb_ref, vb_ref, o_ref, lse_ref, do_ref, dlse_ref, qg_ref, kg_ref,
             dq_ref, dk_ref, dv_ref, dqg_ref, dkg_ref, carry_k, carry_v):
        step = pl.program_id(1)

        @pl.when(jnp.logical_and(pl.program_id(0) == 0, step == 0))
        def _():
            dqg_ref[...] = jnp.zeros(dqg_ref.shape, f32)
            dkg_ref[...] = jnp.zeros(dkg_ref.shape, f32)

        @pl.when(step == 0)
        def _():
            carry_k[...] = jnp.zeros(carry_k.shape, f32)
            carry_v[...] = jnp.zeros(carry_v.shape, f32)

        Q, K, V = _attn_take(q_ref, d, nsub), _attn_take(k_ref, d, nsub), _attn_take(v_ref, d, nsub)
        KB, VB = _attn_take(kb_ref, d, 1), _attn_take(vb_ref, d, 1)
        O, LSE = _attn_take(o_ref, d, nsub), _attn_take(lse_ref, d, nsub)
        DO, DLSE = _attn_take(do_ref, d, nsub), _attn_take(dlse_ref, d, nsub)
        first = jnp.logical_and(lax.broadcasted_iota(jnp.int32, (nd, 1, 1), 0) < d, step == nt - 1)
        qg, kg = qg_ref[...], kg_ref[...]
        dqs, dks, dvs = [], [], []
        for h in range(2):
            sl = slice(h * HEAD, (h + 1) * HEAD)
            qx, kx, vc = Q[:, :, sl], K[:, :, sl], V[:, :, sl]
            kpx, vp = _attn_prev(kx, KB[:, :, sl], d), _attn_prev(vc, VB[:, :, sl], d)
            qn, rq = _attn_norm(qx, qg, HEAD ** -0.5)
            kn_c, rk_c = _attn_norm(kx, kg, 1.0)
            kn_p, _ = _attn_norm(kpx, kg, 1.0)
            s_c, s_p = _attn_scores(qn, kn_c, kn_p, first)
            lse = LSE[:, :, h * HEAD:h * HEAD + 1]
            p_c = jnp.exp(s_c - lse)
            p_p = jnp.exp(s_p - lse)
            dO = DO[:, :, sl]
            dOb = dO.astype(bf16)
            dp_c = jnp.einsum("gqe,gke->gqk", dOb, vc.astype(bf16), preferred_element_type=f32)
            dp_p = jnp.einsum("gqe,gke->gqk", dOb, vp.astype(bf16), preferred_element_type=f32)
            corr = DLSE[:, :, h * HEAD:h * HEAD + 1] - jnp.sum(dO * O[:, :, sl], axis=-1, keepdims=True)
            ds_c = (p_c * (dp_c + corr)).astype(bf16)
            ds_p = (p_p * (dp_p + corr)).astype(bf16)
            qnb = qn.astype(bf16)
            dqn = (jnp.einsum("gqk,gke->gqe", ds_c, kn_c.astype(bf16), preferred_element_type=f32)
                   + jnp.einsum("gqk,gke->gqe", ds_p, kn_p.astype(bf16), preferred_element_type=f32))
            dkn_p = jnp.einsum("gqk,gqe->gke", ds_p, qnb, preferred_element_type=f32)
            dv_p = jnp.einsum("gqk,gqe->gke", p_p.astype(bf16), dOb, preferred_element_type=f32)
            dkn = jnp.einsum("gqk,gqe->gke", ds_c, qnb, preferred_element_type=f32) + to_before(dkn_p, carry_k[h])
            dv = (jnp.einsum("gqk,gqe->gke", p_c.astype(bf16), dOb, preferred_element_type=f32)
                  + to_before(dv_p, carry_v[h]))
            carry_k[h] = dkn_p[:d]
            carry_v[h] = dv_p[:d]
            dq, dqg = norm_bwd(dqn, qx, rq, qg, HEAD ** -0.5)
            dk, dkg = norm_bwd(dkn, kx, rk_c, kg, 1.0)
            dqg_ref[...] += dqg
            dkg_ref[...] += dkg
            dqs.append(dq)
            dks.append(dk)
            dvs.append(dv)
        _attn_put(dq_ref, jnp.concatenate(dqs, axis=-1), d)
        _attn_put(dk_ref, jnp.concatenate(dks, axis=-1), d)
        _attn_put(dv_ref, jnp.concatenate(dvs, axis=-1), d)

    cur, before, own, _ = _attn_specs(gi, d, nt, True)
    par = pl.BlockSpec((1, HEAD), lambda hp, n: (0, 0))
    shp = jax.ShapeDtypeStruct((S, 2 * 128), f32)
    pshp = jax.ShapeDtypeStruct((1, HEAD), f32)
    return pl.pallas_call(
        body, name=f"attn_bwd{gi}", grid=(2, nt), in_specs=cur + before + [own] * 4 + [par] * 2,
        out_specs=[own] * 3 + [par] * 2, out_shape=[shp] * 3 + [pshp] * 2,
        scratch_shapes=[pltpu.VMEM((2, d, ATTN_BLK, HEAD), f32)] * 2,
        compiler_params=_cparams(("arbitrary", "arbitrary")),
    )(pqkv, pqkv, pqkv, pqkv, pqkv, o, lse, do, dlse, qg, kg)


def _rms(x, g):
    rs = lax.rsqrt(jnp.mean(x * x, axis=-1, keepdims=True) + RMS_EPS)
    return x * rs * g


def _f_rms(x, g):
    return _rms(x, g)


def _f_resid_rms(coef, x, f, g):
    xn = x + coef * f
    return xn, _rms(xn, g)


def _f_rms_bwd(n_parts, *args):
    dns = args[:n_parts]
    x, dres, g = args[n_parts:]
    dn = dns[0]
    for t in dns[1:]:
        dn = dn + t
    rs = lax.rsqrt(jnp.mean(x * x, axis=-1, keepdims=True) + RMS_EPS)
    xh = x * rs
    dxh = dn * g
    dx = dres + rs * (dxh - xh * jnp.mean(dxh * xh, axis=-1, keepdims=True))
    return dx, dx, jnp.sum(dn * xh, axis=0, keepdims=True)


def _f_loss(x, f, tgt):
    y = x + 0.5 * f
    diff = y - tgt
    part = 0.5 * jnp.sum(jnp.mean(diff * diff, axis=-1, keepdims=True), axis=0, keepdims=True)
    dy = diff * (1.0 / D)
    return dy, dy, jnp.broadcast_to(part, (1, 128))


def _dotb(a, b, dims):
    return lax.dot_general(a.astype(bf16), b.astype(bf16), dims, preferred_element_type=f32)


_NN = (((1,), (0,)), ((), ()))
_NT = (((1,), (1,)), ((), ()))
_TN = (((0,), (0,)), ((), ()))


def _rwkv_pre_core(prkv, prkv_prev, plora, plora_prev, mu_rkv, mu_lora, w0, w2p, a0, a2p, g2p, k_k, k_a):
    xs = prkv + (prkv_prev - prkv) * mu_rkv
    xl = plora + (plora_prev - plora) * mu_lora
    r, k, v = xs[:, :D], xs[:, D:2 * D], xs[:, 2 * D:]
    wd, ad, gd = xl[:, :128], xl[:, 128:256], xl[:, 256:]
    tw = jnp.tanh(wd)
    zw = w0 + _dotb(tw, w2p, _NN)
    sp = jnp.maximum(-zw, 0.0) + jnp.log(1.0 + jnp.exp(-jnp.abs(zw)))
    lw = -jnp.exp(-sp - 0.5)
    a = jax.nn.sigmoid(a0 + _dotb(ad, a2p, _NN))
    sg = jax.nn.sigmoid(gd)
    return dict(r=r, k=k, v=v, tw=tw, zw=zw, lw=lw, a=a, sg=sg, ad=ad)


def _rows_down(x, halo, blk):
    before = jnp.where(blk > 0, halo[HALO - 1:HALO, :], 0.0)
    row = lax.broadcasted_iota(jnp.int32, (x.shape[0], 1), 0)
    return jnp.where(row == 0, before, pltpu.roll(x, 1, 0))


def _rows_up(x, after):
    n = x.shape[0]
    row = lax.broadcasted_iota(jnp.int32, (n, 1), 0)
    return jnp.where(row == n - 1, after, pltpu.roll(x, n - 1, 0))


def _f_rwkv_pre(prkv, plora, mu_rkv, mu_lora, w0, w2p, a0, a2p, g2p, k_k, k_a, halo_rkv, halo_lora, blk):
    c = _rwkv_pre_core(prkv, _rows_down(prkv, halo_rkv, blk), plora, _rows_down(plora, halo_lora, blk),
                       mu_rkv, mu_lora, w0, w2p, a0, a2p, g2p, k_k, k_a)
    g = _dotb(c["sg"], g2p, _NN)
    k, a = c["k"], c["a"]
    return c["r"], c["lw"], k * (1.0 + (a - 1.0) * k_a), c["v"], k * k_k, a, g


def _f_rwkv_pre_bwd(prkv, plora, dr, dlw, dk2, dv, dkkr, da, dya, yap,
                    mu_rkv, mu_lora, w0, w2p, a0, a2p, g2p, k_k, k_a, halo_rkv, halo_lora, next_rkv, next_lora, blk):
    prkv_prev, plora_prev = _rows_down(prkv, halo_rkv, blk), _rows_down(plora, halo_lora, blk)
    c = _rwkv_pre_core(prkv, prkv_prev, plora, plora_prev, mu_rkv, mu_lora, w0, w2p, a0, a2p, g2p, k_k, k_a)
    k, a, sg, tw, zw, lw = c["k"], c["a"], c["sg"], c["tw"], c["zw"], c["lw"]
    dg = dya * yap
    dsg = _dotb(dg, g2p, _NT)
    dgd = dsg * sg * (1.0 - sg)
    dg2p = _dotb(sg, dg, _TN)
    dk = dk2 * (1.0 + (a - 1.0) * k_a) + dkkr * k_k
    da_t = da + dk2 * k * k_a
    dk_a = jnp.sum(dk2 * k * (a - 1.0), axis=0, keepdims=True)
    dk_k = jnp.sum(dkkr * k, axis=0, keepdims=True)
    dza = da_t * a * (1.0 - a)
    da0 = jnp.sum(dza, axis=0, keepdims=True)
    dad = _dotb(dza, a2p, _NT)
    da2p = _dotb(c["ad"], dza, _TN)
    dzw = dlw * lw * jax.nn.sigmoid(-zw)
    dw0 = jnp.sum(dzw, axis=0, keepdims=True)
    dtw = _dotb(dzw, w2p, _NT)
    dw2p = _dotb(tw, dzw, _TN)
    dwd = dtw * (1.0 - tw * tw)
    dxs = jnp.concatenate([dr, dk, dv], axis=1)
    dxl = jnp.concatenate([dwd, dad, dgd], axis=1)
    dmu_rkv = jnp.sum(dxs * (prkv_prev - prkv), axis=0, keepdims=True)
    dmu_lora = jnp.sum(dxl * (plora_prev - plora), axis=0, keepdims=True)
    to_next_rkv, to_next_lora = dxs * mu_rkv, dxl * mu_lora
    return (dxs * (1.0 - mu_rkv) + _rows_up(to_next_rkv, next_rkv), dxl * (1.0 - mu_lora) + _rows_up(to_next_lora, next_lora),
            dmu_rkv, dmu_lora, dw0, da0, dk_k, dk_a, dw2p, da2p, dg2p, to_next_rkv[0:1], to_next_lora[0:1])


def _group_alpha(l0, l1, l2):
    m = jnp.maximum(jnp.maximum(l0, l1), l2)
    e0, e1, e2 = jnp.exp(l0 - m), jnp.exp(l1 - m), jnp.exp(l2 - m)
    inv = 1.0 / (e0 + e1 + e2)
    return jnp.concatenate([e0 * inv, e1 * inv, e2 * inv], axis=1)


def _f_combine(o0, o1, o2, l0, l1, l2):
    return jnp.concatenate([o0, o1, o2], axis=1) * _group_alpha(l0, l1, l2)


def _f_combine_bwd(dyb, o0, o1, o2, l0, l1, l2, bd):
    alpha = _group_alpha(l0, l1, l2)
    hi, lo = _sp(dyb * jnp.concatenate([o0, o1, o2], axis=1))
    ones = bd.astype(bf16)
    e = jnp.dot(hi, ones, preferred_element_type=f32) + jnp.dot(lo, ones, preferred_element_type=f32)
    ae = alpha * e
    tot = ae[:, :256] + ae[:, 256:512] + ae[:, 512:]
    do = dyb * alpha
    dl = ae - alpha * jnp.concatenate([tot, tot, tot], axis=1)
    return do[:, :256], do[:, 256:512], do[:, 512:], dl[:, :256], dl[:, 256:512], dl[:, 512:]


def _f_merge(pgate, ta, tb, b_gate):
    gate = jax.nn.sigmoid(pgate + b_gate)
    return gate[:, :D] * ta + gate[:, D:] * tb


def _f_merge_bwd(dm, pgate, ta, tb, b_gate):
    gate = jax.nn.sigmoid(pgate + b_gate)
    ga, gb = gate[:, :D], gate[:, D:]
    dpg = jnp.concatenate([dm * ta * ga * (1.0 - ga), dm * tb * gb * (1.0 - gb)], axis=1)
    return dm * ga, dm * gb, dpg, jnp.sum(dpg, axis=0, keepdims=True)


def _f_adamw(w, g, m, v):
    m2 = ADAM_B1 * m + (1.0 - ADAM_B1) * g
    v2 = ADAM_B2 * v + (1.0 - ADAM_B2) * jnp.square(g)
    m_hat = m2 / (1.0 - ADAM_B1 ** ADAM_STEP)
    v_hat = v2 / (1.0 - ADAM_B2 ** ADAM_STEP)
    delta = -ADAM_LR * (m_hat / (jnp.sqrt(v_hat) + ADAM_EPS) + ADAM_WD * w)
    return delta, m2, v2


def _ffn_bwd(tag, dxo, dxo_b, x_in, n, gate, up, act, g, WiT, Wo, cross=None):
    du = _ffn_dact(f"{tag}_dact", dxo_b, Wo, gate, up)
    dWo = _mm(f"{tag}_dwo", act, dxo_b, "tn", out_dtype=GRAD_WIRE, scale=0.5)
    drms = (functools.partial(_f_rms_bwd, 1), [x_in, dxo], [g], [(D, f32), (D, bf16)], [(1, D)])
    dx, dx_b, dg, *recv = _mm(f"{tag}_dn", du, WiT, "nn", cross=cross, epilogue=drms)
    dWiT = _mm(f"{tag}_dwi", du, n, "tn", out_dtype=GRAD_WIRE)
    return dx, dx_b, dg, dWiT, dWo, (recv[0] if recv else None)


def _local_step(x0, tgt, W, P, hooks=None):
    S = x0.shape[0]
    (n1,) = _rowwise("f1_rms", _f_rms, [x0], [P["ffn1_norm"]], [(D, bf16)])
    hooks = hooks or {}
    if "gather_mid" in hooks:
        pack, weights = hooks["gather_mid"]
        gate1, up1, act1, gathered = _ffn_up("f1_up", n1, W["f1_iT"], gather=pack)
        W = {**W, **weights(gathered)}
    else:
        gate1, up1, act1 = _ffn_up("f1_up", n1, W["f1_iT"])
    mix_rms = (lambda f, x, g: _f_resid_rms(0.5, x, f, g), [x0], [P["mix_norm"]], [(D, f32), (D, bf16)], [])
    if "gather_in" in hooks:
        pack, weights = hooks["gather_in"]
        x1, h, gathered = _mm("f1_down", act1, W["f1_o"], "nn", gather=pack, epilogue=mix_rms)
        W = {**W, **weights(gathered)}
    else:
        x1, h = _mm("f1_down", act1, W["f1_o"], "nn", epilogue=mix_rms)
    prkv = _mm("p_rkv", h, W["in_rkvT"], "nt")
    plora = _mm("p_lora", h, W["in_loraT"], "nt")
    pqkv = _mm("p_qkv", h, W["in_qkvT"], "nt")
    pgate = _mm("p_gate", h, W["in_gateT"], "nt")
    pre_params = [P["mu_rkv"], P["mu_lora"], P["w0"], W["w2p"], P["a0"], W["a2p"], W["g2p"], P["k_k"], P["k_a"]]
    r, lw, k2, v, kkr, a, g = _rowwise("rwkv_pre", _f_rwkv_pre, [prkv, plora], pre_params, [(D, f32)] * 7, tm=128,
                                       halos=(0, 1))
    hm = [r, lw, k2, v, kkr, a]
    hp = [P["r_k"].reshape(RW_HEADS, 1, HEAD), P["ln_w"].reshape(RW_HEADS, 1, HEAD), P["ln_b"].reshape(RW_HEADS, 1, HEAD)]
    if "gather_late" in hooks:
        pack, weights = hooks["gather_late"]
        yap, ya, wkv_h, U_h, inv_h, S0s, gathered = _wkv_fwd(*hm, g, *hp, late_pack=pack)
        W = {**W, **weights(gathered)}
    else:
        yap, ya, wkv_h, U_h, inv_h, S0s = _wkv_fwd(*hm, g, *hp)
    ta = _mm("proj_a", ya, W["pr"], "nn")
    n_grp = len(ATTN_PAIRS)
    attn = [_attn_fwd(pqkv, P["q_norm"], P["k_norm"], gi, S) for gi in range(n_grp)]
    o_g, lse_g = [t[0] for t in attn], [t[1] for t in attn]
    (yb,) = _rowwise("attn_combine", _f_combine, [*o_g, *lse_g], [], [(ATTN_W, bf16)])
    merge = (lambda tb, pg, ta, bg: (tb, _f_merge(pg, ta, tb, bg)), [pgate, ta], [P["b_gate"]], [(D, f32), (D, bf16)], [])
    tb, merged = _mm("proj_b", yb, W["paT"], "nt", epilogue=merge)
    f2_rms = (lambda f, x, g: _f_resid_rms(1.0, x, f, g), [x1], [P["ffn2_norm"]], [(D, f32), (D, bf16)], [])
    x2, n2 = _mm("mix_out", merged, W["out"], "nn", epilogue=f2_rms)
    gate2, up2, act2 = _ffn_up("f2_up", n2, W["f2_iT"])
    loss_head = (lambda f, x, t: _f_loss(x, f, t), [x2, tgt], [], [(D, f32), (D, bf16)], [(1, 128)])
    dx3, dx3_b, loss = _mm("f2_down", act2, W["f2_o"], "nn", epilogue=loss_head)
    G, Gs = {}, {}
    dx2, dx2_b, Gs["ffn2_norm"], G["f2_iT"], G["f2_o"], _ = _ffn_bwd("f2", dx3, dx3_b, x2, n2, gate2, up2, act2,
                                                                    P["ffn2_norm"], W["f2_iT"], W["f2_o"])
    merge_bwd = (_f_merge_bwd, [pgate, ta, tb], [P["b_gate"]], [(D, bf16), (D, bf16), (2 * D, bf16)], [(1, 2 * D)])
    dta, dtb, dpgate, Gs["b_gate"] = _mm("d_merged", dx2_b, W["out"], "nt", epilogue=merge_bwd)
    G["out"] = _mm("dw_out", merged, dx2_b, "tn", out_dtype=GRAD_WIRE)
    dya = _mm("d_ya", dta, W["pr"], "nt")
    G["pr"] = _mm("dw_pr", ya, dta, "tn", out_dtype=GRAD_WIRE)
    bd = (jnp.arange(ATTN_W)[:, None] // HEAD == jnp.arange(ATTN_W)[None, :] // HEAD).astype(f32)
    combine_bwd = (_f_combine_bwd, [*o_g, *lse_g], [bd], [(ATTN_W // n_grp, f32)] * (2 * n_grp), [])
    dol = _mm("d_yb", dtb, W["paT"], "nn", epilogue=combine_bwd)
    G["paT"] = _mm("dw_pa", dtb, yb, "tn", out_dtype=GRAD_WIRE)
    if "reduce_late" in hooks:
        pieces_late = hooks["reduce_late"](G)
        hg = _wkv_bwd(dya, g, *hm, wkv_h, U_h, inv_h, S0s, *hp, late_pieces=pieces_late)
        G["late"] = (pieces_late, hg[9])
    else:
        hg = _wkv_bwd(dya, g, *hm, wkv_h, U_h, inv_h, S0s, *hp)
    dr, dlw, dk2, dv, dkkr, da = hg[:6]
    Gs["r_k"], Gs["ln_w"], Gs["ln_b"] = (t.reshape(1, D) for t in hg[6:9])
    lp = sum(LORA_PAD)
    (dprkv, dplora, Gs["mu_rkv"], Gs["mu_lora"], Gs["w0"], Gs["a0"], Gs["k_k"], Gs["k_a"],
     dw2p, da2p, dg2p) = _rowwise(
        "rwkv_pre_bwd", _f_rwkv_pre_bwd,
        [prkv, plora, dr, dlw, dk2, dv, dkkr, da, dya, yap], pre_params,
        [(3 * D, bf16), (lp, bf16)],
        [(1, 3 * D), (1, lp), (1, D), (1, D), (1, D), (1, D), (LORA_PAD[0], D), (LORA_PAD[1], D), (LORA_PAD[2], D)],
        tm=128, halos=(0, 1), carries=((1, 3 * D), (1, lp)), reverse=True)
    G["w2T"], G["a2T"], G["g2T"] = dw2p[:LORA_W[0]].T, da2p[:LORA_W[1]].T, dg2p[:LORA_W[2]].T
    dattn = [_attn_bwd(pqkv, o_g[gi], lse_g[gi], dol[gi], dol[n_grp + gi], P["q_norm"], P["k_norm"], gi, S)
             for gi in range(n_grp)]
    Gs["q_norm"] = dattn[0][3] + dattn[1][3] + dattn[2][3]
    Gs["k_norm"] = dattn[0][4] + dattn[1][4] + dattn[2][4]
    dpqkv = jnp.concatenate([dattn[gi][kind] for kind in range(3) for gi in range(n_grp)], axis=1).astype(bf16)
    dh = [_mm("dh_rkv", dprkv, W["in_rkvT"], "nn"), _mm("dh_lora", dplora, W["in_loraT"], "nn"),
          _mm("dh_gate", dpgate, W["in_gateT"], "nn")]
    mix_drms = (functools.partial(_f_rms_bwd, 4), [*dh, x1, dx2], [P["mix_norm"]], [(D, f32), (D, bf16)], [(1, D)])
    dx1, dx1_b, Gs["mix_norm"] = _mm("dh_qkv", dpqkv, W["in_qkvT"], "nn", epilogue=mix_drms)
    dW_rkv = _mm("dw_rkv", dprkv, h, "tn", out_dtype=GRAD_WIRE)
    dW_lora = _mm("dw_lora", dplora, h, "tn", out_dtype=GRAD_WIRE)
    dW_qkv = _mm("dw_qkv", dpqkv, h, "tn", out_dtype=GRAD_WIRE)
    dW_gate = _mm("dw_gate", dpgate, h, "tn", out_dtype=GRAD_WIRE)
    o1, o2 = LORA_PAD[0], LORA_PAD[0] + LORA_PAD[1]
    G["inT"] = jnp.concatenate([dW_rkv, dW_lora[:LORA_W[0]], dW_lora[o1:o1 + LORA_W[1]], dW_lora[o2:o2 + LORA_W[2]],
                                dW_qkv, dW_gate], axis=0)
    part_mid = hooks["reduce_mid"](G) if "reduce_mid" in hooks else None
    dx0, _, Gs["ffn1_norm"], G["f1_iT"], G["f1_o"], recv_mid = _ffn_bwd(
        "f1", dx1, dx1_b, x0, n1, gate1, up1, act1, P["ffn1_norm"], W["f1_iT"], W["f1_o"], cross=part_mid)
    G["mid"] = (part_mid, recv_mid)
    return loss[0, 0], dx0, G, Gs


def _peer(k):
    x, y, c = lax.axis_index("x"), lax.axis_index("y"), lax.axis_index("c")
    px = 1 - x if k & 4 else x
    py = 1 - y if k & 2 else y
    pc = 1 - c if k & 1 else c
    return (px, py, pc), 4 * px + 2 * py + pc


def _gather_phases(x_ref, out_ref, send_sems, recv_sems, local_sem):
    x, y, c = lax.axis_index("x"), lax.axis_index("y"), lax.axis_index("c")
    me, sibling = (x, y, c), (x, y, 1 - c)
    chips = [(1 - x, y), (x, 1 - y), (1 - x, 1 - y)]

    def slot(px, py, pc):
        return out_ref.at[4 * px + 2 * py + pc]

    def copy(k, block, to, src=None):
        return pltpu.make_async_remote_copy(
            src_ref=slot(*block) if src is None else src, dst_ref=slot(*block), send_sem=send_sems.at[k],
            recv_sem=recv_sems.at[k], device_id=to, device_id_type=MESH)

    def mine():
        return pltpu.make_async_copy(x_ref, slot(*me), local_sem)

    def first():
        return [copy(0, me, sibling, src=x_ref)] + [copy(1 + j, me, (*chip, c), src=x_ref) for j, chip in enumerate(chips)]

    def passed():
        return [copy(4 + j, (*chip, c), sibling) for j, chip in enumerate(chips)]

    def start():
        mine().start()
        for cp in first():
            cp.start()

    def forward():
        for j, (chip, cp) in enumerate(zip(chips, passed())):
            copy(1 + j, (*chip, c), me).wait_recv()
            cp.start()

    def finish():
        copy(0, sibling, me).wait_recv()
        for j, chip in enumerate(chips):
            copy(4 + j, (*chip, 1 - c), me).wait_recv()
        for cp in first() + passed():
            cp.wait_send()
        mine().wait()

    return start, forward, finish


GATHER_SEMS = [pltpu.SemaphoreType.DMA((N_DEV - 1,)), pltpu.SemaphoreType.DMA((N_DEV - 1,)), pltpu.SemaphoreType.DMA(())]


def _all_gather(pack):
    R, C = pack.shape

    def body(x_ref, out_ref, send_sems, recv_sems, local_sem):
        for phase in _gather_phases(x_ref, out_ref, send_sems, recv_sems, local_sem):
            phase()

    return pl.pallas_call(
        body, name="weight_all_gather", out_shape=jax.ShapeDtypeStruct((N_DEV, R, C), pack.dtype),
        in_specs=[pl.BlockSpec(memory_space=pl.ANY)], out_specs=pl.BlockSpec(memory_space=pl.ANY),
        scratch_shapes=GATHER_SEMS,
    )(pack)


def _cross_phases(p_ref, out_ref, send_sems, recv_sems):
    x, y, c = lax.axis_index("x"), lax.axis_index("y"), lax.axis_index("c")

    def copies():
        out = []
        for j, (fx, fy) in enumerate([(1, 0), (0, 1), (1, 1)]):
            px = 1 - x if fx else x
            py = 1 - y if fy else y
            out.append(pltpu.make_async_remote_copy(src_ref=p_ref.at[2 * px + py], dst_ref=out_ref.at[j],
                                                    send_sem=send_sems.at[j], recv_sem=recv_sems.at[j],
                                                    device_id=(px, py, c), device_id_type=MESH))
        return out

    def start():
        for cp in copies():
            cp.start()

    def finish():
        for cp in copies():
            cp.wait()

    return start, finish


CROSS_SEMS = [pltpu.SemaphoreType.DMA((3,)), pltpu.SemaphoreType.DMA((3,))]


def _direct_phases(piece_refs, rows, out_ref, send_sems, recv_sems):
    offs = [sum(rows[:i]) for i in range(len(rows))]

    def copies():
        out = []
        for i, g_ref in enumerate(piece_refs):
            for k in range(1, N_DEV):
                dev, idx = _peer(k)
                out.append(pltpu.make_async_remote_copy(
                    src_ref=g_ref.at[idx], dst_ref=out_ref.at[k - 1, pl.ds(offs[i], rows[i])],
                    send_sem=send_sems.at[i * (N_DEV - 1) + k - 1], recv_sem=recv_sems.at[i * (N_DEV - 1) + k - 1],
                    device_id=dev, device_id_type=MESH))
        return out

    def start():
        for cp in copies():
            cp.start()

    def finish():
        for cp in copies():
            cp.wait()

    return start, finish


def _sum_direct(pieces, recv, me, tag):
    n = len(pieces)
    C = pieces[0].shape[2]
    nblk = [p.shape[1] // PACK_BLOCK for p in pieces]
    lo = [sum(nblk[:i]) for i in range(n)]
    R = sum(nblk) * PACK_BLOCK

    def body(me_ref, *refs):
        g_refs, r_ref, o_ref = refs[:n], refs[n], refs[n + 1]
        rb = pl.program_id(0)
        for i in range(n):
            @pl.when(jnp.logical_and(rb >= lo[i], rb < lo[i] + nblk[i]))
            def _(g_ref=g_refs[i]):
                acc = g_ref[...].astype(f32)
                for k in range(N_DEV - 1):
                    acc = acc + r_ref[k].astype(f32)
                o_ref[...] = acc

    def piece_spec(i):
        return pl.BlockSpec((None, PACK_BLOCK, C), lambda rb, me_ref: (me_ref[0], jnp.clip(rb - lo[i], 0, nblk[i] - 1), 0))

    return pl.pallas_call(
        body, name=f"grad_sum_{tag}",
        grid_spec=pltpu.PrefetchScalarGridSpec(
            num_scalar_prefetch=1, grid=(R // PACK_BLOCK,),
            in_specs=[piece_spec(i) for i in range(n)] + [pl.BlockSpec((N_DEV - 1, PACK_BLOCK, C), lambda rb, me_ref: (0, rb, 0))],
            out_specs=pl.BlockSpec((PACK_BLOCK, C), lambda rb, me_ref: (rb, 0))),
        out_shape=jax.ShapeDtypeStruct((R, C), f32),
        compiler_params=_cparams(("arbitrary",)),
    )(me, *pieces, recv)


N_CHIP = 4


def _grad_pair(pieces, tag):
    n = len(pieces)
    C = pieces[0].shape[2]
    rows = [p.shape[1] for p in pieces]
    offs = [sum(rows[:i]) for i in range(n)]
    R = sum(rows)

    def body(*refs):
        g_refs, (other_ref, send_sems, recv_sems) = refs[:n], refs[n:]
        x, y, c = lax.axis_index("x"), lax.axis_index("y"), lax.axis_index("c")
        copies = []
        for i, g_ref in enumerate(g_refs):
            for k in range(N_CHIP):
                cp = pltpu.make_async_remote_copy(
                    src_ref=g_ref.at[4 * (k // 2) + 2 * (k % 2) + 1 - c], dst_ref=other_ref.at[k, pl.ds(offs[i], rows[i])],
                    send_sem=send_sems.at[i * N_CHIP + k], recv_sem=recv_sems.at[i * N_CHIP + k],
                    device_id=(x, y, 1 - c), device_id_type=MESH)
                cp.start()
                copies.append(cp)
        for cp in copies:
            cp.wait()

    return pl.pallas_call(
        body, name=f"grad_pair_{tag}", out_shape=jax.ShapeDtypeStruct((N_CHIP, R, C), pieces[0].dtype),
        in_specs=[pl.BlockSpec(memory_space=pl.ANY)] * n, out_specs=pl.BlockSpec(memory_space=pl.ANY),
        scratch_shapes=[pltpu.SemaphoreType.DMA((n * N_CHIP,))] * 2,
    )(*pieces)


def _pair_add(pieces, other, c, tag):
    n = len(pieces)
    C = pieces[0].shape[2]
    nblk = [p.shape[1] // PACK_BLOCK for p in pieces]
    lo = [sum(nblk[:i]) for i in range(n)]
    R = sum(nblk) * PACK_BLOCK

    def body(c_ref, *refs):
        g_refs, o_ref, out_ref = refs[:n], refs[n], refs[n + 1]
        rb = pl.program_id(1)
        for i in range(n):
            @pl.when(jnp.logical_and(rb >= lo[i], rb < lo[i] + nblk[i]))
            def _(g_ref=g_refs[i]):
                out_ref[...] = (g_ref[...].astype(f32) + o_ref[...].astype(f32)).astype(out_ref.dtype)

    def piece_spec(i):
        return pl.BlockSpec((1, None, PACK_BLOCK, C),
                            lambda k, rb, c_ref: (k, c_ref[0], jnp.clip(rb - lo[i], 0, nblk[i] - 1), 0))

    blk = pl.BlockSpec((1, PACK_BLOCK, C), lambda k, rb, c_ref: (k, rb, 0))
    return pl.pallas_call(
        body, name=f"pair_add_{tag}",
        grid_spec=pltpu.PrefetchScalarGridSpec(
            num_scalar_prefetch=1, grid=(N_CHIP, R // PACK_BLOCK),
            in_specs=[piece_spec(i) for i in range(n)] + [blk], out_specs=blk),
        out_shape=jax.ShapeDtypeStruct((N_CHIP, R, C), other.dtype),
        compiler_params=_cparams(("arbitrary", "arbitrary")),
    )(c, *[p.reshape(N_CHIP, 2, p.shape[1], C) for p in pieces], other)


def _grad_cross(part):
    _, R, C = part.shape

    def body(p_ref, out_ref, send_sems, recv_sems):
        for phase in _cross_phases(p_ref, out_ref, send_sems, recv_sems):
            phase()

    return pl.pallas_call(
        body, name="grad_cross", out_shape=jax.ShapeDtypeStruct((3, R, C), part.dtype),
        in_specs=[pl.BlockSpec(memory_space=pl.ANY)], out_specs=pl.BlockSpec(memory_space=pl.ANY),
        scratch_shapes=CROSS_SEMS,
    )(part)


def _grad_sum(part, recv, my_chip, tr, tag):
    _, R, C = part.shape

    def body(chip_ref, p_ref, r_ref, o_ref):
        acc = p_ref[0].astype(f32)
        for j in range(3):
            acc = acc + r_ref[j].astype(f32)
        o_ref[...] = acc

    return pl.pallas_call(
        body, name=f"grad_sum_{tag}",
        grid_spec=pltpu.PrefetchScalarGridSpec(
            num_scalar_prefetch=1, grid=(R // tr,),
            in_specs=[pl.BlockSpec((1, tr, C), lambda i, chip_ref: (chip_ref[0], i, 0)),
                      pl.BlockSpec((3, tr, C), lambda i, chip_ref: (0, i, 0))],
            out_specs=pl.BlockSpec((tr, C), lambda i, chip_ref: (i, 0))),
        out_shape=jax.ShapeDtypeStruct((R, C), f32),
        compiler_params=_cparams(("arbitrary",)),
    )(my_chip, part, recv)


def _small_all_reduce(small):
    R, C = small.shape

    def body(x_ref, o_ref, buf, send_sems, recv_sems):
        _, me = _peer(0)
        buf[me] = x_ref[...]
        sends = []
        for k in range(1, N_DEV):
            dev, _ = _peer(k)
            cp = pltpu.make_async_remote_copy(src_ref=x_ref, dst_ref=buf.at[me], send_sem=send_sems.at[k - 1],
                                              recv_sem=recv_sems.at[k - 1], device_id=dev, device_id_type=MESH)
            cp.start()
            sends.append(cp)
        for k in range(1, N_DEV):
            dev, idx = _peer(k)
            pltpu.make_async_remote_copy(src_ref=x_ref, dst_ref=buf.at[idx], send_sem=send_sems.at[k - 1],
                                         recv_sem=recv_sems.at[k - 1], device_id=dev, device_id_type=MESH).wait_recv()
        for cp in sends:
            cp.wait_send()
        acc = buf[0]
        for i in range(1, N_DEV):
            acc = acc + buf[i]
        o_ref[...] = acc

    return pl.pallas_call(
        body, name="small_all_reduce", out_shape=jax.ShapeDtypeStruct((R, C), f32),
        in_specs=[pl.BlockSpec(memory_space=pltpu.VMEM)], out_specs=pl.BlockSpec(memory_space=pltpu.VMEM),
        scratch_shapes=[pltpu.VMEM((N_DEV, R, C), f32), pltpu.SemaphoreType.DMA((N_DEV - 1,)),
                        pltpu.SemaphoreType.DMA((N_DEV - 1,))],
    )(small)


_LORA = (("rwkv_w2", True), ("rwkv_a2", True), ("rwkv_g2", True))
_GROUPS_FIRST = ((("ffn1_w_in", True),),)
_GROUPS_MID = ((("ffn1_w_out", False),), _LORA)
_GROUPS_IN = ((("w_in", True),),)
_GROUPS_LATE = ((("w_proj_rwkv", False),), (("w_proj_attn", True),), (("w_out", False),),
                (("ffn2_w_in", True),), (("ffn2_w_out", False),))
_GRADS_MID = ((("w_in", True),), _LORA)
_GRADS_LAST = ((("ffn1_w_in", True),), (("ffn1_w_out", False),))
_BIG = tuple(item for group in _GROUPS_FIRST + _GROUPS_MID + _GROUPS_IN + _GROUPS_LATE for item in group)
_SMALL = ("ffn1_norm", "mix_norm", "b_gate", "rwkv_mu", "rwkv_w0", "rwkv_a0", "rwkv_k_k", "rwkv_k_a", "rwkv_r_k",
          "rwkv_ln_w", "rwkv_ln_b", "attn_q_norm", "attn_k_norm", "ffn2_norm")


def _pack_layout(like, groups):
    items, spans, off = {}, [], 0
    for group in groups:
        start = off
        for name, _ in group:
            shp = like[name].shape
            n = shp[0] * shp[1] // D
            items[name] = (off, n)
            off += n
        off = -(-off // PACK_BLOCK) * PACK_BLOCK
        spans.append((start, off - start))
    return items, spans, off


def _pack_big(shards, groups):
    items, _, rows = _pack_layout(shards, groups)
    parts, at = [], 0
    for group in groups:
        for name, tr in group:
            off, n = items[name]
            t = shards[name]
            if off > at:
                parts.append(jnp.zeros((off - at, D), t.dtype))
            parts.append((t.T if tr else t).reshape(n, D))
            at = off + n
    if rows > at:
        parts.append(jnp.zeros((rows - at, D), parts[0].dtype))
    return jnp.concatenate(parts, axis=0)


def _unpack_big(pack, like, groups):
    items, _, _ = _pack_layout(like, groups)
    out = {}
    for group in groups:
        for name, tr in group:
            off, n = items[name]
            shp = like[name].shape
            t = pack[off:off + n]
            out[name] = t.reshape(shp[1], shp[0]).T if tr else t.reshape(shp)
    return out


def _unpack_gathered(gathered, like, groups):
    items, _, _ = _pack_layout(like, groups)
    full = {}
    for group in groups:
        for name, tr in group:
            shp = like[name].shape
            off, rows = items[name]
            r_loc, c_loc = (shp[1], shp[0]) if tr else shp
            full[name] = gathered[:, off:off + rows].reshape(N_DEV * r_loc, c_loc)
    return full


def _grad_pieces(g_full, like, groups):
    items, spans, _ = _pack_layout(like, groups)
    pieces = []
    for group, (_, rows_pad) in zip(groups, spans):
        parts = [g_full[n].astype(GRAD_WIRE).reshape(N_DEV, items[n][1], D) for n, _ in group]
        piece = parts[0] if len(parts) == 1 else jnp.concatenate(parts, axis=1)
        if rows_pad > piece.shape[1]:
            piece = jnp.pad(piece, ((0, 0), (0, rows_pad - piece.shape[1]), (0, 0)))
        pieces.append(piece)
    return pieces


def _small_rows(name, t):
    flat = t.reshape(-1)
    pad = (-flat.shape[0]) % D
    return jnp.pad(flat, (0, pad)).reshape(-1, D)


def _pack_small(vals):
    parts = [_small_rows(n, vals[n]) for n in _SMALL]
    used = sum(p.shape[0] for p in parts)
    parts.append(jnp.zeros((SMALL_ROWS - used, D), f32))
    return jnp.concatenate(parts, axis=0)


def _unpack_small(pack, like):
    out, off = {}, 0
    for n in _SMALL:
        size = like[n].size
        rows = -(-size // D)
        out[n] = pack[off:off + rows].reshape(-1)[:size].reshape(like[n].shape)
        off += rows
    return out


def _build_W_mid(full):
    dt = full["rwkv_w2"].dtype
    z64, z96 = jnp.zeros((64, D), dt), jnp.zeros((96, D), dt)
    return {
        "f1_o": full["ffn1_w_out"],
        "w2p": jnp.concatenate([full["rwkv_w2"].T, z64], axis=0),
        "a2p": jnp.concatenate([full["rwkv_a2"].T, z64], axis=0),
        "g2p": jnp.concatenate([full["rwkv_g2"].T, z96], axis=0),
    }


def _build_W_in(full):
    inT = full["w_in"]
    z64, z96 = jnp.zeros((64, D), inT.dtype), jnp.zeros((96, D), inT.dtype)
    return {
        "in_rkvT": inT[:3 * D],
        "in_loraT": jnp.concatenate([inT[3072:3136], z64, inT[3136:3200], z64, inT[3200:3360], z96], axis=0),
        "in_qkvT": inT[3360:3360 + 3 * ATTN_W], "in_gateT": inT[3360 + 3 * ATTN_W:],
    }


def _build_W_late(full):
    return {"pr": full["w_proj_rwkv"], "paT": full["w_proj_attn"], "out": full["w_out"],
            "f2_iT": full["ffn2_w_in"], "f2_o": full["ffn2_w_out"]}


def _build_W_first(full):
    return {"f1_iT": full["ffn1_w_in"]}


def _build_W(full):
    return {**_build_W_first(full), **_build_W_mid(full), **_build_W_in(full), **_build_W_late(full)}


_G_NAMES = {"ffn1_w_in": "f1_iT", "ffn1_w_out": "f1_o", "w_in": "inT", "rwkv_w2": "w2T", "rwkv_a2": "a2T",
            "rwkv_g2": "g2T", "w_proj_rwkv": "pr", "w_proj_attn": "paT", "w_out": "out", "ffn2_w_in": "f2_iT",
            "ffn2_w_out": "f2_o"}


def _named_grads(G, groups):
    return {n: G[_G_NAMES[n]] for group in groups for n, _ in group}


def _reduce_start(G, like, groups, my_c, tag):
    pieces = _grad_pieces(_named_grads(G, groups), like, groups)
    return _pair_add(pieces, _grad_pair(pieces, tag), my_c, tag)


def _build_P(Wl):
    mu = Wl["rwkv_mu"]
    z64f, z96f = jnp.zeros((1, 64), f32), jnp.zeros((1, 96), f32)
    return {
        "ffn1_norm": Wl["ffn1_norm"][None], "mix_norm": Wl["mix_norm"][None], "ffn2_norm": Wl["ffn2_norm"][None],
        "b_gate": Wl["b_gate"][None], "mu_rkv": mu[None, :3 * D],
        "mu_lora": jnp.concatenate([mu[None, 3072:3136], z64f, mu[None, 3136:3200], z64f, mu[None, 3200:3360], z96f], axis=1),
        "w0": Wl["rwkv_w0"][None], "a0": Wl["rwkv_a0"][None], "k_k": Wl["rwkv_k_k"][None], "k_a": Wl["rwkv_k_a"][None],
        "r_k": Wl["rwkv_r_k"].reshape(1, D), "ln_w": Wl["rwkv_ln_w"][None], "ln_b": Wl["rwkv_ln_b"][None],
        "q_norm": Wl["attn_q_norm"][None], "k_norm": Wl["attn_k_norm"][None],
    }


def kernel(x, ffn1_norm, ffn1_w_in, ffn1_w_out, mix_norm, w_in, b_gate, rwkv_mu, rwkv_w0, rwkv_w2, rwkv_a0, rwkv_a2, rwkv_g2, rwkv_k_k, rwkv_k_a, rwkv_r_k, rwkv_ln_w, rwkv_ln_b, attn_q_norm, attn_k_norm, w_proj_rwkv, w_proj_attn, w_out, ffn2_norm, ffn2_w_in, ffn2_w_out, loss_target, m_ffn1_norm, m_ffn1_w_in, m_ffn1_w_out, m_mix_norm, m_w_in, m_b_gate, m_rwkv_mu, m_rwkv_w0, m_rwkv_w2, m_rwkv_a0, m_rwkv_a2, m_rwkv_g2, m_rwkv_k_k, m_rwkv_k_a, m_rwkv_r_k, m_rwkv_ln_w, m_rwkv_ln_b, m_attn_q_norm, m_attn_k_norm, m_w_proj_rwkv, m_w_proj_attn, m_w_out, m_ffn2_norm, m_ffn2_w_in, m_ffn2_w_out, v_ffn1_norm, v_ffn1_w_in, v_ffn1_w_out, v_mix_norm, v_w_in, v_b_gate, v_rwkv_mu, v_rwkv_w0, v_rwkv_w2, v_rwkv_a0, v_rwkv_a2, v_rwkv_g2, v_rwkv_k_k, v_rwkv_k_a, v_rwkv_r_k, v_rwkv_ln_w, v_rwkv_ln_b, v_attn_q_norm, v_attn_k_norm, v_w_proj_rwkv, v_w_proj_attn, v_w_out, v_ffn2_norm, v_ffn2_w_in, v_ffn2_w_out):
    names = ("ffn1_norm", "ffn1_w_in", "ffn1_w_out", "mix_norm", "w_in", "b_gate", "rwkv_mu", "rwkv_w0", "rwkv_w2",
             "rwkv_a0", "rwkv_a2", "rwkv_g2", "rwkv_k_k", "rwkv_k_a", "rwkv_r_k", "rwkv_ln_w", "rwkv_ln_b",
             "attn_q_norm", "attn_k_norm", "w_proj_rwkv", "w_proj_attn", "w_out", "ffn2_norm", "ffn2_w_in", "ffn2_w_out")
    w_all = (ffn1_norm, ffn1_w_in, ffn1_w_out, mix_norm, w_in, b_gate, rwkv_mu, rwkv_w0, rwkv_w2, rwkv_a0, rwkv_a2,
             rwkv_g2, rwkv_k_k, rwkv_k_a, rwkv_r_k, rwkv_ln_w, rwkv_ln_b, attn_q_norm, attn_k_norm, w_proj_rwkv,
             w_proj_attn, w_out, ffn2_norm, ffn2_w_in, ffn2_w_out)
    m_all = (m_ffn1_norm, m_ffn1_w_in, m_ffn1_w_out, m_mix_norm, m_w_in, m_b_gate, m_rwkv_mu, m_rwkv_w0, m_rwkv_w2,
             m_rwkv_a0, m_rwkv_a2, m_rwkv_g2, m_rwkv_k_k, m_rwkv_k_a, m_rwkv_r_k, m_rwkv_ln_w, m_rwkv_ln_b,
             m_attn_q_norm, m_attn_k_norm, m_w_proj_rwkv, m_w_proj_attn, m_w_out, m_ffn2_norm, m_ffn2_w_in, m_ffn2_w_out)
    v_all = (v_ffn1_norm, v_ffn1_w_in, v_ffn1_w_out, v_mix_norm, v_w_in, v_b_gate, v_rwkv_mu, v_rwkv_w0, v_rwkv_w2,
             v_rwkv_a0, v_rwkv_a2, v_rwkv_g2, v_rwkv_k_k, v_rwkv_k_a, v_rwkv_r_k, v_rwkv_ln_w, v_rwkv_ln_b,
             v_attn_q_norm, v_attn_k_norm, v_w_proj_rwkv, v_w_proj_attn, v_w_out, v_ffn2_norm, v_ffn2_w_in, v_ffn2_w_out)
    Wl = {n: t[0] for n, t in zip(names, w_all)}
    Ml = {n: t[0] for n, t in zip(names, m_all)}
    Vl = {n: t[0] for n, t in zip(names, v_all)}
    big = [n for n, _ in _BIG]

    my_c = lax.axis_index("c").astype(jnp.int32).reshape(1)
    my_chip = (2 * lax.axis_index("x") + lax.axis_index("y")).astype(jnp.int32).reshape(1)

    def pack(groups):
        return _pack_big(Wl, groups).astype(bf16)

    gathered = _all_gather(pack(_GROUPS_FIRST))
    W, P = _build_W_first(_unpack_gathered(gathered, Wl, _GROUPS_FIRST)), _build_P(Wl)
    hooks = {"gather_mid": (pack(_GROUPS_MID), lambda g: _build_W_mid(_unpack_gathered(g, Wl, _GROUPS_MID))),
             "gather_in": (pack(_GROUPS_IN), lambda g: _build_W_in(_unpack_gathered(g, Wl, _GROUPS_IN))),
             "gather_late": (pack(_GROUPS_LATE), lambda g: _build_W_late(_unpack_gathered(g, Wl, _GROUPS_LATE))),
             "reduce_mid": lambda G: _reduce_start(G, Wl, _GRADS_MID, my_c, "mid"),
             "reduce_late": lambda G: _grad_pieces(_named_grads(G, _GROUPS_LATE), Wl, _GROUPS_LATE)}

    loss_local, dx0, G, Gs = _local_step(x[0], loss_target[0], W, P, hooks)

    part_last = _reduce_start(G, Wl, _GRADS_LAST, my_c, "last")
    g_big = _unpack_big(_grad_sum(part_last, _grad_cross(part_last), my_chip, 128, "last"), Wl, _GRADS_LAST)
    g_big.update(_unpack_big(_grad_sum(*G["mid"], my_chip, 128, "mid"), Wl, _GRADS_MID))
    me = (4 * lax.axis_index("x") + 2 * lax.axis_index("y") + lax.axis_index("c")).astype(jnp.int32).reshape(1)
    g_big.update(_unpack_big(_sum_direct(*G["late"], me, "late"), Wl, _GROUPS_LATE))

    mu_g = Gs["mu_rkv"], Gs["mu_lora"]
    o1, o2 = LORA_PAD[0], LORA_PAD[0] + LORA_PAD[1]
    g_small_local = {
        "ffn1_norm": Gs["ffn1_norm"], "mix_norm": Gs["mix_norm"], "b_gate": Gs["b_gate"],
        "rwkv_mu": jnp.concatenate([mu_g[0], mu_g[1][:, :64], mu_g[1][:, o1:o1 + 64], mu_g[1][:, o2:o2 + 160]], axis=1),
        "rwkv_w0": Gs["w0"], "rwkv_a0": Gs["a0"], "rwkv_k_k": Gs["k_k"], "rwkv_k_a": Gs["k_a"], "rwkv_r_k": Gs["r_k"],
        "rwkv_ln_w": Gs["ln_w"], "rwkv_ln_b": Gs["ln_b"], "attn_q_norm": Gs["q_norm"], "attn_k_norm": Gs["k_norm"],
        "ffn2_norm": Gs["ffn2_norm"]}
    gs_pack = _small_all_reduce(_pack_small(g_small_local))

    out_g, out_d, out_m, out_v = dict(g_big), {}, {}, {}
    for n in big:
        cols = Wl[n].shape[1]
        out_d[n], out_m[n], out_v[n] = _rowwise(f"adamw_{n}", _f_adamw, [Wl[n], g_big[n], Ml[n], Vl[n]], [],
                                                 [(cols, f32)] * 3)
    ds_pack, ms_pack, vs_pack = _rowwise(
        "adamw_small", _f_adamw, [_pack_small(Wl), gs_pack, _pack_small(Ml), _pack_small(Vl)], [], [(D, f32)] * 3)
    for out, pack in ((out_g, gs_pack), (out_d, ds_pack), (out_m, ms_pack), (out_v, vs_pack)):
        out.update(_unpack_small(pack, Wl))

    loss = lax.psum(loss_local, ("x", "y", "c"))
    return (loss, dx0[None], *[out_g[n][None] for n in names], *[out_d[n][None] for n in names],
            *[out_m[n][None] for n in names], *[out_v[n][None] for n in names])
```

```python
import functools

import jax
import jax.numpy as jnp
from jax import lax
from jax.experimental import pallas as pl
from jax.experimental.pallas import tpu as pltpu

f32 = jnp.float32
bf16 = jnp.bfloat16
MESH = pl.DeviceIdType.MESH

N_DEV = 8
D = 1024
D_FF = 2816
HEAD = 64
RW_HEADS = 16
ATTN_PAIRS = ((128, 1), (512, 4), (2048, 16))
ATTN_BLK = 128
ATTN_W = 768
LORA_PAD = (128, 128, 256)
LORA_W = (64, 64, 160)
GN_EPS = 64e-5
RMS_EPS = 1e-6
NEG_INF = -1e30
WKV_T = 64
WKV_SUB = 2
GRAD_WIRE = bf16
PACK_BLOCK = 128
SMALL_ROWS = 24
VMEM_LIMIT = 56 * 1024 * 1024

ADAM_LR, ADAM_B1, ADAM_B2, ADAM_EPS, ADAM_WD, ADAM_STEP = 0.001, 0.9, 0.999, 1e-08, 0.01, 10


def _cparams(sem):
    return pltpu.CompilerParams(dimension_semantics=sem, vmem_limit_bytes=VMEM_LIMIT)


HALO = 8


def _rowwise(name, fn, rows, params, outs, accs=(), tm=256, halos=(), carries=(), reverse=False):
    S = rows[0].shape[0]
    tm = min(tm, S)
    while S % tm:
        tm -= 8
    nb = S // tm
    n_in = len(rows) + len(params) + len(halos)
    n_out = len(outs)
    n_acc = len(accs)
    n_car = len(carries)

    def blk_of(i):
        return nb - 1 - i if reverse else i

    def body(*refs):
        step = pl.program_id(0)
        carry_refs = refs[n_in + n_out + n_acc:]
        if n_car:
            @pl.when(step == 0)
            def _():
                for c_ref in carry_refs:
                    c_ref[...] = jnp.zeros(c_ref.shape, f32)
        args = [r[...] for r in refs[:n_in]] + [c[...] for c in carry_refs]
        res = fn(*args, blk=blk_of(step)) if (halos or carries) else fn(*args)
        if not isinstance(res, (tuple, list)):
            res = (res,)
        out_refs = refs[n_in:n_in + n_out + n_acc]
        for j in range(n_out):
            out_refs[j][...] = res[j].astype(out_refs[j].dtype)
        if n_acc:
            @pl.when(step == 0)
            def _():
                for j in range(n_acc):
                    out_refs[n_out + j][...] = jnp.zeros(out_refs[n_out + j].shape, f32)
            for j in range(n_acc):
                out_refs[n_out + j][...] += res[n_out + j]
        for j in range(n_car):
            carry_refs[j][...] = res[n_out + n_acc + j]

    in_specs = [pl.BlockSpec((tm, a.shape[1]), lambda i: (blk_of(i), 0)) for a in rows]
    in_specs += [pl.BlockSpec(p.shape, lambda i, nd=p.ndim: (0,) * nd) for p in params]
    in_specs += [pl.BlockSpec((HALO, rows[h].shape[1]), lambda i: (jnp.maximum(blk_of(i) * (tm // HALO) - 1, 0), 0))
                 for h in halos]
    out_specs = [pl.BlockSpec((tm, w), lambda i: (blk_of(i), 0)) for w, _ in outs]
    out_specs += [pl.BlockSpec(s, lambda i: (0, 0)) for s in accs]
    out_shape = [jax.ShapeDtypeStruct((S, w), dt) for w, dt in outs]
    out_shape += [jax.ShapeDtypeStruct(s, f32) for s in accs]
    res = pl.pallas_call(
        body, name=name, grid=(nb,), in_specs=in_specs, out_specs=out_specs, out_shape=out_shape,
        scratch_shapes=[pltpu.VMEM(s, f32) for s in carries],
        compiler_params=_cparams(("arbitrary",)),
    )(*rows, *params, *[rows[h] for h in halos])
    return res


MM_VMEM_BUDGET = 40 * 1024 * 1024
MM_STEP_US = 0.35
MM_FLOPS_PER_US = 9.0e8
MM_HBM_BYTES_PER_US = 3.0e6


def _tile_options(n, cap):
    opts = [d for d in range(128, min(n, cap) + 1, 128) if n % d == 0]
    return opts or [n]


def _mm_tiles(M, N, K, sa, sb, so, whole_rows=False):
    best, best_cost = None, None
    for tm in _tile_options(M, 512 if whole_rows else 2048):
        for tn in ([N] if whole_rows else _tile_options(N, 2048)):
            for tk in _tile_options(K, 4096):
                vmem = 2 * (tm * tk * sa + tk * tn * sb) + 2 * tm * tn * so + (tm * tn * 4 if tk < K else 0)
                if vmem > MM_VMEM_BUDGET:
                    continue
                steps = (M // tm) * (N // tn) * (K // tk)
                traffic = M * K * sa * (N // tn) + K * N * sb * (M // tm) + M * N * so
                cost = (max(2.0 * M * N * K / MM_FLOPS_PER_US, traffic / MM_HBM_BYTES_PER_US) + steps * MM_STEP_US
                        + (tm * tk * sa + tk * tn * sb) / MM_HBM_BYTES_PER_US)
                if best_cost is None or cost < best_cost:
                    best, best_cost = (tm, tn, tk), cost
    return best


def _mm(name, a, b, mode, out_dtype=f32, scale=None, gather=None, cross=None, epilogue=None):
    halves = a.ndim == 3
    sizes = (jnp.dtype(a.dtype).itemsize, jnp.dtype(b.dtype).itemsize, jnp.dtype(out_dtype).itemsize)
    whole = epilogue is not None
    if mode == "nn":
        (M, K), N = (a.shape[1], 2 * a.shape[2]) if halves else a.shape, b.shape[1]
        tm, tn, tk = _mm_tiles(M, N, K // 2 if halves else K, *sizes, whole_rows=whole)
    elif mode == "nt":
        (M, K), N = a.shape, b.shape[0]
        tm, tn, tk = _mm_tiles(M, N, K, *sizes, whole_rows=whole)
    else:
        (K, M), N = (a.shape[1], 2 * a.shape[2]) if halves else a.shape, b.shape[1]
        tm, tn, tk = _mm_tiles(M // 2 if halves else M, N, K, *sizes, whole_rows=whole)
    nk = K // tk
    if mode == "nn":
        per = K // 2 // tk
        a_spec = (pl.BlockSpec((None, tm, tk), lambda i, j, k: (k // per, i, k % per)) if halves
                  else pl.BlockSpec((tm, tk), lambda i, j, k: (i, k)))
        b_spec = pl.BlockSpec((tk, tn), lambda i, j, k: (k, j))
        dims = (((1,), (0,)), ((), ()))
    elif mode == "nt":
        a_spec = pl.BlockSpec((tm, tk), lambda i, j, k: (i, k))
        b_spec = pl.BlockSpec((tn, tk), lambda i, j, k: (j, k))
        dims = (((1,), (1,)), ((), ()))
    else:
        per = M // 2 // tm
        a_spec = (pl.BlockSpec((None, tk, tm), lambda i, j, k: (i // per, k, i % per)) if halves
                  else pl.BlockSpec((tk, tm), lambda i, j, k: (k, i)))
        b_spec = pl.BlockSpec((tk, tn), lambda i, j, k: (k, j))
        dims = (((0,), (0,)), ((), ()))

    hosted = gather if gather is not None else cross
    grid = (M // tm, N // tn, nk)
    steps = grid[0] * grid[1] * grid[2]
    ep_rows, ep_params, ep_outs, ep_accs = ([], [], [], []) if epilogue is None else epilogue[1:]
    n_ep_in, n_ep_out = len(ep_rows) + len(ep_params), len(ep_outs) + len(ep_accs)
    assert epilogue is None or tn == N

    def body(a_ref, b_ref, *rest):
        rest = list(rest)
        src_ref = rest.pop(0) if hosted is not None else None
        ep_in, rest = rest[:n_ep_in], rest[n_ep_in:]
        if epilogue is None:
            o_ref = rest.pop(0)
        else:
            out_refs, rest = rest[:n_ep_out], rest[n_ep_out:]
        dst_ref = rest.pop(0) if hosted is not None else None
        scratch = rest
        step = (pl.program_id(0) * grid[1] + pl.program_id(1)) * grid[2] + pl.program_id(2)
        if hosted is not None:
            n_sem = len(GATHER_SEMS if gather is not None else CROSS_SEMS)
            sems, scratch = scratch[len(scratch) - n_sem:], scratch[:len(scratch) - n_sem]
            if gather is not None:
                start, forward, done = _gather_phases(src_ref, dst_ref, *sems)
                pl.when(step == steps // 2)(forward)
            else:
                start, done = _cross_phases(src_ref, dst_ref, *sems)
            pl.when(step == 0)(start)
        part = lax.dot_general(a_ref[...].astype(bf16), b_ref[...].astype(bf16), dims,
                               preferred_element_type=f32)

        def finish(acc):
            if epilogue is None:
                o_ref[...] = (acc if scale is None else acc * scale).astype(o_ref.dtype)
                return
            res = epilogue[0](acc, *[r[...] for r in ep_in])
            for j in range(len(ep_outs)):
                out_refs[j][...] = res[j].astype(out_refs[j].dtype)
            for j in range(len(ep_accs)):
                acc_out = out_refs[len(ep_outs) + j]

                @pl.when(step == nk - 1)
                def _(acc_out=acc_out):
                    acc_out[...] = jnp.zeros(acc_out.shape, f32)
                acc_out[...] += res[len(ep_outs) + j]

        if nk == 1:
            finish(part)
        else:
            acc_ref = scratch[0]
            k = pl.program_id(2)

            @pl.when(k == 0)
            def _():
                acc_ref[...] = part

            @pl.when(k > 0)
            def _():
                acc_ref[...] += part

            @pl.when(k == nk - 1)
            def _():
                finish(acc_ref[...])
        if hosted is not None:
            pl.when(step == steps - 1)(done)

    hbm = pl.BlockSpec(memory_space=pl.ANY)
    in_specs = [a_spec, b_spec] + [hbm] * (hosted is not None)
    in_specs += [pl.BlockSpec((tm, r.shape[1]), lambda i, j, k: (i, 0)) for r in ep_rows]
    in_specs += [pl.BlockSpec(p.shape, lambda i, j, k: (0, 0)) for p in ep_params]
    if epilogue is None:
        out_specs = [pl.BlockSpec((tm, tn), lambda i, j, k: (i, j))]
        out_shape = [jax.ShapeDtypeStruct((M, N), out_dtype)]
    else:
        out_specs = [pl.BlockSpec((tm, w), lambda i, j, k: (i, 0)) for w, _ in ep_outs]
        out_specs += [pl.BlockSpec(s, lambda i, j, k: (0, 0)) for s in ep_accs]
        out_shape = [jax.ShapeDtypeStruct((M, w), dt) for w, dt in ep_outs]
        out_shape += [jax.ShapeDtypeStruct(s, f32) for s in ep_accs]
    scratch_shapes = [] if nk == 1 else [pltpu.VMEM((tm, tn), f32)]
    if gather is not None:
        out_specs.append(hbm)
        out_shape.append(jax.ShapeDtypeStruct((N_DEV,) + gather.shape, gather.dtype))
        scratch_shapes = scratch_shapes + GATHER_SEMS
    elif cross is not None:
        out_specs.append(hbm)
        out_shape.append(jax.ShapeDtypeStruct((3,) + cross.shape[1:], cross.dtype))
        scratch_shapes = scratch_shapes + CROSS_SEMS
    sequential = hosted is not None or ep_accs
    res = pl.pallas_call(
        body, name=name, grid=grid, in_specs=in_specs,
        out_specs=out_specs, out_shape=out_shape, scratch_shapes=scratch_shapes,
        compiler_params=_cparams(("arbitrary",) * 3 if sequential else ("parallel", "parallel", "arbitrary")),
    )(a, b, *([hosted] if hosted is not None else []), *ep_rows, *ep_params)
    return res[0] if (hosted is None and epilogue is None) else res


FFN_TM, FFN_TN = 512, 1408
FFN_SAVE = bf16


def _ffn_up(name, n, WiT, gather=None):
    S = n.shape[0]
    grid = (S // FFN_TM, D_FF // FFN_TN)
    steps = grid[0] * grid[1]

    def body(n_ref, wg_ref, wu_ref, *rest):
        if gather is None:
            g_ref, u_ref, act_ref = rest
        else:
            src_ref, g_ref, u_ref, act_ref, dst_ref, *sems = rest
            step = pl.program_id(0) * grid[1] + pl.program_id(1)
            start, forward, done = _gather_phases(src_ref, dst_ref, *sems)
            pl.when(step == 0)(start)
            pl.when(step == steps // 2)(forward)
        x = n_ref[...]
        gate = lax.dot_general(x, wg_ref[...], _NT, preferred_element_type=f32)
        up = lax.dot_general(x, wu_ref[...], _NT, preferred_element_type=f32)
        g_ref[...] = gate.astype(g_ref.dtype)
        u_ref[...] = up.astype(u_ref.dtype)
        act_ref[...] = (gate * jax.nn.sigmoid(gate) * up).astype(act_ref.dtype)
        if gather is not None:
            pl.when(step == steps - 1)(done)

    hbm = pl.BlockSpec(memory_space=pl.ANY)
    tile = pl.BlockSpec((FFN_TM, FFN_TN), lambda i, j: (i, j))
    in_specs = [pl.BlockSpec((FFN_TM, D), lambda i, j: (i, 0)), pl.BlockSpec((FFN_TN, D), lambda i, j: (j, 0)),
                pl.BlockSpec((FFN_TN, D), lambda i, j: (j + D_FF // FFN_TN, 0))]
    out_specs = [tile, tile, tile]
    out_shape = [jax.ShapeDtypeStruct((S, D_FF), FFN_SAVE), jax.ShapeDtypeStruct((S, D_FF), FFN_SAVE),
                 jax.ShapeDtypeStruct((S, D_FF), bf16)]
    if gather is not None:
        in_specs.append(hbm)
        out_specs.append(hbm)
        out_shape.append(jax.ShapeDtypeStruct((N_DEV,) + gather.shape, gather.dtype))
    return pl.pallas_call(
        body, name=name, grid=grid, in_specs=in_specs, out_specs=out_specs, out_shape=out_shape,
        scratch_shapes=GATHER_SEMS if gather is not None else [],
        compiler_params=_cparams(("arbitrary", "arbitrary")),
    )(n, WiT, WiT, *([gather] if gather is not None else []))


def _ffn_dact(name, dy, Wo, gate, up):
    S = dy.shape[0]

    def body(dy_ref, wo_ref, g_ref, u_ref, d_ref):
        dact = 0.5 * lax.dot_general(dy_ref[...], wo_ref[...], _NT, preferred_element_type=f32)
        gate, up = g_ref[...].astype(f32), u_ref[...].astype(f32)
        sg = jax.nn.sigmoid(gate)
        d_ref[0] = (dact * up * (sg * (1.0 + gate * (1.0 - sg)))).astype(d_ref.dtype)
        d_ref[1] = (dact * gate * sg).astype(d_ref.dtype)

    tile = pl.BlockSpec((FFN_TM, FFN_TN), lambda i, j: (i, j))
    return pl.pallas_call(
        body, name=name, grid=(S // FFN_TM, D_FF // FFN_TN),
        in_specs=[pl.BlockSpec((FFN_TM, D), lambda i, j: (i, 0)), pl.BlockSpec((FFN_TN, D), lambda i, j: (j, 0)), tile, tile],
        out_specs=pl.BlockSpec((2, FFN_TM, FFN_TN), lambda i, j: (0, i, j)),
        out_shape=jax.ShapeDtypeStruct((2, S, D_FF), bf16),
        compiler_params=_cparams(("parallel", "parallel")),
    )(dy, Wo, gate, up)


def _sp(x):
    hi = x.astype(bf16)
    return hi, (x - hi.astype(f32)).astype(bf16)


def _cat(parts):
    return tuple(jnp.concatenate(p, axis=1) for p in zip(*parts))


def _bmm(eq, a, b):
    (ah, al), (bh, bl) = a, b
    dot = functools.partial(jnp.einsum, eq, preferred_element_type=f32)
    return dot(ah, bh) + (dot(ah, bl) + dot(al, bh))


def _tri_dot(eq, tri, x):
    h1 = x.astype(bf16)
    r1 = x - h1.astype(f32)
    h2 = r1.astype(bf16)
    h3 = (r1 - h2.astype(f32)).astype(bf16)
    dot = functools.partial(jnp.einsum, eq, preferred_element_type=f32)
    return dot(tri, h1) + (dot(tri, h2) + dot(tri, h3))


def _tri_masks(T):
    ti = lax.broadcasted_iota(jnp.int32, (T, T), 0)
    si = lax.broadcasted_iota(jnp.int32, (T, T), 1)
    return ti >= si, ti > si


def _wkv_prep(r, lw, k, kkr, a):
    H, T, _ = r.shape
    low_i, low_s = _tri_masks(T)
    nrm = jnp.sqrt(jnp.sum(kkr * kkr, axis=-1, keepdims=True))
    den = jnp.maximum(nrm, 1e-12)
    kk = kkr / den
    tri = jnp.broadcast_to(low_i.astype(bf16)[None], (H, T, T))
    cl = _tri_dot("hts,hsn->htn", tri, lw)
    c = jnp.exp(cl)
    cprev = jnp.exp(cl - lw)
    cinv = jnp.exp(-cl)
    bt, kt = _sp(kk * a * cinv), _sp(k * cinv)
    L = _cat([_sp(r * c), _sp(-kk * cprev)])
    Mb = _bmm("htn,hsn->hts", L, bt)
    Mk = _bmm("htn,hsn->hts", L, kt)
    A_rb = jnp.where(low_i[None], Mb[:, :T], 0.0)
    A_ab = jnp.where(low_s[None], Mb[:, T:], 0.0)
    Mk = jnp.concatenate([jnp.where(low_i[None], Mk[:, :T], 0.0), jnp.where(low_s[None], Mk[:, T:], 0.0)], axis=1)
    return dict(kk=kk, den=den, nrm=nrm, c=c, cprev=cprev, cinv=cinv, L=L, kt=kt, bt=bt,
                A_ab=A_ab, A_rb=A_rb, Mk=Mk, cT=c[:, T - 1:T, :])


def _tri_inverse(A):
    T = A.shape[-1]
    eye = (lax.broadcasted_iota(jnp.int32, (T, T), 0) == lax.broadcasted_iota(jnp.int32, (T, T), 1)).astype(f32)
    inv = eye[None] + A
    X = A
    n = 1
    while 2 * n < T:
        Xs = _sp(X)
        X = _bmm("hts,hsu->htu", Xs, Xs)
        inv = inv + _bmm("hts,hsu->htu", _sp(inv), _sp(X))
        n *= 2
    return inv


def _wkv_chunk_fwd(S0, r, lw, k, v, kkr, a):
    T = r.shape[1]
    q = _wkv_prep(r, lw, k, kkr, a)
    inv = _tri_inverse(q["A_ab"])
    vs = _sp(v)
    P = _bmm("htk,hvk->htv", q["L"], _sp(S0)) + _bmm("hts,hsv->htv", _sp(q["Mk"]), vs)
    U = _bmm("hts,hsv->htv", _sp(inv), _sp(P[:, T:]))
    Us = _sp(U)
    Y = P[:, :T] + _bmm("hts,hsv->htv", _sp(q["A_rb"]), Us)
    S1 = (S0 + _bmm("htv,htk->hvk", _cat([Us, vs]), _cat([q["bt"], q["kt"]]))) * q["cT"]
    return Y, U, inv, S1


def _wkv_chunk_bwd(S0, Hin, Q, r, lw, k, v, kkr, a, U, inv, dY):
    H, T, _ = r.shape
    low_i, low_s = _tri_masks(T)
    q = _wkv_prep(r, lw, k, kkr, a)
    L, kt, bt = q["L"], q["kt"], q["bt"]
    R = _cat([bt, kt])
    Hh = Hin * q["cT"]
    Hs, S0s, dYs, vs, Us = _sp(Hh), _sp(S0), _sp(dY), _sp(v), _sp(U)
    RH = _bmm("htk,hvk->htv", R, Hs)
    Z = _bmm("hst,hsv->htv", _sp(inv), _sp(RH[:, :T] + _bmm("hst,hsv->htv", _sp(q["A_rb"]), dYs)))
    DZ = _cat([dYs, _sp(Z)])
    both = jnp.concatenate([jnp.broadcast_to(low_i[None], (1, T, T)), jnp.broadcast_to(low_s[None], (1, T, T))], axis=1)
    NU = _sp(jnp.where(both, _bmm("htv,hsv->hts", DZ, Us), 0.0))
    NV = _sp(jnp.where(both, _bmm("htv,hsv->hts", DZ, vs), 0.0))
    ra = _bmm("htv,hvk->htk", DZ, S0s) + _bmm("hts,hsk->htk", NU, bt) + _bmm("hts,hsk->htk", NV, kt)
    dr = ra[:, :T] * q["c"]
    da = ra[:, T:] * q["cprev"]
    dv = RH[:, T:] + _bmm("hst,hsv->htv", _sp(q["Mk"]), DZ)
    VH = _bmm("htv,hvk->htk", _cat([vs, Us]), Hs)
    dk = (VH[:, :T] + _bmm("hst,hsk->htk", NV, L)) * q["cinv"]
    db = (VH[:, T:] + _bmm("hst,hsk->htk", NU, L)) * q["cinv"]
    H0 = Hh + _bmm("htv,htk->hvk", DZ, L)
    kk = q["kk"]
    e = r * dr - kk * a * db - k * dk
    f = -kk * da
    tri_i = jnp.broadcast_to(low_i.astype(bf16)[None], (H, T, T))
    tri_s = jnp.broadcast_to(low_s.astype(bf16)[None], (H, T, T))
    dlw = _tri_dot("hst,hsn->htn", tri_i, e) + _tri_dot("hst,hsn->htn", tri_s, f) + Q
    Qn = Q + jnp.sum(e + f, axis=1, keepdims=True)
    dkk = db * a - da
    dasig = db * kk
    proj = jnp.sum(dkk * kk, axis=-1, keepdims=True)
    dkkr = jnp.where(q["nrm"] > 1e-12, dkk - kk * proj, dkk) / q["den"]
    return dr, dlw, dk, dv, dkkr, dasig, H0, Qn


def _heads(ref, rows=slice(None)):
    return jnp.stack([ref[rows, h * HEAD:(h + 1) * HEAD] for h in range(RW_HEADS)], axis=0)


def _put_heads(ref, val, rows=slice(None)):
    for h in range(RW_HEADS):
        ref[rows, h * HEAD:(h + 1) * HEAD] = val[h]


def _wkv_fwd(r, lw, k, v, kkr, a, g, r_k, ln_w, ln_b, late_pack=None):
    S = r.shape[0]
    H, N, T = RW_HEADS, HEAD, WKV_T
    TS = T * WKV_SUB
    nc = S // TS
    hosting = late_pack is not None

    def body(r_ref, lw_ref, k_ref, v_ref, kkr_ref, a_ref, g_ref, rk_ref, lnw_ref, lnb_ref, *rest):
        if hosting:
            pack_ref, y_ref, yg_ref, wkv_ref, u_ref, inv_ref, s0_ref, gathered_ref, state, *sems = rest
            start, forward, finish = _gather_phases(pack_ref, gathered_ref, *sems)
            pl.when(pl.program_id(0) == 0)(start)
            pl.when(pl.program_id(0) == nc // 2)(forward)
        else:
            y_ref, yg_ref, wkv_ref, u_ref, inv_ref, s0_ref, state = rest

        @pl.when(pl.program_id(0) == 0)
        def _():
            state[...] = jnp.zeros(state.shape, f32)

        S0 = state[...]
        for c in range(WKV_SUB):
            rows = slice(c * T, (c + 1) * T)
            s0_ref[c] = S0
            rr, kk2, vv = _heads(r_ref, rows), _heads(k_ref, rows), _heads(v_ref, rows)
            Y, U, inv, S0 = _wkv_chunk_fwd(S0, rr, _heads(lw_ref, rows), kk2, vv, _heads(kkr_ref, rows),
                                           _heads(a_ref, rows))
            wkv_ref[:, rows, :] = Y
            u_ref[:, rows, :] = U
            inv_ref[:, rows, :] = inv
            mean = jnp.mean(Y, axis=-1, keepdims=True)
            var = jnp.mean(jnp.square(Y - mean), axis=-1, keepdims=True)
            yn = (Y - mean) * lax.rsqrt(var + GN_EPS)
            bonus = jnp.sum(rr * kk2 * rk_ref[...], axis=-1, keepdims=True) * vv
            _put_heads(y_ref, yn * lnw_ref[...] + lnb_ref[...] + bonus, rows)
        state[...] = S0
        yg_ref[...] = (y_ref[...] * g_ref[...]).astype(yg_ref.dtype)
        if hosting:
            pl.when(pl.program_id(0) == nc - 1)(finish)

    tok = pl.BlockSpec((TS, H * N), lambda i: (i, 0))
    blk = pl.BlockSpec((H, TS, N), lambda i: (0, i, 0))
    par = pl.BlockSpec((H, 1, N), lambda i: (0, 0, 0))
    hbm = pl.BlockSpec(memory_space=pl.ANY)
    seq = jax.ShapeDtypeStruct((H, S, N), f32)
    out_specs = [tok, tok, blk, blk, pl.BlockSpec((H, TS, T), lambda i: (0, i, 0)),
                 pl.BlockSpec((WKV_SUB, H, N, N), lambda i: (i, 0, 0, 0))]
    out_shape = [jax.ShapeDtypeStruct((S, H * N), f32), jax.ShapeDtypeStruct((S, H * N), bf16), seq, seq,
                 jax.ShapeDtypeStruct((H, S, T), f32),
                 jax.ShapeDtypeStruct((S // T, H, N, N), f32)]
    if hosting:
        out_specs.append(hbm)
        out_shape.append(jax.ShapeDtypeStruct((N_DEV,) + late_pack.shape, late_pack.dtype))
    return pl.pallas_call(
        body, name="wkv_fwd", grid=(nc,), in_specs=[tok] * 7 + [par] * 3 + [hbm] * hosting,
        out_specs=out_specs, out_shape=out_shape,
        scratch_shapes=[pltpu.VMEM((H, N, N), f32)] + (GATHER_SEMS if hosting else []),
        compiler_params=_cparams(("arbitrary",)),
    )(r, lw, k, v, kkr, a, g, r_k, ln_w, ln_b, *([late_pack] if hosting else []))


def _wkv_bwd(dy, g, r, lw, k, v, kkr, a, wkv, U, inv, S0s, r_k, ln_w, ln_b, late_pieces=None):
    S = r.shape[0]
    H, N, T = RW_HEADS, HEAD, WKV_T
    TS = T * WKV_SUB
    nc = S // TS
    hosting = late_pieces is not None
    n_late = len(late_pieces) if hosting else 0

    def body(dy_ref, g_ref, r_ref, lw_ref, k_ref, v_ref, kkr_ref, a_ref, wkv_ref, u_ref, inv_ref, s0_ref,
             rk_ref, lnw_ref, lnb_ref, *rest):
        if hosting:
            piece_refs, rest = rest[:n_late], rest[n_late:]
            (dr_ref, dlw_ref, dk_ref, dv_ref, dkkr_ref, da_ref, drk_ref, dlnw_ref, dlnb_ref, recv_ref,
             hst, qst, *sems) = rest
            start, finish = _direct_phases(piece_refs, [p.shape[1] for p in late_pieces], recv_ref, *sems)
            pl.when(pl.program_id(0) == 0)(start)
        else:
            dr_ref, dlw_ref, dk_ref, dv_ref, dkkr_ref, da_ref, drk_ref, dlnw_ref, dlnb_ref, hst, qst = rest

        @pl.when(pl.program_id(0) == 0)
        def _():
            hst[...] = jnp.zeros(hst.shape, f32)
            qst[...] = jnp.zeros(qst.shape, f32)
            drk_ref[...] = jnp.zeros(drk_ref.shape, f32)
            dlnw_ref[...] = jnp.zeros(dlnw_ref.shape, f32)
            dlnb_ref[...] = jnp.zeros(dlnb_ref.shape, f32)

        dyg = dy_ref[...] * g_ref[...]
        rk = rk_ref[...]
        Hst, Qst = hst[...], qst[...]
        for c in reversed(range(WKV_SUB)):
            rows = slice(c * T, (c + 1) * T)
            dya = _heads(dyg, rows)
            rr, kk2, vv, Y = _heads(r_ref, rows), _heads(k_ref, rows), _heads(v_ref, rows), wkv_ref[:, rows, :]
            s = jnp.sum(rr * kk2 * rk, axis=-1, keepdims=True)
            ds = jnp.sum(dya * vv, axis=-1, keepdims=True)
            mean = jnp.mean(Y, axis=-1, keepdims=True)
            var = jnp.mean(jnp.square(Y - mean), axis=-1, keepdims=True)
            rstd = lax.rsqrt(var + GN_EPS)
            yn = (Y - mean) * rstd
            dyn = dya * lnw_ref[...]
            dY = rstd * (dyn - jnp.mean(dyn, axis=-1, keepdims=True) - yn * jnp.mean(dyn * yn, axis=-1, keepdims=True))
            drk_ref[...] += jnp.sum(ds * rr * kk2, axis=1, keepdims=True)
            dlnw_ref[...] += jnp.sum(dya * yn, axis=1, keepdims=True)
            dlnb_ref[...] += jnp.sum(dya, axis=1, keepdims=True)
            dr, dlw, dk, dv, dkkr, dasig, Hst, Qst = _wkv_chunk_bwd(
                s0_ref[c], Hst, Qst, rr, _heads(lw_ref, rows), kk2, vv, _heads(kkr_ref, rows), _heads(a_ref, rows),
                u_ref[:, rows, :], inv_ref[:, rows, :], dY)
            _put_heads(dr_ref, dr + ds * kk2 * rk, rows)
            _put_heads(dlw_ref, dlw, rows)
            _put_heads(dk_ref, dk + ds * rr * rk, rows)
            _put_heads(dv_ref, dv + dya * s, rows)
            _put_heads(dkkr_ref, dkkr, rows)
            _put_heads(da_ref, dasig, rows)
        hst[...] = Hst
        qst[...] = Qst
        if hosting:
            pl.when(pl.program_id(0) == nc - 1)(finish)

    tok = pl.BlockSpec((TS, H * N), lambda i: (nc - 1 - i, 0))
    blk = pl.BlockSpec((H, TS, N), lambda i: (0, nc - 1 - i, 0))
    par = pl.BlockSpec((H, 1, N), lambda i: (0, 0, 0))
    hbm = pl.BlockSpec(memory_space=pl.ANY)
    seq = jax.ShapeDtypeStruct((S, H * N), f32)
    pout = jax.ShapeDtypeStruct((H, 1, N), f32)
    out_specs, out_shape = [tok] * 6 + [par] * 3, [seq] * 6 + [pout] * 3
    sems = []
    if hosting:
        rows_late = sum(p.shape[1] for p in late_pieces)
        out_specs.append(hbm)
        out_shape.append(jax.ShapeDtypeStruct((N_DEV - 1, rows_late, late_pieces[0].shape[2]), late_pieces[0].dtype))
        sems = [pltpu.SemaphoreType.DMA((n_late * (N_DEV - 1),))] * 2
    return pl.pallas_call(
        body, name="wkv_bwd", grid=(nc,),
        in_specs=([tok] * 8 + [blk] * 2 + [pl.BlockSpec((H, TS, T), lambda i: (0, nc - 1 - i, 0))]
                  + [pl.BlockSpec((WKV_SUB, H, N, N), lambda i: (nc - 1 - i, 0, 0, 0))]
                  + [par] * 3 + [hbm] * n_late),
        out_specs=out_specs, out_shape=out_shape,
        scratch_shapes=[pltpu.VMEM((H, N, N), f32), pltpu.VMEM((H, 1, N), f32)] + sems,
        compiler_params=_cparams(("arbitrary",)),
    )(dy, g, r, lw, k, v, kkr, a, wkv, U, inv, S0s, r_k, ln_w, ln_b, *(late_pieces if hosting else []))


ATTN_TT = 2048


def _attn_rows(d, i, j):
    return pl.ds(ATTN_BLK * d * i + j, ATTN_BLK, stride=d) if d > 1 else pl.ds(ATTN_BLK * i, ATTN_BLK)


def _attn_take(ref, d, nsub):
    return jnp.stack([ref[_attn_rows(d, i, j), :] for i in range(nsub) for j in range(d)], axis=0)


def _attn_put(ref, val, d):
    for i in range(val.shape[0] // d):
        for j in range(d):
            ref[_attn_rows(d, i, j), :] = val[i * d + j]


def _attn_prev(cur, before, d):
    return before if cur.shape[0] == d else jnp.concatenate([before, cur[:cur.shape[0] - d]], axis=0)


def _attn_specs(gi, d, nt, reverse):
    per_tile = ATTN_TT // (ATTN_BLK * d)

    def tile(n):
        return nt - 1 - n if reverse else n

    def col(kind):
        return lambda hp, n: (tile(n), kind * (ATTN_W // 128) + 2 * gi + hp)

    def col_before(kind):
        return lambda hp, n: (jnp.maximum(tile(n) * per_tile - 1, 0), kind * (ATTN_W // 128) + 2 * gi + hp)

    cur = [pl.BlockSpec((ATTN_TT, 128), col(kind)) for kind in range(3)]
    before = [pl.BlockSpec((ATTN_BLK * d, 128), col_before(kind)) for kind in (1, 2)]
    own = pl.BlockSpec((ATTN_TT, 128), lambda hp, n: (tile(n), hp))
    return cur, before, own, tile


def _attn_norm(x, gain, scale):
    rs = lax.rsqrt(jnp.mean(x * x, axis=-1, keepdims=True) + RMS_EPS)
    return x * rs * (gain * scale), rs


def _attn_scores(qn, kn_c, kn_p, first):
    s_c = jnp.einsum("gqe,gke->gqk", qn.astype(bf16), kn_c.astype(bf16), preferred_element_type=f32)
    s_p = jnp.einsum("gqe,gke->gqk", qn.astype(bf16), kn_p.astype(bf16), preferred_element_type=f32)
    qi = lax.broadcasted_iota(jnp.int32, (1, ATTN_BLK, ATTN_BLK), 1)
    ki = lax.broadcasted_iota(jnp.int32, (1, ATTN_BLK, ATTN_BLK), 2)
    s_c = jnp.where(qi >= ki, s_c, NEG_INF)
    s_p = jnp.where(jnp.logical_and(ki >= qi, jnp.logical_not(first)), s_p, NEG_INF)
    return s_c, s_p


def _attn_fwd(pqkv, qg, kg, gi, S):
    d = ATTN_PAIRS[gi][1]
    nt = S // ATTN_TT
    nsub = ATTN_TT // (ATTN_BLK * d)
    nd = nsub * d

    def body(q_ref, k_ref, v_ref, kb_ref, vb_ref, qg_ref, kg_ref, o_ref, lse_ref):
        Q, K, V = _attn_take(q_ref, d, nsub), _attn_take(k_ref, d, nsub), _attn_take(v_ref, d, nsub)
        KB, VB = _attn_take(kb_ref, d, 1), _attn_take(vb_ref, d, 1)
        first = jnp.logical_and(lax.broadcasted_iota(jnp.int32, (nd, 1, 1), 0) < d, pl.program_id(1) == 0)
        outs, lses = [], []
        for h in range(2):
            sl = slice(h * HEAD, (h + 1) * HEAD)
            kc, vc = K[:, :, sl], V[:, :, sl]
            kp, vp = _attn_prev(kc, KB[:, :, sl], d), _attn_prev(vc, VB[:, :, sl], d)
            qn, _ = _attn_norm(Q[:, :, sl], qg_ref[...], HEAD ** -0.5)
            kn_c, _ = _attn_norm(kc, kg_ref[...], 1.0)
            kn_p, _ = _attn_norm(kp, kg_ref[...], 1.0)
            s_c, s_p = _attn_scores(qn, kn_c, kn_p, first)
            m = jnp.maximum(jnp.max(s_c, axis=-1, keepdims=True), jnp.max(s_p, axis=-1, keepdims=True))
            p_c = jnp.exp(s_c - m)
            p_p = jnp.exp(s_p - m)
            den = jnp.sum(p_c, axis=-1, keepdims=True) + jnp.sum(p_p, axis=-1, keepdims=True)
            inv = 1.0 / den
            o = jnp.einsum("gqk,gke->gqe", (p_c * inv).astype(bf16), vc.astype(bf16), preferred_element_type=f32)
            o += jnp.einsum("gqk,gke->gqe", (p_p * inv).astype(bf16), vp.astype(bf16), preferred_element_type=f32)
            outs.append(o)
            lses.append(jnp.broadcast_to(m + jnp.log(den), o.shape))
        _attn_put(o_ref, jnp.concatenate(outs, axis=-1), d)
        _attn_put(lse_ref, jnp.concatenate(lses, axis=-1), d)

    cur, before, own, _ = _attn_specs(gi, d, nt, False)
    par = pl.BlockSpec((1, HEAD), lambda hp, n: (0, 0))
    shp = jax.ShapeDtypeStruct((S, 2 * 128), f32)
    return pl.pallas_call(
        body, name=f"attn_fwd{gi}", grid=(2, nt), in_specs=cur + before + [par] * 2, out_specs=[own, own],
        out_shape=[shp, shp], compiler_params=_cparams(("arbitrary", "arbitrary")),
    )(pqkv, pqkv, pqkv, pqkv, pqkv, qg, kg)


def _attn_bwd(pqkv, o, lse, do, dlse, qg, kg, gi, S):
    d = ATTN_PAIRS[gi][1]
    nt = S // ATTN_TT
    nsub = ATTN_TT // (ATTN_BLK * d)
    nd = nsub * d

    def norm_bwd(dxn, x, rs, gain, scale):
        xh = x * rs
        dxh = dxn * (gain * scale)
        dx = rs * (dxh - xh * jnp.mean(dxh * xh, axis=-1, keepdims=True))
        dgain = jnp.sum(jnp.sum(dxn * xh * scale, axis=1), axis=0, keepdims=True)
        return dx, dgain

    def to_before(part, carried):
        return carried if nsub == 1 else jnp.concatenate([part[d:], carried], axis=0)

    def body(q_ref, k_ref, v_ref, kb_ref, vb_ref, o_ref, lse_ref, do_ref, dlse_ref, qg_ref, kg_ref,
             dq_ref, dk_ref, dv_ref, dqg_ref, dkg_ref, carry_k, carry_v):
        step = pl.program_id(1)

        @pl.when(jnp.logical_and(pl.program_id(0) == 0, step == 0))
        def _():
            dqg_ref[...] = jnp.zeros(dqg_ref.shape, f32)
            dkg_ref[...] = jnp.zeros(dkg_ref.shape, f32)

        @pl.when(step == 0)
        def _():
            carry_k[...] = jnp.zeros(carry_k.shape, f32)
            carry_v[...] = jnp.zeros(carry_v.shape, f32)

        Q, K, V = _attn_take(q_ref, d, nsub), _attn_take(k_ref, d, nsub), _attn_take(v_ref, d, nsub)
        KB, VB = _attn_take(kb_ref, d, 1), _attn_take(vb_ref, d, 1)
        O, LSE = _attn_take(o_ref, d, nsub), _attn_take(lse_ref, d, nsub)
        DO, DLSE = _attn_take(do_ref, d, nsub), _attn_take(dlse_ref, d, nsub)
        first = jnp.logical_and(lax.broadcasted_iota(jnp.int32, (nd, 1, 1), 0) < d, step == nt - 1)
        qg, kg = qg_ref[...], kg_ref[...]
        dqs, dks, dvs = [], [], []
        for h in range(2):
            sl = slice(h * HEAD, (h + 1) * HEAD)
            qx, kx, vc = Q[:, :, sl], K[:, :, sl], V[:, :, sl]
            kpx, vp = _attn_prev(kx, KB[:, :, sl], d), _attn_prev(vc, VB[:, :, sl], d)
            qn, rq = _attn_norm(qx, qg, HEAD ** -0.5)
            kn_c, rk_c = _attn_norm(kx, kg, 1.0)
            kn_p, _ = _attn_norm(kpx, kg, 1.0)
            s_c, s_p = _attn_scores(qn, kn_c, kn_p, first)
            lse = LSE[:, :, h * HEAD:h * HEAD + 1]
            p_c = jnp.exp(s_c - lse)
            p_p = jnp.exp(s_p - lse)
            dO = DO[:, :, sl]
            dOb = dO.astype(bf16)
            dp_c = jnp.einsum("gqe,gke->gqk", dOb, vc.astype(bf16), preferred_element_type=f32)
            dp_p = jnp.einsum("gqe,gke->gqk", dOb, vp.astype(bf16), preferred_element_type=f32)
            corr = DLSE[:, :, h * HEAD:h * HEAD + 1] - jnp.sum(dO * O[:, :, sl], axis=-1, keepdims=True)
            ds_c = (p_c * (dp_c + corr)).astype(bf16)
            ds_p = (p_p * (dp_p + corr)).astype(bf16)
            qnb = qn.astype(bf16)
            dqn = (jnp.einsum("gqk,gke->gqe", ds_c, kn_c.astype(bf16), preferred_element_type=f32)
                   + jnp.einsum("gqk,gke->gqe", ds_p, kn_p.astype(bf16), preferred_element_type=f32))
            dkn_p = jnp.einsum("gqk,gqe->gke", ds_p, qnb, preferred_element_type=f32)
            dv_p = jnp.einsum("gqk,gqe->gke", p_p.astype(bf16), dOb, preferred_element_type=f32)
            dkn = jnp.einsum("gqk,gqe->gke", ds_c, qnb, preferred_element_type=f32) + to_before(dkn_p, carry_k[h])
            dv = (jnp.einsum("gqk,gqe->gke", p_c.astype(bf16), dOb, preferred_element_type=f32)
                  + to_before(dv_p, carry_v[h]))
            carry_k[h] = dkn_p[:d]
            carry_v[h] = dv_p[:d]
            dq, dqg = norm_bwd(dqn, qx, rq, qg, HEAD ** -0.5)
            dk, dkg = norm_bwd(dkn, kx, rk_c, kg, 1.0)
            dqg_ref[...] += dqg
            dkg_ref[...] += dkg
            dqs.append(dq)
            dks.append(dk)
            dvs.append(dv)
        _attn_put(dq_ref, jnp.concatenate(dqs, axis=-1), d)
        _attn_put(dk_ref, jnp.concatenate(dks, axis=-1), d)
        _attn_put(dv_ref, jnp.concatenate(dvs, axis=-1), d)

    cur, before, own, _ = _attn_specs(gi, d, nt, True)
    par = pl.BlockSpec((1, HEAD), lambda hp, n: (0, 0))
    shp = jax.ShapeDtypeStruct((S, 2 * 128), f32)
    pshp = jax.ShapeDtypeStruct((1, HEAD), f32)
    return pl.pallas_call(
        body, name=f"attn_bwd{gi}", grid=(2, nt), in_specs=cur + before + [own] * 4 + [par] * 2,
        out_specs=[own] * 3 + [par] * 2, out_shape=[shp] * 3 + [pshp] * 2,
        scratch_shapes=[pltpu.VMEM((2, d, ATTN_BLK, HEAD), f32)] * 2,
        compiler_params=_cparams(("arbitrary", "arbitrary")),
    )(pqkv, pqkv, pqkv, pqkv, pqkv, o, lse, do, dlse, qg, kg)


def _rms(x, g):
    rs = lax.rsqrt(jnp.mean(x * x, axis=-1, keepdims=True) + RMS_EPS)
    return x * rs * g


def _f_rms(x, g):
    return _rms(x, g)


def _f_resid_rms(coef, x, f, g):
    xn = x + coef * f
    return xn, _rms(xn, g)


def _f_rms_bwd(n_parts, *args):
    dns = args[:n_parts]
    x, dres, g = args[n_parts:]
    dn = dns[0]
    for t in dns[1:]:
        dn = dn + t
    rs = lax.rsqrt(jnp.mean(x * x, axis=-1, keepdims=True) + RMS_EPS)
    xh = x * rs
    dxh = dn * g
    dx = dres + rs * (dxh - xh * jnp.mean(dxh * xh, axis=-1, keepdims=True))
    return dx, dx, jnp.sum(dn * xh, axis=0, keepdims=True)


def _f_loss(x, f, tgt):
    y = x + 0.5 * f
    diff = y - tgt
    part = 0.5 * jnp.sum(jnp.mean(diff * diff, axis=-1, keepdims=True), axis=0, keepdims=True)
    dy = diff * (1.0 / D)
    return dy, dy, jnp.broadcast_to(part, (1, 128))


def _dotb(a, b, dims):
    return lax.dot_general(a.astype(bf16), b.astype(bf16), dims, preferred_element_type=f32)


_NN = (((1,), (0,)), ((), ()))
_NT = (((1,), (1,)), ((), ()))
_TN = (((0,), (0,)), ((), ()))


def _rwkv_pre_core(prkv, prkv_prev, plora, plora_prev, mu_rkv, mu_lora, w0, w2p, a0, a2p, g2p, k_k, k_a):
    xs = prkv + (prkv_prev - prkv) * mu_rkv
    xl = plora + (plora_prev - plora) * mu_lora
    r, k, v = xs[:, :D], xs[:, D:2 * D], xs[:, 2 * D:]
    wd, ad, gd = xl[:, :128], xl[:, 128:256], xl[:, 256:]
    tw = jnp.tanh(wd)
    zw = w0 + _dotb(tw, w2p, _NN)
    sp = jnp.maximum(-zw, 0.0) + jnp.log(1.0 + jnp.exp(-jnp.abs(zw)))
    lw = -jnp.exp(-sp - 0.5)
    a = jax.nn.sigmoid(a0 + _dotb(ad, a2p, _NN))
    sg = jax.nn.sigmoid(gd)
    return dict(r=r, k=k, v=v, tw=tw, zw=zw, lw=lw, a=a, sg=sg, ad=ad)


def _rows_down(x, halo, blk):
    before = jnp.where(blk > 0, halo[HALO - 1:HALO, :], 0.0)
    row = lax.broadcasted_iota(jnp.int32, (x.shape[0], 1), 0)
    return jnp.where(row == 0, before, pltpu.roll(x, 1, 0))


def _rows_up(x, after):
    n = x.shape[0]
    row = lax.broadcasted_iota(jnp.int32, (n, 1), 0)
    return jnp.where(row == n - 1, after, pltpu.roll(x, n - 1, 0))


def _f_rwkv_pre(prkv, plora, mu_rkv, mu_lora, w0, w2p, a0, a2p, g2p, k_k, k_a, halo_rkv, halo_lora, blk):
    c = _rwkv_pre_core(prkv, _rows_down(prkv, halo_rkv, blk), plora, _rows_down(plora, halo_lora, blk),
                       mu_rkv, mu_lora, w0, w2p, a0, a2p, g2p, k_k, k_a)
    g = _dotb(c["sg"], g2p, _NN)
    k, a = c["k"], c["a"]
    return c["r"], c["lw"], k * (1.0 + (a - 1.0) * k_a), c["v"], k * k_k, a, g


def _f_rwkv_pre_bwd(prkv, plora, dr, dlw, dk2, dv, dkkr, da, dya, yap,
                    mu_rkv, mu_lora, w0, w2p, a0, a2p, g2p, k_k, k_a, halo_rkv, halo_lora, next_rkv, next_lora, blk):
    prkv_prev, plora_prev = _rows_down(prkv, halo_rkv, blk), _rows_down(plora, halo_lora, blk)
    c = _rwkv_pre_core(prkv, prkv_prev, plora, plora_prev, mu_rkv, mu_lora, w0, w2p, a0, a2p, g2p, k_k, k_a)
    k, a, sg, tw, zw, lw = c["k"], c["a"], c["sg"], c["tw"], c["zw"], c["lw"]
    dg = dya * yap
    dsg = _dotb(dg, g2p, _NT)
    dgd = dsg * sg * (1.0 - sg)
    dg2p = _dotb(sg, dg, _TN)
    dk = dk2 * (1.0 + (a - 1.0) * k_a) + dkkr * k_k
    da_t = da + dk2 * k * k_a
    dk_a = jnp.sum(dk2 * k * (a - 1.0), axis=0, keepdims=True)
    dk_k = jnp.sum(dkkr * k, axis=0, keepdims=True)
    dza = da_t * a * (1.0 - a)
    da0 = jnp.sum(dza, axis=0, keepdims=True)
    dad = _dotb(dza, a2p, _NT)
    da2p = _dotb(c["ad"], dza, _TN)
    dzw = dlw * lw * jax.nn.sigmoid(-zw)
    dw0 = jnp.sum(dzw, axis=0, keepdims=True)
    dtw = _dotb(dzw, w2p, _NT)
    dw2p = _dotb(tw, dzw, _TN)
    dwd = dtw * (1.0 - tw * tw)
    dxs = jnp.concatenate([dr, dk, dv], axis=1)
    dxl = jnp.concatenate([dwd, dad, dgd], axis=1)
    dmu_rkv = jnp.sum(dxs * (prkv_prev - prkv), axis=0, keepdims=True)
    dmu_lora = jnp.sum(dxl * (plora_prev - plora), axis=0, keepdims=True)
    to_next_rkv, to_next_lora = dxs * mu_rkv, dxl * mu_lora
    return (dxs * (1.0 - mu_rkv) + _rows_up(to_next_rkv, next_rkv), dxl * (1.0 - mu_lora) + _rows_up(to_next_lora, next_lora),
            dmu_rkv, dmu_lora, dw0, da0, dk_k, dk_a, dw2p, da2p, dg2p, to_next_rkv[0:1], to_next_lora[0:1])


def _group_alpha(l0, l1, l2):
    m = jnp.maximum(jnp.maximum(l0, l1), l2)
    e0, e1, e2 = jnp.exp(l0 - m), jnp.exp(l1 - m), jnp.exp(l2 - m)
    inv = 1.0 / (e0 + e1 + e2)
    return jnp.concatenate([e0 * inv, e1 * inv, e2 * inv], axis=1)


def _f_combine(o0, o1, o2, l0, l1, l2):
    return jnp.concatenate([o0, o1, o2], axis=1) * _group_alpha(l0, l1, l2)


def _f_combine_bwd(dyb, o0, o1, o2, l0, l1, l2, bd):
    alpha = _group_alpha(l0, l1, l2)
    hi, lo = _sp(dyb * jnp.concatenate([o0, o1, o2], axis=1))
    ones = bd.astype(bf16)
    e = jnp.dot(hi, ones, preferred_element_type=f32) + jnp.dot(lo, ones, preferred_element_type=f32)
    ae = alpha * e
    tot = ae[:, :256] + ae[:, 256:512] + ae[:, 512:]
    do = dyb * alpha
    dl = ae - alpha * jnp.concatenate([tot, tot, tot], axis=1)
    return do[:, :256], do[:, 256:512], do[:, 512:], dl[:, :256], dl[:, 256:512], dl[:, 512:]


def _f_merge(pgate, ta, tb, b_gate):
    gate = jax.nn.sigmoid(pgate + b_gate)
    return gate[:, :D] * ta + gate[:, D:] * tb


def _f_merge_bwd(dm, pgate, ta, tb, b_gate):
    gate = jax.nn.sigmoid(pgate + b_gate)
    ga, gb = gate[:, :D], gate[:, D:]
    dpg = jnp.concatenate([dm * ta * ga * (1.0 - ga), dm * tb * gb * (1.0 - gb)], axis=1)
    return dm * ga, dm * gb, dpg, jnp.sum(dpg, axis=0, keepdims=True)


def _f_adamw(w, g, m, v):
    m2 = ADAM_B1 * m + (1.0 - ADAM_B1) * g
    v2 = ADAM_B2 * v + (1.0 - ADAM_B2) * jnp.square(g)
    m_hat = m2 / (1.0 - ADAM_B1 ** ADAM_STEP)
    v_hat = v2 / (1.0 - ADAM_B2 ** ADAM_STEP)
    delta = -ADAM_LR * (m_hat / (jnp.sqrt(v_hat) + ADAM_EPS) + ADAM_WD * w)
    return delta, m2, v2


def _ffn_bwd(tag, dxo, dxo_b, x_in, n, gate, up, act, g, WiT, Wo, cross=None):
    du = _ffn_dact(f"{tag}_dact", dxo_b, Wo, gate, up)
    dWo = _mm(f"{tag}_dwo", act, dxo_b, "tn", out_dtype=GRAD_WIRE, scale=0.5)
    drms = (functools.partial(_f_rms_bwd, 1), [x_in, dxo], [g], [(D, f32), (D, bf16)], [(1, D)])
    dx, dx_b, dg, *recv = _mm(f"{tag}_dn", du, WiT, "nn", cross=cross, epilogue=drms)
    dWiT = _mm(f"{tag}_dwi", du, n, "tn", out_dtype=GRAD_WIRE)
    return dx, dx_b, dg, dWiT, dWo, (recv[0] if recv else None)


def _local_step(x0, tgt, W, P, hooks=None):
    S = x0.shape[0]
    (n1,) = _rowwise("f1_rms", _f_rms, [x0], [P["ffn1_norm"]], [(D, bf16)])
    hooks = hooks or {}
    if "gather_mid" in hooks:
        pack, weights = hooks["gather_mid"]
        gate1, up1, act1, gathered = _ffn_up("f1_up", n1, W["f1_iT"], gather=pack)
        W = {**W, **weights(gathered)}
    else:
        gate1, up1, act1 = _ffn_up("f1_up", n1, W["f1_iT"])
    mix_rms = (lambda f, x, g: _f_resid_rms(0.5, x, f, g), [x0], [P["mix_norm"]], [(D, f32), (D, bf16)], [])
    if "gather_in" in hooks:
        pack, weights = hooks["gather_in"]
        x1, h, gathered = _mm("f1_down", act1, W["f1_o"], "nn", gather=pack, epilogue=mix_rms)
        W = {**W, **weights(gathered)}
    else:
        x1, h = _mm("f1_down", act1, W["f1_o"], "nn", epilogue=mix_rms)
    prkv = _mm("p_rkv", h, W["in_rkvT"], "nt")
    plora = _mm("p_lora", h, W["in_loraT"], "nt")
    pqkv = _mm("p_qkv", h, W["in_qkvT"], "nt")
    pgate = _mm("p_gate", h, W["in_gateT"], "nt")
    pre_params = [P["mu_rkv"], P["mu_lora"], P["w0"], W["w2p"], P["a0"], W["a2p"], W["g2p"], P["k_k"], P["k_a"]]
    r, lw, k2, v, kkr, a, g = _rowwise("rwkv_pre", _f_rwkv_pre, [prkv, plora], pre_params, [(D, f32)] * 7, tm=128,
                                       halos=(0, 1))
    hm = [r, lw, k2, v, kkr, a]
    hp = [P["r_k"].reshape(RW_HEADS, 1, HEAD), P["ln_w"].reshape(RW_HEADS, 1, HEAD), P["ln_b"].reshape(RW_HEADS, 1, HEAD)]
    if "gather_late" in hooks:
        pack, weights = hooks["gather_late"]
        yap, ya, wkv_h, U_h, inv_h, S0s, gathered = _wkv_fwd(*hm, g, *hp, late_pack=pack)
        W = {**W, **weights(gathered)}
    else:
        yap, ya, wkv_h, U_h, inv_h, S0s = _wkv_fwd(*hm, g, *hp)
    ta = _mm("proj_a", ya, W["pr"], "nn")
    n_grp = len(ATTN_PAIRS)
    attn = [_attn_fwd(pqkv, P["q_norm"], P["k_norm"], gi, S) for gi in range(n_grp)]
    o_g, lse_g = [t[0] for t in attn], [t[1] for t in attn]
    (yb,) = _rowwise("attn_combine", _f_combine, [*o_g, *lse_g], [], [(ATTN_W, bf16)])
    merge = (lambda tb, pg, ta, bg: (tb, _f_merge(pg, ta, tb, bg)), [pgate, ta], [P["b_gate"]], [(D, f32), (D, bf16)], [])
    tb, merged = _mm("proj_b", yb, W["paT"], "nt", epilogue=merge)
    f2_rms = (lambda f, x, g: _f_resid_rms(1.0, x, f, g), [x1], [P["ffn2_norm"]], [(D, f32), (D, bf16)], [])
    x2, n2 = _mm("mix_out", merged, W["out"], "nn", epilogue=f2_rms)
    gate2, up2, act2 = _ffn_up("f2_up", n2, W["f2_iT"])
    loss_head = (lambda f, x, t: _f_loss(x, f, t), [x2, tgt], [], [(D, f32), (D, bf16)], [(1, 128)])
    dx3, dx3_b, loss = _mm("f2_down", act2, W["f2_o"], "nn", epilogue=loss_head)
    G, Gs = {}, {}
    dx2, dx2_b, Gs["ffn2_norm"], G["f2_iT"], G["f2_o"], _ = _ffn_bwd("f2", dx3, dx3_b, x2, n2, gate2, up2, act2,
                                                                    P["ffn2_norm"], W["f2_iT"], W["f2_o"])
    merge_bwd = (_f_merge_bwd, [pgate, ta, tb], [P["b_gate"]], [(D, bf16), (D, bf16), (2 * D, bf16)], [(1, 2 * D)])
    dta, dtb, dpgate, Gs["b_gate"] = _mm("d_merged", dx2_b, W["out"], "nt", epilogue=merge_bwd)
    G["out"] = _mm("dw_out", merged, dx2_b, "tn", out_dtype=GRAD_WIRE)
    dya = _mm("d_ya", dta, W["pr"], "nt")
    G["pr"] = _mm("dw_pr", ya, dta, "tn", out_dtype=GRAD_WIRE)
    bd = (jnp.arange(ATTN_W)[:, None] // HEAD == jnp.arange(ATTN_W)[None, :] // HEAD).astype(f32)
    combine_bwd = (_f_combine_bwd, [*o_g, *lse_g], [bd], [(ATTN_W // n_grp, f32)] * (2 * n_grp), [])
    dol = _mm("d_yb", dtb, W["paT"], "nn", epilogue=combine_bwd)
    G["paT"] = _mm("dw_pa", dtb, yb, "tn", out_dtype=GRAD_WIRE)
    if "reduce_late" in hooks:
        pieces_late = hooks["reduce_late"](G)
        hg = _wkv_bwd(dya, g, *hm, wkv_h, U_h, inv_h, S0s, *hp, late_pieces=pieces_late)
        G["late"] = (pieces_late, hg[9])
    else:
        hg = _wkv_bwd(dya, g, *hm, wkv_h, U_h, inv_h, S0s, *hp)
    dr, dlw, dk2, dv, dkkr, da = hg[:6]
    Gs["r_k"], Gs["ln_w"], Gs["ln_b"] = (t.reshape(1, D) for t in hg[6:9])
    lp = sum(LORA_PAD)
    (dprkv, dplora, Gs["mu_rkv"], Gs["mu_lora"], Gs["w0"], Gs["a0"], Gs["k_k"], Gs["k_a"],
     dw2p, da2p, dg2p) = _rowwise(
        "rwkv_pre_bwd", _f_rwkv_pre_bwd,
        [prkv, plora, dr, dlw, dk2, dv, dkkr, da, dya, yap], pre_params,
        [(3 * D, bf16), (lp, bf16)],
        [(1, 3 * D), (1, lp), (1, D), (1, D), (1, D), (1, D), (LORA_PAD[0], D), (LORA_PAD[1], D), (LORA_PAD[2], D)],
        tm=128, halos=(0, 1), carries=((1, 3 * D), (1, lp)), reverse=True)
    G["w2T"], G["a2T"], G["g2T"] = dw2p[:LORA_W[0]].T, da2p[:LORA_W[1]].T, dg2p[:LORA_W[2]].T
    dattn = [_attn_bwd(pqkv, o_g[gi], lse_g[gi], dol[gi], dol[n_grp + gi], P["q_norm"], P["k_norm"], gi, S)
             for gi in range(n_grp)]
    Gs["q_norm"] = dattn[0][3] + dattn[1][3] + dattn[2][3]
    Gs["k_norm"] = dattn[0][4] + dattn[1][4] + dattn[2][4]
    dpqkv = jnp.concatenate([dattn[gi][kind] for kind in range(3) for gi in range(n_grp)], axis=1).astype(bf16)
    dh = [_mm("dh_rkv", dprkv, W["in_rkvT"], "nn"), _mm("dh_lora", dplora, W["in_loraT"], "nn"),
          _mm("dh_gate", dpgate, W["in_gateT"], "nn")]
    mix_drms = (functools.partial(_f_rms_bwd, 4), [*dh, x1, dx2], [P["mix_norm"]], [(D, f32), (D, bf16)], [(1, D)])
    dx1, dx1_b, Gs["mix_norm"] = _mm("dh_qkv", dpqkv, W["in_qkvT"], "nn", epilogue=mix_drms)
    dW_rkv = _mm("dw_rkv", dprkv, h, "tn", out_dtype=GRAD_WIRE)
    dW_lora = _mm("dw_lora", dplora, h, "tn", out_dtype=GRAD_WIRE)
    dW_qkv = _mm("dw_qkv", dpqkv, h, "tn", out_dtype=GRAD_WIRE)
    dW_gate = _mm("dw_gate", dpgate, h, "tn", out_dtype=GRAD_WIRE)
    o1, o2 = LORA_PAD[0], LORA_PAD[0] + LORA_PAD[1]
    G["inT"] = jnp.concatenate([dW_rkv, dW_lora[:LORA_W[0]], dW_lora[o1:o1 + LORA_W[1]], dW_lora[o2:o2 + LORA_W[2]],
                                dW_qkv, dW_gate], axis=0)
    part_mid = hooks["reduce_mid"](G) if "reduce_mid" in hooks else None
    dx0, _, Gs["ffn1_norm"], G["f1_iT"], G["f1_o"], recv_mid = _ffn_bwd(
        "f1", dx1, dx1_b, x0, n1, gate1, up1, act1, P["ffn1_norm"], W["f1_iT"], W["f1_o"], cross=part_mid)
    G["mid"] = (part_mid, recv_mid)
    return loss[0, 0], dx0, G, Gs


def _peer(k):
    x, y, c = lax.axis_index("x"), lax.axis_index("y"), lax.axis_index("c")
    px = 1 - x if k & 4 else x
    py = 1 - y if k & 2 else y
    pc = 1 - c if k & 1 else c
    return (px, py, pc), 4 * px + 2 * py + pc


def _gather_phases(x_ref, out_ref, send_sems, recv_sems, local_sem):
    x, y, c = lax.axis_index("x"), lax.axis_index("y"), lax.axis_index("c")
    me, sibling = (x, y, c), (x, y, 1 - c)
    chips = [(1 - x, y), (x, 1 - y), (1 - x, 1 - y)]

    def slot(px, py, pc):
        return out_ref.at[4 * px + 2 * py + pc]

    def copy(k, block, to, src=None):
        return pltpu.make_async_remote_copy(
            src_ref=slot(*block) if src is None else src, dst_ref=slot(*block), send_sem=send_sems.at[k],
            recv_sem=recv_sems.at[k], device_id=to, device_id_type=MESH)

    def mine():
        return pltpu.make_async_copy(x_ref, slot(*me), local_sem)

    def first():
        return [copy(0, me, sibling, src=x_ref)] + [copy(1 + j, me, (*chip, c), src=x_ref) for j, chip in enumerate(chips)]

    def passed():
        return [copy(4 + j, (*chip, c), sibling) for j, chip in enumerate(chips)]

    def start():
        mine().start()
        for cp in first():
            cp.start()

    def forward():
        for j, (chip, cp) in enumerate(zip(chips, passed())):
            copy(1 + j, (*chip, c), me).wait_recv()
            cp.start()

    def finish():
        copy(0, sibling, me).wait_recv()
        for j, chip in enumerate(chips):
            copy(4 + j, (*chip, 1 - c), me).wait_recv()
        for cp in first() + passed():
            cp.wait_send()
        mine().wait()

    return start, forward, finish


GATHER_SEMS = [pltpu.SemaphoreType.DMA((N_DEV - 1,)), pltpu.SemaphoreType.DMA((N_DEV - 1,)), pltpu.SemaphoreType.DMA(())]


def _all_gather(pack):
    R, C = pack.shape

    def body(x_ref, out_ref, send_sems, recv_sems, local_sem):
        for phase in _gather_phases(x_ref, out_ref, send_sems, recv_sems, local_sem):
            phase()

    return pl.pallas_call(
        body, name="weight_all_gather", out_shape=jax.ShapeDtypeStruct((N_DEV, R, C), pack.dtype),
        in_specs=[pl.BlockSpec(memory_space=pl.ANY)], out_specs=pl.BlockSpec(memory_space=pl.ANY),
        scratch_shapes=GATHER_SEMS,
    )(pack)


def _cross_phases(p_ref, out_ref, send_sems, recv_sems):
    x, y, c = lax.axis_index("x"), lax.axis_index("y"), lax.axis_index("c")

    def copies():
        out = []
        for j, (fx, fy) in enumerate([(1, 0), (0, 1), (1, 1)]):
            px = 1 - x if fx else x
            py = 1 - y if fy else y
            out.append(pltpu.make_async_remote_copy(src_ref=p_ref.at[2 * px + py], dst_ref=out_ref.at[j],
                                                    send_sem=send_sems.at[j], recv_sem=recv_sems.at[j],
                                                    device_id=(px, py, c), device_id_type=MESH))
        return out

    def start():
        for cp in copies():
            cp.start()

    def finish():
        for cp in copies():
            cp.wait()

    return start, finish


CROSS_SEMS = [pltpu.SemaphoreType.DMA((3,)), pltpu.SemaphoreType.DMA((3,))]


def _direct_phases(piece_refs, rows, out_ref, send_sems, recv_sems):
    offs = [sum(rows[:i]) for i in range(len(rows))]

    def copies():
        out = []
        for i, g_ref in enumerate(piece_refs):
            for k in range(1, N_DEV):
                dev, idx = _peer(k)
                out.append(pltpu.make_async_remote_copy(
                    src_ref=g_ref.at[idx], dst_ref=out_ref.at[k - 1, pl.ds(offs[i], rows[i])],
                    send_sem=send_sems.at[i * (N_DEV - 1) + k - 1], recv_sem=recv_sems.at[i * (N_DEV - 1) + k - 1],
                    device_id=dev, device_id_type=MESH))
        return out

    def start():
        for cp in copies():
            cp.start()

    def finish():
        for cp in copies():
            cp.wait()

    return start, finish


def _sum_direct(pieces, recv, me, tag):
    n = len(pieces)
    C = pieces[0].shape[2]
    nblk = [p.shape[1] // PACK_BLOCK for p in pieces]
    lo = [sum(nblk[:i]) for i in range(n)]
    R = sum(nblk) * PACK_BLOCK

    def body(me_ref, *refs):
        g_refs, r_ref, o_ref = refs[:n], refs[n], refs[n + 1]
        rb = pl.program_id(0)
        for i in range(n):
            @pl.when(jnp.logical_and(rb >= lo[i], rb < lo[i] + nblk[i]))
            def _(g_ref=g_refs[i]):
                acc = g_ref[...].astype(f32)
                for k in range(N_DEV - 1):
                    acc = acc + r_ref[k].astype(f32)
                o_ref[...] = acc

    def piece_spec(i):
        return pl.BlockSpec((None, PACK_BLOCK, C), lambda rb, me_ref: (me_ref[0], jnp.clip(rb - lo[i], 0, nblk[i] - 1), 0))

    return pl.pallas_call(
        body, name=f"grad_sum_{tag}",
        grid_spec=pltpu.PrefetchScalarGridSpec(
            num_scalar_prefetch=1, grid=(R // PACK_BLOCK,),
            in_specs=[piece_spec(i) for i in range(n)] + [pl.BlockSpec((N_DEV - 1, PACK_BLOCK, C), lambda rb, me_ref: (0, rb, 0))],
            out_specs=pl.BlockSpec((PACK_BLOCK, C), lambda rb, me_ref: (rb, 0))),
        out_shape=jax.ShapeDtypeStruct((R, C), f32),
        compiler_params=_cparams(("arbitrary",)),
    )(me, *pieces, recv)


N_CHIP = 4


def _grad_pair(pieces, tag):
    n = len(pieces)
    C = pieces[0].shape[2]
    rows = [p.shape[1] for p in pieces]
    offs = [sum(rows[:i]) for i in range(n)]
    R = sum(rows)

    def body(*refs):
        g_refs, (other_ref, send_sems, recv_sems) = refs[:n], refs[n:]
        x, y, c = lax.axis_index("x"), lax.axis_index("y"), lax.axis_index("c")
        copies = []
        for i, g_ref in enumerate(g_refs):
            for k in range(N_CHIP):
                cp = pltpu.make_async_remote_copy(
                    src_ref=g_ref.at[4 * (k // 2) + 2 * (k % 2) + 1 - c], dst_ref=other_ref.at[k, pl.ds(offs[i], rows[i])],
                    send_sem=send_sems.at[i * N_CHIP + k], recv_sem=recv_sems.at[i * N_CHIP + k],
                    device_id=(x, y, 1 - c), device_id_type=MESH)
                cp.start()
                copies.append(cp)
        for cp in copies:
            cp.wait()

    return pl.pallas_call(
        body, name=f"grad_pair_{tag}", out_shape=jax.ShapeDtypeStruct((N_CHIP, R, C), pieces[0].dtype),
        in_specs=[pl.BlockSpec(memory_space=pl.ANY)] * n, out_specs=pl.BlockSpec(memory_space=pl.ANY),
        scratch_shapes=[pltpu.SemaphoreType.DMA((n * N_CHIP,))] * 2,
    )(*pieces)


def _pair_add(pieces, other, c, tag):
    n = len(pieces)
    C = pieces[0].shape[2]
    nblk = [p.shape[1] // PACK_BLOCK for p in pieces]
    lo = [sum(nblk[:i]) for i in range(n)]
    R = sum(nblk) * PACK_BLOCK

    def body(c_ref, *refs):
        g_refs, o_ref, out_ref = refs[:n], refs[n], refs[n + 1]
        rb = pl.program_id(1)
        for i in range(n):
            @pl.when(jnp.logical_and(rb >= lo[i], rb < lo[i] + nblk[i]))
            def _(g_ref=g_refs[i]):
                out_ref[...] = (g_ref[...].astype(f32) + o_ref[...].astype(f32)).astype(out_ref.dtype)

    def piece_spec(i):
        return pl.BlockSpec((1, None, PACK_BLOCK, C),
                            lambda k, rb, c_ref: (k, c_ref[0], jnp.clip(rb - lo[i], 0, nblk[i] - 1), 0))

    blk = pl.BlockSpec((1, PACK_BLOCK, C), lambda k, rb, c_ref: (k, rb, 0))
    return pl.pallas_call(
        body, name=f"pair_add_{tag}",
        grid_spec=pltpu.PrefetchScalarGridSpec(
            num_scalar_prefetch=1, grid=(N_CHIP, R // PACK_BLOCK),
            in_specs=[piece_spec(i) for i in range(n)] + [blk], out_specs=blk),
        out_shape=jax.ShapeDtypeStruct((N_CHIP, R, C), other.dtype),
        compiler_params=_cparams(("arbitrary", "arbitrary")),
    )(c, *[p.reshape(N_CHIP, 2, p.shape[1], C) for p in pieces], other)


def _grad_cross(part):
    _, R, C = part.shape

    def body(p_ref, out_ref, send_sems, recv_sems):
        for phase in _cross_phases(p_ref, out_ref, send_sems, recv_sems):
            phase()

    return pl.pallas_call(
        body, name="grad_cross", out_shape=jax.ShapeDtypeStruct((3, R, C), part.dtype),
        in_specs=[pl.BlockSpec(memory_space=pl.ANY)], out_specs=pl.BlockSpec(memory_space=pl.ANY),
        scratch_shapes=CROSS_SEMS,
    )(part)


def _grad_sum(part, recv, my_chip, tr, tag):
    _, R, C = part.shape

    def body(chip_ref, p_ref, r_ref, o_ref):
        acc = p_ref[0].astype(f32)
        for j in range(3):
            acc = acc + r_ref[j].astype(f32)
        o_ref[...] = acc

    return pl.pallas_call(
        body, name=f"grad_sum_{tag}",
        grid_spec=pltpu.PrefetchScalarGridSpec(
            num_scalar_prefetch=1, grid=(R // tr,),
            in_specs=[pl.BlockSpec((1, tr, C), lambda i, chip_ref: (chip_ref[0], i, 0)),
                      pl.BlockSpec((3, tr, C), lambda i, chip_ref: (0, i, 0))],
            out_specs=pl.BlockSpec((tr, C), lambda i, chip_ref: (i, 0))),
        out_shape=jax.ShapeDtypeStruct((R, C), f32),
        compiler_params=_cparams(("arbitrary",)),
    )(my_chip, part, recv)


def _small_all_reduce(small):
    R, C = small.shape

    def body(x_ref, o_ref, buf, send_sems, recv_sems):
        _, me = _peer(0)
        buf[me] = x_ref[...]
        sends = []
        for k in range(1, N_DEV):
            dev, _ = _peer(k)
            cp = pltpu.make_async_remote_copy(src_ref=x_ref, dst_ref=buf.at[me], send_sem=send_sems.at[k - 1],
                                              recv_sem=recv_sems.at[k - 1], device_id=dev, device_id_type=MESH)
            cp.start()
            sends.append(cp)
        for k in range(1, N_DEV):
            dev, idx = _peer(k)
            pltpu.make_async_remote_copy(src_ref=x_ref, dst_ref=buf.at[idx], send_sem=send_sems.at[k - 1],
                                         recv_sem=recv_sems.at[k - 1], device_id=dev, device_id_type=MESH).wait_recv()
        for cp in sends:
            cp.wait_send()
        acc = buf[0]
        for i in range(1, N_DEV):
            acc = acc + buf[i]
        o_ref[...] = acc

    return pl.pallas_call(
        body, name="small_all_reduce", out_shape=jax.ShapeDtypeStruct((R, C), f32),
        in_specs=[pl.BlockSpec(memory_space=pltpu.VMEM)], out_specs=pl.BlockSpec(memory_space=pltpu.VMEM),
        scratch_shapes=[pltpu.VMEM((N_DEV, R, C), f32), pltpu.SemaphoreType.DMA((N_DEV - 1,)),
                        pltpu.SemaphoreType.DMA((N_DEV - 1,))],
    )(small)


_LORA = (("rwkv_w2", True), ("rwkv_a2", True), ("rwkv_g2", True))
_GROUPS_FIRST = ((("ffn1_w_in", True),),)
_GROUPS_MID = ((("ffn1_w_out", False),), _LORA)
_GROUPS_IN = ((("w_in", True),),)
_GROUPS_LATE = ((("w_proj_rwkv", False),), (("w_proj_attn", True),), (("w_out", False),),
                (("ffn2_w_in", True),), (("ffn2_w_out", False),))
_GRADS_MID = ((("w_in", True),), _LORA)
_GRADS_LAST = ((("ffn1_w_in", True),), (("ffn1_w_out", False),))
_BIG = tuple(item for group in _GROUPS_FIRST + _GROUPS_MID + _GROUPS_IN + _GROUPS_LATE for item in group)
_SMALL = ("ffn1_norm", "mix_norm", "b_gate", "rwkv_mu", "rwkv_w0", "rwkv_a0", "rwkv_k_k", "rwkv_k_a", "rwkv_r_k",
          "rwkv_ln_w", "rwkv_ln_b", "attn_q_norm", "attn_k_norm", "ffn2_norm")


def _pack_layout(like, groups):
    items, spans, off = {}, [], 0
    for group in groups:
        start = off
        for name, _ in group:
            shp = like[name].shape
            n = shp[0] * shp[1] // D
            items[name] = (off, n)
            off += n
        off = -(-off // PACK_BLOCK) * PACK_BLOCK
        spans.append((start, off - start))
    return items, spans, off


def _pack_big(shards, groups):
    items, _, rows = _pack_layout(shards, groups)
    parts, at = [], 0
    for group in groups:
        for name, tr in group:
            off, n = items[name]
            t = shards[name]
            if off > at:
                parts.append(jnp.zeros((off - at, D), t.dtype))
            parts.append((t.T if tr else t).reshape(n, D))
            at = off + n
    if rows > at:
        parts.append(jnp.zeros((rows - at, D), parts[0].dtype))
    return jnp.concatenate(parts, axis=0)


def _unpack_big(pack, like, groups):
    items, _, _ = _pack_layout(like, groups)
    out = {}
    for group in groups:
        for name, tr in group:
            off, n = items[name]
            shp = like[name].shape
            t = pack[off:off + n]
            out[name] = t.reshape(shp[1], shp[0]).T if tr else t.reshape(shp)
    return out


def _unpack_gathered(gathered, like, groups):
    items, _, _ = _pack_layout(like, groups)
    full = {}
    for group in groups:
        for name, tr in group:
            shp = like[name].shape
            off, rows = items[name]
            r_loc, c_loc = (shp[1], shp[0]) if tr else shp
            full[name] = gathered[:, off:off + rows].reshape(N_DEV * r_loc, c_loc)
    return full


def _grad_pieces(g_full, like, groups):
    items, spans, _ = _pack_layout(like, groups)
    pieces = []
    for group, (_, rows_pad) in zip(groups, spans):
        parts = [g_full[n].astype(GRAD_WIRE).reshape(N_DEV, items[n][1], D) for n, _ in group]
        piece = parts[0] if len(parts) == 1 else jnp.concatenate(parts, axis=1)
        if rows_pad > piece.shape[1]:
            piece = jnp.pad(piece, ((0, 0), (0, rows_pad - piece.shape[1]), (0, 0)))
        pieces.append(piece)
    return pieces


def _small_rows(name, t):
    flat = t.reshape(-1)
    pad = (-flat.shape[0]) % D
    return jnp.pad(flat, (0, pad)).reshape(-1, D)


def _pack_small(vals):
    parts = [_small_rows(n, vals[n]) for n in _SMALL]
    used = sum(p.shape[0] for p in parts)
    parts.append(jnp.zeros((SMALL_ROWS - used, D), f32))
    return jnp.concatenate(parts, axis=0)


def _unpack_small(pack, like):
    out, off = {}, 0
    for n in _SMALL:
        size = like[n].size
        rows = -(-size // D)
        out[n] = pack[off:off + rows].reshape(-1)[:size].reshape(like[n].shape)
        off += rows
    return out


def _build_W_mid(full):
    dt = full["rwkv_w2"].dtype
    z64, z96 = jnp.zeros((64, D), dt), jnp.zeros((96, D), dt)
    return {
        "f1_o": full["ffn1_w_out"],
        "w2p": jnp.concatenate([full["rwkv_w2"].T, z64], axis=0),
        "a2p": jnp.concatenate([full["rwkv_a2"].T, z64], axis=0),
        "g2p": jnp.concatenate([full["rwkv_g2"].T, z96], axis=0),
    }


def _build_W_in(full):
    inT = full["w_in"]
    z64, z96 = jnp.zeros((64, D), inT.dtype), jnp.zeros((96, D), inT.dtype)
    return {
        "in_rkvT": inT[:3 * D],
        "in_loraT": jnp.concatenate([inT[3072:3136], z64, inT[3136:3200], z64, inT[3200:3360], z96], axis=0),
        "in_qkvT": inT[3360:3360 + 3 * ATTN_W], "in_gateT": inT[3360 + 3 * ATTN_W:],
    }


def _build_W_late(full):
    return {"pr": full["w_proj_rwkv"], "paT": full["w_proj_attn"], "out": full["w_out"],
            "f2_iT": full["ffn2_w_in"], "f2_o": full["ffn2_w_out"]}


def _build_W_first(full):
    return {"f1_iT": full["ffn1_w_in"]}


def _build_W(full):
    return {**_build_W_first(full), **_build_W_mid(full), **_build_W_in(full), **_build_W_late(full)}


_G_NAMES = {"ffn1_w_in": "f1_iT", "ffn1_w_out": "f1_o", "w_in": "inT", "rwkv_w2": "w2T", "rwkv_a2": "a2T",
            "rwkv_g2": "g2T", "w_proj_rwkv": "pr", "w_proj_attn": "paT", "w_out": "out", "ffn2_w_in": "f2_iT",
            "ffn2_w_out": "f2_o"}


def _named_grads(G, groups):
    return {n: G[_G_NAMES[n]] for group in groups for n, _ in group}


def _reduce_start(G, like, groups, my_c, tag):
    pieces = _grad_pieces(_named_grads(G, groups), like, groups)
    return _pair_add(pieces, _grad_pair(pieces, tag), my_c, tag)


def _build_P(Wl):
    mu = Wl["rwkv_mu"]
    z64f, z96f = jnp.zeros((1, 64), f32), jnp.zeros((1, 96), f32)
    return {
        "ffn1_norm": Wl["ffn1_norm"][None], "mix_norm": Wl["mix_norm"][None], "ffn2_norm": Wl["ffn2_norm"][None],
        "b_gate": Wl["b_gate"][None], "mu_rkv": mu[None, :3 * D],
        "mu_lora": jnp.concatenate([mu[None, 3072:3136], z64f, mu[None, 3136:3200], z64f, mu[None, 3200:3360], z96f], axis=1),
        "w0": Wl["rwkv_w0"][None], "a0": Wl["rwkv_a0"][None], "k_k": Wl["rwkv_k_k"][None], "k_a": Wl["rwkv_k_a"][None],
        "r_k": Wl["rwkv_r_k"].reshape(1, D), "ln_w": Wl["rwkv_ln_w"][None], "ln_b": Wl["rwkv_ln_b"][None],
        "q_norm": Wl["attn_q_norm"][None], "k_norm": Wl["attn_k_norm"][None],
    }


def kernel(x, ffn1_norm, ffn1_w_in, ffn1_w_out, mix_norm, w_in, b_gate, rwkv_mu, rwkv_w0, rwkv_w2, rwkv_a0, rwkv_a2, rwkv_g2, rwkv_k_k, rwkv_k_a, rwkv_r_k, rwkv_ln_w, rwkv_ln_b, attn_q_norm, attn_k_norm, w_proj_rwkv, w_proj_attn, w_out, ffn2_norm, ffn2_w_in, ffn2_w_out, loss_target, m_ffn1_norm, m_ffn1_w_in, m_ffn1_w_out, m_mix_norm, m_w_in, m_b_gate, m_rwkv_mu, m_rwkv_w0, m_rwkv_w2, m_rwkv_a0, m_rwkv_a2, m_rwkv_g2, m_rwkv_k_k, m_rwkv_k_a, m_rwkv_r_k, m_rwkv_ln_w, m_rwkv_ln_b, m_attn_q_norm, m_attn_k_norm, m_w_proj_rwkv, m_w_proj_attn, m_w_out, m_ffn2_norm, m_ffn2_w_in, m_ffn2_w_out, v_ffn1_norm, v_ffn1_w_in, v_ffn1_w_out, v_mix_norm, v_w_in, v_b_gate, v_rwkv_mu, v_rwkv_w0, v_rwkv_w2, v_rwkv_a0, v_rwkv_a2, v_rwkv_g2, v_rwkv_k_k, v_rwkv_k_a, v_rwkv_r_k, v_rwkv_ln_w, v_rwkv_ln_b, v_attn_q_norm, v_attn_k_norm, v_w_proj_rwkv, v_w_proj_attn, v_w_out, v_ffn2_norm, v_ffn2_w_in, v_ffn2_w_out):
    names = ("ffn1_norm", "ffn1_w_in", "ffn1_w_out", "mix_norm", "w_in", "b_gate", "rwkv_mu", "rwkv_w0", "rwkv_w2",
             "rwkv_a0", "rwkv_a2", "rwkv_g2", "rwkv_k_k", "rwkv_k_a", "rwkv_r_k", "rwkv_ln_w", "rwkv_ln_b",
             "attn_q_norm", "attn_k_norm", "w_proj_rwkv", "w_proj_attn", "w_out", "ffn2_norm", "ffn2_w_in", "ffn2_w_out")
    w_all = (ffn1_norm, ffn1_w_in, ffn1_w_out, mix_norm, w_in, b_gate, rwkv_mu, rwkv_w0, rwkv_w2, rwkv_a0, rwkv_a2,
             rwkv_g2, rwkv_k_k, rwkv_k_a, rwkv_r_k, rwkv_ln_w, rwkv_ln_b, attn_q_norm, attn_k_norm, w_proj_rwkv,
             w_proj_attn, w_out, ffn2_norm, ffn2_w_in, ffn2_w_out)
    m_all = (m_ffn1_norm, m_ffn1_w_in, m_ffn1_w_out, m_mix_norm, m_w_in, m_b_gate, m_rwkv_mu, m_rwkv_w0, m_rwkv_w2,
             m_rwkv_a0, m_rwkv_a2, m_rwkv_g2, m_rwkv_k_k, m_rwkv_k_a, m_rwkv_r_k, m_rwkv_ln_w, m_rwkv_ln_b,
             m_attn_q_norm, m_attn_k_norm, m_w_proj_rwkv, m_w_proj_attn, m_w_out, m_ffn2_norm, m_ffn2_w_in, m_ffn2_w_out)
    v_all = (v_ffn1_norm, v_ffn1_w_in, v_ffn1_w_out, v_mix_norm, v_w_in, v_b_gate, v_rwkv_mu, v_rwkv_w0, v_rwkv_w2,
             v_rwkv_a0, v_rwkv_a2, v_rwkv_g2, v_rwkv_k_k, v_rwkv_k_a, v_rwkv_r_k, v_rwkv_ln_w, v_rwkv_ln_b,
             v_attn_q_norm, v_attn_k_norm, v_w_proj_rwkv, v_w_proj_attn, v_w_out, v_ffn2_norm, v_ffn2_w_in, v_ffn2_w_out)
    Wl = {n: t[0] for n, t in zip(names, w_all)}
    Ml = {n: t[0] for n, t in zip(names, m_all)}
    Vl = {n: t[0] for n, t in zip(names, v_all)}
    big = [n for n, _ in _BIG]

    my_c = lax.axis_index("c").astype(jnp.int32).reshape(1)
    my_chip = (2 * lax.axis_index("x") + lax.axis_index("y")).astype(jnp.int32).reshape(1)

    def pack(groups):
        return _pack_big(Wl, groups).astype(bf16)

    gathered = _all_gather(pack(_GROUPS_FIRST))
    W, P = _build_W_first(_unpack_gathered(gathered, Wl, _GROUPS_FIRST)), _build_P(Wl)
    hooks = {"gather_mid": (pack(_GROUPS_MID), lambda g: _build_W_mid(_unpack_gathered(g, Wl, _GROUPS_MID))),
             "gather_in": (pack(_GROUPS_IN), lambda g: _build_W_in(_unpack_gathered(g, Wl, _GROUPS_IN))),
             "gather_late": (pack(_GROUPS_LATE), lambda g: _build_W_late(_unpack_gathered(g, Wl, _GROUPS_LATE))),
             "reduce_mid": lambda G: _reduce_start(G, Wl, _GRADS_MID, my_c, "mid"),
             "reduce_late": lambda G: _grad_pieces(_named_grads(G, _GROUPS_LATE), Wl, _GROUPS_LATE)}

    loss_local, dx0, G, Gs = _local_step(x[0], loss_target[0], W, P, hooks)

    part_last = _reduce_start(G, Wl, _GRADS_LAST, my_c, "last")
    g_big = _unpack_big(_grad_sum(part_last, _grad_cross(part_last), my_chip, 128, "last"), Wl, _GRADS_LAST)
    g_big.update(_unpack_big(_grad_sum(*G["mid"], my_chip, 128, "mid"), Wl, _GRADS_MID))
    me = (4 * lax.axis_index("x") + 2 * lax.axis_index("y") + lax.axis_index("c")).astype(jnp.int32).reshape(1)
    g_big.update(_unpack_big(_sum_direct(*G["late"], me, "late"), Wl, _GROUPS_LATE))

    mu_g = Gs["mu_rkv"], Gs["mu_lora"]
    o1, o2 = LORA_PAD[0], LORA_PAD[0] + LORA_PAD[1]
    g_small_local = {
        "ffn1_norm": Gs["ffn1_norm"], "mix_norm": Gs["mix_norm"], "b_gate": Gs["b_gate"],
        "rwkv_mu": jnp.concatenate([mu_g[0], mu_g[1][:, :64], mu_g[1][:, o1:o1 + 64], mu_g[1][:, o2:o2 + 160]], axis=1),
        "rwkv_w0": Gs["w0"], "rwkv_a0": Gs["a0"], "rwkv_k_k": Gs["k_k"], "rwkv_k_a": Gs["k_a"], "rwkv_r_k": Gs["r_k"],
        "rwkv_ln_w": Gs["ln_w"], "rwkv_ln_b": Gs["ln_b"], "attn_q_norm": Gs["q_norm"], "attn_k_norm": Gs["k_norm"],
        "ffn2_norm": Gs["ffn2_norm"]}
    gs_pack = _small_all_reduce(_pack_small(g_small_local))

    out_g, out_d, out_m, out_v = dict(g_big), {}, {}, {}
    for n in big:
        cols = Wl[n].shape[1]
        out_d[n], out_m[n], out_v[n] = _rowwise(f"adamw_{n}", _f_adamw, [Wl[n], g_big[n], Ml[n], Vl[n]], [],
                                                 [(cols, f32)] * 3)
    ds_pack, ms_pack, vs_pack = _rowwise(
        "adamw_small", _f_adamw, [_pack_small(Wl), gs_pack, _pack_small(Ml), _pack_small(Vl)], [], [(D, f32)] * 3)
    for out, pack in ((out_g, gs_pack), (out_d, ds_pack), (out_m, ms_pack), (out_v, vs_pack)):
        out.update(_unpack_small(pack, Wl))

    loss = lax.psum(loss_local, ("x", "y", "c"))
    return (loss, dx0[None], *[out_g[n][None] for n in names], *[out_d[n][None] for n in names],
            *[out_m[n][None] for n in names], *[out_v[n][None] for n in names])
```

```python
import functools
import math

import jax
import jax.numpy as jnp
from jax import lax
from jax.experimental import pallas as pl
from jax.experimental.pallas import tpu as pltpu

f32 = jnp.float32
bf16 = jnp.bfloat16
MESH = pl.DeviceIdType.MESH

N_DEV = 8
D = 1024
D_FF = 2816
HEAD = 64
RW_HEADS = 16
ATTN_PAIRS = ((128, 1), (512, 4), (2048, 16))
ATTN_BLK = 128
ATTN_W = 768
LORA_PAD = (128, 128, 256)
LORA_W = (64, 64, 160)
GN_EPS = 64e-5
RMS_EPS = 1e-6
NEG_INF = -1e30
WKV_T = 64
WKV_SUB = 2
GRAD_WIRE = bf16
PACK_BLOCK = 128
SMALL_ROWS = 24
VMEM_LIMIT = 56 * 1024 * 1024

ADAM_LR, ADAM_B1, ADAM_B2, ADAM_EPS, ADAM_WD, ADAM_STEP = 0.001, 0.9, 0.999, 1e-08, 0.01, 10


def _cparams(sem):
    return pltpu.CompilerParams(dimension_semantics=sem, vmem_limit_bytes=VMEM_LIMIT)


HALO = 8


def _rowwise(name, fn, rows, params, outs, accs=(), tm=256, halos=(), carries=(), reverse=False):
    S = rows[0].shape[0]
    tm = min(tm, S)
    while S % tm:
        tm -= 8
    nb = S // tm
    n_in = len(rows) + len(params) + len(halos)
    n_out = len(outs)
    n_acc = len(accs)
    n_car = len(carries)

    def blk_of(i):
        return nb - 1 - i if reverse else i

    def body(*refs):
        step = pl.program_id(0)
        carry_refs = refs[n_in + n_out + n_acc:]
        if n_car:
            @pl.when(step == 0)
            def _():
                for c_ref in carry_refs:
                    c_ref[...] = jnp.zeros(c_ref.shape, f32)
        args = [r[...] for r in refs[:n_in]] + [c[...] for c in carry_refs]
        res = fn(*args, blk=blk_of(step)) if (halos or carries) else fn(*args)
        if not isinstance(res, (tuple, list)):
            res = (res,)
        out_refs = refs[n_in:n_in + n_out + n_acc]
        for j in range(n_out):
            out_refs[j][...] = res[j].astype(out_refs[j].dtype)
        if n_acc:
            @pl.when(step == 0)
            def _():
                for j in range(n_acc):
                    out_refs[n_out + j][...] = jnp.zeros(out_refs[n_out + j].shape, f32)
            for j in range(n_acc):
                out_refs[n_out + j][...] += res[n_out + j]
        for j in range(n_car):
            carry_refs[j][...] = res[n_out + n_acc + j]

    in_specs = [pl.BlockSpec((tm, a.shape[1]), lambda i: (blk_of(i), 0)) for a in rows]
    in_specs += [pl.BlockSpec(p.shape, lambda i, nd=p.ndim: (0,) * nd) for p in params]
    in_specs += [pl.BlockSpec((HALO, rows[h].shape[1]), lambda i: (jnp.maximum(blk_of(i) * (tm // HALO) - 1, 0), 0))
                 for h in halos]
    out_specs = [pl.BlockSpec((tm, w), lambda i: (blk_of(i), 0)) for w, _ in outs]
    out_specs += [pl.BlockSpec(s, lambda i: (0, 0)) for s in accs]
    out_shape = [jax.ShapeDtypeStruct((S, w), dt) for w, dt in outs]
    out_shape += [jax.ShapeDtypeStruct(s, f32) for s in accs]
    res = pl.pallas_call(
        body, name=name, grid=(nb,), in_specs=in_specs, out_specs=out_specs, out_shape=out_shape,
        scratch_shapes=[pltpu.VMEM(s, f32) for s in carries],
        compiler_params=_cparams(("arbitrary",)),
    )(*rows, *params, *[rows[h] for h in halos])
    return res


MM_VMEM_BUDGET = 40 * 1024 * 1024
MM_STEP_US = 0.35
MM_FLOPS_PER_US = 9.0e8
MM_HBM_BYTES_PER_US = 3.0e6


def _tile_options(n, cap):
    opts = [d for d in range(128, min(n, cap) + 1, 128) if n % d == 0]
    return opts or [n]


def _mm_tiles(M, N, K, sa, sb, so, whole_rows=False):
    best, best_cost = None, None
    for tm in _tile_options(M, 512 if whole_rows else 2048):
        for tn in ([N] if whole_rows else _tile_options(N, 2048)):
            for tk in _tile_options(K, 4096):
                vmem = 2 * (tm * tk * sa + tk * tn * sb) + 2 * tm * tn * so + (tm * tn * 4 if tk < K else 0)
                if vmem > MM_VMEM_BUDGET:
                    continue
                steps = (M // tm) * (N // tn) * (K // tk)
                traffic = M * K * sa * (N // tn) + K * N * sb * (M // tm) + M * N * so
                cost = (max(2.0 * M * N * K / MM_FLOPS_PER_US, traffic / MM_HBM_BYTES_PER_US) + steps * MM_STEP_US
                        + (tm * tk * sa + tk * tn * sb) / MM_HBM_BYTES_PER_US)
                if best_cost is None or cost < best_cost:
                    best, best_cost = (tm, tn, tk), cost
    return best


def _mm(name, a, b, mode, out_dtype=f32, scale=None, gather=None, cross=None, epilogue=None):
    halves = a.ndim == 3
    sizes = (jnp.dtype(a.dtype).itemsize, jnp.dtype(b.dtype).itemsize, jnp.dtype(out_dtype).itemsize)
    whole = epilogue is not None
    if mode == "nn":
        (M, K), N = (a.shape[1], 2 * a.shape[2]) if halves else a.shape, b.shape[1]
        tm, tn, tk = _mm_tiles(M, N, K // 2 if halves else K, *sizes, whole_rows=whole)
    elif mode == "nt":
        (M, K), N = a.shape, b.shape[0]
        tm, tn, tk = _mm_tiles(M, N, K, *sizes, whole_rows=whole)
    else:
        (K, M), N = (a.shape[1], 2 * a.shape[2]) if halves else a.shape, b.shape[1]
        tm, tn, tk = _mm_tiles(M // 2 if halves else M, N, K, *sizes, whole_rows=whole)
    nk = K // tk
    if mode == "nn":
        per = K // 2 // tk
        a_spec = (pl.BlockSpec((None, tm, tk), lambda i, j, k: (k // per, i, k % per)) if halves
                  else pl.BlockSpec((tm, tk), lambda i, j, k: (i, k)))
        b_spec = pl.BlockSpec((tk, tn), lambda i, j, k: (k, j))
        dims = (((1,), (0,)), ((), ()))
    elif mode == "nt":
        a_spec = pl.BlockSpec((tm, tk), lambda i, j, k: (i, k))
        b_spec = pl.BlockSpec((tn, tk), lambda i, j, k: (j, k))
        dims = (((1,), (1,)), ((), ()))
    else:
        per = M // 2 // tm
        a_spec = (pl.BlockSpec((None, tk, tm), lambda i, j, k: (i // per, k, i % per)) if halves
                  else pl.BlockSpec((tk, tm), lambda i, j, k: (k, i)))
        b_spec = pl.BlockSpec((tk, tn), lambda i, j, k: (k, j))
        dims = (((0,), (0,)), ((), ()))

    hosted = gather if gather is not None else cross
    grid = (M // tm, N // tn, nk)
    steps = grid[0] * grid[1] * grid[2]
    ep_rows, ep_params, ep_outs, ep_accs = ([], [], [], []) if epilogue is None else epilogue[1:]
    n_ep_in, n_ep_out = len(ep_rows) + len(ep_params), len(ep_outs) + len(ep_accs)
    assert epilogue is None or tn == N

    def body(a_ref, b_ref, *rest):
        rest = list(rest)
        src_ref = rest.pop(0) if hosted is not None else None
        ep_in, rest = rest[:n_ep_in], rest[n_ep_in:]
        if epilogue is None:
            o_ref = rest.pop(0)
        else:
            out_refs, rest = rest[:n_ep_out], rest[n_ep_out:]
        dst_ref = rest.pop(0) if hosted is not None else None
        scratch = rest
        step = (pl.program_id(0) * grid[1] + pl.program_id(1)) * grid[2] + pl.program_id(2)
        if hosted is not None:
            n_sem = len(GATHER_SEMS if gather is not None else CROSS_SEMS)
            sems, scratch = scratch[len(scratch) - n_sem:], scratch[:len(scratch) - n_sem]
            if gather is not None:
                start, forward, done = _gather_phases(src_ref, dst_ref, *sems)
                pl.when(step == steps // 2)(forward)
            else:
                start, done = _cross_phases(src_ref, dst_ref, *sems)
            pl.when(step == 0)(start)
        part = lax.dot_general(a_ref[...].astype(bf16), b_ref[...].astype(bf16), dims,
                               preferred_element_type=f32)

        def finish(acc):
            if epilogue is None:
                o_ref[...] = (acc if scale is None else acc * scale).astype(o_ref.dtype)
                return
            res = epilogue[0](acc, *[r[...] for r in ep_in])
            for j in range(len(ep_outs)):
                out_refs[j][...] = res[j].astype(out_refs[j].dtype)
            for j in range(len(ep_accs)):
                acc_out = out_refs[len(ep_outs) + j]

                @pl.when(step == nk - 1)
                def _(acc_out=acc_out):
                    acc_out[...] = jnp.zeros(acc_out.shape, f32)
                acc_out[...] += res[len(ep_outs) + j]

        if nk == 1:
            finish(part)
        else:
            acc_ref = scratch[0]
            k = pl.program_id(2)

            @pl.when(k == 0)
            def _():
                acc_ref[...] = part

            @pl.when(k > 0)
            def _():
                acc_ref[...] += part

            @pl.when(k == nk - 1)
            def _():
                finish(acc_ref[...])
        if hosted is not None:
            pl.when(step == steps - 1)(done)

    hbm = pl.BlockSpec(memory_space=pl.ANY)
    in_specs = [a_spec, b_spec] + [hbm] * (hosted is not None)
    in_specs += [pl.BlockSpec((tm, r.shape[1]), lambda i, j, k: (i, 0)) for r in ep_rows]
    in_specs += [pl.BlockSpec(p.shape, lambda i, j, k: (0, 0)) for p in ep_params]
    if epilogue is None:
        out_specs = [pl.BlockSpec((tm, tn), lambda i, j, k: (i, j))]
        out_shape = [jax.ShapeDtypeStruct((M, N), out_dtype)]
    else:
        out_specs = [pl.BlockSpec((tm, w), lambda i, j, k: (i, 0)) for w, _ in ep_outs]
        out_specs += [pl.BlockSpec(s, lambda i, j, k: (0, 0)) for s in ep_accs]
        out_shape = [jax.ShapeDtypeStruct((M, w), dt) for w, dt in ep_outs]
        out_shape += [jax.ShapeDtypeStruct(s, f32) for s in ep_accs]
    scratch_shapes = [] if nk == 1 else [pltpu.VMEM((tm, tn), f32)]
    if gather is not None:
        out_specs.append(hbm)
        out_shape.append(jax.ShapeDtypeStruct((N_DEV,) + gather.shape, gather.dtype))
        scratch_shapes = scratch_shapes + GATHER_SEMS
    elif cross is not None:
        out_specs.append(hbm)
        out_shape.append(jax.ShapeDtypeStruct((3,) + cross.shape[1:], cross.dtype))
        scratch_shapes = scratch_shapes + CROSS_SEMS
    sequential = hosted is not None or ep_accs
    res = pl.pallas_call(
        body, name=name, grid=grid, in_specs=in_specs,
        out_specs=out_specs, out_shape=out_shape, scratch_shapes=scratch_shapes,
        compiler_params=_cparams(("arbitrary",) * 3 if sequential else ("parallel", "parallel", "arbitrary")),
    )(a, b, *([hosted] if hosted is not None else []), *ep_rows, *ep_params)
    return res[0] if (hosted is None and epilogue is None) else res


FFN_TM, FFN_TN = 512, 1408
FFN_SAVE = bf16


def _ffn_up(name, n, WiT, gather=None):
    S = n.shape[0]
    grid = (S // FFN_TM, D_FF // FFN_TN)
    steps = grid[0] * grid[1]

    def body(n_ref, wg_ref, wu_ref, *rest):
        if gather is None:
            g_ref, u_ref, act_ref = rest
        else:
            src_ref, g_ref, u_ref, act_ref, dst_ref, *sems = rest
            step = pl.program_id(0) * grid[1] + pl.program_id(1)
            start, forward, done = _gather_phases(src_ref, dst_ref, *sems)
            pl.when(step == 0)(start)
            pl.when(step == steps // 2)(forward)
        x = n_ref[...]
        gate = lax.dot_general(x, wg_ref[...], _NT, preferred_element_type=f32)
        up = lax.dot_general(x, wu_ref[...], _NT, preferred_element_type=f32)
        g_ref[...] = gate.astype(g_ref.dtype)
        u_ref[...] = up.astype(u_ref.dtype)
        act_ref[...] = (gate * jax.nn.sigmoid(gate) * up).astype(act_ref.dtype)
        if gather is not None:
            pl.when(step == steps - 1)(done)

    hbm = pl.BlockSpec(memory_space=pl.ANY)
    tile = pl.BlockSpec((FFN_TM, FFN_TN), lambda i, j: (i, j))
    in_specs = [pl.BlockSpec((FFN_TM, D), lambda i, j: (i, 0)), pl.BlockSpec((FFN_TN, D), lambda i, j: (j, 0)),
                pl.BlockSpec((FFN_TN, D), lambda i, j: (j + D_FF // FFN_TN, 0))]
    out_specs = [tile, tile, tile]
    out_shape = [jax.ShapeDtypeStruct((S, D_FF), FFN_SAVE), jax.ShapeDtypeStruct((S, D_FF), FFN_SAVE),
                 jax.ShapeDtypeStruct((S, D_FF), bf16)]
    if gather is not None:
        in_specs.append(hbm)
        out_specs.append(hbm)
        out_shape.append(jax.ShapeDtypeStruct((N_DEV,) + gather.shape, gather.dtype))
    return pl.pallas_call(
        body, name=name, grid=grid, in_specs=in_specs, out_specs=out_specs, out_shape=out_shape,
        scratch_shapes=GATHER_SEMS if gather is not None else [],
        compiler_params=_cparams(("arbitrary", "arbitrary")),
    )(n, WiT, WiT, *([gather] if gather is not None else []))


def _ffn_dact(name, dy, Wo, gate, up):
    S = dy.shape[0]

    def body(dy_ref, wo_ref, g_ref, u_ref, d_ref):
        dact = 0.5 * lax.dot_general(dy_ref[...], wo_ref[...], _NT, preferred_element_type=f32)
        gate, up = g_ref[...].astype(f32), u_ref[...].astype(f32)
        sg = jax.nn.sigmoid(gate)
        d_ref[0] = (dact * up * (sg * (1.0 + gate * (1.0 - sg)))).astype(d_ref.dtype)
        d_ref[1] = (dact * gate * sg).astype(d_ref.dtype)

    tile = pl.BlockSpec((FFN_TM, FFN_TN), lambda i, j: (i, j))
    return pl.pallas_call(
        body, name=name, grid=(S // FFN_TM, D_FF // FFN_TN),
        in_specs=[pl.BlockSpec((FFN_TM, D), lambda i, j: (i, 0)), pl.BlockSpec((FFN_TN, D), lambda i, j: (j, 0)), tile, tile],
        out_specs=pl.BlockSpec((2, FFN_TM, FFN_TN), lambda i, j: (0, i, j)),
        out_shape=jax.ShapeDtypeStruct((2, S, D_FF), bf16),
        compiler_params=_cparams(("parallel", "parallel")),
    )(dy, Wo, gate, up)


def _sp(x):
    hi = x.astype(bf16)
    return hi, (x - hi.astype(f32)).astype(bf16)


def _cat(parts):
    return tuple(jnp.concatenate(p, axis=1) for p in zip(*parts))


def _bmm(eq, a, b):
    (ah, al), (bh, bl) = a, b
    dot = functools.partial(jnp.einsum, eq, preferred_element_type=f32)
    return dot(ah, bh) + (dot(ah, bl) + dot(al, bh))


def _tri_dot(eq, tri, x):
    h1 = x.astype(bf16)
    r1 = x - h1.astype(f32)
    h2 = r1.astype(bf16)
    h3 = (r1 - h2.astype(f32)).astype(bf16)
    dot = functools.partial(jnp.einsum, eq, preferred_element_type=f32)
    return dot(tri, h1) + (dot(tri, h2) + dot(tri, h3))


def _tri_masks(T):
    ti = lax.broadcasted_iota(jnp.int32, (T, T), 0)
    si = lax.broadcasted_iota(jnp.int32, (T, T), 1)
    return ti >= si, ti > si


def _wkv_prep(r, lw, k, kkr, a):
    H, T, _ = r.shape
    low_i, low_s = _tri_masks(T)
    nrm = jnp.sqrt(jnp.sum(kkr * kkr, axis=-1, keepdims=True))
    den = jnp.maximum(nrm, 1e-12)
    kk = kkr / den
    tri = jnp.broadcast_to(low_i.astype(bf16)[None], (H, T, T))
    cl = _tri_dot("hts,hsn->htn", tri, lw)
    c = jnp.exp(cl)
    cprev = jnp.exp(cl - lw)
    cinv = jnp.exp(-cl)
    bt, kt = _sp(kk * a * cinv), _sp(k * cinv)
    L = _cat([_sp(r * c), _sp(-kk * cprev)])
    Mb = _bmm("htn,hsn->hts", L, bt)
    Mk = _bmm("htn,hsn->hts", L, kt)
    A_rb = jnp.where(low_i[None], Mb[:, :T], 0.0)
    A_ab = jnp.where(low_s[None], Mb[:, T:], 0.0)
    Mk = jnp.concatenate([jnp.where(low_i[None], Mk[:, :T], 0.0), jnp.where(low_s[None], Mk[:, T:], 0.0)], axis=1)
    return dict(kk=kk, den=den, nrm=nrm, c=c, cprev=cprev, cinv=cinv, L=L, kt=kt, bt=bt,
                A_ab=A_ab, A_rb=A_rb, Mk=Mk, cT=c[:, T - 1:T, :])


def _tri_inverse(A):
    T = A.shape[-1]
    eye = (lax.broadcasted_iota(jnp.int32, (T, T), 0) == lax.broadcasted_iota(jnp.int32, (T, T), 1)).astype(f32)
    inv = eye[None] + A
    X = A
    n = 1
    while 2 * n < T:
        Xs = _sp(X)
        X = _bmm("hts,hsu->htu", Xs, Xs)
        inv = inv + _bmm("hts,hsu->htu", _sp(inv), _sp(X))
        n *= 2
    return inv


def _wkv_chunk_fwd(S0, r, lw, k, v, kkr, a):
    T = r.shape[1]
    q = _wkv_prep(r, lw, k, kkr, a)
    inv = _tri_inverse(q["A_ab"])
    vs = _sp(v)
    P = _bmm("htk,hvk->htv", q["L"], _sp(S0)) + _bmm("hts,hsv->htv", _sp(q["Mk"]), vs)
    U = _bmm("hts,hsv->htv", _sp(inv), _sp(P[:, T:]))
    Us = _sp(U)
    Y = P[:, :T] + _bmm("hts,hsv->htv", _sp(q["A_rb"]), Us)
    S1 = (S0 + _bmm("htv,htk->hvk", _cat([Us, vs]), _cat([q["bt"], q["kt"]]))) * q["cT"]
    return Y, U, inv, S1


def _wkv_chunk_bwd(S0, Hin, Q, r, lw, k, v, kkr, a, U, inv, dY):
    H, T, _ = r.shape
    low_i, low_s = _tri_masks(T)
    q = _wkv_prep(r, lw, k, kkr, a)
    L, kt, bt = q["L"], q["kt"], q["bt"]
    R = _cat([bt, kt])
    Hh = Hin * q["cT"]
    Hs, S0s, dYs, vs, Us = _sp(Hh), _sp(S0), _sp(dY), _sp(v), _sp(U)
    RH = _bmm("htk,hvk->htv", R, Hs)
    Z = _bmm("hst,hsv->htv", _sp(inv), _sp(RH[:, :T] + _bmm("hst,hsv->htv", _sp(q["A_rb"]), dYs)))
    DZ = _cat([dYs, _sp(Z)])
    both = jnp.concatenate([jnp.broadcast_to(low_i[None], (1, T, T)), jnp.broadcast_to(low_s[None], (1, T, T))], axis=1)
    NU = _sp(jnp.where(both, _bmm("htv,hsv->hts", DZ, Us), 0.0))
    NV = _sp(jnp.where(both, _bmm("htv,hsv->hts", DZ, vs), 0.0))
    ra = _bmm("htv,hvk->htk", DZ, S0s) + _bmm("hts,hsk->htk", NU, bt) + _bmm("hts,hsk->htk", NV, kt)
    dr = ra[:, :T] * q["c"]
    da = ra[:, T:] * q["cprev"]
    dv = RH[:, T:] + _bmm("hst,hsv->htv", _sp(q["Mk"]), DZ)
    VH = _bmm("htv,hvk->htk", _cat([vs, Us]), Hs)
    dk = (VH[:, :T] + _bmm("hst,hsk->htk", NV, L)) * q["cinv"]
    db = (VH[:, T:] + _bmm("hst,hsk->htk", NU, L)) * q["cinv"]
    H0 = Hh + _bmm("htv,htk->hvk", DZ, L)
    kk = q["kk"]
    e = r * dr - kk * a * db - k * dk
    f = -kk * da
    tri_i = jnp.broadcast_to(low_i.astype(bf16)[None], (H, T, T))
    tri_s = jnp.broadcast_to(low_s.astype(bf16)[None], (H, T, T))
    dlw = _tri_dot("hst,hsn->htn", tri_i, e) + _tri_dot("hst,hsn->htn", tri_s, f) + Q
    Qn = Q + jnp.sum(e + f, axis=1, keepdims=True)
    dkk = db * a - da
    dasig = db * kk
    proj = jnp.sum(dkk * kk, axis=-1, keepdims=True)
    dkkr = jnp.where(q["nrm"] > 1e-12, dkk - kk * proj, dkk) / q["den"]
    return dr, dlw, dk, dv, dkkr, dasig, H0, Qn


def _heads(ref, rows=slice(None)):
    return jnp.stack([ref[rows, h * HEAD:(h + 1) * HEAD] for h in range(RW_HEADS)], axis=0)


def _put_heads(ref, val, rows=slice(None)):
    for h in range(RW_HEADS):
        ref[rows, h * HEAD:(h + 1) * HEAD] = val[h]


def _wkv_fwd(r, lw, k, v, kkr, a, g, r_k, ln_w, ln_b, late_pack=None):
    S = r.shape[0]
    H, N, T = RW_HEADS, HEAD, WKV_T
    TS = T * WKV_SUB
    nc = S // TS
    hosting = late_pack is not None

    def body(r_ref, lw_ref, k_ref, v_ref, kkr_ref, a_ref, g_ref, rk_ref, lnw_ref, lnb_ref, *rest):
        if hosting:
            pack_ref, y_ref, yg_ref, wkv_ref, u_ref, inv_ref, s0_ref, gathered_ref, state, *sems = rest
            start, forward, finish = _gather_phases(pack_ref, gathered_ref, *sems)
            pl.when(pl.program_id(0) == 0)(start)
            pl.when(pl.program_id(0) == nc // 2)(forward)
        else:
            y_ref, yg_ref, wkv_ref, u_ref, inv_ref, s0_ref, state = rest

        @pl.when(pl.program_id(0) == 0)
        def _():
            state[...] = jnp.zeros(state.shape, f32)

        S0 = state[...]
        for c in range(WKV_SUB):
            rows = slice(c * T, (c + 1) * T)
            s0_ref[c] = S0
            rr, kk2, vv = _heads(r_ref, rows), _heads(k_ref, rows), _heads(v_ref, rows)
            Y, U, inv, S0 = _wkv_chunk_fwd(S0, rr, _heads(lw_ref, rows), kk2, vv, _heads(kkr_ref, rows),
                                           _heads(a_ref, rows))
            wkv_ref[:, rows, :] = Y
            u_ref[:, rows, :] = U
            inv_ref[:, rows, :] = inv
            mean = jnp.mean(Y, axis=-1, keepdims=True)
            var = jnp.mean(jnp.square(Y - mean), axis=-1, keepdims=True)
            yn = (Y - mean) * lax.rsqrt(var + GN_EPS)
            bonus = jnp.sum(rr * kk2 * rk_ref[...], axis=-1, keepdims=True) * vv
            _put_heads(y_ref, yn * lnw_ref[...] + lnb_ref[...] + bonus, rows)
        state[...] = S0
        yg_ref[...] = (y_ref[...] * g_ref[...]).astype(yg_ref.dtype)
        if hosting:
            pl.when(pl.program_id(0) == nc - 1)(finish)

    tok = pl.BlockSpec((TS, H * N), lambda i: (i, 0))
    blk = pl.BlockSpec((H, TS, N), lambda i: (0, i, 0))
    par = pl.BlockSpec((H, 1, N), lambda i: (0, 0, 0))
    hbm = pl.BlockSpec(memory_space=pl.ANY)
    seq = jax.ShapeDtypeStruct((H, S, N), f32)
    out_specs = [tok, tok, blk, blk, pl.BlockSpec((H, TS, T), lambda i: (0, i, 0)),
                 pl.BlockSpec((WKV_SUB, H, N, N), lambda i: (i, 0, 0, 0))]
    out_shape = [jax.ShapeDtypeStruct((S, H * N), f32), jax.ShapeDtypeStruct((S, H * N), bf16), seq, seq,
                 jax.ShapeDtypeStruct((H, S, T), f32),
                 jax.ShapeDtypeStruct((S // T, H, N, N), f32)]
    if hosting:
        out_specs.append(hbm)
        out_shape.append(jax.ShapeDtypeStruct((N_DEV,) + late_pack.shape, late_pack.dtype))
    return pl.pallas_call(
        body, name="wkv_fwd", grid=(nc,), in_specs=[tok] * 7 + [par] * 3 + [hbm] * hosting,
        out_specs=out_specs, out_shape=out_shape,
        scratch_shapes=[pltpu.VMEM((H, N, N), f32)] + (GATHER_SEMS if hosting else []),
        compiler_params=_cparams(("arbitrary",)),
    )(r, lw, k, v, kkr, a, g, r_k, ln_w, ln_b, *([late_pack] if hosting else []))


def _wkv_bwd(dy, g, r, lw, k, v, kkr, a, wkv, U, inv, S0s, r_k, ln_w, ln_b, late_pieces=None):
    S = r.shape[0]
    H, N, T = RW_HEADS, HEAD, WKV_T
    TS = T * WKV_SUB
    nc = S // TS
    hosting = late_pieces is not None
    n_late = len(late_pieces) if hosting else 0

    def body(dy_ref, g_ref, r_ref, lw_ref, k_ref, v_ref, kkr_ref, a_ref, wkv_ref, u_ref, inv_ref, s0_ref,
             rk_ref, lnw_ref, lnb_ref, *rest):
        if hosting:
            piece_refs, rest = rest[:n_late], rest[n_late:]
            (dr_ref, dlw_ref, dk_ref, dv_ref, dkkr_ref, da_ref, drk_ref, dlnw_ref, dlnb_ref, recv_ref,
             hst, qst, *sems) = rest
            start, finish = _direct_phases(piece_refs, [p.shape[1] for p in late_pieces], recv_ref, *sems)
            pl.when(pl.program_id(0) == 0)(start)
        else:
            dr_ref, dlw_ref, dk_ref, dv_ref, dkkr_ref, da_ref, drk_ref, dlnw_ref, dlnb_ref, hst, qst = rest

        @pl.when(pl.program_id(0) == 0)
        def _():
            hst[...] = jnp.zeros(hst.shape, f32)
            qst[...] = jnp.zeros(qst.shape, f32)
            drk_ref[...] = jnp.zeros(drk_ref.shape, f32)
            dlnw_ref[...] = jnp.zeros(dlnw_ref.shape, f32)
            dlnb_ref[...] = jnp.zeros(dlnb_ref.shape, f32)

        dyg = dy_ref[...] * g_ref[...]
        rk = rk_ref[...]
        Hst, Qst = hst[...], qst[...]
        for c in reversed(range(WKV_SUB)):
            rows = slice(c * T, (c + 1) * T)
            dya = _heads(dyg, rows)
            rr, kk2, vv, Y = _heads(r_ref, rows), _heads(k_ref, rows), _heads(v_ref, rows), wkv_ref[:, rows, :]
            s = jnp.sum(rr * kk2 * rk, axis=-1, keepdims=True)
            ds = jnp.sum(dya * vv, axis=-1, keepdims=True)
            mean = jnp.mean(Y, axis=-1, keepdims=True)
            var = jnp.mean(jnp.square(Y - mean), axis=-1, keepdims=True)
            rstd = lax.rsqrt(var + GN_EPS)
            yn = (Y - mean) * rstd
            dyn = dya * lnw_ref[...]
            dY = rstd * (dyn - jnp.mean(dyn, axis=-1, keepdims=True) - yn * jnp.mean(dyn * yn, axis=-1, keepdims=True))
            drk_ref[...] += jnp.sum(ds * rr * kk2, axis=1, keepdims=True)
            dlnw_ref[...] += jnp.sum(dya * yn, axis=1, keepdims=True)
            dlnb_ref[...] += jnp.sum(dya, axis=1, keepdims=True)
            dr, dlw, dk, dv, dkkr, dasig, Hst, Qst = _wkv_chunk_bwd(
                s0_ref[c], Hst, Qst, rr, _heads(lw_ref, rows), kk2, vv, _heads(kkr_ref, rows), _heads(a_ref, rows),
                u_ref[:, rows, :], inv_ref[:, rows, :], dY)
            _put_heads(dr_ref, dr + ds * kk2 * rk, rows)
            _put_heads(dlw_ref, dlw, rows)
            _put_heads(dk_ref, dk + ds * rr * rk, rows)
            _put_heads(dv_ref, dv + dya * s, rows)
            _put_heads(dkkr_ref, dkkr, rows)
            _put_heads(da_ref, dasig, rows)
        hst[...] = Hst
        qst[...] = Qst
        if hosting:
            pl.when(pl.program_id(0) == nc - 1)(finish)

    tok = pl.BlockSpec((TS, H * N), lambda i: (nc - 1 - i, 0))
    blk = pl.BlockSpec((H, TS, N), lambda i: (0, nc - 1 - i, 0))
    par = pl.BlockSpec((H, 1, N), lambda i: (0, 0, 0))
    hbm = pl.BlockSpec(memory_space=pl.ANY)
    seq = jax.ShapeDtypeStruct((S, H * N), f32)
    pout = jax.ShapeDtypeStruct((H, 1, N), f32)
    out_specs, out_shape = [tok] * 6 + [par] * 3, [seq] * 6 + [pout] * 3
    sems = []
    if hosting:
        rows_late = sum(p.shape[1] for p in late_pieces)
        out_specs.append(hbm)
        out_shape.append(jax.ShapeDtypeStruct((N_DEV - 1, rows_late, late_pieces[0].shape[2]), late_pieces[0].dtype))
        sems = [pltpu.SemaphoreType.DMA((n_late * (N_DEV - 1),))] * 2
    return pl.pallas_call(
        body, name="wkv_bwd", grid=(nc,),
        in_specs=([tok] * 8 + [blk] * 2 + [pl.BlockSpec((H, TS, T), lambda i: (0, nc - 1 - i, 0))]
                  + [pl.BlockSpec((WKV_SUB, H, N, N), lambda i: (nc - 1 - i, 0, 0, 0))]
                  + [par] * 3 + [hbm] * n_late),
        out_specs=out_specs, out_shape=out_shape,
        scratch_shapes=[pltpu.VMEM((H, N, N), f32), pltpu.VMEM((H, 1, N), f32)] + sems,
        compiler_params=_cparams(("arbitrary",)),
    )(dy, g, r, lw, k, v, kkr, a, wkv, U, inv, S0s, r_k, ln_w, ln_b, *(late_pieces if hosting else []))


ATTN_TT = 2048


def _attn_rows(d, i, j):
    return pl.ds(ATTN_BLK * d * i + j, ATTN_BLK, stride=d) if d > 1 else pl.ds(ATTN_BLK * i, ATTN_BLK)


def _attn_take(ref, d, nsub):
    return jnp.stack([ref[_attn_rows(d, i, j), :] for i in range(nsub) for j in range(d)], axis=0)


def _attn_put(ref, val, d):
    for i in range(val.shape[0] // d):
        for j in range(d):
            ref[_attn_rows(d, i, j), :] = val[i * d + j]


def _attn_prev(cur, before, d):
    return before if cur.shape[0] == d else jnp.concatenate([before, cur[:cur.shape[0] - d]], axis=0)


def _attn_specs(gi, d, nt, reverse):
    per_tile = ATTN_TT // (ATTN_BLK * d)

    def tile(n):
        return nt - 1 - n if reverse else n

    def col(kind):
        return lambda hp, n: (tile(n), kind * (ATTN_W // 128) + 2 * gi + hp)

    def col_before(kind):
        return lambda hp, n: (jnp.maximum(tile(n) * per_tile - 1, 0), kind * (ATTN_W // 128) + 2 * gi + hp)

    cur = [pl.BlockSpec((ATTN_TT, 128), col(kind)) for kind in range(3)]
    before = [pl.BlockSpec((ATTN_BLK * d, 128), col_before(kind)) for kind in (1, 2)]
    own = pl.BlockSpec((ATTN_TT, 128), lambda hp, n: (tile(n), hp))
    return cur, before, own, tile


def _attn_norm(x, gain, scale):
    rs = lax.rsqrt(jnp.mean(x * x, axis=-1, keepdims=True) + RMS_EPS)
    return x * rs * (gain * scale), rs


def _attn_scores(qn, kn_c, kn_p, first):
    s_c = jnp.einsum("gqe,gke->gqk", qn.astype(bf16), kn_c.astype(bf16), preferred_element_type=f32)
    s_p = jnp.einsum("gqe,gke->gqk", qn.astype(bf16), kn_p.astype(bf16), preferred_element_type=f32)
    qi = lax.broadcasted_iota(jnp.int32, (1, ATTN_BLK, ATTN_BLK), 1)
    ki = lax.broadcasted_iota(jnp.int32, (1, ATTN_BLK, ATTN_BLK), 2)
    s_c = jnp.where(qi >= ki, s_c, NEG_INF)
    s_p = jnp.where(jnp.logical_and(ki >= qi, jnp.logical_not(first)), s_p, NEG_INF)
    return s_c, s_p


def _attn_fwd(pqkv, qg, kg, gi, S):
    d = ATTN_PAIRS[gi][1]
    nt = S // ATTN_TT
    nsub = ATTN_TT // (ATTN_BLK * d)
    nd = nsub * d

    def body(q_ref, k_ref, v_ref, kb_ref, vb_ref, qg_ref, kg_ref, o_ref, lse_ref):
        Q, K, V = _attn_take(q_ref, d, nsub), _attn_take(k_ref, d, nsub), _attn_take(v_ref, d, nsub)
        KB, VB = _attn_take(kb_ref, d, 1), _attn_take(vb_ref, d, 1)
        first = jnp.logical_and(lax.broadcasted_iota(jnp.int32, (nd, 1, 1), 0) < d, pl.program_id(1) == 0)
        outs, lses = [], []
        for h in range(2):
            sl = slice(h * HEAD, (h + 1) * HEAD)
            kc, vc = K[:, :, sl], V[:, :, sl]
            kp, vp = _attn_prev(kc, KB[:, :, sl], d), _attn_prev(vc, VB[:, :, sl], d)
            qn, _ = _attn_norm(Q[:, :, sl], qg_ref[...], HEAD ** -0.5)
            kn_c, _ = _attn_norm(kc, kg_ref[...], 1.0)
            kn_p, _ = _attn_norm(kp, kg_ref[...], 1.0)
            s_c, s_p = _attn_scores(qn, kn_c, kn_p, first)
            m = jnp.maximum(jnp.max(s_c, axis=-1, keepdims=True), jnp.max(s_p, axis=-1, keepdims=True))
            p_c = jnp.exp(s_c - m)
            p_p = jnp.exp(s_p - m)
            den = jnp.sum(p_c, axis=-1, keepdims=True) + jnp.sum(p_p, axis=-1, keepdims=True)
            inv = 1.0 / den
            o = jnp.einsum("gqk,gke->gqe", (p_c * inv).astype(bf16), vc.astype(bf16), preferred_element_type=f32)
            o += jnp.einsum("gqk,gke->gqe", (p_p * inv).astype(bf16), vp.astype(bf16), preferred_element_type=f32)
            outs.append(o)
            lses.append(jnp.broadcast_to(m + jnp.log(den), o.shape))
        _attn_put(o_ref, jnp.concatenate(outs, axis=-1), d)
        _attn_put(lse_ref, jnp.concatenate(lses, axis=-1), d)

    cur, before, own, _ = _attn_specs(gi, d, nt, False)
    par = pl.BlockSpec((1, HEAD), lambda hp, n: (0, 0))
    shp = jax.ShapeDtypeStruct((S, 2 * 128), f32)
    return pl.pallas_call(
        body, name=f"attn_fwd{gi}", grid=(2, nt), in_specs=cur + before + [par] * 2, out_specs=[own, own],
        out_shape=[shp, shp], compiler_params=_cparams(("arbitrary", "arbitrary")),
    )(pqkv, pqkv, pqkv, pqkv, pqkv, qg, kg)


def _attn_bwd(pqkv, o, lse, do, dlse, qg, kg, gi, S):
    d = ATTN_PAIRS[gi][1]
    nt = S // ATTN_TT
    nsub = ATTN_TT // (ATTN_BLK * d)
    nd = nsub * d

    def norm_bwd(dxn, x, rs, gain, scale):
        xh = x * rs
        dxh = dxn * (gain * scale)
        dx = rs * (dxh - xh * jnp.mean(dxh * xh, axis=-1, keepdims=True))
        dgain = jnp.sum(jnp.sum(dxn * xh * scale, axis=1), axis=0, keepdims=True)
        return dx, dgain

    def to_before(part, carried):
        return carried if nsub == 1 else jnp.concatenate([part[d:], carried], axis=0)

    def body(q_ref, k_ref, v_ref, kb_ref, vb_ref, o_ref, lse_ref, do_ref, dlse_ref, qg_ref, kg_ref,
             dq_ref, dk_ref, dv_ref, dqg_ref, dkg_ref, carry_k, carry_v):
        step = pl.program_id(1)

        @pl.when(jnp.logical_and(pl.program_id(0) == 0, step == 0))
        def _():
            dqg_ref[...] = jnp.zeros(dqg_ref.shape, f32)
            dkg_ref[...] = jnp.zeros(dkg_ref.shape, f32)

        @pl.when(step == 0)
        def _():
            carry_k[...] = jnp.zeros(carry_k.shape, f32)
            carry_v[...] = jnp.zeros(carry_v.shape, f32)

        Q, K, V = _attn_take(q_ref, d, nsub), _attn_take(k_ref, d, nsub), _attn_take(v_ref, d, nsub)
        KB, VB = _attn_take(kb_ref, d, 1), _attn_take(vb_ref, d, 1)
        O, LSE = _attn_take(o_ref, d, nsub), _attn_take(lse_ref, d, nsub)
        DO, DLSE = _attn_take(do_ref, d, nsub), _attn_take(dlse_ref, d, nsub)
        first = jnp.logical_and(lax.broadcasted_iota(jnp.int32, (nd, 1, 1), 0) < d, step == nt - 1)
        qg, kg = qg_ref[...], kg_ref[...]
        dqs, dks, dvs = [], [], []
        for h in range(2):
            sl = slice(h * HEAD, (h + 1) * HEAD)
            qx, kx, vc = Q[:, :, sl], K[:, :, sl], V[:, :, sl]
            kpx, vp = _attn_prev(kx, KB[:, :, sl], d), _attn_prev(vc, VB[:, :, sl], d)
            qn, rq = _attn_norm(qx, qg, HEAD ** -0.5)
            kn_c, rk_c = _attn_norm(kx, kg, 1.0)
            kn_p, _ = _attn_norm(kpx, kg, 1.0)
            s_c, s_p = _attn_scores(qn, kn_c, kn_p, first)
            lse = LSE[:, :, h * HEAD:h * HEAD + 1]
            p_c = jnp.exp(s_c - lse)
            p_p = jnp.exp(s_p - lse)
            dO = DO[:, :, sl]
            dOb = dO.astype(bf16)
            dp_c = jnp.einsum("gqe,gke->gqk", dOb, vc.astype(bf16), preferred_element_type=f32)
            dp_p = jnp.einsum("gqe,gke->gqk", dOb, vp.astype(bf16), preferred_element_type=f32)
            corr = DLSE[:, :, h * HEAD:h * HEAD + 1] - jnp.sum(dO * O[:, :, sl], axis=-1, keepdims=True)
            ds_c = (p_c * (dp_c + corr)).astype(bf16)
            ds_p = (p_p * (dp_p + corr)).astype(bf16)
            qnb = qn.astype(bf16)
            dqn = (jnp.einsum("gqk,gke->gqe", ds_c, kn_c.astype(bf16), preferred_element_type=f32)
                   + jnp.einsum("gqk,gke->gqe", ds_p, kn_p.astype(bf16), preferred_element_type=f32))
            dkn_p = jnp.einsum("gqk,gqe->gke", ds_p, qnb, preferred_element_type=f32)
            dv_p = jnp.einsum("gqk,gqe->gke", p_p.astype(bf16), dOb, preferred_element_type=f32)
            dkn = jnp.einsum("gqk,gqe->gke", ds_c, qnb, preferred_element_type=f32) + to_before(dkn_p, carry_k[h])
            dv = (jnp.einsum("gqk,gqe->gke", p_c.astype(bf16), dOb, preferred_element_type=f32)
                  + to_before(dv_p, carry_v[h]))
            carry_k[h] = dkn_p[:d]
            carry_v[h] = dv_p[:d]
            dq, dqg = norm_bwd(dqn, qx, rq, qg, HEAD ** -0.5)
            dk, dkg = norm_bwd(dkn, kx, rk_c, kg, 1.0)
            dqg_ref[...] += dqg
            dkg_ref[...] += dkg
            dqs.append(dq)
            dks.append(dk)
            dvs.append(dv)
        _attn_put(dq_ref, jnp.concatenate(dqs, axis=-1), d)
        _attn_put(dk_ref, jnp.concatenate(dks, axis=-1), d)
        _attn_put(dv_ref, jnp.concatenate(dvs, axis=-1), d)

    cur, before, own, _ = _attn_specs(gi, d, nt, True)
    par = pl.BlockSpec((1, HEAD), lambda hp, n: (0, 0))
    shp = jax.ShapeDtypeStruct((S, 2 * 128), f32)
    pshp = jax.ShapeDtypeStruct((1, HEAD), f32)
    return pl.pallas_call(
        body, name=f"attn_bwd{gi}", grid=(2, nt), in_specs=cur + before + [own] * 4 + [par] * 2,
        out_specs=[own] * 3 + [par] * 2, out_shape=[shp] * 3 + [pshp] * 2,
        scratch_shapes=[pltpu.VMEM((2, d, ATTN_BLK, HEAD), f32)] * 2,
        compiler_params=_cparams(("arbitrary", "arbitrary")),
    )(pqkv, pqkv, pqkv, pqkv, pqkv, o, lse, do, dlse, qg, kg)


def _rms(x, g):
    rs = lax.rsqrt(jnp.mean(x * x, axis=-1, keepdims=True) + RMS_EPS)
    return x * rs * g


def _f_rms(x, g):
    return _rms(x, g)


def _f_resid_rms(coef, x, f, g):
    xn = x + coef * f
    return xn, _rms(xn, g)


def _f_rms_bwd(n_parts, *args):
    dns = args[:n_parts]
    x, dres, g = args[n_parts:]
    dn = dns[0]
    for t in dns[1:]:
        dn = dn + t
    rs = lax.rsqrt(jnp.mean(x * x, axis=-1, keepdims=True) + RMS_EPS)
    xh = x * rs
    dxh = dn * g
    dx = dres + rs * (dxh - xh * jnp.mean(dxh * xh, axis=-1, keepdims=True))
    return dx, dx, jnp.sum(dn * xh, axis=0, keepdims=True)


def _f_loss(x, f, tgt):
    y = x + 0.5 * f
    diff = y - tgt
    part = 0.5 * jnp.sum(jnp.mean(diff * diff, axis=-1, keepdims=True), axis=0, keepdims=True)
    dy = diff * (1.0 / D)
    return dy, dy, jnp.broadcast_to(part, (1, 128))


def _dotb(a, b, dims):
    return lax.dot_general(a.astype(bf16), b.astype(bf16), dims, preferred_element_type=f32)


_NN = (((1,), (0,)), ((), ()))
_NT = (((1,), (1,)), ((), ()))
_TN = (((0,), (0,)), ((), ()))


def _rwkv_pre_core(prkv, prkv_prev, plora, plora_prev, mu_rkv, mu_lora, w0, w2p, a0, a2p, g2p, k_k, k_a):
    xs = prkv + (prkv_prev - prkv) * mu_rkv
    xl = plora + (plora_prev - plora) * mu_lora
    r, k, v = xs[:, :D], xs[:, D:2 * D], xs[:, 2 * D:]
    wd, ad, gd = xl[:, :128], xl[:, 128:256], xl[:, 256:]
    tw = jnp.tanh(wd)
    zw = w0 + _dotb(tw, w2p, _NN)
    sp = jnp.maximum(-zw, 0.0) + jnp.log(1.0 + jnp.exp(-jnp.abs(zw)))
    lw = -jnp.exp(-sp - 0.5)
    a = jax.nn.sigmoid(a0 + _dotb(ad, a2p, _NN))
    sg = jax.nn.sigmoid(gd)
    return dict(r=r, k=k, v=v, tw=tw, zw=zw, lw=lw, a=a, sg=sg, ad=ad)


def _rows_down(x, halo, blk):
    before = jnp.where(blk > 0, halo[HALO - 1:HALO, :], 0.0)
    row = lax.broadcasted_iota(jnp.int32, (x.shape[0], 1), 0)
    return jnp.where(row == 0, before, pltpu.roll(x, 1, 0))


def _rows_up(x, after):
    n = x.shape[0]
    row = lax.broadcasted_iota(jnp.int32, (n, 1), 0)
    return jnp.where(row == n - 1, after, pltpu.roll(x, n - 1, 0))


def _f_rwkv_pre(prkv, plora, mu_rkv, mu_lora, w0, w2p, a0, a2p, g2p, k_k, k_a, halo_rkv, halo_lora, blk):
    c = _rwkv_pre_core(prkv, _rows_down(prkv, halo_rkv, blk), plora, _rows_down(plora, halo_lora, blk),
                       mu_rkv, mu_lora, w0, w2p, a0, a2p, g2p, k_k, k_a)
    g = _dotb(c["sg"], g2p, _NN)
    k, a = c["k"], c["a"]
    return c["r"], c["lw"], k * (1.0 + (a - 1.0) * k_a), c["v"], k * k_k, a, g


def _f_rwkv_pre_bwd(prkv, plora, dr, dlw, dk2, dv, dkkr, da, dya, yap,
                    mu_rkv, mu_lora, w0, w2p, a0, a2p, g2p, k_k, k_a, halo_rkv, halo_lora, next_rkv, next_lora, blk):
    prkv_prev, plora_prev = _rows_down(prkv, halo_rkv, blk), _rows_down(plora, halo_lora, blk)
    c = _rwkv_pre_core(prkv, prkv_prev, plora, plora_prev, mu_rkv, mu_lora, w0, w2p, a0, a2p, g2p, k_k, k_a)
    k, a, sg, tw, zw, lw = c["k"], c["a"], c["sg"], c["tw"], c["zw"], c["lw"]
    dg = dya * yap
    dsg = _dotb(dg, g2p, _NT)
    dgd = dsg * sg * (1.0 - sg)
    dg2p = _dotb(sg, dg, _TN)
    dk = dk2 * (1.0 + (a - 1.0) * k_a) + dkkr * k_k
    da_t = da + dk2 * k * k_a
    dk_a = jnp.sum(dk2 * k * (a - 1.0), axis=0, keepdims=True)
    dk_k = jnp.sum(dkkr * k, axis=0, keepdims=True)
    dza = da_t * a * (1.0 - a)
    da0 = jnp.sum(dza, axis=0, keepdims=True)
    dad = _dotb(dza, a2p, _NT)
    da2p = _dotb(c["ad"], dza, _TN)
    dzw = dlw * lw * jax.nn.sigmoid(-zw)
    dw0 = jnp.sum(dzw, axis=0, keepdims=True)
    dtw = _dotb(dzw, w2p, _NT)
    dw2p = _dotb(tw, dzw, _TN)
    dwd = dtw * (1.0 - tw * tw)
    dxs = jnp.concatenate([dr, dk, dv], axis=1)
    dxl = jnp.concatenate([dwd, dad, dgd], axis=1)
    dmu_rkv = jnp.sum(dxs * (prkv_prev - prkv), axis=0, keepdims=True)
    dmu_lora = jnp.sum(dxl * (plora_prev - plora), axis=0, keepdims=True)
    to_next_rkv, to_next_lora = dxs * mu_rkv, dxl * mu_lora
    return (dxs * (1.0 - mu_rkv) + _rows_up(to_next_rkv, next_rkv), dxl * (1.0 - mu_lora) + _rows_up(to_next_lora, next_lora),
            dmu_rkv, dmu_lora, dw0, da0, dk_k, dk_a, dw2p, da2p, dg2p, to_next_rkv[0:1], to_next_lora[0:1])


def _group_alpha(l0, l1, l2):
    m = jnp.maximum(jnp.maximum(l0, l1), l2)
    e0, e1, e2 = jnp.exp(l0 - m), jnp.exp(l1 - m), jnp.exp(l2 - m)
    inv = 1.0 / (e0 + e1 + e2)
    return jnp.concatenate([e0 * inv, e1 * inv, e2 * inv], axis=1)


def _f_combine(o0, o1, o2, l0, l1, l2):
    return jnp.concatenate([o0, o1, o2], axis=1) * _group_alpha(l0, l1, l2)


def _f_combine_bwd(dyb, o0, o1, o2, l0, l1, l2, bd):
    alpha = _group_alpha(l0, l1, l2)
    hi, lo = _sp(dyb * jnp.concatenate([o0, o1, o2], axis=1))
    ones = bd.astype(bf16)
    e = jnp.dot(hi, ones, preferred_element_type=f32) + jnp.dot(lo, ones, preferred_element_type=f32)
    ae = alpha * e
    tot = ae[:, :256] + ae[:, 256:512] + ae[:, 512:]
    do = dyb * alpha
    dl = ae - alpha * jnp.concatenate([tot, tot, tot], axis=1)
    return do[:, :256], do[:, 256:512], do[:, 512:], dl[:, :256], dl[:, 256:512], dl[:, 512:]


def _f_merge(pgate, ta, tb, b_gate):
    gate = jax.nn.sigmoid(pgate + b_gate)
    return gate[:, :D] * ta + gate[:, D:] * tb


def _f_merge_bwd(dm, pgate, ta, tb, b_gate):
    gate = jax.nn.sigmoid(pgate + b_gate)
    ga, gb = gate[:, :D], gate[:, D:]
    dpg = jnp.concatenate([dm * ta * ga * (1.0 - ga), dm * tb * gb * (1.0 - gb)], axis=1)
    return dm * ga, dm * gb, dpg, jnp.sum(dpg, axis=0, keepdims=True)


def _f_adamw(w, g, m, v):
    m2 = ADAM_B1 * m + (1.0 - ADAM_B1) * g
    v2 = ADAM_B2 * v + (1.0 - ADAM_B2) * jnp.square(g)
    m_hat = m2 / (1.0 - ADAM_B1 ** ADAM_STEP)
    v_hat = v2 / (1.0 - ADAM_B2 ** ADAM_STEP)
    delta = -ADAM_LR * (m_hat / (jnp.sqrt(v_hat) + ADAM_EPS) + ADAM_WD * w)
    return delta, m2, v2


def _ffn_bwd(tag, dxo, dxo_b, x_in, n, gate, up, act, g, WiT, Wo, cross=None):
    du = _ffn_dact(f"{tag}_dact", dxo_b, Wo, gate, up)
    dWo = _mm(f"{tag}_dwo", act, dxo_b, "tn", out_dtype=GRAD_WIRE, scale=0.5)
    drms = (functools.partial(_f_rms_bwd, 1), [x_in, dxo], [g], [(D, f32), (D, bf16)], [(1, D)])
    dx, dx_b, dg, *recv = _mm(f"{tag}_dn", du, WiT, "nn", cross=cross, epilogue=drms)
    dWiT = _mm(f"{tag}_dwi", du, n, "tn", out_dtype=GRAD_WIRE)
    return dx, dx_b, dg, dWiT, dWo, (recv[0] if recv else None)


def _local_step(x0, tgt, W, P, hooks=None):
    S = x0.shape[0]
    (n1,) = _rowwise("f1_rms", _f_rms, [x0], [P["ffn1_norm"]], [(D, bf16)])
    hooks = hooks or {}
    if "gather_mid" in hooks:
        pack, weights = hooks["gather_mid"]
        gate1, up1, act1, gathered = _ffn_up("f1_up", n1, W["f1_iT"], gather=pack)
        W = {**W, **weights(gathered)}
    else:
        gate1, up1, act1 = _ffn_up("f1_up", n1, W["f1_iT"])
    mix_rms = (lambda f, x, g: _f_resid_rms(0.5, x, f, g), [x0], [P["mix_norm"]], [(D, f32), (D, bf16)], [])
    if "gather_in" in hooks:
        pack, weights = hooks["gather_in"]
        x1, h, gathered = _mm("f1_down", act1, W["f1_o"], "nn", gather=pack, epilogue=mix_rms)
        W = {**W, **weights(gathered)}
    else:
        x1, h = _mm("f1_down", act1, W["f1_o"], "nn", epilogue=mix_rms)
    prkv = _mm("p_rkv", h, W["in_rkvT"], "nt")
    plora = _mm("p_lora", h, W["in_loraT"], "nt")
    pqkv = _mm("p_qkv", h, W["in_qkvT"], "nt")
    pgate = _mm("p_gate", h, W["in_gateT"], "nt")
    pre_params = [P["mu_rkv"], P["mu_lora"], P["w0"], W["w2p"], P["a0"], W["a2p"], W["g2p"], P["k_k"], P["k_a"]]
    r, lw, k2, v, kkr, a, g = _rowwise("rwkv_pre", _f_rwkv_pre, [prkv, plora], pre_params, [(D, f32)] * 7, tm=128,
                                       halos=(0, 1))
    hm = [r, lw, k2, v, kkr, a]
    hp = [P["r_k"].reshape(RW_HEADS, 1, HEAD), P["ln_w"].reshape(RW_HEADS, 1, HEAD), P["ln_b"].reshape(RW_HEADS, 1, HEAD)]
    if "gather_late" in hooks:
        pack, weights = hooks["gather_late"]
        yap, ya, wkv_h, U_h, inv_h, S0s, gathered = _wkv_fwd(*hm, g, *hp, late_pack=pack)
        W = {**W, **weights(gathered)}
    else:
        yap, ya, wkv_h, U_h, inv_h, S0s = _wkv_fwd(*hm, g, *hp)
    ta = _mm("proj_a", ya, W["pr"], "nn")
    n_grp = len(ATTN_PAIRS)
    attn = [_attn_fwd(pqkv, P["q_norm"], P["k_norm"], gi, S) for gi in range(n_grp)]
    o_g, lse_g = [t[0] for t in attn], [t[1] for t in attn]
    (yb,) = _rowwise("attn_combine", _f_combine, [*o_g, *lse_g], [], [(ATTN_W, bf16)])
    merge = (lambda tb, pg, ta, bg: (tb, _f_merge(pg, ta, tb, bg)), [pgate, ta], [P["b_gate"]], [(D, f32), (D, bf16)], [])
    tb, merged = _mm("proj_b", yb, W["paT"], "nt", epilogue=merge)
    f2_rms = (lambda f, x, g: _f_resid_rms(1.0, x, f, g), [x1], [P["ffn2_norm"]], [(D, f32), (D, bf16)], [])
    x2, n2 = _mm("mix_out", merged, W["out"], "nn", epilogue=f2_rms)
    gate2, up2, act2 = _ffn_up("f2_up", n2, W["f2_iT"])
    loss_head = (lambda f, x, t: _f_loss(x, f, t), [x2, tgt], [], [(D, f32), (D, bf16)], [(1, 128)])
    dx3, dx3_b, loss = _mm("f2_down", act2, W["f2_o"], "nn", epilogue=loss_head)
    G, Gs = {}, {}
    dx2, dx2_b, Gs["ffn2_norm"], G["f2_iT"], G["f2_o"], _ = _ffn_bwd("f2", dx3, dx3_b, x2, n2, gate2, up2, act2,
                                                                    P["ffn2_norm"], W["f2_iT"], W["f2_o"])
    merge_bwd = (_f_merge_bwd, [pgate, ta, tb], [P["b_gate"]], [(D, bf16), (D, bf16), (2 * D, bf16)], [(1, 2 * D)])
    dta, dtb, dpgate, Gs["b_gate"] = _mm("d_merged", dx2_b, W["out"], "nt", epilogue=merge_bwd)
    G["out"] = _mm("dw_out", merged, dx2_b, "tn", out_dtype=GRAD_WIRE)
    dya = _mm("d_ya", dta, W["pr"], "nt")
    G["pr"] = _mm("dw_pr", ya, dta, "tn", out_dtype=GRAD_WIRE)
    bd = (jnp.arange(ATTN_W)[:, None] // HEAD == jnp.arange(ATTN_W)[None, :] // HEAD).astype(f32)
    combine_bwd = (_f_combine_bwd, [*o_g, *lse_g], [bd], [(ATTN_W // n_grp, f32)] * (2 * n_grp), [])
    dol = _mm("d_yb", dtb, W["paT"], "nn", epilogue=combine_bwd)
    G["paT"] = _mm("dw_pa", dtb, yb, "tn", out_dtype=GRAD_WIRE)
    if "reduce_late" in hooks:
        pieces_late = hooks["reduce_late"](G)
        hg = _wkv_bwd(dya, g, *hm, wkv_h, U_h, inv_h, S0s, *hp, late_pieces=pieces_late)
        G["late"] = (pieces_late, hg[9])
    else:
        hg = _wkv_bwd(dya, g, *hm, wkv_h, U_h, inv_h, S0s, *hp)
    dr, dlw, dk2, dv, dkkr, da = hg[:6]
    Gs["r_k"], Gs["ln_w"], Gs["ln_b"] = (t.reshape(1, D) for t in hg[6:9])
    lp = sum(LORA_PAD)
    (dprkv, dplora, Gs["mu_rkv"], Gs["mu_lora"], Gs["w0"], Gs["a0"], Gs["k_k"], Gs["k_a"],
     dw2p, da2p, dg2p) = _rowwise(
        "rwkv_pre_bwd", _f_rwkv_pre_bwd,
        [prkv, plora, dr, dlw, dk2, dv, dkkr, da, dya, yap], pre_params,
        [(3 * D, bf16), (lp, bf16)],
        [(1, 3 * D), (1, lp), (1, D), (1, D), (1, D), (1, D), (LORA_PAD[0], D), (LORA_PAD[1], D), (LORA_PAD[2], D)],
        tm=128, halos=(0, 1), carries=((1, 3 * D), (1, lp)), reverse=True)
    G["w2T"], G["a2T"], G["g2T"] = dw2p[:LORA_W[0]].T, da2p[:LORA_W[1]].T, dg2p[:LORA_W[2]].T
    dattn = [_attn_bwd(pqkv, o_g[gi], lse_g[gi], dol[gi], dol[n_grp + gi], P["q_norm"], P["k_norm"], gi, S)
             for gi in range(n_grp)]
    Gs["q_norm"] = dattn[0][3] + dattn[1][3] + dattn[2][3]
    Gs["k_norm"] = dattn[0][4] + dattn[1][4] + dattn[2][4]
    dpqkv = jnp.concatenate([dattn[gi][kind] for kind in range(3) for gi in range(n_grp)], axis=1).astype(bf16)
    dh = [_mm("dh_rkv", dprkv, W["in_rkvT"], "nn"), _mm("dh_lora", dplora, W["in_loraT"], "nn"),
          _mm("dh_gate", dpgate, W["in_gateT"], "nn")]
    mix_drms = (functools.partial(_f_rms_bwd, 4), [*dh, x1, dx2], [P["mix_norm"]], [(D, f32), (D, bf16)], [(1, D)])
    dx1, dx1_b, Gs["mix_norm"] = _mm("dh_qkv", dpqkv, W["in_qkvT"], "nn", epilogue=mix_drms)
    dW_rkv = _mm("dw_rkv", dprkv, h, "tn", out_dtype=GRAD_WIRE)
    dW_lora = _mm("dw_lora", dplora, h, "tn", out_dtype=GRAD_WIRE)
    dW_qkv = _mm("dw_qkv", dpqkv, h, "tn", out_dtype=GRAD_WIRE)
    dW_gate = _mm("dw_gate", dpgate, h, "tn", out_dtype=GRAD_WIRE)
    o1, o2 = LORA_PAD[0], LORA_PAD[0] + LORA_PAD[1]
    G["inT"] = jnp.concatenate([dW_rkv, dW_lora[:LORA_W[0]], dW_lora[o1:o1 + LORA_W[1]], dW_lora[o2:o2 + LORA_W[2]],
                                dW_qkv, dW_gate], axis=0)
    part_mid = hooks["reduce_mid"](G) if "reduce_mid" in hooks else None
    dx0, _, Gs["ffn1_norm"], G["f1_iT"], G["f1_o"], recv_mid = _ffn_bwd(
        "f1", dx1, dx1_b, x0, n1, gate1, up1, act1, P["ffn1_norm"], W["f1_iT"], W["f1_o"], cross=part_mid)
    G["mid"] = (part_mid, recv_mid)
    return loss[0, 0], dx0, G, Gs


def _peer(k):
    x, y, c = lax.axis_index("x"), lax.axis_index("y"), lax.axis_index("c")
    px = 1 - x if k & 4 else x
    py = 1 - y if k & 2 else y
    pc = 1 - c if k & 1 else c
    return (px, py, pc), 4 * px + 2 * py + pc


def _gather_phases(x_ref, out_ref, send_sems, recv_sems, local_sem):
    x, y, c = lax.axis_index("x"), lax.axis_index("y"), lax.axis_index("c")
    me, sibling = (x, y, c), (x, y, 1 - c)
    chips = [(1 - x, y), (x, 1 - y), (1 - x, 1 - y)]

    def slot(px, py, pc):
        return out_ref.at[4 * px + 2 * py + pc]

    def copy(k, block, to, src=None):
        return pltpu.make_async_remote_copy(
            src_ref=slot(*block) if src is None else src, dst_ref=slot(*block), send_sem=send_sems.at[k],
            recv_sem=recv_sems.at[k], device_id=to, device_id_type=MESH)

    def mine():
        return pltpu.make_async_copy(x_ref, slot(*me), local_sem)

    def first():
        return [copy(0, me, sibling, src=x_ref)] + [copy(1 + j, me, (*chip, c), src=x_ref) for j, chip in enumerate(chips)]

    def passed():
        return [copy(4 + j, (*chip, c), sibling) for j, chip in enumerate(chips)]

    def start():
        mine().start()
        for cp in first():
            cp.start()

    def forward():
        for j, (chip, cp) in enumerate(zip(chips, passed())):
            copy(1 + j, (*chip, c), me).wait_recv()
            cp.start()

    def finish():
        copy(0, sibling, me).wait_recv()
        for j, chip in enumerate(chips):
            copy(4 + j, (*chip, 1 - c), me).wait_recv()
        for cp in first() + passed():
            cp.wait_send()
        mine().wait()

    return start, forward, finish


GATHER_SEMS = [pltpu.SemaphoreType.DMA((N_DEV - 1,)), pltpu.SemaphoreType.DMA((N_DEV - 1,)), pltpu.SemaphoreType.DMA(())]


def _all_gather(pack):
    R, C = pack.shape

    def body(x_ref, out_ref, send_sems, recv_sems, local_sem):
        for phase in _gather_phases(x_ref, out_ref, send_sems, recv_sems, local_sem):
            phase()

    return pl.pallas_call(
        body, name="weight_all_gather", out_shape=jax.ShapeDtypeStruct((N_DEV, R, C), pack.dtype),
        in_specs=[pl.BlockSpec(memory_space=pl.ANY)], out_specs=pl.BlockSpec(memory_space=pl.ANY),
        scratch_shapes=GATHER_SEMS,
    )(pack)


def _cross_phases(p_ref, out_ref, send_sems, recv_sems):
    x, y, c = lax.axis_index("x"), lax.axis_index("y"), lax.axis_index("c")

    def copies():
        out = []
        for j, (fx, fy) in enumerate([(1, 0), (0, 1), (1, 1)]):
            px = 1 - x if fx else x
            py = 1 - y if fy else y
            out.append(pltpu.make_async_remote_copy(src_ref=p_ref.at[2 * px + py], dst_ref=out_ref.at[j],
                                                    send_sem=send_sems.at[j], recv_sem=recv_sems.at[j],
                                                    device_id=(px, py, c), device_id_type=MESH))
        return out

    def start():
        for cp in copies():
            cp.start()

    def finish():
        for cp in copies():
            cp.wait()

    return start, finish


CROSS_SEMS = [pltpu.SemaphoreType.DMA((3,)), pltpu.SemaphoreType.DMA((3,))]


def _direct_phases(piece_refs, rows, out_ref, send_sems, recv_sems):
    offs = [sum(rows[:i]) for i in range(len(rows))]

    def copies():
        out = []
        for i, g_ref in enumerate(piece_refs):
            for k in range(1, N_DEV):
                dev, idx = _peer(k)
                out.append(pltpu.make_async_remote_copy(
                    src_ref=g_ref.at[idx], dst_ref=out_ref.at[k - 1, pl.ds(offs[i], rows[i])],
                    send_sem=send_sems.at[i * (N_DEV - 1) + k - 1], recv_sem=recv_sems.at[i * (N_DEV - 1) + k - 1],
                    device_id=dev, device_id_type=MESH))
        return out

    def start():
        for cp in copies():
            cp.start()

    def finish():
        for cp in copies():
            cp.wait()

    return start, finish


def _sum_direct(pieces, recv, me, tag):
    n = len(pieces)
    C = pieces[0].shape[2]
    rows_blk = functools.reduce(math.gcd, [p.shape[1] for p in pieces])
    nblk = [p.shape[1] // rows_blk for p in pieces]
    lo = [sum(nblk[:i]) for i in range(n)]
    R = sum(nblk) * rows_blk

    def body(me_ref, *refs):
        g_refs, r_ref, o_ref = refs[:n], refs[n], refs[n + 1]
        rb = pl.program_id(0)
        for i in range(n):
            @pl.when(jnp.logical_and(rb >= lo[i], rb < lo[i] + nblk[i]))
            def _(g_ref=g_refs[i]):
                acc = g_ref[...].astype(f32)
                for k in range(N_DEV - 1):
                    acc = acc + r_ref[k].astype(f32)
                o_ref[...] = acc

    def piece_spec(i):
        return pl.BlockSpec((None, rows_blk, C), lambda rb, me_ref: (me_ref[0], jnp.clip(rb - lo[i], 0, nblk[i] - 1), 0))

    return pl.pallas_call(
        body, name=f"grad_sum_{tag}",
        grid_spec=pltpu.PrefetchScalarGridSpec(
            num_scalar_prefetch=1, grid=(R // rows_blk,),
            in_specs=[piece_spec(i) for i in range(n)] + [pl.BlockSpec((N_DEV - 1, rows_blk, C), lambda rb, me_ref: (0, rb, 0))],
            out_specs=pl.BlockSpec((rows_blk, C), lambda rb, me_ref: (rb, 0))),
        out_shape=jax.ShapeDtypeStruct((R, C), f32),
        compiler_params=_cparams(("arbitrary",)),
    )(me, *pieces, recv)


N_CHIP = 4


def _grad_pair(pieces, tag):
    n = len(pieces)
    C = pieces[0].shape[2]
    rows = [p.shape[1] for p in pieces]
    offs = [sum(rows[:i]) for i in range(n)]
    R = sum(rows)

    def body(*refs):
        g_refs, (other_ref, send_sems, recv_sems) = refs[:n], refs[n:]
        x, y, c = lax.axis_index("x"), lax.axis_index("y"), lax.axis_index("c")
        copies = []
        for i, g_ref in enumerate(g_refs):
            for k in range(N_CHIP):
                cp = pltpu.make_async_remote_copy(
                    src_ref=g_ref.at[4 * (k // 2) + 2 * (k % 2) + 1 - c], dst_ref=other_ref.at[k, pl.ds(offs[i], rows[i])],
                    send_sem=send_sems.at[i * N_CHIP + k], recv_sem=recv_sems.at[i * N_CHIP + k],
                    device_id=(x, y, 1 - c), device_id_type=MESH)
                cp.start()
                copies.append(cp)
        for cp in copies:
            cp.wait()

    return pl.pallas_call(
        body, name=f"grad_pair_{tag}", out_shape=jax.ShapeDtypeStruct((N_CHIP, R, C), pieces[0].dtype),
        in_specs=[pl.BlockSpec(memory_space=pl.ANY)] * n, out_specs=pl.BlockSpec(memory_space=pl.ANY),
        scratch_shapes=[pltpu.SemaphoreType.DMA((n * N_CHIP,))] * 2,
    )(*pieces)


def _pair_add(pieces, other, c, tag):
    n = len(pieces)
    C = pieces[0].shape[2]
    rows_blk = functools.reduce(math.gcd, [p.shape[1] for p in pieces])
    nblk = [p.shape[1] // rows_blk for p in pieces]
    lo = [sum(nblk[:i]) for i in range(n)]
    R = sum(nblk) * rows_blk

    def body(c_ref, *refs):
        g_refs, o_ref, out_ref = refs[:n], refs[n], refs[n + 1]
        rb = pl.program_id(1)
        for i in range(n):
            @pl.when(jnp.logical_and(rb >= lo[i], rb < lo[i] + nblk[i]))
            def _(g_ref=g_refs[i]):
                out_ref[...] = (g_ref[...].astype(f32) + o_ref[...].astype(f32)).astype(out_ref.dtype)

    def piece_spec(i):
        return pl.BlockSpec((1, None, rows_blk, C),
                            lambda k, rb, c_ref: (k, c_ref[0], jnp.clip(rb - lo[i], 0, nblk[i] - 1), 0))

    blk = pl.BlockSpec((1, rows_blk, C), lambda k, rb, c_ref: (k, rb, 0))
    return pl.pallas_call(
        body, name=f"pair_add_{tag}",
        grid_spec=pltpu.PrefetchScalarGridSpec(
            num_scalar_prefetch=1, grid=(N_CHIP, R // rows_blk),
            in_specs=[piece_spec(i) for i in range(n)] + [blk], out_specs=blk),
        out_shape=jax.ShapeDtypeStruct((N_CHIP, R, C), other.dtype),
        compiler_params=_cparams(("arbitrary", "arbitrary")),
    )(c, *[p.reshape(N_CHIP, 2, p.shape[1], C) for p in pieces], other)


def _grad_cross(part):
    _, R, C = part.shape

    def body(p_ref, out_ref, send_sems, recv_sems):
        for phase in _cross_phases(p_ref, out_ref, send_sems, recv_sems):
            phase()

    return pl.pallas_call(
        body, name="grad_cross", out_shape=jax.ShapeDtypeStruct((3, R, C), part.dtype),
        in_specs=[pl.BlockSpec(memory_space=pl.ANY)], out_specs=pl.BlockSpec(memory_space=pl.ANY),
        scratch_shapes=CROSS_SEMS,
    )(part)


def _grad_sum(part, recv, my_chip, tag):
    _, R, C = part.shape
    tr = max(t for t in range(16, 513, 16) if R % t == 0)

    def body(chip_ref, p_ref, r_ref, o_ref):
        acc = p_ref[0].astype(f32)
        for j in range(3):
            acc = acc + r_ref[j].astype(f32)
        o_ref[...] = acc

    return pl.pallas_call(
        body, name=f"grad_sum_{tag}",
        grid_spec=pltpu.PrefetchScalarGridSpec(
            num_scalar_prefetch=1, grid=(R // tr,),
            in_specs=[pl.BlockSpec((1, tr, C), lambda i, chip_ref: (chip_ref[0], i, 0)),
                      pl.BlockSpec((3, tr, C), lambda i, chip_ref: (0, i, 0))],
            out_specs=pl.BlockSpec((tr, C), lambda i, chip_ref: (i, 0))),
        out_shape=jax.ShapeDtypeStruct((R, C), f32),
        compiler_params=_cparams(("arbitrary",)),
    )(my_chip, part, recv)


def _small_all_reduce(small):
    R, C = small.shape

    def body(x_ref, o_ref, buf, send_sems, recv_sems):
        _, me = _peer(0)
        buf[me] = x_ref[...]
        sends = []
        for k in range(1, N_DEV):
            dev, _ = _peer(k)
            cp = pltpu.make_async_remote_copy(src_ref=x_ref, dst_ref=buf.at[me], send_sem=send_sems.at[k - 1],
                                              recv_sem=recv_sems.at[k - 1], device_id=dev, device_id_type=MESH)
            cp.start()
            sends.append(cp)
        for k in range(1, N_DEV):
            dev, idx = _peer(k)
            pltpu.make_async_remote_copy(src_ref=x_ref, dst_ref=buf.at[idx], send_sem=send_sems.at[k - 1],
                                         recv_sem=recv_sems.at[k - 1], device_id=dev, device_id_type=MESH).wait_recv()
        for cp in sends:
            cp.wait_send()
        acc = buf[0]
        for i in range(1, N_DEV):
            acc = acc + buf[i]
        o_ref[...] = acc

    return pl.pallas_call(
        body, name="small_all_reduce", out_shape=jax.ShapeDtypeStruct((R, C), f32),
        in_specs=[pl.BlockSpec(memory_space=pltpu.VMEM)], out_specs=pl.BlockSpec(memory_space=pltpu.VMEM),
        scratch_shapes=[pltpu.VMEM((N_DEV, R, C), f32), pltpu.SemaphoreType.DMA((N_DEV - 1,)),
                        pltpu.SemaphoreType.DMA((N_DEV - 1,))],
    )(small)


_LORA = (("rwkv_w2", True), ("rwkv_a2", True), ("rwkv_g2", True))
_GROUPS_FIRST = ((("ffn1_w_in", True),),)
_GROUPS_MID = ((("ffn1_w_out", False),), _LORA)
_GROUPS_IN = ((("w_in", True),),)
_GROUPS_LATE = ((("w_proj_rwkv", False),), (("w_proj_attn", True),), (("w_out", False),),
                (("ffn2_w_in", True),), (("ffn2_w_out", False),))
_GRADS_MID = ((("w_in", True),), _LORA)
_GRADS_LAST = ((("ffn1_w_in", True),), (("ffn1_w_out", False),))
_BIG = tuple(item for group in _GROUPS_FIRST + _GROUPS_MID + _GROUPS_IN + _GROUPS_LATE for item in group)
_SMALL = ("ffn1_norm", "mix_norm", "b_gate", "rwkv_mu", "rwkv_w0", "rwkv_a0", "rwkv_k_k", "rwkv_k_a", "rwkv_r_k",
          "rwkv_ln_w", "rwkv_ln_b", "attn_q_norm", "attn_k_norm", "ffn2_norm")


def _pack_block(like, groups):
    rows = [sum(like[n].shape[0] * like[n].shape[1] // D for n, _ in group) for group in groups]
    g = functools.reduce(math.gcd, rows)
    return g if (g % 16 == 0 and g >= PACK_BLOCK) else PACK_BLOCK


def _pack_layout(like, groups):
    block = _pack_block(like, groups)
    items, spans, off = {}, [], 0
    for group in groups:
        start = off
        for name, _ in group:
            shp = like[name].shape
            n = shp[0] * shp[1] // D
            items[name] = (off, n)
            off += n
        off = -(-off // block) * block
        spans.append((start, off - start))
    return items, spans, off


def _pack_big(shards, groups):
    items, _, rows = _pack_layout(shards, groups)
    parts, at = [], 0
    for group in groups:
        for name, tr in group:
            off, n = items[name]
            t = shards[name]
            if off > at:
                parts.append(jnp.zeros((off - at, D), t.dtype))
            parts.append((t.T if tr else t).reshape(n, D))
            at = off + n
    if rows > at:
        parts.append(jnp.zeros((rows - at, D), parts[0].dtype))
    return jnp.concatenate(parts, axis=0)


def _unpack_big(pack, like, groups):
    items, _, _ = _pack_layout(like, groups)
    out = {}
    for group in groups:
        for name, tr in group:
            off, n = items[name]
            shp = like[name].shape
            t = pack[off:off + n]
            out[name] = t.reshape(shp[1], shp[0]).T if tr else t.reshape(shp)
    return out


def _unpack_gathered(gathered, like, groups):
    items, _, _ = _pack_layout(like, groups)
    full = {}
    for group in groups:
        for name, tr in group:
            shp = like[name].shape
            off, rows = items[name]
            r_loc, c_loc = (shp[1], shp[0]) if tr else shp
            full[name] = gathered[:, off:off + rows].reshape(N_DEV * r_loc, c_loc)
    return full


def _grad_pieces(g_full, like, groups):
    items, spans, _ = _pack_layout(like, groups)
    pieces = []
    for group, (_, rows_pad) in zip(groups, spans):
        parts = [g_full[n].astype(GRAD_WIRE).reshape(N_DEV, items[n][1], D) for n, _ in group]
        piece = parts[0] if len(parts) == 1 else jnp.concatenate(parts, axis=1)
        if rows_pad > piece.shape[1]:
            piece = jnp.pad(piece, ((0, 0), (0, rows_pad - piece.shape[1]), (0, 0)))
        pieces.append(piece)
    return pieces


def _small_rows(name, t):
    flat = t.reshape(-1)
    pad = (-flat.shape[0]) % D
    return jnp.pad(flat, (0, pad)).reshape(-1, D)


def _pack_small(vals):
    parts = [_small_rows(n, vals[n]) for n in _SMALL]
    used = sum(p.shape[0] for p in parts)
    parts.append(jnp.zeros((SMALL_ROWS - used, D), f32))
    return jnp.concatenate(parts, axis=0)


def _unpack_small(pack, like):
    out, off = {}, 0
    for n in _SMALL:
        size = like[n].size
        rows = -(-size // D)
        out[n] = pack[off:off + rows].reshape(-1)[:size].reshape(like[n].shape)
        off += rows
    return out


def _build_W_mid(full):
    dt = full["rwkv_w2"].dtype
    z64, z96 = jnp.zeros((64, D), dt), jnp.zeros((96, D), dt)
    return {
        "f1_o": full["ffn1_w_out"],
        "w2p": jnp.concatenate([full["rwkv_w2"].T, z64], axis=0),
        "a2p": jnp.concatenate([full["rwkv_a2"].T, z64], axis=0),
        "g2p": jnp.concatenate([full["rwkv_g2"].T, z96], axis=0),
    }


def _build_W_in(full):
    inT = full["w_in"]
    z64, z96 = jnp.zeros((64, D), inT.dtype), jnp.zeros((96, D), inT.dtype)
    return {
        "in_rkvT": inT[:3 * D],
        "in_loraT": jnp.concatenate([inT[3072:3136], z64, inT[3136:3200], z64, inT[3200:3360], z96], axis=0),
        "in_qkvT": inT[3360:3360 + 3 * ATTN_W], "in_gateT": inT[3360 + 3 * ATTN_W:],
    }


def _build_W_late(full):
    return {"pr": full["w_proj_rwkv"], "paT": full["w_proj_attn"], "out": full["w_out"],
            "f2_iT": full["ffn2_w_in"], "f2_o": full["ffn2_w_out"]}


def _build_W_first(full):
    return {"f1_iT": full["ffn1_w_in"]}


def _build_W(full):
    return {**_build_W_first(full), **_build_W_mid(full), **_build_W_in(full), **_build_W_late(full)}


_G_NAMES = {"ffn1_w_in": "f1_iT", "ffn1_w_out": "f1_o", "w_in": "inT", "rwkv_w2": "w2T", "rwkv_a2": "a2T",
            "rwkv_g2": "g2T", "w_proj_rwkv": "pr", "w_proj_attn": "paT", "w_out": "out", "ffn2_w_in": "f2_iT",
            "ffn2_w_out": "f2_o"}


def _named_grads(G, groups):
    return {n: G[_G_NAMES[n]] for group in groups for n, _ in group}


def _reduce_start(G, like, groups, my_c, tag):
    pieces = _grad_pieces(_named_grads(G, groups), like, groups)
    return _pair_add(pieces, _grad_pair(pieces, tag), my_c, tag)


def _build_P(Wl):
    mu = Wl["rwkv_mu"]
    z64f, z96f = jnp.zeros((1, 64), f32), jnp.zeros((1, 96), f32)
    return {
        "ffn1_norm": Wl["ffn1_norm"][None], "mix_norm": Wl["mix_norm"][None], "ffn2_norm": Wl["ffn2_norm"][None],
        "b_gate": Wl["b_gate"][None], "mu_rkv": mu[None, :3 * D],
        "mu_lora": jnp.concatenate([mu[None, 3072:3136], z64f, mu[None, 3136:3200], z64f, mu[None, 3200:3360], z96f], axis=1),
        "w0": Wl["rwkv_w0"][None], "a0": Wl["rwkv_a0"][None], "k_k": Wl["rwkv_k_k"][None], "k_a": Wl["rwkv_k_a"][None],
        "r_k": Wl["rwkv_r_k"].reshape(1, D), "ln_w": Wl["rwkv_ln_w"][None], "ln_b": Wl["rwkv_ln_b"][None],
        "q_norm": Wl["attn_q_norm"][None], "k_norm": Wl["attn_k_norm"][None],
    }


def kernel(x, ffn1_norm, ffn1_w_in, ffn1_w_out, mix_norm, w_in, b_gate, rwkv_mu, rwkv_w0, rwkv_w2, rwkv_a0, rwkv_a2, rwkv_g2, rwkv_k_k, rwkv_k_a, rwkv_r_k, rwkv_ln_w, rwkv_ln_b, attn_q_norm, attn_k_norm, w_proj_rwkv, w_proj_attn, w_out, ffn2_norm, ffn2_w_in, ffn2_w_out, loss_target, m_ffn1_norm, m_ffn1_w_in, m_ffn1_w_out, m_mix_norm, m_w_in, m_b_gate, m_rwkv_mu, m_rwkv_w0, m_rwkv_w2, m_rwkv_a0, m_rwkv_a2, m_rwkv_g2, m_rwkv_k_k, m_rwkv_k_a, m_rwkv_r_k, m_rwkv_ln_w, m_rwkv_ln_b, m_attn_q_norm, m_attn_k_norm, m_w_proj_rwkv, m_w_proj_attn, m_w_out, m_ffn2_norm, m_ffn2_w_in, m_ffn2_w_out, v_ffn1_norm, v_ffn1_w_in, v_ffn1_w_out, v_mix_norm, v_w_in, v_b_gate, v_rwkv_mu, v_rwkv_w0, v_rwkv_w2, v_rwkv_a0, v_rwkv_a2, v_rwkv_g2, v_rwkv_k_k, v_rwkv_k_a, v_rwkv_r_k, v_rwkv_ln_w, v_rwkv_ln_b, v_attn_q_norm, v_attn_k_norm, v_w_proj_rwkv, v_w_proj_attn, v_w_out, v_ffn2_norm, v_ffn2_w_in, v_ffn2_w_out):
    names = ("ffn1_norm", "ffn1_w_in", "ffn1_w_out", "mix_norm", "w_in", "b_gate", "rwkv_mu", "rwkv_w0", "rwkv_w2",
             "rwkv_a0", "rwkv_a2", "rwkv_g2", "rwkv_k_k", "rwkv_k_a", "rwkv_r_k", "rwkv_ln_w", "rwkv_ln_b",
             "attn_q_norm", "attn_k_norm", "w_proj_rwkv", "w_proj_attn", "w_out", "ffn2_norm", "ffn2_w_in", "ffn2_w_out")
    w_all = (ffn1_norm, ffn1_w_in, ffn1_w_out, mix_norm, w_in, b_gate, rwkv_mu, rwkv_w0, rwkv_w2, rwkv_a0, rwkv_a2,
             rwkv_g2, rwkv_k_k, rwkv_k_a, rwkv_r_k, rwkv_ln_w, rwkv_ln_b, attn_q_norm, attn_k_norm, w_proj_rwkv,
             w_proj_attn, w_out, ffn2_norm, ffn2_w_in, ffn2_w_out)
    m_all = (m_ffn1_norm, m_ffn1_w_in, m_ffn1_w_out, m_mix_norm, m_w_in, m_b_gate, m_rwkv_mu, m_rwkv_w0, m_rwkv_w2,
             m_rwkv_a0, m_rwkv_a2, m_rwkv_g2, m_rwkv_k_k, m_rwkv_k_a, m_rwkv_r_k, m_rwkv_ln_w, m_rwkv_ln_b,
             m_attn_q_norm, m_attn_k_norm, m_w_proj_rwkv, m_w_proj_attn, m_w_out, m_ffn2_norm, m_ffn2_w_in, m_ffn2_w_out)
    v_all = (v_ffn1_norm, v_ffn1_w_in, v_ffn1_w_out, v_mix_norm, v_w_in, v_b_gate, v_rwkv_mu, v_rwkv_w0, v_rwkv_w2,
             v_rwkv_a0, v_rwkv_a2, v_rwkv_g2, v_rwkv_k_k, v_rwkv_k_a, v_rwkv_r_k, v_rwkv_ln_w, v_rwkv_ln_b,
             v_attn_q_norm, v_attn_k_norm, v_w_proj_rwkv, v_w_proj_attn, v_w_out, v_ffn2_norm, v_ffn2_w_in, v_ffn2_w_out)
    Wl = {n: t[0] for n, t in zip(names, w_all)}
    Ml = {n: t[0] for n, t in zip(names, m_all)}
    Vl = {n: t[0] for n, t in zip(names, v_all)}
    big = [n for n, _ in _BIG]

    my_c = lax.axis_index("c").astype(jnp.int32).reshape(1)
    my_chip = (2 * lax.axis_index("x") + lax.axis_index("y")).astype(jnp.int32).reshape(1)

    def pack(groups):
        return _pack_big(Wl, groups).astype(bf16)

    gathered = _all_gather(pack(_GROUPS_FIRST))
    W, P = _build_W_first(_unpack_gathered(gathered, Wl, _GROUPS_FIRST)), _build_P(Wl)
    hooks = {"gather_mid": (pack(_GROUPS_MID), lambda g: _build_W_mid(_unpack_gathered(g, Wl, _GROUPS_MID))),
             "gather_in": (pack(_GROUPS_IN), lambda g: _build_W_in(_unpack_gathered(g, Wl, _GROUPS_IN))),
             "gather_late": (pack(_GROUPS_LATE), lambda g: _build_W_late(_unpack_gathered(g, Wl, _GROUPS_LATE))),
             "reduce_mid": lambda G: _reduce_start(G, Wl, _GRADS_MID, my_c, "mid"),
             "reduce_late": lambda G: _grad_pieces(_named_grads(G, _GROUPS_LATE), Wl, _GROUPS_LATE)}

    loss_local, dx0, G, Gs = _local_step(x[0], loss_target[0], W, P, hooks)

    part_last = _reduce_start(G, Wl, _GRADS_LAST, my_c, "last")
    g_big = _unpack_big(_grad_sum(part_last, _grad_cross(part_last), my_chip, "last"), Wl, _GRADS_LAST)
    g_big.update(_unpack_big(_grad_sum(*G["mid"], my_chip, "mid"), Wl, _GRADS_MID))
    me = (4 * lax.axis_index("x") + 2 * lax.axis_index("y") + lax.axis_index("c")).astype(jnp.int32).reshape(1)
    g_big.update(_unpack_big(_sum_direct(*G["late"], me, "late"), Wl, _GROUPS_LATE))

    mu_g = Gs["mu_rkv"], Gs["mu_lora"]
    o1, o2 = LORA_PAD[0], LORA_PAD[0] + LORA_PAD[1]
    g_small_local = {
        "ffn1_norm": Gs["ffn1_norm"], "mix_norm": Gs["mix_norm"], "b_gate": Gs["b_gate"],
        "rwkv_mu": jnp.concatenate([mu_g[0], mu_g[1][:, :64], mu_g[1][:, o1:o1 + 64], mu_g[1][:, o2:o2 + 160]], axis=1),
        "rwkv_w0": Gs["w0"], "rwkv_a0": Gs["a0"], "rwkv_k_k": Gs["k_k"], "rwkv_k_a": Gs["k_a"], "rwkv_r_k": Gs["r_k"],
        "rwkv_ln_w": Gs["ln_w"], "rwkv_ln_b": Gs["ln_b"], "attn_q_norm": Gs["q_norm"], "attn_k_norm": Gs["k_norm"],
        "ffn2_norm": Gs["ffn2_norm"]}
    gs_pack = _small_all_reduce(_pack_small(g_small_local))

    out_g, out_d, out_m, out_v = dict(g_big), {}, {}, {}
    for n in big:
        cols = Wl[n].shape[1]
        out_d[n], out_m[n], out_v[n] = _rowwise(f"adamw_{n}", _f_adamw, [Wl[n], g_big[n], Ml[n], Vl[n]], [],
                                                 [(cols, f32)] * 3)
    ds_pack, ms_pack, vs_pack = _rowwise(
        "adamw_small", _f_adamw, [_pack_small(Wl), gs_pack, _pack_small(Ml), _pack_small(Vl)], [], [(D, f32)] * 3)
    for out, pack in ((out_g, gs_pack), (out_d, ds_pack), (out_m, ms_pack), (out_v, vs_pack)):
        out.update(_unpack_small(pack, Wl))

    loss = lax.psum(loss_local, ("x", "y", "c"))
    return (loss, dx0[None], *[out_g[n][None] for n in names], *[out_d[n][None] for n in names],
            *[out_m[n][None] for n in names], *[out_v[n][None] for n in names])
```

```python
import functools
import math

import jax
import jax.numpy as jnp
from jax import lax
from jax.experimental import pallas as pl
from jax.experimental.pallas import tpu as pltpu

f32 = jnp.float32
bf16 = jnp.bfloat16
MESH = pl.DeviceIdType.MESH

N_DEV = 8
D = 1024
D_FF = 2816
HEAD = 64
RW_HEADS = 16
ATTN_PAIRS = ((128, 1), (512, 4), (2048, 16))
ATTN_BLK = 128
ATTN_W = 768
LORA_PAD = (128, 128, 256)
LORA_W = (64, 64, 160)
GN_EPS = 64e-5
RMS_EPS = 1e-6
NEG_INF = -1e30
WKV_T = 64
WKV_SUB = 2
GRAD_WIRE = bf16
PACK_BLOCK = 128
SMALL_ROWS = 24
VMEM_LIMIT = 56 * 1024 * 1024

ADAM_LR, ADAM_B1, ADAM_B2, ADAM_EPS, ADAM_WD, ADAM_STEP = 0.001, 0.9, 0.999, 1e-08, 0.01, 10


def _cparams(sem):
    return pltpu.CompilerParams(dimension_semantics=sem, vmem_limit_bytes=VMEM_LIMIT)


HALO = 8


def _rowwise(name, fn, rows, params, outs, accs=(), tm=256, halos=(), carries=(), reverse=False):
    S = rows[0].shape[0]
    tm = min(tm, S)
    while S % tm:
        tm -= 8
    nb = S // tm
    n_in = len(rows) + len(params) + len(halos)
    n_out = len(outs)
    n_acc = len(accs)
    n_car = len(carries)

    def blk_of(i):
        return nb - 1 - i if reverse else i

    def body(*refs):
        step = pl.program_id(0)
        carry_refs = refs[n_in + n_out + n_acc:]
        if n_car:
            @pl.when(step == 0)
            def _():
                for c_ref in carry_refs:
                    c_ref[...] = jnp.zeros(c_ref.shape, f32)
        args = [r[...] for r in refs[:n_in]] + [c[...] for c in carry_refs]
        res = fn(*args, blk=blk_of(step)) if (halos or carries) else fn(*args)
        if not isinstance(res, (tuple, list)):
            res = (res,)
        out_refs = refs[n_in:n_in + n_out + n_acc]
        for j in range(n_out):
            out_refs[j][...] = res[j].astype(out_refs[j].dtype)
        if n_acc:
            @pl.when(step == 0)
            def _():
                for j in range(n_acc):
                    out_refs[n_out + j][...] = jnp.zeros(out_refs[n_out + j].shape, f32)
            for j in range(n_acc):
                out_refs[n_out + j][...] += res[n_out + j]
        for j in range(n_car):
            carry_refs[j][...] = res[n_out + n_acc + j]

    in_specs = [pl.BlockSpec((tm, a.shape[1]), lambda i: (blk_of(i), 0)) for a in rows]
    in_specs += [pl.BlockSpec(p.shape, lambda i, nd=p.ndim: (0,) * nd) for p in params]
    in_specs += [pl.BlockSpec((HALO, rows[h].shape[1]), lambda i: (jnp.maximum(blk_of(i) * (tm // HALO) - 1, 0), 0))
                 for h in halos]
    out_specs = [pl.BlockSpec((tm, w), lambda i: (blk_of(i), 0)) for w, _ in outs]
    out_specs += [pl.BlockSpec(s, lambda i: (0, 0)) for s in accs]
    out_shape = [jax.ShapeDtypeStruct((S, w), dt) for w, dt in outs]
    out_shape += [jax.ShapeDtypeStruct(s, f32) for s in accs]
    res = pl.pallas_call(
        body, name=name, grid=(nb,), in_specs=in_specs, out_specs=out_specs, out_shape=out_shape,
        scratch_shapes=[pltpu.VMEM(s, f32) for s in carries],
        compiler_params=_cparams(("arbitrary",)),
    )(*rows, *params, *[rows[h] for h in halos])
    return res


MM_VMEM_BUDGET = 40 * 1024 * 1024
MM_STEP_US = 0.35
MM_FLOPS_PER_US = 9.0e8
MM_HBM_BYTES_PER_US = 3.0e6


def _tile_options(n, cap):
    opts = [d for d in range(128, min(n, cap) + 1, 128) if n % d == 0]
    return opts or [n]


def _mm_tiles(M, N, K, sa, sb, so, whole_rows=False):
    best, best_cost = None, None
    for tm in _tile_options(M, 512 if whole_rows else 2048):
        for tn in ([N] if whole_rows else _tile_options(N, 2048)):
            for tk in _tile_options(K, 4096):
                vmem = 2 * (tm * tk * sa + tk * tn * sb) + 2 * tm * tn * so + (tm * tn * 4 if tk < K else 0)
                if vmem > MM_VMEM_BUDGET:
                    continue
                steps = (M // tm) * (N // tn) * (K // tk)
                traffic = M * K * sa * (N // tn) + K * N * sb * (M // tm) + M * N * so
                cost = (max(2.0 * M * N * K / MM_FLOPS_PER_US, traffic / MM_HBM_BYTES_PER_US) + steps * MM_STEP_US
                        + (tm * tk * sa + tk * tn * sb) / MM_HBM_BYTES_PER_US)
                if best_cost is None or cost < best_cost:
                    best, best_cost = (tm, tn, tk), cost
    return best


def _mm(name, a, b, mode, out_dtype=f32, scale=None, gather=None, cross=None, epilogue=None):
    halves = a.ndim == 3
    sizes = (jnp.dtype(a.dtype).itemsize, jnp.dtype(b.dtype).itemsize, jnp.dtype(out_dtype).itemsize)
    whole = epilogue is not None
    if mode == "nn":
        (M, K), N = (a.shape[1], 2 * a.shape[2]) if halves else a.shape, b.shape[1]
        tm, tn, tk = _mm_tiles(M, N, K // 2 if halves else K, *sizes, whole_rows=whole)
    elif mode == "nt":
        (M, K), N = a.shape, b.shape[0]
        tm, tn, tk = _mm_tiles(M, N, K, *sizes, whole_rows=whole)
    else:
        (K, M), N = (a.shape[1], 2 * a.shape[2]) if halves else a.shape, b.shape[1]
        tm, tn, tk = _mm_tiles(M // 2 if halves else M, N, K, *sizes, whole_rows=whole)
    nk = K // tk
    if mode == "nn":
        per = K // 2 // tk
        a_spec = (pl.BlockSpec((None, tm, tk), lambda i, j, k: (k // per, i, k % per)) if halves
                  else pl.BlockSpec((tm, tk), lambda i, j, k: (i, k)))
        b_spec = pl.BlockSpec((tk, tn), lambda i, j, k: (k, j))
        dims = (((1,), (0,)), ((), ()))
    elif mode == "nt":
        a_spec = pl.BlockSpec((tm, tk), lambda i, j, k: (i, k))
        b_spec = pl.BlockSpec((tn, tk), lambda i, j, k: (j, k))
        dims = (((1,), (1,)), ((), ()))
    else:
        per = M // 2 // tm
        a_spec = (pl.BlockSpec((None, tk, tm), lambda i, j, k: (i // per, k, i % per)) if halves
                  else pl.BlockSpec((tk, tm), lambda i, j, k: (k, i)))
        b_spec = pl.BlockSpec((tk, tn), lambda i, j, k: (k, j))
        dims = (((0,), (0,)), ((), ()))

    hosted = gather if gather is not None else cross
    grid = (M // tm, N // tn, nk)
    steps = grid[0] * grid[1] * grid[2]
    ep_rows, ep_params, ep_outs, ep_accs = ([], [], [], []) if epilogue is None else epilogue[1:]
    n_ep_in, n_ep_out = len(ep_rows) + len(ep_params), len(ep_outs) + len(ep_accs)
    assert epilogue is None or tn == N

    def body(a_ref, b_ref, *rest):
        rest = list(rest)
        src_ref = rest.pop(0) if hosted is not None else None
        ep_in, rest = rest[:n_ep_in], rest[n_ep_in:]
        if epilogue is None:
            o_ref = rest.pop(0)
        else:
            out_refs, rest = rest[:n_ep_out], rest[n_ep_out:]
        dst_ref = rest.pop(0) if hosted is not None else None
        scratch = rest
        step = (pl.program_id(0) * grid[1] + pl.program_id(1)) * grid[2] + pl.program_id(2)
        if hosted is not None:
            n_sem = len(GATHER_SEMS if gather is not None else CROSS_SEMS)
            sems, scratch = scratch[len(scratch) - n_sem:], scratch[:len(scratch) - n_sem]
            if gather is not None:
                start, forward, done = _gather_phases(src_ref, dst_ref, *sems)
                pl.when(step == steps // 2)(forward)
            else:
                start, done = _cross_phases(src_ref, dst_ref, *sems)
            pl.when(step == 0)(start)
        part = lax.dot_general(a_ref[...].astype(bf16), b_ref[...].astype(bf16), dims,
                               preferred_element_type=f32)

        def finish(acc):
            if epilogue is None:
                o_ref[...] = (acc if scale is None else acc * scale).astype(o_ref.dtype)
                return
            res = epilogue[0](acc, *[r[...] for r in ep_in])
            for j in range(len(ep_outs)):
                out_refs[j][...] = res[j].astype(out_refs[j].dtype)
            for j in range(len(ep_accs)):
                acc_out = out_refs[len(ep_outs) + j]

                @pl.when(step == nk - 1)
                def _(acc_out=acc_out):
                    acc_out[...] = jnp.zeros(acc_out.shape, f32)
                acc_out[...] += res[len(ep_outs) + j]

        if nk == 1:
            finish(part)
        else:
            acc_ref = scratch[0]
            k = pl.program_id(2)

            @pl.when(k == 0)
            def _():
                acc_ref[...] = part

            @pl.when(k > 0)
            def _():
                acc_ref[...] += part

            @pl.when(k == nk - 1)
            def _():
                finish(acc_ref[...])
        if hosted is not None:
            pl.when(step == steps - 1)(done)

    hbm = pl.BlockSpec(memory_space=pl.ANY)
    in_specs = [a_spec, b_spec] + [hbm] * (hosted is not None)
    in_specs += [pl.BlockSpec((tm, r.shape[1]), lambda i, j, k: (i, 0)) for r in ep_rows]
    in_specs += [pl.BlockSpec(p.shape, lambda i, j, k: (0, 0)) for p in ep_params]
    if epilogue is None:
        out_specs = [pl.BlockSpec((tm, tn), lambda i, j, k: (i, j))]
        out_shape = [jax.ShapeDtypeStruct((M, N), out_dtype)]
    else:
        out_specs = [pl.BlockSpec((tm, w), lambda i, j, k: (i, 0)) for w, _ in ep_outs]
        out_specs += [pl.BlockSpec(s, lambda i, j, k: (0, 0)) for s in ep_accs]
        out_shape = [jax.ShapeDtypeStruct((M, w), dt) for w, dt in ep_outs]
        out_shape += [jax.ShapeDtypeStruct(s, f32) for s in ep_accs]
    scratch_shapes = [] if nk == 1 else [pltpu.VMEM((tm, tn), f32)]
    if gather is not None:
        out_specs.append(hbm)
        out_shape.append(jax.ShapeDtypeStruct((N_DEV,) + gather.shape, gather.dtype))
        scratch_shapes = scratch_shapes + GATHER_SEMS
    elif cross is not None:
        out_specs.append(hbm)
        out_shape.append(jax.ShapeDtypeStruct((3,) + cross.shape[1:], cross.dtype))
        scratch_shapes = scratch_shapes + CROSS_SEMS
    sequential = hosted is not None or ep_accs
    res = pl.pallas_call(
        body, name=name, grid=grid, in_specs=in_specs,
        out_specs=out_specs, out_shape=out_shape, scratch_shapes=scratch_shapes,
        compiler_params=_cparams(("arbitrary",) * 3 if sequential else ("parallel", "parallel", "arbitrary")),
    )(a, b, *([hosted] if hosted is not None else []), *ep_rows, *ep_params)
    return res[0] if (hosted is None and epilogue is None) else res


FFN_TM, FFN_TN = 512, 1408
FFN_SAVE = bf16


def _ffn_up(name, n, WiT, gather=None):
    S = n.shape[0]
    grid = (S // FFN_TM, D_FF // FFN_TN)
    steps = grid[0] * grid[1]

    def body(n_ref, wg_ref, wu_ref, *rest):
        if gather is None:
            g_ref, u_ref, act_ref = rest
        else:
            src_ref, g_ref, u_ref, act_ref, dst_ref, *sems = rest
            step = pl.program_id(0) * grid[1] + pl.program_id(1)
            start, forward, done = _gather_phases(src_ref, dst_ref, *sems)
            pl.when(step == 0)(start)
            pl.when(step == steps // 2)(forward)
        x = n_ref[...]
        gate = lax.dot_general(x, wg_ref[...], _NT, preferred_element_type=f32)
        up = lax.dot_general(x, wu_ref[...], _NT, preferred_element_type=f32)
        g_ref[...] = gate.astype(g_ref.dtype)
        u_ref[...] = up.astype(u_ref.dtype)
        act_ref[...] = (gate * jax.nn.sigmoid(gate) * up).astype(act_ref.dtype)
        if gather is not None:
            pl.when(step == steps - 1)(done)

    hbm = pl.BlockSpec(memory_space=pl.ANY)
    tile = pl.BlockSpec((FFN_TM, FFN_TN), lambda i, j: (i, j))
    in_specs = [pl.BlockSpec((FFN_TM, D), lambda i, j: (i, 0)), pl.BlockSpec((FFN_TN, D), lambda i, j: (j, 0)),
                pl.BlockSpec((FFN_TN, D), lambda i, j: (j + D_FF // FFN_TN, 0))]
    out_specs = [tile, tile, tile]
    out_shape = [jax.ShapeDtypeStruct((S, D_FF), FFN_SAVE), jax.ShapeDtypeStruct((S, D_FF), FFN_SAVE),
                 jax.ShapeDtypeStruct((S, D_FF), bf16)]
    if gather is not None:
        in_specs.append(hbm)
        out_specs.append(hbm)
        out_shape.append(jax.ShapeDtypeStruct((N_DEV,) + gather.shape, gather.dtype))
    return pl.pallas_call(
        body, name=name, grid=grid, in_specs=in_specs, out_specs=out_specs, out_shape=out_shape,
        scratch_shapes=GATHER_SEMS if gather is not None else [],
        compiler_params=_cparams(("arbitrary", "arbitrary")),
    )(n, WiT, WiT, *([gather] if gather is not None else []))


def _ffn_dact(name, dy, Wo, gate, up):
    S = dy.shape[0]

    def body(dy_ref, wo_ref, g_ref, u_ref, d_ref):
        dact = 0.5 * lax.dot_general(dy_ref[...], wo_ref[...], _NT, preferred_element_type=f32)
        gate, up = g_ref[...].astype(f32), u_ref[...].astype(f32)
        sg = jax.nn.sigmoid(gate)
        d_ref[0] = (dact * up * (sg * (1.0 + gate * (1.0 - sg)))).astype(d_ref.dtype)
        d_ref[1] = (dact * gate * sg).astype(d_ref.dtype)

    tile = pl.BlockSpec((FFN_TM, FFN_TN), lambda i, j: (i, j))
    return pl.pallas_call(
        body, name=name, grid=(S // FFN_TM, D_FF // FFN_TN),
        in_specs=[pl.BlockSpec((FFN_TM, D), lambda i, j: (i, 0)), pl.BlockSpec((FFN_TN, D), lambda i, j: (j, 0)), tile, tile],
        out_specs=pl.BlockSpec((2, FFN_TM, FFN_TN), lambda i, j: (0, i, j)),
        out_shape=jax.ShapeDtypeStruct((2, S, D_FF), bf16),
        compiler_params=_cparams(("parallel", "parallel")),
    )(dy, Wo, gate, up)


def _sp(x):
    hi = x.astype(bf16)
    return hi, (x - hi.astype(f32)).astype(bf16)


def _cat(parts):
    return tuple(jnp.concatenate(p, axis=1) for p in zip(*parts))


def _bmm(eq, a, b):
    (ah, al), (bh, bl) = a, b
    dot = functools.partial(jnp.einsum, eq, preferred_element_type=f32)
    return dot(ah, bh) + (dot(ah, bl) + dot(al, bh))


def _tri_dot(eq, tri, x):
    h1 = x.astype(bf16)
    r1 = x - h1.astype(f32)
    h2 = r1.astype(bf16)
    h3 = (r1 - h2.astype(f32)).astype(bf16)
    dot = functools.partial(jnp.einsum, eq, preferred_element_type=f32)
    return dot(tri, h1) + (dot(tri, h2) + dot(tri, h3))


def _tri_masks(T):
    ti = lax.broadcasted_iota(jnp.int32, (T, T), 0)
    si = lax.broadcasted_iota(jnp.int32, (T, T), 1)
    return ti >= si, ti > si


def _wkv_prep(r, lw, k, kkr, a):
    H, T, _ = r.shape
    low_i, low_s = _tri_masks(T)
    nrm = jnp.sqrt(jnp.sum(kkr * kkr, axis=-1, keepdims=True))
    den = jnp.maximum(nrm, 1e-12)
    kk = kkr / den
    tri = jnp.broadcast_to(low_i.astype(bf16)[None], (H, T, T))
    cl = _tri_dot("hts,hsn->htn", tri, lw)
    c = jnp.exp(cl)
    cprev = jnp.exp(cl - lw)
    cinv = jnp.exp(-cl)
    bt, kt = _sp(kk * a * cinv), _sp(k * cinv)
    L = _cat([_sp(r * c), _sp(-kk * cprev)])
    Mb = _bmm("htn,hsn->hts", L, bt)
    Mk = _bmm("htn,hsn->hts", L, kt)
    A_rb = jnp.where(low_i[None], Mb[:, :T], 0.0)
    A_ab = jnp.where(low_s[None], Mb[:, T:], 0.0)
    Mk = jnp.concatenate([jnp.where(low_i[None], Mk[:, :T], 0.0), jnp.where(low_s[None], Mk[:, T:], 0.0)], axis=1)
    return dict(kk=kk, den=den, nrm=nrm, c=c, cprev=cprev, cinv=cinv, L=L, kt=kt, bt=bt,
                A_ab=A_ab, A_rb=A_rb, Mk=Mk, cT=c[:, T - 1:T, :])


def _tri_inverse(A):
    T = A.shape[-1]
    eye = (lax.broadcasted_iota(jnp.int32, (T, T), 0) == lax.broadcasted_iota(jnp.int32, (T, T), 1)).astype(f32)
    inv = eye[None] + A
    X = A
    n = 1
    while 2 * n < T:
        Xs = _sp(X)
        X = _bmm("hts,hsu->htu", Xs, Xs)
        inv = inv + _bmm("hts,hsu->htu", _sp(inv), _sp(X))
        n *= 2
    return inv


def _wkv_chunk_fwd(S0, r, lw, k, v, kkr, a):
    T = r.shape[1]
    q = _wkv_prep(r, lw, k, kkr, a)
    inv = _tri_inverse(q["A_ab"])
    vs = _sp(v)
    P = _bmm("htk,hvk->htv", q["L"], _sp(S0)) + _bmm("hts,hsv->htv", _sp(q["Mk"]), vs)
    U = _bmm("hts,hsv->htv", _sp(inv), _sp(P[:, T:]))
    Us = _sp(U)
    Y = P[:, :T] + _bmm("hts,hsv->htv", _sp(q["A_rb"]), Us)
    S1 = (S0 + _bmm("htv,htk->hvk", _cat([Us, vs]), _cat([q["bt"], q["kt"]]))) * q["cT"]
    return Y, U, inv, S1


def _wkv_chunk_bwd(S0, Hin, Q, r, lw, k, v, kkr, a, U, inv, dY):
    H, T, _ = r.shape
    low_i, low_s = _tri_masks(T)
    q = _wkv_prep(r, lw, k, kkr, a)
    L, kt, bt = q["L"], q["kt"], q["bt"]
    R = _cat([bt, kt])
    Hh = Hin * q["cT"]
    Hs, S0s, dYs, vs, Us = _sp(Hh), _sp(S0), _sp(dY), _sp(v), _sp(U)
    RH = _bmm("htk,hvk->htv", R, Hs)
    Z = _bmm("hst,hsv->htv", _sp(inv), _sp(RH[:, :T] + _bmm("hst,hsv->htv", _sp(q["A_rb"]), dYs)))
    DZ = _cat([dYs, _sp(Z)])
    both = jnp.concatenate([jnp.broadcast_to(low_i[None], (1, T, T)), jnp.broadcast_to(low_s[None], (1, T, T))], axis=1)
    NU = _sp(jnp.where(both, _bmm("htv,hsv->hts", DZ, Us), 0.0))
    NV = _sp(jnp.where(both, _bmm("htv,hsv->hts", DZ, vs), 0.0))
    ra = _bmm("htv,hvk->htk", DZ, S0s) + _bmm("hts,hsk->htk", NU, bt) + _bmm("hts,hsk->htk", NV, kt)
    dr = ra[:, :T] * q["c"]
    da = ra[:, T:] * q["cprev"]
    dv = RH[:, T:] + _bmm("hst,hsv->htv", _sp(q["Mk"]), DZ)
    VH = _bmm("htv,hvk->htk", _cat([vs, Us]), Hs)
    dk = (VH[:, :T] + _bmm("hst,hsk->htk", NV, L)) * q["cinv"]
    db = (VH[:, T:] + _bmm("hst,hsk->htk", NU, L)) * q["cinv"]
    H0 = Hh + _bmm("htv,htk->hvk", DZ, L)
    kk = q["kk"]
    e = r * dr - kk * a * db - k * dk
    f = -kk * da
    tri_i = jnp.broadcast_to(low_i.astype(bf16)[None], (H, T, T))
    tri_s = jnp.broadcast_to(low_s.astype(bf16)[None], (H, T, T))
    dlw = _tri_dot("hst,hsn->htn", tri_i, e) + _tri_dot("hst,hsn->htn", tri_s, f) + Q
    Qn = Q + jnp.sum(e + f, axis=1, keepdims=True)
    dkk = db * a - da
    dasig = db * kk
    proj = jnp.sum(dkk * kk, axis=-1, keepdims=True)
    dkkr = jnp.where(q["nrm"] > 1e-12, dkk - kk * proj, dkk) / q["den"]
    return dr, dlw, dk, dv, dkkr, dasig, H0, Qn


def _heads(ref, rows=slice(None)):
    return jnp.stack([ref[rows, h * HEAD:(h + 1) * HEAD] for h in range(RW_HEADS)], axis=0)


def _put_heads(ref, val, rows=slice(None)):
    for h in range(RW_HEADS):
        ref[rows, h * HEAD:(h + 1) * HEAD] = val[h]


def _wkv_fwd(r, lw, k, v, kkr, a, g, r_k, ln_w, ln_b, late_pack=None):
    S = r.shape[0]
    H, N, T = RW_HEADS, HEAD, WKV_T
    TS = T * WKV_SUB
    nc = S // TS
    hosting = late_pack is not None

    def body(r_ref, lw_ref, k_ref, v_ref, kkr_ref, a_ref, g_ref, rk_ref, lnw_ref, lnb_ref, *rest):
        if hosting:
            pack_ref, y_ref, yg_ref, wkv_ref, u_ref, inv_ref, s0_ref, gathered_ref, state, *sems = rest
            start, forward, finish = _gather_phases(pack_ref, gathered_ref, *sems)
            pl.when(pl.program_id(0) == 0)(start)
            pl.when(pl.program_id(0) == nc // 2)(forward)
        else:
            y_ref, yg_ref, wkv_ref, u_ref, inv_ref, s0_ref, state = rest

        @pl.when(pl.program_id(0) == 0)
        def _():
            state[...] = jnp.zeros(state.shape, f32)

        S0 = state[...]
        for c in range(WKV_SUB):
            rows = slice(c * T, (c + 1) * T)
            s0_ref[c] = S0
            rr, kk2, vv = _heads(r_ref, rows), _heads(k_ref, rows), _heads(v_ref, rows)
            Y, U, inv, S0 = _wkv_chunk_fwd(S0, rr, _heads(lw_ref, rows), kk2, vv, _heads(kkr_ref, rows),
                                           _heads(a_ref, rows))
            wkv_ref[:, rows, :] = Y
            u_ref[:, rows, :] = U
            inv_ref[:, rows, :] = inv
            mean = jnp.mean(Y, axis=-1, keepdims=True)
            var = jnp.mean(jnp.square(Y - mean), axis=-1, keepdims=True)
            yn = (Y - mean) * lax.rsqrt(var + GN_EPS)
            bonus = jnp.sum(rr * kk2 * rk_ref[...], axis=-1, keepdims=True) * vv
            _put_heads(y_ref, yn * lnw_ref[...] + lnb_ref[...] + bonus, rows)
        state[...] = S0
        yg_ref[...] = (y_ref[...] * g_ref[...]).astype(yg_ref.dtype)
        if hosting:
            pl.when(pl.program_id(0) == nc - 1)(finish)

    tok = pl.BlockSpec((TS, H * N), lambda i: (i, 0))
    blk = pl.BlockSpec((H, TS, N), lambda i: (0, i, 0))
    par = pl.BlockSpec((H, 1, N), lambda i: (0, 0, 0))
    hbm = pl.BlockSpec(memory_space=pl.ANY)
    seq = jax.ShapeDtypeStruct((H, S, N), f32)
    out_specs = [tok, tok, blk, blk, pl.BlockSpec((H, TS, T), lambda i: (0, i, 0)),
                 pl.BlockSpec((WKV_SUB, H, N, N), lambda i: (i, 0, 0, 0))]
    out_shape = [jax.ShapeDtypeStruct((S, H * N), f32), jax.ShapeDtypeStruct((S, H * N), bf16), seq, seq,
                 jax.ShapeDtypeStruct((H, S, T), f32),
                 jax.ShapeDtypeStruct((S // T, H, N, N), f32)]
    if hosting:
        out_specs.append(hbm)
        out_shape.append(jax.ShapeDtypeStruct((N_DEV,) + late_pack.shape, late_pack.dtype))
    return pl.pallas_call(
        body, name="wkv_fwd", grid=(nc,), in_specs=[tok] * 7 + [par] * 3 + [hbm] * hosting,
        out_specs=out_specs, out_shape=out_shape,
        scratch_shapes=[pltpu.VMEM((H, N, N), f32)] + (GATHER_SEMS if hosting else []),
        compiler_params=_cparams(("arbitrary",)),
    )(r, lw, k, v, kkr, a, g, r_k, ln_w, ln_b, *([late_pack] if hosting else []))


def _wkv_bwd(dy, g, r, lw, k, v, kkr, a, wkv, U, inv, S0s, r_k, ln_w, ln_b, late_pieces=None):
    S = r.shape[0]
    H, N, T = RW_HEADS, HEAD, WKV_T
    TS = T * WKV_SUB
    nc = S // TS
    hosting = late_pieces is not None
    n_late = len(late_pieces) if hosting else 0

    def body(dy_ref, g_ref, r_ref, lw_ref, k_ref, v_ref, kkr_ref, a_ref, wkv_ref, u_ref, inv_ref, s0_ref,
             rk_ref, lnw_ref, lnb_ref, *rest):
        if hosting:
            piece_refs, rest = rest[:n_late], rest[n_late:]
            (dr_ref, dlw_ref, dk_ref, dv_ref, dkkr_ref, da_ref, drk_ref, dlnw_ref, dlnb_ref, recv_ref,
             hst, qst, *sems) = rest
            start, finish = _direct_phases(piece_refs, [p.shape[1] for p in late_pieces], recv_ref, *sems)
            pl.when(pl.program_id(0) == 0)(start)
        else:
            dr_ref, dlw_ref, dk_ref, dv_ref, dkkr_ref, da_ref, drk_ref, dlnw_ref, dlnb_ref, hst, qst = rest

        @pl.when(pl.program_id(0) == 0)
        def _():
            hst[...] = jnp.zeros(hst.shape, f32)
            qst[...] = jnp.zeros(qst.shape, f32)
            drk_ref[...] = jnp.zeros(drk_ref.shape, f32)
            dlnw_ref[...] = jnp.zeros(dlnw_ref.shape, f32)
            dlnb_ref[...] = jnp.zeros(dlnb_ref.shape, f32)

        dyg = dy_ref[...] * g_ref[...]
        rk = rk_ref[...]
        Hst, Qst = hst[...], qst[...]
        for c in reversed(range(WKV_SUB)):
            rows = slice(c * T, (c + 1) * T)
            dya = _heads(dyg, rows)
            rr, kk2, vv, Y = _heads(r_ref, rows), _heads(k_ref, rows), _heads(v_ref, rows), wkv_ref[:, rows, :]
            s = jnp.sum(rr * kk2 * rk, axis=-1, keepdims=True)
            ds = jnp.sum(dya * vv, axis=-1, keepdims=True)
            mean = jnp.mean(Y, axis=-1, keepdims=True)
            var = jnp.mean(jnp.square(Y - mean), axis=-1, keepdims=True)
            rstd = lax.rsqrt(var + GN_EPS)
            yn = (Y - mean) * rstd
            dyn = dya * lnw_ref[...]
            dY = rstd * (dyn - jnp.mean(dyn, axis=-1, keepdims=True) - yn * jnp.mean(dyn * yn, axis=-1, keepdims=True))
            drk_ref[...] += jnp.sum(ds * rr * kk2, axis=1, keepdims=True)
            dlnw_ref[...] += jnp.sum(dya * yn, axis=1, keepdims=True)
            dlnb_ref[...] += jnp.sum(dya, axis=1, keepdims=True)
            dr, dlw, dk, dv, dkkr, dasig, Hst, Qst = _wkv_chunk_bwd(
                s0_ref[c], Hst, Qst, rr, _heads(lw_ref, rows), kk2, vv, _heads(kkr_ref, rows), _heads(a_ref, rows),
                u_ref[:, rows, :], inv_ref[:, rows, :], dY)
            _put_heads(dr_ref, dr + ds * kk2 * rk, rows)
            _put_heads(dlw_ref, dlw, rows)
            _put_heads(dk_ref, dk + ds * rr * rk, rows)
            _put_heads(dv_ref, dv + dya * s, rows)
            _put_heads(dkkr_ref, dkkr, rows)
            _put_heads(da_ref, dasig, rows)
        hst[...] = Hst
        qst[...] = Qst
        if hosting:
            pl.when(pl.program_id(0) == nc - 1)(finish)

    tok = pl.BlockSpec((TS, H * N), lambda i: (nc - 1 - i, 0))
    blk = pl.BlockSpec((H, TS, N), lambda i: (0, nc - 1 - i, 0))
    par = pl.BlockSpec((H, 1, N), lambda i: (0, 0, 0))
    hbm = pl.BlockSpec(memory_space=pl.ANY)
    seq = jax.ShapeDtypeStruct((S, H * N), f32)
    pout = jax.ShapeDtypeStruct((H, 1, N), f32)
    out_specs, out_shape = [tok] * 6 + [par] * 3, [seq] * 6 + [pout] * 3
    sems = []
    if hosting:
        rows_late = sum(p.shape[1] for p in late_pieces)
        out_specs.append(hbm)
        out_shape.append(jax.ShapeDtypeStruct((N_DEV - 1, rows_late, late_pieces[0].shape[2]), late_pieces[0].dtype))
        sems = [pltpu.SemaphoreType.DMA((n_late * (N_DEV - 1),))] * 2
    return pl.pallas_call(
        body, name="wkv_bwd", grid=(nc,),
        in_specs=([tok] * 8 + [blk] * 2 + [pl.BlockSpec((H, TS, T), lambda i: (0, nc - 1 - i, 0))]
                  + [pl.BlockSpec((WKV_SUB, H, N, N), lambda i: (nc - 1 - i, 0, 0, 0))]
                  + [par] * 3 + [hbm] * n_late),
        out_specs=out_specs, out_shape=out_shape,
        scratch_shapes=[pltpu.VMEM((H, N, N), f32), pltpu.VMEM((H, 1, N), f32)] + sems,
        compiler_params=_cparams(("arbitrary",)),
    )(dy, g, r, lw, k, v, kkr, a, wkv, U, inv, S0s, r_k, ln_w, ln_b, *(late_pieces if hosting else []))


ATTN_TT = 2048


def _attn_rows(d, i, j):
    return pl.ds(ATTN_BLK * d * i + j, ATTN_BLK, stride=d) if d > 1 else pl.ds(ATTN_BLK * i, ATTN_BLK)


def _attn_take(ref, d, nsub):
    return jnp.stack([ref[_attn_rows(d, i, j), :] for i in range(nsub) for j in range(d)], axis=0)


def _attn_put(ref, val, d):
    for i in range(val.shape[0] // d):
        for j in range(d):
            ref[_attn_rows(d, i, j), :] = val[i * d + j]


def _attn_prev(cur, before, d):
    return before if cur.shape[0] == d else jnp.concatenate([before, cur[:cur.shape[0] - d]], axis=0)


def _attn_specs(gi, d, nt, reverse):
    per_tile = ATTN_TT // (ATTN_BLK * d)

    def tile(n):
        return nt - 1 - n if reverse else n

    def col(kind):
        return lambda hp, n: (tile(n), kind * (ATTN_W // 128) + 2 * gi + hp)

    def col_before(kind):
        return lambda hp, n: (jnp.maximum(tile(n) * per_tile - 1, 0), kind * (ATTN_W // 128) + 2 * gi + hp)

    cur = [pl.BlockSpec((ATTN_TT, 128), col(kind)) for kind in range(3)]
    before = [pl.BlockSpec((ATTN_BLK * d, 128), col_before(kind)) for kind in (1, 2)]
    own = pl.BlockSpec((ATTN_TT, 128), lambda hp, n: (tile(n), hp))
    return cur, before, own, tile


def _attn_norm(x, gain, scale):
    rs = lax.rsqrt(jnp.mean(x * x, axis=-1, keepdims=True) + RMS_EPS)
    return x * rs * (gain * scale), rs


def _attn_scores(qn, kn_c, kn_p, first):
    s_c = jnp.einsum("gqe,gke->gqk", qn.astype(bf16), kn_c.astype(bf16), preferred_element_type=f32)
    s_p = jnp.einsum("gqe,gke->gqk", qn.astype(bf16), kn_p.astype(bf16), preferred_element_type=f32)
    qi = lax.broadcasted_iota(jnp.int32, (1, ATTN_BLK, ATTN_BLK), 1)
    ki = lax.broadcasted_iota(jnp.int32, (1, ATTN_BLK, ATTN_BLK), 2)
    s_c = jnp.where(qi >= ki, s_c, NEG_INF)
    s_p = jnp.where(jnp.logical_and(ki >= qi, jnp.logical_not(first)), s_p, NEG_INF)
    return s_c, s_p


def _attn_fwd(pqkv, qg, kg, gi, S):
    d = ATTN_PAIRS[gi][1]
    nt = S // ATTN_TT
    nsub = ATTN_TT // (ATTN_BLK * d)
    nd = nsub * d

    def body(q_ref, k_ref, v_ref, kb_ref, vb_ref, qg_ref, kg_ref, o_ref, lse_ref):
        Q, K, V = _attn_take(q_ref, d, nsub), _attn_take(k_ref, d, nsub), _attn_take(v_ref, d, nsub)
        KB, VB = _attn_take(kb_ref, d, 1), _attn_take(vb_ref, d, 1)
        first = jnp.logical_and(lax.broadcasted_iota(jnp.int32, (nd, 1, 1), 0) < d, pl.program_id(1) == 0)
        outs, lses = [], []
        for h in range(2):
            sl = slice(h * HEAD, (h + 1) * HEAD)
            kc, vc = K[:, :, sl], V[:, :, sl]
            kp, vp = _attn_prev(kc, KB[:, :, sl], d), _attn_prev(vc, VB[:, :, sl], d)
            qn, _ = _attn_norm(Q[:, :, sl], qg_ref[...], HEAD ** -0.5)
            kn_c, _ = _attn_norm(kc, kg_ref[...], 1.0)
            kn_p, _ = _attn_norm(kp, kg_ref[...], 1.0)
            s_c, s_p = _attn_scores(qn, kn_c, kn_p, first)
            m = jnp.maximum(jnp.max(s_c, axis=-1, keepdims=True), jnp.max(s_p, axis=-1, keepdims=True))
            p_c = jnp.exp(s_c - m)
            p_p = jnp.exp(s_p - m)
            den = jnp.sum(p_c, axis=-1, keepdims=True) + jnp.sum(p_p, axis=-1, keepdims=True)
            inv = 1.0 / den
            o = jnp.einsum("gqk,gke->gqe", (p_c * inv).astype(bf16), vc.astype(bf16), preferred_element_type=f32)
            o += jnp.einsum("gqk,gke->gqe", (p_p * inv).astype(bf16), vp.astype(bf16), preferred_element_type=f32)
            outs.append(o)
            lses.append(jnp.broadcast_to(m + jnp.log(den), o.shape))
        _attn_put(o_ref, jnp.concatenate(outs, axis=-1), d)
        _attn_put(lse_ref, jnp.concatenate(lses, axis=-1), d)

    cur, before, own, _ = _attn_specs(gi, d, nt, False)
    par = pl.BlockSpec((1, HEAD), lambda hp, n: (0, 0))
    shp = jax.ShapeDtypeStruct((S, 2 * 128), f32)
    return pl.pallas_call(
        body, name=f"attn_fwd{gi}", grid=(2, nt), in_specs=cur + before + [par] * 2, out_specs=[own, own],
        out_shape=[shp, shp], compiler_params=_cparams(("arbitrary", "arbitrary")),
    )(pqkv, pqkv, pqkv, pqkv, pqkv, qg, kg)


def _attn_bwd(pqkv, o, lse, do, dlse, qg, kg, gi, S):
    d = ATTN_PAIRS[gi][1]
    nt = S // ATTN_TT
    nsub = ATTN_TT // (ATTN_BLK * d)
    nd = nsub * d

    def norm_bwd(dxn, x, rs, gain, scale):
        xh = x * rs
        dxh = dxn * (gain * scale)
        dx = rs * (dxh - xh * jnp.mean(dxh * xh, axis=-1, keepdims=True))
        dgain = jnp.sum(jnp.sum(dxn * xh * scale, axis=1), axis=0, keepdims=True)
        return dx, dgain

    def to_before(part, carried):
        return carried if nsub == 1 else jnp.concatenate([part[d:], carried], axis=0)

    def body(q_ref, k_ref, v_ref, kb_ref, vb_ref, o_ref, lse_ref, do_ref, dlse_ref, qg_ref, kg_ref,
             dq_ref, dk_ref, dv_ref, dqg_ref, dkg_ref, carry_k, carry_v):
        step = pl.program_id(1)

        @pl.when(jnp.logical_and(pl.program_id(0) == 0, step == 0))
        def _():
            dqg_ref[...] = jnp.zeros(dqg_ref.shape, f32)
            dkg_ref[...] = jnp.zeros(dkg_ref.shape, f32)

        @pl.when(step == 0)
        def _():
            carry_k[...] = jnp.zeros(carry_k.shape, f32)
            carry_v[...] = jnp.zeros(carry_v.shape, f32)

        Q, K, V = _attn_take(q_ref, d, nsub), _attn_take(k_ref, d, nsub), _attn_take(v_ref, d, nsub)
        KB, VB = _attn_take(kb_ref, d, 1), _attn_take(vb_ref, d, 1)
        O, LSE = _attn_take(o_ref, d, nsub), _attn_take(lse_ref, d, nsub)
        DO, DLSE = _attn_take(do_ref, d, nsub), _attn_take(dlse_ref, d, nsub)
        first = jnp.logical_and(lax.broadcasted_iota(jnp.int32, (nd, 1, 1), 0) < d, step == nt - 1)
        qg, kg = qg_ref[...], kg_ref[...]
        dqs, dks, dvs = [], [], []
        for h in range(2):
            sl = slice(h * HEAD, (h + 1) * HEAD)
            qx, kx, vc = Q[:, :, sl], K[:, :, sl], V[:, :, sl]
            kpx, vp = _attn_prev(kx, KB[:, :, sl], d), _attn_prev(vc, VB[:, :, sl], d)
            qn, rq = _attn_norm(qx, qg, HEAD ** -0.5)
            kn_c, rk_c = _attn_norm(kx, kg, 1.0)
            kn_p, _ = _attn_norm(kpx, kg, 1.0)
            s_c, s_p = _attn_scores(qn, kn_c, kn_p, first)
            lse = LSE[:, :, h * HEAD:h * HEAD + 1]
            p_c = jnp.exp(s_c - lse)
            p_p = jnp.exp(s_p - lse)
            dO = DO[:, :, sl]
            dOb = dO.astype(bf16)
            dp_c = jnp.einsum("gqe,gke->gqk", dOb, vc.astype(bf16), preferred_element_type=f32)
            dp_p = jnp.einsum("gqe,gke->gqk", dOb, vp.astype(bf16), preferred_element_type=f32)
            corr = DLSE[:, :, h * HEAD:h * HEAD + 1] - jnp.sum(dO * O[:, :, sl], axis=-1, keepdims=True)
            ds_c = (p_c * (dp_c + corr)).astype(bf16)
            ds_p = (p_p * (dp_p + corr)).astype(bf16)
            qnb = qn.astype(bf16)
            dqn = (jnp.einsum("gqk,gke->gqe", ds_c, kn_c.astype(bf16), preferred_element_type=f32)
                   + jnp.einsum("gqk,gke->gqe", ds_p, kn_p.astype(bf16), preferred_element_type=f32))
            dkn_p = jnp.einsum("gqk,gqe->gke", ds_p, qnb, preferred_element_type=f32)
            dv_p = jnp.einsum("gqk,gqe->gke", p_p.astype(bf16), dOb, preferred_element_type=f32)
            dkn = jnp.einsum("gqk,gqe->gke", ds_c, qnb, preferred_element_type=f32) + to_before(dkn_p, carry_k[h])
            dv = (jnp.einsum("gqk,gqe->gke", p_c.astype(bf16), dOb, preferred_element_type=f32)
                  + to_before(dv_p, carry_v[h]))
            carry_k[h] = dkn_p[:d]
            carry_v[h] = dv_p[:d]
            dq, dqg = norm_bwd(dqn, qx, rq, qg, HEAD ** -0.5)
            dk, dkg = norm_bwd(dkn, kx, rk_c, kg, 1.0)
            dqg_ref[...] += dqg
            dkg_ref[...] += dkg
            dqs.append(dq)
            dks.append(dk)
            dvs.append(dv)
        _attn_put(dq_ref, jnp.concatenate(dqs, axis=-1), d)
        _attn_put(dk_ref, jnp.concatenate(dks, axis=-1), d)
        _attn_put(dv_ref, jnp.concatenate(dvs, axis=-1), d)

    cur, before, own, _ = _attn_specs(gi, d, nt, True)
    par = pl.BlockSpec((1, HEAD), lambda hp, n: (0, 0))
    shp = jax.ShapeDtypeStruct((S, 2 * 128), f32)
    pshp = jax.ShapeDtypeStruct((1, HEAD), f32)
    return pl.pallas_call(
        body, name=f"attn_bwd{gi}", grid=(2, nt), in_specs=cur + before + [own] * 4 + [par] * 2,
        out_specs=[own] * 3 + [par] * 2, out_shape=[shp] * 3 + [pshp] * 2,
        scratch_shapes=[pltpu.VMEM((2, d, ATTN_BLK, HEAD), f32)] * 2,
        compiler_params=_cparams(("arbitrary", "arbitrary")),
    )(pqkv, pqkv, pqkv, pqkv, pqkv, o, lse, do, dlse, qg, kg)


def _rms(x, g):
    rs = lax.rsqrt(jnp.mean(x * x, axis=-1, keepdims=True) + RMS_EPS)
    return x * rs * g


def _f_rms(x, g):
    return _rms(x, g)


def _f_resid_rms(coef, x, f, g):
    xn = x + coef * f
    return xn, _rms(xn, g)


def _f_rms_bwd(n_parts, *args):
    dns = args[:n_parts]
    x, dres, g = args[n_parts:]
    dn = dns[0]
    for t in dns[1:]:
        dn = dn + t
    rs = lax.rsqrt(jnp.mean(x * x, axis=-1, keepdims=True) + RMS_EPS)
    xh = x * rs
    dxh = dn * g
    dx = dres + rs * (dxh - xh * jnp.mean(dxh * xh, axis=-1, keepdims=True))
    return dx, dx, jnp.sum(dn * xh, axis=0, keepdims=True)


def _f_loss(x, f, tgt):
    y = x + 0.5 * f
    diff = y - tgt
    part = 0.5 * jnp.sum(jnp.mean(diff * diff, axis=-1, keepdims=True), axis=0, keepdims=True)
    dy = diff * (1.0 / D)
    return dy, dy, jnp.broadcast_to(part, (1, 128))


def _dotb(a, b, dims):
    return lax.dot_general(a.astype(bf16), b.astype(bf16), dims, preferred_element_type=f32)


_NN = (((1,), (0,)), ((), ()))
_NT = (((1,), (1,)), ((), ()))
_TN = (((0,), (0,)), ((), ()))


def _rwkv_pre_core(prkv, prkv_prev, plora, plora_prev, mu_rkv, mu_lora, w0, w2p, a0, a2p, g2p, k_k, k_a):
    xs = prkv + (prkv_prev - prkv) * mu_rkv
    xl = plora + (plora_prev - plora) * mu_lora
    r, k, v = xs[:, :D], xs[:, D:2 * D], xs[:, 2 * D:]
    wd, ad, gd = xl[:, :128], xl[:, 128:256], xl[:, 256:]
    tw = jnp.tanh(wd)
    zw = w0 + _dotb(tw, w2p, _NN)
    sp = jnp.maximum(-zw, 0.0) + jnp.log(1.0 + jnp.exp(-jnp.abs(zw)))
    lw = -jnp.exp(-sp - 0.5)
    a = jax.nn.sigmoid(a0 + _dotb(ad, a2p, _NN))
    sg = jax.nn.sigmoid(gd)
    return dict(r=r, k=k, v=v, tw=tw, zw=zw, lw=lw, a=a, sg=sg, ad=ad)


def _rows_down(x, halo, blk):
    before = jnp.where(blk > 0, halo[HALO - 1:HALO, :], 0.0)
    row = lax.broadcasted_iota(jnp.int32, (x.shape[0], 1), 0)
    return jnp.where(row == 0, before, pltpu.roll(x, 1, 0))


def _rows_up(x, after):
    n = x.shape[0]
    row = lax.broadcasted_iota(jnp.int32, (n, 1), 0)
    return jnp.where(row == n - 1, after, pltpu.roll(x, n - 1, 0))


def _f_rwkv_pre(prkv, plora, mu_rkv, mu_lora, w0, w2p, a0, a2p, g2p, k_k, k_a, halo_rkv, halo_lora, blk):
    c = _rwkv_pre_core(prkv, _rows_down(prkv, halo_rkv, blk), plora, _rows_down(plora, halo_lora, blk),
                       mu_rkv, mu_lora, w0, w2p, a0, a2p, g2p, k_k, k_a)
    g = _dotb(c["sg"], g2p, _NN)
    k, a = c["k"], c["a"]
    return c["r"], c["lw"], k * (1.0 + (a - 1.0) * k_a), c["v"], k * k_k, a, g


def _f_rwkv_pre_bwd(prkv, plora, dr, dlw, dk2, dv, dkkr, da, dya, yap,
                    mu_rkv, mu_lora, w0, w2p, a0, a2p, g2p, k_k, k_a, halo_rkv, halo_lora, next_rkv, next_lora, blk):
    prkv_prev, plora_prev = _rows_down(prkv, halo_rkv, blk), _rows_down(plora, halo_lora, blk)
    c = _rwkv_pre_core(prkv, prkv_prev, plora, plora_prev, mu_rkv, mu_lora, w0, w2p, a0, a2p, g2p, k_k, k_a)
    k, a, sg, tw, zw, lw = c["k"], c["a"], c["sg"], c["tw"], c["zw"], c["lw"]
    dg = dya * yap
    dsg = _dotb(dg, g2p, _NT)
    dgd = dsg * sg * (1.0 - sg)
    dg2p = _dotb(sg, dg, _TN)
    dk = dk2 * (1.0 + (a - 1.0) * k_a) + dkkr * k_k
    da_t = da + dk2 * k * k_a
    dk_a = jnp.sum(dk2 * k * (a - 1.0), axis=0, keepdims=True)
    dk_k = jnp.sum(dkkr * k, axis=0, keepdims=True)
    dza = da_t * a * (1.0 - a)
    da0 = jnp.sum(dza, axis=0, keepdims=True)
    dad = _dotb(dza, a2p, _NT)
    da2p = _dotb(c["ad"], dza, _TN)
    dzw = dlw * lw * jax.nn.sigmoid(-zw)
    dw0 = jnp.sum(dzw, axis=0, keepdims=True)
    dtw = _dotb(dzw, w2p, _NT)
    dw2p = _dotb(tw, dzw, _TN)
    dwd = dtw * (1.0 - tw * tw)
    dxs = jnp.concatenate([dr, dk, dv], axis=1)
    dxl = jnp.concatenate([dwd, dad, dgd], axis=1)
    dmu_rkv = jnp.sum(dxs * (prkv_prev - prkv), axis=0, keepdims=True)
    dmu_lora = jnp.sum(dxl * (plora_prev - plora), axis=0, keepdims=True)
    to_next_rkv, to_next_lora = dxs * mu_rkv, dxl * mu_lora
    return (dxs * (1.0 - mu_rkv) + _rows_up(to_next_rkv, next_rkv), dxl * (1.0 - mu_lora) + _rows_up(to_next_lora, next_lora),
            dmu_rkv, dmu_lora, dw0, da0, dk_k, dk_a, dw2p, da2p, dg2p, to_next_rkv[0:1], to_next_lora[0:1])


def _group_alpha(l0, l1, l2):
    m = jnp.maximum(jnp.maximum(l0, l1), l2)
    e0, e1, e2 = jnp.exp(l0 - m), jnp.exp(l1 - m), jnp.exp(l2 - m)
    inv = 1.0 / (e0 + e1 + e2)
    return jnp.concatenate([e0 * inv, e1 * inv, e2 * inv], axis=1)


def _f_combine(o0, o1, o2, l0, l1, l2):
    return jnp.concatenate([o0, o1, o2], axis=1) * _group_alpha(l0, l1, l2)


def _f_combine_bwd(dyb, o0, o1, o2, l0, l1, l2, bd):
    alpha = _group_alpha(l0, l1, l2)
    hi, lo = _sp(dyb * jnp.concatenate([o0, o1, o2], axis=1))
    ones = bd.astype(bf16)
    e = jnp.dot(hi, ones, preferred_element_type=f32) + jnp.dot(lo, ones, preferred_element_type=f32)
    ae = alpha * e
    tot = ae[:, :256] + ae[:, 256:512] + ae[:, 512:]
    do = dyb * alpha
    dl = ae - alpha * jnp.concatenate([tot, tot, tot], axis=1)
    return do[:, :256], do[:, 256:512], do[:, 512:], dl[:, :256], dl[:, 256:512], dl[:, 512:]


def _f_merge(pgate, ta, tb, b_gate):
    gate = jax.nn.sigmoid(pgate + b_gate)
    return gate[:, :D] * ta + gate[:, D:] * tb


def _f_merge_bwd(dm, pgate, ta, tb, b_gate):
    gate = jax.nn.sigmoid(pgate + b_gate)
    ga, gb = gate[:, :D], gate[:, D:]
    dpg = jnp.concatenate([dm * ta * ga * (1.0 - ga), dm * tb * gb * (1.0 - gb)], axis=1)
    return dm * ga, dm * gb, dpg, jnp.sum(dpg, axis=0, keepdims=True)


def _f_adamw(w, g, m, v):
    m2 = ADAM_B1 * m + (1.0 - ADAM_B1) * g
    v2 = ADAM_B2 * v + (1.0 - ADAM_B2) * jnp.square(g)
    m_hat = m2 / (1.0 - ADAM_B1 ** ADAM_STEP)
    v_hat = v2 / (1.0 - ADAM_B2 ** ADAM_STEP)
    delta = -ADAM_LR * (m_hat / (jnp.sqrt(v_hat) + ADAM_EPS) + ADAM_WD * w)
    return delta, m2, v2


def _ffn_bwd(tag, dxo, dxo_b, x_in, n, gate, up, act, g, WiT, Wo, cross=None):
    du = _ffn_dact(f"{tag}_dact", dxo_b, Wo, gate, up)
    dWo = _mm(f"{tag}_dwo", act, dxo_b, "tn", out_dtype=GRAD_WIRE, scale=0.5)
    drms = (functools.partial(_f_rms_bwd, 1), [x_in, dxo], [g], [(D, f32), (D, bf16)], [(1, D)])
    dx, dx_b, dg, *recv = _mm(f"{tag}_dn", du, WiT, "nn", cross=cross, epilogue=drms)
    dWiT = _mm(f"{tag}_dwi", du, n, "tn", out_dtype=GRAD_WIRE)
    return dx, dx_b, dg, dWiT, dWo, (recv[0] if recv else None)


def _local_step(x0, tgt, W, P, hooks=None):
    S = x0.shape[0]
    (n1,) = _rowwise("f1_rms", _f_rms, [x0], [P["ffn1_norm"]], [(D, bf16)])
    hooks = hooks or {}
    if "gather_mid" in hooks:
        pack, weights = hooks["gather_mid"]
        gate1, up1, act1, gathered = _ffn_up("f1_up", n1, W["f1_iT"], gather=pack)
        W = {**W, **weights(gathered)}
    else:
        gate1, up1, act1 = _ffn_up("f1_up", n1, W["f1_iT"])
    mix_rms = (lambda f, x, g: _f_resid_rms(0.5, x, f, g), [x0], [P["mix_norm"]], [(D, f32), (D, bf16)], [])
    if "gather_in" in hooks:
        pack, weights = hooks["gather_in"]
        x1, h, gathered = _mm("f1_down", act1, W["f1_o"], "nn", gather=pack, epilogue=mix_rms)
        W = {**W, **weights(gathered)}
    else:
        x1, h = _mm("f1_down", act1, W["f1_o"], "nn", epilogue=mix_rms)
    prkv = _mm("p_rkv", h, W["in_rkvT"], "nt")
    plora = _mm("p_lora", h, W["in_loraT"], "nt")
    pqkv = _mm("p_qkv", h, W["in_qkvT"], "nt")
    pgate = _mm("p_gate", h, W["in_gateT"], "nt")
    pre_params = [P["mu_rkv"], P["mu_lora"], P["w0"], W["w2p"], P["a0"], W["a2p"], W["g2p"], P["k_k"], P["k_a"]]
    r, lw, k2, v, kkr, a, g = _rowwise("rwkv_pre", _f_rwkv_pre, [prkv, plora], pre_params, [(D, f32)] * 7, tm=128,
                                       halos=(0, 1))
    hm = [r, lw, k2, v, kkr, a]
    hp = [P["r_k"].reshape(RW_HEADS, 1, HEAD), P["ln_w"].reshape(RW_HEADS, 1, HEAD), P["ln_b"].reshape(RW_HEADS, 1, HEAD)]
    if "gather_late" in hooks:
        pack, weights = hooks["gather_late"]
        yap, ya, wkv_h, U_h, inv_h, S0s, gathered = _wkv_fwd(*hm, g, *hp, late_pack=pack)
        W = {**W, **weights(gathered)}
    else:
        yap, ya, wkv_h, U_h, inv_h, S0s = _wkv_fwd(*hm, g, *hp)
    ta = _mm("proj_a", ya, W["pr"], "nn")
    n_grp = len(ATTN_PAIRS)
    attn = [_attn_fwd(pqkv, P["q_norm"], P["k_norm"], gi, S) for gi in range(n_grp)]
    o_g, lse_g = [t[0] for t in attn], [t[1] for t in attn]
    (yb,) = _rowwise("attn_combine", _f_combine, [*o_g, *lse_g], [], [(ATTN_W, bf16)])
    merge = (lambda tb, pg, ta, bg: (tb, _f_merge(pg, ta, tb, bg)), [pgate, ta], [P["b_gate"]], [(D, f32), (D, bf16)], [])
    tb, merged = _mm("proj_b", yb, W["paT"], "nt", epilogue=merge)
    f2_rms = (lambda f, x, g: _f_resid_rms(1.0, x, f, g), [x1], [P["ffn2_norm"]], [(D, f32), (D, bf16)], [])
    x2, n2 = _mm("mix_out", merged, W["out"], "nn", epilogue=f2_rms)
    gate2, up2, act2 = _ffn_up("f2_up", n2, W["f2_iT"])
    loss_head = (lambda f, x, t: _f_loss(x, f, t), [x2, tgt], [], [(D, f32), (D, bf16)], [(1, 128)])
    dx3, dx3_b, loss = _mm("f2_down", act2, W["f2_o"], "nn", epilogue=loss_head)
    G, Gs = {}, {}
    dx2, dx2_b, Gs["ffn2_norm"], G["f2_iT"], G["f2_o"], _ = _ffn_bwd("f2", dx3, dx3_b, x2, n2, gate2, up2, act2,
                                                                    P["ffn2_norm"], W["f2_iT"], W["f2_o"])
    merge_bwd = (_f_merge_bwd, [pgate, ta, tb], [P["b_gate"]], [(D, bf16), (D, bf16), (2 * D, bf16)], [(1, 2 * D)])
    dta, dtb, dpgate, Gs["b_gate"] = _mm("d_merged", dx2_b, W["out"], "nt", epilogue=merge_bwd)
    G["out"] = _mm("dw_out", merged, dx2_b, "tn", out_dtype=GRAD_WIRE)
    dya = _mm("d_ya", dta, W["pr"], "nt")
    G["pr"] = _mm("dw_pr", ya, dta, "tn", out_dtype=GRAD_WIRE)
    bd = (jnp.arange(ATTN_W)[:, None] // HEAD == jnp.arange(ATTN_W)[None, :] // HEAD).astype(f32)
    combine_bwd = (_f_combine_bwd, [*o_g, *lse_g], [bd], [(ATTN_W // n_grp, f32)] * (2 * n_grp), [])
    dol = _mm("d_yb", dtb, W["paT"], "nn", epilogue=combine_bwd)
    G["paT"] = _mm("dw_pa", dtb, yb, "tn", out_dtype=GRAD_WIRE)
    if "reduce_late" in hooks:
        pieces_late = hooks["reduce_late"](G)
        hg = _wkv_bwd(dya, g, *hm, wkv_h, U_h, inv_h, S0s, *hp, late_pieces=pieces_late)
        G["late"] = (pieces_late, hg[9])
    else:
        hg = _wkv_bwd(dya, g, *hm, wkv_h, U_h, inv_h, S0s, *hp)
    dr, dlw, dk2, dv, dkkr, da = hg[:6]
    Gs["r_k"], Gs["ln_w"], Gs["ln_b"] = (t.reshape(1, D) for t in hg[6:9])
    lp = sum(LORA_PAD)
    (dprkv, dplora, Gs["mu_rkv"], Gs["mu_lora"], Gs["w0"], Gs["a0"], Gs["k_k"], Gs["k_a"],
     dw2p, da2p, dg2p) = _rowwise(
        "rwkv_pre_bwd", _f_rwkv_pre_bwd,
        [prkv, plora, dr, dlw, dk2, dv, dkkr, da, dya, yap], pre_params,
        [(3 * D, bf16), (lp, bf16)],
        [(1, 3 * D), (1, lp), (1, D), (1, D), (1, D), (1, D), (LORA_PAD[0], D), (LORA_PAD[1], D), (LORA_PAD[2], D)],
        tm=128, halos=(0, 1), carries=((1, 3 * D), (1, lp)), reverse=True)
    G["w2T"], G["a2T"], G["g2T"] = dw2p[:LORA_W[0]].T, da2p[:LORA_W[1]].T, dg2p[:LORA_W[2]].T
    dattn = [_attn_bwd(pqkv, o_g[gi], lse_g[gi], dol[gi], dol[n_grp + gi], P["q_norm"], P["k_norm"], gi, S)
             for gi in range(n_grp)]
    Gs["q_norm"] = dattn[0][3] + dattn[1][3] + dattn[2][3]
    Gs["k_norm"] = dattn[0][4] + dattn[1][4] + dattn[2][4]
    dpqkv = jnp.concatenate([dattn[gi][kind] for kind in range(3) for gi in range(n_grp)], axis=1).astype(bf16)
    dh = [_mm("dh_rkv", dprkv, W["in_rkvT"], "nn"), _mm("dh_lora", dplora, W["in_loraT"], "nn"),
          _mm("dh_gate", dpgate, W["in_gateT"], "nn")]
    mix_drms = (functools.partial(_f_rms_bwd, 4), [*dh, x1, dx2], [P["mix_norm"]], [(D, f32), (D, bf16)], [(1, D)])
    dx1, dx1_b, Gs["mix_norm"] = _mm("dh_qkv", dpqkv, W["in_qkvT"], "nn", epilogue=mix_drms)
    dW_rkv = _mm("dw_rkv", dprkv, h, "tn", out_dtype=GRAD_WIRE)
    dW_lora = _mm("dw_lora", dplora, h, "tn", out_dtype=GRAD_WIRE)
    dW_qkv = _mm("dw_qkv", dpqkv, h, "tn", out_dtype=GRAD_WIRE)
    dW_gate = _mm("dw_gate", dpgate, h, "tn", out_dtype=GRAD_WIRE)
    o1, o2 = LORA_PAD[0], LORA_PAD[0] + LORA_PAD[1]
    G["inT"] = jnp.concatenate([dW_rkv, dW_lora[:LORA_W[0]], dW_lora[o1:o1 + LORA_W[1]], dW_lora[o2:o2 + LORA_W[2]],
                                dW_qkv, dW_gate], axis=0)
    part_mid = hooks["reduce_mid"](G) if "reduce_mid" in hooks else None
    dx0, _, Gs["ffn1_norm"], G["f1_iT"], G["f1_o"], recv_mid = _ffn_bwd(
        "f1", dx1, dx1_b, x0, n1, gate1, up1, act1, P["ffn1_norm"], W["f1_iT"], W["f1_o"], cross=part_mid)
    G["mid"] = (part_mid, recv_mid)
    return loss[0, 0], dx0, G, Gs


def _peer(k):
    x, y, c = lax.axis_index("x"), lax.axis_index("y"), lax.axis_index("c")
    px = 1 - x if k & 4 else x
    py = 1 - y if k & 2 else y
    pc = 1 - c if k & 1 else c
    return (px, py, pc), 4 * px + 2 * py + pc


def _gather_phases(x_ref, out_ref, send_sems, recv_sems, local_sem):
    x, y, c = lax.axis_index("x"), lax.axis_index("y"), lax.axis_index("c")
    me, sibling = (x, y, c), (x, y, 1 - c)
    chips = [(1 - x, y), (x, 1 - y), (1 - x, 1 - y)]

    def slot(px, py, pc):
        return out_ref.at[4 * px + 2 * py + pc]

    def copy(k, block, to, src=None):
        return pltpu.make_async_remote_copy(
            src_ref=slot(*block) if src is None else src, dst_ref=slot(*block), send_sem=send_sems.at[k],
            recv_sem=recv_sems.at[k], device_id=to, device_id_type=MESH)

    def mine():
        return pltpu.make_async_copy(x_ref, slot(*me), local_sem)

    def first():
        return [copy(0, me, sibling, src=x_ref)] + [copy(1 + j, me, (*chip, c), src=x_ref) for j, chip in enumerate(chips)]

    def passed():
        return [copy(4 + j, (*chip, c), sibling) for j, chip in enumerate(chips)]

    def start():
        mine().start()
        for cp in first():
            cp.start()

    def forward():
        for j, (chip, cp) in enumerate(zip(chips, passed())):
            copy(1 + j, (*chip, c), me).wait_recv()
            cp.start()

    def finish():
        copy(0, sibling, me).wait_recv()
        for j, chip in enumerate(chips):
            copy(4 + j, (*chip, 1 - c), me).wait_recv()
        for cp in first() + passed():
            cp.wait_send()
        mine().wait()

    return start, forward, finish


GATHER_SEMS = [pltpu.SemaphoreType.DMA((N_DEV - 1,)), pltpu.SemaphoreType.DMA((N_DEV - 1,)), pltpu.SemaphoreType.DMA(())]


def _all_gather(pack):
    R, C = pack.shape

    def body(x_ref, out_ref, send_sems, recv_sems, local_sem):
        for phase in _gather_phases(x_ref, out_ref, send_sems, recv_sems, local_sem):
            phase()

    return pl.pallas_call(
        body, name="weight_all_gather", out_shape=jax.ShapeDtypeStruct((N_DEV, R, C), pack.dtype),
        in_specs=[pl.BlockSpec(memory_space=pl.ANY)], out_specs=pl.BlockSpec(memory_space=pl.ANY),
        scratch_shapes=GATHER_SEMS,
    )(pack)


def _cross_phases(p_ref, out_ref, send_sems, recv_sems):
    x, y, c = lax.axis_index("x"), lax.axis_index("y"), lax.axis_index("c")

    def copies():
        out = []
        for j, (fx, fy) in enumerate([(1, 0), (0, 1), (1, 1)]):
            px = 1 - x if fx else x
            py = 1 - y if fy else y
            out.append(pltpu.make_async_remote_copy(src_ref=p_ref.at[2 * px + py], dst_ref=out_ref.at[j],
                                                    send_sem=send_sems.at[j], recv_sem=recv_sems.at[j],
                                                    device_id=(px, py, c), device_id_type=MESH))
        return out

    def start():
        for cp in copies():
            cp.start()

    def finish():
        for cp in copies():
            cp.wait()

    return start, finish


CROSS_SEMS = [pltpu.SemaphoreType.DMA((3,)), pltpu.SemaphoreType.DMA((3,))]


def _direct_phases(piece_refs, rows, out_ref, send_sems, recv_sems):
    offs = [sum(rows[:i]) for i in range(len(rows))]

    def copies():
        out = []
        for i, g_ref in enumerate(piece_refs):
            for k in range(1, N_DEV):
                dev, idx = _peer(k)
                out.append(pltpu.make_async_remote_copy(
                    src_ref=g_ref.at[idx], dst_ref=out_ref.at[k - 1, pl.ds(offs[i], rows[i])],
                    send_sem=send_sems.at[i * (N_DEV - 1) + k - 1], recv_sem=recv_sems.at[i * (N_DEV - 1) + k - 1],
                    device_id=dev, device_id_type=MESH))
        return out

    def start():
        for cp in copies():
            cp.start()

    def finish():
        for cp in copies():
            cp.wait()

    return start, finish


def _sum_direct(pieces, recv, me, tag):
    n = len(pieces)
    C = pieces[0].shape[2]
    rows_blk = functools.reduce(math.gcd, [p.shape[1] for p in pieces])
    nblk = [p.shape[1] // rows_blk for p in pieces]
    lo = [sum(nblk[:i]) for i in range(n)]
    R = sum(nblk) * rows_blk

    def body(me_ref, *refs):
        g_refs, r_ref, o_ref = refs[:n], refs[n], refs[n + 1]
        rb = pl.program_id(0)
        for i in range(n):
            @pl.when(jnp.logical_and(rb >= lo[i], rb < lo[i] + nblk[i]))
            def _(g_ref=g_refs[i]):
                acc = g_ref[...].astype(f32)
                for k in range(N_DEV - 1):
                    acc = acc + r_ref[k].astype(f32)
                o_ref[...] = acc

    def piece_spec(i):
        return pl.BlockSpec((None, rows_blk, C), lambda rb, me_ref: (me_ref[0], jnp.clip(rb - lo[i], 0, nblk[i] - 1), 0))

    return pl.pallas_call(
        body, name=f"grad_sum_{tag}",
        grid_spec=pltpu.PrefetchScalarGridSpec(
            num_scalar_prefetch=1, grid=(R // rows_blk,),
            in_specs=[piece_spec(i) for i in range(n)] + [pl.BlockSpec((N_DEV - 1, rows_blk, C), lambda rb, me_ref: (0, rb, 0))],
            out_specs=pl.BlockSpec((rows_blk, C), lambda rb, me_ref: (rb, 0))),
        out_shape=jax.ShapeDtypeStruct((R, C), f32),
        compiler_params=_cparams(("arbitrary",)),
    )(me, *pieces, recv)


N_CHIP = 4


def _grad_pair(pieces, tag):
    n = len(pieces)
    C = pieces[0].shape[2]
    rows = [p.shape[1] for p in pieces]
    offs = [sum(rows[:i]) for i in range(n)]
    R = sum(rows)

    def body(*refs):
        g_refs, (other_ref, send_sems, recv_sems) = refs[:n], refs[n:]
        x, y, c = lax.axis_index("x"), lax.axis_index("y"), lax.axis_index("c")
        copies = []
        for i, g_ref in enumerate(g_refs):
            for k in range(N_CHIP):
                cp = pltpu.make_async_remote_copy(
                    src_ref=g_ref.at[4 * (k // 2) + 2 * (k % 2) + 1 - c], dst_ref=other_ref.at[k, pl.ds(offs[i], rows[i])],
                    send_sem=send_sems.at[i * N_CHIP + k], recv_sem=recv_sems.at[i * N_CHIP + k],
                    device_id=(x, y, 1 - c), device_id_type=MESH)
                cp.start()
                copies.append(cp)
        for cp in copies:
            cp.wait()

    return pl.pallas_call(
        body, name=f"grad_pair_{tag}", out_shape=jax.ShapeDtypeStruct((N_CHIP, R, C), pieces[0].dtype),
        in_specs=[pl.BlockSpec(memory_space=pl.ANY)] * n, out_specs=pl.BlockSpec(memory_space=pl.ANY),
        scratch_shapes=[pltpu.SemaphoreType.DMA((n * N_CHIP,))] * 2,
    )(*pieces)


def _pair_add(pieces, other, c, tag):
    n = len(pieces)
    C = pieces[0].shape[2]
    rows_blk = functools.reduce(math.gcd, [p.shape[1] for p in pieces])
    nblk = [p.shape[1] // rows_blk for p in pieces]
    lo = [sum(nblk[:i]) for i in range(n)]
    R = sum(nblk) * rows_blk

    def body(c_ref, *refs):
        g_refs, o_ref, out_ref = refs[:n], refs[n], refs[n + 1]
        rb = pl.program_id(1)
        for i in range(n):
            @pl.when(jnp.logical_and(rb >= lo[i], rb < lo[i] + nblk[i]))
            def _(g_ref=g_refs[i]):
                out_ref[...] = (g_ref[...].astype(f32) + o_ref[...].astype(f32)).astype(out_ref.dtype)

    def piece_spec(i):
        return pl.BlockSpec((1, None, rows_blk, C),
                            lambda k, rb, c_ref: (k, c_ref[0], jnp.clip(rb - lo[i], 0, nblk[i] - 1), 0))

    blk = pl.BlockSpec((1, rows_blk, C), lambda k, rb, c_ref: (k, rb, 0))
    return pl.pallas_call(
        body, name=f"pair_add_{tag}",
        grid_spec=pltpu.PrefetchScalarGridSpec(
            num_scalar_prefetch=1, grid=(N_CHIP, R // rows_blk),
            in_specs=[piece_spec(i) for i in range(n)] + [blk], out_specs=blk),
        out_shape=jax.ShapeDtypeStruct((N_CHIP, R, C), other.dtype),
        compiler_params=_cparams(("arbitrary", "arbitrary")),
    )(c, *[p.reshape(N_CHIP, 2, p.shape[1], C) for p in pieces], other)


def _grad_cross(part):
    _, R, C = part.shape

    def body(p_ref, out_ref, send_sems, recv_sems):
        for phase in _cross_phases(p_ref, out_ref, send_sems, recv_sems):
            phase()

    return pl.pallas_call(
        body, name="grad_cross", out_shape=jax.ShapeDtypeStruct((3, R, C), part.dtype),
        in_specs=[pl.BlockSpec(memory_space=pl.ANY)], out_specs=pl.BlockSpec(memory_space=pl.ANY),
        scratch_shapes=CROSS_SEMS,
    )(part)


def _grad_sum(part, recv, my_chip, tag):
    _, R, C = part.shape
    tr = max(t for t in range(16, 513, 16) if R % t == 0)

    def body(chip_ref, p_ref, r_ref, o_ref):
        acc = p_ref[0].astype(f32)
        for j in range(3):
            acc = acc + r_ref[j].astype(f32)
        o_ref[...] = acc

    return pl.pallas_call(
        body, name=f"grad_sum_{tag}",
        grid_spec=pltpu.PrefetchScalarGridSpec(
            num_scalar_prefetch=1, grid=(R // tr,),
            in_specs=[pl.BlockSpec((1, tr, C), lambda i, chip_ref: (chip_ref[0], i, 0)),
                      pl.BlockSpec((3, tr, C), lambda i, chip_ref: (0, i, 0))],
            out_specs=pl.BlockSpec((tr, C), lambda i, chip_ref: (i, 0))),
        out_shape=jax.ShapeDtypeStruct((R, C), f32),
        compiler_params=_cparams(("arbitrary",)),
    )(my_chip, part, recv)


def _small_all_reduce(small):
    R, C = small.shape

    def body(x_ref, o_ref, buf, send_sems, recv_sems):
        _, me = _peer(0)
        buf[me] = x_ref[...]
        sends = []
        for k in range(1, N_DEV):
            dev, _ = _peer(k)
            cp = pltpu.make_async_remote_copy(src_ref=x_ref, dst_ref=buf.at[me], send_sem=send_sems.at[k - 1],
                                              recv_sem=recv_sems.at[k - 1], device_id=dev, device_id_type=MESH)
            cp.start()
            sends.append(cp)
        for k in range(1, N_DEV):
            dev, idx = _peer(k)
            pltpu.make_async_remote_copy(src_ref=x_ref, dst_ref=buf.at[idx], send_sem=send_sems.at[k - 1],
                                         recv_sem=recv_sems.at[k - 1], device_id=dev, device_id_type=MESH).wait_recv()
        for cp in sends:
            cp.wait_send()
        acc = buf[0]
        for i in range(1, N_DEV):
            acc = acc + buf[i]
        o_ref[...] = acc

    return pl.pallas_call(
        body, name="small_all_reduce", out_shape=jax.ShapeDtypeStruct((R, C), f32),
        in_specs=[pl.BlockSpec(memory_space=pltpu.VMEM)], out_specs=pl.BlockSpec(memory_space=pltpu.VMEM),
        scratch_shapes=[pltpu.VMEM((N_DEV, R, C), f32), pltpu.SemaphoreType.DMA((N_DEV - 1,)),
                        pltpu.SemaphoreType.DMA((N_DEV - 1,))],
    )(small)


_LORA = (("rwkv_w2", True), ("rwkv_a2", True), ("rwkv_g2", True))
_GROUPS_FIRST = ((("ffn1_w_in", True),),)
_GROUPS_MID = ((("ffn1_w_out", False),), _LORA)
_GROUPS_IN = ((("w_in", True),),)
_GROUPS_LATE = ((("w_proj_rwkv", False),), (("w_proj_attn", True),), (("w_out", False),),
                (("ffn2_w_in", True),), (("ffn2_w_out", False),))
_GRADS_MID = ((("w_in", True),), _LORA)
_GRADS_LAST = ((("ffn1_w_in", True),), (("ffn1_w_out", False),))
_BIG = tuple(item for group in _GROUPS_FIRST + _GROUPS_MID + _GROUPS_IN + _GROUPS_LATE for item in group)
_SMALL = ("ffn1_norm", "mix_norm", "b_gate", "rwkv_mu", "rwkv_w0", "rwkv_a0", "rwkv_k_k", "rwkv_k_a", "rwkv_r_k",
          "rwkv_ln_w", "rwkv_ln_b", "attn_q_norm", "attn_k_norm", "ffn2_norm")


def _pack_block(like, groups):
    rows = [sum(like[n].shape[0] * like[n].shape[1] // D for n, _ in group) for group in groups]
    g = functools.reduce(math.gcd, rows)
    return g if (g % 16 == 0 and g >= PACK_BLOCK) else PACK_BLOCK


def _pack_layout(like, groups):
    block = _pack_block(like, groups)
    items, spans, off = {}, [], 0
    for group in groups:
        start = off
        for name, _ in group:
            shp = like[name].shape
            n = shp[0] * shp[1] // D
            items[name] = (off, n)
            off += n
        off = -(-off // block) * block
        spans.append((start, off - start))
    return items, spans, off


def _pack_big(shards, groups):
    items, _, rows = _pack_layout(shards, groups)
    parts, at = [], 0
    for group in groups:
        for name, tr in group:
            off, n = items[name]
            t = shards[name]
            if off > at:
                parts.append(jnp.zeros((off - at, D), t.dtype))
            parts.append((t.T if tr else t).reshape(n, D))
            at = off + n
    if rows > at:
        parts.append(jnp.zeros((rows - at, D), parts[0].dtype))
    return jnp.concatenate(parts, axis=0)


def _unpack_big(pack, like, groups):
    items, _, _ = _pack_layout(like, groups)
    out = {}
    for group in groups:
        for name, tr in group:
            off, n = items[name]
            shp = like[name].shape
            t = pack[off:off + n]
            out[name] = t.reshape(shp[1], shp[0]).T if tr else t.reshape(shp)
    return out


def _unpack_gathered(gathered, like, groups):
    items, _, _ = _pack_layout(like, groups)
    full = {}
    for group in groups:
        for name, tr in group:
            shp = like[name].shape
            off, rows = items[name]
            r_loc, c_loc = (shp[1], shp[0]) if tr else shp
            full[name] = gathered[:, off:off + rows].reshape(N_DEV * r_loc, c_loc)
    return full


def _grad_pieces(g_full, like, groups):
    items, spans, _ = _pack_layout(like, groups)
    pieces = []
    for group, (_, rows_pad) in zip(groups, spans):
        parts = [g_full[n].astype(GRAD_WIRE).reshape(N_DEV, items[n][1], D) for n, _ in group]
        piece = parts[0] if len(parts) == 1 else jnp.concatenate(parts, axis=1)
        if rows_pad > piece.shape[1]:
            piece = jnp.pad(piece, ((0, 0), (0, rows_pad - piece.shape[1]), (0, 0)))
        pieces.append(piece)
    return pieces


def _small_rows(name, t):
    flat = t.reshape(-1)
    pad = (-flat.shape[0]) % D
    return jnp.pad(flat, (0, pad)).reshape(-1, D)


def _pack_small(vals, last_row=None):
    parts = [_small_rows(n, vals[n]) for n in _SMALL]
    used = sum(p.shape[0] for p in parts) + (last_row is not None)
    parts.append(jnp.zeros((SMALL_ROWS - used, D), f32))
    if last_row is not None:
        parts.append(last_row)
    return jnp.concatenate(parts, axis=0)


def _unpack_small(pack, like):
    out, off = {}, 0
    for n in _SMALL:
        size = like[n].size
        rows = -(-size // D)
        out[n] = pack[off:off + rows].reshape(-1)[:size].reshape(like[n].shape)
        off += rows
    return out


def _build_W_mid(full):
    dt = full["rwkv_w2"].dtype
    z64, z96 = jnp.zeros((64, D), dt), jnp.zeros((96, D), dt)
    return {
        "f1_o": full["ffn1_w_out"],
        "w2p": jnp.concatenate([full["rwkv_w2"].T, z64], axis=0),
        "a2p": jnp.concatenate([full["rwkv_a2"].T, z64], axis=0),
        "g2p": jnp.concatenate([full["rwkv_g2"].T, z96], axis=0),
    }


def _build_W_in(full):
    inT = full["w_in"]
    z64, z96 = jnp.zeros((64, D), inT.dtype), jnp.zeros((96, D), inT.dtype)
    return {
        "in_rkvT": inT[:3 * D],
        "in_loraT": jnp.concatenate([inT[3072:3136], z64, inT[3136:3200], z64, inT[3200:3360], z96], axis=0),
        "in_qkvT": inT[3360:3360 + 3 * ATTN_W], "in_gateT": inT[3360 + 3 * ATTN_W:],
    }


def _build_W_late(full):
    return {"pr": full["w_proj_rwkv"], "paT": full["w_proj_attn"], "out": full["w_out"],
            "f2_iT": full["ffn2_w_in"], "f2_o": full["ffn2_w_out"]}


def _build_W_first(full):
    return {"f1_iT": full["ffn1_w_in"]}


def _build_W(full):
    return {**_build_W_first(full), **_build_W_mid(full), **_build_W_in(full), **_build_W_late(full)}


_G_NAMES = {"ffn1_w_in": "f1_iT", "ffn1_w_out": "f1_o", "w_in": "inT", "rwkv_w2": "w2T", "rwkv_a2": "a2T",
            "rwkv_g2": "g2T", "w_proj_rwkv": "pr", "w_proj_attn": "paT", "w_out": "out", "ffn2_w_in": "f2_iT",
            "ffn2_w_out": "f2_o"}


def _named_grads(G, groups):
    return {n: G[_G_NAMES[n]] for group in groups for n, _ in group}


def _reduce_start(G, like, groups, my_c, tag):
    pieces = _grad_pieces(_named_grads(G, groups), like, groups)
    return _pair_add(pieces, _grad_pair(pieces, tag), my_c, tag)


def _build_P(Wl):
    mu = Wl["rwkv_mu"]
    z64f, z96f = jnp.zeros((1, 64), f32), jnp.zeros((1, 96), f32)
    return {
        "ffn1_norm": Wl["ffn1_norm"][None], "mix_norm": Wl["mix_norm"][None], "ffn2_norm": Wl["ffn2_norm"][None],
        "b_gate": Wl["b_gate"][None], "mu_rkv": mu[None, :3 * D],
        "mu_lora": jnp.concatenate([mu[None, 3072:3136], z64f, mu[None, 3136:3200], z64f, mu[None, 3200:3360], z96f], axis=1),
        "w0": Wl["rwkv_w0"][None], "a0": Wl["rwkv_a0"][None], "k_k": Wl["rwkv_k_k"][None], "k_a": Wl["rwkv_k_a"][None],
        "r_k": Wl["rwkv_r_k"].reshape(1, D), "ln_w": Wl["rwkv_ln_w"][None], "ln_b": Wl["rwkv_ln_b"][None],
        "q_norm": Wl["attn_q_norm"][None], "k_norm": Wl["attn_k_norm"][None],
    }


def kernel(x, ffn1_norm, ffn1_w_in, ffn1_w_out, mix_norm, w_in, b_gate, rwkv_mu, rwkv_w0, rwkv_w2, rwkv_a0, rwkv_a2, rwkv_g2, rwkv_k_k, rwkv_k_a, rwkv_r_k, rwkv_ln_w, rwkv_ln_b, attn_q_norm, attn_k_norm, w_proj_rwkv, w_proj_attn, w_out, ffn2_norm, ffn2_w_in, ffn2_w_out, loss_target, m_ffn1_norm, m_ffn1_w_in, m_ffn1_w_out, m_mix_norm, m_w_in, m_b_gate, m_rwkv_mu, m_rwkv_w0, m_rwkv_w2, m_rwkv_a0, m_rwkv_a2, m_rwkv_g2, m_rwkv_k_k, m_rwkv_k_a, m_rwkv_r_k, m_rwkv_ln_w, m_rwkv_ln_b, m_attn_q_norm, m_attn_k_norm, m_w_proj_rwkv, m_w_proj_attn, m_w_out, m_ffn2_norm, m_ffn2_w_in, m_ffn2_w_out, v_ffn1_norm, v_ffn1_w_in, v_ffn1_w_out, v_mix_norm, v_w_in, v_b_gate, v_rwkv_mu, v_rwkv_w0, v_rwkv_w2, v_rwkv_a0, v_rwkv_a2, v_rwkv_g2, v_rwkv_k_k, v_rwkv_k_a, v_rwkv_r_k, v_rwkv_ln_w, v_rwkv_ln_b, v_attn_q_norm, v_attn_k_norm, v_w_proj_rwkv, v_w_proj_attn, v_w_out, v_ffn2_norm, v_ffn2_w_in, v_ffn2_w_out):
    names = ("ffn1_norm", "ffn1_w_in", "ffn1_w_out", "mix_norm", "w_in", "b_gate", "rwkv_mu", "rwkv_w0", "rwkv_w2",
             "rwkv_a0", "rwkv_a2", "rwkv_g2", "rwkv_k_k", "rwkv_k_a", "rwkv_r_k", "rwkv_ln_w", "rwkv_ln_b",
             "attn_q_norm", "attn_k_norm", "w_proj_rwkv", "w_proj_attn", "w_out", "ffn2_norm", "ffn2_w_in", "ffn2_w_out")
    w_all = (ffn1_norm, ffn1_w_in, ffn1_w_out, mix_norm, w_in, b_gate, rwkv_mu, rwkv_w0, rwkv_w2, rwkv_a0, rwkv_a2,
             rwkv_g2, rwkv_k_k, rwkv_k_a, rwkv_r_k, rwkv_ln_w, rwkv_ln_b, attn_q_norm, attn_k_norm, w_proj_rwkv,
             w_proj_attn, w_out, ffn2_norm, ffn2_w_in, ffn2_w_out)
    m_all = (m_ffn1_norm, m_ffn1_w_in, m_ffn1_w_out, m_mix_norm, m_w_in, m_b_gate, m_rwkv_mu, m_rwkv_w0, m_rwkv_w2,
             m_rwkv_a0, m_rwkv_a2, m_rwkv_g2, m_rwkv_k_k, m_rwkv_k_a, m_rwkv_r_k, m_rwkv_ln_w, m_rwkv_ln_b,
             m_attn_q_norm, m_attn_k_norm, m_w_proj_rwkv, m_w_proj_attn, m_w_out, m_ffn2_norm, m_ffn2_w_in, m_ffn2_w_out)
    v_all = (v_ffn1_norm, v_ffn1_w_in, v_ffn1_w_out, v_mix_norm, v_w_in, v_b_gate, v_rwkv_mu, v_rwkv_w0, v_rwkv_w2,
             v_rwkv_a0, v_rwkv_a2, v_rwkv_g2, v_rwkv_k_k, v_rwkv_k_a, v_rwkv_r_k, v_rwkv_ln_w, v_rwkv_ln_b,
             v_attn_q_norm, v_attn_k_norm, v_w_proj_rwkv, v_w_proj_attn, v_w_out, v_ffn2_norm, v_ffn2_w_in, v_ffn2_w_out)
    Wl = {n: t[0] for n, t in zip(names, w_all)}
    Ml = {n: t[0] for n, t in zip(names, m_all)}
    Vl = {n: t[0] for n, t in zip(names, v_all)}
    big = [n for n, _ in _BIG]

    my_c = lax.axis_index("c").astype(jnp.int32).reshape(1)
    my_chip = (2 * lax.axis_index("x") + lax.axis_index("y")).astype(jnp.int32).reshape(1)

    def pack(groups):
        return _pack_big(Wl, groups).astype(bf16)

    gathered = _all_gather(pack(_GROUPS_FIRST))
    W, P = _build_W_first(_unpack_gathered(gathered, Wl, _GROUPS_FIRST)), _build_P(Wl)
    hooks = {"gather_mid": (pack(_GROUPS_MID), lambda g: _build_W_mid(_unpack_gathered(g, Wl, _GROUPS_MID))),
             "gather_in": (pack(_GROUPS_IN), lambda g: _build_W_in(_unpack_gathered(g, Wl, _GROUPS_IN))),
             "gather_late": (pack(_GROUPS_LATE), lambda g: _build_W_late(_unpack_gathered(g, Wl, _GROUPS_LATE))),
             "reduce_mid": lambda G: _reduce_start(G, Wl, _GRADS_MID, my_c, "mid"),
             "reduce_late": lambda G: _grad_pieces(_named_grads(G, _GROUPS_LATE), Wl, _GROUPS_LATE)}

    loss_local, dx0, G, Gs = _local_step(x[0], loss_target[0], W, P, hooks)

    part_last = _reduce_start(G, Wl, _GRADS_LAST, my_c, "last")
    g_big = _unpack_big(_grad_sum(part_last, _grad_cross(part_last), my_chip, "last"), Wl, _GRADS_LAST)
    g_big.update(_unpack_big(_grad_sum(*G["mid"], my_chip, "mid"), Wl, _GRADS_MID))
    me = (4 * lax.axis_index("x") + 2 * lax.axis_index("y") + lax.axis_index("c")).astype(jnp.int32).reshape(1)
    g_big.update(_unpack_big(_sum_direct(*G["late"], me, "late"), Wl, _GROUPS_LATE))

    mu_g = Gs["mu_rkv"], Gs["mu_lora"]
    o1, o2 = LORA_PAD[0], LORA_PAD[0] + LORA_PAD[1]
    g_small_local = {
        "ffn1_norm": Gs["ffn1_norm"], "mix_norm": Gs["mix_norm"], "b_gate": Gs["b_gate"],
        "rwkv_mu": jnp.concatenate([mu_g[0], mu_g[1][:, :64], mu_g[1][:, o1:o1 + 64], mu_g[1][:, o2:o2 + 160]], axis=1),
        "rwkv_w0": Gs["w0"], "rwkv_a0": Gs["a0"], "rwkv_k_k": Gs["k_k"], "rwkv_k_a": Gs["k_a"], "rwkv_r_k": Gs["r_k"],
        "rwkv_ln_w": Gs["ln_w"], "rwkv_ln_b": Gs["ln_b"], "attn_q_norm": Gs["q_norm"], "attn_k_norm": Gs["k_norm"],
        "ffn2_norm": Gs["ffn2_norm"]}
    gs_pack = _small_all_reduce(_pack_small(g_small_local, last_row=jnp.full((1, D), loss_local, f32)))

    out_g, out_d, out_m, out_v = dict(g_big), {}, {}, {}
    for n in big:
        cols = Wl[n].shape[1]
        out_d[n], out_m[n], out_v[n] = _rowwise(f"adamw_{n}", _f_adamw, [Wl[n], g_big[n], Ml[n], Vl[n]], [],
                                                 [(cols, f32)] * 3)
    ds_pack, ms_pack, vs_pack = _rowwise(
        "adamw_small", _f_adamw, [_pack_small(Wl), gs_pack, _pack_small(Ml), _pack_small(Vl)], [], [(D, f32)] * 3)
    for out, pack in ((out_g, gs_pack), (out_d, ds_pack), (out_m, ms_pack), (out_v, vs_pack)):
        out.update(_unpack_small(pack, Wl))

    loss = gs_pack[SMALL_ROWS - 1, 0]
    return (loss, dx0[None], *[out_g[n][None] for n in names], *[out_d[n][None] for n in names],
            *[out_m[n][None] for n in names], *[out_v[n][None] for n in names])
```
